```python
import math
import jax, jax.numpy as jnp
from jax import lax
import numpy as np

D_MODEL = 1024
BATCH = 8
SEQ = 2048
DEPTH = 4

MEM_LEN = 256
RNN_WIDTH = D_MODEL
RNN_BLOCKS = 4
RNN_BLOCK = RNN_WIDTH // RNN_BLOCKS
CONV_WIDTH = 4
LRU_C = 8.0
HEAD_DIM = 64
N_Q_HEADS = D_MODEL // HEAD_DIM
N_KV_HEADS = 2
GROUP = N_Q_HEADS // N_KV_HEADS
ATTN_WIDTH = N_Q_HEADS * HEAD_DIM
KV_WIDTH = N_KV_HEADS * HEAD_DIM
WINDOW = 128
BLOCK = 128
ROPE_THETA = 500000.0
ROT_DIM = HEAD_DIM // 4
IN_COLS = 2 * RNN_WIDTH + ATTN_WIDTH + 2 * KV_WIDTH + 2 * D_MODEL
CROSS_HEADS = 4
CROSS_HEAD_DIM = D_MODEL // CROSS_HEADS
CROSS_WIDTH = CROSS_HEADS * CROSS_HEAD_DIM
D_FF = -(-8 * D_MODEL // (3 * 256)) * 256
LN_EPS = 1e-5
DEEPNORM_ALPHA = (2 * DEPTH) ** 0.25
DEEPNORM_BETA = (8 * DEPTH) ** -0.25
NEG_INF = -1e30

kernel_name = "hawk_swa_sink_hybrid_deepnorm_trunk"


def layer_norm(x, g, b):
    xf = x.astype(jnp.float32)
    mu = jnp.mean(xf, axis=-1, keepdims=True)
    var = jnp.mean(jnp.square(xf - mu), axis=-1, keepdims=True)
    y = (xf - mu) * lax.rsqrt(var + LN_EPS)
    return (y * g.astype(jnp.float32) + b.astype(jnp.float32)).astype(x.dtype)


def rope_tables(seq_len):
    pos = jnp.arange(seq_len, dtype=jnp.float32)
    inv_freq = ROPE_THETA ** (-jnp.arange(0, ROT_DIM, 2, dtype=jnp.float32) / ROT_DIM)
    ang = pos[:, None] * inv_freq[None, :]
    return jnp.cos(ang), jnp.sin(ang)


def apply_partial_rope(t, cos, sin):
    half = ROT_DIM // 2
    c = cos[None, :, None, :].astype(t.dtype)
    s = sin[None, :, None, :].astype(t.dtype)
    t1, t2, rest = t[..., :half], t[..., half:ROT_DIM], t[..., ROT_DIM:]
    return jnp.concatenate([t1 * c - t2 * s, t2 * c + t1 * s, rest], axis=-1)


def rglru_branch(xr, gr, conv_w, conv_b, w_rg, b_rg, w_ig, b_ig, lru_lambda):
    B, S, _ = xr.shape
    xp = jnp.pad(xr, ((0, 0), (CONV_WIDTH - 1, 0), (0, 0)))
    xc = conv_b
    for k in range(CONV_WIDTH):
        xc = xc + xp[:, k:k + S] * conv_w[k]
    xb = xc.reshape(B, S, RNN_BLOCKS, RNN_BLOCK)
    r = jax.nn.sigmoid(jnp.einsum('bsnc,ncd->bsnd', xb, w_rg).reshape(B, S, RNN_WIDTH) + b_rg)
    i = jax.nn.sigmoid(jnp.einsum('bsnc,ncd->bsnd', xb, w_ig).reshape(B, S, RNN_WIDTH) + b_ig)
    log_a = -LRU_C * r.astype(jnp.float32) * jax.nn.softplus(-lru_lambda.astype(jnp.float32))
    a = jnp.exp(log_a)
    mult = jnp.sqrt(-jnp.expm1(2.0 * log_a))
    b_in = mult * (i * xc).astype(jnp.float32)

    def combine(lhs, rhs):
        a1, b1 = lhs
        a2, b2 = rhs
        return a1 * a2, a2 * b1 + b2

    _, h = lax.associative_scan(combine, (a, b_in), axis=1)
    return h.astype(xr.dtype) * jax.nn.gelu(gr)


def swa_sink_branch(q, k, v, sinks, cos, sin):
    B, S, _ = q.shape
    NB = S // BLOCK
    q = apply_partial_rope(q.reshape(B, S, N_Q_HEADS, HEAD_DIM), cos, sin)
    k = apply_partial_rope(k.reshape(B, S, N_KV_HEADS, HEAD_DIM), cos, sin)
    v = v.reshape(B, S, N_KV_HEADS, HEAD_DIM)
    qb = q.reshape(B, NB, BLOCK, N_KV_HEADS, GROUP, HEAD_DIM)

    def band(t):
        tp = jnp.pad(t, ((0, 0), (BLOCK, 0), (0, 0), (0, 0))).reshape(B, NB + 1, BLOCK, N_KV_HEADS, HEAD_DIM)
        return jnp.concatenate([tp[:, :-1], tp[:, 1:]], axis=2)

    kb, vb = band(k), band(v)
    scores = jnp.einsum('bnqhgd,bnjhd->bnhgqj', qb, kb).astype(jnp.float32) * (HEAD_DIM ** -0.5)
    blk = jnp.arange(NB)[:, None, None]
    qpos = blk * BLOCK + jnp.arange(BLOCK)[None, :, None]
    kpos = (blk - 1) * BLOCK + jnp.arange(2 * BLOCK)[None, None, :]
    valid = (kpos <= qpos) & (kpos > qpos - WINDOW) & (kpos >= 0)
    scores = jnp.where(valid[None, :, None, None], scores, NEG_INF)
    sink = sinks.astype(jnp.float32).reshape(N_KV_HEADS, GROUP)[None, None, :, :, None, None]
    sink = jnp.broadcast_to(sink, scores.shape[:-1] + (1,))
    probs = jax.nn.softmax(jnp.concatenate([scores, sink], axis=-1), axis=-1)[..., :-1]
    out = jnp.einsum('bnhgqj,bnjhd->bnqhgd', probs.astype(vb.dtype), vb)
    return out.reshape(B, S, ATTN_WIDTH)


def hybrid_mixer(u, w_in, conv_w, conv_b, w_rg, b_rg, w_ig, b_ig, lru_lambda,
                 w_br_rnn, w_br_attn, sinks, w_out, cos, sin):
    widths = (RNN_WIDTH, RNN_WIDTH, ATTN_WIDTH, KV_WIDTH, KV_WIDTH, D_MODEL, D_MODEL)
    points = np.cumsum(widths)[:-1].tolist()
    proj = u @ w_in
    xr, gr, q, k, v, g_rnn, g_attn = jnp.split(proj, points, axis=-1)
    y_rnn = rglru_branch(xr, gr, conv_w, conv_b, w_rg, b_rg, w_ig, b_ig, lru_lambda)
    y_attn = swa_sink_branch(q, k, v, sinks, cos, sin)
    merged = jax.nn.sigmoid(g_rnn) * (y_rnn @ w_br_rnn) + jax.nn.sigmoid(g_attn) * (y_attn @ w_br_attn)
    return merged @ w_out


def cross_attention(u, mem, cq_w, ckv_w, co_w):
    B, S, _ = u.shape
    M = mem.shape[1]
    q = (u @ cq_w).reshape(B, S, CROSS_HEADS, CROSS_HEAD_DIM)
    k, v = jnp.split(mem @ ckv_w, 2, axis=-1)
    k = k.reshape(B, M, CROSS_HEADS, CROSS_HEAD_DIM)
    v = v.reshape(B, M, CROSS_HEADS, CROSS_HEAD_DIM)
    s = jnp.einsum('bshd,bmhd->bhsm', q, k).astype(jnp.float32) * (CROSS_HEAD_DIM ** -0.5)
    p = jax.nn.softmax(s, axis=-1)
    o = jnp.einsum('bhsm,bmhd->bshd', p.astype(v.dtype), v).reshape(B, S, CROSS_WIDTH)
    return o @ co_w


def swiglu(u, wi, wo):
    gate, up = jnp.split(u @ wi, 2, axis=-1)
    return (jax.nn.silu(gate) * up) @ wo


def _fwd_setup_inputs(seed: int = 0) -> dict:
    key = jax.random.key(seed)
    ks = jax.random.split(key, 26)
    L = DEPTH
    f32 = jnp.float32

    def nrm(k, shape, scale):
        return jax.random.normal(k, shape, f32) * scale

    u = jax.random.uniform(ks[9], (L, RNN_WIDTH), f32, 0.9, 0.999)
    p = u ** (1.0 / LRU_C)
    lru_lambda = jnp.log(p) - jnp.log1p(-p)
    return {
        "x": nrm(ks[0], (BATCH, SEQ, D_MODEL), 1.0),
        "mem": nrm(ks[1], (BATCH, MEM_LEN, D_MODEL), 1.0),
        "w_in": nrm(ks[2], (L, D_MODEL, IN_COLS), D_MODEL ** -0.5),
        "conv_w": nrm(ks[3], (L, CONV_WIDTH, RNN_WIDTH), CONV_WIDTH ** -0.5),
        "conv_b": nrm(ks[4], (L, RNN_WIDTH), 0.01),
        "w_rg": nrm(ks[5], (L, RNN_BLOCKS, RNN_BLOCK, RNN_BLOCK), RNN_BLOCK ** -0.5),
        "b_rg": nrm(ks[6], (L, RNN_WIDTH), 0.01),
        "w_ig": nrm(ks[7], (L, RNN_BLOCKS, RNN_BLOCK, RNN_BLOCK), RNN_BLOCK ** -0.5),
        "b_ig": nrm(ks[8], (L, RNN_WIDTH), 0.01),
        "lru_lambda": lru_lambda,
        "w_br_rnn": nrm(ks[10], (L, RNN_WIDTH, D_MODEL), RNN_WIDTH ** -0.5),
        "w_br_attn": nrm(ks[11], (L, ATTN_WIDTH, D_MODEL), ATTN_WIDTH ** -0.5),
        "sinks": nrm(ks[12], (L, N_Q_HEADS), 0.5),
        "w_out": nrm(ks[13], (L, D_MODEL, D_MODEL), DEEPNORM_BETA * D_MODEL ** -0.5),
        "ln1_g": 1.0 + nrm(ks[14], (L, D_MODEL), 0.02),
        "ln1_b": nrm(ks[15], (L, D_MODEL), 0.02),
        "cq_w": nrm(ks[16], (L, D_MODEL, CROSS_WIDTH), D_MODEL ** -0.5),
        "ckv_w": nrm(ks[17], (L, D_MODEL, 2 * CROSS_WIDTH), D_MODEL ** -0.5),
        "co_w": nrm(ks[18], (L, CROSS_WIDTH, D_MODEL), DEEPNORM_BETA * CROSS_WIDTH ** -0.5),
        "ln2_g": 1.0 + nrm(ks[19], (L, D_MODEL), 0.02),
        "ln2_b": nrm(ks[20], (L, D_MODEL), 0.02),
        "ffn_wi": nrm(ks[21], (L, D_MODEL, 2 * D_FF), D_MODEL ** -0.5),
        "ffn_wo": nrm(ks[22], (L, D_FF, D_MODEL), DEEPNORM_BETA * D_FF ** -0.5),
        "ln3_g": 1.0 + nrm(ks[23], (L, D_MODEL), 0.02),
        "ln3_b": nrm(ks[24], (L, D_MODEL), 0.02),
    }


def _fwd_reference(x, mem, w_in, conv_w, conv_b, w_rg, b_rg, w_ig, b_ig, lru_lambda,
              w_br_rnn, w_br_attn, sinks, w_out, ln1_g, ln1_b,
              cq_w, ckv_w, co_w, ln2_g, ln2_b,
              ffn_wi, ffn_wo, ln3_g, ln3_b):
    cos, sin = rope_tables(x.shape[1])
    h = x
    for l in range(DEPTH):
        mix = hybrid_mixer(h, w_in[l], conv_w[l], conv_b[l], w_rg[l], b_rg[l], w_ig[l], b_ig[l],
                           lru_lambda[l], w_br_rnn[l], w_br_attn[l], sinks[l], w_out[l], cos, sin)
        h = layer_norm(DEEPNORM_ALPHA * h + mix, ln1_g[l], ln1_b[l])
        h = layer_norm(DEEPNORM_ALPHA * h + cross_attention(h, mem, cq_w[l], ckv_w[l], co_w[l]),
                       ln2_g[l], ln2_b[l])
        h = layer_norm(DEEPNORM_ALPHA * h + swiglu(h, ffn_wi[l], ffn_wo[l]), ln3_g[l], ln3_b[l])
    return h


import jax as _jax
import jax.numpy as _jnp

TWIN_FORMAT = 'train_step'
FWD_PARAMS = ['x', 'mem', 'w_in', 'conv_w', 'conv_b', 'w_rg', 'b_rg', 'w_ig', 'b_ig', 'lru_lambda', 'w_br_rnn', 'w_br_attn', 'sinks', 'w_out', 'ln1_g', 'ln1_b', 'cq_w', 'ckv_w', 'co_w', 'ln2_g', 'ln2_b', 'ffn_wi', 'ffn_wo', 'ln3_g', 'ln3_b']
TWIN_WEIGHTS = ['w_in', 'conv_w', 'conv_b', 'w_rg', 'b_rg', 'w_ig', 'b_ig', 'lru_lambda', 'w_br_rnn', 'w_br_attn', 'sinks', 'w_out', 'ln1_g', 'ln1_b', 'cq_w', 'ckv_w', 'co_w', 'ln2_g', 'ln2_b', 'ffn_wi', 'ffn_wo', 'ln3_g', 'ln3_b']
TWIN_DIFF_INPUT = 'x'
TWIN_INPUTS = ['x', 'mem', 'w_in', 'conv_w', 'conv_b', 'w_rg', 'b_rg', 'w_ig', 'b_ig', 'lru_lambda', 'w_br_rnn', 'w_br_attn', 'sinks', 'w_out', 'ln1_g', 'ln1_b', 'cq_w', 'ckv_w', 'co_w', 'ln2_g', 'ln2_b', 'ffn_wi', 'ffn_wo', 'ln3_g', 'ln3_b', 'loss_target', 'm_w_in', 'm_conv_w', 'm_conv_b', 'm_w_rg', 'm_b_rg', 'm_w_ig', 'm_b_ig', 'm_lru_lambda', 'm_w_br_rnn', 'm_w_br_attn', 'm_sinks', 'm_w_out', 'm_ln1_g', 'm_ln1_b', 'm_cq_w', 'm_ckv_w', 'm_co_w', 'm_ln2_g', 'm_ln2_b', 'm_ffn_wi', 'm_ffn_wo', 'm_ln3_g', 'm_ln3_b', 'v_w_in', 'v_conv_w', 'v_conv_b', 'v_w_rg', 'v_b_rg', 'v_w_ig', 'v_b_ig', 'v_lru_lambda', 'v_w_br_rnn', 'v_w_br_attn', 'v_sinks', 'v_w_out', 'v_ln1_g', 'v_ln1_b', 'v_cq_w', 'v_ckv_w', 'v_co_w', 'v_ln2_g', 'v_ln2_b', 'v_ffn_wi', 'v_ffn_wo', 'v_ln3_g', 'v_ln3_b']
TWIN_OUTPUTS = ['loss', 'grad_x', 'grad_w_in', 'grad_conv_w', 'grad_conv_b', 'grad_w_rg', 'grad_b_rg', 'grad_w_ig', 'grad_b_ig', 'grad_lru_lambda', 'grad_w_br_rnn', 'grad_w_br_attn', 'grad_sinks', 'grad_w_out', 'grad_ln1_g', 'grad_ln1_b', 'grad_cq_w', 'grad_ckv_w', 'grad_co_w', 'grad_ln2_g', 'grad_ln2_b', 'grad_ffn_wi', 'grad_ffn_wo', 'grad_ln3_g', 'grad_ln3_b', 'delta_w_in', 'delta_conv_w', 'delta_conv_b', 'delta_w_rg', 'delta_b_rg', 'delta_w_ig', 'delta_b_ig', 'delta_lru_lambda', 'delta_w_br_rnn', 'delta_w_br_attn', 'delta_sinks', 'delta_w_out', 'delta_ln1_g', 'delta_ln1_b', 'delta_cq_w', 'delta_ckv_w', 'delta_co_w', 'delta_ln2_g', 'delta_ln2_b', 'delta_ffn_wi', 'delta_ffn_wo', 'delta_ln3_g', 'delta_ln3_b', 'new_m_w_in', 'new_m_conv_w', 'new_m_conv_b', 'new_m_w_rg', 'new_m_b_rg', 'new_m_w_ig', 'new_m_b_ig', 'new_m_lru_lambda', 'new_m_w_br_rnn', 'new_m_w_br_attn', 'new_m_sinks', 'new_m_w_out', 'new_m_ln1_g', 'new_m_ln1_b', 'new_m_cq_w', 'new_m_ckv_w', 'new_m_co_w', 'new_m_ln2_g', 'new_m_ln2_b', 'new_m_ffn_wi', 'new_m_ffn_wo', 'new_m_ln3_g', 'new_m_ln3_b', 'new_v_w_in', 'new_v_conv_w', 'new_v_conv_b', 'new_v_w_rg', 'new_v_b_rg', 'new_v_w_ig', 'new_v_b_ig', 'new_v_lru_lambda', 'new_v_w_br_rnn', 'new_v_w_br_attn', 'new_v_sinks', 'new_v_w_out', 'new_v_ln1_g', 'new_v_ln1_b', 'new_v_cq_w', 'new_v_ckv_w', 'new_v_co_w', 'new_v_ln2_g', 'new_v_ln2_b', 'new_v_ffn_wi', 'new_v_ffn_wo', 'new_v_ln3_g', 'new_v_ln3_b']
TWIN_LEAF_KINDS = {'loss': 'loss', 'grad_x': 'grad_x', 'grad_w_in': 'grad_w', 'grad_conv_w': 'grad_w', 'grad_conv_b': 'grad_w', 'grad_w_rg': 'grad_w', 'grad_b_rg': 'grad_w', 'grad_w_ig': 'grad_w', 'grad_b_ig': 'grad_w', 'grad_lru_lambda': 'grad_w', 'grad_w_br_rnn': 'grad_w', 'grad_w_br_attn': 'grad_w', 'grad_sinks': 'grad_w', 'grad_w_out': 'grad_w', 'grad_ln1_g': 'grad_w', 'grad_ln1_b': 'grad_w', 'grad_cq_w': 'grad_w', 'grad_ckv_w': 'grad_w', 'grad_co_w': 'grad_w', 'grad_ln2_g': 'grad_w', 'grad_ln2_b': 'grad_w', 'grad_ffn_wi': 'grad_w', 'grad_ffn_wo': 'grad_w', 'grad_ln3_g': 'grad_w', 'grad_ln3_b': 'grad_w', 'delta_w_in': 'delta_w', 'delta_conv_w': 'delta_w', 'delta_conv_b': 'delta_w', 'delta_w_rg': 'delta_w', 'delta_b_rg': 'delta_w', 'delta_w_ig': 'delta_w', 'delta_b_ig': 'delta_w', 'delta_lru_lambda': 'delta_w', 'delta_w_br_rnn': 'delta_w', 'delta_w_br_attn': 'delta_w', 'delta_sinks': 'delta_w', 'delta_w_out': 'delta_w', 'delta_ln1_g': 'delta_w', 'delta_ln1_b': 'delta_w', 'delta_cq_w': 'delta_w', 'delta_ckv_w': 'delta_w', 'delta_co_w': 'delta_w', 'delta_ln2_g': 'delta_w', 'delta_ln2_b': 'delta_w', 'delta_ffn_wi': 'delta_w', 'delta_ffn_wo': 'delta_w', 'delta_ln3_g': 'delta_w', 'delta_ln3_b': 'delta_w', 'new_m_w_in': 'new_m', 'new_m_conv_w': 'new_m', 'new_m_conv_b': 'new_m', 'new_m_w_rg': 'new_m', 'new_m_b_rg': 'new_m', 'new_m_w_ig': 'new_m', 'new_m_b_ig': 'new_m', 'new_m_lru_lambda': 'new_m', 'new_m_w_br_rnn': 'new_m', 'new_m_w_br_attn': 'new_m', 'new_m_sinks': 'new_m', 'new_m_w_out': 'new_m', 'new_m_ln1_g': 'new_m', 'new_m_ln1_b': 'new_m', 'new_m_cq_w': 'new_m', 'new_m_ckv_w': 'new_m', 'new_m_co_w': 'new_m', 'new_m_ln2_g': 'new_m', 'new_m_ln2_b': 'new_m', 'new_m_ffn_wi': 'new_m', 'new_m_ffn_wo': 'new_m', 'new_m_ln3_g': 'new_m', 'new_m_ln3_b': 'new_m', 'new_v_w_in': 'new_v', 'new_v_conv_w': 'new_v', 'new_v_conv_b': 'new_v', 'new_v_w_rg': 'new_v', 'new_v_b_rg': 'new_v', 'new_v_w_ig': 'new_v', 'new_v_b_ig': 'new_v', 'new_v_lru_lambda': 'new_v', 'new_v_w_br_rnn': 'new_v', 'new_v_w_br_attn': 'new_v', 'new_v_sinks': 'new_v', 'new_v_w_out': 'new_v', 'new_v_ln1_g': 'new_v', 'new_v_ln1_b': 'new_v', 'new_v_cq_w': 'new_v', 'new_v_ckv_w': 'new_v', 'new_v_co_w': 'new_v', 'new_v_ln2_g': 'new_v', 'new_v_ln2_b': 'new_v', 'new_v_ffn_wi': 'new_v', 'new_v_ffn_wo': 'new_v', 'new_v_ln3_g': 'new_v', 'new_v_ln3_b': 'new_v'}


def _forward(args):
    return _fwd_reference(*[args[k] for k in FWD_PARAMS])


def _output_shape():
    out = _jax.eval_shape(lambda: _forward(_fwd_setup_inputs(0)))
    return out.shape, out.dtype

N_MICROBATCH = 1
ADAM_LR = 0.001
ADAM_B1 = 0.9
ADAM_B2 = 0.999
ADAM_EPS = 1e-08
ADAM_WD = 0.01
ADAM_STEP = 10
PER_EXAMPLE_BATCH_AXIS = {'x': 0, 'mem': 0, 'loss_target': 0}
SHARED_INPUTS = []
_WEIGHT_DTYPES = {'w_in': _jnp.float32, 'conv_w': _jnp.float32, 'conv_b': _jnp.float32, 'w_rg': _jnp.float32, 'b_rg': _jnp.float32, 'w_ig': _jnp.float32, 'b_ig': _jnp.float32, 'lru_lambda': _jnp.float32, 'w_br_rnn': _jnp.float32, 'w_br_attn': _jnp.float32, 'sinks': _jnp.float32, 'w_out': _jnp.float32, 'ln1_g': _jnp.float32, 'ln1_b': _jnp.float32, 'cq_w': _jnp.float32, 'ckv_w': _jnp.float32, 'co_w': _jnp.float32, 'ln2_g': _jnp.float32, 'ln2_b': _jnp.float32, 'ffn_wi': _jnp.float32, 'ffn_wo': _jnp.float32, 'ln3_g': _jnp.float32, 'ln3_b': _jnp.float32}
MOMENT_SCALE = {'w_in': 5.461891e-03, 'conv_w': 8.620389e-03, 'conv_b': 9.606812e-02, 'w_rg': 2.158815e-03, 'b_rg': 1.931262e-03, 'w_ig': 3.847663e-03, 'b_ig': 3.422812e-03, 'lru_lambda': 4.120119e-03, 'w_br_rnn': 8.490272e-03, 'w_br_attn': 3.832981e-03, 'sinks': 2.768807e-03, 'w_out': 2.049093e-02, 'ln1_g': 5.664227e-01, 'ln1_b': 2.972499e-01, 'cq_w': 3.382050e-03, 'ckv_w': 3.740446e-03, 'co_w': 9.587400e-03, 'ln2_g': 5.665745e-01, 'ln2_b': 2.975397e-01, 'ffn_wi': 1.156446e-02, 'ffn_wo': 4.487069e-02, 'ln3_g': 8.067704e+00, 'ln3_b': 6.116907e-01}


def _to_microbatches(a, axis):
    t = _jnp.moveaxis(a, axis, 0)
    t = t.reshape((N_MICROBATCH, t.shape[0] // N_MICROBATCH) + t.shape[1:])
    return _jnp.moveaxis(t, 1, axis + 1)


def setup_inputs(seed: int = 0) -> dict:
    inp = _fwd_setup_inputs(seed)
    key = _jax.random.fold_in(_jax.random.key(seed), 7919)
    shape, _ = _output_shape()
    out = dict(inp)
    out["loss_target"] = _jax.random.normal(_jax.random.fold_in(key, 0), shape, _jnp.float32)
    for i, name in enumerate(TWIN_WEIGHTS):
        w = inp[name].astype(_jnp.float32)
        if MOMENT_SCALE is None:
            s = _jnp.sqrt(_jnp.mean(_jnp.square(w)) + 1e-30)
        else:
            s = MOMENT_SCALE[name]
        km, kv = _jax.random.split(_jax.random.fold_in(key, i + 1))
        out[name] = w
        out["m_" + name] = s * _jax.random.normal(km, w.shape, _jnp.float32)
        out["v_" + name] = (s * s) * _jax.random.uniform(kv, w.shape, _jnp.float32, 0.5, 1.5)
    if N_MICROBATCH > 1:
        for name, axis in PER_EXAMPLE_BATCH_AXIS.items():
            out[name] = _to_microbatches(out[name], axis)
    return {'x': out['x'], 'mem': out['mem'], 'w_in': out['w_in'], 'conv_w': out['conv_w'], 'conv_b': out['conv_b'], 'w_rg': out['w_rg'], 'b_rg': out['b_rg'], 'w_ig': out['w_ig'], 'b_ig': out['b_ig'], 'lru_lambda': out['lru_lambda'], 'w_br_rnn': out['w_br_rnn'], 'w_br_attn': out['w_br_attn'], 'sinks': out['sinks'], 'w_out': out['w_out'], 'ln1_g': out['ln1_g'], 'ln1_b': out['ln1_b'], 'cq_w': out['cq_w'], 'ckv_w': out['ckv_w'], 'co_w': out['co_w'], 'ln2_g': out['ln2_g'], 'ln2_b': out['ln2_b'], 'ffn_wi': out['ffn_wi'], 'ffn_wo': out['ffn_wo'], 'ln3_g': out['ln3_g'], 'ln3_b': out['ln3_b'], 'loss_target': out['loss_target'], 'm_w_in': out['m_w_in'], 'm_conv_w': out['m_conv_w'], 'm_conv_b': out['m_conv_b'], 'm_w_rg': out['m_w_rg'], 'm_b_rg': out['m_b_rg'], 'm_w_ig': out['m_w_ig'], 'm_b_ig': out['m_b_ig'], 'm_lru_lambda': out['m_lru_lambda'], 'm_w_br_rnn': out['m_w_br_rnn'], 'm_w_br_attn': out['m_w_br_attn'], 'm_sinks': out['m_sinks'], 'm_w_out': out['m_w_out'], 'm_ln1_g': out['m_ln1_g'], 'm_ln1_b': out['m_ln1_b'], 'm_cq_w': out['m_cq_w'], 'm_ckv_w': out['m_ckv_w'], 'm_co_w': out['m_co_w'], 'm_ln2_g': out['m_ln2_g'], 'm_ln2_b': out['m_ln2_b'], 'm_ffn_wi': out['m_ffn_wi'], 'm_ffn_wo': out['m_ffn_wo'], 'm_ln3_g': out['m_ln3_g'], 'm_ln3_b': out['m_ln3_b'], 'v_w_in': out['v_w_in'], 'v_conv_w': out['v_conv_w'], 'v_conv_b': out['v_conv_b'], 'v_w_rg': out['v_w_rg'], 'v_b_rg': out['v_b_rg'], 'v_w_ig': out['v_w_ig'], 'v_b_ig': out['v_b_ig'], 'v_lru_lambda': out['v_lru_lambda'], 'v_w_br_rnn': out['v_w_br_rnn'], 'v_w_br_attn': out['v_w_br_attn'], 'v_sinks': out['v_sinks'], 'v_w_out': out['v_w_out'], 'v_ln1_g': out['v_ln1_g'], 'v_ln1_b': out['v_ln1_b'], 'v_cq_w': out['v_cq_w'], 'v_ckv_w': out['v_ckv_w'], 'v_co_w': out['v_co_w'], 'v_ln2_g': out['v_ln2_g'], 'v_ln2_b': out['v_ln2_b'], 'v_ffn_wi': out['v_ffn_wi'], 'v_ffn_wo': out['v_ffn_wo'], 'v_ln3_g': out['v_ln3_g'], 'v_ln3_b': out['v_ln3_b']}


def _loss(weights, diff, rest, loss_target):
    with _jax.named_scope("forward"):
        args = {**rest, TWIN_DIFF_INPUT: diff, **{k: w.astype(_WEIGHT_DTYPES[k]) for k, w in weights.items()}}
        y = _forward(args)
    with _jax.named_scope("loss_head"):
        err = _jnp.square(y.astype(_jnp.float32) - loss_target)
        return 0.5 * _jnp.sum(_jnp.mean(err, axis=-1)) if err.ndim else 0.5 * err


def _adamw(w, g, m, v):
    m = ADAM_B1 * m + (1.0 - ADAM_B1) * g
    v = ADAM_B2 * v + (1.0 - ADAM_B2) * _jnp.square(g)
    m_hat = m / (1.0 - ADAM_B1 ** ADAM_STEP)
    v_hat = v / (1.0 - ADAM_B2 ** ADAM_STEP)
    delta = -ADAM_LR * (m_hat / (_jnp.sqrt(v_hat) + ADAM_EPS) + ADAM_WD * w)
    return delta, m, v


def reference(x, mem, w_in, conv_w, conv_b, w_rg, b_rg, w_ig, b_ig, lru_lambda, w_br_rnn, w_br_attn, sinks, w_out, ln1_g, ln1_b, cq_w, ckv_w, co_w, ln2_g, ln2_b, ffn_wi, ffn_wo, ln3_g, ln3_b, loss_target, m_w_in, m_conv_w, m_conv_b, m_w_rg, m_b_rg, m_w_ig, m_b_ig, m_lru_lambda, m_w_br_rnn, m_w_br_attn, m_sinks, m_w_out, m_ln1_g, m_ln1_b, m_cq_w, m_ckv_w, m_co_w, m_ln2_g, m_ln2_b, m_ffn_wi, m_ffn_wo, m_ln3_g, m_ln3_b, v_w_in, v_conv_w, v_conv_b, v_w_rg, v_b_rg, v_w_ig, v_b_ig, v_lru_lambda, v_w_br_rnn, v_w_br_attn, v_sinks, v_w_out, v_ln1_g, v_ln1_b, v_cq_w, v_ckv_w, v_co_w, v_ln2_g, v_ln2_b, v_ffn_wi, v_ffn_wo, v_ln3_g, v_ln3_b):
    given = dict(x=x, mem=mem, w_in=w_in, conv_w=conv_w, conv_b=conv_b, w_rg=w_rg, b_rg=b_rg, w_ig=w_ig, b_ig=b_ig, lru_lambda=lru_lambda, w_br_rnn=w_br_rnn, w_br_attn=w_br_attn, sinks=sinks, w_out=w_out, ln1_g=ln1_g, ln1_b=ln1_b, cq_w=cq_w, ckv_w=ckv_w, co_w=co_w, ln2_g=ln2_g, ln2_b=ln2_b, ffn_wi=ffn_wi, ffn_wo=ffn_wo, ln3_g=ln3_g, ln3_b=ln3_b, loss_target=loss_target, m_w_in=m_w_in, m_conv_w=m_conv_w, m_conv_b=m_conv_b, m_w_rg=m_w_rg, m_b_rg=m_b_rg, m_w_ig=m_w_ig, m_b_ig=m_b_ig, m_lru_lambda=m_lru_lambda, m_w_br_rnn=m_w_br_rnn, m_w_br_attn=m_w_br_attn, m_sinks=m_sinks, m_w_out=m_w_out, m_ln1_g=m_ln1_g, m_ln1_b=m_ln1_b, m_cq_w=m_cq_w, m_ckv_w=m_ckv_w, m_co_w=m_co_w, m_ln2_g=m_ln2_g, m_ln2_b=m_ln2_b, m_ffn_wi=m_ffn_wi, m_ffn_wo=m_ffn_wo, m_ln3_g=m_ln3_g, m_ln3_b=m_ln3_b, v_w_in=v_w_in, v_conv_w=v_conv_w, v_conv_b=v_conv_b, v_w_rg=v_w_rg, v_b_rg=v_b_rg, v_w_ig=v_w_ig, v_b_ig=v_b_ig, v_lru_lambda=v_lru_lambda, v_w_br_rnn=v_w_br_rnn, v_w_br_attn=v_w_br_attn, v_sinks=v_sinks, v_w_out=v_w_out, v_ln1_g=v_ln1_g, v_ln1_b=v_ln1_b, v_cq_w=v_cq_w, v_ckv_w=v_ckv_w, v_co_w=v_co_w, v_ln2_g=v_ln2_g, v_ln2_b=v_ln2_b, v_ffn_wi=v_ffn_wi, v_ffn_wo=v_ffn_wo, v_ln3_g=v_ln3_g, v_ln3_b=v_ln3_b)
    weights = {n: given[n] for n in TWIN_WEIGHTS}
    shared = {n: given[n] for n in SHARED_INPUTS}
    per_example = {n: given[n] for n in ['x', 'mem']}
    grad_fn = _jax.value_and_grad(_loss, argnums=(0, 1))

    def one_microbatch(ex, loss_target):
        ex = dict(ex)
        diff = ex.pop(TWIN_DIFF_INPUT)
        return grad_fn(weights, diff, {**shared, **ex}, loss_target)

    if N_MICROBATCH == 1:
        loss, (grad_w, grad_x) = one_microbatch(per_example, given["loss_target"])
    else:
        def body(carry, xs):
            loss_sum, grad_sum = carry
            l_k, (gw_k, gx_k) = one_microbatch(xs[0], xs[1])
            with _jax.named_scope("update"):
                return (loss_sum + l_k, _jax.tree.map(_jnp.add, grad_sum, gw_k)), gx_k

        init = (_jnp.zeros((), _jnp.float32), _jax.tree.map(_jnp.zeros_like, weights))
        (loss, grad_w), grad_x = _jax.lax.scan(body, init, (per_example, given["loss_target"]))
    with _jax.named_scope("update"):
        delta_w, new_m, new_v = {}, {}, {}
        for n in TWIN_WEIGHTS:
            delta_w[n], new_m[n], new_v[n] = _adamw(weights[n], grad_w[n], given["m_" + n], given["v_" + n])
    return (loss, grad_x, *[grad_w[n] for n in TWIN_WEIGHTS], *[delta_w[n] for n in TWIN_WEIGHTS],
            *[new_m[n] for n in TWIN_WEIGHTS], *[new_v[n] for n in TWIN_WEIGHTS])
```

```python
import functools
import math

import jax
import jax.numpy as jnp
from jax import lax
from jax.experimental import pallas as pl
from jax.experimental.pallas import tpu as pltpu

F32 = jnp.float32
BF16 = jnp.bfloat16

D_MODEL = 1024
DEPTH = 4
N_DEV = 8
RNN_BLOCKS = 4
RNN_BLOCK = 256
CONV_WIDTH = 4
LRU_C = 8.0
HEAD_DIM = 64
N_Q_HEADS = 16
N_KV_HEADS = 2
GROUP = 8
KV_WIDTH = 128
BLOCK = 128
ROPE_THETA = 500000.0
ROT_DIM = 16
IN_COLS = 5376
CROSS_HEADS = 4
CROSS_HEAD_DIM = 256
D_FF = 2816
LN_EPS = 1e-5
ALPHA = (2 * DEPTH) ** 0.25
NEG_INF = -1e30

ADAM_LR = 0.001
ADAM_B1 = 0.9
ADAM_B2 = 0.999
ADAM_EPS = 1e-08
ADAM_WD = 0.01
ADAM_STEP = 10

C_XR, C_GR, C_Q, C_K, C_V, C_GRNN, C_GATTN = 0, 1024, 2048, 3072, 3200, 3328, 4352

TIME_CHUNK = 256
ROW_TILE = 256

MESH_AXES = ("x", "y", "c")

PACK_ROWS = (("w_in", 672), ("w_br_rnn", 128), ("w_br_attn", 128), ("w_out", 128), ("cq_w", 128),
             ("ckv_w", 256), ("co_w", 128), ("ffn_wi", 704), ("ffn_wo", 352), ("w_rg", 32), ("w_ig", 32))
PACK_OFF = {}
_o = 0
for _n, _r in PACK_ROWS:
    PACK_OFF[_n] = (_o, _r)
    _o += _r
LAYER_ROWS = _o

SMALL_NAMES = ("conv_b", "b_rg", "b_ig", "lru_lambda", "sinks", "ln1_g", "ln1_b", "ln2_g", "ln2_b", "ln3_g", "ln3_b")
SMALL_ROWS = 16


def _pick(dim, cands):
    for c in cands:
        if dim % c == 0:
            return c
    return dim


_DIMS = {"nn": (((1,), (0,)), ((), ())), "nt": (((1,), (1,)), ((), ())), "tn": (((0,), (0,)), ((), ()))}


def _matmul(a, b, mode, name, add=None, add_scale=1.0):
    if mode == "nn":
        (M, K), (_, N) = a.shape, b.shape
    elif mode == "nt":
        (M, K), (N, _) = a.shape, b.shape
    else:
        (K, M), (_, N) = a.shape, b.shape
    tm = _pick(M, (512, 256, 128, 64, 32, 16, 8))
    tn = _pick(N, (768, 512, 256, 128))
    tk = _pick(K, (1792, 1408, 1024, 768, 512, 256, 128))
    nk = K // tk
    dims = _DIMS[mode]

    def body(*refs):
        if add is None:
            a_ref, b_ref, o_ref, acc_ref = refs
        else:
            a_ref, b_ref, c_ref, o_ref, acc_ref = refs
        k = pl.program_id(2)

        @pl.when(k == 0)
        def _():
            acc_ref[...] = jnp.zeros_like(acc_ref)

        acc_ref[...] += lax.dot_general(a_ref[...].astype(BF16), b_ref[...].astype(BF16), dims,
                                        preferred_element_type=F32)

        @pl.when(k == nk - 1)
        def _():
            r = acc_ref[...]
            if add is not None:
                r = r + add_scale * c_ref[...]
            o_ref[...] = r

    if mode == "nn":
        a_spec = pl.BlockSpec((tm, tk), lambda i, j, k: (i, k))
        b_spec = pl.BlockSpec((tk, tn), lambda i, j, k: (k, j))
    elif mode == "nt":
        a_spec = pl.BlockSpec((tm, tk), lambda i, j, k: (i, k))
        b_spec = pl.BlockSpec((tn, tk), lambda i, j, k: (j, k))
    else:
        a_spec = pl.BlockSpec((tk, tm), lambda i, j, k: (k, i))
        b_spec = pl.BlockSpec((tk, tn), lambda i, j, k: (k, j))
    in_specs = [a_spec, b_spec]
    args = [a, b]
    if add is not None:
        in_specs.append(pl.BlockSpec((tm, tn), lambda i, j, k: (i, j)))
        args.append(add)
    return pl.pallas_call(
        body, name=name, grid=(M // tm, N // tn, nk),
        in_specs=in_specs, out_specs=pl.BlockSpec((tm, tn), lambda i, j, k: (i, j)),
        out_shape=jax.ShapeDtypeStruct((M, N), F32),
        scratch_shapes=[pltpu.VMEM((tm, tn), F32)],
        compiler_params=pltpu.CompilerParams(dimension_semantics=("parallel", "parallel", "arbitrary")),
    )(*args)


def _ln_fwd(h, f, g, b, name):
    T, D = h.shape
    tr = _pick(T, (ROW_TILE, 128, 64, 32, 16, 8))

    def body(h_ref, f_ref, g_ref, b_ref, o_ref, xh_ref, rs_ref):
        z = ALPHA * h_ref[...] + f_ref[...]
        mu = jnp.mean(z, axis=-1, keepdims=True)
        zc = z - mu
        var = jnp.mean(zc * zc, axis=-1, keepdims=True)
        rstd = lax.rsqrt(var + LN_EPS)
        xh = zc * rstd
        xh_ref[...] = xh
        rs_ref[...] = rstd
        o_ref[...] = xh * g_ref[...] + b_ref[...]

    row = pl.BlockSpec((tr, D), lambda i: (i, 0))
    vec = pl.BlockSpec((1, D), lambda i: (0, 0))
    return pl.pallas_call(
        body, name=name, grid=(T // tr,), in_specs=[row, row, vec, vec],
        out_specs=[row, row, pl.BlockSpec((tr, 1), lambda i: (i, 0))],
        out_shape=[jax.ShapeDtypeStruct((T, D), F32), jax.ShapeDtypeStruct((T, D), F32),
                   jax.ShapeDtypeStruct((T, 1), F32)],
        compiler_params=pltpu.CompilerParams(dimension_semantics=("parallel",)),
    )(h, f, g, b)


def _ln_bwd(dout, xh, rstd, g, name):
    T, D = dout.shape
    tr = _pick(T, (ROW_TILE, 128, 64, 32, 16, 8))

    def body(do_ref, xh_ref, rs_ref, g_ref, dz_ref, dg_ref, db_ref):
        @pl.when(pl.program_id(0) == 0)
        def _():
            dg_ref[...] = jnp.zeros_like(dg_ref)
            db_ref[...] = jnp.zeros_like(db_ref)

        do = do_ref[...]
        xh = xh_ref[...]
        dxh = do * g_ref[...]
        m1 = jnp.mean(dxh, axis=-1, keepdims=True)
        m2 = jnp.mean(dxh * xh, axis=-1, keepdims=True)
        dz_ref[...] = rs_ref[...] * (dxh - m1 - xh * m2)
        dg_ref[...] += jnp.sum(do * xh, axis=0, keepdims=True)
        db_ref[...] += jnp.sum(do, axis=0, keepdims=True)

    row = pl.BlockSpec((tr, D), lambda i: (i, 0))
    vec = pl.BlockSpec((1, D), lambda i: (0, 0))
    return pl.pallas_call(
        body, name=name, grid=(T // tr,),
        in_specs=[row, row, pl.BlockSpec((tr, 1), lambda i: (i, 0)), vec],
        out_specs=[row, vec, vec],
        out_shape=[jax.ShapeDtypeStruct((T, D), F32), jax.ShapeDtypeStruct((1, D), F32),
                   jax.ShapeDtypeStruct((1, D), F32)],
        compiler_params=pltpu.CompilerParams(dimension_semantics=("arbitrary",)),
    )(dout, xh, rstd, g)


_GELU_C = math.sqrt(2.0 / math.pi)


def _gelu(x):
    t = jnp.tanh(_GELU_C * (x + 0.044715 * x * x * x))
    return 0.5 * x * (1.0 + t), t


def _gelu_grad(x, t):
    return 0.5 * (1.0 + t) + 0.5 * x * (1.0 - t * t) * _GELU_C * (1.0 + 3 * 0.044715 * x * x)


def _sigmoid(x):
    return 1.0 / (1.0 + jnp.exp(-x))


def _softplus_neg(lam):
    z = jnp.exp(-jnp.abs(lam))
    u = 1.0 + z
    l1p = jnp.where(u == 1.0, z, jnp.log(u) * z / jnp.where(u == 1.0, 1.0, u - 1.0))
    return jnp.maximum(-lam, 0.0) + l1p


def _neg_expm1(x):
    series = x * (1.0 + x * 0.5 * (1.0 + x * (1.0 / 3.0) * (1.0 + x * 0.25 * (1.0 + x * 0.2))))
    return -jnp.where(x > -0.05, series, jnp.exp(x) - 1.0)


def _scan_fwd(a, b):
    n = a.shape[0]
    rows = lax.broadcasted_iota(jnp.int32, a.shape, 0)
    s = 1
    while s < n:
        keep = rows >= s
        b = jnp.where(keep, a * pltpu.roll(b, s, 0) + b, b)
        a = jnp.where(keep, a * pltpu.roll(a, s, 0), a)
        s *= 2
    return a, b


def _scan_bwd(c, b):
    n = c.shape[0]
    rows = lax.broadcasted_iota(jnp.int32, c.shape, 0)
    s = 1
    while s < n:
        keep = rows < n - s
        b = jnp.where(keep, c * pltpu.roll(b, n - s, 0) + b, b)
        c = jnp.where(keep, c * pltpu.roll(c, n - s, 0), c)
        s *= 2
    return c, b


def _rnn_gates(xc, wr, br, wi, bi, sp):
    xb = xc.astype(BF16)
    r = _sigmoid(jnp.dot(xb, wr, preferred_element_type=F32) + br)
    i = _sigmoid(jnp.dot(xb, wi, preferred_element_type=F32) + bi)
    la = -LRU_C * r * sp
    a = jnp.exp(la)
    om = _neg_expm1(2.0 * la)
    mult = jnp.sqrt(om)
    return r, i, a, om, mult


def _rnn_specs(T):
    C = RNN_BLOCK
    col = lambda off: pl.BlockSpec((T, C), lambda n, off=off: (0, off // C + n))
    vec = pl.BlockSpec((1, C), lambda n: (0, n))
    cw = pl.BlockSpec((CONV_WIDTH, C), lambda n: (0, n))
    w = pl.BlockSpec((1, C, C), lambda n: (n, 0, 0))
    own = pl.BlockSpec((T, C), lambda n: (0, n))
    return col, vec, cw, w, own


def _rnn_fwd(P, cw, cb, wrg, brg, wig, big, lam, name):
    T = P.shape[0]
    C = RNN_BLOCK
    tc = _pick(T, (TIME_CHUNK,))
    nch = T // tc

    def body(x_ref, g_ref, cw_ref, cb_ref, wr_ref, br_ref, wi_ref, bi_ref, lam_ref, y_ref, hs_ref, xs_ref):
        sp = _softplus_neg(lam_ref[...])
        wr = wr_ref[0]
        wi = wi_ref[0]
        xs_ref[0:8, :] = jnp.zeros((8, C), F32)

        def chunk(c, hprev):
            r0 = pl.multiple_of(c * tc, tc)
            x = x_ref[pl.ds(r0, tc), :]
            xs_ref[8:, :] = x
            xc = cb_ref[...] + jnp.zeros((tc, C), F32)
            for k in range(CONV_WIDTH):
                xc = xc + xs_ref[pl.ds(8 - (CONV_WIDTH - 1 - k), tc), :] * cw_ref[k:k + 1, :]
            xs_ref[0:8, :] = x[tc - 8:, :]
            r, i, a, om, mult = _rnn_gates(xc, wr, br_ref[...], wi, bi_ref[...], sp)
            acum, bcum = _scan_fwd(a, mult * (i * xc))
            h = acum * hprev + bcum
            hs_ref[pl.ds(r0, tc), :] = h
            ge, _ = _gelu(g_ref[pl.ds(r0, tc), :])
            y_ref[pl.ds(r0, tc), :] = h * ge
            return h[tc - 1:tc, :]

        lax.fori_loop(0, nch, chunk, jnp.zeros((1, C), F32))

    col, vec, cwspec, w, own = _rnn_specs(T)
    return pl.pallas_call(
        body, name=name, grid=(RNN_BLOCKS,),
        in_specs=[col(C_XR), col(C_GR), cwspec, vec, w, vec, w, vec, vec],
        out_specs=[own, own],
        out_shape=[jax.ShapeDtypeStruct((T, D_MODEL), F32), jax.ShapeDtypeStruct((T, D_MODEL), F32)],
        scratch_shapes=[pltpu.VMEM((tc + 8, C), F32)],
        compiler_params=pltpu.CompilerParams(dimension_semantics=("parallel",)),
    )(P, P, cw, cb, wrg, brg, wig, big, lam)


def _rnn_bwd(P, hs, dy, cw, cb, wrg, brg, wig, big, lam, name):
    T = P.shape[0]
    C = RNN_BLOCK
    tc = _pick(T, (TIME_CHUNK,))
    nch = T // tc

    def body(x_ref, g_ref, hs_ref, dy_ref, cw_ref, cb_ref, wr_ref, br_ref, wi_ref, bi_ref, lam_ref,
             dx_ref, dg_ref, dcw_ref, dcb_ref, dwr_ref, dbr_ref, dwi_ref, dbi_ref, dlam_ref,
             xs_ref, hp_ref, an_ref, dn_ref):
        lam_v = lam_ref[...]
        sp = _softplus_neg(lam_v)
        wr = wr_ref[0]
        wi = wi_ref[0]
        dcw_ref[...] = jnp.zeros_like(dcw_ref)
        dcb_ref[...] = jnp.zeros_like(dcb_ref)
        dwr_ref[...] = jnp.zeros_like(dwr_ref)
        dbr_ref[...] = jnp.zeros_like(dbr_ref)
        dwi_ref[...] = jnp.zeros_like(dwi_ref)
        dbi_ref[...] = jnp.zeros_like(dbi_ref)
        dlam_ref[...] = jnp.zeros_like(dlam_ref)
        an_ref[tc:, :] = jnp.zeros((8, C), F32)
        dn_ref[tc:, :] = jnp.zeros((8, C), F32)

        def chunk(step, gnext):
            c = nch - 1 - step
            r0 = pl.multiple_of(c * tc, tc)
            p0 = pl.multiple_of(jnp.maximum(r0 - 8, 0), 8)
            live = c > 0
            x = x_ref[pl.ds(r0, tc), :]
            xs_ref[0:8, :] = jnp.where(live, x_ref[pl.ds(p0, 8), :], 0.0)
            xs_ref[8:, :] = x
            xsh = [xs_ref[pl.ds(8 - (CONV_WIDTH - 1 - k), tc), :] for k in range(CONV_WIDTH)]
            xc = cb_ref[...] + jnp.zeros((tc, C), F32)
            for k in range(CONV_WIDTH):
                xc = xc + xsh[k] * cw_ref[k:k + 1, :]
            r, i, a, om, mult = _rnn_gates(xc, wr, br_ref[...], wi, bi_ref[...], sp)
            h = hs_ref[pl.ds(r0, tc), :]
            hp_ref[0:8, :] = jnp.where(live, hs_ref[pl.ds(p0, 8), :], 0.0)
            hp_ref[8:, :] = h
            hm1 = hp_ref[pl.ds(7, tc), :]
            g = g_ref[pl.ds(r0, tc), :]
            ge, th = _gelu(g)
            dy = dy_ref[pl.ds(r0, tc), :]
            dg_ref[pl.ds(r0, tc), :] = dy * h * _gelu_grad(g, th)
            an_ref[0:tc, :] = a
            coef = an_ref[pl.ds(1, tc), :]
            ccum, bcum = _scan_bwd(coef, dy * ge)
            G = bcum + ccum * gnext
            an_ref[tc:, :] = a[0:8, :]
            da = G * hm1
            ixc = i * xc
            dmult = G * ixc
            di = G * mult * xc
            dxc = G * mult * i
            dla = da * a - dmult * (1.0 - om) / mult
            dr = dla * (-LRU_C * sp)
            dlam_ref[...] += jnp.sum(dla * r, axis=0, keepdims=True)
            dzr = dr * r * (1.0 - r)
            dzi = di * i * (1.0 - i)
            dbr_ref[...] += jnp.sum(dzr, axis=0, keepdims=True)
            dbi_ref[...] += jnp.sum(dzi, axis=0, keepdims=True)
            xb = xc.astype(BF16)
            dzrb = dzr.astype(BF16)
            dzib = dzi.astype(BF16)
            dwr_ref[0] += lax.dot_general(xb, dzrb, _DIMS["tn"], preferred_element_type=F32)
            dwi_ref[0] += lax.dot_general(xb, dzib, _DIMS["tn"], preferred_element_type=F32)
            dxc = dxc + lax.dot_general(dzrb, wr, _DIMS["nt"], preferred_element_type=F32)
            dxc = dxc + lax.dot_general(dzib, wi, _DIMS["nt"], preferred_element_type=F32)
            dcb_ref[...] += jnp.sum(dxc, axis=0, keepdims=True)
            for k in range(CONV_WIDTH):
                dcw_ref[k:k + 1, :] += jnp.sum(dxc * xsh[k], axis=0, keepdims=True)
            dn_ref[0:tc, :] = dxc
            dx = jnp.zeros((tc, C), F32)
            for k in range(CONV_WIDTH):
                dx = dx + dn_ref[pl.ds(CONV_WIDTH - 1 - k, tc), :] * cw_ref[k:k + 1, :]
            dn_ref[tc:, :] = dxc[0:8, :]
            dx_ref[pl.ds(r0, tc), :] = dx
            return G[0:1, :]

        lax.fori_loop(0, nch, chunk, jnp.zeros((1, C), F32))
        dlam_ref[...] = dlam_ref[...] * (LRU_C * _sigmoid(-lam_v))

    col, vec, cwspec, w, own = _rnn_specs(T)
    vshape = jax.ShapeDtypeStruct((1, D_MODEL), F32)
    wshape = jax.ShapeDtypeStruct((RNN_BLOCKS, C, C), F32)
    return pl.pallas_call(
        body, name=name, grid=(RNN_BLOCKS,),
        in_specs=[col(C_XR), col(C_GR), own, own, cwspec, vec, w, vec, w, vec, vec],
        out_specs=[own, own, cwspec, vec, w, vec, w, vec, vec],
        out_shape=[jax.ShapeDtypeStruct((T, D_MODEL), F32), jax.ShapeDtypeStruct((T, D_MODEL), F32),
                   jax.ShapeDtypeStruct((CONV_WIDTH, D_MODEL), F32), vshape, wshape, vshape, wshape, vshape, vshape],
        scratch_shapes=[pltpu.VMEM((tc + 8, C), F32), pltpu.VMEM((tc + 8, C), F32),
                        pltpu.VMEM((tc + 8, C), F32), pltpu.VMEM((tc + 8, C), F32)],
        compiler_params=pltpu.CompilerParams(dimension_semantics=("parallel",)),
    )(P, P, hs, dy, cw, cb, wrg, brg, wig, big, lam)


def _rope_table(T):
    half = ROT_DIM // 2
    pos = jnp.arange(T, dtype=F32)
    inv_freq = ROPE_THETA ** (-jnp.arange(0, ROT_DIM, 2, dtype=F32) / ROT_DIM)
    ang = pos[:, None] * inv_freq[None, :]
    cos, sin = jnp.cos(ang), jnp.sin(ang)
    one = jnp.ones((T, HEAD_DIM - ROT_DIM), F32)
    zero = jnp.zeros((T, HEAD_DIM - ROT_DIM), F32)
    z8 = jnp.zeros((T, half), F32)
    c = jnp.concatenate([cos, cos, one], axis=1)
    a = jnp.concatenate([-sin, z8, zero], axis=1)
    b = jnp.concatenate([z8, sin, zero], axis=1)
    return jnp.stack([jnp.tile(c, (1, 2)), jnp.tile(a, (1, 2)), jnp.tile(b, (1, 2))])


def _rope(x, tab, sign):
    W = x.shape[1]
    rep = W // 128
    c = jnp.tile(tab[0], (1, rep)) if rep > 1 else tab[0]
    a = jnp.tile(tab[1], (1, rep)) if rep > 1 else tab[1]
    b = jnp.tile(tab[2], (1, rep)) if rep > 1 else tab[2]
    return x * c + sign * (pltpu.roll(x, W - ROT_DIM // 2, 1) * a + pltpu.roll(x, ROT_DIM // 2, 1) * b)


def _swa_mask(n):
    rows = lax.broadcasted_iota(jnp.int32, (GROUP * BLOCK, 2 * BLOCK), 0) & (BLOCK - 1)
    cols = lax.broadcasted_iota(jnp.int32, (GROUP * BLOCK, 2 * BLOCK), 1)
    return (cols > rows) & (cols <= rows + BLOCK) & ((n > 0) | (cols >= BLOCK))


def _swa_probs(qg, k2, sink, valid):
    s = lax.dot_general(qg, k2, _DIMS["nt"], preferred_element_type=F32) * (HEAD_DIM ** -0.5)
    s = jnp.where(valid, s, NEG_INF)
    m = jnp.maximum(jnp.max(s, axis=1, keepdims=True), sink)
    p = jnp.exp(s - m)
    ps = jnp.exp(sink - m)
    inv = 1.0 / (jnp.sum(p, axis=1, keepdims=True) + ps)
    return p * inv, ps * inv


def _swa_specs(T):
    nb = T // BLOCK
    qspec = pl.BlockSpec((BLOCK, D_MODEL), lambda n: (n, C_Q // D_MODEL))
    cur = lambda off: pl.BlockSpec((BLOCK, KV_WIDTH), lambda n, off=off: (n, off // KV_WIDTH))
    prev = lambda off: pl.BlockSpec((BLOCK, KV_WIDTH), lambda n, off=off: (jnp.maximum(n - 1, 0), off // KV_WIDTH))
    tcur = pl.BlockSpec((3, BLOCK, 128), lambda n: (0, n, 0))
    tprev = pl.BlockSpec((3, BLOCK, 128), lambda n: (0, jnp.maximum(n - 1, 0), 0))
    sink = pl.BlockSpec((N_KV_HEADS, GROUP * BLOCK, 1), lambda n: (0, 0, 0))
    own = pl.BlockSpec((BLOCK, D_MODEL), lambda n: (n, 0))
    return nb, qspec, cur, prev, tcur, tprev, sink, own


def _stack_heads(x, hk):
    return jnp.concatenate([x[:, (hk * GROUP + g) * HEAD_DIM:(hk * GROUP + g + 1) * HEAD_DIM] for g in range(GROUP)],
                           axis=0)


def _swa_fwd(P, tab, sink_col, name):
    T = P.shape[0]
    nb, qspec, cur, prev, tcur, tprev, sink, own = _swa_specs(T)

    def body(q_ref, kc_ref, kp_ref, vc_ref, vp_ref, tc_ref, tp_ref, sk_ref, o_ref):
        n = pl.program_id(0)
        valid = _swa_mask(n)
        q = _rope(q_ref[...], tc_ref[...], 1.0).astype(BF16)
        k2 = jnp.concatenate([_rope(kp_ref[...], tp_ref[...], 1.0), _rope(kc_ref[...], tc_ref[...], 1.0)],
                             axis=0).astype(BF16)
        v2 = jnp.concatenate([vp_ref[...], vc_ref[...]], axis=0).astype(BF16)
        for hk in range(N_KV_HEADS):
            sl = slice(hk * HEAD_DIM, (hk + 1) * HEAD_DIM)
            pn, _ = _swa_probs(_stack_heads(q, hk), k2[:, sl], sk_ref[hk], valid)
            og = jnp.dot(pn.astype(BF16), v2[:, sl], preferred_element_type=F32)
            for g in range(GROUP):
                h = hk * GROUP + g
                o_ref[:, h * HEAD_DIM:(h + 1) * HEAD_DIM] = og[g * BLOCK:(g + 1) * BLOCK, :]

    return pl.pallas_call(
        body, name=name, grid=(nb,),
        in_specs=[qspec, cur(C_K), prev(C_K), cur(C_V), prev(C_V), tcur, tprev, sink],
        out_specs=own, out_shape=jax.ShapeDtypeStruct((T, D_MODEL), F32),
        compiler_params=pltpu.CompilerParams(dimension_semantics=("parallel",)),
    )(P, P, P, P, P, tab, tab, sink_col)


def _swa_bwd(P, do, tab, sink_col, name):
    T = P.shape[0]
    nb, qspec, cur, prev, tcur, tprev, sink, own = _swa_specs(T)

    def body(q_ref, kc_ref, kp_ref, vc_ref, vp_ref, do_ref, tc_ref, tp_ref, sk_ref,
             dq_ref, dk_ref, dv_ref, ds_ref):
        n = pl.program_id(0)

        @pl.when(n == 0)
        def _():
            dk_ref[...] = jnp.zeros_like(dk_ref)
            dv_ref[...] = jnp.zeros_like(dv_ref)
            ds_ref[...] = jnp.zeros_like(ds_ref)

        valid = _swa_mask(n)
        tcur_v = tc_ref[...]
        tprev_v = tp_ref[...]
        q = _rope(q_ref[...], tcur_v, 1.0).astype(BF16)
        k2 = jnp.concatenate([_rope(kp_ref[...], tprev_v, 1.0), _rope(kc_ref[...], tcur_v, 1.0)], axis=0).astype(BF16)
        v2 = jnp.concatenate([vp_ref[...], vc_ref[...]], axis=0).astype(BF16)
        dob = do_ref[...].astype(BF16)
        dq_parts = []
        dk_parts = []
        dv_parts = []
        for hk in range(N_KV_HEADS):
            sl = slice(hk * HEAD_DIM, (hk + 1) * HEAD_DIM)
            qg = _stack_heads(q, hk)
            dog = _stack_heads(dob, hk)
            pn, psn = _swa_probs(qg, k2[:, sl], sk_ref[hk], valid)
            dp = lax.dot_general(dog, v2[:, sl], _DIMS["nt"], preferred_element_type=F32)
            delta = jnp.sum(pn * dp, axis=1, keepdims=True)
            dsc = (pn * (dp - delta) * (HEAD_DIM ** -0.5)).astype(BF16)
            dsink = -psn * delta
            for g in range(GROUP):
                ds_ref[hk, g:g + 1, :] += jnp.broadcast_to(
                    jnp.sum(dsink[g * BLOCK:(g + 1) * BLOCK], axis=0, keepdims=True), (1, 128))
            dqg = jnp.dot(dsc, k2[:, sl], preferred_element_type=F32)
            dq_parts += [dqg[g * BLOCK:(g + 1) * BLOCK, :] for g in range(GROUP)]
            dk_parts.append(lax.dot_general(dsc, qg, _DIMS["tn"], preferred_element_type=F32))
            dv_parts.append(lax.dot_general(pn.astype(BF16), dog, _DIMS["tn"], preferred_element_type=F32))
        dq_ref[...] = _rope(jnp.concatenate(dq_parts, axis=1), tcur_v, -1.0)
        dk2 = jnp.concatenate(dk_parts, axis=1)
        dv2 = jnp.concatenate(dv_parts, axis=1)
        c0 = pl.multiple_of(n * BLOCK, BLOCK)
        p0 = pl.multiple_of(jnp.maximum(n - 1, 0) * BLOCK, BLOCK)
        dk_ref[pl.ds(p0, BLOCK), :] += _rope(dk2[:BLOCK], tprev_v, -1.0)
        dv_ref[pl.ds(p0, BLOCK), :] += dv2[:BLOCK]
        dk_ref[pl.ds(c0, BLOCK), :] += _rope(dk2[BLOCK:], tcur_v, -1.0)
        dv_ref[pl.ds(c0, BLOCK), :] += dv2[BLOCK:]

    full = pl.BlockSpec((T, KV_WIDTH), lambda n: (0, 0))
    return pl.pallas_call(
        body, name=name, grid=(nb,),
        in_specs=[qspec, cur(C_K), prev(C_K), cur(C_V), prev(C_V), own, tcur, tprev, sink],
        out_specs=[own, full, full, pl.BlockSpec((N_KV_HEADS, GROUP, 128), lambda n: (0, 0, 0))],
        out_shape=[jax.ShapeDtypeStruct((T, D_MODEL), F32), jax.ShapeDtypeStruct((T, KV_WIDTH), F32),
                   jax.ShapeDtypeStruct((T, KV_WIDTH), F32), jax.ShapeDtypeStruct((N_KV_HEADS, GROUP, 128), F32)],
        compiler_params=pltpu.CompilerParams(dimension_semantics=("arbitrary",)),
    )(P, P, P, P, P, do, tab, tab, sink_col)


_MW = 256


def _gate_specs(T):
    tr = _pick(T, (ROW_TILE, 128, 64, 32, 16, 8))
    col = lambda off: pl.BlockSpec((tr, _MW), lambda i, j, off=off: (i, off // _MW + j))
    own = pl.BlockSpec((tr, _MW), lambda i, j: (i, j))
    return tr, col, own


def _merge_fwd(P, mr, ma, name):
    T = P.shape[0]
    tr, col, own = _gate_specs(T)

    def body(gr_ref, ga_ref, mr_ref, ma_ref, o_ref):
        o_ref[...] = _sigmoid(gr_ref[...]) * mr_ref[...] + _sigmoid(ga_ref[...]) * ma_ref[...]

    return pl.pallas_call(
        body, name=name, grid=(T // tr, D_MODEL // _MW), in_specs=[col(C_GRNN), col(C_GATTN), own, own],
        out_specs=own, out_shape=jax.ShapeDtypeStruct((T, D_MODEL), F32),
        compiler_params=pltpu.CompilerParams(dimension_semantics=("parallel", "parallel")),
    )(P, P, mr, ma)


def _merge_bwd(P, mr, ma, dm, name):
    T = P.shape[0]
    tr, col, own = _gate_specs(T)

    def body(gr_ref, ga_ref, mr_ref, ma_ref, dm_ref, dmr_ref, dma_ref, dgr_ref, dga_ref):
        dm = dm_ref[...]
        sr = _sigmoid(gr_ref[...])
        sa = _sigmoid(ga_ref[...])
        dmr_ref[...] = dm * sr
        dma_ref[...] = dm * sa
        dgr_ref[...] = dm * mr_ref[...] * sr * (1.0 - sr)
        dga_ref[...] = dm * ma_ref[...] * sa * (1.0 - sa)

    shp = jax.ShapeDtypeStruct((T, D_MODEL), F32)
    return pl.pallas_call(
        body, name=name, grid=(T // tr, D_MODEL // _MW), in_specs=[col(C_GRNN), col(C_GATTN), own, own, own],
        out_specs=[own] * 4, out_shape=[shp] * 4,
        compiler_params=pltpu.CompilerParams(dimension_semantics=("parallel", "parallel")),
    )(P, P, mr, ma, dm)


def _swiglu_fwd(U, name):
    T = U.shape[0]
    tr, col, own = _gate_specs(T)

    def body(g_ref, u_ref, o_ref):
        g = g_ref[...]
        o_ref[...] = g * _sigmoid(g) * u_ref[...]

    return pl.pallas_call(
        body, name=name, grid=(T // tr, D_FF // _MW), in_specs=[col(0), col(D_FF)],
        out_specs=own, out_shape=jax.ShapeDtypeStruct((T, D_FF), F32),
        compiler_params=pltpu.CompilerParams(dimension_semantics=("parallel", "parallel")),
    )(U, U)


def _swiglu_bwd(U, dact, name):
    T = U.shape[0]
    tr, col, own = _gate_specs(T)

    def body(g_ref, u_ref, da_ref, dg_ref, du_ref):
        g = g_ref[...]
        da = da_ref[...]
        s = _sigmoid(g)
        du_ref[...] = da * g * s
        dg_ref[...] = da * u_ref[...] * s * (1.0 + g * (1.0 - s))

    shp = jax.ShapeDtypeStruct((T, D_FF), F32)
    return pl.pallas_call(
        body, name=name, grid=(T // tr, D_FF // _MW), in_specs=[col(0), col(D_FF), own],
        out_specs=[own, own], out_shape=[shp, shp],
        compiler_params=pltpu.CompilerParams(dimension_semantics=("parallel", "parallel")),
    )(U, U, dact)


def _cross_probs(qh, kh):
    s = lax.dot_general(qh, kh, _DIMS["nt"], preferred_element_type=F32) * (CROSS_HEAD_DIM ** -0.5)
    p = jnp.exp(s - jnp.max(s, axis=1, keepdims=True))
    return p / jnp.sum(p, axis=1, keepdims=True)


def _cross_fwd(q, kv, name):
    T = q.shape[0]
    M = kv.shape[0]
    tr = _pick(T, (ROW_TILE, 128, 64, 32, 16, 8))
    W = CROSS_HEAD_DIM

    def body(q_ref, kv_ref, o_ref):
        for h in range(CROSS_HEADS):
            qh = q_ref[:, h * W:(h + 1) * W].astype(BF16)
            kh = kv_ref[:, h * W:(h + 1) * W].astype(BF16)
            vh = kv_ref[:, D_MODEL + h * W:D_MODEL + (h + 1) * W].astype(BF16)
            pn = _cross_probs(qh, kh)
            o_ref[:, h * W:(h + 1) * W] = jnp.dot(pn.astype(BF16), vh, preferred_element_type=F32)

    row = pl.BlockSpec((tr, D_MODEL), lambda i: (i, 0))
    return pl.pallas_call(
        body, name=name, grid=(T // tr,), in_specs=[row, pl.BlockSpec((M, 2 * D_MODEL), lambda i: (0, 0))],
        out_specs=row, out_shape=jax.ShapeDtypeStruct((T, D_MODEL), F32),
        compiler_params=pltpu.CompilerParams(dimension_semantics=("parallel",)),
    )(q, kv)


def _cross_bwd(q, kv, do, name):
    T = q.shape[0]
    M = kv.shape[0]
    tr = _pick(T, (ROW_TILE, 128, 64, 32, 16, 8))
    W = CROSS_HEAD_DIM

    def body(q_ref, kv_ref, do_ref, dq_ref, dkv_ref):
        @pl.when(pl.program_id(0) == 0)
        def _():
            dkv_ref[...] = jnp.zeros_like(dkv_ref)

        for h in range(CROSS_HEADS):
            qh = q_ref[:, h * W:(h + 1) * W].astype(BF16)
            kh = kv_ref[:, h * W:(h + 1) * W].astype(BF16)
            vh = kv_ref[:, D_MODEL + h * W:D_MODEL + (h + 1) * W].astype(BF16)
            doh = do_ref[:, h * W:(h + 1) * W].astype(BF16)
            pn = _cross_probs(qh, kh)
            dp = lax.dot_general(doh, vh, _DIMS["nt"], preferred_element_type=F32)
            delta = jnp.sum(pn * dp, axis=1, keepdims=True)
            dsc = (pn * (dp - delta) * (W ** -0.5)).astype(BF16)
            dq_ref[:, h * W:(h + 1) * W] = jnp.dot(dsc, kh, preferred_element_type=F32)
            dkv_ref[:, h * W:(h + 1) * W] += lax.dot_general(dsc, qh, _DIMS["tn"], preferred_element_type=F32)
            dkv_ref[:, D_MODEL + h * W:D_MODEL + (h + 1) * W] += lax.dot_general(
                pn.astype(BF16), doh, _DIMS["tn"], preferred_element_type=F32)

    row = pl.BlockSpec((tr, D_MODEL), lambda i: (i, 0))
    full = pl.BlockSpec((M, 2 * D_MODEL), lambda i: (0, 0))
    return pl.pallas_call(
        body, name=name, grid=(T // tr,), in_specs=[row, full, row], out_specs=[row, full],
        out_shape=[jax.ShapeDtypeStruct((T, D_MODEL), F32), jax.ShapeDtypeStruct((M, 2 * D_MODEL), F32)],
        compiler_params=pltpu.CompilerParams(dimension_semantics=("arbitrary",)),
    )(q, kv, do)


def _loss_head(y, target, name):
    T, D = y.shape
    tr = _pick(T, (ROW_TILE, 128, 64, 32, 16, 8))

    def body(y_ref, t_ref, l_ref, dy_ref):
        @pl.when(pl.program_id(0) == 0)
        def _():
            l_ref[...] = jnp.zeros_like(l_ref)

        err = y_ref[...] - t_ref[...]
        dy_ref[...] = err * (1.0 / D)
        l_ref[...] += jnp.broadcast_to(0.5 * jnp.sum(jnp.mean(err * err, axis=-1, keepdims=True), axis=0, keepdims=True),
                                       (8, 128))

    row = pl.BlockSpec((tr, D), lambda i: (i, 0))
    return pl.pallas_call(
        body, name=name, grid=(T // tr,), in_specs=[row, row],
        out_specs=[pl.BlockSpec((8, 128), lambda i: (0, 0)), row],
        out_shape=[jax.ShapeDtypeStruct((8, 128), F32), jax.ShapeDtypeStruct((T, D), F32)],
        compiler_params=pltpu.CompilerParams(dimension_semantics=("arbitrary",)),
    )(y, target)


def _sum_slots(recv, name):
    _, R, C = recv.shape
    tr = _pick(R, (ROW_TILE, 128, 64, 32, 16, 8))

    def body(r_ref, o_ref):
        acc = r_ref[0].astype(F32)
        for d in range(1, N_DEV):
            acc = acc + r_ref[d].astype(F32)
        o_ref[...] = acc

    return pl.pallas_call(
        body, name=name, grid=(R // tr,), in_specs=[pl.BlockSpec((N_DEV, tr, C), lambda i: (0, i, 0))],
        out_specs=pl.BlockSpec((tr, C), lambda i: (i, 0)), out_shape=jax.ShapeDtypeStruct((R, C), F32),
        compiler_params=pltpu.CompilerParams(dimension_semantics=("parallel",)),
    )(recv)


def _adamw(w, g, m, v, name):
    shape = w.shape
    C = shape[-1]
    R = math.prod(shape[:-1])
    w2, g2, m2, v2 = (t.reshape(R, C) for t in (w, g, m, v))
    tr = _pick(R, (ROW_TILE, 128, 64, 32, 16, 8))

    def body(w_ref, g_ref, m_ref, v_ref, d_ref, mo_ref, vo_ref):
        gg = g_ref[...]
        mn = ADAM_B1 * m_ref[...] + (1.0 - ADAM_B1) * gg
        vn = ADAM_B2 * v_ref[...] + (1.0 - ADAM_B2) * (gg * gg)
        m_hat = mn / (1.0 - ADAM_B1 ** ADAM_STEP)
        v_hat = vn / (1.0 - ADAM_B2 ** ADAM_STEP)
        d_ref[...] = -ADAM_LR * (m_hat / (jnp.sqrt(v_hat) + ADAM_EPS) + ADAM_WD * w_ref[...])
        mo_ref[...] = mn
        vo_ref[...] = vn

    blk = pl.BlockSpec((tr, C), lambda i: (i, 0))
    shp = jax.ShapeDtypeStruct((R, C), F32)
    d, mo, vo = pl.pallas_call(
        body, name=name, grid=(R // tr,), in_specs=[blk] * 4, out_specs=[blk] * 3, out_shape=[shp] * 3,
        compiler_params=pltpu.CompilerParams(dimension_semantics=("parallel",)),
    )(w2, g2, m2, v2)
    return d.reshape(shape), mo.reshape(shape), vo.reshape(shape)


def _all_gather(buf, name):
    R, C = buf.shape

    def body(x_ref, out_ref, send_sems, recv_sems, local_sem):
        x, y, c = lax.axis_index("x"), lax.axis_index("y"), lax.axis_index("c")
        me, sibling = (x, y, c), (x, y, 1 - c)
        chips = [(1 - x, y), (x, 1 - y), (1 - x, 1 - y)]

        def slot(px, py, pc):
            return out_ref.at[4 * px + 2 * py + pc]

        def copy(k, block, to, src=None):
            return pltpu.make_async_remote_copy(
                src_ref=slot(*block) if src is None else src, dst_ref=slot(*block),
                send_sem=send_sems.at[k], recv_sem=recv_sems.at[k],
                device_id=to, device_id_type=pl.DeviceIdType.MESH)

        mine = pltpu.make_async_copy(x_ref, slot(*me), local_sem)
        mine.start()
        first = [copy(0, me, sibling, src=x_ref)]
        first += [copy(1 + j, me, (*chip, c), src=x_ref) for j, chip in enumerate(chips)]
        for cp in first:
            cp.start()
        passed = [copy(4 + j, (*chip, c), sibling) for j, chip in enumerate(chips)]
        for j, chip in enumerate(chips):
            copy(1 + j, (*chip, c), me).wait_recv()
            passed[j].start()
        copy(0, sibling, me).wait_recv()
        for j, chip in enumerate(chips):
            copy(4 + j, (*chip, 1 - c), me).wait_recv()
        for cp in first + passed:
            cp.wait_send()
        mine.wait()

    return pl.pallas_call(
        body, name=name, out_shape=jax.ShapeDtypeStruct((N_DEV, R, C), buf.dtype),
        in_specs=[pl.BlockSpec(memory_space=pl.ANY)], out_specs=pl.BlockSpec(memory_space=pl.ANY),
        scratch_shapes=[pltpu.SemaphoreType.DMA((7,)), pltpu.SemaphoreType.DMA((7,)), pltpu.SemaphoreType.DMA],
    )(buf)


def _exchange(send, name):
    _, R, C = send.shape

    def body(s_ref, out_ref, send_sems, recv_sems, local_sem):
        x, y, c = lax.axis_index("x"), lax.axis_index("y"), lax.axis_index("c")
        my_slot = 4 * x + 2 * y + c
        mine = pltpu.make_async_copy(s_ref.at[my_slot], out_ref.at[my_slot], local_sem)
        mine.start()
        sends, recvs = [], []
        for k in range(1, N_DEV):
            px, py, pc = x ^ ((k >> 2) & 1), y ^ ((k >> 1) & 1), c ^ (k & 1)
            peer_slot = 4 * px + 2 * py + pc
            sends.append(pltpu.make_async_remote_copy(
                src_ref=s_ref.at[peer_slot], dst_ref=out_ref.at[my_slot],
                send_sem=send_sems.at[k - 1], recv_sem=recv_sems.at[k - 1],
                device_id=(px, py, pc), device_id_type=pl.DeviceIdType.MESH))
            recvs.append(pltpu.make_async_remote_copy(
                src_ref=s_ref.at[my_slot], dst_ref=out_ref.at[peer_slot],
                send_sem=send_sems.at[k - 1], recv_sem=recv_sems.at[k - 1],
                device_id=(px, py, pc), device_id_type=pl.DeviceIdType.MESH))
        for cp in sends:
            cp.start()
        for cp in recvs:
            cp.wait_recv()
        for cp in sends:
            cp.wait_send()
        mine.wait()

    return pl.pallas_call(
        body, name=name, out_shape=jax.ShapeDtypeStruct((N_DEV, R, C), send.dtype),
        in_specs=[pl.BlockSpec(memory_space=pl.ANY)], out_specs=pl.BlockSpec(memory_space=pl.ANY),
        scratch_shapes=[pltpu.SemaphoreType.DMA((7,)), pltpu.SemaphoreType.DMA((7,)), pltpu.SemaphoreType.DMA],
    )(send)


def _row(a, l):
    return a[l:l + 1]


def _layer_fwd(h, mem, W, l, tab):
    s = {}
    n = f"l{l}_"
    s["h0"] = h
    P = _matmul(h, W["w_in"][l], "nn", n + "proj")
    s["P"] = P
    sink_col = jnp.repeat(W["sinks"][l].reshape(N_KV_HEADS, GROUP), BLOCK, axis=1)[:, :, None]
    s["sink_col"] = sink_col
    y_rnn, hs = _rnn_fwd(P, W["conv_w"][l], _row(W["conv_b"], l), W["w_rg"][l], _row(W["b_rg"], l),
                         W["w_ig"][l], _row(W["b_ig"], l), _row(W["lru_lambda"], l), n + "rnn_fwd")
    y_attn = _swa_fwd(P, tab, sink_col, n + "swa_fwd")
    mr = _matmul(y_rnn, W["w_br_rnn"][l], "nn", n + "br_rnn")
    ma = _matmul(y_attn, W["w_br_attn"][l], "nn", n + "br_attn")
    merged = _merge_fwd(P, mr, ma, n + "merge_fwd")
    mix = _matmul(merged, W["w_out"][l], "nn", n + "w_out")
    h1, xh1, rs1 = _ln_fwd(h, mix, _row(W["ln1_g"], l), _row(W["ln1_b"], l), n + "ln1_fwd")
    s.update(hs=hs, y_rnn=y_rnn, y_attn=y_attn, mr=mr, ma=ma, merged=merged, xh1=xh1, rs1=rs1, h1=h1)

    qc = _matmul(h1, W["cq_w"][l], "nn", n + "cq")
    kv = _matmul(mem, W["ckv_w"][l], "nn", n + "ckv")
    oc = _cross_fwd(qc, kv, n + "cross_fwd")
    ca = _matmul(oc, W["co_w"][l], "nn", n + "co")
    h2, xh2, rs2 = _ln_fwd(h1, ca, _row(W["ln2_g"], l), _row(W["ln2_b"], l), n + "ln2_fwd")
    s.update(qc=qc, kv=kv, oc=oc, xh2=xh2, rs2=rs2, h2=h2)

    U = _matmul(h2, W["ffn_wi"][l], "nn", n + "ffn_wi")
    act = _swiglu_fwd(U, n + "swiglu_fwd")
    f = _matmul(act, W["ffn_wo"][l], "nn", n + "ffn_wo")
    h3, xh3, rs3 = _ln_fwd(h2, f, _row(W["ln3_g"], l), _row(W["ln3_b"], l), n + "ln3_fwd")
    s.update(U=U, act=act, xh3=xh3, rs3=rs3)
    return h3, s


def _layer_bwd(dh3, mem, W, l, tab, s):
    n = f"l{l}_"
    g = {}
    dz3, g["ln3_g"], g["ln3_b"] = _ln_bwd(dh3, s["xh3"], s["rs3"], _row(W["ln3_g"], l), n + "ln3_bwd")
    g["ffn_wo"] = _matmul(s["act"], dz3, "tn", n + "d_ffn_wo")
    dact = _matmul(dz3, W["ffn_wo"][l], "nt", n + "d_act")
    dgate, dup = _swiglu_bwd(s["U"], dact, n + "swiglu_bwd")
    dU = jnp.concatenate([dgate, dup], axis=1)
    g["ffn_wi"] = _matmul(s["h2"], dU, "tn", n + "d_ffn_wi")
    dh2 = _matmul(dU, W["ffn_wi"][l], "nt", n + "d_h2", add=dz3, add_scale=ALPHA)
    dz2, g["ln2_g"], g["ln2_b"] = _ln_bwd(dh2, s["xh2"], s["rs2"], _row(W["ln2_g"], l), n + "ln2_bwd")
    g["co_w"] = _matmul(s["oc"], dz2, "tn", n + "d_co")
    doc = _matmul(dz2, W["co_w"][l], "nt", n + "d_oc")
    dqc, dkv = _cross_bwd(s["qc"], s["kv"], doc, n + "cross_bwd")
    g["ckv_w"] = _matmul(mem, dkv, "tn", n + "d_ckv")
    g["cq_w"] = _matmul(s["h1"], dqc, "tn", n + "d_cq")
    dh1 = _matmul(dqc, W["cq_w"][l], "nt", n + "d_h1", add=dz2, add_scale=ALPHA)
    dz1, g["ln1_g"], g["ln1_b"] = _ln_bwd(dh1, s["xh1"], s["rs1"], _row(W["ln1_g"], l), n + "ln1_bwd")
    g["w_out"] = _matmul(s["merged"], dz1, "tn", n + "d_w_out")
    dmerged = _matmul(dz1, W["w_out"][l], "nt", n + "d_merged")
    dmr, dma, dgrnn, dgattn = _merge_bwd(s["P"], s["mr"], s["ma"], dmerged, n + "merge_bwd")
    g["w_br_rnn"] = _matmul(s["y_rnn"], dmr, "tn", n + "d_br_rnn")
    g["w_br_attn"] = _matmul(s["y_attn"], dma, "tn", n + "d_br_attn")
    dy_rnn = _matmul(dmr, W["w_br_rnn"][l], "nt", n + "d_y_rnn")
    dy_attn = _matmul(dma, W["w_br_attn"][l], "nt", n + "d_y_attn")
    dxr, dgr, g["conv_w"], g["conv_b"], g["w_rg"], g["b_rg"], g["w_ig"], g["b_ig"], g["lru_lambda"] = _rnn_bwd(
        s["P"], s["hs"], dy_rnn, W["conv_w"][l], _row(W["conv_b"], l), W["w_rg"][l], _row(W["b_rg"], l),
        W["w_ig"][l], _row(W["b_ig"], l), _row(W["lru_lambda"], l), n + "rnn_bwd")
    dq, dk, dv, dsk = _swa_bwd(s["P"], dy_attn, tab, s["sink_col"], n + "swa_bwd")
    g["sinks"] = dsk[:, :, 0].reshape(1, N_Q_HEADS)
    dP = jnp.concatenate([dxr, dgr, dq, dk, dv, dgrnn, dgattn], axis=1)
    g["w_in"] = _matmul(s["h0"], dP, "tn", n + "d_w_in")
    dh = _matmul(dP, W["w_in"][l], "nt", n + "d_h0", add=dz1, add_scale=ALPHA)
    return dh, g


def _local_step(x, mem, target, W, on_layer_grads):
    T = x.shape[0]
    tab = _rope_table(T)
    h = x
    saved = []
    for l in range(DEPTH):
        h, s = _layer_fwd(h, mem, W, l, tab)
        saved.append(s)
    lblk, dh = _loss_head(h, target, "loss_head")
    for l in reversed(range(DEPTH)):
        dh, g = _layer_bwd(dh, mem, W, l, tab, saved[l])
        on_layer_grads(l, g)
    return lblk[0, 0], dh


def _pack_shards(shards, l, dtype):
    return jnp.concatenate([shards[n][l].astype(dtype).reshape(r, D_MODEL) for n, r in PACK_ROWS], axis=0)


def _unpack_full(G, name):
    o, r = PACK_OFF[name]
    blk = G[:, o:o + r, :]
    if name in ("w_in", "ckv_w", "ffn_wi"):
        return jnp.transpose(blk.reshape(N_DEV, D_MODEL, r), (1, 0, 2)).reshape(D_MODEL, N_DEV * r)
    if name in ("w_rg", "w_ig"):
        return jnp.transpose(blk.reshape(N_DEV, RNN_BLOCKS, RNN_BLOCK // N_DEV, RNN_BLOCK), (1, 0, 2, 3)).reshape(
            RNN_BLOCKS, RNN_BLOCK, RNN_BLOCK)
    return blk.reshape(N_DEV * r, D_MODEL)


def _pack_blocks(g, dtype):
    parts = []
    for name, r in PACK_ROWS:
        a = g[name].astype(dtype)
        if name in ("w_in", "ckv_w", "ffn_wi"):
            per = a.shape[1] // N_DEV
            a = jnp.transpose(a.reshape(D_MODEL, N_DEV, per), (1, 0, 2))
        elif name in ("w_rg", "w_ig"):
            a = jnp.transpose(a.reshape(RNN_BLOCKS, N_DEV, RNN_BLOCK // N_DEV, RNN_BLOCK), (1, 0, 2, 3))
        parts.append(a.reshape(N_DEV, r, D_MODEL))
    return jnp.concatenate(parts, axis=1)


def _pack_small(g):
    rows = [g["conv_w"]]
    for nme in SMALL_NAMES:
        a = g[nme]
        if nme == "sinks":
            a = jnp.pad(a, ((0, 0), (0, D_MODEL - N_Q_HEADS)))
        rows.append(a)
    rows.append(jnp.zeros((SMALL_ROWS - CONV_WIDTH - len(SMALL_NAMES), D_MODEL), F32))
    return jnp.concatenate(rows, axis=0)


_SHARD_SHAPES = {"w_in": (1024, 672), "w_br_rnn": (128, 1024), "w_br_attn": (128, 1024), "w_out": (128, 1024),
                 "cq_w": (128, 1024), "ckv_w": (1024, 256), "co_w": (128, 1024), "ffn_wi": (1024, 704),
                 "ffn_wo": (352, 1024), "w_rg": (4, 32, 256), "w_ig": (4, 32, 256)}

WEIGHT_NAMES = ("w_in", "conv_w", "conv_b", "w_rg", "b_rg", "w_ig", "b_ig", "lru_lambda", "w_br_rnn", "w_br_attn",
                "sinks", "w_out", "ln1_g", "ln1_b", "cq_w", "ckv_w", "co_w", "ln2_g", "ln2_b", "ffn_wi", "ffn_wo",
                "ln3_g", "ln3_b")


def kernel(x, mem, w_in, conv_w, conv_b, w_rg, b_rg, w_ig, b_ig, lru_lambda, w_br_rnn, w_br_attn, sinks, w_out, ln1_g, ln1_b, cq_w, ckv_w, co_w, ln2_g, ln2_b, ffn_wi, ffn_wo, ln3_g, ln3_b, loss_target, m_w_in, m_conv_w, m_conv_b, m_w_rg, m_b_rg, m_w_ig, m_b_ig, m_lru_lambda, m_w_br_rnn, m_w_br_attn, m_sinks, m_w_out, m_ln1_g, m_ln1_b, m_cq_w, m_ckv_w, m_co_w, m_ln2_g, m_ln2_b, m_ffn_wi, m_ffn_wo, m_ln3_g, m_ln3_b, v_w_in, v_conv_w, v_conv_b, v_w_rg, v_b_rg, v_w_ig, v_b_ig, v_lru_lambda, v_w_br_rnn, v_w_br_attn, v_sinks, v_w_out, v_ln1_g, v_ln1_b, v_cq_w, v_ckv_w, v_co_w, v_ln2_g, v_ln2_b, v_ffn_wi, v_ffn_wo, v_ln3_g, v_ln3_b):
    w = dict(w_in=w_in, conv_w=conv_w, conv_b=conv_b, w_rg=w_rg, b_rg=b_rg, w_ig=w_ig, b_ig=b_ig,
             lru_lambda=lru_lambda, w_br_rnn=w_br_rnn, w_br_attn=w_br_attn, sinks=sinks, w_out=w_out, ln1_g=ln1_g,
             ln1_b=ln1_b, cq_w=cq_w, ckv_w=ckv_w, co_w=co_w, ln2_g=ln2_g, ln2_b=ln2_b, ffn_wi=ffn_wi, ffn_wo=ffn_wo,
             ln3_g=ln3_g, ln3_b=ln3_b)
    m = dict(w_in=m_w_in, conv_w=m_conv_w, conv_b=m_conv_b, w_rg=m_w_rg, b_rg=m_b_rg, w_ig=m_w_ig, b_ig=m_b_ig,
             lru_lambda=m_lru_lambda, w_br_rnn=m_w_br_rnn, w_br_attn=m_w_br_attn, sinks=m_sinks, w_out=m_w_out,
             ln1_g=m_ln1_g, ln1_b=m_ln1_b, cq_w=m_cq_w, ckv_w=m_ckv_w, co_w=m_co_w, ln2_g=m_ln2_g, ln2_b=m_ln2_b,
             ffn_wi=m_ffn_wi, ffn_wo=m_ffn_wo, ln3_g=m_ln3_g, ln3_b=m_ln3_b)
    v = dict(w_in=v_w_in, conv_w=v_conv_w, conv_b=v_conv_b, w_rg=v_w_rg, b_rg=v_b_rg, w_ig=v_w_ig, b_ig=v_b_ig,
             lru_lambda=v_lru_lambda, w_br_rnn=v_w_br_rnn, w_br_attn=v_w_br_attn, sinks=v_sinks, w_out=v_w_out,
             ln1_g=v_ln1_g, ln1_b=v_ln1_b, cq_w=v_cq_w, ckv_w=v_ckv_w, co_w=v_co_w, ln2_g=v_ln2_g, ln2_b=v_ln2_b,
             ffn_wi=v_ffn_wi, ffn_wo=v_ffn_wo, ln3_g=v_ln3_g, ln3_b=v_ln3_b)
    my_dev = 4 * lax.axis_index("x") + 2 * lax.axis_index("y") + lax.axis_index("c")

    packed = jnp.concatenate([_pack_shards(w, l, BF16) for l in range(DEPTH)], axis=0)
    gathered = _all_gather(packed, "gather_weights").reshape(N_DEV, DEPTH, LAYER_ROWS, D_MODEL)
    conv_all = _all_gather(conv_w.reshape(DEPTH * CONV_WIDTH, D_MODEL // N_DEV), "gather_conv")
    W = {n: [_unpack_full(gathered[:, l], n) for l in range(DEPTH)] for n, _ in PACK_ROWS}
    W["conv_w"] = jnp.transpose(conv_all, (1, 0, 2)).reshape(DEPTH, CONV_WIDTH, D_MODEL)
    for n in SMALL_NAMES:
        W[n] = w[n]

    summed = [None] * DEPTH
    small = [None] * DEPTH

    def on_layer_grads(l, g):
        recv = _exchange(_pack_blocks(g, F32), f"l{l}_exchange_grads")
        summed[l] = _sum_slots(recv, f"l{l}_sum_grads")
        small[l] = _pack_small(g)

    loss_local, dx = _local_step(x[0], mem[0], loss_target[0], W, on_layer_grads)
    loss = lax.psum(loss_local, MESH_AXES)

    small_all = _all_gather(jnp.concatenate(small, axis=0), "gather_small_grads")
    small_sum = _sum_slots(small_all, "sum_small_grads").reshape(DEPTH, SMALL_ROWS, D_MODEL)
    G = jnp.stack(summed)

    grads = {}
    for n, r in PACK_ROWS:
        o, _ = PACK_OFF[n]
        grads[n] = G[:, o:o + r, :].reshape((DEPTH,) + _SHARD_SHAPES[n])
    conv_full = small_sum[:, :CONV_WIDTH, :]
    grads["conv_w"] = lax.dynamic_slice_in_dim(conv_full, my_dev * (D_MODEL // N_DEV), D_MODEL // N_DEV, axis=2)
    for i, n in enumerate(SMALL_NAMES):
        row = small_sum[:, CONV_WIDTH + i, :]
        grads[n] = row[:, :N_Q_HEADS] if n == "sinks" else row

    deltas, new_m, new_v = {}, {}, {}
    for n in WEIGHT_NAMES:
        deltas[n], new_m[n], new_v[n] = _adamw(w[n], grads[n], m[n], v[n], "adamw_" + n)

    return (loss, dx[None], *[grads[n] for n in WEIGHT_NAMES], *[deltas[n] for n in WEIGHT_NAMES],
            *[new_m[n] for n in WEIGHT_NAMES], *[new_v[n] for n in WEIGHT_NAMES])
```

```python
import functools
import math

import jax
import jax.numpy as jnp
from jax import lax
from jax.experimental import pallas as pl
from jax.experimental.pallas import tpu as pltpu

F32 = jnp.float32
BF16 = jnp.bfloat16

D_MODEL = 1024
DEPTH = 4
N_DEV = 8
RNN_BLOCKS = 4
RNN_BLOCK = 256
CONV_WIDTH = 4
LRU_C = 8.0
HEAD_DIM = 64
N_Q_HEADS = 16
N_KV_HEADS = 2
GROUP = 8
KV_WIDTH = 128
BLOCK = 128
ROPE_THETA = 500000.0
ROT_DIM = 16
IN_COLS = 5376
CROSS_HEADS = 4
CROSS_HEAD_DIM = 256
D_FF = 2816
LN_EPS = 1e-5
ALPHA = (2 * DEPTH) ** 0.25
NEG_INF = -1e30

ADAM_LR = 0.001
ADAM_B1 = 0.9
ADAM_B2 = 0.999
ADAM_EPS = 1e-08
ADAM_WD = 0.01
ADAM_STEP = 10

C_XR, C_GR, C_Q, C_K, C_V, C_GRNN, C_GATTN = 0, 1024, 2048, 3072, 3200, 3328, 4352

TIME_CHUNK = 256
ROW_TILE = 256

MESH_AXES = ("x", "y", "c")

PACK_ROWS = (("w_in", 672), ("w_br_rnn", 128), ("w_br_attn", 128), ("w_out", 128), ("cq_w", 128),
             ("ckv_w", 256), ("co_w", 128), ("ffn_wi", 704), ("ffn_wo", 352), ("w_rg", 32), ("w_ig", 32))
PACK_OFF = {}
_o = 0
for _n, _r in PACK_ROWS:
    PACK_OFF[_n] = (_o, _r)
    _o += _r
LAYER_ROWS = _o

SMALL_NAMES = ("conv_b", "b_rg", "b_ig", "lru_lambda", "sinks", "ln1_g", "ln1_b", "ln2_g", "ln2_b", "ln3_g", "ln3_b")
SMALL_ROWS = 16


def _pick(dim, cands):
    for c in cands:
        if dim % c == 0:
            return c
    return dim


_DIMS = {"nn": (((1,), (0,)), ((), ())), "nt": (((1,), (1,)), ((), ())), "tn": (((0,), (0,)), ((), ()))}


def _matmul(a, b, mode, name, add=None, add_scale=1.0, out_dtype=F32):
    if mode == "nn":
        (M, K), (_, N) = a.shape, b.shape
    elif mode == "nt":
        (M, K), (N, _) = a.shape, b.shape
    else:
        (K, M), (_, N) = a.shape, b.shape
    tm = _pick(M, (768, 512, 1408, 256, 128, 64, 32, 16, 8))
    tn = _pick(N, (768, 512, 256, 128))
    tk = _pick(K, (1792, 1408, 1024, 768, 512, 256, 128))
    nk = K // tk
    dims = _DIMS[mode]

    def body(*refs):
        if add is None:
            a_ref, b_ref, o_ref, acc_ref = refs
        else:
            a_ref, b_ref, c_ref, o_ref, acc_ref = refs
        k = pl.program_id(2)

        @pl.when(k == 0)
        def _():
            acc_ref[...] = jnp.zeros_like(acc_ref)

        acc_ref[...] += lax.dot_general(a_ref[...].astype(BF16), b_ref[...].astype(BF16), dims,
                                        preferred_element_type=F32)

        @pl.when(k == nk - 1)
        def _():
            r = acc_ref[...]
            if add is not None:
                r = r + add_scale * c_ref[...]
            o_ref[...] = r.astype(out_dtype)

    if mode == "nn":
        a_spec = pl.BlockSpec((tm, tk), lambda i, j, k: (i, k))
        b_spec = pl.BlockSpec((tk, tn), lambda i, j, k: (k, j))
    elif mode == "nt":
        a_spec = pl.BlockSpec((tm, tk), lambda i, j, k: (i, k))
        b_spec = pl.BlockSpec((tn, tk), lambda i, j, k: (j, k))
    else:
        a_spec = pl.BlockSpec((tk, tm), lambda i, j, k: (k, i))
        b_spec = pl.BlockSpec((tk, tn), lambda i, j, k: (k, j))
    in_specs = [a_spec, b_spec]
    args = [a, b]
    if add is not None:
        in_specs.append(pl.BlockSpec((tm, tn), lambda i, j, k: (i, j)))
        args.append(add)
    return pl.pallas_call(
        body, name=name, grid=(M // tm, N // tn, nk),
        in_specs=in_specs, out_specs=pl.BlockSpec((tm, tn), lambda i, j, k: (i, j)),
        out_shape=jax.ShapeDtypeStruct((M, N), out_dtype),
        scratch_shapes=[pltpu.VMEM((tm, tn), F32)],
        compiler_params=pltpu.CompilerParams(dimension_semantics=("parallel", "parallel", "arbitrary")),
    )(*args)


def _ln_fwd(h, f, g, b, name):
    T, D = h.shape
    tr = _pick(T, (ROW_TILE, 128, 64, 32, 16, 8))

    def body(h_ref, f_ref, g_ref, b_ref, o_ref, xh_ref, rs_ref):
        z = ALPHA * h_ref[...] + f_ref[...]
        mu = jnp.mean(z, axis=-1, keepdims=True)
        zc = z - mu
        var = jnp.mean(zc * zc, axis=-1, keepdims=True)
        rstd = lax.rsqrt(var + LN_EPS)
        xh = zc * rstd
        xh_ref[...] = xh
        rs_ref[...] = rstd
        o_ref[...] = xh * g_ref[...] + b_ref[...]

    row = pl.BlockSpec((tr, D), lambda i: (i, 0))
    vec = pl.BlockSpec((1, D), lambda i: (0, 0))
    return pl.pallas_call(
        body, name=name, grid=(T // tr,), in_specs=[row, row, vec, vec],
        out_specs=[row, row, pl.BlockSpec((tr, 1), lambda i: (i, 0))],
        out_shape=[jax.ShapeDtypeStruct((T, D), F32), jax.ShapeDtypeStruct((T, D), F32),
                   jax.ShapeDtypeStruct((T, 1), F32)],
        compiler_params=pltpu.CompilerParams(dimension_semantics=("parallel",)),
    )(h, f, g, b)


def _ln_bwd(dout, xh, rstd, g, name):
    T, D = dout.shape
    tr = _pick(T, (ROW_TILE, 128, 64, 32, 16, 8))

    def body(do_ref, xh_ref, rs_ref, g_ref, dz_ref, dg_ref, db_ref):
        @pl.when(pl.program_id(0) == 0)
        def _():
            dg_ref[...] = jnp.zeros_like(dg_ref)
            db_ref[...] = jnp.zeros_like(db_ref)

        do = do_ref[...]
        xh = xh_ref[...]
        dxh = do * g_ref[...]
        m1 = jnp.mean(dxh, axis=-1, keepdims=True)
        m2 = jnp.mean(dxh * xh, axis=-1, keepdims=True)
        dz_ref[...] = rs_ref[...] * (dxh - m1 - xh * m2)
        dg_ref[...] += jnp.sum(do * xh, axis=0, keepdims=True)
        db_ref[...] += jnp.sum(do, axis=0, keepdims=True)

    row = pl.BlockSpec((tr, D), lambda i: (i, 0))
    vec = pl.BlockSpec((1, D), lambda i: (0, 0))
    return pl.pallas_call(
        body, name=name, grid=(T // tr,),
        in_specs=[row, row, pl.BlockSpec((tr, 1), lambda i: (i, 0)), vec],
        out_specs=[row, vec, vec],
        out_shape=[jax.ShapeDtypeStruct((T, D), F32), jax.ShapeDtypeStruct((1, D), F32),
                   jax.ShapeDtypeStruct((1, D), F32)],
        compiler_params=pltpu.CompilerParams(dimension_semantics=("arbitrary",)),
    )(dout, xh, rstd, g)


_GELU_C = math.sqrt(2.0 / math.pi)


def _gelu(x):
    t = jnp.tanh(_GELU_C * (x + 0.044715 * x * x * x))
    return 0.5 * x * (1.0 + t), t


def _gelu_grad(x, t):
    return 0.5 * (1.0 + t) + 0.5 * x * (1.0 - t * t) * _GELU_C * (1.0 + 3 * 0.044715 * x * x)


def _sigmoid(x):
    return 1.0 / (1.0 + jnp.exp(-x))


def _softplus_neg(lam):
    z = jnp.exp(-jnp.abs(lam))
    u = 1.0 + z
    l1p = jnp.where(u == 1.0, z, jnp.log(u) * z / jnp.where(u == 1.0, 1.0, u - 1.0))
    return jnp.maximum(-lam, 0.0) + l1p


def _neg_expm1(x):
    series = x * (1.0 + x * 0.5 * (1.0 + x * (1.0 / 3.0) * (1.0 + x * 0.25 * (1.0 + x * 0.2))))
    return -jnp.where(x > -0.05, series, jnp.exp(x) - 1.0)


def _scan_fwd(a, b):
    n = a.shape[0]
    rows = lax.broadcasted_iota(jnp.int32, a.shape, 0)
    s = 1
    while s < n:
        keep = rows >= s
        b = jnp.where(keep, a * pltpu.roll(b, s, 0) + b, b)
        a = jnp.where(keep, a * pltpu.roll(a, s, 0), a)
        s *= 2
    return a, b


def _scan_bwd(c, b):
    n = c.shape[0]
    rows = lax.broadcasted_iota(jnp.int32, c.shape, 0)
    s = 1
    while s < n:
        keep = rows < n - s
        b = jnp.where(keep, c * pltpu.roll(b, n - s, 0) + b, b)
        c = jnp.where(keep, c * pltpu.roll(c, n - s, 0), c)
        s *= 2
    return c, b


def _rnn_gates(xc, wr, br, wi, bi, sp):
    xb = xc.astype(BF16)
    r = _sigmoid(jnp.dot(xb, wr, preferred_element_type=F32) + br)
    i = _sigmoid(jnp.dot(xb, wi, preferred_element_type=F32) + bi)
    la = -LRU_C * r * sp
    a = jnp.exp(la)
    om = _neg_expm1(2.0 * la)
    mult = jnp.sqrt(om)
    return r, i, a, om, mult


def _rnn_specs(T):
    C = RNN_BLOCK
    col = lambda off: pl.BlockSpec((T, C), lambda n, off=off: (0, off // C + n))
    vec = pl.BlockSpec((1, C), lambda n: (0, n))
    cw = pl.BlockSpec((CONV_WIDTH, C), lambda n: (0, n))
    w = pl.BlockSpec((1, C, C), lambda n: (n, 0, 0))
    own = pl.BlockSpec((T, C), lambda n: (0, n))
    return col, vec, cw, w, own


def _rnn_fwd(P, cw, cb, wrg, brg, wig, big, lam, name):
    T = P.shape[0]
    C = RNN_BLOCK
    tc = _pick(T, (TIME_CHUNK,))
    nch = T // tc

    def body(x_ref, g_ref, cw_ref, cb_ref, wr_ref, br_ref, wi_ref, bi_ref, lam_ref, y_ref, hs_ref, xs_ref):
        sp = _softplus_neg(lam_ref[...])
        wr = wr_ref[0]
        wi = wi_ref[0]
        xs_ref[0:8, :] = jnp.zeros((8, C), F32)

        def chunk(c, hprev):
            r0 = pl.multiple_of(c * tc, tc)
            x = x_ref[pl.ds(r0, tc), :]
            xs_ref[8:, :] = x
            xc = cb_ref[...] + jnp.zeros((tc, C), F32)
            for k in range(CONV_WIDTH):
                xc = xc + xs_ref[pl.ds(8 - (CONV_WIDTH - 1 - k), tc), :] * cw_ref[k:k + 1, :]
            xs_ref[0:8, :] = x[tc - 8:, :]
            r, i, a, om, mult = _rnn_gates(xc, wr, br_ref[...], wi, bi_ref[...], sp)
            acum, bcum = _scan_fwd(a, mult * (i * xc))
            h = acum * hprev + bcum
            hs_ref[pl.ds(r0, tc), :] = h
            ge, _ = _gelu(g_ref[pl.ds(r0, tc), :])
            y_ref[pl.ds(r0, tc), :] = h * ge
            return h[tc - 1:tc, :]

        lax.fori_loop(0, nch, chunk, jnp.zeros((1, C), F32))

    col, vec, cwspec, w, own = _rnn_specs(T)
    return pl.pallas_call(
        body, name=name, grid=(RNN_BLOCKS,),
        in_specs=[col(C_XR), col(C_GR), cwspec, vec, w, vec, w, vec, vec],
        out_specs=[own, own],
        out_shape=[jax.ShapeDtypeStruct((T, D_MODEL), F32), jax.ShapeDtypeStruct((T, D_MODEL), F32)],
        scratch_shapes=[pltpu.VMEM((tc + 8, C), F32)],
        compiler_params=pltpu.CompilerParams(dimension_semantics=("parallel",)),
    )(P, P, cw, cb, wrg, brg, wig, big, lam)


def _rnn_bwd(P, hs, dy, cw, cb, wrg, brg, wig, big, lam, name):
    T = P.shape[0]
    C = RNN_BLOCK
    tc = _pick(T, (TIME_CHUNK,))
    nch = T // tc

    def body(x_ref, g_ref, hs_ref, dy_ref, cw_ref, cb_ref, wr_ref, br_ref, wi_ref, bi_ref, lam_ref,
             dx_ref, dg_ref, dcw_ref, dcb_ref, dwr_ref, dbr_ref, dwi_ref, dbi_ref, dlam_ref,
             xs_ref, hp_ref, an_ref, dn_ref):
        lam_v = lam_ref[...]
        sp = _softplus_neg(lam_v)
        wr = wr_ref[0]
        wi = wi_ref[0]
        dcw_ref[...] = jnp.zeros_like(dcw_ref)
        dcb_ref[...] = jnp.zeros_like(dcb_ref)
        dwr_ref[...] = jnp.zeros_like(dwr_ref)
        dbr_ref[...] = jnp.zeros_like(dbr_ref)
        dwi_ref[...] = jnp.zeros_like(dwi_ref)
        dbi_ref[...] = jnp.zeros_like(dbi_ref)
        dlam_ref[...] = jnp.zeros_like(dlam_ref)
        an_ref[tc:, :] = jnp.zeros((8, C), F32)
        dn_ref[tc:, :] = jnp.zeros((8, C), F32)

        def chunk(step, gnext):
            c = nch - 1 - step
            r0 = pl.multiple_of(c * tc, tc)
            p0 = pl.multiple_of(jnp.maximum(r0 - 8, 0), 8)
            live = c > 0
            x = x_ref[pl.ds(r0, tc), :]
            xs_ref[0:8, :] = jnp.where(live, x_ref[pl.ds(p0, 8), :], 0.0)
            xs_ref[8:, :] = x
            xsh = [xs_ref[pl.ds(8 - (CONV_WIDTH - 1 - k), tc), :] for k in range(CONV_WIDTH)]
            xc = cb_ref[...] + jnp.zeros((tc, C), F32)
            for k in range(CONV_WIDTH):
                xc = xc + xsh[k] * cw_ref[k:k + 1, :]
            r, i, a, om, mult = _rnn_gates(xc, wr, br_ref[...], wi, bi_ref[...], sp)
            h = hs_ref[pl.ds(r0, tc), :]
            hp_ref[0:8, :] = jnp.where(live, hs_ref[pl.ds(p0, 8), :], 0.0)
            hp_ref[8:, :] = h
            hm1 = hp_ref[pl.ds(7, tc), :]
            g = g_ref[pl.ds(r0, tc), :]
            ge, th = _gelu(g)
            dy = dy_ref[pl.ds(r0, tc), :]
            dg_ref[pl.ds(r0, tc), :] = dy * h * _gelu_grad(g, th)
            an_ref[0:tc, :] = a
            coef = an_ref[pl.ds(1, tc), :]
            ccum, bcum = _scan_bwd(coef, dy * ge)
            G = bcum + ccum * gnext
            an_ref[tc:, :] = a[0:8, :]
            da = G * hm1
            ixc = i * xc
            dmult = G * ixc
            di = G * mult * xc
            dxc = G * mult * i
            dla = da * a - dmult * (1.0 - om) / mult
            dr = dla * (-LRU_C * sp)
            dlam_ref[...] += jnp.sum(dla * r, axis=0, keepdims=True)
            dzr = dr * r * (1.0 - r)
            dzi = di * i * (1.0 - i)
            dbr_ref[...] += jnp.sum(dzr, axis=0, keepdims=True)
            dbi_ref[...] += jnp.sum(dzi, axis=0, keepdims=True)
            xb = xc.astype(BF16)
            dzrb = dzr.astype(BF16)
            dzib = dzi.astype(BF16)
            dwr_ref[0] += lax.dot_general(xb, dzrb, _DIMS["tn"], preferred_element_type=F32)
            dwi_ref[0] += lax.dot_general(xb, dzib, _DIMS["tn"], preferred_element_type=F32)
            dxc = dxc + lax.dot_general(dzrb, wr, _DIMS["nt"], preferred_element_type=F32)
            dxc = dxc + lax.dot_general(dzib, wi, _DIMS["nt"], preferred_element_type=F32)
            dcb_ref[...] += jnp.sum(dxc, axis=0, keepdims=True)
            for k in range(CONV_WIDTH):
                dcw_ref[k:k + 1, :] += jnp.sum(dxc * xsh[k], axis=0, keepdims=True)
            dn_ref[0:tc, :] = dxc
            dx = jnp.zeros((tc, C), F32)
            for k in range(CONV_WIDTH):
                dx = dx + dn_ref[pl.ds(CONV_WIDTH - 1 - k, tc), :] * cw_ref[k:k + 1, :]
            dn_ref[tc:, :] = dxc[0:8, :]
            dx_ref[pl.ds(r0, tc), :] = dx
            return G[0:1, :]

        lax.fori_loop(0, nch, chunk, jnp.zeros((1, C), F32))
        dlam_ref[...] = dlam_ref[...] * (LRU_C * _sigmoid(-lam_v))

    col, vec, cwspec, w, own = _rnn_specs(T)
    vshape = jax.ShapeDtypeStruct((1, D_MODEL), F32)
    wshape = jax.ShapeDtypeStruct((RNN_BLOCKS, C, C), F32)
    return pl.pallas_call(
        body, name=name, grid=(RNN_BLOCKS,),
        in_specs=[col(C_XR), col(C_GR), own, own, cwspec, vec, w, vec, w, vec, vec],
        out_specs=[own, own, cwspec, vec, w, vec, w, vec, vec],
        out_shape=[jax.ShapeDtypeStruct((T, D_MODEL), F32), jax.ShapeDtypeStruct((T, D_MODEL), F32),
                   jax.ShapeDtypeStruct((CONV_WIDTH, D_MODEL), F32), vshape, wshape, vshape, wshape, vshape, vshape],
        scratch_shapes=[pltpu.VMEM((tc + 8, C), F32), pltpu.VMEM((tc + 8, C), F32),
                        pltpu.VMEM((tc + 8, C), F32), pltpu.VMEM((tc + 8, C), F32)],
        compiler_params=pltpu.CompilerParams(dimension_semantics=("parallel",)),
    )(P, P, hs, dy, cw, cb, wrg, brg, wig, big, lam)


def _rope_table(T):
    half = ROT_DIM // 2
    pos = jnp.arange(T, dtype=F32)
    inv_freq = ROPE_THETA ** (-jnp.arange(0, ROT_DIM, 2, dtype=F32) / ROT_DIM)
    ang = pos[:, None] * inv_freq[None, :]
    cos, sin = jnp.cos(ang), jnp.sin(ang)
    one = jnp.ones((T, HEAD_DIM - ROT_DIM), F32)
    zero = jnp.zeros((T, HEAD_DIM - ROT_DIM), F32)
    z8 = jnp.zeros((T, half), F32)
    c = jnp.concatenate([cos, cos, one], axis=1)
    a = jnp.concatenate([-sin, z8, zero], axis=1)
    b = jnp.concatenate([z8, sin, zero], axis=1)
    return jnp.stack([jnp.tile(c, (1, 2)), jnp.tile(a, (1, 2)), jnp.tile(b, (1, 2))])


def _rope(x, tab, sign):
    W = x.shape[1]
    rep = W // 128
    c = jnp.tile(tab[0], (1, rep)) if rep > 1 else tab[0]
    a = jnp.tile(tab[1], (1, rep)) if rep > 1 else tab[1]
    b = jnp.tile(tab[2], (1, rep)) if rep > 1 else tab[2]
    return x * c + sign * (pltpu.roll(x, W - ROT_DIM // 2, 1) * a + pltpu.roll(x, ROT_DIM // 2, 1) * b)


def _swa_mask(n):
    rows = lax.broadcasted_iota(jnp.int32, (GROUP * BLOCK, 2 * BLOCK), 0) & (BLOCK - 1)
    cols = lax.broadcasted_iota(jnp.int32, (GROUP * BLOCK, 2 * BLOCK), 1)
    return (cols > rows) & (cols <= rows + BLOCK) & ((n > 0) | (cols >= BLOCK))


def _swa_probs(qg, k2, sink, valid):
    s = lax.dot_general(qg, k2, _DIMS["nt"], preferred_element_type=F32) * (HEAD_DIM ** -0.5)
    s = jnp.where(valid, s, NEG_INF)
    m = jnp.maximum(jnp.max(s, axis=1, keepdims=True), sink)
    p = jnp.exp(s - m)
    ps = jnp.exp(sink - m)
    inv = 1.0 / (jnp.sum(p, axis=1, keepdims=True) + ps)
    return p * inv, ps * inv


def _swa_specs(T):
    nb = T // BLOCK
    qspec = pl.BlockSpec((BLOCK, D_MODEL), lambda n: (n, C_Q // D_MODEL))
    cur = lambda off: pl.BlockSpec((BLOCK, KV_WIDTH), lambda n, off=off: (n, off // KV_WIDTH))
    prev = lambda off: pl.BlockSpec((BLOCK, KV_WIDTH), lambda n, off=off: (jnp.maximum(n - 1, 0), off // KV_WIDTH))
    tcur = pl.BlockSpec((3, BLOCK, 128), lambda n: (0, n, 0))
    tprev = pl.BlockSpec((3, BLOCK, 128), lambda n: (0, jnp.maximum(n - 1, 0), 0))
    sink = pl.BlockSpec((N_KV_HEADS, GROUP * BLOCK, 1), lambda n: (0, 0, 0))
    own = pl.BlockSpec((BLOCK, D_MODEL), lambda n: (n, 0))
    return nb, qspec, cur, prev, tcur, tprev, sink, own


def _stack_heads(x, hk):
    return jnp.concatenate([x[:, (hk * GROUP + g) * HEAD_DIM:(hk * GROUP + g + 1) * HEAD_DIM] for g in range(GROUP)],
                           axis=0)


def _swa_fwd(P, tab, sink_col, name):
    T = P.shape[0]
    nb, qspec, cur, prev, tcur, tprev, sink, own = _swa_specs(T)

    def body(q_ref, kc_ref, kp_ref, vc_ref, vp_ref, tc_ref, tp_ref, sk_ref, o_ref):
        n = pl.program_id(0)
        valid = _swa_mask(n)
        q = _rope(q_ref[...], tc_ref[...], 1.0).astype(BF16)
        k2 = jnp.concatenate([_rope(kp_ref[...], tp_ref[...], 1.0), _rope(kc_ref[...], tc_ref[...], 1.0)],
                             axis=0).astype(BF16)
        v2 = jnp.concatenate([vp_ref[...], vc_ref[...]], axis=0).astype(BF16)
        for hk in range(N_KV_HEADS):
            sl = slice(hk * HEAD_DIM, (hk + 1) * HEAD_DIM)
            pn, _ = _swa_probs(_stack_heads(q, hk), k2[:, sl], sk_ref[hk], valid)
            og = jnp.dot(pn.astype(BF16), v2[:, sl], preferred_element_type=F32)
            for g in range(GROUP):
                h = hk * GROUP + g
                o_ref[:, h * HEAD_DIM:(h + 1) * HEAD_DIM] = og[g * BLOCK:(g + 1) * BLOCK, :]

    return pl.pallas_call(
        body, name=name, grid=(nb,),
        in_specs=[qspec, cur(C_K), prev(C_K), cur(C_V), prev(C_V), tcur, tprev, sink],
        out_specs=own, out_shape=jax.ShapeDtypeStruct((T, D_MODEL), F32),
        compiler_params=pltpu.CompilerParams(dimension_semantics=("parallel",)),
    )(P, P, P, P, P, tab, tab, sink_col)


def _swa_bwd(P, do, tab, sink_col, name):
    T = P.shape[0]
    nb, qspec, cur, prev, tcur, tprev, sink, own = _swa_specs(T)

    def body(q_ref, kc_ref, kp_ref, vc_ref, vp_ref, do_ref, tc_ref, tp_ref, sk_ref,
             dq_ref, dk_ref, dv_ref, ds_ref):
        n = pl.program_id(0)

        @pl.when(n == 0)
        def _():
            dk_ref[...] = jnp.zeros_like(dk_ref)
            dv_ref[...] = jnp.zeros_like(dv_ref)
            ds_ref[...] = jnp.zeros_like(ds_ref)

        valid = _swa_mask(n)
        tcur_v = tc_ref[...]
        tprev_v = tp_ref[...]
        q = _rope(q_ref[...], tcur_v, 1.0).astype(BF16)
        k2 = jnp.concatenate([_rope(kp_ref[...], tprev_v, 1.0), _rope(kc_ref[...], tcur_v, 1.0)], axis=0).astype(BF16)
        v2 = jnp.concatenate([vp_ref[...], vc_ref[...]], axis=0).astype(BF16)
        dob = do_ref[...].astype(BF16)
        dq_parts = []
        dk_parts = []
        dv_parts = []
        for hk in range(N_KV_HEADS):
            sl = slice(hk * HEAD_DIM, (hk + 1) * HEAD_DIM)
            qg = _stack_heads(q, hk)
            dog = _stack_heads(dob, hk)
            pn, psn = _swa_probs(qg, k2[:, sl], sk_ref[hk], valid)
            dp = lax.dot_general(dog, v2[:, sl], _DIMS["nt"], preferred_element_type=F32)
            delta = jnp.sum(pn * dp, axis=1, keepdims=True)
            dsc = (pn * (dp - delta) * (HEAD_DIM ** -0.5)).astype(BF16)
            dsink = -psn * delta
            for g in range(GROUP):
                ds_ref[hk, g:g + 1, :] += jnp.broadcast_to(
                    jnp.sum(dsink[g * BLOCK:(g + 1) * BLOCK], axis=0, keepdims=True), (1, 128))
            dqg = jnp.dot(dsc, k2[:, sl], preferred_element_type=F32)
            dq_parts += [dqg[g * BLOCK:(g + 1) * BLOCK, :] for g in range(GROUP)]
            dk_parts.append(lax.dot_general(dsc, qg, _DIMS["tn"], preferred_element_type=F32))
            dv_parts.append(lax.dot_general(pn.astype(BF16), dog, _DIMS["tn"], preferred_element_type=F32))
        dq_ref[...] = _rope(jnp.concatenate(dq_parts, axis=1), tcur_v, -1.0)
        dk2 = jnp.concatenate(dk_parts, axis=1)
        dv2 = jnp.concatenate(dv_parts, axis=1)
        c0 = pl.multiple_of(n * BLOCK, BLOCK)
        p0 = pl.multiple_of(jnp.maximum(n - 1, 0) * BLOCK, BLOCK)
        dk_ref[pl.ds(p0, BLOCK), :] += _rope(dk2[:BLOCK], tprev_v, -1.0)
        dv_ref[pl.ds(p0, BLOCK), :] += dv2[:BLOCK]
        dk_ref[pl.ds(c0, BLOCK), :] += _rope(dk2[BLOCK:], tcur_v, -1.0)
        dv_ref[pl.ds(c0, BLOCK), :] += dv2[BLOCK:]

    full = pl.BlockSpec((T, KV_WIDTH), lambda n: (0, 0))
    return pl.pallas_call(
        body, name=name, grid=(nb,),
        in_specs=[qspec, cur(C_K), prev(C_K), cur(C_V), prev(C_V), own, tcur, tprev, sink],
        out_specs=[own, full, full, pl.BlockSpec((N_KV_HEADS, GROUP, 128), lambda n: (0, 0, 0))],
        out_shape=[jax.ShapeDtypeStruct((T, D_MODEL), F32), jax.ShapeDtypeStruct((T, KV_WIDTH), F32),
                   jax.ShapeDtypeStruct((T, KV_WIDTH), F32), jax.ShapeDtypeStruct((N_KV_HEADS, GROUP, 128), F32)],
        compiler_params=pltpu.CompilerParams(dimension_semantics=("arbitrary",)),
    )(P, P, P, P, P, do, tab, tab, sink_col)


_MW = 256
_FW = 1408


def _gate_specs(T, rows, width):
    tr = _pick(T, (rows, 256, 128, 64, 32, 16, 8))
    col = lambda off: pl.BlockSpec((tr, width), lambda i, j, off=off: (i, off // width + j))
    own = pl.BlockSpec((tr, width), lambda i, j: (i, j))
    return tr, col, own


def _merge_fwd(P, mr, ma, name):
    T = P.shape[0]
    tr, col, own = _gate_specs(T, 1024, _MW)

    def body(gr_ref, ga_ref, mr_ref, ma_ref, o_ref):
        o_ref[...] = _sigmoid(gr_ref[...]) * mr_ref[...] + _sigmoid(ga_ref[...]) * ma_ref[...]

    return pl.pallas_call(
        body, name=name, grid=(T // tr, D_MODEL // _MW), in_specs=[col(C_GRNN), col(C_GATTN), own, own],
        out_specs=own, out_shape=jax.ShapeDtypeStruct((T, D_MODEL), F32),
        compiler_params=pltpu.CompilerParams(dimension_semantics=("parallel", "parallel")),
    )(P, P, mr, ma)


def _merge_bwd(P, mr, ma, dm, name):
    T = P.shape[0]
    tr, col, own = _gate_specs(T, 512, _MW)

    def body(gr_ref, ga_ref, mr_ref, ma_ref, dm_ref, dmr_ref, dma_ref, dgr_ref, dga_ref):
        dm = dm_ref[...]
        sr = _sigmoid(gr_ref[...])
        sa = _sigmoid(ga_ref[...])
        dmr_ref[...] = dm * sr
        dma_ref[...] = dm * sa
        dgr_ref[...] = dm * mr_ref[...] * sr * (1.0 - sr)
        dga_ref[...] = dm * ma_ref[...] * sa * (1.0 - sa)

    shp = jax.ShapeDtypeStruct((T, D_MODEL), F32)
    return pl.pallas_call(
        body, name=name, grid=(T // tr, D_MODEL // _MW), in_specs=[col(C_GRNN), col(C_GATTN), own, own, own],
        out_specs=[own] * 4, out_shape=[shp] * 4,
        compiler_params=pltpu.CompilerParams(dimension_semantics=("parallel", "parallel")),
    )(P, P, mr, ma, dm)


def _swiglu_fwd(U, name):
    T = U.shape[0]
    tr, col, own = _gate_specs(T, 256, _FW)

    def body(g_ref, u_ref, o_ref):
        g = g_ref[...]
        o_ref[...] = g * _sigmoid(g) * u_ref[...]

    return pl.pallas_call(
        body, name=name, grid=(T // tr, D_FF // _FW), in_specs=[col(0), col(D_FF)],
        out_specs=own, out_shape=jax.ShapeDtypeStruct((T, D_FF), F32),
        compiler_params=pltpu.CompilerParams(dimension_semantics=("parallel", "parallel")),
    )(U, U)


def _swiglu_bwd(U, dact, name):
    T = U.shape[0]
    nf = D_FF // _FW
    tr = _pick(T, (256, 128, 64, 32, 16, 8))

    def body(g_ref, u_ref, da_ref, o_ref):
        g = g_ref[...]
        da = da_ref[...]
        s = _sigmoid(g)
        is_gate = pl.program_id(1) < nf
        o_ref[...] = jnp.where(is_gate, da * u_ref[...] * s * (1.0 + g * (1.0 - s)), da * g * s)

    return pl.pallas_call(
        body, name=name, grid=(T // tr, 2 * nf),
        in_specs=[pl.BlockSpec((tr, _FW), lambda i, j: (i, j % nf)),
                  pl.BlockSpec((tr, _FW), lambda i, j: (i, nf + j % nf)),
                  pl.BlockSpec((tr, _FW), lambda i, j: (i, j % nf))],
        out_specs=pl.BlockSpec((tr, _FW), lambda i, j: (i, j)),
        out_shape=jax.ShapeDtypeStruct((T, 2 * D_FF), F32),
        compiler_params=pltpu.CompilerParams(dimension_semantics=("parallel", "parallel")),
    )(U, U, dact)


def _cross_probs(qh, kh):
    s = lax.dot_general(qh, kh, _DIMS["nt"], preferred_element_type=F32) * (CROSS_HEAD_DIM ** -0.5)
    p = jnp.exp(s - jnp.max(s, axis=1, keepdims=True))
    return p / jnp.sum(p, axis=1, keepdims=True)


def _cross_fwd(q, kv, name):
    T = q.shape[0]
    M = kv.shape[0]
    tr = _pick(T, (ROW_TILE, 128, 64, 32, 16, 8))
    W = CROSS_HEAD_DIM

    def body(q_ref, kv_ref, o_ref):
        for h in range(CROSS_HEADS):
            qh = q_ref[:, h * W:(h + 1) * W].astype(BF16)
            kh = kv_ref[:, h * W:(h + 1) * W].astype(BF16)
            vh = kv_ref[:, D_MODEL + h * W:D_MODEL + (h + 1) * W].astype(BF16)
            pn = _cross_probs(qh, kh)
            o_ref[:, h * W:(h + 1) * W] = jnp.dot(pn.astype(BF16), vh, preferred_element_type=F32)

    row = pl.BlockSpec((tr, D_MODEL), lambda i: (i, 0))
    return pl.pallas_call(
        body, name=name, grid=(T // tr,), in_specs=[row, pl.BlockSpec((M, 2 * D_MODEL), lambda i: (0, 0))],
        out_specs=row, out_shape=jax.ShapeDtypeStruct((T, D_MODEL), F32),
        compiler_params=pltpu.CompilerParams(dimension_semantics=("parallel",)),
    )(q, kv)


def _cross_bwd(q, kv, do, name):
    T = q.shape[0]
    M = kv.shape[0]
    tr = _pick(T, (ROW_TILE, 128, 64, 32, 16, 8))
    W = CROSS_HEAD_DIM

    def body(q_ref, kv_ref, do_ref, dq_ref, dkv_ref):
        @pl.when(pl.program_id(0) == 0)
        def _():
            dkv_ref[...] = jnp.zeros_like(dkv_ref)

        for h in range(CROSS_HEADS):
            qh = q_ref[:, h * W:(h + 1) * W].astype(BF16)
            kh = kv_ref[:, h * W:(h + 1) * W].astype(BF16)
            vh = kv_ref[:, D_MODEL + h * W:D_MODEL + (h + 1) * W].astype(BF16)
            doh = do_ref[:, h * W:(h + 1) * W].astype(BF16)
            pn = _cross_probs(qh, kh)
            dp = lax.dot_general(doh, vh, _DIMS["nt"], preferred_element_type=F32)
            delta = jnp.sum(pn * dp, axis=1, keepdims=True)
            dsc = (pn * (dp - delta) * (W ** -0.5)).astype(BF16)
            dq_ref[:, h * W:(h + 1) * W] = jnp.dot(dsc, kh, preferred_element_type=F32)
            dkv_ref[:, h * W:(h + 1) * W] += lax.dot_general(dsc, qh, _DIMS["tn"], preferred_element_type=F32)
            dkv_ref[:, D_MODEL + h * W:D_MODEL + (h + 1) * W] += lax.dot_general(
                pn.astype(BF16), doh, _DIMS["tn"], preferred_element_type=F32)

    row = pl.BlockSpec((tr, D_MODEL), lambda i: (i, 0))
    full = pl.BlockSpec((M, 2 * D_MODEL), lambda i: (0, 0))
    return pl.pallas_call(
        body, name=name, grid=(T // tr,), in_specs=[row, full, row], out_specs=[row, full],
        out_shape=[jax.ShapeDtypeStruct((T, D_MODEL), F32), jax.ShapeDtypeStruct((M, 2 * D_MODEL), F32)],
        compiler_params=pltpu.CompilerParams(dimension_semantics=("arbitrary",)),
    )(q, kv, do)


def _loss_head(y, target, name):
    T, D = y.shape
    tr = _pick(T, (ROW_TILE, 128, 64, 32, 16, 8))

    def body(y_ref, t_ref, l_ref, dy_ref):
        @pl.when(pl.program_id(0) == 0)
        def _():
            l_ref[...] = jnp.zeros_like(l_ref)

        err = y_ref[...] - t_ref[...]
        dy_ref[...] = err * (1.0 / D)
        l_ref[...] += jnp.broadcast_to(0.5 * jnp.sum(jnp.mean(err * err, axis=-1, keepdims=True), axis=0, keepdims=True),
                                       (8, 128))

    row = pl.BlockSpec((tr, D), lambda i: (i, 0))
    return pl.pallas_call(
        body, name=name, grid=(T // tr,), in_specs=[row, row],
        out_specs=[pl.BlockSpec((8, 128), lambda i: (0, 0)), row],
        out_shape=[jax.ShapeDtypeStruct((8, 128), F32), jax.ShapeDtypeStruct((T, D), F32)],
        compiler_params=pltpu.CompilerParams(dimension_semantics=("arbitrary",)),
    )(y, target)


def _sum_slots(recv, name):
    _, R, C = recv.shape
    tr = _pick(R, (ROW_TILE, 128, 64, 32, 16, 8))

    def body(r_ref, o_ref):
        acc = r_ref[0].astype(F32)
        for d in range(1, N_DEV):
            acc = acc + r_ref[d].astype(F32)
        o_ref[...] = acc

    return pl.pallas_call(
        body, name=name, grid=(R // tr,), in_specs=[pl.BlockSpec((N_DEV, tr, C), lambda i: (0, i, 0))],
        out_specs=pl.BlockSpec((tr, C), lambda i: (i, 0)), out_shape=jax.ShapeDtypeStruct((R, C), F32),
        compiler_params=pltpu.CompilerParams(dimension_semantics=("parallel",)),
    )(recv)


def _adamw(w, g, m, v, name):
    shape = w.shape
    C = shape[-1]
    R = math.prod(shape[:-1])
    w2, g2, m2, v2 = (t.reshape(R, C) for t in (w, g, m, v))
    tr = _pick(R, (ROW_TILE, 128, 64, 32, 16, 8))

    def body(w_ref, g_ref, m_ref, v_ref, d_ref, mo_ref, vo_ref):
        gg = g_ref[...]
        mn = ADAM_B1 * m_ref[...] + (1.0 - ADAM_B1) * gg
        vn = ADAM_B2 * v_ref[...] + (1.0 - ADAM_B2) * (gg * gg)
        m_hat = mn / (1.0 - ADAM_B1 ** ADAM_STEP)
        v_hat = vn / (1.0 - ADAM_B2 ** ADAM_STEP)
        d_ref[...] = -ADAM_LR * (m_hat / (jnp.sqrt(v_hat) + ADAM_EPS) + ADAM_WD * w_ref[...])
        mo_ref[...] = mn
        vo_ref[...] = vn

    blk = pl.BlockSpec((tr, C), lambda i: (i, 0))
    shp = jax.ShapeDtypeStruct((R, C), F32)
    d, mo, vo = pl.pallas_call(
        body, name=name, grid=(R // tr,), in_specs=[blk] * 4, out_specs=[blk] * 3, out_shape=[shp] * 3,
        compiler_params=pltpu.CompilerParams(dimension_semantics=("parallel",)),
    )(w2, g2, m2, v2)
    return d.reshape(shape), mo.reshape(shape), vo.reshape(shape)


def _all_gather_many(bufs, name):
    n = len(bufs)

    def body(*refs):
        xs, outs = refs[:n], refs[n:2 * n]
        send_sems, recv_sems, local_sems = refs[2 * n:]
        x, y, c = lax.axis_index("x"), lax.axis_index("y"), lax.axis_index("c")
        me, sibling = (x, y, c), (x, y, 1 - c)
        chips = [(1 - x, y), (x, 1 - y), (1 - x, 1 - y)]

        def slot(i, px, py, pc):
            return outs[i].at[4 * px + 2 * py + pc]

        def copy(i, k, block, to, src=None):
            return pltpu.make_async_remote_copy(
                src_ref=slot(i, *block) if src is None else src, dst_ref=slot(i, *block),
                send_sem=send_sems.at[7 * i + k], recv_sem=recv_sems.at[7 * i + k],
                device_id=to, device_id_type=pl.DeviceIdType.MESH)

        mine = [pltpu.make_async_copy(xs[i], slot(i, *me), local_sems.at[i]) for i in range(n)]
        for cp in mine:
            cp.start()
        first = [copy(i, 0, me, sibling, src=xs[i]) for i in range(n)]
        for j, chip in enumerate(chips):
            first += [copy(i, 1 + j, me, (*chip, c), src=xs[i]) for i in range(n)]
        for cp in first:
            cp.start()
        passed = []
        for j, chip in enumerate(chips):
            for i in range(n):
                copy(i, 1 + j, (*chip, c), me).wait_recv()
                passed.append(copy(i, 4 + j, (*chip, c), sibling))
                passed[-1].start()
        for i in range(n):
            copy(i, 0, sibling, me).wait_recv()
        for j, chip in enumerate(chips):
            for i in range(n):
                copy(i, 4 + j, (*chip, 1 - c), me).wait_recv()
        for cp in first + passed:
            cp.wait_send()
        for cp in mine:
            cp.wait()

    hbm = pl.BlockSpec(memory_space=pl.ANY)
    return pl.pallas_call(
        body, name=name, out_shape=[jax.ShapeDtypeStruct((N_DEV,) + b.shape, b.dtype) for b in bufs],
        in_specs=[hbm] * n, out_specs=[hbm] * n,
        scratch_shapes=[pltpu.SemaphoreType.DMA((7 * n,)), pltpu.SemaphoreType.DMA((7 * n,)),
                        pltpu.SemaphoreType.DMA((n,))],
    )(*bufs)


def _all_gather(buf, name):
    return _all_gather_many([buf], name)[0]


def _exchange(send, name):
    _, R, C = send.shape

    def body(s_ref, out_ref, send_sems, recv_sems, local_sem):
        x, y, c = lax.axis_index("x"), lax.axis_index("y"), lax.axis_index("c")
        my_slot = 4 * x + 2 * y + c
        mine = pltpu.make_async_copy(s_ref.at[my_slot], out_ref.at[my_slot], local_sem)
        mine.start()
        sends, recvs = [], []
        for k in range(1, N_DEV):
            px, py, pc = x ^ ((k >> 2) & 1), y ^ ((k >> 1) & 1), c ^ (k & 1)
            peer_slot = 4 * px + 2 * py + pc
            sends.append(pltpu.make_async_remote_copy(
                src_ref=s_ref.at[peer_slot], dst_ref=out_ref.at[my_slot],
                send_sem=send_sems.at[k - 1], recv_sem=recv_sems.at[k - 1],
                device_id=(px, py, pc), device_id_type=pl.DeviceIdType.MESH))
            recvs.append(pltpu.make_async_remote_copy(
                src_ref=s_ref.at[my_slot], dst_ref=out_ref.at[peer_slot],
                send_sem=send_sems.at[k - 1], recv_sem=recv_sems.at[k - 1],
                device_id=(px, py, pc), device_id_type=pl.DeviceIdType.MESH))
        for cp in sends:
            cp.start()
        for cp in recvs:
            cp.wait_recv()
        for cp in sends:
            cp.wait_send()
        mine.wait()

    return pl.pallas_call(
        body, name=name, out_shape=jax.ShapeDtypeStruct((N_DEV, R, C), send.dtype),
        in_specs=[pl.BlockSpec(memory_space=pl.ANY)], out_specs=pl.BlockSpec(memory_space=pl.ANY),
        scratch_shapes=[pltpu.SemaphoreType.DMA((7,)), pltpu.SemaphoreType.DMA((7,)), pltpu.SemaphoreType.DMA],
    )(send)


def _row(a, l):
    return a[l:l + 1]


def _layer_fwd(h, mem, W, l, tab):
    s = {}
    n = f"l{l}_"
    s["h0"] = h
    P = _matmul(h, W["w_in"][l], "nt", n + "proj")
    s["P"] = P
    sink_col = jnp.repeat(W["sinks"][l].reshape(N_KV_HEADS, GROUP), BLOCK, axis=1)[:, :, None]
    s["sink_col"] = sink_col
    y_rnn, hs = _rnn_fwd(P, W["conv_w"][l], _row(W["conv_b"], l), W["w_rg"][l], _row(W["b_rg"], l),
                         W["w_ig"][l], _row(W["b_ig"], l), _row(W["lru_lambda"], l), n + "rnn_fwd")
    y_attn = _swa_fwd(P, tab, sink_col, n + "swa_fwd")
    mr = _matmul(y_rnn, W["w_br_rnn"][l], "nn", n + "br_rnn")
    ma = _matmul(y_attn, W["w_br_attn"][l], "nn", n + "br_attn")
    merged = _merge_fwd(P, mr, ma, n + "merge_fwd")
    mix = _matmul(merged, W["w_out"][l], "nn", n + "w_out")
    h1, xh1, rs1 = _ln_fwd(h, mix, _row(W["ln1_g"], l), _row(W["ln1_b"], l), n + "ln1_fwd")
    s.update(hs=hs, y_rnn=y_rnn, y_attn=y_attn, mr=mr, ma=ma, merged=merged, xh1=xh1, rs1=rs1, h1=h1)

    qc = _matmul(h1, W["cq_w"][l], "nn", n + "cq")
    kv = _matmul(mem, W["ckv_w"][l], "nt", n + "ckv")
    oc = _cross_fwd(qc, kv, n + "cross_fwd")
    ca = _matmul(oc, W["co_w"][l], "nn", n + "co")
    h2, xh2, rs2 = _ln_fwd(h1, ca, _row(W["ln2_g"], l), _row(W["ln2_b"], l), n + "ln2_fwd")
    s.update(qc=qc, kv=kv, oc=oc, xh2=xh2, rs2=rs2, h2=h2)

    U = _matmul(h2, W["ffn_wi"][l], "nt", n + "ffn_wi")
    act = _swiglu_fwd(U, n + "swiglu_fwd")
    f = _matmul(act, W["ffn_wo"][l], "nn", n + "ffn_wo")
    h3, xh3, rs3 = _ln_fwd(h2, f, _row(W["ln3_g"], l), _row(W["ln3_b"], l), n + "ln3_fwd")
    s.update(U=U, act=act, xh3=xh3, rs3=rs3)
    return h3, s


def _layer_bwd(dh3, mem, W, l, tab, s):
    n = f"l{l}_"
    g = {}
    dz3, g["ln3_g"], g["ln3_b"] = _ln_bwd(dh3, s["xh3"], s["rs3"], _row(W["ln3_g"], l), n + "ln3_bwd")
    g["ffn_wo"] = _matmul(s["act"], dz3, "tn", n + "d_ffn_wo", out_dtype=BF16)
    dact = _matmul(dz3, W["ffn_wo"][l], "nt", n + "d_act")
    dU = _swiglu_bwd(s["U"], dact, n + "swiglu_bwd")
    g["ffn_wi"] = _matmul(dU, s["h2"], "tn", n + "d_ffn_wi", out_dtype=BF16)
    dh2 = _matmul(dU, W["ffn_wi"][l], "nn", n + "d_h2", add=dz3, add_scale=ALPHA)
    dz2, g["ln2_g"], g["ln2_b"] = _ln_bwd(dh2, s["xh2"], s["rs2"], _row(W["ln2_g"], l), n + "ln2_bwd")
    g["co_w"] = _matmul(s["oc"], dz2, "tn", n + "d_co", out_dtype=BF16)
    doc = _matmul(dz2, W["co_w"][l], "nt", n + "d_oc")
    dqc, dkv = _cross_bwd(s["qc"], s["kv"], doc, n + "cross_bwd")
    g["ckv_w"] = _matmul(dkv, mem, "tn", n + "d_ckv", out_dtype=BF16)
    g["cq_w"] = _matmul(s["h1"], dqc, "tn", n + "d_cq", out_dtype=BF16)
    dh1 = _matmul(dqc, W["cq_w"][l], "nt", n + "d_h1", add=dz2, add_scale=ALPHA)
    dz1, g["ln1_g"], g["ln1_b"] = _ln_bwd(dh1, s["xh1"], s["rs1"], _row(W["ln1_g"], l), n + "ln1_bwd")
    g["w_out"] = _matmul(s["merged"], dz1, "tn", n + "d_w_out", out_dtype=BF16)
    dmerged = _matmul(dz1, W["w_out"][l], "nt", n + "d_merged")
    dmr, dma, dgrnn, dgattn = _merge_bwd(s["P"], s["mr"], s["ma"], dmerged, n + "merge_bwd")
    g["w_br_rnn"] = _matmul(s["y_rnn"], dmr, "tn", n + "d_br_rnn", out_dtype=BF16)
    g["w_br_attn"] = _matmul(s["y_attn"], dma, "tn", n + "d_br_attn", out_dtype=BF16)
    dy_rnn = _matmul(dmr, W["w_br_rnn"][l], "nt", n + "d_y_rnn")
    dy_attn = _matmul(dma, W["w_br_attn"][l], "nt", n + "d_y_attn")
    dxr, dgr, g["conv_w"], g["conv_b"], g["w_rg"], g["b_rg"], g["w_ig"], g["b_ig"], g["lru_lambda"] = _rnn_bwd(
        s["P"], s["hs"], dy_rnn, W["conv_w"][l], _row(W["conv_b"], l), W["w_rg"][l], _row(W["b_rg"], l),
        W["w_ig"][l], _row(W["b_ig"], l), _row(W["lru_lambda"], l), n + "rnn_bwd")
    dq, dk, dv, dsk = _swa_bwd(s["P"], dy_attn, tab, s["sink_col"], n + "swa_bwd")
    g["sinks"] = dsk[:, :, 0].reshape(1, N_Q_HEADS)
    dP = jnp.concatenate([dxr, dgr, dq, dk, dv, dgrnn, dgattn], axis=1)
    g["w_in"] = _matmul(dP, s["h0"], "tn", n + "d_w_in", out_dtype=BF16)
    dh = _matmul(dP, W["w_in"][l], "nn", n + "d_h0", add=dz1, add_scale=ALPHA)
    return dh, g


def _local_step(x, mem, target, W, on_layer_grads):
    T = x.shape[0]
    tab = _rope_table(T)
    h = x
    saved = []
    for l in range(DEPTH):
        h, s = _layer_fwd(h, mem, W, l, tab)
        saved.append(s)
    lblk, dh = _loss_head(h, target, "loss_head")
    for l in reversed(range(DEPTH)):
        dh, g = _layer_bwd(dh, mem, W, l, tab, saved[l])
        on_layer_grads(l, g)
    return lblk[0, 0], dh


COL_SHARDED = ("w_in", "ckv_w", "ffn_wi")
GATE_MATS = ("w_rg", "w_ig")


def _shard_rows(shards, l):
    out = []
    for n, r in PACK_ROWS:
        a = shards[n][l].astype(BF16)
        if n in COL_SHARDED:
            a = a.T
        elif n in GATE_MATS:
            a = a.reshape(RNN_BLOCKS * RNN_BLOCK // N_DEV, RNN_BLOCK)
        out.append(a)
    return out


def _full_weight(G, name):
    if name in GATE_MATS:
        return jnp.transpose(G.reshape(N_DEV, RNN_BLOCKS, RNN_BLOCK // N_DEV, RNN_BLOCK), (1, 0, 2, 3)).reshape(
            RNN_BLOCKS, RNN_BLOCK, RNN_BLOCK)
    return G.reshape(N_DEV * G.shape[1], G.shape[2])


def _pack_blocks(g):
    parts = []
    for name, r in PACK_ROWS:
        a = g[name]
        if name in GATE_MATS:
            a = jnp.transpose(a.astype(BF16).reshape(RNN_BLOCKS, N_DEV, RNN_BLOCK // N_DEV, RNN_BLOCK), (1, 0, 2, 3))
        parts.append(a.reshape(N_DEV, r, D_MODEL))
    return jnp.concatenate(parts, axis=1)


def _pack_small(g):
    rows = [g["conv_w"]]
    for nme in SMALL_NAMES:
        a = g[nme]
        if nme == "sinks":
            a = jnp.pad(a, ((0, 0), (0, D_MODEL - N_Q_HEADS)))
        rows.append(a)
    rows.append(jnp.zeros((SMALL_ROWS - CONV_WIDTH - len(SMALL_NAMES), D_MODEL), F32))
    return jnp.concatenate(rows, axis=0)


_SHARD_SHAPES = {"w_in": (1024, 672), "w_br_rnn": (128, 1024), "w_br_attn": (128, 1024), "w_out": (128, 1024),
                 "cq_w": (128, 1024), "ckv_w": (1024, 256), "co_w": (128, 1024), "ffn_wi": (1024, 704),
                 "ffn_wo": (352, 1024), "w_rg": (4, 32, 256), "w_ig": (4, 32, 256)}

WEIGHT_NAMES = ("w_in", "conv_w", "conv_b", "w_rg", "b_rg", "w_ig", "b_ig", "lru_lambda", "w_br_rnn", "w_br_attn",
                "sinks", "w_out", "ln1_g", "ln1_b", "cq_w", "ckv_w", "co_w", "ln2_g", "ln2_b", "ffn_wi", "ffn_wo",
                "ln3_g", "ln3_b")


def kernel(x, mem, w_in, conv_w, conv_b, w_rg, b_rg, w_ig, b_ig, lru_lambda, w_br_rnn, w_br_attn, sinks, w_out, ln1_g, ln1_b, cq_w, ckv_w, co_w, ln2_g, ln2_b, ffn_wi, ffn_wo, ln3_g, ln3_b, loss_target, m_w_in, m_conv_w, m_conv_b, m_w_rg, m_b_rg, m_w_ig, m_b_ig, m_lru_lambda, m_w_br_rnn, m_w_br_attn, m_sinks, m_w_out, m_ln1_g, m_ln1_b, m_cq_w, m_ckv_w, m_co_w, m_ln2_g, m_ln2_b, m_ffn_wi, m_ffn_wo, m_ln3_g, m_ln3_b, v_w_in, v_conv_w, v_conv_b, v_w_rg, v_b_rg, v_w_ig, v_b_ig, v_lru_lambda, v_w_br_rnn, v_w_br_attn, v_sinks, v_w_out, v_ln1_g, v_ln1_b, v_cq_w, v_ckv_w, v_co_w, v_ln2_g, v_ln2_b, v_ffn_wi, v_ffn_wo, v_ln3_g, v_ln3_b):
    w = dict(w_in=w_in, conv_w=conv_w, conv_b=conv_b, w_rg=w_rg, b_rg=b_rg, w_ig=w_ig, b_ig=b_ig,
             lru_lambda=lru_lambda, w_br_rnn=w_br_rnn, w_br_attn=w_br_attn, sinks=sinks, w_out=w_out, ln1_g=ln1_g,
             ln1_b=ln1_b, cq_w=cq_w, ckv_w=ckv_w, co_w=co_w, ln2_g=ln2_g, ln2_b=ln2_b, ffn_wi=ffn_wi, ffn_wo=ffn_wo,
             ln3_g=ln3_g, ln3_b=ln3_b)
    m = dict(w_in=m_w_in, conv_w=m_conv_w, conv_b=m_conv_b, w_rg=m_w_rg, b_rg=m_b_rg, w_ig=m_w_ig, b_ig=m_b_ig,
             lru_lambda=m_lru_lambda, w_br_rnn=m_w_br_rnn, w_br_attn=m_w_br_attn, sinks=m_sinks, w_out=m_w_out,
             ln1_g=m_ln1_g, ln1_b=m_ln1_b, cq_w=m_cq_w, ckv_w=m_ckv_w, co_w=m_co_w, ln2_g=m_ln2_g, ln2_b=m_ln2_b,
             ffn_wi=m_ffn_wi, ffn_wo=m_ffn_wo, ln3_g=m_ln3_g, ln3_b=m_ln3_b)
    v = dict(w_in=v_w_in, conv_w=v_conv_w, conv_b=v_conv_b, w_rg=v_w_rg, b_rg=v_b_rg, w_ig=v_w_ig, b_ig=v_b_ig,
             lru_lambda=v_lru_lambda, w_br_rnn=v_w_br_rnn, w_br_attn=v_w_br_attn, sinks=v_sinks, w_out=v_w_out,
             ln1_g=v_ln1_g, ln1_b=v_ln1_b, cq_w=v_cq_w, ckv_w=v_ckv_w, co_w=v_co_w, ln2_g=v_ln2_g, ln2_b=v_ln2_b,
             ffn_wi=v_ffn_wi, ffn_wo=v_ffn_wo, ln3_g=v_ln3_g, ln3_b=v_ln3_b)
    my_dev = 4 * lax.axis_index("x") + 2 * lax.axis_index("y") + lax.axis_index("c")

    W = {n: [] for n, _ in PACK_ROWS}
    for l in range(DEPTH):
        gathered = _all_gather_many(_shard_rows(w, l), f"l{l}_gather_weights")
        for (n, _), G in zip(PACK_ROWS, gathered):
            W[n].append(_full_weight(G, n))
    conv_all = _all_gather(conv_w.reshape(DEPTH * CONV_WIDTH, D_MODEL // N_DEV), "gather_conv")
    W["conv_w"] = jnp.transpose(conv_all, (1, 0, 2)).reshape(DEPTH, CONV_WIDTH, D_MODEL)
    for n in SMALL_NAMES:
        W[n] = w[n]

    summed = [None] * DEPTH
    small = [None] * DEPTH

    def on_layer_grads(l, g):
        recv = _exchange(_pack_blocks(g), f"l{l}_exchange_grads")
        summed[l] = _sum_slots(recv, f"l{l}_sum_grads")
        small[l] = _pack_small(g)

    loss_local, dx = _local_step(x[0], mem[0], loss_target[0], W, on_layer_grads)
    loss = lax.psum(loss_local, MESH_AXES)

    small_all = _all_gather(jnp.concatenate(small, axis=0), "gather_small_grads")
    small_sum = _sum_slots(small_all, "sum_small_grads").reshape(DEPTH, SMALL_ROWS, D_MODEL)
    G = jnp.stack(summed)

    grads = {}
    for n, r in PACK_ROWS:
        o, _ = PACK_OFF[n]
        blk = G[:, o:o + r, :]
        grads[n] = jnp.transpose(blk, (0, 2, 1)) if n in COL_SHARDED else blk.reshape((DEPTH,) + _SHARD_SHAPES[n])
    conv_full = small_sum[:, :CONV_WIDTH, :]
    grads["conv_w"] = lax.dynamic_slice_in_dim(conv_full, my_dev * (D_MODEL // N_DEV), D_MODEL // N_DEV, axis=2)
    for i, n in enumerate(SMALL_NAMES):
        row = small_sum[:, CONV_WIDTH + i, :]
        grads[n] = row[:, :N_Q_HEADS] if n == "sinks" else row

    deltas, new_m, new_v = {}, {}, {}
    for n in WEIGHT_NAMES:
        deltas[n], new_m[n], new_v[n] = _adamw(w[n], grads[n], m[n], v[n], "adamw_" + n)

    return (loss, dx[None], *[grads[n] for n in WEIGHT_NAMES], *[deltas[n] for n in WEIGHT_NAMES],
            *[new_m[n] for n in WEIGHT_NAMES], *[new_v[n] for n in WEIGHT_NAMES])
```

```python
import functools
import math

import jax
import jax.numpy as jnp
from jax import lax
from jax.experimental import pallas as pl
from jax.experimental.pallas import tpu as pltpu

F32 = jnp.float32
BF16 = jnp.bfloat16

D_MODEL = 1024
DEPTH = 4
N_DEV = 8
RNN_BLOCKS = 4
RNN_BLOCK = 256
CONV_WIDTH = 4
LRU_C = 8.0
HEAD_DIM = 64
N_Q_HEADS = 16
N_KV_HEADS = 2
GROUP = 8
KV_WIDTH = 128
BLOCK = 128
ROPE_THETA = 500000.0
ROT_DIM = 16
IN_COLS = 5376
CROSS_HEADS = 4
CROSS_HEAD_DIM = 256
D_FF = 2816
LN_EPS = 1e-5
ALPHA = (2 * DEPTH) ** 0.25
NEG_INF = -1e30

ADAM_LR = 0.001
ADAM_B1 = 0.9
ADAM_B2 = 0.999
ADAM_EPS = 1e-08
ADAM_WD = 0.01
ADAM_STEP = 10

C_XR, C_GR, C_Q, C_K, C_V, C_GRNN, C_GATTN = 0, 1024, 2048, 3072, 3200, 3328, 4352

TIME_CHUNK = 256
ROW_TILE = 256

MESH_AXES = ("x", "y", "c")

PACK_ROWS = (("w_in", 672), ("w_br_rnn", 128), ("w_br_attn", 128), ("w_out", 128), ("cq_w", 128),
             ("ckv_w", 256), ("co_w", 128), ("ffn_wi", 704), ("ffn_wo", 352), ("w_rg", 32), ("w_ig", 32))
PACK_OFF = {}
_o = 0
for _n, _r in PACK_ROWS:
    PACK_OFF[_n] = (_o, _r)
    _o += _r
LAYER_ROWS = _o

SMALL_NAMES = ("conv_b", "b_rg", "b_ig", "lru_lambda", "sinks", "ln1_g", "ln1_b", "ln2_g", "ln2_b", "ln3_g", "ln3_b")
SMALL_ROWS = 16


def _pick(dim, cands):
    for c in cands:
        if dim % c == 0:
            return c
    return dim


_DIMS = {"nn": (((1,), (0,)), ((), ())), "nt": (((1,), (1,)), ((), ())), "tn": (((0,), (0,)), ((), ()))}


def _matmul(a, b, mode, name, add=None, add_scale=1.0, out_dtype=F32, after=None):
    if mode == "nn":
        (M, K), (_, N) = a.shape, b.shape
    elif mode == "nt":
        (M, K), (N, _) = a.shape, b.shape
    else:
        (K, M), (_, N) = a.shape, b.shape
    tm = _pick(M, (768, 512, 1408, 256, 128, 64, 32, 16, 8))
    tn = _pick(N, (768, 512, 256, 128))
    tk = _pick(K, (1792, 1408, 1024, 768, 512, 256, 128))
    nk = K // tk
    dims = _DIMS[mode]

    def body(*refs):
        if after is not None:
            refs = refs[:-3] + refs[-2:]
        if add is None:
            a_ref, b_ref, o_ref, acc_ref = refs
        else:
            a_ref, b_ref, c_ref, o_ref, acc_ref = refs
        k = pl.program_id(2)

        @pl.when(k == 0)
        def _():
            acc_ref[...] = jnp.zeros_like(acc_ref)

        acc_ref[...] += lax.dot_general(a_ref[...].astype(BF16), b_ref[...].astype(BF16), dims,
                                        preferred_element_type=F32)

        @pl.when(k == nk - 1)
        def _():
            r = acc_ref[...]
            if add is not None:
                r = r + add_scale * c_ref[...]
            o_ref[...] = r.astype(out_dtype)

    if mode == "nn":
        a_spec = pl.BlockSpec((tm, tk), lambda i, j, k: (i, k))
        b_spec = pl.BlockSpec((tk, tn), lambda i, j, k: (k, j))
    elif mode == "nt":
        a_spec = pl.BlockSpec((tm, tk), lambda i, j, k: (i, k))
        b_spec = pl.BlockSpec((tn, tk), lambda i, j, k: (j, k))
    else:
        a_spec = pl.BlockSpec((tk, tm), lambda i, j, k: (k, i))
        b_spec = pl.BlockSpec((tk, tn), lambda i, j, k: (k, j))
    in_specs = [a_spec, b_spec]
    args = [a, b]
    if add is not None:
        in_specs.append(pl.BlockSpec((tm, tn), lambda i, j, k: (i, j)))
        args.append(add)
    if after is not None:
        in_specs.append(pl.BlockSpec(memory_space=pl.ANY))
        args.append(after)
    return pl.pallas_call(
        body, name=name, grid=(M // tm, N // tn, nk),
        in_specs=in_specs, out_specs=pl.BlockSpec((tm, tn), lambda i, j, k: (i, j)),
        out_shape=jax.ShapeDtypeStruct((M, N), out_dtype),
        scratch_shapes=[pltpu.VMEM((tm, tn), F32)],
        compiler_params=pltpu.CompilerParams(dimension_semantics=("parallel", "parallel", "arbitrary")),
    )(*args)


def _ln_fwd(h, f, g, b, name):
    T, D = h.shape
    tr = _pick(T, (ROW_TILE, 128, 64, 32, 16, 8))

    def body(h_ref, f_ref, g_ref, b_ref, o_ref, xh_ref, rs_ref):
        z = ALPHA * h_ref[...] + f_ref[...]
        mu = jnp.mean(z, axis=-1, keepdims=True)
        zc = z - mu
        var = jnp.mean(zc * zc, axis=-1, keepdims=True)
        rstd = lax.rsqrt(var + LN_EPS)
        xh = zc * rstd
        xh_ref[...] = xh
        rs_ref[...] = rstd
        o_ref[...] = xh * g_ref[...] + b_ref[...]

    row = pl.BlockSpec((tr, D), lambda i: (i, 0))
    vec = pl.BlockSpec((1, D), lambda i: (0, 0))
    return pl.pallas_call(
        body, name=name, grid=(T // tr,), in_specs=[row, row, vec, vec],
        out_specs=[row, row, pl.BlockSpec((tr, 1), lambda i: (i, 0))],
        out_shape=[jax.ShapeDtypeStruct((T, D), F32), jax.ShapeDtypeStruct((T, D), F32),
                   jax.ShapeDtypeStruct((T, 1), F32)],
        compiler_params=pltpu.CompilerParams(dimension_semantics=("parallel",)),
    )(h, f, g, b)


def _ln_bwd(dout, xh, rstd, g, name, after=None):
    T, D = dout.shape
    tr = _pick(T, (ROW_TILE, 128, 64, 32, 16, 8))

    def body(do_ref, xh_ref, rs_ref, g_ref, *rest):
        dz_ref, dg_ref, db_ref = rest[-3:]

        @pl.when(pl.program_id(0) == 0)
        def _():
            dg_ref[...] = jnp.zeros_like(dg_ref)
            db_ref[...] = jnp.zeros_like(db_ref)

        do = do_ref[...]
        xh = xh_ref[...]
        dxh = do * g_ref[...]
        m1 = jnp.mean(dxh, axis=-1, keepdims=True)
        m2 = jnp.mean(dxh * xh, axis=-1, keepdims=True)
        dz_ref[...] = rs_ref[...] * (dxh - m1 - xh * m2)
        dg_ref[...] += jnp.sum(do * xh, axis=0, keepdims=True)
        db_ref[...] += jnp.sum(do, axis=0, keepdims=True)

    row = pl.BlockSpec((tr, D), lambda i: (i, 0))
    vec = pl.BlockSpec((1, D), lambda i: (0, 0))
    in_specs = [row, row, pl.BlockSpec((tr, 1), lambda i: (i, 0)), vec]
    args = [dout, xh, rstd, g]
    if after is not None:
        in_specs.append(pl.BlockSpec(memory_space=pl.ANY))
        args.append(after)
    return pl.pallas_call(
        body, name=name, grid=(T // tr,),
        in_specs=in_specs, out_specs=[row, vec, vec],
        out_shape=[jax.ShapeDtypeStruct((T, D), F32), jax.ShapeDtypeStruct((1, D), F32),
                   jax.ShapeDtypeStruct((1, D), F32)],
        compiler_params=pltpu.CompilerParams(dimension_semantics=("arbitrary",)),
    )(*args)


_GELU_C = math.sqrt(2.0 / math.pi)


def _gelu(x):
    t = jnp.tanh(_GELU_C * (x + 0.044715 * x * x * x))
    return 0.5 * x * (1.0 + t), t


def _gelu_grad(x, t):
    return 0.5 * (1.0 + t) + 0.5 * x * (1.0 - t * t) * _GELU_C * (1.0 + 3 * 0.044715 * x * x)


def _sigmoid(x):
    return 1.0 / (1.0 + jnp.exp(-x))


def _softplus_neg(lam):
    z = jnp.exp(-jnp.abs(lam))
    u = 1.0 + z
    l1p = jnp.where(u == 1.0, z, jnp.log(u) * z / jnp.where(u == 1.0, 1.0, u - 1.0))
    return jnp.maximum(-lam, 0.0) + l1p


def _neg_expm1(x):
    series = x * (1.0 + x * 0.5 * (1.0 + x * (1.0 / 3.0) * (1.0 + x * 0.25 * (1.0 + x * 0.2))))
    return -jnp.where(x > -0.05, series, jnp.exp(x) - 1.0)


def _scan_fwd(a, b):
    n = a.shape[0]
    rows = lax.broadcasted_iota(jnp.int32, a.shape, 0)
    s = 1
    while s < n:
        keep = rows >= s
        b = jnp.where(keep, a * pltpu.roll(b, s, 0) + b, b)
        a = jnp.where(keep, a * pltpu.roll(a, s, 0), a)
        s *= 2
    return a, b


def _scan_bwd(c, b):
    n = c.shape[0]
    rows = lax.broadcasted_iota(jnp.int32, c.shape, 0)
    s = 1
    while s < n:
        keep = rows < n - s
        b = jnp.where(keep, c * pltpu.roll(b, n - s, 0) + b, b)
        c = jnp.where(keep, c * pltpu.roll(c, n - s, 0), c)
        s *= 2
    return c, b


def _rnn_gates(xc, wr, br, wi, bi, sp):
    xb = xc.astype(BF16)
    r = _sigmoid(jnp.dot(xb, wr, preferred_element_type=F32) + br)
    i = _sigmoid(jnp.dot(xb, wi, preferred_element_type=F32) + bi)
    la = -LRU_C * r * sp
    a = jnp.exp(la)
    om = _neg_expm1(2.0 * la)
    mult = jnp.sqrt(om)
    return r, i, a, om, mult


def _rnn_specs(T):
    C = RNN_BLOCK
    col = lambda off: pl.BlockSpec((T, C), lambda n, off=off: (0, off // C + n))
    vec = pl.BlockSpec((1, C), lambda n: (0, n))
    cw = pl.BlockSpec((CONV_WIDTH, C), lambda n: (0, n))
    w = pl.BlockSpec((1, C, C), lambda n: (n, 0, 0))
    own = pl.BlockSpec((T, C), lambda n: (0, n))
    return col, vec, cw, w, own


def _rnn_fwd(P, cw, cb, wrg, brg, wig, big, lam, name):
    T = P.shape[0]
    C = RNN_BLOCK
    tc = _pick(T, (TIME_CHUNK,))
    nch = T // tc

    def body(x_ref, g_ref, cw_ref, cb_ref, wr_ref, br_ref, wi_ref, bi_ref, lam_ref, y_ref, hs_ref, xs_ref):
        sp = _softplus_neg(lam_ref[...])
        wr = wr_ref[0]
        wi = wi_ref[0]
        xs_ref[0:8, :] = jnp.zeros((8, C), F32)

        def chunk(c, hprev):
            r0 = pl.multiple_of(c * tc, tc)
            x = x_ref[pl.ds(r0, tc), :]
            xs_ref[8:, :] = x
            xc = cb_ref[...] + jnp.zeros((tc, C), F32)
            for k in range(CONV_WIDTH):
                xc = xc + xs_ref[pl.ds(8 - (CONV_WIDTH - 1 - k), tc), :] * cw_ref[k:k + 1, :]
            xs_ref[0:8, :] = x[tc - 8:, :]
            r, i, a, om, mult = _rnn_gates(xc, wr, br_ref[...], wi, bi_ref[...], sp)
            acum, bcum = _scan_fwd(a, mult * (i * xc))
            h = acum * hprev + bcum
            hs_ref[pl.ds(r0, tc), :] = h
            ge, _ = _gelu(g_ref[pl.ds(r0, tc), :])
            y_ref[pl.ds(r0, tc), :] = h * ge
            return h[tc - 1:tc, :]

        lax.fori_loop(0, nch, chunk, jnp.zeros((1, C), F32))

    col, vec, cwspec, w, own = _rnn_specs(T)
    return pl.pallas_call(
        body, name=name, grid=(RNN_BLOCKS,),
        in_specs=[col(C_XR), col(C_GR), cwspec, vec, w, vec, w, vec, vec],
        out_specs=[own, own],
        out_shape=[jax.ShapeDtypeStruct((T, D_MODEL), F32), jax.ShapeDtypeStruct((T, D_MODEL), F32)],
        scratch_shapes=[pltpu.VMEM((tc + 8, C), F32)],
        compiler_params=pltpu.CompilerParams(dimension_semantics=("parallel",)),
    )(P, P, cw, cb, wrg, brg, wig, big, lam)


def _rnn_bwd(P, hs, dy, cw, cb, wrg, brg, wig, big, lam, name):
    T = P.shape[0]
    C = RNN_BLOCK
    tc = _pick(T, (TIME_CHUNK,))
    nch = T // tc

    def body(x_ref, g_ref, hs_ref, dy_ref, cw_ref, cb_ref, wr_ref, br_ref, wi_ref, bi_ref, lam_ref,
             dx_ref, dg_ref, dcw_ref, dcb_ref, dwr_ref, dbr_ref, dwi_ref, dbi_ref, dlam_ref,
             xs_ref, hp_ref, an_ref, dn_ref):
        lam_v = lam_ref[...]
        sp = _softplus_neg(lam_v)
        wr = wr_ref[0]
        wi = wi_ref[0]
        dcw_ref[...] = jnp.zeros_like(dcw_ref)
        dcb_ref[...] = jnp.zeros_like(dcb_ref)
        dwr_ref[...] = jnp.zeros_like(dwr_ref)
        dbr_ref[...] = jnp.zeros_like(dbr_ref)
        dwi_ref[...] = jnp.zeros_like(dwi_ref)
        dbi_ref[...] = jnp.zeros_like(dbi_ref)
        dlam_ref[...] = jnp.zeros_like(dlam_ref)
        an_ref[tc:, :] = jnp.zeros((8, C), F32)
        dn_ref[tc:, :] = jnp.zeros((8, C), F32)

        def chunk(step, gnext):
            c = nch - 1 - step
            r0 = pl.multiple_of(c * tc, tc)
            p0 = pl.multiple_of(jnp.maximum(r0 - 8, 0), 8)
            live = c > 0
            x = x_ref[pl.ds(r0, tc), :]
            xs_ref[0:8, :] = jnp.where(live, x_ref[pl.ds(p0, 8), :], 0.0)
            xs_ref[8:, :] = x
            xsh = [xs_ref[pl.ds(8 - (CONV_WIDTH - 1 - k), tc), :] for k in range(CONV_WIDTH)]
            xc = cb_ref[...] + jnp.zeros((tc, C), F32)
            for k in range(CONV_WIDTH):
                xc = xc + xsh[k] * cw_ref[k:k + 1, :]
            r, i, a, om, mult = _rnn_gates(xc, wr, br_ref[...], wi, bi_ref[...], sp)
            h = hs_ref[pl.ds(r0, tc), :]
            hp_ref[0:8, :] = jnp.where(live, hs_ref[pl.ds(p0, 8), :], 0.0)
            hp_ref[8:, :] = h
            hm1 = hp_ref[pl.ds(7, tc), :]
            g = g_ref[pl.ds(r0, tc), :]
            ge, th = _gelu(g)
            dy = dy_ref[pl.ds(r0, tc), :]
            dg_ref[pl.ds(r0, tc), :] = dy * h * _gelu_grad(g, th)
            an_ref[0:tc, :] = a
            coef = an_ref[pl.ds(1, tc), :]
            ccum, bcum = _scan_bwd(coef, dy * ge)
            G = bcum + ccum * gnext
            an_ref[tc:, :] = a[0:8, :]
            da = G * hm1
            ixc = i * xc
            dmult = G * ixc
            di = G * mult * xc
            dxc = G * mult * i
            dla = da * a - dmult * (1.0 - om) / mult
            dr = dla * (-LRU_C * sp)
            dlam_ref[...] += jnp.sum(dla * r, axis=0, keepdims=True)
            dzr = dr * r * (1.0 - r)
            dzi = di * i * (1.0 - i)
            dbr_ref[...] += jnp.sum(dzr, axis=0, keepdims=True)
            dbi_ref[...] += jnp.sum(dzi, axis=0, keepdims=True)
            xb = xc.astype(BF16)
            dzrb = dzr.astype(BF16)
            dzib = dzi.astype(BF16)
            dwr_ref[0] += lax.dot_general(xb, dzrb, _DIMS["tn"], preferred_element_type=F32)
            dwi_ref[0] += lax.dot_general(xb, dzib, _DIMS["tn"], preferred_element_type=F32)
            dxc = dxc + lax.dot_general(dzrb, wr, _DIMS["nt"], preferred_element_type=F32)
            dxc = dxc + lax.dot_general(dzib, wi, _DIMS["nt"], preferred_element_type=F32)
            dcb_ref[...] += jnp.sum(dxc, axis=0, keepdims=True)
            for k in range(CONV_WIDTH):
                dcw_ref[k:k + 1, :] += jnp.sum(dxc * xsh[k], axis=0, keepdims=True)
            dn_ref[0:tc, :] = dxc
            dx = jnp.zeros((tc, C), F32)
            for k in range(CONV_WIDTH):
                dx = dx + dn_ref[pl.ds(CONV_WIDTH - 1 - k, tc), :] * cw_ref[k:k + 1, :]
            dn_ref[tc:, :] = dxc[0:8, :]
            dx_ref[pl.ds(r0, tc), :] = dx
            return G[0:1, :]

        lax.fori_loop(0, nch, chunk, jnp.zeros((1, C), F32))
        dlam_ref[...] = dlam_ref[...] * (LRU_C * _sigmoid(-lam_v))

    col, vec, cwspec, w, own = _rnn_specs(T)
    vshape = jax.ShapeDtypeStruct((1, D_MODEL), F32)
    wshape = jax.ShapeDtypeStruct((RNN_BLOCKS, C, C), F32)
    return pl.pallas_call(
        body, name=name, grid=(RNN_BLOCKS,),
        in_specs=[col(C_XR), col(C_GR), own, own, cwspec, vec, w, vec, w, vec, vec],
        out_specs=[own, own, cwspec, vec, w, vec, w, vec, vec],
        out_shape=[jax.ShapeDtypeStruct((T, D_MODEL), F32), jax.ShapeDtypeStruct((T, D_MODEL), F32),
                   jax.ShapeDtypeStruct((CONV_WIDTH, D_MODEL), F32), vshape, wshape, vshape, wshape, vshape, vshape],
        scratch_shapes=[pltpu.VMEM((tc + 8, C), F32), pltpu.VMEM((tc + 8, C), F32),
                        pltpu.VMEM((tc + 8, C), F32), pltpu.VMEM((tc + 8, C), F32)],
        compiler_params=pltpu.CompilerParams(dimension_semantics=("parallel",)),
    )(P, P, hs, dy, cw, cb, wrg, brg, wig, big, lam)


def _rope_table(T):
    half = ROT_DIM // 2
    pos = jnp.arange(T, dtype=F32)
    inv_freq = ROPE_THETA ** (-jnp.arange(0, ROT_DIM, 2, dtype=F32) / ROT_DIM)
    ang = pos[:, None] * inv_freq[None, :]
    cos, sin = jnp.cos(ang), jnp.sin(ang)
    one = jnp.ones((T, HEAD_DIM - ROT_DIM), F32)
    zero = jnp.zeros((T, HEAD_DIM - ROT_DIM), F32)
    z8 = jnp.zeros((T, half), F32)
    c = jnp.concatenate([cos, cos, one], axis=1)
    a = jnp.concatenate([-sin, z8, zero], axis=1)
    b = jnp.concatenate([z8, sin, zero], axis=1)
    return jnp.stack([jnp.tile(c, (1, 2)), jnp.tile(a, (1, 2)), jnp.tile(b, (1, 2))])


def _rope(x, tab, sign):
    W = x.shape[1]
    rep = W // 128
    c = jnp.tile(tab[0], (1, rep)) if rep > 1 else tab[0]
    a = jnp.tile(tab[1], (1, rep)) if rep > 1 else tab[1]
    b = jnp.tile(tab[2], (1, rep)) if rep > 1 else tab[2]
    return x * c + sign * (pltpu.roll(x, W - ROT_DIM // 2, 1) * a + pltpu.roll(x, ROT_DIM // 2, 1) * b)


def _swa_mask(n):
    rows = lax.broadcasted_iota(jnp.int32, (GROUP * BLOCK, 2 * BLOCK), 0) & (BLOCK - 1)
    cols = lax.broadcasted_iota(jnp.int32, (GROUP * BLOCK, 2 * BLOCK), 1)
    return (cols > rows) & (cols <= rows + BLOCK) & ((n > 0) | (cols >= BLOCK))


def _swa_probs(qg, k2, sink, valid):
    s = lax.dot_general(qg, k2, _DIMS["nt"], preferred_element_type=F32) * (HEAD_DIM ** -0.5)
    s = jnp.where(valid, s, NEG_INF)
    m = jnp.maximum(jnp.max(s, axis=1, keepdims=True), sink)
    p = jnp.exp(s - m)
    ps = jnp.exp(sink - m)
    inv = 1.0 / (jnp.sum(p, axis=1, keepdims=True) + ps)
    return p * inv, ps * inv


def _swa_specs(T):
    nb = T // BLOCK
    qspec = pl.BlockSpec((BLOCK, D_MODEL), lambda n: (n, C_Q // D_MODEL))
    cur = lambda off: pl.BlockSpec((BLOCK, KV_WIDTH), lambda n, off=off: (n, off // KV_WIDTH))
    prev = lambda off: pl.BlockSpec((BLOCK, KV_WIDTH), lambda n, off=off: (jnp.maximum(n - 1, 0), off // KV_WIDTH))
    tcur = pl.BlockSpec((3, BLOCK, 128), lambda n: (0, n, 0))
    tprev = pl.BlockSpec((3, BLOCK, 128), lambda n: (0, jnp.maximum(n - 1, 0), 0))
    sink = pl.BlockSpec((N_KV_HEADS, GROUP * BLOCK, 1), lambda n: (0, 0, 0))
    own = pl.BlockSpec((BLOCK, D_MODEL), lambda n: (n, 0))
    return nb, qspec, cur, prev, tcur, tprev, sink, own


def _stack_heads(x, hk):
    return jnp.concatenate([x[:, (hk * GROUP + g) * HEAD_DIM:(hk * GROUP + g + 1) * HEAD_DIM] for g in range(GROUP)],
                           axis=0)


def _swa_fwd(P, tab, sink_col, name):
    T = P.shape[0]
    nb, qspec, cur, prev, tcur, tprev, sink, own = _swa_specs(T)

    def body(q_ref, kc_ref, kp_ref, vc_ref, vp_ref, tc_ref, tp_ref, sk_ref, o_ref):
        n = pl.program_id(0)
        valid = _swa_mask(n)
        q = _rope(q_ref[...], tc_ref[...], 1.0).astype(BF16)
        k2 = jnp.concatenate([_rope(kp_ref[...], tp_ref[...], 1.0), _rope(kc_ref[...], tc_ref[...], 1.0)],
                             axis=0).astype(BF16)
        v2 = jnp.concatenate([vp_ref[...], vc_ref[...]], axis=0).astype(BF16)
        for hk in range(N_KV_HEADS):
            sl = slice(hk * HEAD_DIM, (hk + 1) * HEAD_DIM)
            pn, _ = _swa_probs(_stack_heads(q, hk), k2[:, sl], sk_ref[hk], valid)
            og = jnp.dot(pn.astype(BF16), v2[:, sl], preferred_element_type=F32)
            for g in range(GROUP):
                h = hk * GROUP + g
                o_ref[:, h * HEAD_DIM:(h + 1) * HEAD_DIM] = og[g * BLOCK:(g + 1) * BLOCK, :]

    return pl.pallas_call(
        body, name=name, grid=(nb,),
        in_specs=[qspec, cur(C_K), prev(C_K), cur(C_V), prev(C_V), tcur, tprev, sink],
        out_specs=own, out_shape=jax.ShapeDtypeStruct((T, D_MODEL), F32),
        compiler_params=pltpu.CompilerParams(dimension_semantics=("parallel",)),
    )(P, P, P, P, P, tab, tab, sink_col)


def _swa_bwd(P, do, tab, sink_col, name):
    T = P.shape[0]
    nb, qspec, cur, prev, tcur, tprev, sink, own = _swa_specs(T)

    def body(q_ref, kc_ref, kp_ref, vc_ref, vp_ref, do_ref, tc_ref, tp_ref, sk_ref,
             dq_ref, dk_ref, dv_ref, ds_ref):
        n = pl.program_id(0)

        @pl.when(n == 0)
        def _():
            dk_ref[...] = jnp.zeros_like(dk_ref)
            dv_ref[...] = jnp.zeros_like(dv_ref)
            ds_ref[...] = jnp.zeros_like(ds_ref)

        valid = _swa_mask(n)
        tcur_v = tc_ref[...]
        tprev_v = tp_ref[...]
        q = _rope(q_ref[...], tcur_v, 1.0).astype(BF16)
        k2 = jnp.concatenate([_rope(kp_ref[...], tprev_v, 1.0), _rope(kc_ref[...], tcur_v, 1.0)], axis=0).astype(BF16)
        v2 = jnp.concatenate([vp_ref[...], vc_ref[...]], axis=0).astype(BF16)
        dob = do_ref[...].astype(BF16)
        dq_parts = []
        dk_parts = []
        dv_parts = []
        for hk in range(N_KV_HEADS):
            sl = slice(hk * HEAD_DIM, (hk + 1) * HEAD_DIM)
            qg = _stack_heads(q, hk)
            dog = _stack_heads(dob, hk)
            pn, psn = _swa_probs(qg, k2[:, sl], sk_ref[hk], valid)
            dp = lax.dot_general(dog, v2[:, sl], _DIMS["nt"], preferred_element_type=F32)
            delta = jnp.sum(pn * dp, axis=1, keepdims=True)
            dsc = (pn * (dp - delta) * (HEAD_DIM ** -0.5)).astype(BF16)
            dsink = -psn * delta
            for g in range(GROUP):
                ds_ref[hk, g:g + 1, :] += jnp.broadcast_to(
                    jnp.sum(dsink[g * BLOCK:(g + 1) * BLOCK], axis=0, keepdims=True), (1, 128))
            dqg = jnp.dot(dsc, k2[:, sl], preferred_element_type=F32)
            dq_parts += [dqg[g * BLOCK:(g + 1) * BLOCK, :] for g in range(GROUP)]
            dk_parts.append(lax.dot_general(dsc, qg, _DIMS["tn"], preferred_element_type=F32))
            dv_parts.append(lax.dot_general(pn.astype(BF16), dog, _DIMS["tn"], preferred_element_type=F32))
        dq_ref[...] = _rope(jnp.concatenate(dq_parts, axis=1), tcur_v, -1.0)
        dk2 = jnp.concatenate(dk_parts, axis=1)
        dv2 = jnp.concatenate(dv_parts, axis=1)
        c0 = pl.multiple_of(n * BLOCK, BLOCK)
        p0 = pl.multiple_of(jnp.maximum(n - 1, 0) * BLOCK, BLOCK)
        dk_ref[pl.ds(p0, BLOCK), :] += _rope(dk2[:BLOCK], tprev_v, -1.0)
        dv_ref[pl.ds(p0, BLOCK), :] += dv2[:BLOCK]
        dk_ref[pl.ds(c0, BLOCK), :] += _rope(dk2[BLOCK:], tcur_v, -1.0)
        dv_ref[pl.ds(c0, BLOCK), :] += dv2[BLOCK:]

    full = pl.BlockSpec((T, KV_WIDTH), lambda n: (0, 0))
    return pl.pallas_call(
        body, name=name, grid=(nb,),
        in_specs=[qspec, cur(C_K), prev(C_K), cur(C_V), prev(C_V), own, tcur, tprev, sink],
        out_specs=[own, full, full, pl.BlockSpec((N_KV_HEADS, GROUP, 128), lambda n: (0, 0, 0))],
        out_shape=[jax.ShapeDtypeStruct((T, D_MODEL), F32), jax.ShapeDtypeStruct((T, KV_WIDTH), F32),
                   jax.ShapeDtypeStruct((T, KV_WIDTH), F32), jax.ShapeDtypeStruct((N_KV_HEADS, GROUP, 128), F32)],
        compiler_params=pltpu.CompilerParams(dimension_semantics=("arbitrary",)),
    )(P, P, P, P, P, do, tab, tab, sink_col)


_MW = 256
_FW = 1408


def _gate_specs(T, rows, width):
    tr = _pick(T, (rows, 256, 128, 64, 32, 16, 8))
    col = lambda off: pl.BlockSpec((tr, width), lambda i, j, off=off: (i, off // width + j))
    own = pl.BlockSpec((tr, width), lambda i, j: (i, j))
    return tr, col, own


def _merge_fwd(P, mr, ma, name):
    T = P.shape[0]
    tr, col, own = _gate_specs(T, 1024, _MW)

    def body(gr_ref, ga_ref, mr_ref, ma_ref, o_ref):
        o_ref[...] = _sigmoid(gr_ref[...]) * mr_ref[...] + _sigmoid(ga_ref[...]) * ma_ref[...]

    return pl.pallas_call(
        body, name=name, grid=(T // tr, D_MODEL // _MW), in_specs=[col(C_GRNN), col(C_GATTN), own, own],
        out_specs=own, out_shape=jax.ShapeDtypeStruct((T, D_MODEL), F32),
        compiler_params=pltpu.CompilerParams(dimension_semantics=("parallel", "parallel")),
    )(P, P, mr, ma)


def _merge_bwd(P, mr, ma, dm, name):
    T = P.shape[0]
    tr, col, own = _gate_specs(T, 512, _MW)

    def body(gr_ref, ga_ref, mr_ref, ma_ref, dm_ref, dmr_ref, dma_ref, dgr_ref, dga_ref):
        dm = dm_ref[...]
        sr = _sigmoid(gr_ref[...])
        sa = _sigmoid(ga_ref[...])
        dmr_ref[...] = dm * sr
        dma_ref[...] = dm * sa
        dgr_ref[...] = dm * mr_ref[...] * sr * (1.0 - sr)
        dga_ref[...] = dm * ma_ref[...] * sa * (1.0 - sa)

    shp = jax.ShapeDtypeStruct((T, D_MODEL), F32)
    return pl.pallas_call(
        body, name=name, grid=(T // tr, D_MODEL // _MW), in_specs=[col(C_GRNN), col(C_GATTN), own, own, own],
        out_specs=[own] * 4, out_shape=[shp] * 4,
        compiler_params=pltpu.CompilerParams(dimension_semantics=("parallel", "parallel")),
    )(P, P, mr, ma, dm)


def _swiglu_fwd(U, name):
    T = U.shape[0]
    tr, col, own = _gate_specs(T, 256, _FW)

    def body(g_ref, u_ref, o_ref):
        g = g_ref[...]
        o_ref[...] = g * _sigmoid(g) * u_ref[...]

    return pl.pallas_call(
        body, name=name, grid=(T // tr, D_FF // _FW), in_specs=[col(0), col(D_FF)],
        out_specs=own, out_shape=jax.ShapeDtypeStruct((T, D_FF), F32),
        compiler_params=pltpu.CompilerParams(dimension_semantics=("parallel", "parallel")),
    )(U, U)


def _swiglu_bwd(U, dact, name):
    T = U.shape[0]
    nf = D_FF // _FW
    tr = _pick(T, (256, 128, 64, 32, 16, 8))

    def body(g_ref, u_ref, da_ref, o_ref):
        g = g_ref[...]
        da = da_ref[...]
        s = _sigmoid(g)
        is_gate = pl.program_id(1) < nf
        o_ref[...] = jnp.where(is_gate, da * u_ref[...] * s * (1.0 + g * (1.0 - s)), da * g * s)

    return pl.pallas_call(
        body, name=name, grid=(T // tr, 2 * nf),
        in_specs=[pl.BlockSpec((tr, _FW), lambda i, j: (i, j % nf)),
                  pl.BlockSpec((tr, _FW), lambda i, j: (i, nf + j % nf)),
                  pl.BlockSpec((tr, _FW), lambda i, j: (i, j % nf))],
        out_specs=pl.BlockSpec((tr, _FW), lambda i, j: (i, j)),
        out_shape=jax.ShapeDtypeStruct((T, 2 * D_FF), F32),
        compiler_params=pltpu.CompilerParams(dimension_semantics=("parallel", "parallel")),
    )(U, U, dact)


def _cross_probs(qh, kh):
    s = lax.dot_general(qh, kh, _DIMS["nt"], preferred_element_type=F32) * (CROSS_HEAD_DIM ** -0.5)
    p = jnp.exp(s - jnp.max(s, axis=1, keepdims=True))
    return p / jnp.sum(p, axis=1, keepdims=True)


def _cross_fwd(q, kv, name):
    T = q.shape[0]
    M = kv.shape[0]
    tr = _pick(T, (ROW_TILE, 128, 64, 32, 16, 8))
    W = CROSS_HEAD_DIM

    def body(q_ref, kv_ref, o_ref):
        for h in range(CROSS_HEADS):
            qh = q_ref[:, h * W:(h + 1) * W].astype(BF16)
            kh = kv_ref[:, h * W:(h + 1) * W].astype(BF16)
            vh = kv_ref[:, D_MODEL + h * W:D_MODEL + (h + 1) * W].astype(BF16)
            pn = _cross_probs(qh, kh)
            o_ref[:, h * W:(h + 1) * W] = jnp.dot(pn.astype(BF16), vh, preferred_element_type=F32)

    row = pl.BlockSpec((tr, D_MODEL), lambda i: (i, 0))
    return pl.pallas_call(
        body, name=name, grid=(T // tr,), in_specs=[row, pl.BlockSpec((M, 2 * D_MODEL), lambda i: (0, 0))],
        out_specs=row, out_shape=jax.ShapeDtypeStruct((T, D_MODEL), F32),
        compiler_params=pltpu.CompilerParams(dimension_semantics=("parallel",)),
    )(q, kv)


def _cross_bwd(q, kv, do, name):
    T = q.shape[0]
    M = kv.shape[0]
    tr = _pick(T, (ROW_TILE, 128, 64, 32, 16, 8))
    W = CROSS_HEAD_DIM

    def body(q_ref, kv_ref, do_ref, dq_ref, dkv_ref):
        @pl.when(pl.program_id(0) == 0)
        def _():
            dkv_ref[...] = jnp.zeros_like(dkv_ref)

        for h in range(CROSS_HEADS):
            qh = q_ref[:, h * W:(h + 1) * W].astype(BF16)
            kh = kv_ref[:, h * W:(h + 1) * W].astype(BF16)
            vh = kv_ref[:, D_MODEL + h * W:D_MODEL + (h + 1) * W].astype(BF16)
            doh = do_ref[:, h * W:(h + 1) * W].astype(BF16)
            pn = _cross_probs(qh, kh)
            dp = lax.dot_general(doh, vh, _DIMS["nt"], preferred_element_type=F32)
            delta = jnp.sum(pn * dp, axis=1, keepdims=True)
            dsc = (pn * (dp - delta) * (W ** -0.5)).astype(BF16)
            dq_ref[:, h * W:(h + 1) * W] = jnp.dot(dsc, kh, preferred_element_type=F32)
            dkv_ref[:, h * W:(h + 1) * W] += lax.dot_general(dsc, qh, _DIMS["tn"], preferred_element_type=F32)
            dkv_ref[:, D_MODEL + h * W:D_MODEL + (h + 1) * W] += lax.dot_general(
                pn.astype(BF16), doh, _DIMS["tn"], preferred_element_type=F32)

    row = pl.BlockSpec((tr, D_MODEL), lambda i: (i, 0))
    full = pl.BlockSpec((M, 2 * D_MODEL), lambda i: (0, 0))
    return pl.pallas_call(
        body, name=name, grid=(T // tr,), in_specs=[row, full, row], out_specs=[row, full],
        out_shape=[jax.ShapeDtypeStruct((T, D_MODEL), F32), jax.ShapeDtypeStruct((M, 2 * D_MODEL), F32)],
        compiler_params=pltpu.CompilerParams(dimension_semantics=("arbitrary",)),
    )(q, kv, do)


def _loss_head(y, target, name):
    T, D = y.shape
    tr = _pick(T, (ROW_TILE, 128, 64, 32, 16, 8))

    def body(y_ref, t_ref, l_ref, dy_ref):
        @pl.when(pl.program_id(0) == 0)
        def _():
            l_ref[...] = jnp.zeros_like(l_ref)

        err = y_ref[...] - t_ref[...]
        dy_ref[...] = err * (1.0 / D)
        l_ref[...] += jnp.broadcast_to(0.5 * jnp.sum(jnp.mean(err * err, axis=-1, keepdims=True), axis=0, keepdims=True),
                                       (8, 128))

    row = pl.BlockSpec((tr, D), lambda i: (i, 0))
    return pl.pallas_call(
        body, name=name, grid=(T // tr,), in_specs=[row, row],
        out_specs=[pl.BlockSpec((8, 128), lambda i: (0, 0)), row],
        out_shape=[jax.ShapeDtypeStruct((8, 128), F32), jax.ShapeDtypeStruct((T, D), F32)],
        compiler_params=pltpu.CompilerParams(dimension_semantics=("arbitrary",)),
    )(y, target)


def _sum_slots(recv, name, mine=None):
    _, R, C = recv.shape
    tr = _pick(R, (ROW_TILE, 128, 64, 32, 16, 8))

    def body(*refs):
        r_ref, o_ref = refs[0], refs[-1]
        acc = r_ref[0].astype(F32)
        if mine is not None:
            acc = refs[1][...].astype(F32) + acc
        for d in range(1, N_DEV):
            acc = acc + r_ref[d].astype(F32)
        o_ref[...] = acc

    in_specs = [pl.BlockSpec((N_DEV, tr, C), lambda i: (0, i, 0))]
    args = [recv]
    if mine is not None:
        in_specs.append(pl.BlockSpec((tr, C), lambda i: (i, 0)))
        args.append(mine)
    return pl.pallas_call(
        body, name=name, grid=(R // tr,), in_specs=in_specs,
        out_specs=pl.BlockSpec((tr, C), lambda i: (i, 0)), out_shape=jax.ShapeDtypeStruct((R, C), F32),
        compiler_params=pltpu.CompilerParams(dimension_semantics=("parallel",)),
    )(*args)


def _adamw(w, g, m, v, name):
    shape = w.shape
    C = shape[-1]
    R = math.prod(shape[:-1])
    w2, g2, m2, v2 = (t.reshape(R, C) for t in (w, g, m, v))
    tr = _pick(R, (ROW_TILE, 128, 64, 32, 16, 8))

    def body(w_ref, g_ref, m_ref, v_ref, d_ref, mo_ref, vo_ref):
        gg = g_ref[...]
        mn = ADAM_B1 * m_ref[...] + (1.0 - ADAM_B1) * gg
        vn = ADAM_B2 * v_ref[...] + (1.0 - ADAM_B2) * (gg * gg)
        m_hat = mn / (1.0 - ADAM_B1 ** ADAM_STEP)
        v_hat = vn / (1.0 - ADAM_B2 ** ADAM_STEP)
        d_ref[...] = -ADAM_LR * (m_hat / (jnp.sqrt(v_hat) + ADAM_EPS) + ADAM_WD * w_ref[...])
        mo_ref[...] = mn
        vo_ref[...] = vn

    blk = pl.BlockSpec((tr, C), lambda i: (i, 0))
    shp = jax.ShapeDtypeStruct((R, C), F32)
    d, mo, vo = pl.pallas_call(
        body, name=name, grid=(R // tr,), in_specs=[blk] * 4, out_specs=[blk] * 3, out_shape=[shp] * 3,
        compiler_params=pltpu.CompilerParams(dimension_semantics=("parallel",)),
    )(w2, g2, m2, v2)
    return d.reshape(shape), mo.reshape(shape), vo.reshape(shape)


def _all_gather_many(bufs, name):
    n = len(bufs)

    def body(*refs):
        xs, outs = refs[:n], refs[n:2 * n]
        send_sems, recv_sems, local_sems = refs[2 * n:]
        x, y, c = lax.axis_index("x"), lax.axis_index("y"), lax.axis_index("c")
        me, sibling = (x, y, c), (x, y, 1 - c)
        chips = [(1 - x, y), (x, 1 - y), (1 - x, 1 - y)]

        def slot(i, px, py, pc):
            return outs[i].at[4 * px + 2 * py + pc]

        def copy(i, k, block, to, src=None):
            return pltpu.make_async_remote_copy(
                src_ref=slot(i, *block) if src is None else src, dst_ref=slot(i, *block),
                send_sem=send_sems.at[7 * i + k], recv_sem=recv_sems.at[7 * i + k],
                device_id=to, device_id_type=pl.DeviceIdType.MESH)

        mine = [pltpu.make_async_copy(xs[i], slot(i, *me), local_sems.at[i]) for i in range(n)]
        for cp in mine:
            cp.start()
        first = [copy(i, 0, me, sibling, src=xs[i]) for i in range(n)]
        for j, chip in enumerate(chips):
            first += [copy(i, 1 + j, me, (*chip, c), src=xs[i]) for i in range(n)]
        for cp in first:
            cp.start()
        passed = []
        for j, chip in enumerate(chips):
            for i in range(n):
                copy(i, 1 + j, (*chip, c), me).wait_recv()
                passed.append(copy(i, 4 + j, (*chip, c), sibling))
                passed[-1].start()
        for i in range(n):
            copy(i, 0, sibling, me).wait_recv()
        for j, chip in enumerate(chips):
            for i in range(n):
                copy(i, 4 + j, (*chip, 1 - c), me).wait_recv()
        for cp in first + passed:
            cp.wait_send()
        for cp in mine:
            cp.wait()

    hbm = pl.BlockSpec(memory_space=pl.ANY)
    return pl.pallas_call(
        body, name=name, out_shape=[jax.ShapeDtypeStruct((N_DEV,) + b.shape, b.dtype) for b in bufs],
        in_specs=[hbm] * n, out_specs=[hbm] * n,
        scratch_shapes=[pltpu.SemaphoreType.DMA((7 * n,)), pltpu.SemaphoreType.DMA((7 * n,)),
                        pltpu.SemaphoreType.DMA((n,))],
    )(*bufs)


def _all_gather(buf, name):
    return _all_gather_many([buf], name)[0]


_HBM = pl.BlockSpec(memory_space=pltpu.HBM)
_SEM = pl.BlockSpec(memory_space=pltpu.SEMAPHORE)
_EFFECT = pltpu.SideEffectType.DATAFLOW_SIDE_EFFECTING


def _peer(k):
    x, y, c = lax.axis_index("x"), lax.axis_index("y"), lax.axis_index("c")
    return x ^ ((k >> 2) & 1), y ^ ((k >> 1) & 1), c ^ (k & 1)


def _my_slot():
    return 4 * lax.axis_index("x") + 2 * lax.axis_index("y") + lax.axis_index("c")


def _split_copy(src_refs, land_refs, send_sems, recv_sems, i, k, gather):
    px, py, pc = _peer(k)
    src = src_refs[i] if gather else src_refs[i].at[4 * px + 2 * py + pc]
    return pltpu.make_async_remote_copy(
        src_ref=src, dst_ref=land_refs[i].at[_my_slot()],
        send_sem=send_sems.at[7 * i + k - 1], recv_sem=recv_sems.at[7 * i + k - 1],
        device_id=(px, py, pc), device_id_type=pl.DeviceIdType.MESH)


def _split_start(srcs, lands, gather, name, after=None):
    n = len(srcs)
    extra = [] if after is None else [after]

    def body(*refs):
        src_refs, land_refs = refs[:n], refs[n:2 * n]
        m = 2 * n + len(extra)
        send_sems, recv_sems = refs[m], refs[m + 1]
        token = refs[-1]
        for i in range(n):
            for k in range(1, N_DEV):
                _split_copy(src_refs, land_refs, send_sems, recv_sems, i, k, gather).start()
        token[...] = jnp.zeros_like(token)

    outs = pl.pallas_call(
        body, name=name,
        out_shape=(pltpu.SemaphoreType.DMA((7 * n,)), pltpu.SemaphoreType.DMA((7 * n,)),
                   *[pltpu.HBM(a.shape, a.dtype) for a in srcs], *[pltpu.HBM(a.shape, a.dtype) for a in lands],
                   jax.ShapeDtypeStruct((8, 128), F32)),
        in_specs=[_HBM] * (2 * n) + [pl.BlockSpec(memory_space=pl.ANY)] * len(extra),
        out_specs=(_SEM, _SEM, *([_HBM] * (2 * n)), pl.BlockSpec(memory_space=pltpu.VMEM)),
        input_output_aliases={i: 2 + i for i in range(2 * n)},
        compiler_params=pltpu.CompilerParams(has_side_effects=_EFFECT),
    )(*[pltpu.with_memory_space_constraint(a, pltpu.HBM) for a in list(srcs) + list(lands)], *extra)
    return outs[0], outs[1], outs[2:2 + n], outs[2 + n:2 + 2 * n], outs[-1]


def _split_wait(send_sems, recv_sems, srcs, lands, gather, after, name):
    n = len(srcs)

    def body(*refs):
        src_refs, land_refs = refs[:n], refs[n:2 * n]
        ssem, rsem = refs[2 * n], refs[2 * n + 1]
        for i in range(n):
            for k in range(1, N_DEV):
                cp = _split_copy(src_refs, land_refs, ssem, rsem, i, k, gather)
                cp.wait_send()
                cp.wait_recv()

    outs = pl.pallas_call(
        body, name=name,
        out_shape=(*[pltpu.HBM(a.shape, a.dtype) for a in srcs], *[pltpu.HBM(a.shape, a.dtype) for a in lands]),
        in_specs=[*([_HBM] * (2 * n)), _SEM, _SEM, pl.BlockSpec(memory_space=pl.ANY)],
        out_specs=tuple([_HBM] * (2 * n)),
        input_output_aliases={i: i for i in range(2 * n)},
        compiler_params=pltpu.CompilerParams(has_side_effects=_EFFECT),
    )(*srcs, *lands, send_sems, recv_sems, after)
    return outs[:n], outs[n:]


def _row(a, l):
    return a[l:l + 1]


def _layer_fwd(h, mem, W, l, tab, after=None):
    s = {}
    n = f"l{l}_"
    s["h0"] = h
    P = _matmul(h, W["w_in"][l], "nt", n + "proj", after=after)
    s["P"] = P
    sink_col = jnp.repeat(W["sinks"][l].reshape(N_KV_HEADS, GROUP), BLOCK, axis=1)[:, :, None]
    s["sink_col"] = sink_col
    y_rnn, hs = _rnn_fwd(P, W["conv_w"][l], _row(W["conv_b"], l), W["w_rg"][l], _row(W["b_rg"], l),
                         W["w_ig"][l], _row(W["b_ig"], l), _row(W["lru_lambda"], l), n + "rnn_fwd")
    y_attn = _swa_fwd(P, tab, sink_col, n + "swa_fwd")
    mr = _matmul(y_rnn, W["w_br_rnn"][l], "nn", n + "br_rnn")
    ma = _matmul(y_attn, W["w_br_attn"][l], "nn", n + "br_attn")
    merged = _merge_fwd(P, mr, ma, n + "merge_fwd")
    mix = _matmul(merged, W["w_out"][l], "nn", n + "w_out")
    h1, xh1, rs1 = _ln_fwd(h, mix, _row(W["ln1_g"], l), _row(W["ln1_b"], l), n + "ln1_fwd")
    s.update(hs=hs, y_rnn=y_rnn, y_attn=y_attn, mr=mr, ma=ma, merged=merged, xh1=xh1, rs1=rs1, h1=h1)

    qc = _matmul(h1, W["cq_w"][l], "nn", n + "cq")
    kv = _matmul(mem, W["ckv_w"][l], "nt", n + "ckv")
    oc = _cross_fwd(qc, kv, n + "cross_fwd")
    ca = _matmul(oc, W["co_w"][l], "nn", n + "co")
    h2, xh2, rs2 = _ln_fwd(h1, ca, _row(W["ln2_g"], l), _row(W["ln2_b"], l), n + "ln2_fwd")
    s.update(qc=qc, kv=kv, oc=oc, xh2=xh2, rs2=rs2, h2=h2)

    U = _matmul(h2, W["ffn_wi"][l], "nt", n + "ffn_wi")
    act = _swiglu_fwd(U, n + "swiglu_fwd")
    f = _matmul(act, W["ffn_wo"][l], "nn", n + "ffn_wo")
    h3, xh3, rs3 = _ln_fwd(h2, f, _row(W["ln3_g"], l), _row(W["ln3_b"], l), n + "ln3_fwd")
    s.update(U=U, act=act, xh3=xh3, rs3=rs3)
    return h3, s


def _layer_bwd(dh3, mem, W, l, tab, s, after=None):
    n = f"l{l}_"
    g = {}
    dz3, g["ln3_g"], g["ln3_b"] = _ln_bwd(dh3, s["xh3"], s["rs3"], _row(W["ln3_g"], l), n + "ln3_bwd", after=after)
    g["ffn_wo"] = _matmul(s["act"], dz3, "tn", n + "d_ffn_wo", out_dtype=BF16)
    dact = _matmul(dz3, W["ffn_wo"][l], "nt", n + "d_act")
    dU = _swiglu_bwd(s["U"], dact, n + "swiglu_bwd")
    g["ffn_wi"] = _matmul(dU, s["h2"], "tn", n + "d_ffn_wi", out_dtype=BF16)
    dh2 = _matmul(dU, W["ffn_wi"][l], "nn", n + "d_h2", add=dz3, add_scale=ALPHA)
    dz2, g["ln2_g"], g["ln2_b"] = _ln_bwd(dh2, s["xh2"], s["rs2"], _row(W["ln2_g"], l), n + "ln2_bwd")
    g["co_w"] = _matmul(s["oc"], dz2, "tn", n + "d_co", out_dtype=BF16)
    doc = _matmul(dz2, W["co_w"][l], "nt", n + "d_oc")
    dqc, dkv = _cross_bwd(s["qc"], s["kv"], doc, n + "cross_bwd")
    g["ckv_w"] = _matmul(dkv, mem, "tn", n + "d_ckv", out_dtype=BF16)
    g["cq_w"] = _matmul(s["h1"], dqc, "tn", n + "d_cq", out_dtype=BF16)
    dh1 = _matmul(dqc, W["cq_w"][l], "nt", n + "d_h1", add=dz2, add_scale=ALPHA)
    dz1, g["ln1_g"], g["ln1_b"] = _ln_bwd(dh1, s["xh1"], s["rs1"], _row(W["ln1_g"], l), n + "ln1_bwd")
    g["w_out"] = _matmul(s["merged"], dz1, "tn", n + "d_w_out", out_dtype=BF16)
    dmerged = _matmul(dz1, W["w_out"][l], "nt", n + "d_merged")
    dmr, dma, dgrnn, dgattn = _merge_bwd(s["P"], s["mr"], s["ma"], dmerged, n + "merge_bwd")
    g["w_br_rnn"] = _matmul(s["y_rnn"], dmr, "tn", n + "d_br_rnn", out_dtype=BF16)
    g["w_br_attn"] = _matmul(s["y_attn"], dma, "tn", n + "d_br_attn", out_dtype=BF16)
    dy_rnn = _matmul(dmr, W["w_br_rnn"][l], "nt", n + "d_y_rnn")
    dy_attn = _matmul(dma, W["w_br_attn"][l], "nt", n + "d_y_attn")
    dxr, dgr, g["conv_w"], g["conv_b"], g["w_rg"], g["b_rg"], g["w_ig"], g["b_ig"], g["lru_lambda"] = _rnn_bwd(
        s["P"], s["hs"], dy_rnn, W["conv_w"][l], _row(W["conv_b"], l), W["w_rg"][l], _row(W["b_rg"], l),
        W["w_ig"][l], _row(W["b_ig"], l), _row(W["lru_lambda"], l), n + "rnn_bwd")
    dq, dk, dv, dsk = _swa_bwd(s["P"], dy_attn, tab, s["sink_col"], n + "swa_bwd")
    g["sinks"] = dsk[:, :, 0].reshape(1, N_Q_HEADS)
    dP = jnp.concatenate([dxr, dgr, dq, dk, dv, dgrnn, dgattn], axis=1)
    g["w_in"] = _matmul(dP, s["h0"], "tn", n + "d_w_in", out_dtype=BF16)
    dh = _matmul(dP, W["w_in"][l], "nn", n + "d_h0", add=dz1, add_scale=ALPHA)
    return dh, g


def _local_step(x, mem, target, W, before_layer, on_layer_grads):
    T = x.shape[0]
    tab = _rope_table(T)
    h = x
    saved = []
    for l in range(DEPTH):
        after = before_layer(l, h)
        h, s = _layer_fwd(h, mem, W, l, tab, after)
        saved.append(s)
    lblk, dh = _loss_head(h, target, "loss_head")
    after = None
    for l in reversed(range(DEPTH)):
        dh, g = _layer_bwd(dh, mem, W, l, tab, saved[l], after)
        after = on_layer_grads(l, g, dh)
    return lblk[0, 0], dh


COL_SHARDED = ("w_in", "ckv_w", "ffn_wi")
GATE_MATS = ("w_rg", "w_ig")


def _shard_rows(shards, l):
    out = []
    for n, r in PACK_ROWS:
        a = shards[n][l].astype(BF16)
        if n in COL_SHARDED:
            a = a.T
        elif n in GATE_MATS:
            a = a.reshape(RNN_BLOCKS * RNN_BLOCK // N_DEV, RNN_BLOCK)
        out.append(a)
    return out


def _full_weight(G, name):
    if name in GATE_MATS:
        return jnp.transpose(G.reshape(N_DEV, RNN_BLOCKS, RNN_BLOCK // N_DEV, RNN_BLOCK), (1, 0, 2, 3)).reshape(
            RNN_BLOCKS, RNN_BLOCK, RNN_BLOCK)
    return G.reshape(N_DEV * G.shape[1], G.shape[2])


def _pack_blocks(g):
    parts = []
    for name, r in PACK_ROWS:
        a = g[name]
        if name in GATE_MATS:
            a = jnp.transpose(a.astype(BF16).reshape(RNN_BLOCKS, N_DEV, RNN_BLOCK // N_DEV, RNN_BLOCK), (1, 0, 2, 3))
        parts.append(a.reshape(N_DEV, r, D_MODEL))
    return jnp.concatenate(parts, axis=1)


def _pack_small(g):
    rows = [g["conv_w"]]
    for nme in SMALL_NAMES:
        a = g[nme]
        if nme == "sinks":
            a = jnp.pad(a, ((0, 0), (0, D_MODEL - N_Q_HEADS)))
        rows.append(a)
    rows.append(jnp.zeros((SMALL_ROWS - CONV_WIDTH - len(SMALL_NAMES), D_MODEL), F32))
    return jnp.concatenate(rows, axis=0)


_SHARD_SHAPES = {"w_in": (1024, 672), "w_br_rnn": (128, 1024), "w_br_attn": (128, 1024), "w_out": (128, 1024),
                 "cq_w": (128, 1024), "ckv_w": (1024, 256), "co_w": (128, 1024), "ffn_wi": (1024, 704),
                 "ffn_wo": (352, 1024), "w_rg": (4, 32, 256), "w_ig": (4, 32, 256)}

WEIGHT_NAMES = ("w_in", "conv_w", "conv_b", "w_rg", "b_rg", "w_ig", "b_ig", "lru_lambda", "w_br_rnn", "w_br_attn",
                "sinks", "w_out", "ln1_g", "ln1_b", "cq_w", "ckv_w", "co_w", "ln2_g", "ln2_b", "ffn_wi", "ffn_wo",
                "ln3_g", "ln3_b")


def kernel(x, mem, w_in, conv_w, conv_b, w_rg, b_rg, w_ig, b_ig, lru_lambda, w_br_rnn, w_br_attn, sinks, w_out, ln1_g, ln1_b, cq_w, ckv_w, co_w, ln2_g, ln2_b, ffn_wi, ffn_wo, ln3_g, ln3_b, loss_target, m_w_in, m_conv_w, m_conv_b, m_w_rg, m_b_rg, m_w_ig, m_b_ig, m_lru_lambda, m_w_br_rnn, m_w_br_attn, m_sinks, m_w_out, m_ln1_g, m_ln1_b, m_cq_w, m_ckv_w, m_co_w, m_ln2_g, m_ln2_b, m_ffn_wi, m_ffn_wo, m_ln3_g, m_ln3_b, v_w_in, v_conv_w, v_conv_b, v_w_rg, v_b_rg, v_w_ig, v_b_ig, v_lru_lambda, v_w_br_rnn, v_w_br_attn, v_sinks, v_w_out, v_ln1_g, v_ln1_b, v_cq_w, v_ckv_w, v_co_w, v_ln2_g, v_ln2_b, v_ffn_wi, v_ffn_wo, v_ln3_g, v_ln3_b):
    w = dict(w_in=w_in, conv_w=conv_w, conv_b=conv_b, w_rg=w_rg, b_rg=b_rg, w_ig=w_ig, b_ig=b_ig,
             lru_lambda=lru_lambda, w_br_rnn=w_br_rnn, w_br_attn=w_br_attn, sinks=sinks, w_out=w_out, ln1_g=ln1_g,
             ln1_b=ln1_b, cq_w=cq_w, ckv_w=ckv_w, co_w=co_w, ln2_g=ln2_g, ln2_b=ln2_b, ffn_wi=ffn_wi, ffn_wo=ffn_wo,
             ln3_g=ln3_g, ln3_b=ln3_b)
    m = dict(w_in=m_w_in, conv_w=m_conv_w, conv_b=m_conv_b, w_rg=m_w_rg, b_rg=m_b_rg, w_ig=m_w_ig, b_ig=m_b_ig,
             lru_lambda=m_lru_lambda, w_br_rnn=m_w_br_rnn, w_br_attn=m_w_br_attn, sinks=m_sinks, w_out=m_w_out,
             ln1_g=m_ln1_g, ln1_b=m_ln1_b, cq_w=m_cq_w, ckv_w=m_ckv_w, co_w=m_co_w, ln2_g=m_ln2_g, ln2_b=m_ln2_b,
             ffn_wi=m_ffn_wi, ffn_wo=m_ffn_wo, ln3_g=m_ln3_g, ln3_b=m_ln3_b)
    v = dict(w_in=v_w_in, conv_w=v_conv_w, conv_b=v_conv_b, w_rg=v_w_rg, b_rg=v_b_rg, w_ig=v_w_ig, b_ig=v_b_ig,
             lru_lambda=v_lru_lambda, w_br_rnn=v_w_br_rnn, w_br_attn=v_w_br_attn, sinks=v_sinks, w_out=v_w_out,
             ln1_g=v_ln1_g, ln1_b=v_ln1_b, cq_w=v_cq_w, ckv_w=v_ckv_w, co_w=v_co_w, ln2_g=v_ln2_g, ln2_b=v_ln2_b,
             ffn_wi=v_ffn_wi, ffn_wo=v_ffn_wo, ln3_g=v_ln3_g, ln3_b=v_ln3_b)
    my_dev = 4 * lax.axis_index("x") + 2 * lax.axis_index("y") + lax.axis_index("c")

    W = {n: [None] * DEPTH for n, _ in PACK_ROWS}
    for (n, _), G in zip(PACK_ROWS, _all_gather_many(_shard_rows(w, 0), "l0_gather_weights")):
        W[n][0] = _full_weight(G, n)
    conv_all = _all_gather(conv_w.reshape(DEPTH * CONV_WIDTH, D_MODEL // N_DEV), "gather_conv")
    W["conv_w"] = jnp.transpose(conv_all, (1, 0, 2)).reshape(DEPTH, CONV_WIDTH, D_MODEL)
    for n in SMALL_NAMES:
        W[n] = w[n]
    flying = {}
    token = conv_all
    for l in range(1, DEPTH):
        srcs = _shard_rows(w, l)
        lands = [lax.empty((N_DEV,) + a.shape, a.dtype) for a in srcs]
        flying[l] = _split_start(srcs, lands, True, f"l{l}_gather_start", after=token)
        token = flying[l][4]
    first_token = token

    def before_layer(l, h):
        if l == 0:
            return first_token
        ssem, rsem, srcs, lands, _ = flying.pop(l)
        srcs, lands = _split_wait(ssem, rsem, srcs, lands, True, h, f"l{l}_gather_wait")
        for (n, _), mine, G in zip(PACK_ROWS, srcs, lands):
            W[n][l] = _full_weight(lax.dynamic_update_index_in_dim(G, mine, my_dev, 0), n)
        return None

    summed = [None] * DEPTH
    small = [None] * DEPTH
    sent = {}

    def finish(l, after):
        ssem, rsem, send, land, mine = sent.pop(l)
        _, (recv,) = _split_wait(ssem, rsem, [send], [land], False, after, f"l{l}_exchange_wait")
        summed[l] = _sum_slots(recv, f"l{l}_sum_grads", mine=mine)

    def on_layer_grads(l, g, dh):
        small[l] = _pack_small(g)
        send = _pack_blocks(g)
        mine = lax.dynamic_index_in_dim(send, my_dev, 0, keepdims=False)
        ssem, rsem, (send,), (land,), token = _split_start([send], [jnp.zeros(send.shape, send.dtype)], False,
                                                           f"l{l}_exchange_start")
        if l + 1 in sent:
            finish(l + 1, dh)
        sent[l] = (ssem, rsem, send, land, mine)
        return token

    loss_local, dx = _local_step(x[0], mem[0], loss_target[0], W, before_layer, on_layer_grads)
    finish(0, dx)
    loss = lax.psum(loss_local, MESH_AXES)

    small_all = _all_gather(jnp.concatenate(small, axis=0), "gather_small_grads")
    small_sum = _sum_slots(small_all, "sum_small_grads").reshape(DEPTH, SMALL_ROWS, D_MODEL)
    G = jnp.stack(summed)

    grads = {}
    for n, r in PACK_ROWS:
        o, _ = PACK_OFF[n]
        blk = G[:, o:o + r, :]
        grads[n] = jnp.transpose(blk, (0, 2, 1)) if n in COL_SHARDED else blk.reshape((DEPTH,) + _SHARD_SHAPES[n])
    conv_full = small_sum[:, :CONV_WIDTH, :]
    grads["conv_w"] = lax.dynamic_slice_in_dim(conv_full, my_dev * (D_MODEL // N_DEV), D_MODEL // N_DEV, axis=2)
    for i, n in enumerate(SMALL_NAMES):
        row = small_sum[:, CONV_WIDTH + i, :]
        grads[n] = row[:, :N_Q_HEADS] if n == "sinks" else row

    deltas, new_m, new_v = {}, {}, {}
    for n in WEIGHT_NAMES:
        deltas[n], new_m[n], new_v[n] = _adamw(w[n], grads[n], m[n], v[n], "adamw_" + n)

    return (loss, dx[None], *[grads[n] for n in WEIGHT_NAMES], *[deltas[n] for n in WEIGHT_NAMES],
            *[new_m[n] for n in WEIGHT_NAMES], *[new_v[n] for n in WEIGHT_NAMES])
```

```python
import functools
import math

import jax
import jax.numpy as jnp
from jax import lax
from jax.experimental import pallas as pl
from jax.experimental.pallas import tpu as pltpu

F32 = jnp.float32
BF16 = jnp.bfloat16

D_MODEL = 1024
DEPTH = 4
N_DEV = 8
RNN_BLOCKS = 4
RNN_BLOCK = 256
CONV_WIDTH = 4
LRU_C = 8.0
HEAD_DIM = 64
N_Q_HEADS = 16
N_KV_HEADS = 2
GROUP = 8
KV_WIDTH = 128
BLOCK = 128
ROPE_THETA = 500000.0
ROT_DIM = 16
IN_COLS = 5376
CROSS_HEADS = 4
CROSS_HEAD_DIM = 256
D_FF = 2816
LN_EPS = 1e-5
ALPHA = (2 * DEPTH) ** 0.25
NEG_INF = -1e30

ADAM_LR = 0.001
ADAM_B1 = 0.9
ADAM_B2 = 0.999
ADAM_EPS = 1e-08
ADAM_WD = 0.01
ADAM_STEP = 10

C_XR, C_GR, C_Q, C_K, C_V, C_GRNN, C_GATTN = 0, 1024, 2048, 3072, 3200, 3328, 4352

TIME_CHUNK = 256
ROW_TILE = 256

MESH_AXES = ("x", "y", "c")

PACK_ROWS = (("w_in", 672), ("w_br_rnn", 128), ("w_br_attn", 128), ("w_out", 128), ("cq_w", 128),
             ("ckv_w", 256), ("co_w", 128), ("ffn_wi", 704), ("ffn_wo", 352), ("w_rg", 32), ("w_ig", 32))
SMALL_NAMES = ("conv_b", "b_rg", "b_ig", "lru_lambda", "sinks", "ln1_g", "ln1_b", "ln2_g", "ln2_b", "ln3_g", "ln3_b")
SMALL_ROWS = 16


def _pick(dim, cands):
    for c in cands:
        if dim % c == 0:
            return c
    return dim


_DIMS = {"nn": (((1,), (0,)), ((), ())), "nt": (((1,), (1,)), ((), ())), "tn": (((0,), (0,)), ((), ()))}

MATMUL_VMEM_BUDGET = 44 * 2 ** 20
MATMUL_MAX_TILE = 2048
MXU_DIM = 256
STEP_COST_BYTES = 500_000


def _tile_candidates(dim):
    c = [d for d in range(MXU_DIM, min(dim, MATMUL_MAX_TILE) + 1, MXU_DIM) if dim % d == 0]
    return c or [dim]


def _matmul_tiles(M, N, K, sa, sb, so, has_add):
    best = None
    for tk in _tile_candidates(K):
        nk = K // tk
        for tm in _tile_candidates(M):
            for tn in _tile_candidates(N):
                vmem = 2 * (tm * tk * sa + tk * tn * sb + tm * tn * so) + tm * tn * 4
                vmem += tm * tn * 4 if nk > 1 else 0
                vmem += 2 * tm * tn * 4 if has_add else 0
                vmem += (tm * tk * 2 if sa == 4 else 0) + (tk * tn * 2 if sb == 4 else 0)
                if vmem > MATMUL_VMEM_BUDGET:
                    continue
                steps = (M // tm) * (N // tn) * nk
                exposed = tm * tk * sa + tk * tn * sb + tm * tn * so
                fixed = M * N * so + steps * STEP_COST_BYTES + exposed
                a_in = M * K * sa * ((N // tn) if nk > 1 else 1) + K * N * sb * (M // tm)
                b_in = M * K * sa * (N // tn) + K * N * sb * ((M // tm) if nk > 1 else 1)
                for cost, m_outer in ((a_in + fixed, True), (b_in + fixed, False)):
                    if best is None or cost < best[0]:
                        best = (cost, tm, tn, tk, m_outer)
    return best[1:]


def _matmul(a, b, mode, name, add=None, add_scale=1.0, out_dtype=F32, after=None):
    if mode == "nn":
        (M, K), (_, N) = a.shape, b.shape
    elif mode == "nt":
        (M, K), (N, _) = a.shape, b.shape
    else:
        (K, M), (_, N) = a.shape, b.shape
    tm, tn, tk, m_outer = _matmul_tiles(M, N, K, a.dtype.itemsize, b.dtype.itemsize, jnp.dtype(out_dtype).itemsize,
                                        add is not None)
    nk = K // tk
    dims = _DIMS[mode]

    def body(*refs):
        if after is not None:
            refs = refs[:-2 - (nk > 1)] + refs[-1 - (nk > 1):]
        a_ref, b_ref = refs[0], refs[1]
        c_ref = refs[2] if add is not None else None
        o_ref = refs[3] if add is not None else refs[2]

        def finish(r):
            if add is not None:
                r = r + add_scale * c_ref[...]
            o_ref[...] = r.astype(out_dtype)

        prod = lax.dot_general(a_ref[...].astype(BF16), b_ref[...].astype(BF16), dims, preferred_element_type=F32)
        if nk == 1:
            finish(prod)
            return
        acc_ref = refs[-1]
        k = pl.program_id(2)

        @pl.when(k == 0)
        def _():
            acc_ref[...] = prod

        @pl.when(k > 0)
        def _():
            acc_ref[...] += prod

        @pl.when(k == nk - 1)
        def _():
            finish(acc_ref[...])

    ij = (lambda p, q: (p, q)) if m_outer else (lambda p, q: (q, p))
    if mode == "nn":
        a_spec = pl.BlockSpec((tm, tk), lambda p, q, k: (ij(p, q)[0], k))
        b_spec = pl.BlockSpec((tk, tn), lambda p, q, k: (k, ij(p, q)[1]))
    elif mode == "nt":
        a_spec = pl.BlockSpec((tm, tk), lambda p, q, k: (ij(p, q)[0], k))
        b_spec = pl.BlockSpec((tn, tk), lambda p, q, k: (ij(p, q)[1], k))
    else:
        a_spec = pl.BlockSpec((tk, tm), lambda p, q, k: (k, ij(p, q)[0]))
        b_spec = pl.BlockSpec((tk, tn), lambda p, q, k: (k, ij(p, q)[1]))
    o_spec = pl.BlockSpec((tm, tn), lambda p, q, k: ij(p, q))
    in_specs = [a_spec, b_spec]
    args = [a, b]
    if add is not None:
        in_specs.append(o_spec)
        args.append(add)
    if after is not None:
        in_specs.append(pl.BlockSpec(memory_space=pl.ANY))
        args.append(after)
    return pl.pallas_call(
        body, name=name, grid=(M // tm, N // tn, nk) if m_outer else (N // tn, M // tm, nk),
        in_specs=in_specs, out_specs=o_spec,
        out_shape=jax.ShapeDtypeStruct((M, N), out_dtype),
        scratch_shapes=[pltpu.VMEM((tm, tn), F32)] if nk > 1 else [],
        compiler_params=pltpu.CompilerParams(dimension_semantics=("parallel", "parallel", "arbitrary")),
    )(*args)


def _ln_fwd(h, f, g, b, name):
    T, D = h.shape
    tr = _pick(T, (ROW_TILE, 128, 64, 32, 16, 8))

    def body(h_ref, f_ref, g_ref, b_ref, o_ref, xh_ref, rs_ref):
        z = ALPHA * h_ref[...] + f_ref[...]
        mu = jnp.mean(z, axis=-1, keepdims=True)
        zc = z - mu
        var = jnp.mean(zc * zc, axis=-1, keepdims=True)
        rstd = lax.rsqrt(var + LN_EPS)
        xh = zc * rstd
        xh_ref[...] = xh
        rs_ref[...] = rstd
        o_ref[...] = xh * g_ref[...] + b_ref[...]

    row = pl.BlockSpec((tr, D), lambda i: (i, 0))
    vec = pl.BlockSpec((1, D), lambda i: (0, 0))
    return pl.pallas_call(
        body, name=name, grid=(T // tr,), in_specs=[row, row, vec, vec],
        out_specs=[row, row, pl.BlockSpec((tr, 1), lambda i: (i, 0))],
        out_shape=[jax.ShapeDtypeStruct((T, D), F32), jax.ShapeDtypeStruct((T, D), F32),
                   jax.ShapeDtypeStruct((T, 1), F32)],
        compiler_params=pltpu.CompilerParams(dimension_semantics=("parallel",)),
    )(h, f, g, b)


def _ln_bwd(dout, xh, rstd, g, name, after=None):
    T, D = dout.shape
    tr = _pick(T, (ROW_TILE, 128, 64, 32, 16, 8))

    def body(do_ref, xh_ref, rs_ref, g_ref, *rest):
        dz_ref, dg_ref, db_ref = rest[-3:]

        @pl.when(pl.program_id(0) == 0)
        def _():
            dg_ref[...] = jnp.zeros_like(dg_ref)
            db_ref[...] = jnp.zeros_like(db_ref)

        do = do_ref[...]
        xh = xh_ref[...]
        dxh = do * g_ref[...]
        m1 = jnp.mean(dxh, axis=-1, keepdims=True)
        m2 = jnp.mean(dxh * xh, axis=-1, keepdims=True)
        dz_ref[...] = rs_ref[...] * (dxh - m1 - xh * m2)
        dg_ref[...] += jnp.sum(do * xh, axis=0, keepdims=True)
        db_ref[...] += jnp.sum(do, axis=0, keepdims=True)

    row = pl.BlockSpec((tr, D), lambda i: (i, 0))
    vec = pl.BlockSpec((1, D), lambda i: (0, 0))
    in_specs = [row, row, pl.BlockSpec((tr, 1), lambda i: (i, 0)), vec]
    args = [dout, xh, rstd, g]
    if after is not None:
        in_specs.append(pl.BlockSpec(memory_space=pl.ANY))
        args.append(after)
    return pl.pallas_call(
        body, name=name, grid=(T // tr,),
        in_specs=in_specs, out_specs=[row, vec, vec],
        out_shape=[jax.ShapeDtypeStruct((T, D), F32), jax.ShapeDtypeStruct((1, D), F32),
                   jax.ShapeDtypeStruct((1, D), F32)],
        compiler_params=pltpu.CompilerParams(dimension_semantics=("arbitrary",)),
    )(*args)


_GELU_C = math.sqrt(2.0 / math.pi)


def _gelu(x):
    t = jnp.tanh(_GELU_C * (x + 0.044715 * x * x * x))
    return 0.5 * x * (1.0 + t), t


def _gelu_grad(x, t):
    return 0.5 * (1.0 + t) + 0.5 * x * (1.0 - t * t) * _GELU_C * (1.0 + 3 * 0.044715 * x * x)


def _sigmoid(x):
    return 1.0 / (1.0 + jnp.exp(-x))


def _softplus_neg(lam):
    z = jnp.exp(-jnp.abs(lam))
    u = 1.0 + z
    l1p = jnp.where(u == 1.0, z, jnp.log(u) * z / jnp.where(u == 1.0, 1.0, u - 1.0))
    return jnp.maximum(-lam, 0.0) + l1p


def _neg_expm1(x):
    series = x * (1.0 + x * 0.5 * (1.0 + x * (1.0 / 3.0) * (1.0 + x * 0.25 * (1.0 + x * 0.2))))
    return -jnp.where(x > -0.05, series, jnp.exp(x) - 1.0)


def _scan_fwd(a, b):
    n = a.shape[0]
    rows = lax.broadcasted_iota(jnp.int32, a.shape, 0)
    s = 1
    while s < n:
        keep = rows >= s
        b = jnp.where(keep, a * pltpu.roll(b, s, 0) + b, b)
        a = jnp.where(keep, a * pltpu.roll(a, s, 0), a)
        s *= 2
    return a, b


def _scan_bwd(c, b):
    n = c.shape[0]
    rows = lax.broadcasted_iota(jnp.int32, c.shape, 0)
    s = 1
    while s < n:
        keep = rows < n - s
        b = jnp.where(keep, c * pltpu.roll(b, n - s, 0) + b, b)
        c = jnp.where(keep, c * pltpu.roll(c, n - s, 0), c)
        s *= 2
    return c, b


def _rnn_gates(xc, wr, br, wi, bi, sp):
    xb = xc.astype(BF16)
    r = _sigmoid(jnp.dot(xb, wr, preferred_element_type=F32) + br)
    i = _sigmoid(jnp.dot(xb, wi, preferred_element_type=F32) + bi)
    la = -LRU_C * r * sp
    a = jnp.exp(la)
    om = _neg_expm1(2.0 * la)
    mult = jnp.sqrt(om)
    return r, i, a, om, mult


def _rnn_specs(T):
    C = RNN_BLOCK
    col = lambda off: pl.BlockSpec((T, C), lambda n, off=off: (0, off // C + n))
    vec = pl.BlockSpec((1, C), lambda n: (0, n))
    cw = pl.BlockSpec((CONV_WIDTH, C), lambda n: (0, n))
    w = pl.BlockSpec((1, C, C), lambda n: (n, 0, 0))
    own = pl.BlockSpec((T, C), lambda n: (0, n))
    return col, vec, cw, w, own


def _rnn_fwd(P, cw, cb, wrg, brg, wig, big, lam, name):
    T = P.shape[0]
    C = RNN_BLOCK
    tc = _pick(T, (TIME_CHUNK,))
    nch = T // tc

    def body(x_ref, g_ref, cw_ref, cb_ref, wr_ref, br_ref, wi_ref, bi_ref, lam_ref, y_ref, hs_ref, xs_ref):
        sp = _softplus_neg(lam_ref[...])
        wr = wr_ref[0]
        wi = wi_ref[0]
        xs_ref[0:8, :] = jnp.zeros((8, C), F32)

        def chunk(c, hprev):
            r0 = pl.multiple_of(c * tc, tc)
            x = x_ref[pl.ds(r0, tc), :]
            xs_ref[8:, :] = x
            xc = cb_ref[...] + jnp.zeros((tc, C), F32)
            for k in range(CONV_WIDTH):
                xc = xc + xs_ref[pl.ds(8 - (CONV_WIDTH - 1 - k), tc), :] * cw_ref[k:k + 1, :]
            xs_ref[0:8, :] = x[tc - 8:, :]
            r, i, a, om, mult = _rnn_gates(xc, wr, br_ref[...], wi, bi_ref[...], sp)
            acum, bcum = _scan_fwd(a, mult * (i * xc))
            h = acum * hprev + bcum
            hs_ref[pl.ds(r0, tc), :] = h
            ge, _ = _gelu(g_ref[pl.ds(r0, tc), :])
            y_ref[pl.ds(r0, tc), :] = (h * ge).astype(BF16)
            return h[tc - 1:tc, :]

        lax.fori_loop(0, nch, chunk, jnp.zeros((1, C), F32))

    col, vec, cwspec, w, own = _rnn_specs(T)
    return pl.pallas_call(
        body, name=name, grid=(RNN_BLOCKS,),
        in_specs=[col(C_XR), col(C_GR), cwspec, vec, w, vec, w, vec, vec],
        out_specs=[own, own],
        out_shape=[jax.ShapeDtypeStruct((T, D_MODEL), BF16), jax.ShapeDtypeStruct((T, D_MODEL), F32)],
        scratch_shapes=[pltpu.VMEM((tc + 8, C), F32)],
        compiler_params=pltpu.CompilerParams(dimension_semantics=("parallel",)),
    )(P, P, cw, cb, wrg, brg, wig, big, lam)


def _rnn_bwd(P, hs, dy, cw, cb, wrg, brg, wig, big, lam, name):
    T = P.shape[0]
    C = RNN_BLOCK
    tc = _pick(T, (TIME_CHUNK,))
    nch = T // tc

    def body(x_ref, g_ref, hs_ref, dy_ref, cw_ref, cb_ref, wr_ref, br_ref, wi_ref, bi_ref, lam_ref,
             dx_ref, dg_ref, dcw_ref, dcb_ref, dwr_ref, dbr_ref, dwi_ref, dbi_ref, dlam_ref,
             xs_ref, hp_ref, an_ref, dn_ref):
        lam_v = lam_ref[...]
        sp = _softplus_neg(lam_v)
        wr = wr_ref[0]
        wi = wi_ref[0]
        dcw_ref[...] = jnp.zeros_like(dcw_ref)
        dcb_ref[...] = jnp.zeros_like(dcb_ref)
        dwr_ref[...] = jnp.zeros_like(dwr_ref)
        dbr_ref[...] = jnp.zeros_like(dbr_ref)
        dwi_ref[...] = jnp.zeros_like(dwi_ref)
        dbi_ref[...] = jnp.zeros_like(dbi_ref)
        dlam_ref[...] = jnp.zeros_like(dlam_ref)
        an_ref[tc:, :] = jnp.zeros((8, C), F32)
        dn_ref[tc:, :] = jnp.zeros((8, C), F32)

        def chunk(step, gnext):
            c = nch - 1 - step
            r0 = pl.multiple_of(c * tc, tc)
            p0 = pl.multiple_of(jnp.maximum(r0 - 8, 0), 8)
            live = c > 0
            x = x_ref[pl.ds(r0, tc), :]
            xs_ref[0:8, :] = jnp.where(live, x_ref[pl.ds(p0, 8), :], 0.0)
            xs_ref[8:, :] = x
            xsh = [xs_ref[pl.ds(8 - (CONV_WIDTH - 1 - k), tc), :] for k in range(CONV_WIDTH)]
            xc = cb_ref[...] + jnp.zeros((tc, C), F32)
            for k in range(CONV_WIDTH):
                xc = xc + xsh[k] * cw_ref[k:k + 1, :]
            r, i, a, om, mult = _rnn_gates(xc, wr, br_ref[...], wi, bi_ref[...], sp)
            h = hs_ref[pl.ds(r0, tc), :]
            hp_ref[0:8, :] = jnp.where(live, hs_ref[pl.ds(p0, 8), :], 0.0)
            hp_ref[8:, :] = h
            hm1 = hp_ref[pl.ds(7, tc), :]
            g = g_ref[pl.ds(r0, tc), :]
            ge, th = _gelu(g)
            dy = dy_ref[pl.ds(r0, tc), :]
            dg_ref[pl.ds(r0, tc), :] = (dy * h * _gelu_grad(g, th)).astype(BF16)
            an_ref[0:tc, :] = a
            coef = an_ref[pl.ds(1, tc), :]
            ccum, bcum = _scan_bwd(coef, dy * ge)
            G = bcum + ccum * gnext
            an_ref[tc:, :] = a[0:8, :]
            da = G * hm1
            ixc = i * xc
            dmult = G * ixc
            di = G * mult * xc
            dxc = G * mult * i
            dla = da * a - dmult * (1.0 - om) / mult
            dr = dla * (-LRU_C * sp)
            dlam_ref[...] += jnp.sum(dla * r, axis=0, keepdims=True)
            dzr = dr * r * (1.0 - r)
            dzi = di * i * (1.0 - i)
            dbr_ref[...] += jnp.sum(dzr, axis=0, keepdims=True)
            dbi_ref[...] += jnp.sum(dzi, axis=0, keepdims=True)
            xb = xc.astype(BF16)
            dzrb = dzr.astype(BF16)
            dzib = dzi.astype(BF16)
            dwr_ref[0] += lax.dot_general(xb, dzrb, _DIMS["tn"], preferred_element_type=F32)
            dwi_ref[0] += lax.dot_general(xb, dzib, _DIMS["tn"], preferred_element_type=F32)
            dxc = dxc + lax.dot_general(dzrb, wr, _DIMS["nt"], preferred_element_type=F32)
            dxc = dxc + lax.dot_general(dzib, wi, _DIMS["nt"], preferred_element_type=F32)
            dcb_ref[...] += jnp.sum(dxc, axis=0, keepdims=True)
            for k in range(CONV_WIDTH):
                dcw_ref[k:k + 1, :] += jnp.sum(dxc * xsh[k], axis=0, keepdims=True)
            dn_ref[0:tc, :] = dxc
            dx = jnp.zeros((tc, C), F32)
            for k in range(CONV_WIDTH):
                dx = dx + dn_ref[pl.ds(CONV_WIDTH - 1 - k, tc), :] * cw_ref[k:k + 1, :]
            dn_ref[tc:, :] = dxc[0:8, :]
            dx_ref[pl.ds(r0, tc), :] = dx.astype(BF16)
            return G[0:1, :]

        lax.fori_loop(0, nch, chunk, jnp.zeros((1, C), F32))
        dlam_ref[...] = dlam_ref[...] * (LRU_C * _sigmoid(-lam_v))

    col, vec, cwspec, w, own = _rnn_specs(T)
    vshape = jax.ShapeDtypeStruct((1, D_MODEL), F32)
    wshape = jax.ShapeDtypeStruct((RNN_BLOCKS, C, C), F32)
    return pl.pallas_call(
        body, name=name, grid=(RNN_BLOCKS,),
        in_specs=[col(C_XR), col(C_GR), own, own, cwspec, vec, w, vec, w, vec, vec],
        out_specs=[own, own, cwspec, vec, w, vec, w, vec, vec],
        out_shape=[jax.ShapeDtypeStruct((T, D_MODEL), BF16), jax.ShapeDtypeStruct((T, D_MODEL), BF16),
                   jax.ShapeDtypeStruct((CONV_WIDTH, D_MODEL), F32), vshape, wshape, vshape, wshape, vshape, vshape],
        scratch_shapes=[pltpu.VMEM((tc + 8, C), F32), pltpu.VMEM((tc + 8, C), F32),
                        pltpu.VMEM((tc + 8, C), F32), pltpu.VMEM((tc + 8, C), F32)],
        compiler_params=pltpu.CompilerParams(dimension_semantics=("parallel",)),
    )(P, P, hs, dy, cw, cb, wrg, brg, wig, big, lam)


def _rope_table(T):
    half = ROT_DIM // 2
    pos = jnp.arange(T, dtype=F32)
    inv_freq = ROPE_THETA ** (-jnp.arange(0, ROT_DIM, 2, dtype=F32) / ROT_DIM)
    ang = pos[:, None] * inv_freq[None, :]
    cos, sin = jnp.cos(ang), jnp.sin(ang)
    one = jnp.ones((T, HEAD_DIM - ROT_DIM), F32)
    zero = jnp.zeros((T, HEAD_DIM - ROT_DIM), F32)
    z8 = jnp.zeros((T, half), F32)
    c = jnp.concatenate([cos, cos, one], axis=1)
    a = jnp.concatenate([-sin, z8, zero], axis=1)
    b = jnp.concatenate([z8, sin, zero], axis=1)
    return jnp.stack([jnp.tile(c, (1, 2)), jnp.tile(a, (1, 2)), jnp.tile(b, (1, 2))])


def _rope(x, tab, sign):
    W = x.shape[1]
    rep = W // 128
    c = jnp.tile(tab[0], (1, rep)) if rep > 1 else tab[0]
    a = jnp.tile(tab[1], (1, rep)) if rep > 1 else tab[1]
    b = jnp.tile(tab[2], (1, rep)) if rep > 1 else tab[2]
    return x * c + sign * (pltpu.roll(x, W - ROT_DIM // 2, 1) * a + pltpu.roll(x, ROT_DIM // 2, 1) * b)


def _swa_mask(n):
    rows = lax.broadcasted_iota(jnp.int32, (GROUP * BLOCK, 2 * BLOCK), 0) & (BLOCK - 1)
    cols = lax.broadcasted_iota(jnp.int32, (GROUP * BLOCK, 2 * BLOCK), 1)
    return (cols > rows) & (cols <= rows + BLOCK) & ((n > 0) | (cols >= BLOCK))


def _swa_probs(qg, k2, sink, valid):
    s = lax.dot_general(qg, k2, _DIMS["nt"], preferred_element_type=F32) * (HEAD_DIM ** -0.5)
    s = jnp.where(valid, s, NEG_INF)
    m = jnp.maximum(jnp.max(s, axis=1, keepdims=True), sink)
    p = jnp.exp(s - m)
    ps = jnp.exp(sink - m)
    inv = 1.0 / (jnp.sum(p, axis=1, keepdims=True) + ps)
    return p * inv, ps * inv


def _swa_specs(T):
    nb = T // BLOCK
    qspec = pl.BlockSpec((BLOCK, D_MODEL), lambda n: (n, C_Q // D_MODEL))
    cur = lambda off: pl.BlockSpec((BLOCK, KV_WIDTH), lambda n, off=off: (n, off // KV_WIDTH))
    prev = lambda off: pl.BlockSpec((BLOCK, KV_WIDTH), lambda n, off=off: (jnp.maximum(n - 1, 0), off // KV_WIDTH))
    tcur = pl.BlockSpec((3, BLOCK, 128), lambda n: (0, n, 0))
    tprev = pl.BlockSpec((3, BLOCK, 128), lambda n: (0, jnp.maximum(n - 1, 0), 0))
    sink = pl.BlockSpec((N_KV_HEADS, GROUP * BLOCK, 1), lambda n: (0, 0, 0))
    own = pl.BlockSpec((BLOCK, D_MODEL), lambda n: (n, 0))
    return nb, qspec, cur, prev, tcur, tprev, sink, own


def _stack_heads(x, hk):
    return jnp.concatenate([x[:, (hk * GROUP + g) * HEAD_DIM:(hk * GROUP + g + 1) * HEAD_DIM] for g in range(GROUP)],
                           axis=0)


def _swa_fwd(P, tab, sink_col, name):
    T = P.shape[0]
    nb, qspec, cur, prev, tcur, tprev, sink, own = _swa_specs(T)

    def body(q_ref, kc_ref, kp_ref, vc_ref, vp_ref, tc_ref, tp_ref, sk_ref, o_ref):
        n = pl.program_id(0)
        valid = _swa_mask(n)
        q = _rope(q_ref[...], tc_ref[...], 1.0).astype(BF16)
        k2 = jnp.concatenate([_rope(kp_ref[...], tp_ref[...], 1.0), _rope(kc_ref[...], tc_ref[...], 1.0)],
                             axis=0).astype(BF16)
        v2 = jnp.concatenate([vp_ref[...], vc_ref[...]], axis=0).astype(BF16)
        parts = []
        for hk in range(N_KV_HEADS):
            sl = slice(hk * HEAD_DIM, (hk + 1) * HEAD_DIM)
            pn, _ = _swa_probs(_stack_heads(q, hk), k2[:, sl], sk_ref[hk], valid)
            og = jnp.dot(pn.astype(BF16), v2[:, sl], preferred_element_type=F32)
            parts += [og[g * BLOCK:(g + 1) * BLOCK, :] for g in range(GROUP)]
        o_ref[...] = jnp.concatenate(parts, axis=1).astype(BF16)

    return pl.pallas_call(
        body, name=name, grid=(nb,),
        in_specs=[qspec, cur(C_K), prev(C_K), cur(C_V), prev(C_V), tcur, tprev, sink],
        out_specs=own, out_shape=jax.ShapeDtypeStruct((T, D_MODEL), BF16),
        compiler_params=pltpu.CompilerParams(dimension_semantics=("parallel",)),
    )(P, P, P, P, P, tab, tab, sink_col)


def _swa_bwd(P, do, tab, sink_col, name):
    T = P.shape[0]
    nb, qspec, cur, prev, tcur, tprev, sink, own = _swa_specs(T)

    def body(q_ref, kc_ref, kp_ref, vc_ref, vp_ref, do_ref, tc_ref, tp_ref, sk_ref,
             dq_ref, dk_ref, dv_ref, ds_ref):
        n = pl.program_id(0)

        @pl.when(n == 0)
        def _():
            dk_ref[...] = jnp.zeros_like(dk_ref)
            dv_ref[...] = jnp.zeros_like(dv_ref)
            ds_ref[...] = jnp.zeros_like(ds_ref)

        valid = _swa_mask(n)
        tcur_v = tc_ref[...]
        tprev_v = tp_ref[...]
        q = _rope(q_ref[...], tcur_v, 1.0).astype(BF16)
        k2 = jnp.concatenate([_rope(kp_ref[...], tprev_v, 1.0), _rope(kc_ref[...], tcur_v, 1.0)], axis=0).astype(BF16)
        v2 = jnp.concatenate([vp_ref[...], vc_ref[...]], axis=0).astype(BF16)
        dob = do_ref[...].astype(BF16)
        dq_parts = []
        dk_parts = []
        dv_parts = []
        for hk in range(N_KV_HEADS):
            sl = slice(hk * HEAD_DIM, (hk + 1) * HEAD_DIM)
            qg = _stack_heads(q, hk)
            dog = _stack_heads(dob, hk)
            pn, psn = _swa_probs(qg, k2[:, sl], sk_ref[hk], valid)
            dp = lax.dot_general(dog, v2[:, sl], _DIMS["nt"], preferred_element_type=F32)
            delta = jnp.sum(pn * dp, axis=1, keepdims=True)
            dsc = (pn * (dp - delta) * (HEAD_DIM ** -0.5)).astype(BF16)
            dsink = -psn * delta
            for g in range(GROUP):
                ds_ref[hk, g:g + 1, :] += jnp.broadcast_to(
                    jnp.sum(dsink[g * BLOCK:(g + 1) * BLOCK], axis=0, keepdims=True), (1, 128))
            dqg = jnp.dot(dsc, k2[:, sl], preferred_element_type=F32)
            dq_parts += [dqg[g * BLOCK:(g + 1) * BLOCK, :] for g in range(GROUP)]
            dk_parts.append(lax.dot_general(dsc, qg, _DIMS["tn"], preferred_element_type=F32))
            dv_parts.append(lax.dot_general(pn.astype(BF16), dog, _DIMS["tn"], preferred_element_type=F32))
        dq_ref[...] = _rope(jnp.concatenate(dq_parts, axis=1), tcur_v, -1.0).astype(BF16)
        dk2 = jnp.concatenate(dk_parts, axis=1)
        dv2 = jnp.concatenate(dv_parts, axis=1)
        c0 = pl.multiple_of(n * BLOCK, BLOCK)
        p0 = pl.multiple_of(jnp.maximum(n - 1, 0) * BLOCK, BLOCK)
        dk_ref[pl.ds(p0, BLOCK), :] += _rope(dk2[:BLOCK], tprev_v, -1.0)
        dv_ref[pl.ds(p0, BLOCK), :] += dv2[:BLOCK]
        dk_ref[pl.ds(c0, BLOCK), :] += _rope(dk2[BLOCK:], tcur_v, -1.0)
        dv_ref[pl.ds(c0, BLOCK), :] += dv2[BLOCK:]

    full = pl.BlockSpec((T, KV_WIDTH), lambda n: (0, 0))
    return pl.pallas_call(
        body, name=name, grid=(nb,),
        in_specs=[qspec, cur(C_K), prev(C_K), cur(C_V), prev(C_V), own, tcur, tprev, sink],
        out_specs=[own, full, full, pl.BlockSpec((N_KV_HEADS, GROUP, 128), lambda n: (0, 0, 0))],
        out_shape=[jax.ShapeDtypeStruct((T, D_MODEL), BF16), jax.ShapeDtypeStruct((T, KV_WIDTH), F32),
                   jax.ShapeDtypeStruct((T, KV_WIDTH), F32), jax.ShapeDtypeStruct((N_KV_HEADS, GROUP, 128), F32)],
        compiler_params=pltpu.CompilerParams(dimension_semantics=("arbitrary",)),
    )(P, P, P, P, P, do, tab, tab, sink_col)


_MW = 256
_FW = 1408


def _gate_specs(T, rows, width):
    tr = _pick(T, (rows, 256, 128, 64, 32, 16, 8))
    col = lambda off: pl.BlockSpec((tr, width), lambda i, j, off=off: (i, off // width + j))
    own = pl.BlockSpec((tr, width), lambda i, j: (i, j))
    return tr, col, own


def _merge_fwd(P, mr, ma, name):
    T = P.shape[0]
    tr, col, own = _gate_specs(T, 1024, _MW)

    def body(gr_ref, ga_ref, mr_ref, ma_ref, o_ref):
        o_ref[...] = (_sigmoid(gr_ref[...]) * mr_ref[...] + _sigmoid(ga_ref[...]) * ma_ref[...]).astype(BF16)

    return pl.pallas_call(
        body, name=name, grid=(T // tr, D_MODEL // _MW), in_specs=[col(C_GRNN), col(C_GATTN), own, own],
        out_specs=own, out_shape=jax.ShapeDtypeStruct((T, D_MODEL), BF16),
        compiler_params=pltpu.CompilerParams(dimension_semantics=("parallel", "parallel")),
    )(P, P, mr, ma)


def _merge_bwd(P, mr, ma, dm, name):
    T = P.shape[0]
    tr, col, own = _gate_specs(T, 512, _MW)

    def body(gr_ref, ga_ref, mr_ref, ma_ref, dm_ref, dmr_ref, dma_ref, dgr_ref, dga_ref):
        dm = dm_ref[...]
        sr = _sigmoid(gr_ref[...])
        sa = _sigmoid(ga_ref[...])
        dmr_ref[...] = (dm * sr).astype(BF16)
        dma_ref[...] = (dm * sa).astype(BF16)
        dgr_ref[...] = (dm * mr_ref[...] * sr * (1.0 - sr)).astype(BF16)
        dga_ref[...] = (dm * ma_ref[...] * sa * (1.0 - sa)).astype(BF16)

    shp = jax.ShapeDtypeStruct((T, D_MODEL), BF16)
    return pl.pallas_call(
        body, name=name, grid=(T // tr, D_MODEL // _MW), in_specs=[col(C_GRNN), col(C_GATTN), own, own, own],
        out_specs=[own] * 4, out_shape=[shp] * 4,
        compiler_params=pltpu.CompilerParams(dimension_semantics=("parallel", "parallel")),
    )(P, P, mr, ma, dm)


def _swiglu_fwd(U, name):
    T = U.shape[0]
    tr, col, own = _gate_specs(T, 256, _FW)

    def body(g_ref, u_ref, o_ref):
        g = g_ref[...]
        o_ref[...] = (g * _sigmoid(g) * u_ref[...]).astype(BF16)

    return pl.pallas_call(
        body, name=name, grid=(T // tr, D_FF // _FW), in_specs=[col(0), col(D_FF)],
        out_specs=own, out_shape=jax.ShapeDtypeStruct((T, D_FF), BF16),
        compiler_params=pltpu.CompilerParams(dimension_semantics=("parallel", "parallel")),
    )(U, U)


def _swiglu_bwd(U, dact, name):
    T = U.shape[0]
    nf = D_FF // _FW
    tr = _pick(T, (256, 128, 64, 32, 16, 8))

    def body(g_ref, u_ref, da_ref, o_ref):
        g = g_ref[...]
        da = da_ref[...]
        s = _sigmoid(g)
        is_gate = pl.program_id(1) < nf
        o_ref[...] = jnp.where(is_gate, da * u_ref[...] * s * (1.0 + g * (1.0 - s)), da * g * s).astype(BF16)

    return pl.pallas_call(
        body, name=name, grid=(T // tr, 2 * nf),
        in_specs=[pl.BlockSpec((tr, _FW), lambda i, j: (i, j % nf)),
                  pl.BlockSpec((tr, _FW), lambda i, j: (i, nf + j % nf)),
                  pl.BlockSpec((tr, _FW), lambda i, j: (i, j % nf))],
        out_specs=pl.BlockSpec((tr, _FW), lambda i, j: (i, j)),
        out_shape=jax.ShapeDtypeStruct((T, 2 * D_FF), BF16),
        compiler_params=pltpu.CompilerParams(dimension_semantics=("parallel", "parallel")),
    )(U, U, dact)


def _cross_probs(qh, kh):
    s = lax.dot_general(qh, kh, _DIMS["nt"], preferred_element_type=F32) * (CROSS_HEAD_DIM ** -0.5)
    p = jnp.exp(s - jnp.max(s, axis=1, keepdims=True))
    return p / jnp.sum(p, axis=1, keepdims=True)


def _cross_fwd(q, kv, name):
    T = q.shape[0]
    M = kv.shape[0]
    tr = _pick(T, (ROW_TILE, 128, 64, 32, 16, 8))
    W = CROSS_HEAD_DIM

    def body(q_ref, kv_ref, o_ref):
        for h in range(CROSS_HEADS):
            qh = q_ref[:, h * W:(h + 1) * W].astype(BF16)
            kh = kv_ref[:, h * W:(h + 1) * W].astype(BF16)
            vh = kv_ref[:, D_MODEL + h * W:D_MODEL + (h + 1) * W].astype(BF16)
            pn = _cross_probs(qh, kh)
            o_ref[:, h * W:(h + 1) * W] = jnp.dot(pn.astype(BF16), vh, preferred_element_type=F32).astype(BF16)

    row = pl.BlockSpec((tr, D_MODEL), lambda i: (i, 0))
    return pl.pallas_call(
        body, name=name, grid=(T // tr,), in_specs=[row, pl.BlockSpec((M, 2 * D_MODEL), lambda i: (0, 0))],
        out_specs=row, out_shape=jax.ShapeDtypeStruct((T, D_MODEL), BF16),
        compiler_params=pltpu.CompilerParams(dimension_semantics=("parallel",)),
    )(q, kv)


def _cross_bwd(q, kv, do, name):
    T = q.shape[0]
    M = kv.shape[0]
    tr = _pick(T, (ROW_TILE, 128, 64, 32, 16, 8))
    W = CROSS_HEAD_DIM

    def body(q_ref, kv_ref, do_ref, dq_ref, dkv_ref):
        @pl.when(pl.program_id(0) == 0)
        def _():
            dkv_ref[...] = jnp.zeros_like(dkv_ref)

        for h in range(CROSS_HEADS):
            qh = q_ref[:, h * W:(h + 1) * W].astype(BF16)
            kh = kv_ref[:, h * W:(h + 1) * W].astype(BF16)
            vh = kv_ref[:, D_MODEL + h * W:D_MODEL + (h + 1) * W].astype(BF16)
            doh = do_ref[:, h * W:(h + 1) * W].astype(BF16)
            pn = _cross_probs(qh, kh)
            dp = lax.dot_general(doh, vh, _DIMS["nt"], preferred_element_type=F32)
            delta = jnp.sum(pn * dp, axis=1, keepdims=True)
            dsc = (pn * (dp - delta) * (W ** -0.5)).astype(BF16)
            dq_ref[:, h * W:(h + 1) * W] = jnp.dot(dsc, kh, preferred_element_type=F32).astype(BF16)
            dkv_ref[:, h * W:(h + 1) * W] += lax.dot_general(dsc, qh, _DIMS["tn"], preferred_element_type=F32)
            dkv_ref[:, D_MODEL + h * W:D_MODEL + (h + 1) * W] += lax.dot_general(
                pn.astype(BF16), doh, _DIMS["tn"], preferred_element_type=F32)

    row = pl.BlockSpec((tr, D_MODEL), lambda i: (i, 0))
    full = pl.BlockSpec((M, 2 * D_MODEL), lambda i: (0, 0))
    return pl.pallas_call(
        body, name=name, grid=(T // tr,), in_specs=[row, full, row], out_specs=[row, full],
        out_shape=[jax.ShapeDtypeStruct((T, D_MODEL), BF16), jax.ShapeDtypeStruct((M, 2 * D_MODEL), F32)],
        compiler_params=pltpu.CompilerParams(dimension_semantics=("arbitrary",)),
    )(q, kv, do)


def _loss_head(y, target, name):
    T, D = y.shape
    tr = _pick(T, (ROW_TILE, 128, 64, 32, 16, 8))

    def body(y_ref, t_ref, l_ref, dy_ref):
        @pl.when(pl.program_id(0) == 0)
        def _():
            l_ref[...] = jnp.zeros_like(l_ref)

        err = y_ref[...] - t_ref[...]
        dy_ref[...] = err * (1.0 / D)
        l_ref[...] += jnp.broadcast_to(0.5 * jnp.sum(jnp.mean(err * err, axis=-1, keepdims=True), axis=0, keepdims=True),
                                       (8, 128))

    row = pl.BlockSpec((tr, D), lambda i: (i, 0))
    return pl.pallas_call(
        body, name=name, grid=(T // tr,), in_specs=[row, row],
        out_specs=[pl.BlockSpec((8, 128), lambda i: (0, 0)), row],
        out_shape=[jax.ShapeDtypeStruct((8, 128), F32), jax.ShapeDtypeStruct((T, D), F32)],
        compiler_params=pltpu.CompilerParams(dimension_semantics=("arbitrary",)),
    )(y, target)


def _sum_slots(recv, name, mine=None):
    _, R, C = recv.shape
    tr = _pick(R, (ROW_TILE, 224, 368, 128, 64, 32, 16, 8))

    def body(*refs):
        r_ref, o_ref = refs[0], refs[-1]
        acc = r_ref[0].astype(F32)
        if mine is not None:
            acc = refs[1][...].astype(F32) + acc
        for d in range(1, N_DEV):
            acc = acc + r_ref[d].astype(F32)
        o_ref[...] = acc

    in_specs = [pl.BlockSpec((N_DEV, tr, C), lambda i: (0, i, 0))]
    args = [recv]
    if mine is not None:
        in_specs.append(pl.BlockSpec((tr, C), lambda i: (i, 0)))
        args.append(mine)
    return pl.pallas_call(
        body, name=name, grid=(R // tr,), in_specs=in_specs,
        out_specs=pl.BlockSpec((tr, C), lambda i: (i, 0)), out_shape=jax.ShapeDtypeStruct((R, C), F32),
        compiler_params=pltpu.CompilerParams(dimension_semantics=("parallel",)),
    )(*args)


def _adamw(w, g, m, v, name):
    shape = w.shape
    C = shape[-1]
    R = math.prod(shape[:-1])
    w2, g2, m2, v2 = (t.reshape(R, C) for t in (w, g, m, v))
    tr = _pick(R, (ROW_TILE, 128, 64, 32, 16, 8))

    def body(w_ref, g_ref, m_ref, v_ref, d_ref, mo_ref, vo_ref):
        gg = g_ref[...]
        mn = ADAM_B1 * m_ref[...] + (1.0 - ADAM_B1) * gg
        vn = ADAM_B2 * v_ref[...] + (1.0 - ADAM_B2) * (gg * gg)
        m_hat = mn / (1.0 - ADAM_B1 ** ADAM_STEP)
        v_hat = vn / (1.0 - ADAM_B2 ** ADAM_STEP)
        d_ref[...] = -ADAM_LR * (m_hat / (jnp.sqrt(v_hat) + ADAM_EPS) + ADAM_WD * w_ref[...])
        mo_ref[...] = mn
        vo_ref[...] = vn

    blk = pl.BlockSpec((tr, C), lambda i: (i, 0))
    shp = jax.ShapeDtypeStruct((R, C), F32)
    d, mo, vo = pl.pallas_call(
        body, name=name, grid=(R // tr,), in_specs=[blk] * 4, out_specs=[blk] * 3, out_shape=[shp] * 3,
        compiler_params=pltpu.CompilerParams(dimension_semantics=("parallel",)),
    )(w2, g2, m2, v2)
    return d.reshape(shape), mo.reshape(shape), vo.reshape(shape)


def _all_gather_many(bufs, name):
    n = len(bufs)

    def body(*refs):
        xs, outs = refs[:n], refs[n:2 * n]
        send_sems, recv_sems, local_sems = refs[2 * n:]
        x, y, c = lax.axis_index("x"), lax.axis_index("y"), lax.axis_index("c")
        me, sibling = (x, y, c), (x, y, 1 - c)
        chips = [(1 - x, y), (x, 1 - y), (1 - x, 1 - y)]

        def slot(i, px, py, pc):
            return outs[i].at[4 * px + 2 * py + pc]

        def copy(i, k, block, to, src=None):
            return pltpu.make_async_remote_copy(
                src_ref=slot(i, *block) if src is None else src, dst_ref=slot(i, *block),
                send_sem=send_sems.at[7 * i + k], recv_sem=recv_sems.at[7 * i + k],
                device_id=to, device_id_type=pl.DeviceIdType.MESH)

        mine = [pltpu.make_async_copy(xs[i], slot(i, *me), local_sems.at[i]) for i in range(n)]
        for cp in mine:
            cp.start()
        first = [copy(i, 0, me, sibling, src=xs[i]) for i in range(n)]
        for j, chip in enumerate(chips):
            first += [copy(i, 1 + j, me, (*chip, c), src=xs[i]) for i in range(n)]
        for cp in first:
            cp.start()
        passed = []
        for j, chip in enumerate(chips):
            for i in range(n):
                copy(i, 1 + j, (*chip, c), me).wait_recv()
                passed.append(copy(i, 4 + j, (*chip, c), sibling))
                passed[-1].start()
        for i in range(n):
            copy(i, 0, sibling, me).wait_recv()
        for j, chip in enumerate(chips):
            for i in range(n):
                copy(i, 4 + j, (*chip, 1 - c), me).wait_recv()
        for cp in first + passed:
            cp.wait_send()
        for cp in mine:
            cp.wait()

    hbm = pl.BlockSpec(memory_space=pl.ANY)
    return pl.pallas_call(
        body, name=name, out_shape=[jax.ShapeDtypeStruct((N_DEV,) + b.shape, b.dtype) for b in bufs],
        in_specs=[hbm] * n, out_specs=[hbm] * n,
        scratch_shapes=[pltpu.SemaphoreType.DMA((7 * n,)), pltpu.SemaphoreType.DMA((7 * n,)),
                        pltpu.SemaphoreType.DMA((n,))],
    )(*bufs)


def _all_gather(buf, name):
    return _all_gather_many([buf], name)[0]


_HBM = pl.BlockSpec(memory_space=pltpu.HBM)
_SEM = pl.BlockSpec(memory_space=pltpu.SEMAPHORE)
_EFFECT = pltpu.SideEffectType.DATAFLOW_SIDE_EFFECTING


def _peer(k):
    x, y, c = lax.axis_index("x"), lax.axis_index("y"), lax.axis_index("c")
    return x ^ ((k >> 2) & 1), y ^ ((k >> 1) & 1), c ^ (k & 1)


def _my_slot():
    return 4 * lax.axis_index("x") + 2 * lax.axis_index("y") + lax.axis_index("c")


def _split_copy(src_refs, land_refs, send_sems, recv_sems, i, k, gather):
    px, py, pc = _peer(k)
    src = src_refs[i] if gather else src_refs[i].at[4 * px + 2 * py + pc]
    return pltpu.make_async_remote_copy(
        src_ref=src, dst_ref=land_refs[i].at[_my_slot()],
        send_sem=send_sems.at[7 * i + k - 1], recv_sem=recv_sems.at[7 * i + k - 1],
        device_id=(px, py, pc), device_id_type=pl.DeviceIdType.MESH)


def _split_start(srcs, lands, gather, name, after=None):
    n = len(srcs)
    extra = [] if after is None else [after]

    def body(*refs):
        src_refs, land_refs = refs[:n], refs[n:2 * n]
        m = 2 * n + len(extra)
        send_sems, recv_sems = refs[m], refs[m + 1]
        token = refs[-1]
        for i in range(n):
            for k in range(1, N_DEV):
                _split_copy(src_refs, land_refs, send_sems, recv_sems, i, k, gather).start()
        token[...] = jnp.zeros_like(token)

    outs = pl.pallas_call(
        body, name=name,
        out_shape=(pltpu.SemaphoreType.DMA((7 * n,)), pltpu.SemaphoreType.DMA((7 * n,)),
                   *[pltpu.HBM(a.shape, a.dtype) for a in srcs], *[pltpu.HBM(a.shape, a.dtype) for a in lands],
                   jax.ShapeDtypeStruct((8, 128), F32)),
        in_specs=[_HBM] * (2 * n) + [pl.BlockSpec(memory_space=pl.ANY)] * len(extra),
        out_specs=(_SEM, _SEM, *([_HBM] * (2 * n)), pl.BlockSpec(memory_space=pltpu.VMEM)),
        input_output_aliases={i: 2 + i for i in range(2 * n)},
        compiler_params=pltpu.CompilerParams(has_side_effects=_EFFECT),
    )(*[pltpu.with_memory_space_constraint(a, pltpu.HBM) for a in list(srcs) + list(lands)], *extra)
    return outs[0], outs[1], outs[2:2 + n], outs[2 + n:2 + 2 * n], outs[-1]


def _split_wait(send_sems, recv_sems, srcs, lands, gather, after, name):
    n = len(srcs)

    def body(*refs):
        src_refs, land_refs = refs[:n], refs[n:2 * n]
        ssem, rsem = refs[2 * n], refs[2 * n + 1]
        for i in range(n):
            for k in range(1, N_DEV):
                cp = _split_copy(src_refs, land_refs, ssem, rsem, i, k, gather)
                cp.wait_send()
                cp.wait_recv()

    outs = pl.pallas_call(
        body, name=name,
        out_shape=(*[pltpu.HBM(a.shape, a.dtype) for a in srcs], *[pltpu.HBM(a.shape, a.dtype) for a in lands]),
        in_specs=[*([_HBM] * (2 * n)), _SEM, _SEM, pl.BlockSpec(memory_space=pl.ANY)],
        out_specs=tuple([_HBM] * (2 * n)),
        input_output_aliases={i: i for i in range(2 * n)},
        compiler_params=pltpu.CompilerParams(has_side_effects=_EFFECT),
    )(*srcs, *lands, send_sems, recv_sems, after)
    return outs[:n], outs[n:]


def _row(a, l):
    return a[l:l + 1]


def _layer_fwd(h, mem, W, l, tab, after=None):
    s = {}
    n = f"l{l}_"
    s["h0"] = h
    P = _matmul(h, W["w_in"][l], "nt", n + "proj", after=after)
    s["P"] = P
    sink_col = jnp.repeat(W["sinks"][l].reshape(N_KV_HEADS, GROUP), BLOCK, axis=1)[:, :, None]
    s["sink_col"] = sink_col
    y_rnn, hs = _rnn_fwd(P, W["conv_w"][l], _row(W["conv_b"], l), W["w_rg"][l], _row(W["b_rg"], l),
                         W["w_ig"][l], _row(W["b_ig"], l), _row(W["lru_lambda"], l), n + "rnn_fwd")
    y_attn = _swa_fwd(P, tab, sink_col, n + "swa_fwd")
    mr = _matmul(y_rnn, W["w_br_rnn"][l], "nn", n + "br_rnn")
    ma = _matmul(y_attn, W["w_br_attn"][l], "nn", n + "br_attn")
    merged = _merge_fwd(P, mr, ma, n + "merge_fwd")
    mix = _matmul(merged, W["w_out"][l], "nn", n + "w_out")
    h1, xh1, rs1 = _ln_fwd(h, mix, _row(W["ln1_g"], l), _row(W["ln1_b"], l), n + "ln1_fwd")
    s.update(hs=hs, y_rnn=y_rnn, y_attn=y_attn, mr=mr, ma=ma, merged=merged, xh1=xh1, rs1=rs1, h1=h1)

    qc = _matmul(h1, W["cq_w"][l], "nn", n + "cq", out_dtype=BF16)
    kv = _matmul(mem, W["ckv_w"][l], "nt", n + "ckv", out_dtype=BF16)
    oc = _cross_fwd(qc, kv, n + "cross_fwd")
    ca = _matmul(oc, W["co_w"][l], "nn", n + "co")
    h2, xh2, rs2 = _ln_fwd(h1, ca, _row(W["ln2_g"], l), _row(W["ln2_b"], l), n + "ln2_fwd")
    s.update(qc=qc, kv=kv, oc=oc, xh2=xh2, rs2=rs2, h2=h2)

    U = _matmul(h2, W["ffn_wi"][l], "nt", n + "ffn_wi")
    act = _swiglu_fwd(U, n + "swiglu_fwd")
    f = _matmul(act, W["ffn_wo"][l], "nn", n + "ffn_wo")
    h3, xh3, rs3 = _ln_fwd(h2, f, _row(W["ln3_g"], l), _row(W["ln3_b"], l), n + "ln3_fwd")
    s.update(U=U, act=act, xh3=xh3, rs3=rs3)
    return h3, s


GRAD_PARTS = (("ffn_wi", "ffn_wo", "cq_w", "ckv_w", "co_w"), ("w_out", "w_br_rnn", "w_br_attn"),
              ("w_in", "w_rg", "w_ig"))


def _layer_bwd(dh3, mem, W, l, tab, s, send):
    n = f"l{l}_"
    g = {}
    dz3, g["ln3_g"], g["ln3_b"] = _ln_bwd(dh3, s["xh3"], s["rs3"], _row(W["ln3_g"], l), n + "ln3_bwd")
    g["ffn_wo"] = _matmul(s["act"], dz3, "tn", n + "d_ffn_wo", out_dtype=BF16)
    dact = _matmul(dz3, W["ffn_wo"][l], "nt", n + "d_act")
    dU = _swiglu_bwd(s["U"], dact, n + "swiglu_bwd")
    g["ffn_wi"] = _matmul(dU, s["h2"], "tn", n + "d_ffn_wi", out_dtype=BF16)
    dh2 = _matmul(dU, W["ffn_wi"][l], "nn", n + "d_h2", add=dz3, add_scale=ALPHA)
    dz2, g["ln2_g"], g["ln2_b"] = _ln_bwd(dh2, s["xh2"], s["rs2"], _row(W["ln2_g"], l), n + "ln2_bwd")
    g["co_w"] = _matmul(s["oc"], dz2, "tn", n + "d_co", out_dtype=BF16)
    doc = _matmul(dz2, W["co_w"][l], "nt", n + "d_oc", out_dtype=BF16)
    dqc, dkv = _cross_bwd(s["qc"], s["kv"], doc, n + "cross_bwd")
    g["ckv_w"] = _matmul(dkv, mem, "tn", n + "d_ckv", out_dtype=BF16)
    g["cq_w"] = _matmul(s["h1"], dqc, "tn", n + "d_cq", out_dtype=BF16)
    after = send(l, 0, g)
    dh1 = _matmul(dqc, W["cq_w"][l], "nt", n + "d_h1", add=dz2, add_scale=ALPHA, after=after)
    dz1, g["ln1_g"], g["ln1_b"] = _ln_bwd(dh1, s["xh1"], s["rs1"], _row(W["ln1_g"], l), n + "ln1_bwd")
    g["w_out"] = _matmul(s["merged"], dz1, "tn", n + "d_w_out", out_dtype=BF16)
    dmerged = _matmul(dz1, W["w_out"][l], "nt", n + "d_merged")
    dmr, dma, dgrnn, dgattn = _merge_bwd(s["P"], s["mr"], s["ma"], dmerged, n + "merge_bwd")
    g["w_br_rnn"] = _matmul(s["y_rnn"], dmr, "tn", n + "d_br_rnn", out_dtype=BF16)
    g["w_br_attn"] = _matmul(s["y_attn"], dma, "tn", n + "d_br_attn", out_dtype=BF16)
    after = send(l, 1, g)
    dy_rnn = _matmul(dmr, W["w_br_rnn"][l], "nt", n + "d_y_rnn", after=after)
    dy_attn = _matmul(dma, W["w_br_attn"][l], "nt", n + "d_y_attn", out_dtype=BF16)
    dxr, dgr, g["conv_w"], g["conv_b"], g["w_rg"], g["b_rg"], g["w_ig"], g["b_ig"], g["lru_lambda"] = _rnn_bwd(
        s["P"], s["hs"], dy_rnn, W["conv_w"][l], _row(W["conv_b"], l), W["w_rg"][l], _row(W["b_rg"], l),
        W["w_ig"][l], _row(W["b_ig"], l), _row(W["lru_lambda"], l), n + "rnn_bwd")
    dq, dk, dv, dsk = _swa_bwd(s["P"], dy_attn, tab, s["sink_col"], n + "swa_bwd")
    g["sinks"] = dsk[:, :, 0].reshape(1, N_Q_HEADS)
    dP = jnp.concatenate([dxr, dgr, dq, dk.astype(BF16), dv.astype(BF16), dgrnn, dgattn], axis=1)
    g["w_in"] = _matmul(dP, s["h0"], "tn", n + "d_w_in", out_dtype=BF16)
    after = send(l, 2, g)
    dh = _matmul(dP, W["w_in"][l], "nn", n + "d_h0", add=dz1, add_scale=ALPHA, after=after)
    return dh, g


def _local_step(x, mem, target, W, before_layer, send):
    T = x.shape[0]
    tab = _rope_table(T)
    h = x
    saved = []
    for l in range(DEPTH):
        after = before_layer(l, h)
        h, s = _layer_fwd(h, mem, W, l, tab, after)
        saved.append(s)
    lblk, dh = _loss_head(h, target, "loss_head")
    grads = [None] * DEPTH
    for l in reversed(range(DEPTH)):
        dh, grads[l] = _layer_bwd(dh, mem, W, l, tab, saved[l], send)
    return lblk[0, 0], dh, grads


COL_SHARDED = ("w_in", "ckv_w", "ffn_wi")
GATE_MATS = ("w_rg", "w_ig")


def _shard_rows(shards, l):
    out = []
    for n, r in PACK_ROWS:
        a = shards[n][l].astype(BF16)
        if n in COL_SHARDED:
            a = a.T
        elif n in GATE_MATS:
            a = a.reshape(RNN_BLOCKS * RNN_BLOCK // N_DEV, RNN_BLOCK)
        out.append(a)
    return out


def _full_weight(G, name):
    if name in GATE_MATS:
        return jnp.transpose(G.reshape(N_DEV, RNN_BLOCKS, RNN_BLOCK // N_DEV, RNN_BLOCK), (1, 0, 2, 3)).reshape(
            RNN_BLOCKS, RNN_BLOCK, RNN_BLOCK)
    return G.reshape(N_DEV * G.shape[1], G.shape[2])


SHARD_ROWS = dict(PACK_ROWS)


def _pack_blocks(g, names):
    parts = []
    for name in names:
        a = g[name]
        if name in GATE_MATS:
            a = jnp.transpose(a.astype(BF16).reshape(RNN_BLOCKS, N_DEV, RNN_BLOCK // N_DEV, RNN_BLOCK), (1, 0, 2, 3))
        parts.append(a.reshape(N_DEV, SHARD_ROWS[name], D_MODEL))
    return jnp.concatenate(parts, axis=1)


def _pack_small(g):
    rows = [g["conv_w"]]
    for nme in SMALL_NAMES:
        a = g[nme]
        if nme == "sinks":
            a = jnp.pad(a, ((0, 0), (0, D_MODEL - N_Q_HEADS)))
        rows.append(a)
    rows.append(jnp.zeros((SMALL_ROWS - CONV_WIDTH - len(SMALL_NAMES), D_MODEL), F32))
    return jnp.concatenate(rows, axis=0)


_SHARD_SHAPES = {"w_in": (1024, 672), "w_br_rnn": (128, 1024), "w_br_attn": (128, 1024), "w_out": (128, 1024),
                 "cq_w": (128, 1024), "ckv_w": (1024, 256), "co_w": (128, 1024), "ffn_wi": (1024, 704),
                 "ffn_wo": (352, 1024), "w_rg": (4, 32, 256), "w_ig": (4, 32, 256)}

WEIGHT_NAMES = ("w_in", "conv_w", "conv_b", "w_rg", "b_rg", "w_ig", "b_ig", "lru_lambda", "w_br_rnn", "w_br_attn",
                "sinks", "w_out", "ln1_g", "ln1_b", "cq_w", "ckv_w", "co_w", "ln2_g", "ln2_b", "ffn_wi", "ffn_wo",
                "ln3_g", "ln3_b")


def kernel(x, mem, w_in, conv_w, conv_b, w_rg, b_rg, w_ig, b_ig, lru_lambda, w_br_rnn, w_br_attn, sinks, w_out, ln1_g, ln1_b, cq_w, ckv_w, co_w, ln2_g, ln2_b, ffn_wi, ffn_wo, ln3_g, ln3_b, loss_target, m_w_in, m_conv_w, m_conv_b, m_w_rg, m_b_rg, m_w_ig, m_b_ig, m_lru_lambda, m_w_br_rnn, m_w_br_attn, m_sinks, m_w_out, m_ln1_g, m_ln1_b, m_cq_w, m_ckv_w, m_co_w, m_ln2_g, m_ln2_b, m_ffn_wi, m_ffn_wo, m_ln3_g, m_ln3_b, v_w_in, v_conv_w, v_conv_b, v_w_rg, v_b_rg, v_w_ig, v_b_ig, v_lru_lambda, v_w_br_rnn, v_w_br_attn, v_sinks, v_w_out, v_ln1_g, v_ln1_b, v_cq_w, v_ckv_w, v_co_w, v_ln2_g, v_ln2_b, v_ffn_wi, v_ffn_wo, v_ln3_g, v_ln3_b):
    w = dict(w_in=w_in, conv_w=conv_w, conv_b=conv_b, w_rg=w_rg, b_rg=b_rg, w_ig=w_ig, b_ig=b_ig,
             lru_lambda=lru_lambda, w_br_rnn=w_br_rnn, w_br_attn=w_br_attn, sinks=sinks, w_out=w_out, ln1_g=ln1_g,
             ln1_b=ln1_b, cq_w=cq_w, ckv_w=ckv_w, co_w=co_w, ln2_g=ln2_g, ln2_b=ln2_b, ffn_wi=ffn_wi, ffn_wo=ffn_wo,
             ln3_g=ln3_g, ln3_b=ln3_b)
    m = dict(w_in=m_w_in, conv_w=m_conv_w, conv_b=m_conv_b, w_rg=m_w_rg, b_rg=m_b_rg, w_ig=m_w_ig, b_ig=m_b_ig,
             lru_lambda=m_lru_lambda, w_br_rnn=m_w_br_rnn, w_br_attn=m_w_br_attn, sinks=m_sinks, w_out=m_w_out,
             ln1_g=m_ln1_g, ln1_b=m_ln1_b, cq_w=m_cq_w, ckv_w=m_ckv_w, co_w=m_co_w, ln2_g=m_ln2_g, ln2_b=m_ln2_b,
             ffn_wi=m_ffn_wi, ffn_wo=m_ffn_wo, ln3_g=m_ln3_g, ln3_b=m_ln3_b)
    v = dict(w_in=v_w_in, conv_w=v_conv_w, conv_b=v_conv_b, w_rg=v_w_rg, b_rg=v_b_rg, w_ig=v_w_ig, b_ig=v_b_ig,
             lru_lambda=v_lru_lambda, w_br_rnn=v_w_br_rnn, w_br_attn=v_w_br_attn, sinks=v_sinks, w_out=v_w_out,
             ln1_g=v_ln1_g, ln1_b=v_ln1_b, cq_w=v_cq_w, ckv_w=v_ckv_w, co_w=v_co_w, ln2_g=v_ln2_g, ln2_b=v_ln2_b,
             ffn_wi=v_ffn_wi, ffn_wo=v_ffn_wo, ln3_g=v_ln3_g, ln3_b=v_ln3_b)
    my_dev = 4 * lax.axis_index("x") + 2 * lax.axis_index("y") + lax.axis_index("c")

    W = {n: [None] * DEPTH for n, _ in PACK_ROWS}
    for (n, _), G in zip(PACK_ROWS, _all_gather_many(_shard_rows(w, 0), "l0_gather_weights")):
        W[n][0] = _full_weight(G, n)
    conv_all = _all_gather(conv_w.reshape(DEPTH * CONV_WIDTH, D_MODEL // N_DEV), "gather_conv")
    W["conv_w"] = jnp.transpose(conv_all, (1, 0, 2)).reshape(DEPTH, CONV_WIDTH, D_MODEL)
    for n in SMALL_NAMES:
        W[n] = w[n]
    flying = {}
    token = conv_all
    for l in range(1, DEPTH):
        srcs = _shard_rows(w, l)
        lands = [lax.empty((N_DEV,) + a.shape, a.dtype) for a in srcs]
        flying[l] = _split_start(srcs, lands, True, f"l{l}_gather_start", after=token)
        token = flying[l][4]
    first_token = token

    def before_layer(l, h):
        if l == 0:
            return first_token
        ssem, rsem, srcs, lands, _ = flying.pop(l)
        srcs, lands = _split_wait(ssem, rsem, srcs, lands, True, h, f"l{l}_gather_wait")
        for (n, _), mine, G in zip(PACK_ROWS, srcs, lands):
            W[n][l] = _full_weight(lax.dynamic_update_index_in_dim(G, mine, my_dev, 0), n)
        return None

    summed = {}
    sent = {}

    def finish(l, part, after):
        ssem, rsem, blocks, land, mine = sent.pop((l, part))
        _, (recv,) = _split_wait(ssem, rsem, [blocks], [land], False, after, f"l{l}_exchange_wait{part}")
        summed[(l, part)] = _sum_slots(recv, f"l{l}_sum_grads{part}", mine=mine)

    def send(l, part, g):
        blocks = _pack_blocks(g, GRAD_PARTS[part])
        mine = lax.dynamic_index_in_dim(blocks, my_dev, 0, keepdims=False)
        if (l + 1, part) in sent:
            finish(l + 1, part, blocks)
        ssem, rsem, (blocks,), (land,), token = _split_start([blocks], [jnp.zeros(blocks.shape, blocks.dtype)], False,
                                                             f"l{l}_exchange_start{part}")
        sent[(l, part)] = (ssem, rsem, blocks, land, mine)
        return token

    loss_local, dx, layer_grads = _local_step(x[0], mem[0], loss_target[0], W, before_layer, send)
    for part in range(len(GRAD_PARTS)):
        finish(0, part, dx)
    loss = lax.psum(loss_local, MESH_AXES)

    small_all = _all_gather(jnp.concatenate([_pack_small(g) for g in layer_grads], axis=0), "gather_small_grads")
    small_sum = _sum_slots(small_all, "sum_small_grads").reshape(DEPTH, SMALL_ROWS, D_MODEL)

    grads = {}
    for part, names in enumerate(GRAD_PARTS):
        G = jnp.stack([summed[(l, part)] for l in range(DEPTH)])
        o = 0
        for n in names:
            r = SHARD_ROWS[n]
            blk = G[:, o:o + r, :]
            grads[n] = jnp.transpose(blk, (0, 2, 1)) if n in COL_SHARDED else blk.reshape((DEPTH,) + _SHARD_SHAPES[n])
            o += r
    conv_full = small_sum[:, :CONV_WIDTH, :]
    grads["conv_w"] = lax.dynamic_slice_in_dim(conv_full, my_dev * (D_MODEL // N_DEV), D_MODEL // N_DEV, axis=2)
    for i, n in enumerate(SMALL_NAMES):
        row = small_sum[:, CONV_WIDTH + i, :]
        grads[n] = row[:, :N_Q_HEADS] if n == "sinks" else row

    deltas, new_m, new_v = {}, {}, {}
    for n in WEIGHT_NAMES:
        deltas[n], new_m[n], new_v[n] = _adamw(w[n], grads[n], m[n], v[n], "adamw_" + n)

    return (loss, dx[None], *[grads[n] for n in WEIGHT_NAMES], *[deltas[n] for n in WEIGHT_NAMES],
            *[new_m[n] for n in WEIGHT_NAMES], *[new_v[n] for n in WEIGHT_NAMES])
```

```python
import functools
import math

import jax
import jax.numpy as jnp
from jax import lax
from jax.experimental import pallas as pl
from jax.experimental.pallas import tpu as pltpu

F32 = jnp.float32
BF16 = jnp.bfloat16

D_MODEL = 1024
DEPTH = 4
N_DEV = 8
RNN_BLOCKS = 4
RNN_BLOCK = 256
CONV_WIDTH = 4
LRU_C = 8.0
HEAD_DIM = 64
N_Q_HEADS = 16
N_KV_HEADS = 2
GROUP = 8
KV_WIDTH = 128
BLOCK = 128
ROPE_THETA = 500000.0
ROT_DIM = 16
IN_COLS = 5376
CROSS_HEADS = 4
CROSS_HEAD_DIM = 256
D_FF = 2816
LN_EPS = 1e-5
ALPHA = (2 * DEPTH) ** 0.25
NEG_INF = -1e30

ADAM_LR = 0.001
ADAM_B1 = 0.9
ADAM_B2 = 0.999
ADAM_EPS = 1e-08
ADAM_WD = 0.01
ADAM_STEP = 10

C_XR, C_GR, C_Q, C_K, C_V, C_GRNN, C_GATTN = 0, 1024, 2048, 3072, 3200, 3328, 4352

TIME_CHUNK = 256
ROW_TILE = 256

MESH_AXES = ("x", "y", "c")

PACK_ROWS = (("w_in", 672), ("w_br_rnn", 128), ("w_br_attn", 128), ("w_out", 128), ("cq_w", 128),
             ("ckv_w", 256), ("co_w", 128), ("ffn_wi", 704), ("ffn_wo", 352), ("w_rg", 32), ("w_ig", 32))
SMALL_NAMES = ("conv_b", "b_rg", "b_ig", "lru_lambda", "sinks", "ln1_g", "ln1_b", "ln2_g", "ln2_b", "ln3_g", "ln3_b")
SMALL_ROWS = 16


def _pick(dim, cands):
    for c in cands:
        if dim % c == 0:
            return c
    return dim


_DIMS = {"nn": (((1,), (0,)), ((), ())), "nt": (((1,), (1,)), ((), ())), "tn": (((0,), (0,)), ((), ()))}

MATMUL_VMEM_BUDGET = 44 * 2 ** 20
MATMUL_MAX_TILE = 2048
MXU_DIM = 256
STEP_COST_BYTES = 500_000


def _tile_candidates(dim):
    c = [d for d in range(MXU_DIM, min(dim, MATMUL_MAX_TILE) + 1, MXU_DIM) if dim % d == 0]
    return c or [dim]


def _matmul_tiles(M, N, K, sa, sb, so, has_add):
    best = None
    for tk in _tile_candidates(K):
        nk = K // tk
        for tm in _tile_candidates(M):
            for tn in _tile_candidates(N):
                vmem = 2 * (tm * tk * sa + tk * tn * sb + tm * tn * so) + tm * tn * 4
                vmem += tm * tn * 4 if nk > 1 else 0
                vmem += 2 * tm * tn * 4 if has_add else 0
                vmem += (tm * tk * 2 if sa == 4 else 0) + (tk * tn * 2 if sb == 4 else 0)
                if vmem > MATMUL_VMEM_BUDGET:
                    continue
                steps = (M // tm) * (N // tn) * nk
                exposed = tm * tk * sa + tk * tn * sb + tm * tn * so
                fixed = M * N * so + steps * STEP_COST_BYTES + exposed
                a_in = M * K * sa * ((N // tn) if nk > 1 else 1) + K * N * sb * (M // tm)
                b_in = M * K * sa * (N // tn) + K * N * sb * ((M // tm) if nk > 1 else 1)
                for cost, m_outer in ((a_in + fixed, True), (b_in + fixed, False)):
                    if best is None or cost < best[0]:
                        best = (cost, tm, tn, tk, m_outer)
    return best[1:]


def _matmul(a, b, mode, name, add=None, add_scale=1.0, out_dtype=F32, after=None):
    if mode == "nn":
        (M, K), (_, N) = a.shape, b.shape
    elif mode == "nt":
        (M, K), (N, _) = a.shape, b.shape
    else:
        (K, M), (_, N) = a.shape, b.shape
    tm, tn, tk, m_outer = _matmul_tiles(M, N, K, a.dtype.itemsize, b.dtype.itemsize, jnp.dtype(out_dtype).itemsize,
                                        add is not None)
    nk = K // tk
    dims = _DIMS[mode]

    def body(*refs):
        if after is not None:
            refs = refs[:-2 - (nk > 1)] + refs[-1 - (nk > 1):]
        a_ref, b_ref = refs[0], refs[1]
        c_ref = refs[2] if add is not None else None
        o_ref = refs[3] if add is not None else refs[2]

        def finish(r):
            if add is not None:
                r = r + add_scale * c_ref[...]
            o_ref[...] = r.astype(out_dtype)

        prod = lax.dot_general(a_ref[...].astype(BF16), b_ref[...].astype(BF16), dims, preferred_element_type=F32)
        if nk == 1:
            finish(prod)
            return
        acc_ref = refs[-1]
        k = pl.program_id(2)

        @pl.when(k == 0)
        def _():
            acc_ref[...] = prod

        @pl.when(k > 0)
        def _():
            acc_ref[...] += prod

        @pl.when(k == nk - 1)
        def _():
            finish(acc_ref[...])

    ij = (lambda p, q: (p, q)) if m_outer else (lambda p, q: (q, p))
    if mode == "nn":
        a_spec = pl.BlockSpec((tm, tk), lambda p, q, k: (ij(p, q)[0], k))
        b_spec = pl.BlockSpec((tk, tn), lambda p, q, k: (k, ij(p, q)[1]))
    elif mode == "nt":
        a_spec = pl.BlockSpec((tm, tk), lambda p, q, k: (ij(p, q)[0], k))
        b_spec = pl.BlockSpec((tn, tk), lambda p, q, k: (ij(p, q)[1], k))
    else:
        a_spec = pl.BlockSpec((tk, tm), lambda p, q, k: (k, ij(p, q)[0]))
        b_spec = pl.BlockSpec((tk, tn), lambda p, q, k: (k, ij(p, q)[1]))
    o_spec = pl.BlockSpec((tm, tn), lambda p, q, k: ij(p, q))
    in_specs = [a_spec, b_spec]
    args = [a, b]
    if add is not None:
        in_specs.append(o_spec)
        args.append(add)
    if after is not None:
        in_specs.append(pl.BlockSpec(memory_space=pl.ANY))
        args.append(after)
    return pl.pallas_call(
        body, name=name, grid=(M // tm, N // tn, nk) if m_outer else (N // tn, M // tm, nk),
        in_specs=in_specs, out_specs=o_spec,
        out_shape=jax.ShapeDtypeStruct((M, N), out_dtype),
        scratch_shapes=[pltpu.VMEM((tm, tn), F32)] if nk > 1 else [],
        compiler_params=pltpu.CompilerParams(dimension_semantics=("parallel", "parallel", "arbitrary")),
    )(*args)


def _ln_fwd(h, f, g, b, name):
    T, D = h.shape
    tr = _pick(T, (ROW_TILE, 128, 64, 32, 16, 8))

    def body(h_ref, f_ref, g_ref, b_ref, o_ref, xh_ref, rs_ref):
        z = ALPHA * h_ref[...] + f_ref[...]
        mu = jnp.mean(z, axis=-1, keepdims=True)
        zc = z - mu
        var = jnp.mean(zc * zc, axis=-1, keepdims=True)
        rstd = lax.rsqrt(var + LN_EPS)
        xh = zc * rstd
        xh_ref[...] = xh
        rs_ref[...] = rstd
        o_ref[...] = xh * g_ref[...] + b_ref[...]

    row = pl.BlockSpec((tr, D), lambda i: (i, 0))
    vec = pl.BlockSpec((1, D), lambda i: (0, 0))
    return pl.pallas_call(
        body, name=name, grid=(T // tr,), in_specs=[row, row, vec, vec],
        out_specs=[row, row, pl.BlockSpec((tr, 1), lambda i: (i, 0))],
        out_shape=[jax.ShapeDtypeStruct((T, D), F32), jax.ShapeDtypeStruct((T, D), F32),
                   jax.ShapeDtypeStruct((T, 1), F32)],
        compiler_params=pltpu.CompilerParams(dimension_semantics=("parallel",)),
    )(h, f, g, b)


def _ln_bwd(dout, xh, rstd, g, name, after=None):
    T, D = dout.shape
    tr = _pick(T, (ROW_TILE, 128, 64, 32, 16, 8))

    def body(do_ref, xh_ref, rs_ref, g_ref, *rest):
        dz_ref, dg_ref, db_ref = rest[-3:]

        @pl.when(pl.program_id(0) == 0)
        def _():
            dg_ref[...] = jnp.zeros_like(dg_ref)
            db_ref[...] = jnp.zeros_like(db_ref)

        do = do_ref[...]
        xh = xh_ref[...]
        dxh = do * g_ref[...]
        m1 = jnp.mean(dxh, axis=-1, keepdims=True)
        m2 = jnp.mean(dxh * xh, axis=-1, keepdims=True)
        dz_ref[...] = rs_ref[...] * (dxh - m1 - xh * m2)
        dg_ref[...] += jnp.sum(do * xh, axis=0, keepdims=True)
        db_ref[...] += jnp.sum(do, axis=0, keepdims=True)

    row = pl.BlockSpec((tr, D), lambda i: (i, 0))
    vec = pl.BlockSpec((1, D), lambda i: (0, 0))
    in_specs = [row, row, pl.BlockSpec((tr, 1), lambda i: (i, 0)), vec]
    args = [dout, xh, rstd, g]
    if after is not None:
        in_specs.append(pl.BlockSpec(memory_space=pl.ANY))
        args.append(after)
    return pl.pallas_call(
        body, name=name, grid=(T // tr,),
        in_specs=in_specs, out_specs=[row, vec, vec],
        out_shape=[jax.ShapeDtypeStruct((T, D), F32), jax.ShapeDtypeStruct((1, D), F32),
                   jax.ShapeDtypeStruct((1, D), F32)],
        compiler_params=pltpu.CompilerParams(dimension_semantics=("arbitrary",)),
    )(*args)


_GELU_C = math.sqrt(2.0 / math.pi)


def _gelu(x):
    t = jnp.tanh(_GELU_C * (x + 0.044715 * x * x * x))
    return 0.5 * x * (1.0 + t), t


def _gelu_grad(x, t):
    return 0.5 * (1.0 + t) + 0.5 * x * (1.0 - t * t) * _GELU_C * (1.0 + 3 * 0.044715 * x * x)


def _sigmoid(x):
    return 1.0 / (1.0 + jnp.exp(-x))


def _softplus_neg(lam):
    z = jnp.exp(-jnp.abs(lam))
    u = 1.0 + z
    l1p = jnp.where(u == 1.0, z, jnp.log(u) * z / jnp.where(u == 1.0, 1.0, u - 1.0))
    return jnp.maximum(-lam, 0.0) + l1p


def _neg_expm1(x):
    series = x * (1.0 + x * 0.5 * (1.0 + x * (1.0 / 3.0) * (1.0 + x * 0.25 * (1.0 + x * 0.2))))
    return -jnp.where(x > -0.05, series, jnp.exp(x) - 1.0)


def _scan_fwd(a, b):
    n = a.shape[0]
    rows = lax.broadcasted_iota(jnp.int32, a.shape, 0)
    s = 1
    while s < n:
        keep = rows >= s
        b = jnp.where(keep, a * pltpu.roll(b, s, 0) + b, b)
        a = jnp.where(keep, a * pltpu.roll(a, s, 0), a)
        s *= 2
    return a, b


def _scan_bwd(c, b):
    n = c.shape[0]
    rows = lax.broadcasted_iota(jnp.int32, c.shape, 0)
    s = 1
    while s < n:
        keep = rows < n - s
        b = jnp.where(keep, c * pltpu.roll(b, n - s, 0) + b, b)
        c = jnp.where(keep, c * pltpu.roll(c, n - s, 0), c)
        s *= 2
    return c, b


def _rnn_gates(xc, wr, br, wi, bi, sp):
    xb = xc.astype(BF16)
    r = _sigmoid(jnp.dot(xb, wr, preferred_element_type=F32) + br)
    i = _sigmoid(jnp.dot(xb, wi, preferred_element_type=F32) + bi)
    la = -LRU_C * r * sp
    a = jnp.exp(la)
    om = _neg_expm1(2.0 * la)
    mult = jnp.sqrt(om)
    return r, i, a, om, mult


def _rnn_specs(T):
    C = RNN_BLOCK
    col = lambda off: pl.BlockSpec((T, C), lambda n, off=off: (0, off // C + n))
    vec = pl.BlockSpec((1, C), lambda n: (0, n))
    cw = pl.BlockSpec((CONV_WIDTH, C), lambda n: (0, n))
    w = pl.BlockSpec((1, C, C), lambda n: (n, 0, 0))
    own = pl.BlockSpec((T, C), lambda n: (0, n))
    return col, vec, cw, w, own


def _rnn_fwd(P, cw, cb, wrg, brg, wig, big, lam, name):
    T = P.shape[0]
    C = RNN_BLOCK
    tc = _pick(T, (TIME_CHUNK,))
    nch = T // tc

    def body(x_ref, g_ref, cw_ref, cb_ref, wr_ref, br_ref, wi_ref, bi_ref, lam_ref, y_ref, hs_ref, xs_ref):
        sp = _softplus_neg(lam_ref[...])
        wr = wr_ref[0]
        wi = wi_ref[0]
        xs_ref[0:8, :] = jnp.zeros((8, C), F32)

        def chunk(c, hprev):
            r0 = pl.multiple_of(c * tc, tc)
            x = x_ref[pl.ds(r0, tc), :]
            xs_ref[8:, :] = x
            xc = cb_ref[...] + jnp.zeros((tc, C), F32)
            for k in range(CONV_WIDTH):
                xc = xc + xs_ref[pl.ds(8 - (CONV_WIDTH - 1 - k), tc), :] * cw_ref[k:k + 1, :]
            xs_ref[0:8, :] = x[tc - 8:, :]
            r, i, a, om, mult = _rnn_gates(xc, wr, br_ref[...], wi, bi_ref[...], sp)
            acum, bcum = _scan_fwd(a, mult * (i * xc))
            h = acum * hprev + bcum
            hs_ref[pl.ds(r0, tc), :] = h
            ge, _ = _gelu(g_ref[pl.ds(r0, tc), :])
            y_ref[pl.ds(r0, tc), :] = (h * ge).astype(BF16)
            return h[tc - 1:tc, :]

        lax.fori_loop(0, nch, chunk, jnp.zeros((1, C), F32))

    col, vec, cwspec, w, own = _rnn_specs(T)
    return pl.pallas_call(
        body, name=name, grid=(RNN_BLOCKS,),
        in_specs=[col(C_XR), col(C_GR), cwspec, vec, w, vec, w, vec, vec],
        out_specs=[own, own],
        out_shape=[jax.ShapeDtypeStruct((T, D_MODEL), BF16), jax.ShapeDtypeStruct((T, D_MODEL), F32)],
        scratch_shapes=[pltpu.VMEM((tc + 8, C), F32)],
        compiler_params=pltpu.CompilerParams(dimension_semantics=("parallel",)),
    )(P, P, cw, cb, wrg, brg, wig, big, lam)


def _rnn_bwd(P, hs, dy, cw, cb, wrg, brg, wig, big, lam, name):
    T = P.shape[0]
    C = RNN_BLOCK
    tc = _pick(T, (TIME_CHUNK,))
    nch = T // tc

    def body(x_ref, g_ref, hs_ref, dy_ref, cw_ref, cb_ref, wr_ref, br_ref, wi_ref, bi_ref, lam_ref,
             dx_ref, dg_ref, dcw_ref, dcb_ref, dwr_ref, dbr_ref, dwi_ref, dbi_ref, dlam_ref,
             xs_ref, hp_ref, an_ref, dn_ref):
        lam_v = lam_ref[...]
        sp = _softplus_neg(lam_v)
        wr = wr_ref[0]
        wi = wi_ref[0]
        dcw_ref[...] = jnp.zeros_like(dcw_ref)
        dcb_ref[...] = jnp.zeros_like(dcb_ref)
        dwr_ref[...] = jnp.zeros_like(dwr_ref)
        dbr_ref[...] = jnp.zeros_like(dbr_ref)
        dwi_ref[...] = jnp.zeros_like(dwi_ref)
        dbi_ref[...] = jnp.zeros_like(dbi_ref)
        dlam_ref[...] = jnp.zeros_like(dlam_ref)
        an_ref[tc:, :] = jnp.zeros((8, C), F32)
        dn_ref[tc:, :] = jnp.zeros((8, C), F32)

        def chunk(step, gnext):
            c = nch - 1 - step
            r0 = pl.multiple_of(c * tc, tc)
            p0 = pl.multiple_of(jnp.maximum(r0 - 8, 0), 8)
            live = c > 0
            x = x_ref[pl.ds(r0, tc), :]
            xs_ref[0:8, :] = jnp.where(live, x_ref[pl.ds(p0, 8), :], 0.0)
            xs_ref[8:, :] = x
            xsh = [xs_ref[pl.ds(8 - (CONV_WIDTH - 1 - k), tc), :] for k in range(CONV_WIDTH)]
            xc = cb_ref[...] + jnp.zeros((tc, C), F32)
            for k in range(CONV_WIDTH):
                xc = xc + xsh[k] * cw_ref[k:k + 1, :]
            r, i, a, om, mult = _rnn_gates(xc, wr, br_ref[...], wi, bi_ref[...], sp)
            h = hs_ref[pl.ds(r0, tc), :]
            hp_ref[0:8, :] = jnp.where(live, hs_ref[pl.ds(p0, 8), :], 0.0)
            hp_ref[8:, :] = h
            hm1 = hp_ref[pl.ds(7, tc), :]
            g = g_ref[pl.ds(r0, tc), :]
            ge, th = _gelu(g)
            dy = dy_ref[pl.ds(r0, tc), :]
            dg_ref[pl.ds(r0, tc), :] = (dy * h * _gelu_grad(g, th)).astype(BF16)
            an_ref[0:tc, :] = a
            coef = an_ref[pl.ds(1, tc), :]
            ccum, bcum = _scan_bwd(coef, dy * ge)
            G = bcum + ccum * gnext
            an_ref[tc:, :] = a[0:8, :]
            da = G * hm1
            ixc = i * xc
            dmult = G * ixc
            di = G * mult * xc
            dxc = G * mult * i
            dla = da * a - dmult * (1.0 - om) / mult
            dr = dla * (-LRU_C * sp)
            dlam_ref[...] += jnp.sum(dla * r, axis=0, keepdims=True)
            dzr = dr * r * (1.0 - r)
            dzi = di * i * (1.0 - i)
            dbr_ref[...] += jnp.sum(dzr, axis=0, keepdims=True)
            dbi_ref[...] += jnp.sum(dzi, axis=0, keepdims=True)
            xb = xc.astype(BF16)
            dzrb = dzr.astype(BF16)
            dzib = dzi.astype(BF16)
            dwr_ref[0] += lax.dot_general(xb, dzrb, _DIMS["tn"], preferred_element_type=F32)
            dwi_ref[0] += lax.dot_general(xb, dzib, _DIMS["tn"], preferred_element_type=F32)
            dxc = dxc + lax.dot_general(dzrb, wr, _DIMS["nt"], preferred_element_type=F32)
            dxc = dxc + lax.dot_general(dzib, wi, _DIMS["nt"], preferred_element_type=F32)
            dcb_ref[...] += jnp.sum(dxc, axis=0, keepdims=True)
            for k in range(CONV_WIDTH):
                dcw_ref[k:k + 1, :] += jnp.sum(dxc * xsh[k], axis=0, keepdims=True)
            dn_ref[0:tc, :] = dxc
            dx = jnp.zeros((tc, C), F32)
            for k in range(CONV_WIDTH):
                dx = dx + dn_ref[pl.ds(CONV_WIDTH - 1 - k, tc), :] * cw_ref[k:k + 1, :]
            dn_ref[tc:, :] = dxc[0:8, :]
            dx_ref[pl.ds(r0, tc), :] = dx.astype(BF16)
            return G[0:1, :]

        lax.fori_loop(0, nch, chunk, jnp.zeros((1, C), F32))
        dlam_ref[...] = dlam_ref[...] * (LRU_C * _sigmoid(-lam_v))

    col, vec, cwspec, w, own = _rnn_specs(T)
    vshape = jax.ShapeDtypeStruct((1, D_MODEL), F32)
    wshape = jax.ShapeDtypeStruct((RNN_BLOCKS, C, C), F32)
    return pl.pallas_call(
        body, name=name, grid=(RNN_BLOCKS,),
        in_specs=[col(C_XR), col(C_GR), own, own, cwspec, vec, w, vec, w, vec, vec],
        out_specs=[own, own, cwspec, vec, w, vec, w, vec, vec],
        out_shape=[jax.ShapeDtypeStruct((T, D_MODEL), BF16), jax.ShapeDtypeStruct((T, D_MODEL), BF16),
                   jax.ShapeDtypeStruct((CONV_WIDTH, D_MODEL), F32), vshape, wshape, vshape, wshape, vshape, vshape],
        scratch_shapes=[pltpu.VMEM((tc + 8, C), F32), pltpu.VMEM((tc + 8, C), F32),
                        pltpu.VMEM((tc + 8, C), F32), pltpu.VMEM((tc + 8, C), F32)],
        compiler_params=pltpu.CompilerParams(dimension_semantics=("parallel",)),
    )(P, P, hs, dy, cw, cb, wrg, brg, wig, big, lam)


def _rope_table(T):
    half = ROT_DIM // 2
    pos = jnp.arange(T, dtype=F32)
    inv_freq = ROPE_THETA ** (-jnp.arange(0, ROT_DIM, 2, dtype=F32) / ROT_DIM)
    ang = pos[:, None] * inv_freq[None, :]
    cos, sin = jnp.cos(ang), jnp.sin(ang)
    one = jnp.ones((T, HEAD_DIM - ROT_DIM), F32)
    zero = jnp.zeros((T, HEAD_DIM - ROT_DIM), F32)
    z8 = jnp.zeros((T, half), F32)
    c = jnp.concatenate([cos, cos, one], axis=1)
    a = jnp.concatenate([-sin, z8, zero], axis=1)
    b = jnp.concatenate([z8, sin, zero], axis=1)
    return jnp.stack([jnp.tile(c, (1, 2)), jnp.tile(a, (1, 2)), jnp.tile(b, (1, 2))])


def _rope(x, tab, sign):
    W = x.shape[1]
    rep = W // 128
    c = jnp.tile(tab[0], (1, rep)) if rep > 1 else tab[0]
    a = jnp.tile(tab[1], (1, rep)) if rep > 1 else tab[1]
    b = jnp.tile(tab[2], (1, rep)) if rep > 1 else tab[2]
    return x * c + sign * (pltpu.roll(x, W - ROT_DIM // 2, 1) * a + pltpu.roll(x, ROT_DIM // 2, 1) * b)


def _swa_mask(n):
    rows = lax.broadcasted_iota(jnp.int32, (GROUP * BLOCK, 2 * BLOCK), 0) & (BLOCK - 1)
    cols = lax.broadcasted_iota(jnp.int32, (GROUP * BLOCK, 2 * BLOCK), 1)
    return (cols > rows) & (cols <= rows + BLOCK) & ((n > 0) | (cols >= BLOCK))


def _swa_probs(qg, k2, sink, valid):
    s = lax.dot_general(qg, k2, _DIMS["nt"], preferred_element_type=F32) * (HEAD_DIM ** -0.5)
    s = jnp.where(valid, s, NEG_INF)
    m = jnp.maximum(jnp.max(s, axis=1, keepdims=True), sink)
    p = jnp.exp(s - m)
    ps = jnp.exp(sink - m)
    inv = 1.0 / (jnp.sum(p, axis=1, keepdims=True) + ps)
    return p * inv, ps * inv


def _swa_specs(T):
    nb = T // BLOCK
    qspec = pl.BlockSpec((BLOCK, D_MODEL), lambda n: (n, C_Q // D_MODEL))
    cur = lambda off: pl.BlockSpec((BLOCK, KV_WIDTH), lambda n, off=off: (n, off // KV_WIDTH))
    prev = lambda off: pl.BlockSpec((BLOCK, KV_WIDTH), lambda n, off=off: (jnp.maximum(n - 1, 0), off // KV_WIDTH))
    tcur = pl.BlockSpec((3, BLOCK, 128), lambda n: (0, n, 0))
    tprev = pl.BlockSpec((3, BLOCK, 128), lambda n: (0, jnp.maximum(n - 1, 0), 0))
    sink = pl.BlockSpec((N_KV_HEADS, GROUP * BLOCK, 1), lambda n: (0, 0, 0))
    own = pl.BlockSpec((BLOCK, D_MODEL), lambda n: (n, 0))
    return nb, qspec, cur, prev, tcur, tprev, sink, own


def _stack_heads(x, hk):
    return jnp.concatenate([x[:, (hk * GROUP + g) * HEAD_DIM:(hk * GROUP + g + 1) * HEAD_DIM] for g in range(GROUP)],
                           axis=0)


def _swa_fwd(P, tab, sink_col, name):
    T = P.shape[0]
    nb, qspec, cur, prev, tcur, tprev, sink, own = _swa_specs(T)

    def body(q_ref, kc_ref, kp_ref, vc_ref, vp_ref, tc_ref, tp_ref, sk_ref, o_ref):
        n = pl.program_id(0)
        valid = _swa_mask(n)
        q = _rope(q_ref[...], tc_ref[...], 1.0).astype(BF16)
        k2 = jnp.concatenate([_rope(kp_ref[...], tp_ref[...], 1.0), _rope(kc_ref[...], tc_ref[...], 1.0)],
                             axis=0).astype(BF16)
        v2 = jnp.concatenate([vp_ref[...], vc_ref[...]], axis=0).astype(BF16)
        parts = []
        for hk in range(N_KV_HEADS):
            sl = slice(hk * HEAD_DIM, (hk + 1) * HEAD_DIM)
            pn, _ = _swa_probs(_stack_heads(q, hk), k2[:, sl], sk_ref[hk], valid)
            og = jnp.dot(pn.astype(BF16), v2[:, sl], preferred_element_type=F32)
            parts += [og[g * BLOCK:(g + 1) * BLOCK, :] for g in range(GROUP)]
        o_ref[...] = jnp.concatenate(parts, axis=1).astype(BF16)

    return pl.pallas_call(
        body, name=name, grid=(nb,),
        in_specs=[qspec, cur(C_K), prev(C_K), cur(C_V), prev(C_V), tcur, tprev, sink],
        out_specs=own, out_shape=jax.ShapeDtypeStruct((T, D_MODEL), BF16),
        compiler_params=pltpu.CompilerParams(dimension_semantics=("parallel",)),
    )(P, P, P, P, P, tab, tab, sink_col)


def _swa_bwd(P, do, tab, sink_col, name):
    T = P.shape[0]
    nb, qspec, cur, prev, tcur, tprev, sink, own = _swa_specs(T)

    def body(q_ref, kc_ref, kp_ref, vc_ref, vp_ref, do_ref, tc_ref, tp_ref, sk_ref,
             dq_ref, dk_ref, dv_ref, ds_ref):
        n = pl.program_id(0)

        @pl.when(n == 0)
        def _():
            dk_ref[...] = jnp.zeros_like(dk_ref)
            dv_ref[...] = jnp.zeros_like(dv_ref)
            ds_ref[...] = jnp.zeros_like(ds_ref)

        valid = _swa_mask(n)
        tcur_v = tc_ref[...]
        tprev_v = tp_ref[...]
        q = _rope(q_ref[...], tcur_v, 1.0).astype(BF16)
        k2 = jnp.concatenate([_rope(kp_ref[...], tprev_v, 1.0), _rope(kc_ref[...], tcur_v, 1.0)], axis=0).astype(BF16)
        v2 = jnp.concatenate([vp_ref[...], vc_ref[...]], axis=0).astype(BF16)
        dob = do_ref[...].astype(BF16)
        dq_parts = []
        dk_parts = []
        dv_parts = []
        for hk in range(N_KV_HEADS):
            sl = slice(hk * HEAD_DIM, (hk + 1) * HEAD_DIM)
            qg = _stack_heads(q, hk)
            dog = _stack_heads(dob, hk)
            pn, psn = _swa_probs(qg, k2[:, sl], sk_ref[hk], valid)
            dp = lax.dot_general(dog, v2[:, sl], _DIMS["nt"], preferred_element_type=F32)
            delta = jnp.sum(pn * dp, axis=1, keepdims=True)
            dsc = (pn * (dp - delta) * (HEAD_DIM ** -0.5)).astype(BF16)
            dsink = -psn * delta
            for g in range(GROUP):
                ds_ref[hk, g:g + 1, :] += jnp.broadcast_to(
                    jnp.sum(dsink[g * BLOCK:(g + 1) * BLOCK], axis=0, keepdims=True), (1, 128))
            dqg = jnp.dot(dsc, k2[:, sl], preferred_element_type=F32)
            dq_parts += [dqg[g * BLOCK:(g + 1) * BLOCK, :] for g in range(GROUP)]
            dk_parts.append(lax.dot_general(dsc, qg, _DIMS["tn"], preferred_element_type=F32))
            dv_parts.append(lax.dot_general(pn.astype(BF16), dog, _DIMS["tn"], preferred_element_type=F32))
        dq_ref[...] = _rope(jnp.concatenate(dq_parts, axis=1), tcur_v, -1.0).astype(BF16)
        dk2 = jnp.concatenate(dk_parts, axis=1)
        dv2 = jnp.concatenate(dv_parts, axis=1)
        c0 = pl.multiple_of(n * BLOCK, BLOCK)
        p0 = pl.multiple_of(jnp.maximum(n - 1, 0) * BLOCK, BLOCK)
        dk_ref[pl.ds(p0, BLOCK), :] += _rope(dk2[:BLOCK], tprev_v, -1.0)
        dv_ref[pl.ds(p0, BLOCK), :] += dv2[:BLOCK]
        dk_ref[pl.ds(c0, BLOCK), :] += _rope(dk2[BLOCK:], tcur_v, -1.0)
        dv_ref[pl.ds(c0, BLOCK), :] += dv2[BLOCK:]

    full = pl.BlockSpec((T, KV_WIDTH), lambda n: (0, 0))
    return pl.pallas_call(
        body, name=name, grid=(nb,),
        in_specs=[qspec, cur(C_K), prev(C_K), cur(C_V), prev(C_V), own, tcur, tprev, sink],
        out_specs=[own, full, full, pl.BlockSpec((N_KV_HEADS, GROUP, 128), lambda n: (0, 0, 0))],
        out_shape=[jax.ShapeDtypeStruct((T, D_MODEL), BF16), jax.ShapeDtypeStruct((T, KV_WIDTH), F32),
                   jax.ShapeDtypeStruct((T, KV_WIDTH), F32), jax.ShapeDtypeStruct((N_KV_HEADS, GROUP, 128), F32)],
        compiler_params=pltpu.CompilerParams(dimension_semantics=("arbitrary",)),
    )(P, P, P, P, P, do, tab, tab, sink_col)


_MW = 256
_FW = 1408


def _gate_specs(T, rows, width):
    tr = _pick(T, (rows, 256, 128, 64, 32, 16, 8))
    col = lambda off: pl.BlockSpec((tr, width), lambda i, j, off=off: (i, off // width + j))
    own = pl.BlockSpec((tr, width), lambda i, j: (i, j))
    return tr, col, own


def _merge_fwd(P, mr, ma, name):
    T = P.shape[0]
    tr, col, own = _gate_specs(T, 1024, _MW)

    def body(gr_ref, ga_ref, mr_ref, ma_ref, o_ref):
        o_ref[...] = (_sigmoid(gr_ref[...]) * mr_ref[...] + _sigmoid(ga_ref[...]) * ma_ref[...]).astype(BF16)

    return pl.pallas_call(
        body, name=name, grid=(T // tr, D_MODEL // _MW), in_specs=[col(C_GRNN), col(C_GATTN), own, own],
        out_specs=own, out_shape=jax.ShapeDtypeStruct((T, D_MODEL), BF16),
        compiler_params=pltpu.CompilerParams(dimension_semantics=("parallel", "parallel")),
    )(P, P, mr, ma)


def _merge_bwd(P, mr, ma, dm, name):
    T = P.shape[0]
    tr, col, own = _gate_specs(T, 512, _MW)

    def body(gr_ref, ga_ref, mr_ref, ma_ref, dm_ref, dmr_ref, dma_ref, dgr_ref, dga_ref):
        dm = dm_ref[...]
        sr = _sigmoid(gr_ref[...])
        sa = _sigmoid(ga_ref[...])
        dmr_ref[...] = (dm * sr).astype(BF16)
        dma_ref[...] = (dm * sa).astype(BF16)
        dgr_ref[...] = (dm * mr_ref[...] * sr * (1.0 - sr)).astype(BF16)
        dga_ref[...] = (dm * ma_ref[...] * sa * (1.0 - sa)).astype(BF16)

    shp = jax.ShapeDtypeStruct((T, D_MODEL), BF16)
    return pl.pallas_call(
        body, name=name, grid=(T // tr, D_MODEL // _MW), in_specs=[col(C_GRNN), col(C_GATTN), own, own, own],
        out_specs=[own] * 4, out_shape=[shp] * 4,
        compiler_params=pltpu.CompilerParams(dimension_semantics=("parallel", "parallel")),
    )(P, P, mr, ma, dm)


def _swiglu_fwd(U, name):
    T = U.shape[0]
    tr, col, own = _gate_specs(T, 256, _FW)

    def body(g_ref, u_ref, o_ref):
        g = g_ref[...]
        o_ref[...] = (g * _sigmoid(g) * u_ref[...]).astype(BF16)

    return pl.pallas_call(
        body, name=name, grid=(T // tr, D_FF // _FW), in_specs=[col(0), col(D_FF)],
        out_specs=own, out_shape=jax.ShapeDtypeStruct((T, D_FF), BF16),
        compiler_params=pltpu.CompilerParams(dimension_semantics=("parallel", "parallel")),
    )(U, U)


def _swiglu_bwd(U, dact, name):
    T = U.shape[0]
    nf = D_FF // _FW
    tr = _pick(T, (256, 128, 64, 32, 16, 8))

    def body(g_ref, u_ref, da_ref, o_ref):
        g = g_ref[...]
        da = da_ref[...]
        s = _sigmoid(g)
        is_gate = pl.program_id(1) < nf
        o_ref[...] = jnp.where(is_gate, da * u_ref[...] * s * (1.0 + g * (1.0 - s)), da * g * s).astype(BF16)

    return pl.pallas_call(
        body, name=name, grid=(T // tr, 2 * nf),
        in_specs=[pl.BlockSpec((tr, _FW), lambda i, j: (i, j % nf)),
                  pl.BlockSpec((tr, _FW), lambda i, j: (i, nf + j % nf)),
                  pl.BlockSpec((tr, _FW), lambda i, j: (i, j % nf))],
        out_specs=pl.BlockSpec((tr, _FW), lambda i, j: (i, j)),
        out_shape=jax.ShapeDtypeStruct((T, 2 * D_FF), BF16),
        compiler_params=pltpu.CompilerParams(dimension_semantics=("parallel", "parallel")),
    )(U, U, dact)


def _cross_probs(qh, kh):
    s = lax.dot_general(qh, kh, _DIMS["nt"], preferred_element_type=F32) * (CROSS_HEAD_DIM ** -0.5)
    p = jnp.exp(s - jnp.max(s, axis=1, keepdims=True))
    return p / jnp.sum(p, axis=1, keepdims=True)


def _cross_fwd(q, kv, name):
    T = q.shape[0]
    M = kv.shape[0]
    tr = _pick(T, (ROW_TILE, 128, 64, 32, 16, 8))
    W = CROSS_HEAD_DIM

    def body(q_ref, kv_ref, o_ref):
        for h in range(CROSS_HEADS):
            qh = q_ref[:, h * W:(h + 1) * W].astype(BF16)
            kh = kv_ref[:, h * W:(h + 1) * W].astype(BF16)
            vh = kv_ref[:, D_MODEL + h * W:D_MODEL + (h + 1) * W].astype(BF16)
            pn = _cross_probs(qh, kh)
            o_ref[:, h * W:(h + 1) * W] = jnp.dot(pn.astype(BF16), vh, preferred_element_type=F32).astype(BF16)

    row = pl.BlockSpec((tr, D_MODEL), lambda i: (i, 0))
    return pl.pallas_call(
        body, name=name, grid=(T // tr,), in_specs=[row, pl.BlockSpec((M, 2 * D_MODEL), lambda i: (0, 0))],
        out_specs=row, out_shape=jax.ShapeDtypeStruct((T, D_MODEL), BF16),
        compiler_params=pltpu.CompilerParams(dimension_semantics=("parallel",)),
    )(q, kv)


def _cross_bwd(q, kv, do, name):
    T = q.shape[0]
    M = kv.shape[0]
    tr = _pick(T, (ROW_TILE, 128, 64, 32, 16, 8))
    W = CROSS_HEAD_DIM

    def body(q_ref, kv_ref, do_ref, dq_ref, dkv_ref):
        @pl.when(pl.program_id(0) == 0)
        def _():
            dkv_ref[...] = jnp.zeros_like(dkv_ref)

        for h in range(CROSS_HEADS):
            qh = q_ref[:, h * W:(h + 1) * W].astype(BF16)
            kh = kv_ref[:, h * W:(h + 1) * W].astype(BF16)
            vh = kv_ref[:, D_MODEL + h * W:D_MODEL + (h + 1) * W].astype(BF16)
            doh = do_ref[:, h * W:(h + 1) * W].astype(BF16)
            pn = _cross_probs(qh, kh)
            dp = lax.dot_general(doh, vh, _DIMS["nt"], preferred_element_type=F32)
            delta = jnp.sum(pn * dp, axis=1, keepdims=True)
            dsc = (pn * (dp - delta) * (W ** -0.5)).astype(BF16)
            dq_ref[:, h * W:(h + 1) * W] = jnp.dot(dsc, kh, preferred_element_type=F32).astype(BF16)
            dkv_ref[:, h * W:(h + 1) * W] += lax.dot_general(dsc, qh, _DIMS["tn"], preferred_element_type=F32)
            dkv_ref[:, D_MODEL + h * W:D_MODEL + (h + 1) * W] += lax.dot_general(
                pn.astype(BF16), doh, _DIMS["tn"], preferred_element_type=F32)

    row = pl.BlockSpec((tr, D_MODEL), lambda i: (i, 0))
    full = pl.BlockSpec((M, 2 * D_MODEL), lambda i: (0, 0))
    return pl.pallas_call(
        body, name=name, grid=(T // tr,), in_specs=[row, full, row], out_specs=[row, full],
        out_shape=[jax.ShapeDtypeStruct((T, D_MODEL), BF16), jax.ShapeDtypeStruct((M, 2 * D_MODEL), F32)],
        compiler_params=pltpu.CompilerParams(dimension_semantics=("arbitrary",)),
    )(q, kv, do)


def _loss_head(y, target, name):
    T, D = y.shape
    tr = _pick(T, (ROW_TILE, 128, 64, 32, 16, 8))

    def body(y_ref, t_ref, l_ref, dy_ref):
        @pl.when(pl.program_id(0) == 0)
        def _():
            l_ref[...] = jnp.zeros_like(l_ref)

        err = y_ref[...] - t_ref[...]
        dy_ref[...] = err * (1.0 / D)
        l_ref[...] += jnp.broadcast_to(0.5 * jnp.sum(jnp.mean(err * err, axis=-1, keepdims=True), axis=0, keepdims=True),
                                       (8, 128))

    row = pl.BlockSpec((tr, D), lambda i: (i, 0))
    return pl.pallas_call(
        body, name=name, grid=(T // tr,), in_specs=[row, row],
        out_specs=[pl.BlockSpec((8, 128), lambda i: (0, 0)), row],
        out_shape=[jax.ShapeDtypeStruct((8, 128), F32), jax.ShapeDtypeStruct((T, D), F32)],
        compiler_params=pltpu.CompilerParams(dimension_semantics=("arbitrary",)),
    )(y, target)


def _sum_slots(recv, name, mine=None):
    _, R, C = recv.shape
    tr = _pick(R, (ROW_TILE, 224, 368, 128, 64, 32, 16, 8))

    def body(*refs):
        r_ref, o_ref = refs[0], refs[-1]
        if mine is None:
            acc = r_ref[0].astype(F32)
            for d in range(1, N_DEV):
                acc = acc + r_ref[d].astype(F32)
        else:
            me = _my_slot()
            acc = refs[1][...].astype(F32)
            for d in range(N_DEV):
                acc = acc + jnp.where(d == me, 0.0, r_ref[d].astype(F32))
        o_ref[...] = acc

    in_specs = [pl.BlockSpec((N_DEV, tr, C), lambda i: (0, i, 0))]
    args = [recv]
    if mine is not None:
        in_specs.append(pl.BlockSpec((tr, C), lambda i: (i, 0)))
        args.append(mine)
    return pl.pallas_call(
        body, name=name, grid=(R // tr,), in_specs=in_specs,
        out_specs=pl.BlockSpec((tr, C), lambda i: (i, 0)), out_shape=jax.ShapeDtypeStruct((R, C), F32),
        compiler_params=pltpu.CompilerParams(dimension_semantics=("parallel",)),
    )(*args)


def _adamw(w, g, m, v, name):
    shape = w.shape
    C = shape[-1]
    R = math.prod(shape[:-1])
    w2, g2, m2, v2 = (t.reshape(R, C) for t in (w, g, m, v))
    tr = _pick(R, (ROW_TILE, 128, 64, 32, 16, 8))

    def body(w_ref, g_ref, m_ref, v_ref, d_ref, mo_ref, vo_ref):
        gg = g_ref[...]
        mn = ADAM_B1 * m_ref[...] + (1.0 - ADAM_B1) * gg
        vn = ADAM_B2 * v_ref[...] + (1.0 - ADAM_B2) * (gg * gg)
        m_hat = mn / (1.0 - ADAM_B1 ** ADAM_STEP)
        v_hat = vn / (1.0 - ADAM_B2 ** ADAM_STEP)
        d_ref[...] = -ADAM_LR * (m_hat / (jnp.sqrt(v_hat) + ADAM_EPS) + ADAM_WD * w_ref[...])
        mo_ref[...] = mn
        vo_ref[...] = vn

    blk = pl.BlockSpec((tr, C), lambda i: (i, 0))
    shp = jax.ShapeDtypeStruct((R, C), F32)
    d, mo, vo = pl.pallas_call(
        body, name=name, grid=(R // tr,), in_specs=[blk] * 4, out_specs=[blk] * 3, out_shape=[shp] * 3,
        compiler_params=pltpu.CompilerParams(dimension_semantics=("parallel",)),
    )(w2, g2, m2, v2)
    return d.reshape(shape), mo.reshape(shape), vo.reshape(shape)


def _all_gather_many(bufs, name):
    n = len(bufs)

    def body(*refs):
        xs, outs = refs[:n], refs[n:2 * n]
        send_sems, recv_sems, local_sems = refs[2 * n:]
        x, y, c = lax.axis_index("x"), lax.axis_index("y"), lax.axis_index("c")
        me, sibling = (x, y, c), (x, y, 1 - c)
        chips = [(1 - x, y), (x, 1 - y), (1 - x, 1 - y)]

        def slot(i, px, py, pc):
            return outs[i].at[4 * px + 2 * py + pc]

        def copy(i, k, block, to, src=None):
            return pltpu.make_async_remote_copy(
                src_ref=slot(i, *block) if src is None else src, dst_ref=slot(i, *block),
                send_sem=send_sems.at[7 * i + k], recv_sem=recv_sems.at[7 * i + k],
                device_id=to, device_id_type=pl.DeviceIdType.MESH)

        mine = [pltpu.make_async_copy(xs[i], slot(i, *me), local_sems.at[i]) for i in range(n)]
        for cp in mine:
            cp.start()
        first = [copy(i, 0, me, sibling, src=xs[i]) for i in range(n)]
        for j, chip in enumerate(chips):
            first += [copy(i, 1 + j, me, (*chip, c), src=xs[i]) for i in range(n)]
        for cp in first:
            cp.start()
        passed = []
        for j, chip in enumerate(chips):
            for i in range(n):
                copy(i, 1 + j, (*chip, c), me).wait_recv()
                passed.append(copy(i, 4 + j, (*chip, c), sibling))
                passed[-1].start()
        for i in range(n):
            copy(i, 0, sibling, me).wait_recv()
        for j, chip in enumerate(chips):
            for i in range(n):
                copy(i, 4 + j, (*chip, 1 - c), me).wait_recv()
        for cp in first + passed:
            cp.wait_send()
        for cp in mine:
            cp.wait()

    hbm = pl.BlockSpec(memory_space=pl.ANY)
    return pl.pallas_call(
        body, name=name, out_shape=[jax.ShapeDtypeStruct((N_DEV,) + b.shape, b.dtype) for b in bufs],
        in_specs=[hbm] * n, out_specs=[hbm] * n,
        scratch_shapes=[pltpu.SemaphoreType.DMA((7 * n,)), pltpu.SemaphoreType.DMA((7 * n,)),
                        pltpu.SemaphoreType.DMA((n,))],
    )(*bufs)


def _all_gather(buf, name):
    return _all_gather_many([buf], name)[0]


_HBM = pl.BlockSpec(memory_space=pltpu.HBM)
_SEM = pl.BlockSpec(memory_space=pltpu.SEMAPHORE)
_EFFECT = pltpu.SideEffectType.DATAFLOW_SIDE_EFFECTING


def _peer(k):
    x, y, c = lax.axis_index("x"), lax.axis_index("y"), lax.axis_index("c")
    return x ^ ((k >> 2) & 1), y ^ ((k >> 1) & 1), c ^ (k & 1)


def _my_slot():
    return 4 * lax.axis_index("x") + 2 * lax.axis_index("y") + lax.axis_index("c")


def _split_copy(src_refs, land_refs, send_sems, recv_sems, i, k):
    px, py, pc = _peer(k)
    return pltpu.make_async_remote_copy(
        src_ref=src_refs[i].at[4 * px + 2 * py + pc], dst_ref=land_refs[i].at[_my_slot()],
        send_sem=send_sems.at[7 * i + k - 1], recv_sem=recv_sems.at[7 * i + k - 1],
        device_id=(px, py, pc), device_id_type=pl.DeviceIdType.MESH)


def _split_start(srcs, lands, name):
    n = len(srcs)

    def body(*refs):
        src_refs, land_refs = refs[:n], refs[n:2 * n]
        send_sems, recv_sems = refs[2 * n], refs[2 * n + 1]
        token = refs[-1]
        for i in range(n):
            for k in range(1, N_DEV):
                _split_copy(src_refs, land_refs, send_sems, recv_sems, i, k).start()
        token[...] = jnp.zeros_like(token)

    outs = pl.pallas_call(
        body, name=name,
        out_shape=(pltpu.SemaphoreType.DMA((7 * n,)), pltpu.SemaphoreType.DMA((7 * n,)),
                   *[pltpu.HBM(a.shape, a.dtype) for a in srcs], *[pltpu.HBM(a.shape, a.dtype) for a in lands],
                   jax.ShapeDtypeStruct((8, 128), F32)),
        in_specs=[_HBM] * (2 * n),
        out_specs=(_SEM, _SEM, *([_HBM] * (2 * n)), pl.BlockSpec(memory_space=pltpu.VMEM)),
        input_output_aliases={i: 2 + i for i in range(2 * n)},
        compiler_params=pltpu.CompilerParams(has_side_effects=_EFFECT),
    )(*[pltpu.with_memory_space_constraint(a, pltpu.HBM) for a in list(srcs) + list(lands)])
    return outs[0], outs[1], outs[2:2 + n], outs[2 + n:2 + 2 * n], outs[-1]


def _split_wait(send_sems, recv_sems, srcs, lands, after, name):
    n = len(srcs)

    def body(*refs):
        src_refs, land_refs = refs[:n], refs[n:2 * n]
        ssem, rsem = refs[2 * n], refs[2 * n + 1]
        for i in range(n):
            for k in range(1, N_DEV):
                cp = _split_copy(src_refs, land_refs, ssem, rsem, i, k)
                cp.wait_send()
                cp.wait_recv()

    outs = pl.pallas_call(
        body, name=name,
        out_shape=(*[pltpu.HBM(a.shape, a.dtype) for a in srcs], *[pltpu.HBM(a.shape, a.dtype) for a in lands]),
        in_specs=[*([_HBM] * (2 * n)), _SEM, _SEM, pl.BlockSpec(memory_space=pl.ANY)],
        out_specs=tuple([_HBM] * (2 * n)),
        input_output_aliases={i: i for i in range(2 * n)},
        compiler_params=pltpu.CompilerParams(has_side_effects=_EFFECT),
    )(*srcs, *lands, send_sems, recv_sems, after)
    return outs[:n], outs[n:]


def _gather_first(src_refs, land_refs, send_sems, recv_sems, i, k):
    x, y, c = lax.axis_index("x"), lax.axis_index("y"), lax.axis_index("c")
    to = ((x, y, 1 - c), (1 - x, y, c), (x, 1 - y, c), (1 - x, 1 - y, c))[k]
    return pltpu.make_async_remote_copy(
        src_ref=src_refs[i], dst_ref=land_refs[i].at[_my_slot()],
        send_sem=send_sems.at[4 * i + k], recv_sem=recv_sems.at[4 * i + k],
        device_id=to, device_id_type=pl.DeviceIdType.MESH)


def _gather_second(land_refs, send_sems, recv_sems, i, j):
    x, y, c = lax.axis_index("x"), lax.axis_index("y"), lax.axis_index("c")
    px, py = ((1 - x, y), (x, 1 - y), (1 - x, 1 - y))[j]
    slot = land_refs[i].at[4 * px + 2 * py + c]
    return pltpu.make_async_remote_copy(
        src_ref=slot, dst_ref=slot, send_sem=send_sems.at[3 * i + j], recv_sem=recv_sems.at[3 * i + j],
        device_id=(x, y, 1 - c), device_id_type=pl.DeviceIdType.MESH)


def _gather_start(srcs, lands, name, after):
    n = len(srcs)

    def body(*refs):
        src_refs, land_refs = refs[:n], refs[n:2 * n]
        send_sems, recv_sems = refs[2 * n + 1], refs[2 * n + 2]
        token = refs[-1]
        for k in range(4):
            for i in range(n):
                _gather_first(src_refs, land_refs, send_sems, recv_sems, i, k).start()
        token[...] = jnp.zeros_like(token)

    outs = pl.pallas_call(
        body, name=name,
        out_shape=(pltpu.SemaphoreType.DMA((4 * n,)), pltpu.SemaphoreType.DMA((4 * n,)),
                   *[pltpu.HBM(a.shape, a.dtype) for a in srcs], *[pltpu.HBM(a.shape, a.dtype) for a in lands],
                   jax.ShapeDtypeStruct((8, 128), F32)),
        in_specs=[_HBM] * (2 * n) + [pl.BlockSpec(memory_space=pl.ANY)],
        out_specs=(_SEM, _SEM, *([_HBM] * (2 * n)), pl.BlockSpec(memory_space=pltpu.VMEM)),
        input_output_aliases={i: 2 + i for i in range(2 * n)},
        compiler_params=pltpu.CompilerParams(has_side_effects=_EFFECT),
    )(*[pltpu.with_memory_space_constraint(a, pltpu.HBM) for a in list(srcs) + list(lands)], after)
    return outs[0], outs[1], outs[2:2 + n], outs[2 + n:2 + 2 * n], outs[-1]


def _gather_forward(send1, recv1, srcs, lands, after, name):
    n = len(srcs)

    def body(*refs):
        src_refs, land_refs = refs[:n], refs[n:2 * n]
        s1, r1 = refs[2 * n], refs[2 * n + 1]
        s2, r2 = refs[2 * n + 3], refs[2 * n + 4]
        token = refs[-1]
        for j in range(3):
            for i in range(n):
                _gather_first(src_refs, land_refs, s1, r1, i, 1 + j).wait_recv()
                _gather_second(land_refs, s2, r2, i, j).start()
        for i in range(n):
            _gather_first(src_refs, land_refs, s1, r1, i, 0).wait_recv()
            for k in range(4):
                _gather_first(src_refs, land_refs, s1, r1, i, k).wait_send()
        token[...] = jnp.zeros_like(token)

    outs = pl.pallas_call(
        body, name=name,
        out_shape=(pltpu.SemaphoreType.DMA((3 * n,)), pltpu.SemaphoreType.DMA((3 * n,)),
                   *[pltpu.HBM(a.shape, a.dtype) for a in srcs], *[pltpu.HBM(a.shape, a.dtype) for a in lands],
                   jax.ShapeDtypeStruct((8, 128), F32)),
        in_specs=[*([_HBM] * (2 * n)), _SEM, _SEM, pl.BlockSpec(memory_space=pl.ANY)],
        out_specs=(_SEM, _SEM, *([_HBM] * (2 * n)), pl.BlockSpec(memory_space=pltpu.VMEM)),
        input_output_aliases={i: 2 + i for i in range(2 * n)},
        compiler_params=pltpu.CompilerParams(has_side_effects=_EFFECT),
    )(*srcs, *lands, send1, recv1, after)
    return outs[0], outs[1], outs[2:2 + n], outs[2 + n:2 + 2 * n], outs[-1]


def _gather_wait(send2, recv2, srcs, lands, after, name):
    n = len(srcs)

    def body(*refs):
        land_refs = refs[n:2 * n]
        s2, r2 = refs[2 * n], refs[2 * n + 1]
        for i in range(n):
            for j in range(3):
                cp = _gather_second(land_refs, s2, r2, i, j)
                cp.wait_send()
                cp.wait_recv()

    outs = pl.pallas_call(
        body, name=name,
        out_shape=(*[pltpu.HBM(a.shape, a.dtype) for a in srcs], *[pltpu.HBM(a.shape, a.dtype) for a in lands]),
        in_specs=[*([_HBM] * (2 * n)), _SEM, _SEM, pl.BlockSpec(memory_space=pl.ANY)],
        out_specs=tuple([_HBM] * (2 * n)),
        input_output_aliases={i: i for i in range(2 * n)},
        compiler_params=pltpu.CompilerParams(has_side_effects=_EFFECT),
    )(*srcs, *lands, send2, recv2, after)
    return outs[:n], outs[n:]


def _row(a, l):
    return a[l:l + 1]


def _layer_fwd(h, mem, W, l, tab, after, mid_layer):
    s = {}
    n = f"l{l}_"
    s["h0"] = h
    P = _matmul(h, W["w_in"][l], "nt", n + "proj", after=after)
    s["P"] = P
    sink_col = jnp.repeat(W["sinks"][l].reshape(N_KV_HEADS, GROUP), BLOCK, axis=1)[:, :, None]
    s["sink_col"] = sink_col
    y_rnn, hs = _rnn_fwd(P, W["conv_w"][l], _row(W["conv_b"], l), W["w_rg"][l], _row(W["b_rg"], l),
                         W["w_ig"][l], _row(W["b_ig"], l), _row(W["lru_lambda"], l), n + "rnn_fwd")
    y_attn = _swa_fwd(P, tab, sink_col, n + "swa_fwd")
    mr = _matmul(y_rnn, W["w_br_rnn"][l], "nn", n + "br_rnn")
    ma = _matmul(y_attn, W["w_br_attn"][l], "nn", n + "br_attn")
    merged = _merge_fwd(P, mr, ma, n + "merge_fwd")
    mix = _matmul(merged, W["w_out"][l], "nn", n + "w_out")
    h1, xh1, rs1 = _ln_fwd(h, mix, _row(W["ln1_g"], l), _row(W["ln1_b"], l), n + "ln1_fwd")
    s.update(hs=hs, y_rnn=y_rnn, y_attn=y_attn, mr=mr, ma=ma, merged=merged, xh1=xh1, rs1=rs1, h1=h1)

    qc = _matmul(h1, W["cq_w"][l], "nn", n + "cq", out_dtype=BF16)
    kv = _matmul(mem, W["ckv_w"][l], "nt", n + "ckv", out_dtype=BF16)
    oc = _cross_fwd(qc, kv, n + "cross_fwd")
    ca = _matmul(oc, W["co_w"][l], "nn", n + "co")
    h2, xh2, rs2 = _ln_fwd(h1, ca, _row(W["ln2_g"], l), _row(W["ln2_b"], l), n + "ln2_fwd")
    s.update(qc=qc, kv=kv, oc=oc, xh2=xh2, rs2=rs2, h2=h2)

    U = _matmul(h2, W["ffn_wi"][l], "nt", n + "ffn_wi", after=mid_layer(l, h2))
    act = _swiglu_fwd(U, n + "swiglu_fwd")
    f = _matmul(act, W["ffn_wo"][l], "nn", n + "ffn_wo")
    h3, xh3, rs3 = _ln_fwd(h2, f, _row(W["ln3_g"], l), _row(W["ln3_b"], l), n + "ln3_fwd")
    s.update(U=U, act=act, xh3=xh3, rs3=rs3)
    return h3, s


GRAD_PARTS = (("ffn_wi", "ffn_wo", "cq_w", "ckv_w", "co_w"), ("w_out", "w_br_rnn", "w_br_attn"),
              ("w_in", "w_rg", "w_ig"))


def _layer_bwd(dh3, mem, W, l, tab, s, send):
    n = f"l{l}_"
    g = {}
    dz3, g["ln3_g"], g["ln3_b"] = _ln_bwd(dh3, s["xh3"], s["rs3"], _row(W["ln3_g"], l), n + "ln3_bwd")
    g["ffn_wo"] = _matmul(s["act"], dz3, "tn", n + "d_ffn_wo", out_dtype=BF16)
    dact = _matmul(dz3, W["ffn_wo"][l], "nt", n + "d_act")
    dU = _swiglu_bwd(s["U"], dact, n + "swiglu_bwd")
    g["ffn_wi"] = _matmul(dU, s["h2"], "tn", n + "d_ffn_wi", out_dtype=BF16)
    dh2 = _matmul(dU, W["ffn_wi"][l], "nn", n + "d_h2", add=dz3, add_scale=ALPHA)
    dz2, g["ln2_g"], g["ln2_b"] = _ln_bwd(dh2, s["xh2"], s["rs2"], _row(W["ln2_g"], l), n + "ln2_bwd")
    g["co_w"] = _matmul(s["oc"], dz2, "tn", n + "d_co", out_dtype=BF16)
    doc = _matmul(dz2, W["co_w"][l], "nt", n + "d_oc", out_dtype=BF16)
    dqc, dkv = _cross_bwd(s["qc"], s["kv"], doc, n + "cross_bwd")
    g["ckv_w"] = _matmul(dkv, mem, "tn", n + "d_ckv", out_dtype=BF16)
    g["cq_w"] = _matmul(s["h1"], dqc, "tn", n + "d_cq", out_dtype=BF16)
    after = send(l, 0, g)
    dh1 = _matmul(dqc, W["cq_w"][l], "nt", n + "d_h1", add=dz2, add_scale=ALPHA, after=after)
    dz1, g["ln1_g"], g["ln1_b"] = _ln_bwd(dh1, s["xh1"], s["rs1"], _row(W["ln1_g"], l), n + "ln1_bwd")
    g["w_out"] = _matmul(s["merged"], dz1, "tn", n + "d_w_out", out_dtype=BF16)
    dmerged = _matmul(dz1, W["w_out"][l], "nt", n + "d_merged")
    dmr, dma, dgrnn, dgattn = _merge_bwd(s["P"], s["mr"], s["ma"], dmerged, n + "merge_bwd")
    g["w_br_rnn"] = _matmul(s["y_rnn"], dmr, "tn", n + "d_br_rnn", out_dtype=BF16)
    g["w_br_attn"] = _matmul(s["y_attn"], dma, "tn", n + "d_br_attn", out_dtype=BF16)
    after = send(l, 1, g)
    dy_rnn = _matmul(dmr, W["w_br_rnn"][l], "nt", n + "d_y_rnn", after=after)
    dy_attn = _matmul(dma, W["w_br_attn"][l], "nt", n + "d_y_attn", out_dtype=BF16)
    dxr, dgr, g["conv_w"], g["conv_b"], g["w_rg"], g["b_rg"], g["w_ig"], g["b_ig"], g["lru_lambda"] = _rnn_bwd(
        s["P"], s["hs"], dy_rnn, W["conv_w"][l], _row(W["conv_b"], l), W["w_rg"][l], _row(W["b_rg"], l),
        W["w_ig"][l], _row(W["b_ig"], l), _row(W["lru_lambda"], l), n + "rnn_bwd")
    dq, dk, dv, dsk = _swa_bwd(s["P"], dy_attn, tab, s["sink_col"], n + "swa_bwd")
    g["sinks"] = dsk[:, :, 0].reshape(1, N_Q_HEADS)
    dP = jnp.concatenate([dxr, dgr, dq, dk.astype(BF16), dv.astype(BF16), dgrnn, dgattn], axis=1)
    g["w_in"] = _matmul(dP, s["h0"], "tn", n + "d_w_in", out_dtype=BF16)
    after = send(l, 2, g)
    dh = _matmul(dP, W["w_in"][l], "nn", n + "d_h0", add=dz1, add_scale=ALPHA, after=after)
    return dh, g


def _local_step(x, mem, target, W, before_layer, mid_layer, send):
    T = x.shape[0]
    tab = _rope_table(T)
    h = x
    saved = []
    for l in range(DEPTH):
        after = before_layer(l, h)
        h, s = _layer_fwd(h, mem, W, l, tab, after, mid_layer)
        saved.append(s)
    lblk, dh = _loss_head(h, target, "loss_head")
    grads = [None] * DEPTH
    for l in reversed(range(DEPTH)):
        dh, grads[l] = _layer_bwd(dh, mem, W, l, tab, saved[l], send)
    return lblk[0, 0], dh, grads


COL_SHARDED = ("w_in", "ckv_w", "ffn_wi")
GATE_MATS = ("w_rg", "w_ig")


def _shard_rows(shards, l):
    out = []
    for n, r in PACK_ROWS:
        a = shards[n][l].astype(BF16)
        if n in COL_SHARDED:
            a = a.T
        elif n in GATE_MATS:
            a = a.reshape(RNN_BLOCKS * RNN_BLOCK // N_DEV, RNN_BLOCK)
        out.append(a)
    return out


def _full_weight(G, name):
    if name in GATE_MATS:
        return jnp.transpose(G.reshape(N_DEV, RNN_BLOCKS, RNN_BLOCK // N_DEV, RNN_BLOCK), (1, 0, 2, 3)).reshape(
            RNN_BLOCKS, RNN_BLOCK, RNN_BLOCK)
    return G.reshape(N_DEV * G.shape[1], G.shape[2])


SHARD_ROWS = dict(PACK_ROWS)


def _pack_blocks(g, names):
    parts = []
    for name in names:
        a = g[name]
        if name in GATE_MATS:
            a = jnp.transpose(a.astype(BF16).reshape(RNN_BLOCKS, N_DEV, RNN_BLOCK // N_DEV, RNN_BLOCK), (1, 0, 2, 3))
        parts.append(a.reshape(N_DEV, SHARD_ROWS[name], D_MODEL))
    return jnp.concatenate(parts, axis=1)


def _pack_small(g):
    rows = [g["conv_w"]]
    for nme in SMALL_NAMES:
        a = g[nme]
        if nme == "sinks":
            a = jnp.pad(a, ((0, 0), (0, D_MODEL - N_Q_HEADS)))
        rows.append(a)
    rows.append(jnp.zeros((SMALL_ROWS - CONV_WIDTH - len(SMALL_NAMES), D_MODEL), F32))
    return jnp.concatenate(rows, axis=0)


_SHARD_SHAPES = {"w_in": (1024, 672), "w_br_rnn": (128, 1024), "w_br_attn": (128, 1024), "w_out": (128, 1024),
                 "cq_w": (128, 1024), "ckv_w": (1024, 256), "co_w": (128, 1024), "ffn_wi": (1024, 704),
                 "ffn_wo": (352, 1024), "w_rg": (4, 32, 256), "w_ig": (4, 32, 256)}

WEIGHT_NAMES = ("w_in", "conv_w", "conv_b", "w_rg", "b_rg", "w_ig", "b_ig", "lru_lambda", "w_br_rnn", "w_br_attn",
                "sinks", "w_out", "ln1_g", "ln1_b", "cq_w", "ckv_w", "co_w", "ln2_g", "ln2_b", "ffn_wi", "ffn_wo",
                "ln3_g", "ln3_b")


def kernel(x, mem, w_in, conv_w, conv_b, w_rg, b_rg, w_ig, b_ig, lru_lambda, w_br_rnn, w_br_attn, sinks, w_out, ln1_g, ln1_b, cq_w, ckv_w, co_w, ln2_g, ln2_b, ffn_wi, ffn_wo, ln3_g, ln3_b, loss_target, m_w_in, m_conv_w, m_conv_b, m_w_rg, m_b_rg, m_w_ig, m_b_ig, m_lru_lambda, m_w_br_rnn, m_w_br_attn, m_sinks, m_w_out, m_ln1_g, m_ln1_b, m_cq_w, m_ckv_w, m_co_w, m_ln2_g, m_ln2_b, m_ffn_wi, m_ffn_wo, m_ln3_g, m_ln3_b, v_w_in, v_conv_w, v_conv_b, v_w_rg, v_b_rg, v_w_ig, v_b_ig, v_lru_lambda, v_w_br_rnn, v_w_br_attn, v_sinks, v_w_out, v_ln1_g, v_ln1_b, v_cq_w, v_ckv_w, v_co_w, v_ln2_g, v_ln2_b, v_ffn_wi, v_ffn_wo, v_ln3_g, v_ln3_b):
    w = dict(w_in=w_in, conv_w=conv_w, conv_b=conv_b, w_rg=w_rg, b_rg=b_rg, w_ig=w_ig, b_ig=b_ig,
             lru_lambda=lru_lambda, w_br_rnn=w_br_rnn, w_br_attn=w_br_attn, sinks=sinks, w_out=w_out, ln1_g=ln1_g,
             ln1_b=ln1_b, cq_w=cq_w, ckv_w=ckv_w, co_w=co_w, ln2_g=ln2_g, ln2_b=ln2_b, ffn_wi=ffn_wi, ffn_wo=ffn_wo,
             ln3_g=ln3_g, ln3_b=ln3_b)
    m = dict(w_in=m_w_in, conv_w=m_conv_w, conv_b=m_conv_b, w_rg=m_w_rg, b_rg=m_b_rg, w_ig=m_w_ig, b_ig=m_b_ig,
             lru_lambda=m_lru_lambda, w_br_rnn=m_w_br_rnn, w_br_attn=m_w_br_attn, sinks=m_sinks, w_out=m_w_out,
             ln1_g=m_ln1_g, ln1_b=m_ln1_b, cq_w=m_cq_w, ckv_w=m_ckv_w, co_w=m_co_w, ln2_g=m_ln2_g, ln2_b=m_ln2_b,
             ffn_wi=m_ffn_wi, ffn_wo=m_ffn_wo, ln3_g=m_ln3_g, ln3_b=m_ln3_b)
    v = dict(w_in=v_w_in, conv_w=v_conv_w, conv_b=v_conv_b, w_rg=v_w_rg, b_rg=v_b_rg, w_ig=v_w_ig, b_ig=v_b_ig,
             lru_lambda=v_lru_lambda, w_br_rnn=v_w_br_rnn, w_br_attn=v_w_br_attn, sinks=v_sinks, w_out=v_w_out,
             ln1_g=v_ln1_g, ln1_b=v_ln1_b, cq_w=v_cq_w, ckv_w=v_ckv_w, co_w=v_co_w, ln2_g=v_ln2_g, ln2_b=v_ln2_b,
             ffn_wi=v_ffn_wi, ffn_wo=v_ffn_wo, ln3_g=v_ln3_g, ln3_b=v_ln3_b)
    my_dev = 4 * lax.axis_index("x") + 2 * lax.axis_index("y") + lax.axis_index("c")

    W = {n: [None] * DEPTH for n, _ in PACK_ROWS}
    for (n, _), G in zip(PACK_ROWS, _all_gather_many(_shard_rows(w, 0), "l0_gather_weights")):
        W[n][0] = _full_weight(G, n)
    conv_all = _all_gather(conv_w.reshape(DEPTH * CONV_WIDTH, D_MODEL // N_DEV), "gather_conv")
    W["conv_w"] = jnp.transpose(conv_all, (1, 0, 2)).reshape(DEPTH, CONV_WIDTH, D_MODEL)
    for n in SMALL_NAMES:
        W[n] = w[n]
    flying = {}
    token = conv_all
    for l in range(1, DEPTH):
        srcs = _shard_rows(w, l)
        lands = [lax.empty((N_DEV,) + a.shape, a.dtype) for a in srcs]
        flying[l] = _gather_start(srcs, lands, f"l{l}_gather_start", token)
        token = flying[l][4]
    first_token = token

    def mid_layer(l, h2):
        if l + 1 not in flying:
            return None
        send1, recv1, srcs, lands, _ = flying[l + 1]
        flying[l + 1] = _gather_forward(send1, recv1, srcs, lands, h2, f"l{l + 1}_gather_forward")
        return flying[l + 1][4]

    def before_layer(l, h):
        if l == 0:
            return first_token
        send2, recv2, srcs, lands, _ = flying.pop(l)
        srcs, lands = _gather_wait(send2, recv2, srcs, lands, h, f"l{l}_gather_wait")
        for (n, _), mine, G in zip(PACK_ROWS, srcs, lands):
            W[n][l] = _full_weight(lax.dynamic_update_index_in_dim(G, mine, my_dev, 0), n)
        return None

    summed = {}
    sent = {}

    def finish(l, part, after):
        ssem, rsem, blocks, land, mine = sent.pop((l, part))
        _, (recv,) = _split_wait(ssem, rsem, [blocks], [land], after, f"l{l}_exchange_wait{part}")
        summed[(l, part)] = _sum_slots(recv, f"l{l}_sum_grads{part}", mine=mine)

    def send(l, part, g):
        blocks = _pack_blocks(g, GRAD_PARTS[part])
        mine = lax.dynamic_index_in_dim(blocks, my_dev, 0, keepdims=False)
        if (l + 1, part) in sent:
            finish(l + 1, part, blocks)
        ssem, rsem, (blocks,), (land,), token = _split_start([blocks], [lax.empty(blocks.shape, blocks.dtype)],
                                                             f"l{l}_exchange_start{part}")
        sent[(l, part)] = (ssem, rsem, blocks, land, mine)
        return token

    loss_local, dx, layer_grads = _local_step(x[0], mem[0], loss_target[0], W, before_layer, mid_layer, send)
    for part in range(len(GRAD_PARTS)):
        finish(0, part, dx)
    loss = lax.psum(loss_local, MESH_AXES)

    small_all = _all_gather(jnp.concatenate([_pack_small(g) for g in layer_grads], axis=0), "gather_small_grads")
    small_sum = _sum_slots(small_all, "sum_small_grads").reshape(DEPTH, SMALL_ROWS, D_MODEL)

    grads = {}
    for part, names in enumerate(GRAD_PARTS):
        G = jnp.stack([summed[(l, part)] for l in range(DEPTH)])
        o = 0
        for n in names:
            r = SHARD_ROWS[n]
            blk = G[:, o:o + r, :]
            grads[n] = jnp.transpose(blk, (0, 2, 1)) if n in COL_SHARDED else blk.reshape((DEPTH,) + _SHARD_SHAPES[n])
            o += r
    conv_full = small_sum[:, :CONV_WIDTH, :]
    grads["conv_w"] = lax.dynamic_slice_in_dim(conv_full, my_dev * (D_MODEL // N_DEV), D_MODEL // N_DEV, axis=2)
    for i, n in enumerate(SMALL_NAMES):
        row = small_sum[:, CONV_WIDTH + i, :]
        grads[n] = row[:, :N_Q_HEADS] if n == "sinks" else row

    deltas, new_m, new_v = {}, {}, {}
    for n in WEIGHT_NAMES:
        deltas[n], new_m[n], new_v[n] = _adamw(w[n], grads[n], m[n], v[n], "adamw_" + n)

    return (loss, dx[None], *[grads[n] for n in WEIGHT_NAMES], *[deltas[n] for n in WEIGHT_NAMES],
            *[new_m[n] for n in WEIGHT_NAMES], *[new_v[n] for n in WEIGHT_NAMES])
```

```python
import functools
import math

import jax
import jax.numpy as jnp
from jax import lax
from jax.experimental import pallas as pl
from jax.experimental.pallas import tpu as pltpu

F32 = jnp.float32
BF16 = jnp.bfloat16

D_MODEL = 1024
DEPTH = 4
N_DEV = 8
RNN_BLOCKS = 4
RNN_BLOCK = 256
CONV_WIDTH = 4
LRU_C = 8.0
HEAD_DIM = 64
N_Q_HEADS = 16
N_KV_HEADS = 2
GROUP = 8
KV_WIDTH = 128
BLOCK = 128
ROPE_THETA = 500000.0
ROT_DIM = 16
IN_COLS = 5376
CROSS_HEADS = 4
CROSS_HEAD_DIM = 256
D_FF = 2816
LN_EPS = 1e-5
ALPHA = (2 * DEPTH) ** 0.25
NEG_INF = -1e30

ADAM_LR = 0.001
ADAM_B1 = 0.9
ADAM_B2 = 0.999
ADAM_EPS = 1e-08
ADAM_WD = 0.01
ADAM_STEP = 10

C_XR, C_GR, C_Q, C_K, C_V, C_GRNN, C_GATTN = 0, 1024, 2048, 3072, 3200, 3328, 4352

TIME_CHUNK = 256
ROW_TILE = 256

MESH_AXES = ("x", "y", "c")

PACK_ROWS = (("w_in", 672), ("w_br_rnn", 128), ("w_br_attn", 128), ("w_out", 128), ("cq_w", 128),
             ("ckv_w", 256), ("co_w", 128), ("ffn_wi", 704), ("ffn_wo", 352), ("w_rg", 32), ("w_ig", 32))
SMALL_NAMES = ("conv_b", "b_rg", "b_ig", "lru_lambda", "sinks", "ln1_g", "ln1_b", "ln2_g", "ln2_b", "ln3_g", "ln3_b")
SMALL_ROWS = 16


def _pick(dim, cands):
    for c in cands:
        if dim % c == 0:
            return c
    return dim


_DIMS = {"nn": (((1,), (0,)), ((), ())), "nt": (((1,), (1,)), ((), ())), "tn": (((0,), (0,)), ((), ()))}

MATMUL_VMEM_BUDGET = 44 * 2 ** 20
MATMUL_MAX_TILE = 2048
MXU_DIM = 256
STEP_COST_BYTES = 500_000


def _tile_candidates(dim):
    c = [d for d in range(MXU_DIM, min(dim, MATMUL_MAX_TILE) + 1, MXU_DIM) if dim % d == 0]
    return c or [dim]


def _matmul_tiles(M, N, K, sa, sb, so, has_add):
    best = None
    for tk in _tile_candidates(K):
        nk = K // tk
        for tm in _tile_candidates(M):
            for tn in _tile_candidates(N):
                vmem = 2 * (tm * tk * sa + tk * tn * sb + tm * tn * so) + tm * tn * 4
                vmem += tm * tn * 4 if nk > 1 else 0
                vmem += 2 * tm * tn * 4 if has_add else 0
                vmem += (tm * tk * 2 if sa == 4 else 0) + (tk * tn * 2 if sb == 4 else 0)
                if vmem > MATMUL_VMEM_BUDGET:
                    continue
                steps = (M // tm) * (N // tn) * nk
                exposed = tm * tk * sa + tk * tn * sb + tm * tn * so
                fixed = M * N * so + steps * STEP_COST_BYTES + exposed
                a_in = M * K * sa * ((N // tn) if nk > 1 else 1) + K * N * sb * (M // tm)
                b_in = M * K * sa * (N // tn) + K * N * sb * ((M // tm) if nk > 1 else 1)
                for cost, m_outer in ((a_in + fixed, True), (b_in + fixed, False)):
                    if best is None or cost < best[0]:
                        best = (cost, tm, tn, tk, m_outer)
    return best[1:]


def _matmul(a, b, mode, name, add=None, add_scale=1.0, out_dtype=F32, after=None):
    if mode == "nn":
        (M, K), (_, N) = a.shape, b.shape
    elif mode == "nt":
        (M, K), (N, _) = a.shape, b.shape
    else:
        (K, M), (_, N) = a.shape, b.shape
    tm, tn, tk, m_outer = _matmul_tiles(M, N, K, a.dtype.itemsize, b.dtype.itemsize, jnp.dtype(out_dtype).itemsize,
                                        add is not None)
    nk = K // tk
    dims = _DIMS[mode]

    def body(*refs):
        if after is not None:
            refs = refs[:-2 - (nk > 1)] + refs[-1 - (nk > 1):]
        a_ref, b_ref = refs[0], refs[1]
        c_ref = refs[2] if add is not None else None
        o_ref = refs[3] if add is not None else refs[2]

        def finish(r):
            if add is not None:
                r = r + add_scale * c_ref[...]
            o_ref[...] = r.astype(out_dtype)

        prod = lax.dot_general(a_ref[...].astype(BF16), b_ref[...].astype(BF16), dims, preferred_element_type=F32)
        if nk == 1:
            finish(prod)
            return
        acc_ref = refs[-1]
        k = pl.program_id(2)

        @pl.when(k == 0)
        def _():
            acc_ref[...] = prod

        @pl.when(k > 0)
        def _():
            acc_ref[...] += prod

        @pl.when(k == nk - 1)
        def _():
            finish(acc_ref[...])

    ij = (lambda p, q: (p, q)) if m_outer else (lambda p, q: (q, p))
    if mode == "nn":
        a_spec = pl.BlockSpec((tm, tk), lambda p, q, k: (ij(p, q)[0], k))
        b_spec = pl.BlockSpec((tk, tn), lambda p, q, k: (k, ij(p, q)[1]))
    elif mode == "nt":
        a_spec = pl.BlockSpec((tm, tk), lambda p, q, k: (ij(p, q)[0], k))
        b_spec = pl.BlockSpec((tn, tk), lambda p, q, k: (ij(p, q)[1], k))
    else:
        a_spec = pl.BlockSpec((tk, tm), lambda p, q, k: (k, ij(p, q)[0]))
        b_spec = pl.BlockSpec((tk, tn), lambda p, q, k: (k, ij(p, q)[1]))
    o_spec = pl.BlockSpec((tm, tn), lambda p, q, k: ij(p, q))
    in_specs = [a_spec, b_spec]
    args = [a, b]
    if add is not None:
        in_specs.append(o_spec)
        args.append(add)
    if after is not None:
        in_specs.append(pl.BlockSpec(memory_space=pl.ANY))
        args.append(after)
    return pl.pallas_call(
        body, name=name, grid=(M // tm, N // tn, nk) if m_outer else (N // tn, M // tm, nk),
        in_specs=in_specs, out_specs=o_spec,
        out_shape=jax.ShapeDtypeStruct((M, N), out_dtype),
        scratch_shapes=[pltpu.VMEM((tm, tn), F32)] if nk > 1 else [],
        compiler_params=pltpu.CompilerParams(dimension_semantics=("parallel", "parallel", "arbitrary")),
    )(*args)


def _ln_fwd(h, f, g, b, name):
    T, D = h.shape
    tr = _pick(T, (ROW_TILE, 128, 64, 32, 16, 8))

    def body(h_ref, f_ref, g_ref, b_ref, o_ref, xh_ref, rs_ref):
        z = ALPHA * h_ref[...] + f_ref[...]
        mu = jnp.mean(z, axis=-1, keepdims=True)
        zc = z - mu
        var = jnp.mean(zc * zc, axis=-1, keepdims=True)
        rstd = lax.rsqrt(var + LN_EPS)
        xh = zc * rstd
        xh_ref[...] = xh
        rs_ref[...] = rstd
        o_ref[...] = xh * g_ref[...] + b_ref[...]

    row = pl.BlockSpec((tr, D), lambda i: (i, 0))
    vec = pl.BlockSpec((1, D), lambda i: (0, 0))
    return pl.pallas_call(
        body, name=name, grid=(T // tr,), in_specs=[row, row, vec, vec],
        out_specs=[row, row, pl.BlockSpec((tr, 1), lambda i: (i, 0))],
        out_shape=[jax.ShapeDtypeStruct((T, D), F32), jax.ShapeDtypeStruct((T, D), F32),
                   jax.ShapeDtypeStruct((T, 1), F32)],
        compiler_params=pltpu.CompilerParams(dimension_semantics=("parallel",)),
    )(h, f, g, b)


def _ln_bwd(dout, xh, rstd, g, name, after=None):
    T, D = dout.shape
    tr = _pick(T, (ROW_TILE, 128, 64, 32, 16, 8))

    def body(do_ref, xh_ref, rs_ref, g_ref, *rest):
        dz_ref, dg_ref, db_ref = rest[-3:]

        @pl.when(pl.program_id(0) == 0)
        def _():
            dg_ref[...] = jnp.zeros_like(dg_ref)
            db_ref[...] = jnp.zeros_like(db_ref)

        do = do_ref[...]
        xh = xh_ref[...]
        dxh = do * g_ref[...]
        m1 = jnp.mean(dxh, axis=-1, keepdims=True)
        m2 = jnp.mean(dxh * xh, axis=-1, keepdims=True)
        dz_ref[...] = rs_ref[...] * (dxh - m1 - xh * m2)
        dg_ref[...] += jnp.sum(do * xh, axis=0, keepdims=True)
        db_ref[...] += jnp.sum(do, axis=0, keepdims=True)

    row = pl.BlockSpec((tr, D), lambda i: (i, 0))
    vec = pl.BlockSpec((1, D), lambda i: (0, 0))
    in_specs = [row, row, pl.BlockSpec((tr, 1), lambda i: (i, 0)), vec]
    args = [dout, xh, rstd, g]
    if after is not None:
        in_specs.append(pl.BlockSpec(memory_space=pl.ANY))
        args.append(after)
    return pl.pallas_call(
        body, name=name, grid=(T // tr,),
        in_specs=in_specs, out_specs=[row, vec, vec],
        out_shape=[jax.ShapeDtypeStruct((T, D), F32), jax.ShapeDtypeStruct((1, D), F32),
                   jax.ShapeDtypeStruct((1, D), F32)],
        compiler_params=pltpu.CompilerParams(dimension_semantics=("arbitrary",)),
    )(*args)


_GELU_C = math.sqrt(2.0 / math.pi)


def _gelu(x):
    t = jnp.tanh(_GELU_C * (x + 0.044715 * x * x * x))
    return 0.5 * x * (1.0 + t), t


def _gelu_grad(x, t):
    return 0.5 * (1.0 + t) + 0.5 * x * (1.0 - t * t) * _GELU_C * (1.0 + 3 * 0.044715 * x * x)


def _sigmoid(x):
    return 1.0 / (1.0 + jnp.exp(-x))


def _softplus_neg(lam):
    z = jnp.exp(-jnp.abs(lam))
    u = 1.0 + z
    l1p = jnp.where(u == 1.0, z, jnp.log(u) * z / jnp.where(u == 1.0, 1.0, u - 1.0))
    return jnp.maximum(-lam, 0.0) + l1p


def _neg_expm1(x):
    series = x * (1.0 + x * 0.5 * (1.0 + x * (1.0 / 3.0) * (1.0 + x * 0.25 * (1.0 + x * 0.2))))
    return -jnp.where(x > -0.05, series, jnp.exp(x) - 1.0)


def _scan_fwd(a, b):
    n = a.shape[0]
    rows = lax.broadcasted_iota(jnp.int32, a.shape, 0)
    s = 1
    while s < n:
        keep = rows >= s
        b = jnp.where(keep, a * pltpu.roll(b, s, 0) + b, b)
        a = jnp.where(keep, a * pltpu.roll(a, s, 0), a)
        s *= 2
    return a, b


def _scan_bwd(c, b):
    n = c.shape[0]
    rows = lax.broadcasted_iota(jnp.int32, c.shape, 0)
    s = 1
    while s < n:
        keep = rows < n - s
        b = jnp.where(keep, c * pltpu.roll(b, n - s, 0) + b, b)
        c = jnp.where(keep, c * pltpu.roll(c, n - s, 0), c)
        s *= 2
    return c, b


def _rnn_gates(xc, wr, br, wi, bi, sp):
    xb = xc.astype(BF16)
    r = _sigmoid(jnp.dot(xb, wr, preferred_element_type=F32) + br)
    i = _sigmoid(jnp.dot(xb, wi, preferred_element_type=F32) + bi)
    la = -LRU_C * r * sp
    a = jnp.exp(la)
    om = _neg_expm1(2.0 * la)
    mult = jnp.sqrt(om)
    return r, i, a, om, mult


def _rnn_specs(T):
    C = RNN_BLOCK
    col = lambda off: pl.BlockSpec((T, C), lambda n, off=off: (0, off // C + n))
    vec = pl.BlockSpec((1, C), lambda n: (0, n))
    cw = pl.BlockSpec((CONV_WIDTH, C), lambda n: (0, n))
    w = pl.BlockSpec((1, C, C), lambda n: (n, 0, 0))
    own = pl.BlockSpec((T, C), lambda n: (0, n))
    return col, vec, cw, w, own


def _rnn_fwd(P, cw, cb, wrg, brg, wig, big, lam, name):
    T = P.shape[0]
    C = RNN_BLOCK
    tc = _pick(T, (TIME_CHUNK,))
    nch = T // tc

    def body(x_ref, g_ref, cw_ref, cb_ref, wr_ref, br_ref, wi_ref, bi_ref, lam_ref, y_ref, hs_ref, xs_ref):
        sp = _softplus_neg(lam_ref[...])
        wr = wr_ref[0]
        wi = wi_ref[0]
        xs_ref[0:8, :] = jnp.zeros((8, C), F32)

        def chunk(c, hprev):
            r0 = pl.multiple_of(c * tc, tc)
            x = x_ref[pl.ds(r0, tc), :].astype(F32)
            xs_ref[8:, :] = x
            xc = cb_ref[...] + jnp.zeros((tc, C), F32)
            for k in range(CONV_WIDTH):
                xc = xc + xs_ref[pl.ds(8 - (CONV_WIDTH - 1 - k), tc), :] * cw_ref[k:k + 1, :]
            xs_ref[0:8, :] = x[tc - 8:, :]
            r, i, a, om, mult = _rnn_gates(xc, wr, br_ref[...], wi, bi_ref[...], sp)
            acum, bcum = _scan_fwd(a, mult * (i * xc))
            h = acum * hprev + bcum
            hs_ref[pl.ds(r0, tc), :] = h
            ge, _ = _gelu(g_ref[pl.ds(r0, tc), :].astype(F32))
            y_ref[pl.ds(r0, tc), :] = (h * ge).astype(BF16)
            return h[tc - 1:tc, :]

        lax.fori_loop(0, nch, chunk, jnp.zeros((1, C), F32))

    col, vec, cwspec, w, own = _rnn_specs(T)
    return pl.pallas_call(
        body, name=name, grid=(RNN_BLOCKS,),
        in_specs=[col(C_XR), col(C_GR), cwspec, vec, w, vec, w, vec, vec],
        out_specs=[own, own],
        out_shape=[jax.ShapeDtypeStruct((T, D_MODEL), BF16), jax.ShapeDtypeStruct((T, D_MODEL), F32)],
        scratch_shapes=[pltpu.VMEM((tc + 8, C), F32)],
        compiler_params=pltpu.CompilerParams(dimension_semantics=("parallel",)),
    )(P, P, cw, cb, wrg, brg, wig, big, lam)


def _rnn_bwd(P, hs, dy, cw, cb, wrg, brg, wig, big, lam, name):
    T = P.shape[0]
    C = RNN_BLOCK
    tc = _pick(T, (TIME_CHUNK,))
    nch = T // tc

    def body(x_ref, g_ref, hs_ref, dy_ref, cw_ref, cb_ref, wr_ref, br_ref, wi_ref, bi_ref, lam_ref,
             dx_ref, dg_ref, dcw_ref, dcb_ref, dwr_ref, dbr_ref, dwi_ref, dbi_ref, dlam_ref,
             xs_ref, hp_ref, an_ref, dn_ref):
        lam_v = lam_ref[...]
        sp = _softplus_neg(lam_v)
        wr = wr_ref[0]
        wi = wi_ref[0]
        dcw_ref[...] = jnp.zeros_like(dcw_ref)
        dcb_ref[...] = jnp.zeros_like(dcb_ref)
        dwr_ref[...] = jnp.zeros_like(dwr_ref)
        dbr_ref[...] = jnp.zeros_like(dbr_ref)
        dwi_ref[...] = jnp.zeros_like(dwi_ref)
        dbi_ref[...] = jnp.zeros_like(dbi_ref)
        dlam_ref[...] = jnp.zeros_like(dlam_ref)
        an_ref[tc:, :] = jnp.zeros((8, C), F32)
        dn_ref[tc:, :] = jnp.zeros((8, C), F32)

        def chunk(step, gnext):
            c = nch - 1 - step
            r0 = pl.multiple_of(c * tc, tc)
            p0 = pl.multiple_of(jnp.maximum(r0 - 8, 0), 8)
            q0 = pl.multiple_of(jnp.maximum(r0 - 16, 0), 16)
            live = c > 0
            x = x_ref[pl.ds(r0, tc), :].astype(F32)
            xs_ref[0:8, :] = jnp.where(live, x_ref[pl.ds(q0, 16), :].astype(F32)[8:, :], 0.0)
            xs_ref[8:, :] = x
            xsh = [xs_ref[pl.ds(8 - (CONV_WIDTH - 1 - k), tc), :] for k in range(CONV_WIDTH)]
            xc = cb_ref[...] + jnp.zeros((tc, C), F32)
            for k in range(CONV_WIDTH):
                xc = xc + xsh[k] * cw_ref[k:k + 1, :]
            r, i, a, om, mult = _rnn_gates(xc, wr, br_ref[...], wi, bi_ref[...], sp)
            h = hs_ref[pl.ds(r0, tc), :]
            hp_ref[0:8, :] = jnp.where(live, hs_ref[pl.ds(p0, 8), :], 0.0)
            hp_ref[8:, :] = h
            hm1 = hp_ref[pl.ds(7, tc), :]
            g = g_ref[pl.ds(r0, tc), :].astype(F32)
            ge, th = _gelu(g)
            dy = dy_ref[pl.ds(r0, tc), :]
            dg_ref[pl.ds(r0, tc), :] = (dy * h * _gelu_grad(g, th)).astype(BF16)
            an_ref[0:tc, :] = a
            coef = an_ref[pl.ds(1, tc), :]
            ccum, bcum = _scan_bwd(coef, dy * ge)
            G = bcum + ccum * gnext
            an_ref[tc:, :] = a[0:8, :]
            da = G * hm1
            ixc = i * xc
            dmult = G * ixc
            di = G * mult * xc
            dxc = G * mult * i
            dla = da * a - dmult * (1.0 - om) / mult
            dr = dla * (-LRU_C * sp)
            dlam_ref[...] += jnp.sum(dla * r, axis=0, keepdims=True)
            dzr = dr * r * (1.0 - r)
            dzi = di * i * (1.0 - i)
            dbr_ref[...] += jnp.sum(dzr, axis=0, keepdims=True)
            dbi_ref[...] += jnp.sum(dzi, axis=0, keepdims=True)
            xb = xc.astype(BF16)
            dzrb = dzr.astype(BF16)
            dzib = dzi.astype(BF16)
            dwr_ref[0] += lax.dot_general(xb, dzrb, _DIMS["tn"], preferred_element_type=F32)
            dwi_ref[0] += lax.dot_general(xb, dzib, _DIMS["tn"], preferred_element_type=F32)
            dxc = dxc + lax.dot_general(dzrb, wr, _DIMS["nt"], preferred_element_type=F32)
            dxc = dxc + lax.dot_general(dzib, wi, _DIMS["nt"], preferred_element_type=F32)
            dcb_ref[...] += jnp.sum(dxc, axis=0, keepdims=True)
            for k in range(CONV_WIDTH):
                dcw_ref[k:k + 1, :] += jnp.sum(dxc * xsh[k], axis=0, keepdims=True)
            dn_ref[0:tc, :] = dxc
            dx = jnp.zeros((tc, C), F32)
            for k in range(CONV_WIDTH):
                dx = dx + dn_ref[pl.ds(CONV_WIDTH - 1 - k, tc), :] * cw_ref[k:k + 1, :]
            dn_ref[tc:, :] = dxc[0:8, :]
            dx_ref[pl.ds(r0, tc), :] = dx.astype(BF16)
            return G[0:1, :]

        lax.fori_loop(0, nch, chunk, jnp.zeros((1, C), F32))
        dlam_ref[...] = dlam_ref[...] * (LRU_C * _sigmoid(-lam_v))

    col, vec, cwspec, w, own = _rnn_specs(T)
    vshape = jax.ShapeDtypeStruct((1, D_MODEL), F32)
    wshape = jax.ShapeDtypeStruct((RNN_BLOCKS, C, C), F32)
    return pl.pallas_call(
        body, name=name, grid=(RNN_BLOCKS,),
        in_specs=[col(C_XR), col(C_GR), own, own, cwspec, vec, w, vec, w, vec, vec],
        out_specs=[own, own, cwspec, vec, w, vec, w, vec, vec],
        out_shape=[jax.ShapeDtypeStruct((T, D_MODEL), BF16), jax.ShapeDtypeStruct((T, D_MODEL), BF16),
                   jax.ShapeDtypeStruct((CONV_WIDTH, D_MODEL), F32), vshape, wshape, vshape, wshape, vshape, vshape],
        scratch_shapes=[pltpu.VMEM((tc + 8, C), F32), pltpu.VMEM((tc + 8, C), F32),
                        pltpu.VMEM((tc + 8, C), F32), pltpu.VMEM((tc + 8, C), F32)],
        compiler_params=pltpu.CompilerParams(dimension_semantics=("parallel",)),
    )(P, P, hs, dy, cw, cb, wrg, brg, wig, big, lam)


def _rope_table(T):
    half = ROT_DIM // 2
    pos = jnp.arange(T, dtype=F32)
    inv_freq = ROPE_THETA ** (-jnp.arange(0, ROT_DIM, 2, dtype=F32) / ROT_DIM)
    ang = pos[:, None] * inv_freq[None, :]
    cos, sin = jnp.cos(ang), jnp.sin(ang)
    one = jnp.ones((T, HEAD_DIM - ROT_DIM), F32)
    zero = jnp.zeros((T, HEAD_DIM - ROT_DIM), F32)
    z8 = jnp.zeros((T, half), F32)
    c = jnp.concatenate([cos, cos, one], axis=1)
    a = jnp.concatenate([-sin, z8, zero], axis=1)
    b = jnp.concatenate([z8, sin, zero], axis=1)
    return jnp.stack([jnp.tile(c, (1, 2)), jnp.tile(a, (1, 2)), jnp.tile(b, (1, 2))])


def _rope(x, tab, sign):
    W = x.shape[1]
    rep = W // 128
    c = jnp.tile(tab[0], (1, rep)) if rep > 1 else tab[0]
    a = jnp.tile(tab[1], (1, rep)) if rep > 1 else tab[1]
    b = jnp.tile(tab[2], (1, rep)) if rep > 1 else tab[2]
    return x * c + sign * (pltpu.roll(x, W - ROT_DIM // 2, 1) * a + pltpu.roll(x, ROT_DIM // 2, 1) * b)


def _swa_mask(n):
    rows = lax.broadcasted_iota(jnp.int32, (GROUP * BLOCK, 2 * BLOCK), 0) & (BLOCK - 1)
    cols = lax.broadcasted_iota(jnp.int32, (GROUP * BLOCK, 2 * BLOCK), 1)
    return (cols > rows) & (cols <= rows + BLOCK) & ((n > 0) | (cols >= BLOCK))


def _swa_probs(qg, k2, sink, valid):
    s = lax.dot_general(qg, k2, _DIMS["nt"], preferred_element_type=F32) * (HEAD_DIM ** -0.5)
    s = jnp.where(valid, s, NEG_INF)
    m = jnp.maximum(jnp.max(s, axis=1, keepdims=True), sink)
    p = jnp.exp(s - m)
    ps = jnp.exp(sink - m)
    inv = 1.0 / (jnp.sum(p, axis=1, keepdims=True) + ps)
    return p * inv, ps * inv


def _swa_specs(T):
    nb = T // BLOCK
    qspec = pl.BlockSpec((BLOCK, D_MODEL), lambda n: (n, C_Q // D_MODEL))
    cur = lambda off: pl.BlockSpec((BLOCK, KV_WIDTH), lambda n, off=off: (n, off // KV_WIDTH))
    prev = lambda off: pl.BlockSpec((BLOCK, KV_WIDTH), lambda n, off=off: (jnp.maximum(n - 1, 0), off // KV_WIDTH))
    tcur = pl.BlockSpec((3, BLOCK, 128), lambda n: (0, n, 0))
    tprev = pl.BlockSpec((3, BLOCK, 128), lambda n: (0, jnp.maximum(n - 1, 0), 0))
    sink = pl.BlockSpec((N_KV_HEADS, GROUP * BLOCK, 1), lambda n: (0, 0, 0))
    own = pl.BlockSpec((BLOCK, D_MODEL), lambda n: (n, 0))
    return nb, qspec, cur, prev, tcur, tprev, sink, own


def _stack_heads(x, hk):
    return jnp.concatenate([x[:, (hk * GROUP + g) * HEAD_DIM:(hk * GROUP + g + 1) * HEAD_DIM] for g in range(GROUP)],
                           axis=0)


def _swa_fwd(P, tab, sink_col, name):
    T = P.shape[0]
    nb, qspec, cur, prev, tcur, tprev, sink, own = _swa_specs(T)

    def body(q_ref, kc_ref, kp_ref, vc_ref, vp_ref, tc_ref, tp_ref, sk_ref, o_ref):
        n = pl.program_id(0)
        valid = _swa_mask(n)
        q = _rope(q_ref[...].astype(F32), tc_ref[...], 1.0).astype(BF16)
        k2 = jnp.concatenate([_rope(kp_ref[...].astype(F32), tp_ref[...], 1.0),
                              _rope(kc_ref[...].astype(F32), tc_ref[...], 1.0)], axis=0).astype(BF16)
        v2 = jnp.concatenate([vp_ref[...], vc_ref[...]], axis=0).astype(BF16)
        parts = []
        for hk in range(N_KV_HEADS):
            sl = slice(hk * HEAD_DIM, (hk + 1) * HEAD_DIM)
            pn, _ = _swa_probs(_stack_heads(q, hk), k2[:, sl], sk_ref[hk], valid)
            og = jnp.dot(pn.astype(BF16), v2[:, sl], preferred_element_type=F32)
            parts += [og[g * BLOCK:(g + 1) * BLOCK, :] for g in range(GROUP)]
        o_ref[...] = jnp.concatenate(parts, axis=1).astype(BF16)

    return pl.pallas_call(
        body, name=name, grid=(nb,),
        in_specs=[qspec, cur(C_K), prev(C_K), cur(C_V), prev(C_V), tcur, tprev, sink],
        out_specs=own, out_shape=jax.ShapeDtypeStruct((T, D_MODEL), BF16),
        compiler_params=pltpu.CompilerParams(dimension_semantics=("parallel",)),
    )(P, P, P, P, P, tab, tab, sink_col)


def _swa_bwd(P, do, tab, sink_col, name):
    T = P.shape[0]
    nb, qspec, cur, prev, tcur, tprev, sink, own = _swa_specs(T)

    def body(q_ref, kc_ref, kp_ref, vc_ref, vp_ref, do_ref, tc_ref, tp_ref, sk_ref,
             dq_ref, dk_ref, dv_ref, ds_ref):
        n = pl.program_id(0)

        @pl.when(n == 0)
        def _():
            dk_ref[...] = jnp.zeros_like(dk_ref)
            dv_ref[...] = jnp.zeros_like(dv_ref)
            ds_ref[...] = jnp.zeros_like(ds_ref)

        valid = _swa_mask(n)
        tcur_v = tc_ref[...]
        tprev_v = tp_ref[...]
        q = _rope(q_ref[...].astype(F32), tcur_v, 1.0).astype(BF16)
        k2 = jnp.concatenate([_rope(kp_ref[...].astype(F32), tprev_v, 1.0),
                              _rope(kc_ref[...].astype(F32), tcur_v, 1.0)], axis=0).astype(BF16)
        v2 = jnp.concatenate([vp_ref[...], vc_ref[...]], axis=0).astype(BF16)
        dob = do_ref[...].astype(BF16)
        dq_parts = []
        dk_parts = []
        dv_parts = []
        for hk in range(N_KV_HEADS):
            sl = slice(hk * HEAD_DIM, (hk + 1) * HEAD_DIM)
            qg = _stack_heads(q, hk)
            dog = _stack_heads(dob, hk)
            pn, psn = _swa_probs(qg, k2[:, sl], sk_ref[hk], valid)
            dp = lax.dot_general(dog, v2[:, sl], _DIMS["nt"], preferred_element_type=F32)
            delta = jnp.sum(pn * dp, axis=1, keepdims=True)
            dsc = (pn * (dp - delta) * (HEAD_DIM ** -0.5)).astype(BF16)
            dsink = -psn * delta
            for g in range(GROUP):
                ds_ref[hk, g:g + 1, :] += jnp.broadcast_to(
                    jnp.sum(dsink[g * BLOCK:(g + 1) * BLOCK], axis=0, keepdims=True), (1, 128))
            dqg = jnp.dot(dsc, k2[:, sl], preferred_element_type=F32)
            dq_parts += [dqg[g * BLOCK:(g + 1) * BLOCK, :] for g in range(GROUP)]
            dk_parts.append(lax.dot_general(qg, dsc, _DIMS["tn"], preferred_element_type=F32).T)
            dv_parts.append(lax.dot_general(dog, pn.astype(BF16), _DIMS["tn"], preferred_element_type=F32).T)
        dq_ref[...] = _rope(jnp.concatenate(dq_parts, axis=1), tcur_v, -1.0).astype(BF16)
        dk2 = jnp.concatenate(dk_parts, axis=1)
        dv2 = jnp.concatenate(dv_parts, axis=1)
        c0 = pl.multiple_of(n * BLOCK, BLOCK)
        p0 = pl.multiple_of(jnp.maximum(n - 1, 0) * BLOCK, BLOCK)
        dk_ref[pl.ds(p0, BLOCK), :] += _rope(dk2[:BLOCK], tprev_v, -1.0)
        dv_ref[pl.ds(p0, BLOCK), :] += dv2[:BLOCK]
        dk_ref[pl.ds(c0, BLOCK), :] += _rope(dk2[BLOCK:], tcur_v, -1.0)
        dv_ref[pl.ds(c0, BLOCK), :] += dv2[BLOCK:]

    full = pl.BlockSpec((T, KV_WIDTH), lambda n: (0, 0))
    return pl.pallas_call(
        body, name=name, grid=(nb,),
        in_specs=[qspec, cur(C_K), prev(C_K), cur(C_V), prev(C_V), own, tcur, tprev, sink],
        out_specs=[own, full, full, pl.BlockSpec((N_KV_HEADS, GROUP, 128), lambda n: (0, 0, 0))],
        out_shape=[jax.ShapeDtypeStruct((T, D_MODEL), BF16), jax.ShapeDtypeStruct((T, KV_WIDTH), F32),
                   jax.ShapeDtypeStruct((T, KV_WIDTH), F32), jax.ShapeDtypeStruct((N_KV_HEADS, GROUP, 128), F32)],
        compiler_params=pltpu.CompilerParams(dimension_semantics=("arbitrary",)),
    )(P, P, P, P, P, do, tab, tab, sink_col)


_MW = 256
_FW = 1408


def _gate_specs(T, rows, width):
    tr = _pick(T, (rows, 256, 128, 64, 32, 16, 8))
    col = lambda off: pl.BlockSpec((tr, width), lambda i, j, off=off: (i, off // width + j))
    own = pl.BlockSpec((tr, width), lambda i, j: (i, j))
    return tr, col, own


def _merge_fwd(P, mr, ma, name):
    T = P.shape[0]
    tr, col, own = _gate_specs(T, 1024, _MW)

    def body(gr_ref, ga_ref, mr_ref, ma_ref, o_ref):
        o_ref[...] = (_sigmoid(gr_ref[...].astype(F32)) * mr_ref[...]
                      + _sigmoid(ga_ref[...].astype(F32)) * ma_ref[...]).astype(BF16)

    return pl.pallas_call(
        body, name=name, grid=(T // tr, D_MODEL // _MW), in_specs=[col(C_GRNN), col(C_GATTN), own, own],
        out_specs=own, out_shape=jax.ShapeDtypeStruct((T, D_MODEL), BF16),
        compiler_params=pltpu.CompilerParams(dimension_semantics=("parallel", "parallel")),
    )(P, P, mr, ma)


def _merge_bwd(P, mr, ma, dm, name):
    T = P.shape[0]
    tr, col, own = _gate_specs(T, 512, _MW)

    def body(gr_ref, ga_ref, mr_ref, ma_ref, dm_ref, dmr_ref, dma_ref, dgr_ref, dga_ref):
        dm = dm_ref[...]
        sr = _sigmoid(gr_ref[...].astype(F32))
        sa = _sigmoid(ga_ref[...].astype(F32))
        dmr_ref[...] = (dm * sr).astype(BF16)
        dma_ref[...] = (dm * sa).astype(BF16)
        dgr_ref[...] = (dm * mr_ref[...] * sr * (1.0 - sr)).astype(BF16)
        dga_ref[...] = (dm * ma_ref[...] * sa * (1.0 - sa)).astype(BF16)

    shp = jax.ShapeDtypeStruct((T, D_MODEL), BF16)
    return pl.pallas_call(
        body, name=name, grid=(T // tr, D_MODEL // _MW), in_specs=[col(C_GRNN), col(C_GATTN), own, own, own],
        out_specs=[own] * 4, out_shape=[shp] * 4,
        compiler_params=pltpu.CompilerParams(dimension_semantics=("parallel", "parallel")),
    )(P, P, mr, ma, dm)


def _swiglu_fwd(U, name):
    T = U.shape[0]
    tr, col, own = _gate_specs(T, 256, _FW)

    def body(g_ref, u_ref, o_ref):
        g = g_ref[...].astype(F32)
        o_ref[...] = (g * _sigmoid(g) * u_ref[...].astype(F32)).astype(BF16)

    return pl.pallas_call(
        body, name=name, grid=(T // tr, D_FF // _FW), in_specs=[col(0), col(D_FF)],
        out_specs=own, out_shape=jax.ShapeDtypeStruct((T, D_FF), BF16),
        compiler_params=pltpu.CompilerParams(dimension_semantics=("parallel", "parallel")),
    )(U, U)


def _swiglu_bwd(U, dact, name):
    T = U.shape[0]
    nf = D_FF // _FW
    tr = _pick(T, (256, 128, 64, 32, 16, 8))

    def body(g_ref, u_ref, da_ref, o_ref):
        g = g_ref[...].astype(F32)
        da = da_ref[...].astype(F32)
        s = _sigmoid(g)
        is_gate = pl.program_id(1) < nf
        o_ref[...] = jnp.where(is_gate, da * u_ref[...].astype(F32) * s * (1.0 + g * (1.0 - s)), da * g * s).astype(BF16)

    return pl.pallas_call(
        body, name=name, grid=(T // tr, 2 * nf),
        in_specs=[pl.BlockSpec((tr, _FW), lambda i, j: (i, j % nf)),
                  pl.BlockSpec((tr, _FW), lambda i, j: (i, nf + j % nf)),
                  pl.BlockSpec((tr, _FW), lambda i, j: (i, j % nf))],
        out_specs=pl.BlockSpec((tr, _FW), lambda i, j: (i, j)),
        out_shape=jax.ShapeDtypeStruct((T, 2 * D_FF), BF16),
        compiler_params=pltpu.CompilerParams(dimension_semantics=("parallel", "parallel")),
    )(U, U, dact)


def _cross_probs(qh, kh):
    s = lax.dot_general(qh, kh, _DIMS["nt"], preferred_element_type=F32) * (CROSS_HEAD_DIM ** -0.5)
    p = jnp.exp(s - jnp.max(s, axis=1, keepdims=True))
    return p / jnp.sum(p, axis=1, keepdims=True)


def _cross_fwd(q, kv, name):
    T = q.shape[0]
    M = kv.shape[0]
    tr = _pick(T, (ROW_TILE, 128, 64, 32, 16, 8))
    W = CROSS_HEAD_DIM

    def body(q_ref, kv_ref, o_ref):
        for h in range(CROSS_HEADS):
            qh = q_ref[:, h * W:(h + 1) * W].astype(BF16)
            kh = kv_ref[:, h * W:(h + 1) * W].astype(BF16)
            vh = kv_ref[:, D_MODEL + h * W:D_MODEL + (h + 1) * W].astype(BF16)
            pn = _cross_probs(qh, kh)
            o_ref[:, h * W:(h + 1) * W] = jnp.dot(pn.astype(BF16), vh, preferred_element_type=F32).astype(BF16)

    row = pl.BlockSpec((tr, D_MODEL), lambda i: (i, 0))
    return pl.pallas_call(
        body, name=name, grid=(T // tr,), in_specs=[row, pl.BlockSpec((M, 2 * D_MODEL), lambda i: (0, 0))],
        out_specs=row, out_shape=jax.ShapeDtypeStruct((T, D_MODEL), BF16),
        compiler_params=pltpu.CompilerParams(dimension_semantics=("parallel",)),
    )(q, kv)


def _cross_bwd(q, kv, do, name):
    T = q.shape[0]
    M = kv.shape[0]
    tr = _pick(T, (ROW_TILE, 128, 64, 32, 16, 8))
    W = CROSS_HEAD_DIM

    def body(q_ref, kv_ref, do_ref, dq_ref, dkv_ref):
        @pl.when(pl.program_id(0) == 0)
        def _():
            dkv_ref[...] = jnp.zeros_like(dkv_ref)

        for h in range(CROSS_HEADS):
            qh = q_ref[:, h * W:(h + 1) * W].astype(BF16)
            kh = kv_ref[:, h * W:(h + 1) * W].astype(BF16)
            vh = kv_ref[:, D_MODEL + h * W:D_MODEL + (h + 1) * W].astype(BF16)
            doh = do_ref[:, h * W:(h + 1) * W].astype(BF16)
            pn = _cross_probs(qh, kh)
            dp = lax.dot_general(doh, vh, _DIMS["nt"], preferred_element_type=F32)
            delta = jnp.sum(pn * dp, axis=1, keepdims=True)
            dsc = (pn * (dp - delta) * (W ** -0.5)).astype(BF16)
            dq_ref[:, h * W:(h + 1) * W] = jnp.dot(dsc, kh, preferred_element_type=F32).astype(BF16)
            dkv_ref[:, h * W:(h + 1) * W] += lax.dot_general(dsc, qh, _DIMS["tn"], preferred_element_type=F32)
            dkv_ref[:, D_MODEL + h * W:D_MODEL + (h + 1) * W] += lax.dot_general(
                pn.astype(BF16), doh, _DIMS["tn"], preferred_element_type=F32)

    row = pl.BlockSpec((tr, D_MODEL), lambda i: (i, 0))
    full = pl.BlockSpec((M, 2 * D_MODEL), lambda i: (0, 0))
    return pl.pallas_call(
        body, name=name, grid=(T // tr,), in_specs=[row, full, row], out_specs=[row, full],
        out_shape=[jax.ShapeDtypeStruct((T, D_MODEL), BF16), jax.ShapeDtypeStruct((M, 2 * D_MODEL), F32)],
        compiler_params=pltpu.CompilerParams(dimension_semantics=("arbitrary",)),
    )(q, kv, do)


def _loss_head(y, target, name):
    T, D = y.shape
    tr = _pick(T, (ROW_TILE, 128, 64, 32, 16, 8))

    def body(y_ref, t_ref, l_ref, dy_ref):
        @pl.when(pl.program_id(0) == 0)
        def _():
            l_ref[...] = jnp.zeros_like(l_ref)

        err = y_ref[...] - t_ref[...]
        dy_ref[...] = err * (1.0 / D)
        l_ref[...] += jnp.broadcast_to(0.5 * jnp.sum(jnp.mean(err * err, axis=-1, keepdims=True), axis=0, keepdims=True),
                                       (8, 128))

    row = pl.BlockSpec((tr, D), lambda i: (i, 0))
    return pl.pallas_call(
        body, name=name, grid=(T // tr,), in_specs=[row, row],
        out_specs=[pl.BlockSpec((8, 128), lambda i: (0, 0)), row],
        out_shape=[jax.ShapeDtypeStruct((8, 128), F32), jax.ShapeDtypeStruct((T, D), F32)],
        compiler_params=pltpu.CompilerParams(dimension_semantics=("arbitrary",)),
    )(y, target)


def _sum_slots(recv, name, mine=None):
    _, R, C = recv.shape
    tr = _pick(R, (ROW_TILE, 224, 368, 128, 64, 32, 16, 8))

    def body(*refs):
        r_ref, o_ref = refs[0], refs[-1]
        if mine is None:
            acc = r_ref[0].astype(F32)
            for d in range(1, N_DEV):
                acc = acc + r_ref[d].astype(F32)
        else:
            me = _my_slot()
            acc = refs[1][...].astype(F32)
            for d in range(N_DEV):
                acc = acc + jnp.where(d == me, 0.0, r_ref[d].astype(F32))
        o_ref[...] = acc

    in_specs = [pl.BlockSpec((N_DEV, tr, C), lambda i: (0, i, 0))]
    args = [recv]
    if mine is not None:
        in_specs.append(pl.BlockSpec((tr, C), lambda i: (i, 0)))
        args.append(mine)
    return pl.pallas_call(
        body, name=name, grid=(R // tr,), in_specs=in_specs,
        out_specs=pl.BlockSpec((tr, C), lambda i: (i, 0)), out_shape=jax.ShapeDtypeStruct((R, C), F32),
        compiler_params=pltpu.CompilerParams(dimension_semantics=("parallel",)),
    )(*args)


def _adamw(w, g, m, v, name):
    shape = w.shape
    C = shape[-1]
    R = math.prod(shape[:-1])
    w2, g2, m2, v2 = (t.reshape(R, C) for t in (w, g, m, v))
    tr = _pick(R, (ROW_TILE, 128, 64, 32, 16, 8))

    def body(w_ref, g_ref, m_ref, v_ref, d_ref, mo_ref, vo_ref):
        gg = g_ref[...]
        mn = ADAM_B1 * m_ref[...] + (1.0 - ADAM_B1) * gg
        vn = ADAM_B2 * v_ref[...] + (1.0 - ADAM_B2) * (gg * gg)
        m_hat = mn / (1.0 - ADAM_B1 ** ADAM_STEP)
        v_hat = vn / (1.0 - ADAM_B2 ** ADAM_STEP)
        d_ref[...] = -ADAM_LR * (m_hat / (jnp.sqrt(v_hat) + ADAM_EPS) + ADAM_WD * w_ref[...])
        mo_ref[...] = mn
        vo_ref[...] = vn

    blk = pl.BlockSpec((tr, C), lambda i: (i, 0))
    shp = jax.ShapeDtypeStruct((R, C), F32)
    d, mo, vo = pl.pallas_call(
        body, name=name, grid=(R // tr,), in_specs=[blk] * 4, out_specs=[blk] * 3, out_shape=[shp] * 3,
        compiler_params=pltpu.CompilerParams(dimension_semantics=("parallel",)),
    )(w2, g2, m2, v2)
    return d.reshape(shape), mo.reshape(shape), vo.reshape(shape)


def _all_gather_many(bufs, name):
    n = len(bufs)

    def body(*refs):
        xs, outs = refs[:n], refs[n:2 * n]
        send_sems, recv_sems, local_sems = refs[2 * n:]
        x, y, c = lax.axis_index("x"), lax.axis_index("y"), lax.axis_index("c")
        me, sibling = (x, y, c), (x, y, 1 - c)
        chips = [(1 - x, y), (x, 1 - y), (1 - x, 1 - y)]

        def slot(i, px, py, pc):
            return outs[i].at[4 * px + 2 * py + pc]

        def copy(i, k, block, to, src=None):
            return pltpu.make_async_remote_copy(
                src_ref=slot(i, *block) if src is None else src, dst_ref=slot(i, *block),
                send_sem=send_sems.at[7 * i + k], recv_sem=recv_sems.at[7 * i + k],
                device_id=to, device_id_type=pl.DeviceIdType.MESH)

        mine = [pltpu.make_async_copy(xs[i], slot(i, *me), local_sems.at[i]) for i in range(n)]
        for cp in mine:
            cp.start()
        first = [copy(i, 0, me, sibling, src=xs[i]) for i in range(n)]
        for j, chip in enumerate(chips):
            first += [copy(i, 1 + j, me, (*chip, c), src=xs[i]) for i in range(n)]
        for cp in first:
            cp.start()
        passed = []
        for j, chip in enumerate(chips):
            for i in range(n):
                copy(i, 1 + j, (*chip, c), me).wait_recv()
                passed.append(copy(i, 4 + j, (*chip, c), sibling))
                passed[-1].start()
        for i in range(n):
            copy(i, 0, sibling, me).wait_recv()
        for j, chip in enumerate(chips):
            for i in range(n):
                copy(i, 4 + j, (*chip, 1 - c), me).wait_recv()
        for cp in first + passed:
            cp.wait_send()
        for cp in mine:
            cp.wait()

    hbm = pl.BlockSpec(memory_space=pl.ANY)
    return pl.pallas_call(
        body, name=name, out_shape=[jax.ShapeDtypeStruct((N_DEV,) + b.shape, b.dtype) for b in bufs],
        in_specs=[hbm] * n, out_specs=[hbm] * n,
        scratch_shapes=[pltpu.SemaphoreType.DMA((7 * n,)), pltpu.SemaphoreType.DMA((7 * n,)),
                        pltpu.SemaphoreType.DMA((n,))],
    )(*bufs)


def _all_gather(buf, name):
    return _all_gather_many([buf], name)[0]


_HBM = pl.BlockSpec(memory_space=pltpu.HBM)
_SEM = pl.BlockSpec(memory_space=pltpu.SEMAPHORE)
_EFFECT = pltpu.SideEffectType.DATAFLOW_SIDE_EFFECTING


def _peer(k):
    x, y, c = lax.axis_index("x"), lax.axis_index("y"), lax.axis_index("c")
    return x ^ ((k >> 2) & 1), y ^ ((k >> 1) & 1), c ^ (k & 1)


def _my_slot():
    return 4 * lax.axis_index("x") + 2 * lax.axis_index("y") + lax.axis_index("c")


def _split_copy(src_refs, land_refs, send_sems, recv_sems, i, k):
    px, py, pc = _peer(k)
    return pltpu.make_async_remote_copy(
        src_ref=src_refs[i].at[4 * px + 2 * py + pc], dst_ref=land_refs[i].at[_my_slot()],
        send_sem=send_sems.at[7 * i + k - 1], recv_sem=recv_sems.at[7 * i + k - 1],
        device_id=(px, py, pc), device_id_type=pl.DeviceIdType.MESH)


def _split_start(srcs, lands, name):
    n = len(srcs)

    def body(*refs):
        src_refs, land_refs = refs[:n], refs[n:2 * n]
        send_sems, recv_sems = refs[2 * n], refs[2 * n + 1]
        token = refs[-1]
        for i in range(n):
            for k in range(1, N_DEV):
                _split_copy(src_refs, land_refs, send_sems, recv_sems, i, k).start()
        token[...] = jnp.zeros_like(token)

    outs = pl.pallas_call(
        body, name=name,
        out_shape=(pltpu.SemaphoreType.DMA((7 * n,)), pltpu.SemaphoreType.DMA((7 * n,)),
                   *[pltpu.HBM(a.shape, a.dtype) for a in srcs], *[pltpu.HBM(a.shape, a.dtype) for a in lands],
                   jax.ShapeDtypeStruct((8, 128), F32)),
        in_specs=[_HBM] * (2 * n),
        out_specs=(_SEM, _SEM, *([_HBM] * (2 * n)), pl.BlockSpec(memory_space=pltpu.VMEM)),
        input_output_aliases={i: 2 + i for i in range(2 * n)},
        compiler_params=pltpu.CompilerParams(has_side_effects=_EFFECT),
    )(*[pltpu.with_memory_space_constraint(a, pltpu.HBM) for a in list(srcs) + list(lands)])
    return outs[0], outs[1], outs[2:2 + n], outs[2 + n:2 + 2 * n], outs[-1]


def _split_wait(send_sems, recv_sems, srcs, lands, after, name):
    n = len(srcs)

    def body(*refs):
        src_refs, land_refs = refs[:n], refs[n:2 * n]
        ssem, rsem = refs[2 * n], refs[2 * n + 1]
        for i in range(n):
            for k in range(1, N_DEV):
                cp = _split_copy(src_refs, land_refs, ssem, rsem, i, k)
                cp.wait_send()
                cp.wait_recv()

    outs = pl.pallas_call(
        body, name=name,
        out_shape=(*[pltpu.HBM(a.shape, a.dtype) for a in srcs], *[pltpu.HBM(a.shape, a.dtype) for a in lands]),
        in_specs=[*([_HBM] * (2 * n)), _SEM, _SEM, pl.BlockSpec(memory_space=pl.ANY)],
        out_specs=tuple([_HBM] * (2 * n)),
        input_output_aliases={i: i for i in range(2 * n)},
        compiler_params=pltpu.CompilerParams(has_side_effects=_EFFECT),
    )(*srcs, *lands, send_sems, recv_sems, after)
    return outs[:n], outs[n:]


def _gather_first(src_refs, land_refs, send_sems, recv_sems, i, k):
    x, y, c = lax.axis_index("x"), lax.axis_index("y"), lax.axis_index("c")
    to = ((x, y, 1 - c), (1 - x, y, c), (x, 1 - y, c), (1 - x, 1 - y, c))[k]
    return pltpu.make_async_remote_copy(
        src_ref=src_refs[i], dst_ref=land_refs[i].at[_my_slot()],
        send_sem=send_sems.at[4 * i + k], recv_sem=recv_sems.at[4 * i + k],
        device_id=to, device_id_type=pl.DeviceIdType.MESH)


def _gather_second(land_refs, send_sems, recv_sems, i, j):
    x, y, c = lax.axis_index("x"), lax.axis_index("y"), lax.axis_index("c")
    px, py = ((1 - x, y), (x, 1 - y), (1 - x, 1 - y))[j]
    slot = land_refs[i].at[4 * px + 2 * py + c]
    return pltpu.make_async_remote_copy(
        src_ref=slot, dst_ref=slot, send_sem=send_sems.at[3 * i + j], recv_sem=recv_sems.at[3 * i + j],
        device_id=(x, y, 1 - c), device_id_type=pl.DeviceIdType.MESH)


def _gather_start(srcs, lands, name, after):
    n = len(srcs)

    def body(*refs):
        src_refs, land_refs = refs[:n], refs[n:2 * n]
        send_sems, recv_sems = refs[2 * n + 1], refs[2 * n + 2]
        token = refs[-1]
        for k in range(4):
            for i in range(n):
                _gather_first(src_refs, land_refs, send_sems, recv_sems, i, k).start()
        token[...] = jnp.zeros_like(token)

    outs = pl.pallas_call(
        body, name=name,
        out_shape=(pltpu.SemaphoreType.DMA((4 * n,)), pltpu.SemaphoreType.DMA((4 * n,)),
                   *[pltpu.HBM(a.shape, a.dtype) for a in srcs], *[pltpu.HBM(a.shape, a.dtype) for a in lands],
                   jax.ShapeDtypeStruct((8, 128), F32)),
        in_specs=[_HBM] * (2 * n) + [pl.BlockSpec(memory_space=pl.ANY)],
        out_specs=(_SEM, _SEM, *([_HBM] * (2 * n)), pl.BlockSpec(memory_space=pltpu.VMEM)),
        input_output_aliases={i: 2 + i for i in range(2 * n)},
        compiler_params=pltpu.CompilerParams(has_side_effects=_EFFECT),
    )(*[pltpu.with_memory_space_constraint(a, pltpu.HBM) for a in list(srcs) + list(lands)], after)
    return outs[0], outs[1], outs[2:2 + n], outs[2 + n:2 + 2 * n], outs[-1]


def _gather_forward(send1, recv1, srcs, lands, after, name):
    n = len(srcs)

    def body(*refs):
        src_refs, land_refs = refs[:n], refs[n:2 * n]
        s1, r1 = refs[2 * n], refs[2 * n + 1]
        s2, r2 = refs[2 * n + 3], refs[2 * n + 4]
        token = refs[-1]
        for j in range(3):
            for i in range(n):
                _gather_first(src_refs, land_refs, s1, r1, i, 1 + j).wait_recv()
                _gather_second(land_refs, s2, r2, i, j).start()
        for i in range(n):
            _gather_first(src_refs, land_refs, s1, r1, i, 0).wait_recv()
            for k in range(4):
                _gather_first(src_refs, land_refs, s1, r1, i, k).wait_send()
        token[...] = jnp.zeros_like(token)

    outs = pl.pallas_call(
        body, name=name,
        out_shape=(pltpu.SemaphoreType.DMA((3 * n,)), pltpu.SemaphoreType.DMA((3 * n,)),
                   *[pltpu.HBM(a.shape, a.dtype) for a in srcs], *[pltpu.HBM(a.shape, a.dtype) for a in lands],
                   jax.ShapeDtypeStruct((8, 128), F32)),
        in_specs=[*([_HBM] * (2 * n)), _SEM, _SEM, pl.BlockSpec(memory_space=pl.ANY)],
        out_specs=(_SEM, _SEM, *([_HBM] * (2 * n)), pl.BlockSpec(memory_space=pltpu.VMEM)),
        input_output_aliases={i: 2 + i for i in range(2 * n)},
        compiler_params=pltpu.CompilerParams(has_side_effects=_EFFECT),
    )(*srcs, *lands, send1, recv1, after)
    return outs[0], outs[1], outs[2:2 + n], outs[2 + n:2 + 2 * n], outs[-1]


def _gather_wait(send2, recv2, srcs, lands, after, name):
    n = len(srcs)

    def body(*refs):
        land_refs = refs[n:2 * n]
        s2, r2 = refs[2 * n], refs[2 * n + 1]
        for i in range(n):
            for j in range(3):
                cp = _gather_second(land_refs, s2, r2, i, j)
                cp.wait_send()
                cp.wait_recv()

    outs = pl.pallas_call(
        body, name=name,
        out_shape=(*[pltpu.HBM(a.shape, a.dtype) for a in srcs], *[pltpu.HBM(a.shape, a.dtype) for a in lands]),
        in_specs=[*([_HBM] * (2 * n)), _SEM, _SEM, pl.BlockSpec(memory_space=pl.ANY)],
        out_specs=tuple([_HBM] * (2 * n)),
        input_output_aliases={i: i for i in range(2 * n)},
        compiler_params=pltpu.CompilerParams(has_side_effects=_EFFECT),
    )(*srcs, *lands, send2, recv2, after)
    return outs[:n], outs[n:]


def _row(a, l):
    return a[l:l + 1]


def _tie(a, token):
    return a if token is None else a + token[0, 0]


def _layer_fwd(h, mem, W, l, tab, after, at):
    s = {}
    n = f"l{l}_"
    s["h0"] = h
    P = _matmul(h, W["w_in"][l], "nt", n + "proj", after=after, out_dtype=BF16)
    s["P"] = P
    sink_col = jnp.repeat(W["sinks"][l].reshape(N_KV_HEADS, GROUP), BLOCK, axis=1)[:, :, None]
    s["sink_col"] = sink_col
    y_rnn, hs = _rnn_fwd(P, W["conv_w"][l], _row(W["conv_b"], l), W["w_rg"][l], _row(W["b_rg"], l),
                         W["w_ig"][l], _row(W["b_ig"], l), _row(W["lru_lambda"], l), n + "rnn_fwd")
    y_attn = _swa_fwd(P, tab, _tie(sink_col, at(l, "proj", P)), n + "swa_fwd")
    at(l, "attn", y_attn)
    mr = _matmul(y_rnn, W["w_br_rnn"][l], "nn", n + "br_rnn")
    ma = _matmul(y_attn, W["w_br_attn"][l], "nn", n + "br_attn")
    merged = _merge_fwd(P, mr, ma, n + "merge_fwd")
    mix = _matmul(merged, W["w_out"][l], "nn", n + "w_out")
    h1, xh1, rs1 = _ln_fwd(h, mix, _tie(_row(W["ln1_g"], l), at(l, "mix", mix)), _row(W["ln1_b"], l), n + "ln1_fwd")
    at(l, "ln1", h1)
    s.update(hs=hs, y_rnn=y_rnn, y_attn=y_attn, mr=mr, ma=ma, merged=merged, xh1=xh1, rs1=rs1, h1=h1)

    qc = _matmul(h1, W["cq_w"][l], "nn", n + "cq", out_dtype=BF16)
    kv = _matmul(mem, W["ckv_w"][l], "nt", n + "ckv", out_dtype=BF16)
    oc = _cross_fwd(qc, kv, n + "cross_fwd")
    ca = _matmul(oc, W["co_w"][l], "nn", n + "co")
    h2, xh2, rs2 = _ln_fwd(h1, ca, _row(W["ln2_g"], l), _row(W["ln2_b"], l), n + "ln2_fwd")
    s.update(qc=qc, kv=kv, oc=oc, xh2=xh2, rs2=rs2, h2=h2)

    U = _matmul(h2, W["ffn_wi"][l], "nt", n + "ffn_wi", after=at(l, "ln2", h2), out_dtype=BF16)
    act = _swiglu_fwd(U, n + "swiglu_fwd")
    f = _matmul(act, W["ffn_wo"][l], "nn", n + "ffn_wo")
    h3, xh3, rs3 = _ln_fwd(h2, f, _tie(_row(W["ln3_g"], l), at(l, "ffn", f)), _row(W["ln3_b"], l), n + "ln3_fwd")
    s.update(U=U, act=act, xh3=xh3, rs3=rs3)
    return h3, s


GRAD_PARTS = (("ffn_wi", "ffn_wo", "cq_w", "ckv_w", "co_w"), ("w_out", "w_br_rnn", "w_br_attn"),
              ("w_in", "w_rg", "w_ig"))


def _layer_bwd(dh3, mem, W, l, tab, s, send):
    n = f"l{l}_"
    g = {}
    dz3, g["ln3_g"], g["ln3_b"] = _ln_bwd(dh3, s["xh3"], s["rs3"], _row(W["ln3_g"], l), n + "ln3_bwd")
    g["ffn_wo"] = _matmul(s["act"], dz3, "tn", n + "d_ffn_wo", out_dtype=BF16)
    dact = _matmul(dz3, W["ffn_wo"][l], "nt", n + "d_act", out_dtype=BF16)
    dU = _swiglu_bwd(s["U"], dact, n + "swiglu_bwd")
    g["ffn_wi"] = _matmul(dU, s["h2"], "tn", n + "d_ffn_wi", out_dtype=BF16)
    dh2 = _matmul(dU, W["ffn_wi"][l], "nn", n + "d_h2", add=dz3, add_scale=ALPHA)
    dz2, g["ln2_g"], g["ln2_b"] = _ln_bwd(dh2, s["xh2"], s["rs2"], _row(W["ln2_g"], l), n + "ln2_bwd")
    g["co_w"] = _matmul(s["oc"], dz2, "tn", n + "d_co", out_dtype=BF16)
    doc = _matmul(dz2, W["co_w"][l], "nt", n + "d_oc", out_dtype=BF16)
    dqc, dkv = _cross_bwd(s["qc"], s["kv"], doc, n + "cross_bwd")
    g["ckv_w"] = _matmul(dkv, mem, "tn", n + "d_ckv", out_dtype=BF16)
    g["cq_w"] = _matmul(s["h1"], dqc, "tn", n + "d_cq", out_dtype=BF16)
    after = send(l, 0, g)
    dh1 = _matmul(dqc, W["cq_w"][l], "nt", n + "d_h1", add=dz2, add_scale=ALPHA, after=after)
    dz1, g["ln1_g"], g["ln1_b"] = _ln_bwd(dh1, s["xh1"], s["rs1"], _row(W["ln1_g"], l), n + "ln1_bwd")
    g["w_out"] = _matmul(s["merged"], dz1, "tn", n + "d_w_out", out_dtype=BF16)
    dmerged = _matmul(dz1, W["w_out"][l], "nt", n + "d_merged")
    dmr, dma, dgrnn, dgattn = _merge_bwd(s["P"], s["mr"], s["ma"], dmerged, n + "merge_bwd")
    g["w_br_rnn"] = _matmul(s["y_rnn"], dmr, "tn", n + "d_br_rnn", out_dtype=BF16)
    g["w_br_attn"] = _matmul(s["y_attn"], dma, "tn", n + "d_br_attn", out_dtype=BF16)
    after = send(l, 1, g)
    dy_rnn = _matmul(dmr, W["w_br_rnn"][l], "nt", n + "d_y_rnn", after=after)
    dy_attn = _matmul(dma, W["w_br_attn"][l], "nt", n + "d_y_attn", out_dtype=BF16)
    dxr, dgr, g["conv_w"], g["conv_b"], g["w_rg"], g["b_rg"], g["w_ig"], g["b_ig"], g["lru_lambda"] = _rnn_bwd(
        s["P"], s["hs"], dy_rnn, W["conv_w"][l], _row(W["conv_b"], l), W["w_rg"][l], _row(W["b_rg"], l),
        W["w_ig"][l], _row(W["b_ig"], l), _row(W["lru_lambda"], l), n + "rnn_bwd")
    dq, dk, dv, dsk = _swa_bwd(s["P"], dy_attn, tab, s["sink_col"], n + "swa_bwd")
    g["sinks"] = dsk[:, :, 0].reshape(1, N_Q_HEADS)
    dP = jnp.concatenate([dxr, dgr, dq, dk.astype(BF16), dv.astype(BF16), dgrnn, dgattn], axis=1)
    g["w_in"] = _matmul(dP, s["h0"], "tn", n + "d_w_in", out_dtype=BF16)
    after = send(l, 2, g)
    dh = _matmul(dP, W["w_in"][l], "nn", n + "d_h0", add=dz1, add_scale=ALPHA, after=after)
    return dh, g


def _local_step(x, mem, target, W, at, send):
    T = x.shape[0]
    tab = _rope_table(T)
    h = x
    saved = []
    for l in range(DEPTH):
        after = at(l, "start", h)
        h, s = _layer_fwd(h, mem, W, l, tab, after, at)
        saved.append(s)
    lblk, dh = _loss_head(h, target, "loss_head")
    grads = [None] * DEPTH
    for l in reversed(range(DEPTH)):
        dh, grads[l] = _layer_bwd(dh, mem, W, l, tab, saved[l], send)
    return lblk[0, 0], dh, grads


COL_SHARDED = ("w_in", "ckv_w", "ffn_wi")
GATE_MATS = ("w_rg", "w_ig")


def _shard_rows(shards, l):
    out = []
    for n, r in PACK_ROWS:
        a = shards[n][l].astype(BF16)
        if n in COL_SHARDED:
            a = a.T
        elif n in GATE_MATS:
            a = a.reshape(RNN_BLOCKS * RNN_BLOCK // N_DEV, RNN_BLOCK)
        out.append(a)
    return out


def _full_weight(G, name):
    if name in GATE_MATS:
        return jnp.transpose(G.reshape(N_DEV, RNN_BLOCKS, RNN_BLOCK // N_DEV, RNN_BLOCK), (1, 0, 2, 3)).reshape(
            RNN_BLOCKS, RNN_BLOCK, RNN_BLOCK)
    return G.reshape(N_DEV * G.shape[1], G.shape[2])


SHARD_ROWS = dict(PACK_ROWS)


def _pack_blocks(g, names):
    parts = []
    for name in names:
        a = g[name]
        if name in GATE_MATS:
            a = jnp.transpose(a.astype(BF16).reshape(RNN_BLOCKS, N_DEV, RNN_BLOCK // N_DEV, RNN_BLOCK), (1, 0, 2, 3))
        parts.append(a.reshape(N_DEV, SHARD_ROWS[name], D_MODEL))
    return jnp.concatenate(parts, axis=1)


def _pack_small(g):
    rows = [g["conv_w"]]
    for nme in SMALL_NAMES:
        a = g[nme]
        if nme == "sinks":
            a = jnp.pad(a, ((0, 0), (0, D_MODEL - N_Q_HEADS)))
        rows.append(a)
    rows.append(jnp.zeros((SMALL_ROWS - CONV_WIDTH - len(SMALL_NAMES), D_MODEL), F32))
    return jnp.concatenate(rows, axis=0)


_SHARD_SHAPES = {"w_in": (1024, 672), "w_br_rnn": (128, 1024), "w_br_attn": (128, 1024), "w_out": (128, 1024),
                 "cq_w": (128, 1024), "ckv_w": (1024, 256), "co_w": (128, 1024), "ffn_wi": (1024, 704),
                 "ffn_wo": (352, 1024), "w_rg": (4, 32, 256), "w_ig": (4, 32, 256)}

LAYER0_GROUPS = (("w_in", "w_rg", "w_ig"), ("w_br_rnn", "w_br_attn", "w_out"),
                 ("cq_w", "ckv_w", "co_w", "ffn_wi", "ffn_wo"))
LAYER0_FORWARD_AT = {"proj": 1, "mix": 2}
LAYER0_WAIT_AT = {"attn": 1, "ln1": 2}
NEXT_LAYER_FORWARD_AT = ("ffn", "ln2", "ln2", None)

WEIGHT_NAMES = ("w_in", "conv_w", "conv_b", "w_rg", "b_rg", "w_ig", "b_ig", "lru_lambda", "w_br_rnn", "w_br_attn",
                "sinks", "w_out", "ln1_g", "ln1_b", "cq_w", "ckv_w", "co_w", "ln2_g", "ln2_b", "ffn_wi", "ffn_wo",
                "ln3_g", "ln3_b")


def kernel(x, mem, w_in, conv_w, conv_b, w_rg, b_rg, w_ig, b_ig, lru_lambda, w_br_rnn, w_br_attn, sinks, w_out, ln1_g, ln1_b, cq_w, ckv_w, co_w, ln2_g, ln2_b, ffn_wi, ffn_wo, ln3_g, ln3_b, loss_target, m_w_in, m_conv_w, m_conv_b, m_w_rg, m_b_rg, m_w_ig, m_b_ig, m_lru_lambda, m_w_br_rnn, m_w_br_attn, m_sinks, m_w_out, m_ln1_g, m_ln1_b, m_cq_w, m_ckv_w, m_co_w, m_ln2_g, m_ln2_b, m_ffn_wi, m_ffn_wo, m_ln3_g, m_ln3_b, v_w_in, v_conv_w, v_conv_b, v_w_rg, v_b_rg, v_w_ig, v_b_ig, v_lru_lambda, v_w_br_rnn, v_w_br_attn, v_sinks, v_w_out, v_ln1_g, v_ln1_b, v_cq_w, v_ckv_w, v_co_w, v_ln2_g, v_ln2_b, v_ffn_wi, v_ffn_wo, v_ln3_g, v_ln3_b):
    w = dict(w_in=w_in, conv_w=conv_w, conv_b=conv_b, w_rg=w_rg, b_rg=b_rg, w_ig=w_ig, b_ig=b_ig,
             lru_lambda=lru_lambda, w_br_rnn=w_br_rnn, w_br_attn=w_br_attn, sinks=sinks, w_out=w_out, ln1_g=ln1_g,
             ln1_b=ln1_b, cq_w=cq_w, ckv_w=ckv_w, co_w=co_w, ln2_g=ln2_g, ln2_b=ln2_b, ffn_wi=ffn_wi, ffn_wo=ffn_wo,
             ln3_g=ln3_g, ln3_b=ln3_b)
    m = dict(w_in=m_w_in, conv_w=m_conv_w, conv_b=m_conv_b, w_rg=m_w_rg, b_rg=m_b_rg, w_ig=m_w_ig, b_ig=m_b_ig,
             lru_lambda=m_lru_lambda, w_br_rnn=m_w_br_rnn, w_br_attn=m_w_br_attn, sinks=m_sinks, w_out=m_w_out,
             ln1_g=m_ln1_g, ln1_b=m_ln1_b, cq_w=m_cq_w, ckv_w=m_ckv_w, co_w=m_co_w, ln2_g=m_ln2_g, ln2_b=m_ln2_b,
             ffn_wi=m_ffn_wi, ffn_wo=m_ffn_wo, ln3_g=m_ln3_g, ln3_b=m_ln3_b)
    v = dict(w_in=v_w_in, conv_w=v_conv_w, conv_b=v_conv_b, w_rg=v_w_rg, b_rg=v_b_rg, w_ig=v_w_ig, b_ig=v_b_ig,
             lru_lambda=v_lru_lambda, w_br_rnn=v_w_br_rnn, w_br_attn=v_w_br_attn, sinks=v_sinks, w_out=v_w_out,
             ln1_g=v_ln1_g, ln1_b=v_ln1_b, cq_w=v_cq_w, ckv_w=v_ckv_w, co_w=v_co_w, ln2_g=v_ln2_g, ln2_b=v_ln2_b,
             ffn_wi=v_ffn_wi, ffn_wo=v_ffn_wo, ln3_g=v_ln3_g, ln3_b=v_ln3_b)
    my_dev = 4 * lax.axis_index("x") + 2 * lax.axis_index("y") + lax.axis_index("c")

    W = {n: [None] * DEPTH for n, _ in PACK_ROWS}
    shards0 = dict(zip([n for n, _ in PACK_ROWS], _shard_rows(w, 0)))
    gathered = _all_gather_many([shards0[n] for n in LAYER0_GROUPS[0]], "l0_gather_first")
    for n, G in zip(LAYER0_GROUPS[0], gathered):
        W[n][0] = _full_weight(G, n)
    conv_all = _all_gather(conv_w.reshape(DEPTH * CONV_WIDTH, D_MODEL // N_DEV), "gather_conv")
    W["conv_w"] = jnp.transpose(conv_all, (1, 0, 2)).reshape(DEPTH, CONV_WIDTH, D_MODEL)
    for n in SMALL_NAMES:
        W[n] = w[n]
    flying = {}
    token = conv_all
    groups = [((0, gi), LAYER0_GROUPS[gi], [shards0[n] for n in LAYER0_GROUPS[gi]]) for gi in (1, 2)]
    groups += [((l, 0), [n for n, _ in PACK_ROWS], _shard_rows(w, l)) for l in range(1, DEPTH)]
    for key, names, srcs in groups:
        lands = [lax.empty((N_DEV,) + a.shape, a.dtype) for a in srcs]
        flying[key] = (names,) + _gather_start(srcs, lands, f"l{key[0]}_gather_start{key[1]}", token)
        token = flying[key][5]
    first_token = token

    def forward(key, after):
        names, send1, recv1, srcs, lands, _ = flying[key]
        flying[key] = (names,) + _gather_forward(send1, recv1, srcs, lands, after, f"l{key[0]}_gather_forward{key[1]}")
        return flying[key][5]

    def arrive(key, after):
        names, send2, recv2, srcs, lands, _ = flying.pop(key)
        srcs, lands = _gather_wait(send2, recv2, srcs, lands, after, f"l{key[0]}_gather_wait{key[1]}")
        for n, mine, G in zip(names, srcs, lands):
            W[n][key[0]] = _full_weight(lax.dynamic_update_index_in_dim(G, mine, my_dev, 0), n)

    def at(l, point, x):
        if point == "start":
            if l == 0:
                return first_token
            arrive((l, 0), x)
        elif l == 0 and point in LAYER0_FORWARD_AT:
            return forward((0, LAYER0_FORWARD_AT[point]), x)
        elif l == 0 and point in LAYER0_WAIT_AT:
            arrive((0, LAYER0_WAIT_AT[point]), x)
        elif point == NEXT_LAYER_FORWARD_AT[l]:
            return forward((l + 1, 0), x)
        return None

    summed = {}
    sent = {}

    def finish(l, part, after):
        ssem, rsem, blocks, land, mine = sent.pop((l, part))
        _, (recv,) = _split_wait(ssem, rsem, [blocks], [land], after, f"l{l}_exchange_wait{part}")
        summed[(l, part)] = _sum_slots(recv, f"l{l}_sum_grads{part}", mine=mine)

    def send(l, part, g):
        blocks = _pack_blocks(g, GRAD_PARTS[part])
        mine = lax.dynamic_index_in_dim(blocks, my_dev, 0, keepdims=False)
        if (l + 1, part) in sent:
            finish(l + 1, part, blocks)
        ssem, rsem, (blocks,), (land,), token = _split_start([blocks], [lax.empty(blocks.shape, blocks.dtype)],
                                                             f"l{l}_exchange_start{part}")
        sent[(l, part)] = (ssem, rsem, blocks, land, mine)
        return token

    loss_local, dx, layer_grads = _local_step(x[0], mem[0], loss_target[0], W, at, send)
    for part in range(len(GRAD_PARTS)):
        finish(0, part, dx)
    loss = lax.psum(loss_local, MESH_AXES)

    small_all = _all_gather(jnp.concatenate([_pack_small(g) for g in layer_grads], axis=0), "gather_small_grads")
    small_sum = _sum_slots(small_all, "sum_small_grads").reshape(DEPTH, SMALL_ROWS, D_MODEL)

    grads = {}
    for part, names in enumerate(GRAD_PARTS):
        G = jnp.stack([summed[(l, part)] for l in range(DEPTH)])
        o = 0
        for n in names:
            r = SHARD_ROWS[n]
            blk = G[:, o:o + r, :]
            grads[n] = jnp.transpose(blk, (0, 2, 1)) if n in COL_SHARDED else blk.reshape((DEPTH,) + _SHARD_SHAPES[n])
            o += r
    conv_full = small_sum[:, :CONV_WIDTH, :]
    grads["conv_w"] = lax.dynamic_slice_in_dim(conv_full, my_dev * (D_MODEL // N_DEV), D_MODEL // N_DEV, axis=2)
    for i, n in enumerate(SMALL_NAMES):
        row = small_sum[:, CONV_WIDTH + i, :]
        grads[n] = row[:, :N_Q_HEADS] if n == "sinks" else row

    deltas, new_m, new_v = {}, {}, {}
    for n in WEIGHT_NAMES:
        deltas[n], new_m[n], new_v[n] = _adamw(w[n], grads[n], m[n], v[n], "adamw_" + n)

    return (loss, dx[None], *[grads[n] for n in WEIGHT_NAMES], *[deltas[n] for n in WEIGHT_NAMES],
            *[new_m[n] for n in WEIGHT_NAMES], *[new_v[n] for n in WEIGHT_NAMES])
```

```python
import functools
import math

import jax
import jax.numpy as jnp
from jax import lax
from jax.experimental import pallas as pl
from jax.experimental.pallas import tpu as pltpu

F32 = jnp.float32
BF16 = jnp.bfloat16

D_MODEL = 1024
DEPTH = 4
N_DEV = 8
RNN_BLOCKS = 4
RNN_BLOCK = 256
CONV_WIDTH = 4
LRU_C = 8.0
HEAD_DIM = 64
N_Q_HEADS = 16
N_KV_HEADS = 2
GROUP = 8
KV_WIDTH = 128
BLOCK = 128
ROPE_THETA = 500000.0
ROT_DIM = 16
IN_COLS = 5376
CROSS_HEADS = 4
CROSS_HEAD_DIM = 256
D_FF = 2816
LN_EPS = 1e-5
ALPHA = (2 * DEPTH) ** 0.25
NEG_INF = -1e30

ADAM_LR = 0.001
ADAM_B1 = 0.9
ADAM_B2 = 0.999
ADAM_EPS = 1e-08
ADAM_WD = 0.01
ADAM_STEP = 10

C_XR, C_GR, C_Q, C_K, C_V, C_GRNN, C_GATTN = 0, 1024, 2048, 3072, 3200, 3328, 4352

TIME_CHUNK = 256
ROW_TILE = 256

MESH_AXES = ("x", "y", "c")

PACK_ROWS = (("w_in", 672), ("w_br_rnn", 128), ("w_br_attn", 128), ("w_out", 128), ("cq_w", 128),
             ("ckv_w", 256), ("co_w", 128), ("ffn_wi", 704), ("ffn_wo", 352), ("w_rg", 32), ("w_ig", 32))
SMALL_NAMES = ("conv_b", "b_rg", "b_ig", "lru_lambda", "sinks", "ln1_g", "ln1_b", "ln2_g", "ln2_b", "ln3_g", "ln3_b")
SMALL_ROWS = 16


def _pick(dim, cands):
    for c in cands:
        if dim % c == 0:
            return c
    return dim


_DIMS = {"nn": (((1,), (0,)), ((), ())), "nt": (((1,), (1,)), ((), ())), "tn": (((0,), (0,)), ((), ()))}

MATMUL_VMEM_BUDGET = 44 * 2 ** 20
MATMUL_MAX_TILE = 2048
MXU_DIM = 256
STEP_COST_BYTES = 500_000
MIN_ROW_TILE = 512


def _tile_candidates(dim, whole=False, step=MXU_DIM):
    c = [d for d in range(step, min(dim, MATMUL_MAX_TILE) + 1, step) if dim % d == 0]
    if whole and dim not in c:
        c.append(dim)
    return c or [dim]


def _matmul_tiles(M, N, K, sa, sb, so, has_add):
    best = None
    for tk in _tile_candidates(K, whole=True):
        nk = K // tk
        for tm in [t for t in _tile_candidates(M, step=128) if t >= min(M, MIN_ROW_TILE)]:
            for tn in _tile_candidates(N):
                vmem = 2 * (tm * tk * sa + tk * tn * sb + tm * tn * so) + tm * tn * 4
                vmem += tm * tn * 4 if nk > 1 else 0
                vmem += 2 * tm * tn * 4 if has_add else 0
                vmem += (tm * tk * 2 if sa == 4 else 0) + (tk * tn * 2 if sb == 4 else 0)
                if vmem > MATMUL_VMEM_BUDGET:
                    continue
                steps = (M // tm) * (N // tn) * nk
                exposed = tm * tk * sa + tk * tn * sb + tm * tn * so
                acc_moves = steps * tm * tn * 2 if nk > 1 else 0
                fixed = M * N * so + steps * STEP_COST_BYTES + exposed + acc_moves
                a_in = M * K * sa * ((N // tn) if nk > 1 else 1) + K * N * sb * (M // tm)
                b_in = M * K * sa * (N // tn) + K * N * sb * ((M // tm) if nk > 1 else 1)
                for cost, m_outer in ((a_in + fixed, True), (b_in + fixed, False)):
                    if best is None or cost < best[0]:
                        best = (cost, tm, tn, tk, m_outer)
    return best[1:]


def _matmul(a, b, mode, name, add=None, add_scale=1.0, out_dtype=F32, after=None):
    if mode == "nn":
        (M, K), (_, N) = a.shape, b.shape
    elif mode == "nt":
        (M, K), (N, _) = a.shape, b.shape
    else:
        (K, M), (_, N) = a.shape, b.shape
    tm, tn, tk, m_outer = _matmul_tiles(M, N, K, a.dtype.itemsize, b.dtype.itemsize, jnp.dtype(out_dtype).itemsize,
                                        add is not None)
    nk = K // tk
    dims = _DIMS[mode]

    def body(*refs):
        if after is not None:
            refs = refs[:-2 - (nk > 1)] + refs[-1 - (nk > 1):]
        a_ref, b_ref = refs[0], refs[1]
        c_ref = refs[2] if add is not None else None
        o_ref = refs[3] if add is not None else refs[2]

        def finish(r):
            if add is not None:
                r = r + add_scale * c_ref[...]
            o_ref[...] = r.astype(out_dtype)

        prod = lax.dot_general(a_ref[...].astype(BF16), b_ref[...].astype(BF16), dims, preferred_element_type=F32)
        if nk == 1:
            finish(prod)
            return
        acc_ref = refs[-1]
        k = pl.program_id(2)

        @pl.when(k == 0)
        def _():
            acc_ref[...] = prod

        @pl.when(k > 0)
        def _():
            acc_ref[...] += prod

        @pl.when(k == nk - 1)
        def _():
            finish(acc_ref[...])

    ij = (lambda p, q: (p, q)) if m_outer else (lambda p, q: (q, p))
    if mode == "nn":
        a_spec = pl.BlockSpec((tm, tk), lambda p, q, k: (ij(p, q)[0], k))
        b_spec = pl.BlockSpec((tk, tn), lambda p, q, k: (k, ij(p, q)[1]))
    elif mode == "nt":
        a_spec = pl.BlockSpec((tm, tk), lambda p, q, k: (ij(p, q)[0], k))
        b_spec = pl.BlockSpec((tn, tk), lambda p, q, k: (ij(p, q)[1], k))
    else:
        a_spec = pl.BlockSpec((tk, tm), lambda p, q, k: (k, ij(p, q)[0]))
        b_spec = pl.BlockSpec((tk, tn), lambda p, q, k: (k, ij(p, q)[1]))
    o_spec = pl.BlockSpec((tm, tn), lambda p, q, k: ij(p, q))
    in_specs = [a_spec, b_spec]
    args = [a, b]
    if add is not None:
        in_specs.append(o_spec)
        args.append(add)
    if after is not None:
        in_specs.append(pl.BlockSpec(memory_space=pl.ANY))
        args.append(after)
    return pl.pallas_call(
        body, name=name, grid=(M // tm, N // tn, nk) if m_outer else (N // tn, M // tm, nk),
        in_specs=in_specs, out_specs=o_spec,
        out_shape=jax.ShapeDtypeStruct((M, N), out_dtype),
        scratch_shapes=[pltpu.VMEM((tm, tn), F32)] if nk > 1 else [],
        compiler_params=pltpu.CompilerParams(dimension_semantics=("parallel", "parallel", "arbitrary")),
    )(*args)


LN_ROWS = 512


def _ln_fwd(h, f, g, b, name):
    T, D = h.shape
    tr = _pick(T, (LN_ROWS, 256, 128, 64, 32, 16, 8))

    def body(h_ref, f_ref, g_ref, b_ref, o_ref, xh_ref, rs_ref):
        z = ALPHA * h_ref[...] + f_ref[...]
        mu = jnp.mean(z, axis=-1, keepdims=True)
        zc = z - mu
        var = jnp.mean(zc * zc, axis=-1, keepdims=True)
        rstd = lax.rsqrt(var + LN_EPS)
        xh = zc * rstd
        xh_ref[...] = xh
        rs_ref[...] = rstd
        o_ref[...] = xh * g_ref[...] + b_ref[...]

    row = pl.BlockSpec((tr, D), lambda i: (i, 0))
    vec = pl.BlockSpec((1, D), lambda i: (0, 0))
    return pl.pallas_call(
        body, name=name, grid=(T // tr,), in_specs=[row, row, vec, vec],
        out_specs=[row, row, pl.BlockSpec((tr, 1), lambda i: (i, 0))],
        out_shape=[jax.ShapeDtypeStruct((T, D), F32), jax.ShapeDtypeStruct((T, D), F32),
                   jax.ShapeDtypeStruct((T, 1), F32)],
        compiler_params=pltpu.CompilerParams(dimension_semantics=("parallel",)),
    )(h, f, g, b)


def _ln_bwd(dout, xh, rstd, g, name, after=None):
    T, D = dout.shape
    tr = _pick(T, (LN_ROWS, 256, 128, 64, 32, 16, 8))

    def body(do_ref, xh_ref, rs_ref, g_ref, *rest):
        dz_ref, dg_ref, db_ref = rest[-3:]

        @pl.when(pl.program_id(0) == 0)
        def _():
            dg_ref[...] = jnp.zeros_like(dg_ref)
            db_ref[...] = jnp.zeros_like(db_ref)

        do = do_ref[...]
        xh = xh_ref[...]
        dxh = do * g_ref[...]
        m1 = jnp.mean(dxh, axis=-1, keepdims=True)
        m2 = jnp.mean(dxh * xh, axis=-1, keepdims=True)
        dz_ref[...] = rs_ref[...] * (dxh - m1 - xh * m2)
        dg_ref[...] += jnp.sum(do * xh, axis=0, keepdims=True)
        db_ref[...] += jnp.sum(do, axis=0, keepdims=True)

    row = pl.BlockSpec((tr, D), lambda i: (i, 0))
    vec = pl.BlockSpec((1, D), lambda i: (0, 0))
    in_specs = [row, row, pl.BlockSpec((tr, 1), lambda i: (i, 0)), vec]
    args = [dout, xh, rstd, g]
    if after is not None:
        in_specs.append(pl.BlockSpec(memory_space=pl.ANY))
        args.append(after)
    return pl.pallas_call(
        body, name=name, grid=(T // tr,),
        in_specs=in_specs, out_specs=[row, vec, vec],
        out_shape=[jax.ShapeDtypeStruct((T, D), F32), jax.ShapeDtypeStruct((1, D), F32),
                   jax.ShapeDtypeStruct((1, D), F32)],
        compiler_params=pltpu.CompilerParams(dimension_semantics=("arbitrary",)),
    )(*args)


_GELU_C = math.sqrt(2.0 / math.pi)


def _gelu(x):
    t = jnp.tanh(_GELU_C * (x + 0.044715 * x * x * x))
    return 0.5 * x * (1.0 + t), t


def _gelu_grad(x, t):
    return 0.5 * (1.0 + t) + 0.5 * x * (1.0 - t * t) * _GELU_C * (1.0 + 3 * 0.044715 * x * x)


def _sigmoid(x):
    return 1.0 / (1.0 + jnp.exp(-x))


def _softplus_neg(lam):
    z = jnp.exp(-jnp.abs(lam))
    u = 1.0 + z
    l1p = jnp.where(u == 1.0, z, jnp.log(u) * z / jnp.where(u == 1.0, 1.0, u - 1.0))
    return jnp.maximum(-lam, 0.0) + l1p


def _neg_expm1(x):
    series = x * (1.0 + x * 0.5 * (1.0 + x * (1.0 / 3.0) * (1.0 + x * 0.25 * (1.0 + x * 0.2))))
    return -jnp.where(x > -0.05, series, jnp.exp(x) - 1.0)


def _scan_fwd(a, b):
    n = a.shape[0]
    rows = lax.broadcasted_iota(jnp.int32, a.shape, 0)
    s = 1
    while s < n:
        keep = rows >= s
        b = jnp.where(keep, a * pltpu.roll(b, s, 0) + b, b)
        a = jnp.where(keep, a * pltpu.roll(a, s, 0), a)
        s *= 2
    return a, b


def _scan_bwd(c, b):
    n = c.shape[0]
    rows = lax.broadcasted_iota(jnp.int32, c.shape, 0)
    s = 1
    while s < n:
        keep = rows < n - s
        b = jnp.where(keep, c * pltpu.roll(b, n - s, 0) + b, b)
        c = jnp.where(keep, c * pltpu.roll(c, n - s, 0), c)
        s *= 2
    return c, b


def _rnn_gates(xc, wr, br, wi, bi, sp):
    xb = xc.astype(BF16)
    r = _sigmoid(jnp.dot(xb, wr, preferred_element_type=F32) + br)
    i = _sigmoid(jnp.dot(xb, wi, preferred_element_type=F32) + bi)
    la = -LRU_C * r * sp
    a = jnp.exp(la)
    om = _neg_expm1(2.0 * la)
    mult = jnp.sqrt(om)
    return r, i, a, om, mult


def _rnn_specs(T):
    C = RNN_BLOCK
    col = lambda off: pl.BlockSpec((T, C), lambda n, off=off: (0, off // C + n))
    vec = pl.BlockSpec((1, C), lambda n: (0, n))
    cw = pl.BlockSpec((CONV_WIDTH, C), lambda n: (0, n))
    w = pl.BlockSpec((1, C, C), lambda n: (n, 0, 0))
    own = pl.BlockSpec((T, C), lambda n: (0, n))
    return col, vec, cw, w, own


def _rnn_fwd(P, cw, cb, wrg, brg, wig, big, lam, name):
    T = P.shape[0]
    C = RNN_BLOCK
    tc = _pick(T, (TIME_CHUNK,))
    nch = T // tc

    def body(x_ref, g_ref, cw_ref, cb_ref, wr_ref, br_ref, wi_ref, bi_ref, lam_ref, y_ref, hs_ref, xs_ref):
        sp = _softplus_neg(lam_ref[...])
        wr = wr_ref[0]
        wi = wi_ref[0]
        xs_ref[0:8, :] = jnp.zeros((8, C), F32)

        def chunk(c, hprev):
            r0 = pl.multiple_of(c * tc, tc)
            x = x_ref[pl.ds(r0, tc), :].astype(F32)
            xs_ref[8:, :] = x
            xc = cb_ref[...] + jnp.zeros((tc, C), F32)
            for k in range(CONV_WIDTH):
                xc = xc + xs_ref[pl.ds(8 - (CONV_WIDTH - 1 - k), tc), :] * cw_ref[k:k + 1, :]
            xs_ref[0:8, :] = x[tc - 8:, :]
            r, i, a, om, mult = _rnn_gates(xc, wr, br_ref[...], wi, bi_ref[...], sp)
            acum, bcum = _scan_fwd(a, mult * (i * xc))
            h = acum * hprev + bcum
            hs_ref[pl.ds(r0, tc), :] = h
            ge, _ = _gelu(g_ref[pl.ds(r0, tc), :].astype(F32))
            y_ref[pl.ds(r0, tc), :] = (h * ge).astype(BF16)
            return h[tc - 1:tc, :]

        lax.fori_loop(0, nch, chunk, jnp.zeros((1, C), F32))

    col, vec, cwspec, w, own = _rnn_specs(T)
    return pl.pallas_call(
        body, name=name, grid=(RNN_BLOCKS,),
        in_specs=[col(C_XR), col(C_GR), cwspec, vec, w, vec, w, vec, vec],
        out_specs=[own, own],
        out_shape=[jax.ShapeDtypeStruct((T, D_MODEL), BF16), jax.ShapeDtypeStruct((T, D_MODEL), F32)],
        scratch_shapes=[pltpu.VMEM((tc + 8, C), F32)],
        compiler_params=pltpu.CompilerParams(dimension_semantics=("parallel",)),
    )(P, P, cw, cb, wrg, brg, wig, big, lam)


def _rnn_bwd(P, hs, dy, cw, cb, wrg, brg, wig, big, lam, name):
    T = P.shape[0]
    C = RNN_BLOCK
    tc = _pick(T, (TIME_CHUNK,))
    nch = T // tc

    def body(x_ref, g_ref, hs_ref, dy_ref, cw_ref, cb_ref, wr_ref, br_ref, wi_ref, bi_ref, lam_ref,
             dx_ref, dg_ref, dcw_ref, dcb_ref, dwr_ref, dbr_ref, dwi_ref, dbi_ref, dlam_ref,
             xs_ref, hp_ref, an_ref, dn_ref):
        lam_v = lam_ref[...]
        sp = _softplus_neg(lam_v)
        wr = wr_ref[0]
        wi = wi_ref[0]
        dcw_ref[...] = jnp.zeros_like(dcw_ref)
        dcb_ref[...] = jnp.zeros_like(dcb_ref)
        dwr_ref[...] = jnp.zeros_like(dwr_ref)
        dbr_ref[...] = jnp.zeros_like(dbr_ref)
        dwi_ref[...] = jnp.zeros_like(dwi_ref)
        dbi_ref[...] = jnp.zeros_like(dbi_ref)
        dlam_ref[...] = jnp.zeros_like(dlam_ref)
        an_ref[tc:, :] = jnp.zeros((8, C), F32)
        dn_ref[tc:, :] = jnp.zeros((8, C), F32)

        def chunk(step, gnext):
            c = nch - 1 - step
            r0 = pl.multiple_of(c * tc, tc)
            p0 = pl.multiple_of(jnp.maximum(r0 - 8, 0), 8)
            q0 = pl.multiple_of(jnp.maximum(r0 - 16, 0), 16)
            live = c > 0
            x = x_ref[pl.ds(r0, tc), :].astype(F32)
            xs_ref[0:8, :] = jnp.where(live, x_ref[pl.ds(q0, 16), :].astype(F32)[8:, :], 0.0)
            xs_ref[8:, :] = x
            xsh = [xs_ref[pl.ds(8 - (CONV_WIDTH - 1 - k), tc), :] for k in range(CONV_WIDTH)]
            xc = cb_ref[...] + jnp.zeros((tc, C), F32)
            for k in range(CONV_WIDTH):
                xc = xc + xsh[k] * cw_ref[k:k + 1, :]
            r, i, a, om, mult = _rnn_gates(xc, wr, br_ref[...], wi, bi_ref[...], sp)
            h = hs_ref[pl.ds(r0, tc), :]
            hp_ref[0:8, :] = jnp.where(live, hs_ref[pl.ds(p0, 8), :], 0.0)
            hp_ref[8:, :] = h
            hm1 = hp_ref[pl.ds(7, tc), :]
            g = g_ref[pl.ds(r0, tc), :].astype(F32)
            ge, th = _gelu(g)
            dy = dy_ref[pl.ds(r0, tc), :]
            dg_ref[pl.ds(r0, tc), :] = (dy * h * _gelu_grad(g, th)).astype(BF16)
            an_ref[0:tc, :] = a
            coef = an_ref[pl.ds(1, tc), :]
            ccum, bcum = _scan_bwd(coef, dy * ge)
            G = bcum + ccum * gnext
            an_ref[tc:, :] = a[0:8, :]
            da = G * hm1
            ixc = i * xc
            dmult = G * ixc
            di = G * mult * xc
            dxc = G * mult * i
            dla = da * a - dmult * (1.0 - om) / mult
            dr = dla * (-LRU_C * sp)
            dlam_ref[...] += jnp.sum(dla * r, axis=0, keepdims=True)
            dzr = dr * r * (1.0 - r)
            dzi = di * i * (1.0 - i)
            dbr_ref[...] += jnp.sum(dzr, axis=0, keepdims=True)
            dbi_ref[...] += jnp.sum(dzi, axis=0, keepdims=True)
            xb = xc.astype(BF16)
            dzrb = dzr.astype(BF16)
            dzib = dzi.astype(BF16)
            dwr_ref[0] += lax.dot_general(xb, dzrb, _DIMS["tn"], preferred_element_type=F32)
            dwi_ref[0] += lax.dot_general(xb, dzib, _DIMS["tn"], preferred_element_type=F32)
            dxc = dxc + lax.dot_general(dzrb, wr, _DIMS["nt"], preferred_element_type=F32)
            dxc = dxc + lax.dot_general(dzib, wi, _DIMS["nt"], preferred_element_type=F32)
            dcb_ref[...] += jnp.sum(dxc, axis=0, keepdims=True)
            for k in range(CONV_WIDTH):
                dcw_ref[k:k + 1, :] += jnp.sum(dxc * xsh[k], axis=0, keepdims=True)
            dn_ref[0:tc, :] = dxc
            dx = jnp.zeros((tc, C), F32)
            for k in range(CONV_WIDTH):
                dx = dx + dn_ref[pl.ds(CONV_WIDTH - 1 - k, tc), :] * cw_ref[k:k + 1, :]
            dn_ref[tc:, :] = dxc[0:8, :]
            dx_ref[pl.ds(r0, tc), :] = dx.astype(BF16)
            return G[0:1, :]

        lax.fori_loop(0, nch, chunk, jnp.zeros((1, C), F32))
        dlam_ref[...] = dlam_ref[...] * (LRU_C * _sigmoid(-lam_v))

    col, vec, cwspec, w, own = _rnn_specs(T)
    vshape = jax.ShapeDtypeStruct((1, D_MODEL), F32)
    wshape = jax.ShapeDtypeStruct((RNN_BLOCKS, C, C), F32)
    return pl.pallas_call(
        body, name=name, grid=(RNN_BLOCKS,),
        in_specs=[col(C_XR), col(C_GR), own, own, cwspec, vec, w, vec, w, vec, vec],
        out_specs=[own, own, cwspec, vec, w, vec, w, vec, vec],
        out_shape=[jax.ShapeDtypeStruct((T, D_MODEL), BF16), jax.ShapeDtypeStruct((T, D_MODEL), BF16),
                   jax.ShapeDtypeStruct((CONV_WIDTH, D_MODEL), F32), vshape, wshape, vshape, wshape, vshape, vshape],
        scratch_shapes=[pltpu.VMEM((tc + 8, C), F32), pltpu.VMEM((tc + 8, C), F32),
                        pltpu.VMEM((tc + 8, C), F32), pltpu.VMEM((tc + 8, C), F32)],
        compiler_params=pltpu.CompilerParams(dimension_semantics=("parallel",)),
    )(P, P, hs, dy, cw, cb, wrg, brg, wig, big, lam)


def _rope_table(T):
    half = ROT_DIM // 2
    pos = jnp.arange(T, dtype=F32)
    inv_freq = ROPE_THETA ** (-jnp.arange(0, ROT_DIM, 2, dtype=F32) / ROT_DIM)
    ang = pos[:, None] * inv_freq[None, :]
    cos, sin = jnp.cos(ang), jnp.sin(ang)
    one = jnp.ones((T, HEAD_DIM - ROT_DIM), F32)
    zero = jnp.zeros((T, HEAD_DIM - ROT_DIM), F32)
    z8 = jnp.zeros((T, half), F32)
    c = jnp.concatenate([cos, cos, one], axis=1)
    a = jnp.concatenate([-sin, z8, zero], axis=1)
    b = jnp.concatenate([z8, sin, zero], axis=1)
    return jnp.stack([jnp.tile(c, (1, 2)), jnp.tile(a, (1, 2)), jnp.tile(b, (1, 2))])


def _rope(x, tab, sign):
    W = x.shape[1]
    rep = W // 128
    c = jnp.tile(tab[0], (1, rep)) if rep > 1 else tab[0]
    a = jnp.tile(tab[1], (1, rep)) if rep > 1 else tab[1]
    b = jnp.tile(tab[2], (1, rep)) if rep > 1 else tab[2]
    return x * c + sign * (pltpu.roll(x, W - ROT_DIM // 2, 1) * a + pltpu.roll(x, ROT_DIM // 2, 1) * b)


def _swa_mask(n):
    rows = lax.broadcasted_iota(jnp.int32, (GROUP * BLOCK, 2 * BLOCK), 0) & (BLOCK - 1)
    cols = lax.broadcasted_iota(jnp.int32, (GROUP * BLOCK, 2 * BLOCK), 1)
    return (cols > rows) & (cols <= rows + BLOCK) & ((n > 0) | (cols >= BLOCK))


def _swa_probs(qg, k2, sink, valid):
    s = lax.dot_general(qg, k2, _DIMS["nt"], preferred_element_type=F32) * (HEAD_DIM ** -0.5)
    s = jnp.where(valid, s, NEG_INF)
    m = jnp.maximum(jnp.max(s, axis=1, keepdims=True), sink)
    p = jnp.exp(s - m)
    ps = jnp.exp(sink - m)
    inv = 1.0 / (jnp.sum(p, axis=1, keepdims=True) + ps)
    return p * inv, ps * inv


def _swa_specs(T):
    nb = T // BLOCK
    qspec = pl.BlockSpec((BLOCK, D_MODEL), lambda n: (n, C_Q // D_MODEL))
    cur = lambda off: pl.BlockSpec((BLOCK, KV_WIDTH), lambda n, off=off: (n, off // KV_WIDTH))
    prev = lambda off: pl.BlockSpec((BLOCK, KV_WIDTH), lambda n, off=off: (jnp.maximum(n - 1, 0), off // KV_WIDTH))
    tcur = pl.BlockSpec((3, BLOCK, 128), lambda n: (0, n, 0))
    tprev = pl.BlockSpec((3, BLOCK, 128), lambda n: (0, jnp.maximum(n - 1, 0), 0))
    sink = pl.BlockSpec((N_KV_HEADS, GROUP * BLOCK, 1), lambda n: (0, 0, 0))
    own = pl.BlockSpec((BLOCK, D_MODEL), lambda n: (n, 0))
    return nb, qspec, cur, prev, tcur, tprev, sink, own


def _stack_heads(x, hk):
    return jnp.concatenate([x[:, (hk * GROUP + g) * HEAD_DIM:(hk * GROUP + g + 1) * HEAD_DIM] for g in range(GROUP)],
                           axis=0)


def _swa_fwd(P, tab, sink_col, name):
    T = P.shape[0]
    nb, qspec, cur, prev, tcur, tprev, sink, own = _swa_specs(T)

    def body(q_ref, kc_ref, kp_ref, vc_ref, vp_ref, tc_ref, tp_ref, sk_ref, o_ref):
        n = pl.program_id(0)
        valid = _swa_mask(n)
        q = _rope(q_ref[...].astype(F32), tc_ref[...], 1.0).astype(BF16)
        k2 = jnp.concatenate([_rope(kp_ref[...].astype(F32), tp_ref[...], 1.0),
                              _rope(kc_ref[...].astype(F32), tc_ref[...], 1.0)], axis=0).astype(BF16)
        v2 = jnp.concatenate([vp_ref[...], vc_ref[...]], axis=0).astype(BF16)
        parts = []
        for hk in range(N_KV_HEADS):
            sl = slice(hk * HEAD_DIM, (hk + 1) * HEAD_DIM)
            pn, _ = _swa_probs(_stack_heads(q, hk), k2[:, sl], sk_ref[hk], valid)
            og = jnp.dot(pn.astype(BF16), v2[:, sl], preferred_element_type=F32)
            parts += [og[g * BLOCK:(g + 1) * BLOCK, :] for g in range(GROUP)]
        o_ref[...] = jnp.concatenate(parts, axis=1).astype(BF16)

    return pl.pallas_call(
        body, name=name, grid=(nb,),
        in_specs=[qspec, cur(C_K), prev(C_K), cur(C_V), prev(C_V), tcur, tprev, sink],
        out_specs=own, out_shape=jax.ShapeDtypeStruct((T, D_MODEL), BF16),
        compiler_params=pltpu.CompilerParams(dimension_semantics=("parallel",)),
    )(P, P, P, P, P, tab, tab, sink_col)


def _swa_bwd(P, do, tab, sink_col, name):
    T = P.shape[0]
    nb, qspec, cur, prev, tcur, tprev, sink, own = _swa_specs(T)

    def body(q_ref, kc_ref, kp_ref, vc_ref, vp_ref, do_ref, tc_ref, tp_ref, sk_ref,
             dq_ref, dk_ref, dv_ref, ds_ref):
        n = pl.program_id(0)

        @pl.when(n == 0)
        def _():
            dk_ref[...] = jnp.zeros_like(dk_ref)
            dv_ref[...] = jnp.zeros_like(dv_ref)
            ds_ref[...] = jnp.zeros_like(ds_ref)

        valid = _swa_mask(n)
        tcur_v = tc_ref[...]
        tprev_v = tp_ref[...]
        q = _rope(q_ref[...].astype(F32), tcur_v, 1.0).astype(BF16)
        k2 = jnp.concatenate([_rope(kp_ref[...].astype(F32), tprev_v, 1.0),
                              _rope(kc_ref[...].astype(F32), tcur_v, 1.0)], axis=0).astype(BF16)
        v2 = jnp.concatenate([vp_ref[...], vc_ref[...]], axis=0).astype(BF16)
        dob = do_ref[...].astype(BF16)
        dq_parts = []
        dk_parts = []
        dv_parts = []
        for hk in range(N_KV_HEADS):
            sl = slice(hk * HEAD_DIM, (hk + 1) * HEAD_DIM)
            qg = _stack_heads(q, hk)
            dog = _stack_heads(dob, hk)
            pn, psn = _swa_probs(qg, k2[:, sl], sk_ref[hk], valid)
            dp = lax.dot_general(dog, v2[:, sl], _DIMS["nt"], preferred_element_type=F32)
            delta = jnp.sum(pn * dp, axis=1, keepdims=True)
            dsc = (pn * (dp - delta) * (HEAD_DIM ** -0.5)).astype(BF16)
            dsink = -psn * delta
            for g in range(GROUP):
                ds_ref[hk, g:g + 1, :] += jnp.broadcast_to(
                    jnp.sum(dsink[g * BLOCK:(g + 1) * BLOCK], axis=0, keepdims=True), (1, 128))
            dqg = jnp.dot(dsc, k2[:, sl], preferred_element_type=F32)
            dq_parts += [dqg[g * BLOCK:(g + 1) * BLOCK, :] for g in range(GROUP)]
            dk_parts.append(lax.dot_general(dsc, qg, _DIMS["tn"], preferred_element_type=F32))
            dv_parts.append(lax.dot_general(pn.astype(BF16), dog, _DIMS["tn"], preferred_element_type=F32))
        dq_ref[...] = _rope(jnp.concatenate(dq_parts, axis=1), tcur_v, -1.0).astype(BF16)
        dk2 = jnp.concatenate(dk_parts, axis=1)
        dv2 = jnp.concatenate(dv_parts, axis=1)
        c0 = pl.multiple_of(n * BLOCK, BLOCK)
        p0 = pl.multiple_of(jnp.maximum(n - 1, 0) * BLOCK, BLOCK)
        dk_ref[pl.ds(p0, BLOCK), :] += _rope(dk2[:BLOCK], tprev_v, -1.0)
        dv_ref[pl.ds(p0, BLOCK), :] += dv2[:BLOCK]
        dk_ref[pl.ds(c0, BLOCK), :] += _rope(dk2[BLOCK:], tcur_v, -1.0)
        dv_ref[pl.ds(c0, BLOCK), :] += dv2[BLOCK:]

    full = pl.BlockSpec((T, KV_WIDTH), lambda n: (0, 0))
    return pl.pallas_call(
        body, name=name, grid=(nb,),
        in_specs=[qspec, cur(C_K), prev(C_K), cur(C_V), prev(C_V), own, tcur, tprev, sink],
        out_specs=[own, full, full, pl.BlockSpec((N_KV_HEADS, GROUP, 128), lambda n: (0, 0, 0))],
        out_shape=[jax.ShapeDtypeStruct((T, D_MODEL), BF16), jax.ShapeDtypeStruct((T, KV_WIDTH), F32),
                   jax.ShapeDtypeStruct((T, KV_WIDTH), F32), jax.ShapeDtypeStruct((N_KV_HEADS, GROUP, 128), F32)],
        compiler_params=pltpu.CompilerParams(dimension_semantics=("arbitrary",)),
    )(P, P, P, P, P, do, tab, tab, sink_col)


_MW = 256


def _gate_specs(T, rows, width):
    tr = _pick(T, (rows, 256, 128, 64, 32, 16, 8))
    col = lambda off: pl.BlockSpec((tr, width), lambda i, j, off=off: (i, off // width + j))
    own = pl.BlockSpec((tr, width), lambda i, j: (i, j))
    return tr, col, own


def _merge_fwd(P, mr, ma, name):
    T = P.shape[0]
    tr, col, own = _gate_specs(T, 1024, _MW)

    def body(gr_ref, ga_ref, mr_ref, ma_ref, o_ref):
        o_ref[...] = (_sigmoid(gr_ref[...].astype(F32)) * mr_ref[...]
                      + _sigmoid(ga_ref[...].astype(F32)) * ma_ref[...]).astype(BF16)

    return pl.pallas_call(
        body, name=name, grid=(T // tr, D_MODEL // _MW), in_specs=[col(C_GRNN), col(C_GATTN), own, own],
        out_specs=own, out_shape=jax.ShapeDtypeStruct((T, D_MODEL), BF16),
        compiler_params=pltpu.CompilerParams(dimension_semantics=("parallel", "parallel")),
    )(P, P, mr, ma)


def _merge_bwd(P, mr, ma, dm, name):
    T = P.shape[0]
    tr, col, own = _gate_specs(T, 512, _MW)

    def body(gr_ref, ga_ref, mr_ref, ma_ref, dm_ref, dmr_ref, dma_ref, dgr_ref, dga_ref):
        dm = dm_ref[...]
        sr = _sigmoid(gr_ref[...].astype(F32))
        sa = _sigmoid(ga_ref[...].astype(F32))
        dmr_ref[...] = (dm * sr).astype(BF16)
        dma_ref[...] = (dm * sa).astype(BF16)
        dgr_ref[...] = (dm * mr_ref[...] * sr * (1.0 - sr)).astype(BF16)
        dga_ref[...] = (dm * ma_ref[...] * sa * (1.0 - sa)).astype(BF16)

    shp = jax.ShapeDtypeStruct((T, D_MODEL), BF16)
    return pl.pallas_call(
        body, name=name, grid=(T // tr, D_MODEL // _MW), in_specs=[col(C_GRNN), col(C_GATTN), own, own, own],
        out_specs=[own] * 4, out_shape=[shp] * 4,
        compiler_params=pltpu.CompilerParams(dimension_semantics=("parallel", "parallel")),
    )(P, P, mr, ma, dm)


_FFN_ROWS = 128


def _swiglu_fwd(U, name):
    T = U.shape[0]
    tr = _pick(T, (_FFN_ROWS, 64, 32, 16))
    half = lambda j: pl.BlockSpec((tr, D_FF), lambda i, j=j: (i, j))

    def body(g_ref, u_ref, o_ref):
        g = g_ref[...].astype(F32)
        o_ref[...] = (g * _sigmoid(g) * u_ref[...].astype(F32)).astype(BF16)

    return pl.pallas_call(
        body, name=name, grid=(T // tr,), in_specs=[half(0), half(1)],
        out_specs=half(0), out_shape=jax.ShapeDtypeStruct((T, D_FF), BF16),
        compiler_params=pltpu.CompilerParams(dimension_semantics=("parallel",)),
    )(U, U)


def _swiglu_bwd(U, dact, name):
    T = U.shape[0]
    tr = _pick(T, (_FFN_ROWS, 64, 32, 16))
    half = lambda j: pl.BlockSpec((tr, D_FF), lambda i, j=j: (i, j))

    def body(g_ref, u_ref, da_ref, o_ref):
        g = g_ref[...].astype(F32)
        da = da_ref[...].astype(F32)
        s = _sigmoid(g)
        o_ref[:, :D_FF] = (da * u_ref[...].astype(F32) * s * (1.0 + g * (1.0 - s))).astype(BF16)
        o_ref[:, D_FF:] = (da * g * s).astype(BF16)

    return pl.pallas_call(
        body, name=name, grid=(T // tr,), in_specs=[half(0), half(1), half(0)],
        out_specs=pl.BlockSpec((tr, 2 * D_FF), lambda i: (i, 0)),
        out_shape=jax.ShapeDtypeStruct((T, 2 * D_FF), BF16),
        compiler_params=pltpu.CompilerParams(dimension_semantics=("parallel",)),
    )(U, U, dact)


def _cross_probs(qh, kh):
    s = lax.dot_general(qh, kh, _DIMS["nt"], preferred_element_type=F32) * (CROSS_HEAD_DIM ** -0.5)
    p = jnp.exp(s - jnp.max(s, axis=1, keepdims=True))
    return p / jnp.sum(p, axis=1, keepdims=True)


def _cross_fwd(q, kv, name):
    T = q.shape[0]
    M = kv.shape[0]
    tr = _pick(T, (ROW_TILE, 128, 64, 32, 16, 8))
    W = CROSS_HEAD_DIM

    def body(q_ref, kv_ref, o_ref):
        for h in range(CROSS_HEADS):
            qh = q_ref[:, h * W:(h + 1) * W].astype(BF16)
            kh = kv_ref[:, h * W:(h + 1) * W].astype(BF16)
            vh = kv_ref[:, D_MODEL + h * W:D_MODEL + (h + 1) * W].astype(BF16)
            pn = _cross_probs(qh, kh)
            o_ref[:, h * W:(h + 1) * W] = jnp.dot(pn.astype(BF16), vh, preferred_element_type=F32).astype(BF16)

    row = pl.BlockSpec((tr, D_MODEL), lambda i: (i, 0))
    return pl.pallas_call(
        body, name=name, grid=(T // tr,), in_specs=[row, pl.BlockSpec((M, 2 * D_MODEL), lambda i: (0, 0))],
        out_specs=row, out_shape=jax.ShapeDtypeStruct((T, D_MODEL), BF16),
        compiler_params=pltpu.CompilerParams(dimension_semantics=("parallel",)),
    )(q, kv)


def _cross_bwd(q, kv, do, name):
    T = q.shape[0]
    M = kv.shape[0]
    tr = _pick(T, (ROW_TILE, 128, 64, 32, 16, 8))
    W = CROSS_HEAD_DIM

    def body(q_ref, kv_ref, do_ref, dq_ref, dkv_ref):
        @pl.when(pl.program_id(0) == 0)
        def _():
            dkv_ref[...] = jnp.zeros_like(dkv_ref)

        for h in range(CROSS_HEADS):
            qh = q_ref[:, h * W:(h + 1) * W].astype(BF16)
            kh = kv_ref[:, h * W:(h + 1) * W].astype(BF16)
            vh = kv_ref[:, D_MODEL + h * W:D_MODEL + (h + 1) * W].astype(BF16)
            doh = do_ref[:, h * W:(h + 1) * W].astype(BF16)
            pn = _cross_probs(qh, kh)
            dp = lax.dot_general(doh, vh, _DIMS["nt"], preferred_element_type=F32)
            delta = jnp.sum(pn * dp, axis=1, keepdims=True)
            dsc = (pn * (dp - delta) * (W ** -0.5)).astype(BF16)
            dq_ref[:, h * W:(h + 1) * W] = jnp.dot(dsc, kh, preferred_element_type=F32).astype(BF16)
            dkv_ref[:, h * W:(h + 1) * W] += lax.dot_general(dsc, qh, _DIMS["tn"], preferred_element_type=F32)
            dkv_ref[:, D_MODEL + h * W:D_MODEL + (h + 1) * W] += lax.dot_general(
                pn.astype(BF16), doh, _DIMS["tn"], preferred_element_type=F32)

    row = pl.BlockSpec((tr, D_MODEL), lambda i: (i, 0))
    full = pl.BlockSpec((M, 2 * D_MODEL), lambda i: (0, 0))
    return pl.pallas_call(
        body, name=name, grid=(T // tr,), in_specs=[row, full, row], out_specs=[row, full],
        out_shape=[jax.ShapeDtypeStruct((T, D_MODEL), BF16), jax.ShapeDtypeStruct((M, 2 * D_MODEL), F32)],
        compiler_params=pltpu.CompilerParams(dimension_semantics=("arbitrary",)),
    )(q, kv, do)


def _loss_head(y, target, name):
    T, D = y.shape
    tr = _pick(T, (ROW_TILE, 128, 64, 32, 16, 8))

    def body(y_ref, t_ref, l_ref, dy_ref):
        @pl.when(pl.program_id(0) == 0)
        def _():
            l_ref[...] = jnp.zeros_like(l_ref)

        err = y_ref[...] - t_ref[...]
        dy_ref[...] = err * (1.0 / D)
        l_ref[...] += jnp.broadcast_to(0.5 * jnp.sum(jnp.mean(err * err, axis=-1, keepdims=True), axis=0, keepdims=True),
                                       (8, 128))

    row = pl.BlockSpec((tr, D), lambda i: (i, 0))
    return pl.pallas_call(
        body, name=name, grid=(T // tr,), in_specs=[row, row],
        out_specs=[pl.BlockSpec((8, 128), lambda i: (0, 0)), row],
        out_shape=[jax.ShapeDtypeStruct((8, 128), F32), jax.ShapeDtypeStruct((T, D), F32)],
        compiler_params=pltpu.CompilerParams(dimension_semantics=("arbitrary",)),
    )(y, target)


def _sum_slots(recv, name, mine=None):
    _, R, C = recv.shape
    tr = _pick(R, (ROW_TILE, 224, 368, 128, 64, 32, 16, 8))

    def body(*refs):
        r_ref, o_ref = refs[0], refs[-1]
        if mine is None:
            acc = r_ref[0].astype(F32)
            for d in range(1, N_DEV):
                acc = acc + r_ref[d].astype(F32)
        else:
            me = _my_slot()
            acc = refs[1][...].astype(F32)
            for d in range(N_DEV):
                acc = acc + jnp.where(d == me, 0.0, r_ref[d].astype(F32))
        o_ref[...] = acc

    in_specs = [pl.BlockSpec((N_DEV, tr, C), lambda i: (0, i, 0))]
    args = [recv]
    if mine is not None:
        in_specs.append(pl.BlockSpec((tr, C), lambda i: (i, 0)))
        args.append(mine)
    return pl.pallas_call(
        body, name=name, grid=(R // tr,), in_specs=in_specs,
        out_specs=pl.BlockSpec((tr, C), lambda i: (i, 0)), out_shape=jax.ShapeDtypeStruct((R, C), F32),
        compiler_params=pltpu.CompilerParams(dimension_semantics=("parallel",)),
    )(*args)


def _adamw(w, g, m, v, name):
    shape = w.shape
    C = shape[-1]
    R = math.prod(shape[:-1])
    w2, g2, m2, v2 = (t.reshape(R, C) for t in (w, g, m, v))
    tr = _pick(R, (ROW_TILE, 128, 64, 32, 16, 8))

    def body(w_ref, g_ref, m_ref, v_ref, d_ref, mo_ref, vo_ref):
        gg = g_ref[...]
        mn = ADAM_B1 * m_ref[...] + (1.0 - ADAM_B1) * gg
        vn = ADAM_B2 * v_ref[...] + (1.0 - ADAM_B2) * (gg * gg)
        m_hat = mn / (1.0 - ADAM_B1 ** ADAM_STEP)
        v_hat = vn / (1.0 - ADAM_B2 ** ADAM_STEP)
        d_ref[...] = -ADAM_LR * (m_hat / (jnp.sqrt(v_hat) + ADAM_EPS) + ADAM_WD * w_ref[...])
        mo_ref[...] = mn
        vo_ref[...] = vn

    blk = pl.BlockSpec((tr, C), lambda i: (i, 0))
    shp = jax.ShapeDtypeStruct((R, C), F32)
    d, mo, vo = pl.pallas_call(
        body, name=name, grid=(R // tr,), in_specs=[blk] * 4, out_specs=[blk] * 3, out_shape=[shp] * 3,
        compiler_params=pltpu.CompilerParams(dimension_semantics=("parallel",)),
    )(w2, g2, m2, v2)
    return d.reshape(shape), mo.reshape(shape), vo.reshape(shape)


def _all_gather_many(bufs, name):
    n = len(bufs)

    def body(*refs):
        xs, outs = refs[:n], refs[n:2 * n]
        send_sems, recv_sems, local_sems = refs[2 * n:]
        x, y, c = lax.axis_index("x"), lax.axis_index("y"), lax.axis_index("c")
        me, sibling = (x, y, c), (x, y, 1 - c)
        chips = [(1 - x, y), (x, 1 - y), (1 - x, 1 - y)]

        def slot(i, px, py, pc):
            return outs[i].at[4 * px + 2 * py + pc]

        def copy(i, k, block, to, src=None):
            return pltpu.make_async_remote_copy(
                src_ref=slot(i, *block) if src is None else src, dst_ref=slot(i, *block),
                send_sem=send_sems.at[7 * i + k], recv_sem=recv_sems.at[7 * i + k],
                device_id=to, device_id_type=pl.DeviceIdType.MESH)

        mine = [pltpu.make_async_copy(xs[i], slot(i, *me), local_sems.at[i]) for i in range(n)]
        for cp in mine:
            cp.start()
        first = [copy(i, 0, me, sibling, src=xs[i]) for i in range(n)]
        for j, chip in enumerate(chips):
            first += [copy(i, 1 + j, me, (*chip, c), src=xs[i]) for i in range(n)]
        for cp in first:
            cp.start()
        passed = []
        for j, chip in enumerate(chips):
            for i in range(n):
                copy(i, 1 + j, (*chip, c), me).wait_recv()
                passed.append(copy(i, 4 + j, (*chip, c), sibling))
                passed[-1].start()
        for i in range(n):
            copy(i, 0, sibling, me).wait_recv()
        for j, chip in enumerate(chips):
            for i in range(n):
                copy(i, 4 + j, (*chip, 1 - c), me).wait_recv()
        for cp in first + passed:
            cp.wait_send()
        for cp in mine:
            cp.wait()

    hbm = pl.BlockSpec(memory_space=pl.ANY)
    return pl.pallas_call(
        body, name=name, out_shape=[jax.ShapeDtypeStruct((N_DEV,) + b.shape, b.dtype) for b in bufs],
        in_specs=[hbm] * n, out_specs=[hbm] * n,
        scratch_shapes=[pltpu.SemaphoreType.DMA((7 * n,)), pltpu.SemaphoreType.DMA((7 * n,)),
                        pltpu.SemaphoreType.DMA((n,))],
    )(*bufs)


def _all_gather(buf, name):
    return _all_gather_many([buf], name)[0]


_HBM = pl.BlockSpec(memory_space=pltpu.HBM)
_SEM = pl.BlockSpec(memory_space=pltpu.SEMAPHORE)
_EFFECT = pltpu.SideEffectType.DATAFLOW_SIDE_EFFECTING


def _peer(k):
    x, y, c = lax.axis_index("x"), lax.axis_index("y"), lax.axis_index("c")
    return x ^ ((k >> 2) & 1), y ^ ((k >> 1) & 1), c ^ (k & 1)


def _my_slot():
    return 4 * lax.axis_index("x") + 2 * lax.axis_index("y") + lax.axis_index("c")


def _split_copy(src_refs, land_refs, send_sems, recv_sems, i, k):
    px, py, pc = _peer(k)
    return pltpu.make_async_remote_copy(
        src_ref=src_refs[i].at[4 * px + 2 * py + pc], dst_ref=land_refs[i].at[_my_slot()],
        send_sem=send_sems.at[7 * i + k - 1], recv_sem=recv_sems.at[7 * i + k - 1],
        device_id=(px, py, pc), device_id_type=pl.DeviceIdType.MESH)


def _split_start(srcs, lands, name):
    n = len(srcs)

    def body(*refs):
        src_refs, land_refs = refs[:n], refs[n:2 * n]
        send_sems, recv_sems = refs[2 * n], refs[2 * n + 1]
        token = refs[-1]
        for i in range(n):
            for k in range(1, N_DEV):
                _split_copy(src_refs, land_refs, send_sems, recv_sems, i, k).start()
        token[...] = jnp.zeros_like(token)

    outs = pl.pallas_call(
        body, name=name,
        out_shape=(pltpu.SemaphoreType.DMA((7 * n,)), pltpu.SemaphoreType.DMA((7 * n,)),
                   *[pltpu.HBM(a.shape, a.dtype) for a in srcs], *[pltpu.HBM(a.shape, a.dtype) for a in lands],
                   jax.ShapeDtypeStruct((8, 128), F32)),
        in_specs=[_HBM] * (2 * n),
        out_specs=(_SEM, _SEM, *([_HBM] * (2 * n)), pl.BlockSpec(memory_space=pltpu.VMEM)),
        input_output_aliases={i: 2 + i for i in range(2 * n)},
        compiler_params=pltpu.CompilerParams(has_side_effects=_EFFECT),
    )(*[pltpu.with_memory_space_constraint(a, pltpu.HBM) for a in list(srcs) + list(lands)])
    return outs[0], outs[1], outs[2:2 + n], outs[2 + n:2 + 2 * n], outs[-1]


def _split_wait(send_sems, recv_sems, srcs, lands, after, name):
    n = len(srcs)

    def body(*refs):
        src_refs, land_refs = refs[:n], refs[n:2 * n]
        ssem, rsem = refs[2 * n], refs[2 * n + 1]
        for i in range(n):
            for k in range(1, N_DEV):
                cp = _split_copy(src_refs, land_refs, ssem, rsem, i, k)
                cp.wait_send()
                cp.wait_recv()

    outs = pl.pallas_call(
        body, name=name,
        out_shape=(*[pltpu.HBM(a.shape, a.dtype) for a in srcs], *[pltpu.HBM(a.shape, a.dtype) for a in lands]),
        in_specs=[*([_HBM] * (2 * n)), _SEM, _SEM, pl.BlockSpec(memory_space=pl.ANY)],
        out_specs=tuple([_HBM] * (2 * n)),
        input_output_aliases={i: i for i in range(2 * n)},
        compiler_params=pltpu.CompilerParams(has_side_effects=_EFFECT),
    )(*srcs, *lands, send_sems, recv_sems, after)
    return outs[:n], outs[n:]


def _gather_first(src_refs, land_refs, send_sems, recv_sems, i, k):
    x, y, c = lax.axis_index("x"), lax.axis_index("y"), lax.axis_index("c")
    to = ((x, y, 1 - c), (1 - x, y, c), (x, 1 - y, c), (1 - x, 1 - y, c))[k]
    return pltpu.make_async_remote_copy(
        src_ref=src_refs[i], dst_ref=land_refs[i].at[_my_slot()],
        send_sem=send_sems.at[4 * i + k], recv_sem=recv_sems.at[4 * i + k],
        device_id=to, device_id_type=pl.DeviceIdType.MESH)


def _gather_second(land_refs, send_sems, recv_sems, i, j):
    x, y, c = lax.axis_index("x"), lax.axis_index("y"), lax.axis_index("c")
    px, py = ((1 - x, y), (x, 1 - y), (1 - x, 1 - y))[j]
    slot = land_refs[i].at[4 * px + 2 * py + c]
    return pltpu.make_async_remote_copy(
        src_ref=slot, dst_ref=slot, send_sem=send_sems.at[3 * i + j], recv_sem=recv_sems.at[3 * i + j],
        device_id=(x, y, 1 - c), device_id_type=pl.DeviceIdType.MESH)


def _gather_start(srcs, lands, name, after):
    n = len(srcs)

    def body(*refs):
        src_refs, land_refs = refs[:n], refs[n:2 * n]
        send_sems, recv_sems = refs[2 * n + 1], refs[2 * n + 2]
        token = refs[-1]
        for k in range(4):
            for i in range(n):
                _gather_first(src_refs, land_refs, send_sems, recv_sems, i, k).start()
        token[...] = jnp.zeros_like(token)

    outs = pl.pallas_call(
        body, name=name,
        out_shape=(pltpu.SemaphoreType.DMA((4 * n,)), pltpu.SemaphoreType.DMA((4 * n,)),
                   *[pltpu.HBM(a.shape, a.dtype) for a in srcs], *[pltpu.HBM(a.shape, a.dtype) for a in lands],
                   jax.ShapeDtypeStruct((8, 128), F32)),
        in_specs=[_HBM] * (2 * n) + [pl.BlockSpec(memory_space=pl.ANY)],
        out_specs=(_SEM, _SEM, *([_HBM] * (2 * n)), pl.BlockSpec(memory_space=pltpu.VMEM)),
        input_output_aliases={i: 2 + i for i in range(2 * n)},
        compiler_params=pltpu.CompilerParams(has_side_effects=_EFFECT),
    )(*[pltpu.with_memory_space_constraint(a, pltpu.HBM) for a in list(srcs) + list(lands)], after)
    return outs[0], outs[1], outs[2:2 + n], outs[2 + n:2 + 2 * n], outs[-1]


def _gather_forward(send1, recv1, srcs, lands, after, name):
    n = len(srcs)

    def body(*refs):
        src_refs, land_refs = refs[:n], refs[n:2 * n]
        s1, r1 = refs[2 * n], refs[2 * n + 1]
        s2, r2 = refs[2 * n + 3], refs[2 * n + 4]
        token = refs[-1]
        for j in range(3):
            for i in range(n):
                _gather_first(src_refs, land_refs, s1, r1, i, 1 + j).wait_recv()
                _gather_second(land_refs, s2, r2, i, j).start()
        for i in range(n):
            _gather_first(src_refs, land_refs, s1, r1, i, 0).wait_recv()
            for k in range(4):
                _gather_first(src_refs, land_refs, s1, r1, i, k).wait_send()
        token[...] = jnp.zeros_like(token)

    outs = pl.pallas_call(
        body, name=name,
        out_shape=(pltpu.SemaphoreType.DMA((3 * n,)), pltpu.SemaphoreType.DMA((3 * n,)),
                   *[pltpu.HBM(a.shape, a.dtype) for a in srcs], *[pltpu.HBM(a.shape, a.dtype) for a in lands],
                   jax.ShapeDtypeStruct((8, 128), F32)),
        in_specs=[*([_HBM] * (2 * n)), _SEM, _SEM, pl.BlockSpec(memory_space=pl.ANY)],
        out_specs=(_SEM, _SEM, *([_HBM] * (2 * n)), pl.BlockSpec(memory_space=pltpu.VMEM)),
        input_output_aliases={i: 2 + i for i in range(2 * n)},
        compiler_params=pltpu.CompilerParams(has_side_effects=_EFFECT),
    )(*srcs, *lands, send1, recv1, after)
    return outs[0], outs[1], outs[2:2 + n], outs[2 + n:2 + 2 * n], outs[-1]


def _gather_wait(send2, recv2, srcs, lands, after, name):
    n = len(srcs)

    def body(*refs):
        land_refs = refs[n:2 * n]
        s2, r2 = refs[2 * n], refs[2 * n + 1]
        for i in range(n):
            for j in range(3):
                cp = _gather_second(land_refs, s2, r2, i, j)
                cp.wait_send()
                cp.wait_recv()

    outs = pl.pallas_call(
        body, name=name,
        out_shape=(*[pltpu.HBM(a.shape, a.dtype) for a in srcs], *[pltpu.HBM(a.shape, a.dtype) for a in lands]),
        in_specs=[*([_HBM] * (2 * n)), _SEM, _SEM, pl.BlockSpec(memory_space=pl.ANY)],
        out_specs=tuple([_HBM] * (2 * n)),
        input_output_aliases={i: i for i in range(2 * n)},
        compiler_params=pltpu.CompilerParams(has_side_effects=_EFFECT),
    )(*srcs, *lands, send2, recv2, after)
    return outs[:n], outs[n:]


def _row(a, l):
    return a[l:l + 1]


def _tie(a, token):
    return a if token is None else a + token[0, 0]


def _layer_fwd(h, mem, W, l, tab, after, at):
    s = {}
    n = f"l{l}_"
    s["h0"] = h
    P = _matmul(h, W["w_in"][l], "nt", n + "proj", after=after, out_dtype=BF16)
    s["P"] = P
    sink_col = jnp.repeat(W["sinks"][l].reshape(N_KV_HEADS, GROUP), BLOCK, axis=1)[:, :, None]
    s["sink_col"] = sink_col
    y_rnn, hs = _rnn_fwd(P, W["conv_w"][l], _row(W["conv_b"], l), W["w_rg"][l], _row(W["b_rg"], l),
                         W["w_ig"][l], _row(W["b_ig"], l), _row(W["lru_lambda"], l), n + "rnn_fwd")
    y_attn = _swa_fwd(P, tab, _tie(sink_col, at(l, "proj", P)), n + "swa_fwd")
    at(l, "attn", y_attn)
    mr = _matmul(y_rnn, W["w_br_rnn"][l], "nn", n + "br_rnn")
    ma = _matmul(y_attn, W["w_br_attn"][l], "nn", n + "br_attn")
    merged = _merge_fwd(P, mr, ma, n + "merge_fwd")
    mix = _matmul(merged, W["w_out"][l], "nn", n + "w_out")
    h1, xh1, rs1 = _ln_fwd(h, mix, _tie(_row(W["ln1_g"], l), at(l, "mix", mix)), _row(W["ln1_b"], l), n + "ln1_fwd")
    at(l, "ln1", h1)
    s.update(hs=hs, y_rnn=y_rnn, y_attn=y_attn, mr=mr, ma=ma, merged=merged, xh1=xh1, rs1=rs1, h1=h1)

    qc = _matmul(h1, W["cq_w"][l], "nn", n + "cq", out_dtype=BF16)
    kv = _matmul(mem, W["ckv_w"][l], "nt", n + "ckv", out_dtype=BF16)
    oc = _cross_fwd(qc, kv, n + "cross_fwd")
    ca = _matmul(oc, W["co_w"][l], "nn", n + "co")
    h2, xh2, rs2 = _ln_fwd(h1, ca, _row(W["ln2_g"], l), _row(W["ln2_b"], l), n + "ln2_fwd")
    s.update(qc=qc, kv=kv, oc=oc, xh2=xh2, rs2=rs2, h2=h2)

    U = _matmul(h2, W["ffn_wi"][l], "nt", n + "ffn_wi", after=at(l, "ln2", h2), out_dtype=BF16)
    act = _swiglu_fwd(U, n + "swiglu_fwd")
    f = _matmul(act, W["ffn_wo"][l], "nn", n + "ffn_wo")
    h3, xh3, rs3 = _ln_fwd(h2, f, _tie(_row(W["ln3_g"], l), at(l, "ffn", f)), _row(W["ln3_b"], l), n + "ln3_fwd")
    s.update(U=U, act=act, xh3=xh3, rs3=rs3)
    return h3, s


GRAD_PARTS = (("ffn_wi", "ffn_wo", "cq_w", "ckv_w", "co_w"), ("w_out", "w_br_rnn", "w_br_attn"),
              ("w_in", "w_rg", "w_ig"))


def _layer_bwd(dh3, mem, W, l, tab, s, send):
    n = f"l{l}_"
    g = {}
    dz3, g["ln3_g"], g["ln3_b"] = _ln_bwd(dh3, s["xh3"], s["rs3"], _row(W["ln3_g"], l), n + "ln3_bwd")
    g["ffn_wo"] = _matmul(s["act"], dz3, "tn", n + "d_ffn_wo", out_dtype=BF16)
    dact = _matmul(dz3, W["ffn_wo"][l], "nt", n + "d_act", out_dtype=BF16)
    dU = _swiglu_bwd(s["U"], dact, n + "swiglu_bwd")
    g["ffn_wi"] = _matmul(dU, s["h2"], "tn", n + "d_ffn_wi", out_dtype=BF16)
    dh2 = _matmul(dU, W["ffn_wi"][l], "nn", n + "d_h2", add=dz3, add_scale=ALPHA)
    dz2, g["ln2_g"], g["ln2_b"] = _ln_bwd(dh2, s["xh2"], s["rs2"], _row(W["ln2_g"], l), n + "ln2_bwd")
    g["co_w"] = _matmul(s["oc"], dz2, "tn", n + "d_co", out_dtype=BF16)
    doc = _matmul(dz2, W["co_w"][l], "nt", n + "d_oc", out_dtype=BF16)
    dqc, dkv = _cross_bwd(s["qc"], s["kv"], doc, n + "cross_bwd")
    g["ckv_w"] = _matmul(dkv, mem, "tn", n + "d_ckv", out_dtype=BF16)
    g["cq_w"] = _matmul(s["h1"], dqc, "tn", n + "d_cq", out_dtype=BF16)
    after = send(l, 0, g)
    dh1 = _matmul(dqc, W["cq_w"][l], "nt", n + "d_h1", add=dz2, add_scale=ALPHA, after=after)
    dz1, g["ln1_g"], g["ln1_b"] = _ln_bwd(dh1, s["xh1"], s["rs1"], _row(W["ln1_g"], l), n + "ln1_bwd")
    g["w_out"] = _matmul(s["merged"], dz1, "tn", n + "d_w_out", out_dtype=BF16)
    dmerged = _matmul(dz1, W["w_out"][l], "nt", n + "d_merged")
    dmr, dma, dgrnn, dgattn = _merge_bwd(s["P"], s["mr"], s["ma"], dmerged, n + "merge_bwd")
    g["w_br_rnn"] = _matmul(s["y_rnn"], dmr, "tn", n + "d_br_rnn", out_dtype=BF16)
    g["w_br_attn"] = _matmul(s["y_attn"], dma, "tn", n + "d_br_attn", out_dtype=BF16)
    after = send(l, 1, g)
    dy_rnn = _matmul(dmr, W["w_br_rnn"][l], "nt", n + "d_y_rnn", after=after)
    dy_attn = _matmul(dma, W["w_br_attn"][l], "nt", n + "d_y_attn", out_dtype=BF16)
    dxr, dgr, g["conv_w"], g["conv_b"], g["w_rg"], g["b_rg"], g["w_ig"], g["b_ig"], g["lru_lambda"] = _rnn_bwd(
        s["P"], s["hs"], dy_rnn, W["conv_w"][l], _row(W["conv_b"], l), W["w_rg"][l], _row(W["b_rg"], l),
        W["w_ig"][l], _row(W["b_ig"], l), _row(W["lru_lambda"], l), n + "rnn_bwd")
    dq, dk, dv, dsk = _swa_bwd(s["P"], dy_attn, tab, s["sink_col"], n + "swa_bwd")
    g["sinks"] = dsk[:, :, 0].reshape(1, N_Q_HEADS)
    dP = jnp.concatenate([dxr, dgr, dq, dk.astype(BF16), dv.astype(BF16), dgrnn, dgattn], axis=1)
    g["w_in"] = _matmul(dP, s["h0"], "tn", n + "d_w_in", out_dtype=BF16)
    after = send(l, 2, g)
    dh = _matmul(dP, W["w_in"][l], "nn", n + "d_h0", add=dz1, add_scale=ALPHA, after=after)
    return dh, g


def _local_step(x, mem, target, W, at, send):
    T = x.shape[0]
    tab = _rope_table(T)
    h = x
    saved = []
    for l in range(DEPTH):
        after = at(l, "start", h)
        h, s = _layer_fwd(h, mem, W, l, tab, after, at)
        saved.append(s)
    lblk, dh = _loss_head(h, target, "loss_head")
    grads = [None] * DEPTH
    for l in reversed(range(DEPTH)):
        dh, grads[l] = _layer_bwd(dh, mem, W, l, tab, saved[l], send)
    return lblk[0, 0], dh, grads


COL_SHARDED = ("w_in", "ckv_w", "ffn_wi")
GATE_MATS = ("w_rg", "w_ig")


def _shard_rows(shards, l):
    out = []
    for n, r in PACK_ROWS:
        a = shards[n][l].astype(BF16)
        if n in COL_SHARDED:
            a = a.T
        elif n in GATE_MATS:
            a = a.reshape(RNN_BLOCKS * RNN_BLOCK // N_DEV, RNN_BLOCK)
        out.append(a)
    return out


def _full_weight(G, name):
    if name in GATE_MATS:
        return jnp.transpose(G.reshape(N_DEV, RNN_BLOCKS, RNN_BLOCK // N_DEV, RNN_BLOCK), (1, 0, 2, 3)).reshape(
            RNN_BLOCKS, RNN_BLOCK, RNN_BLOCK)
    return G.reshape(N_DEV * G.shape[1], G.shape[2])


SHARD_ROWS = dict(PACK_ROWS)


def _pack_blocks(g, names):
    parts = []
    for name in names:
        a = g[name]
        if name in GATE_MATS:
            a = jnp.transpose(a.astype(BF16).reshape(RNN_BLOCKS, N_DEV, RNN_BLOCK // N_DEV, RNN_BLOCK), (1, 0, 2, 3))
        parts.append(a.reshape(N_DEV, SHARD_ROWS[name], D_MODEL))
    return jnp.concatenate(parts, axis=1)


def _pack_small(g):
    rows = [g["conv_w"]]
    for nme in SMALL_NAMES:
        a = g[nme]
        if nme == "sinks":
            a = jnp.pad(a, ((0, 0), (0, D_MODEL - N_Q_HEADS)))
        rows.append(a)
    rows.append(jnp.zeros((SMALL_ROWS - CONV_WIDTH - len(SMALL_NAMES), D_MODEL), F32))
    return jnp.concatenate(rows, axis=0)


_SHARD_SHAPES = {"w_in": (1024, 672), "w_br_rnn": (128, 1024), "w_br_attn": (128, 1024), "w_out": (128, 1024),
                 "cq_w": (128, 1024), "ckv_w": (1024, 256), "co_w": (128, 1024), "ffn_wi": (1024, 704),
                 "ffn_wo": (352, 1024), "w_rg": (4, 32, 256), "w_ig": (4, 32, 256)}

LAYER0_GROUPS = (("w_in", "w_rg", "w_ig"), ("w_br_rnn", "w_br_attn", "w_out"),
                 ("cq_w", "ckv_w", "co_w", "ffn_wi", "ffn_wo"))
LAYER0_FORWARD_AT = {"proj": 1, "mix": 2}
LAYER0_WAIT_AT = {"attn": 1, "ln1": 2}
NEXT_LAYER_FORWARD_AT = ("ffn", "ln2", "ln2", None)

WEIGHT_NAMES = ("w_in", "conv_w", "conv_b", "w_rg", "b_rg", "w_ig", "b_ig", "lru_lambda", "w_br_rnn", "w_br_attn",
                "sinks", "w_out", "ln1_g", "ln1_b", "cq_w", "ckv_w", "co_w", "ln2_g", "ln2_b", "ffn_wi", "ffn_wo",
                "ln3_g", "ln3_b")


def kernel(x, mem, w_in, conv_w, conv_b, w_rg, b_rg, w_ig, b_ig, lru_lambda, w_br_rnn, w_br_attn, sinks, w_out, ln1_g, ln1_b, cq_w, ckv_w, co_w, ln2_g, ln2_b, ffn_wi, ffn_wo, ln3_g, ln3_b, loss_target, m_w_in, m_conv_w, m_conv_b, m_w_rg, m_b_rg, m_w_ig, m_b_ig, m_lru_lambda, m_w_br_rnn, m_w_br_attn, m_sinks, m_w_out, m_ln1_g, m_ln1_b, m_cq_w, m_ckv_w, m_co_w, m_ln2_g, m_ln2_b, m_ffn_wi, m_ffn_wo, m_ln3_g, m_ln3_b, v_w_in, v_conv_w, v_conv_b, v_w_rg, v_b_rg, v_w_ig, v_b_ig, v_lru_lambda, v_w_br_rnn, v_w_br_attn, v_sinks, v_w_out, v_ln1_g, v_ln1_b, v_cq_w, v_ckv_w, v_co_w, v_ln2_g, v_ln2_b, v_ffn_wi, v_ffn_wo, v_ln3_g, v_ln3_b):
    w = dict(w_in=w_in, conv_w=conv_w, conv_b=conv_b, w_rg=w_rg, b_rg=b_rg, w_ig=w_ig, b_ig=b_ig,
             lru_lambda=lru_lambda, w_br_rnn=w_br_rnn, w_br_attn=w_br_attn, sinks=sinks, w_out=w_out, ln1_g=ln1_g,
             ln1_b=ln1_b, cq_w=cq_w, ckv_w=ckv_w, co_w=co_w, ln2_g=ln2_g, ln2_b=ln2_b, ffn_wi=ffn_wi, ffn_wo=ffn_wo,
             ln3_g=ln3_g, ln3_b=ln3_b)
    m = dict(w_in=m_w_in, conv_w=m_conv_w, conv_b=m_conv_b, w_rg=m_w_rg, b_rg=m_b_rg, w_ig=m_w_ig, b_ig=m_b_ig,
             lru_lambda=m_lru_lambda, w_br_rnn=m_w_br_rnn, w_br_attn=m_w_br_attn, sinks=m_sinks, w_out=m_w_out,
             ln1_g=m_ln1_g, ln1_b=m_ln1_b, cq_w=m_cq_w, ckv_w=m_ckv_w, co_w=m_co_w, ln2_g=m_ln2_g, ln2_b=m_ln2_b,
             ffn_wi=m_ffn_wi, ffn_wo=m_ffn_wo, ln3_g=m_ln3_g, ln3_b=m_ln3_b)
    v = dict(w_in=v_w_in, conv_w=v_conv_w, conv_b=v_conv_b, w_rg=v_w_rg, b_rg=v_b_rg, w_ig=v_w_ig, b_ig=v_b_ig,
             lru_lambda=v_lru_lambda, w_br_rnn=v_w_br_rnn, w_br_attn=v_w_br_attn, sinks=v_sinks, w_out=v_w_out,
             ln1_g=v_ln1_g, ln1_b=v_ln1_b, cq_w=v_cq_w, ckv_w=v_ckv_w, co_w=v_co_w, ln2_g=v_ln2_g, ln2_b=v_ln2_b,
             ffn_wi=v_ffn_wi, ffn_wo=v_ffn_wo, ln3_g=v_ln3_g, ln3_b=v_ln3_b)
    my_dev = 4 * lax.axis_index("x") + 2 * lax.axis_index("y") + lax.axis_index("c")

    W = {n: [None] * DEPTH for n, _ in PACK_ROWS}
    shards0 = dict(zip([n for n, _ in PACK_ROWS], _shard_rows(w, 0)))
    gathered = _all_gather_many([shards0[n] for n in LAYER0_GROUPS[0]], "l0_gather_first")
    for n, G in zip(LAYER0_GROUPS[0], gathered):
        W[n][0] = _full_weight(G, n)
    conv_all = _all_gather(conv_w.reshape(DEPTH * CONV_WIDTH, D_MODEL // N_DEV), "gather_conv")
    W["conv_w"] = jnp.transpose(conv_all, (1, 0, 2)).reshape(DEPTH, CONV_WIDTH, D_MODEL)
    for n in SMALL_NAMES:
        W[n] = w[n]
    flying = {}
    token = conv_all
    groups = [((0, gi), LAYER0_GROUPS[gi], [shards0[n] for n in LAYER0_GROUPS[gi]]) for gi in (1, 2)]
    groups += [((l, 0), [n for n, _ in PACK_ROWS], _shard_rows(w, l)) for l in range(1, DEPTH)]
    for key, names, srcs in groups:
        lands = [lax.empty((N_DEV,) + a.shape, a.dtype) for a in srcs]
        flying[key] = (names,) + _gather_start(srcs, lands, f"l{key[0]}_gather_start{key[1]}", token)
        token = flying[key][5]
    first_token = token

    def forward(key, after):
        names, send1, recv1, srcs, lands, _ = flying[key]
        flying[key] = (names,) + _gather_forward(send1, recv1, srcs, lands, after, f"l{key[0]}_gather_forward{key[1]}")
        return flying[key][5]

    def arrive(key, after):
        names, send2, recv2, srcs, lands, _ = flying.pop(key)
        srcs, lands = _gather_wait(send2, recv2, srcs, lands, after, f"l{key[0]}_gather_wait{key[1]}")
        for n, mine, G in zip(names, srcs, lands):
            W[n][key[0]] = _full_weight(lax.dynamic_update_index_in_dim(G, mine, my_dev, 0), n)

    def at(l, point, x):
        if point == "start":
            if l == 0:
                return first_token
            arrive((l, 0), x)
        elif l == 0 and point in LAYER0_FORWARD_AT:
            return forward((0, LAYER0_FORWARD_AT[point]), x)
        elif l == 0 and point in LAYER0_WAIT_AT:
            arrive((0, LAYER0_WAIT_AT[point]), x)
        elif point == NEXT_LAYER_FORWARD_AT[l]:
            return forward((l + 1, 0), x)
        return None

    summed = {}
    sent = {}

    def finish(l, part, after):
        ssem, rsem, blocks, land, mine = sent.pop((l, part))
        _, (recv,) = _split_wait(ssem, rsem, [blocks], [land], after, f"l{l}_exchange_wait{part}")
        summed[(l, part)] = _sum_slots(recv, f"l{l}_sum_grads{part}", mine=mine)

    def send(l, part, g):
        blocks = _pack_blocks(g, GRAD_PARTS[part])
        mine = lax.dynamic_index_in_dim(blocks, my_dev, 0, keepdims=False)
        if (l + 1, part) in sent:
            finish(l + 1, part, blocks)
        ssem, rsem, (blocks,), (land,), token = _split_start([blocks], [lax.empty(blocks.shape, blocks.dtype)],
                                                             f"l{l}_exchange_start{part}")
        sent[(l, part)] = (ssem, rsem, blocks, land, mine)
        return token

    loss_local, dx, layer_grads = _local_step(x[0], mem[0], loss_target[0], W, at, send)
    for part in range(len(GRAD_PARTS)):
        finish(0, part, dx)
    loss = lax.psum(loss_local, MESH_AXES)

    small_all = _all_gather(jnp.concatenate([_pack_small(g) for g in layer_grads], axis=0), "gather_small_grads")
    small_sum = _sum_slots(small_all, "sum_small_grads").reshape(DEPTH, SMALL_ROWS, D_MODEL)

    grads = {}
    for part, names in enumerate(GRAD_PARTS):
        G = jnp.stack([summed[(l, part)] for l in range(DEPTH)])
        o = 0
        for n in names:
            r = SHARD_ROWS[n]
            blk = G[:, o:o + r, :]
            grads[n] = jnp.transpose(blk, (0, 2, 1)) if n in COL_SHARDED else blk.reshape((DEPTH,) + _SHARD_SHAPES[n])
            o += r
    conv_full = small_sum[:, :CONV_WIDTH, :]
    grads["conv_w"] = lax.dynamic_slice_in_dim(conv_full, my_dev * (D_MODEL // N_DEV), D_MODEL // N_DEV, axis=2)
    for i, n in enumerate(SMALL_NAMES):
        row = small_sum[:, CONV_WIDTH + i, :]
        grads[n] = row[:, :N_Q_HEADS] if n == "sinks" else row

    deltas, new_m, new_v = {}, {}, {}
    for n in WEIGHT_NAMES:
        deltas[n], new_m[n], new_v[n] = _adamw(w[n], grads[n], m[n], v[n], "adamw_" + n)

    return (loss, dx[None], *[grads[n] for n in WEIGHT_NAMES], *[deltas[n] for n in WEIGHT_NAMES],
            *[new_m[n] for n in WEIGHT_NAMES], *[new_v[n] for n in WEIGHT_NAMES])
```

```python
import functools
import math

import jax
import jax.numpy as jnp
from jax import lax
from jax.experimental import pallas as pl
from jax.experimental.pallas import tpu as pltpu

F32 = jnp.float32
BF16 = jnp.bfloat16

D_MODEL = 1024
DEPTH = 4
N_DEV = 8
RNN_BLOCKS = 4
RNN_BLOCK = 256
CONV_WIDTH = 4
LRU_C = 8.0
HEAD_DIM = 64
N_Q_HEADS = 16
N_KV_HEADS = 2
GROUP = 8
KV_WIDTH = 128
BLOCK = 128
ROPE_THETA = 500000.0
ROT_DIM = 16
IN_COLS = 5376
CROSS_HEADS = 4
CROSS_HEAD_DIM = 256
D_FF = 2816
LN_EPS = 1e-5
ALPHA = (2 * DEPTH) ** 0.25
NEG_INF = -1e30

ADAM_LR = 0.001
ADAM_B1 = 0.9
ADAM_B2 = 0.999
ADAM_EPS = 1e-08
ADAM_WD = 0.01
ADAM_STEP = 10

C_XR, C_GR, C_Q, C_K, C_V, C_GRNN, C_GATTN = 0, 1024, 2048, 3072, 3200, 3328, 4352

TIME_CHUNK = 256
ROW_TILE = 256

MESH_AXES = ("x", "y", "c")

PACK_ROWS = (("w_in", 672), ("w_br_rnn", 128), ("w_br_attn", 128), ("w_out", 128), ("cq_w", 128),
             ("ckv_w", 256), ("co_w", 128), ("ffn_wi", 704), ("ffn_wo", 352), ("w_rg", 32), ("w_ig", 32))
SMALL_NAMES = ("conv_b", "b_rg", "b_ig", "lru_lambda", "sinks", "ln1_g", "ln1_b", "ln2_g", "ln2_b", "ln3_g", "ln3_b")
SMALL_ROWS = 16


def _pick(dim, cands):
    for c in cands:
        if dim % c == 0:
            return c
    return dim


_DIMS = {"nn": (((1,), (0,)), ((), ())), "nt": (((1,), (1,)), ((), ())), "tn": (((0,), (0,)), ((), ()))}

MATMUL_VMEM_BUDGET = 44 * 2 ** 20
MATMUL_MAX_TILE = 2048
MXU_DIM = 256
STEP_COST_BYTES = 500_000
MIN_ROW_TILE = 512


def _tile_candidates(dim, whole=False, step=MXU_DIM):
    c = [d for d in range(step, min(dim, MATMUL_MAX_TILE) + 1, step) if dim % d == 0]
    if whole and dim not in c:
        c.append(dim)
    return c or [dim]


def _matmul_tiles(M, N, K, sa, sb, so, has_add):
    best = None
    for tk in _tile_candidates(K, whole=True):
        nk = K // tk
        for tm in [t for t in _tile_candidates(M, step=128) if t >= min(M, MIN_ROW_TILE)]:
            for tn in _tile_candidates(N):
                vmem = 2 * (tm * tk * sa + tk * tn * sb + tm * tn * so) + tm * tn * 4
                vmem += tm * tn * 4 if nk > 1 else 0
                vmem += 2 * tm * tn * 4 if has_add else 0
                vmem += (tm * tk * 2 if sa == 4 else 0) + (tk * tn * 2 if sb == 4 else 0)
                if vmem > MATMUL_VMEM_BUDGET:
                    continue
                steps = (M // tm) * (N // tn) * nk
                exposed = tm * tk * sa + tk * tn * sb + tm * tn * so
                acc_moves = steps * tm * tn * 2 if nk > 1 else 0
                fixed = M * N * so + steps * STEP_COST_BYTES + exposed + acc_moves
                a_in = M * K * sa * ((N // tn) if nk > 1 else 1) + K * N * sb * (M // tm)
                b_in = M * K * sa * (N // tn) + K * N * sb * ((M // tm) if nk > 1 else 1)
                for cost, m_outer in ((a_in + fixed, True), (b_in + fixed, False)):
                    if best is None or cost < best[0]:
                        best = (cost, tm, tn, tk, m_outer)
    return best[1:]


def _matmul(a, b, mode, name, add=None, add_scale=1.0, out_dtype=F32, after=None):
    if mode == "nn":
        (M, K), (_, N) = a.shape, b.shape
    elif mode == "nt":
        (M, K), (N, _) = a.shape, b.shape
    else:
        (K, M), (_, N) = a.shape, b.shape
    tm, tn, tk, m_outer = _matmul_tiles(M, N, K, a.dtype.itemsize, b.dtype.itemsize, jnp.dtype(out_dtype).itemsize,
                                        add is not None)
    nk = K // tk
    dims = _DIMS[mode]

    def body(*refs):
        if after is not None:
            refs = refs[:-2 - (nk > 1)] + refs[-1 - (nk > 1):]
        a_ref, b_ref = refs[0], refs[1]
        c_ref = refs[2] if add is not None else None
        o_ref = refs[3] if add is not None else refs[2]

        def finish(r):
            if add is not None:
                r = r + add_scale * c_ref[...]
            o_ref[...] = r.astype(out_dtype)

        prod = lax.dot_general(a_ref[...].astype(BF16), b_ref[...].astype(BF16), dims, preferred_element_type=F32)
        if nk == 1:
            finish(prod)
            return
        acc_ref = refs[-1]
        k = pl.program_id(2)

        @pl.when(k == 0)
        def _():
            acc_ref[...] = prod

        @pl.when(k > 0)
        def _():
            acc_ref[...] += prod

        @pl.when(k == nk - 1)
        def _():
            finish(acc_ref[...])

    ij = (lambda p, q: (p, q)) if m_outer else (lambda p, q: (q, p))
    if mode == "nn":
        a_spec = pl.BlockSpec((tm, tk), lambda p, q, k: (ij(p, q)[0], k))
        b_spec = pl.BlockSpec((tk, tn), lambda p, q, k: (k, ij(p, q)[1]))
    elif mode == "nt":
        a_spec = pl.BlockSpec((tm, tk), lambda p, q, k: (ij(p, q)[0], k))
        b_spec = pl.BlockSpec((tn, tk), lambda p, q, k: (ij(p, q)[1], k))
    else:
        a_spec = pl.BlockSpec((tk, tm), lambda p, q, k: (k, ij(p, q)[0]))
        b_spec = pl.BlockSpec((tk, tn), lambda p, q, k: (k, ij(p, q)[1]))
    o_spec = pl.BlockSpec((tm, tn), lambda p, q, k: ij(p, q))
    o_shape = jax.ShapeDtypeStruct((M, N), out_dtype)
    in_specs = [a_spec, b_spec]
    args = [a, b]
    if add is not None:
        in_specs.append(o_spec)
        args.append(add)
    if after is not None:
        in_specs.append(pl.BlockSpec(memory_space=pl.ANY))
        args.append(after)
    return pl.pallas_call(
        body, name=name, grid=(M // tm, N // tn, nk) if m_outer else (N // tn, M // tm, nk),
        in_specs=in_specs, out_specs=o_spec, out_shape=o_shape,
        scratch_shapes=[pltpu.VMEM((tm, tn), F32)] if nk > 1 else [],
        compiler_params=pltpu.CompilerParams(dimension_semantics=("parallel", "parallel", "arbitrary")),
    )(*args)


LN_ROWS = 512


def _matmul_ln(a, w, h, g, b, name):
    T, K = a.shape
    D = w.shape[1]
    tr = _pick(T, (LN_ROWS, 256, 128, 64, 32, 16, 8))

    def body(a_ref, w_ref, h_ref, g_ref, b_ref, o_ref, xh_ref, rs_ref):
        f = jnp.dot(a_ref[...].astype(BF16), w_ref[...], preferred_element_type=F32)
        z = ALPHA * h_ref[...] + f
        mu = jnp.mean(z, axis=-1, keepdims=True)
        zc = z - mu
        var = jnp.mean(zc * zc, axis=-1, keepdims=True)
        rstd = lax.rsqrt(var + LN_EPS)
        xh = zc * rstd
        xh_ref[...] = xh
        rs_ref[...] = rstd
        o_ref[...] = xh * g_ref[...] + b_ref[...]

    row = pl.BlockSpec((tr, D), lambda i: (i, 0))
    vec = pl.BlockSpec((1, D), lambda i: (0, 0))
    return pl.pallas_call(
        body, name=name, grid=(T // tr,),
        in_specs=[pl.BlockSpec((tr, K), lambda i: (i, 0)), pl.BlockSpec((K, D), lambda i: (0, 0)), row, vec, vec],
        out_specs=[row, row, pl.BlockSpec((tr, 1), lambda i: (i, 0))],
        out_shape=[jax.ShapeDtypeStruct((T, D), F32), jax.ShapeDtypeStruct((T, D), F32),
                   jax.ShapeDtypeStruct((T, 1), F32)],
        compiler_params=pltpu.CompilerParams(dimension_semantics=("parallel",)),
    )(a, w, h, g, b)


def _ln_bwd(dout, xh, rstd, g, name, after=None):
    T, D = dout.shape
    tr = _pick(T, (LN_ROWS, 256, 128, 64, 32, 16, 8))

    def body(do_ref, xh_ref, rs_ref, g_ref, *rest):
        dz_ref, dg_ref, db_ref = rest[-3:]

        @pl.when(pl.program_id(0) == 0)
        def _():
            dg_ref[...] = jnp.zeros_like(dg_ref)
            db_ref[...] = jnp.zeros_like(db_ref)

        do = do_ref[...]
        xh = xh_ref[...]
        dxh = do * g_ref[...]
        m1 = jnp.mean(dxh, axis=-1, keepdims=True)
        m2 = jnp.mean(dxh * xh, axis=-1, keepdims=True)
        dz_ref[...] = rs_ref[...] * (dxh - m1 - xh * m2)
        dg_ref[...] += jnp.sum(do * xh, axis=0, keepdims=True)
        db_ref[...] += jnp.sum(do, axis=0, keepdims=True)

    row = pl.BlockSpec((tr, D), lambda i: (i, 0))
    vec = pl.BlockSpec((1, D), lambda i: (0, 0))
    in_specs = [row, row, pl.BlockSpec((tr, 1), lambda i: (i, 0)), vec]
    args = [dout, xh, rstd, g]
    if after is not None:
        in_specs.append(pl.BlockSpec(memory_space=pl.ANY))
        args.append(after)
    return pl.pallas_call(
        body, name=name, grid=(T // tr,),
        in_specs=in_specs, out_specs=[row, vec, vec],
        out_shape=[jax.ShapeDtypeStruct((T, D), F32), jax.ShapeDtypeStruct((1, D), F32),
                   jax.ShapeDtypeStruct((1, D), F32)],
        compiler_params=pltpu.CompilerParams(dimension_semantics=("arbitrary",)),
    )(*args)


_GELU_C = math.sqrt(2.0 / math.pi)


def _gelu(x):
    t = jnp.tanh(_GELU_C * (x + 0.044715 * x * x * x))
    return 0.5 * x * (1.0 + t), t


def _gelu_grad(x, t):
    return 0.5 * (1.0 + t) + 0.5 * x * (1.0 - t * t) * _GELU_C * (1.0 + 3 * 0.044715 * x * x)


def _sigmoid(x):
    return 1.0 / (1.0 + jnp.exp(-x))


def _softplus_neg(lam):
    z = jnp.exp(-jnp.abs(lam))
    u = 1.0 + z
    l1p = jnp.where(u == 1.0, z, jnp.log(u) * z / jnp.where(u == 1.0, 1.0, u - 1.0))
    return jnp.maximum(-lam, 0.0) + l1p


def _neg_expm1(x):
    series = x * (1.0 + x * 0.5 * (1.0 + x * (1.0 / 3.0) * (1.0 + x * 0.25 * (1.0 + x * 0.2))))
    return -jnp.where(x > -0.05, series, jnp.exp(x) - 1.0)


def _scan_fwd(a, b):
    n = a.shape[0]
    rows = lax.broadcasted_iota(jnp.int32, a.shape, 0)
    s = 1
    while s < n:
        keep = rows >= s
        b = jnp.where(keep, a * pltpu.roll(b, s, 0) + b, b)
        a = jnp.where(keep, a * pltpu.roll(a, s, 0), a)
        s *= 2
    return a, b


def _scan_bwd(c, b):
    n = c.shape[0]
    rows = lax.broadcasted_iota(jnp.int32, c.shape, 0)
    s = 1
    while s < n:
        keep = rows < n - s
        b = jnp.where(keep, c * pltpu.roll(b, n - s, 0) + b, b)
        c = jnp.where(keep, c * pltpu.roll(c, n - s, 0), c)
        s *= 2
    return c, b


def _rnn_gates(xc, wr, br, wi, bi, sp):
    xb = xc.astype(BF16)
    r = _sigmoid(jnp.dot(xb, wr, preferred_element_type=F32) + br)
    i = _sigmoid(jnp.dot(xb, wi, preferred_element_type=F32) + bi)
    la = -LRU_C * r * sp
    a = jnp.exp(la)
    om = _neg_expm1(2.0 * la)
    mult = jnp.sqrt(om)
    return r, i, a, om, mult


def _rnn_specs(T):
    C = RNN_BLOCK
    col = lambda off: pl.BlockSpec((T, C), lambda n, off=off: (0, off // C + n))
    vec = pl.BlockSpec((1, C), lambda n: (0, n))
    cw = pl.BlockSpec((CONV_WIDTH, C), lambda n: (0, n))
    w = pl.BlockSpec((1, C, C), lambda n: (n, 0, 0))
    own = pl.BlockSpec((T, C), lambda n: (0, n))
    return col, vec, cw, w, own


def _rnn_fwd(P, cw, cb, wrg, brg, wig, big, lam, name):
    T = P.shape[0]
    C = RNN_BLOCK
    tc = _pick(T, (TIME_CHUNK,))
    nch = T // tc

    def body(x_ref, g_ref, cw_ref, cb_ref, wr_ref, br_ref, wi_ref, bi_ref, lam_ref, y_ref, hs_ref, xs_ref):
        sp = _softplus_neg(lam_ref[...])
        wr = wr_ref[0]
        wi = wi_ref[0]
        xs_ref[0:8, :] = jnp.zeros((8, C), F32)

        def chunk(c, hprev):
            r0 = pl.multiple_of(c * tc, tc)
            x = x_ref[pl.ds(r0, tc), :].astype(F32)
            xs_ref[8:, :] = x
            xc = cb_ref[...] + jnp.zeros((tc, C), F32)
            for k in range(CONV_WIDTH):
                xc = xc + xs_ref[pl.ds(8 - (CONV_WIDTH - 1 - k), tc), :] * cw_ref[k:k + 1, :]
            xs_ref[0:8, :] = x[tc - 8:, :]
            r, i, a, om, mult = _rnn_gates(xc, wr, br_ref[...], wi, bi_ref[...], sp)
            acum, bcum = _scan_fwd(a, mult * (i * xc))
            h = acum * hprev + bcum
            hs_ref[pl.ds(r0, tc), :] = h
            ge, _ = _gelu(g_ref[pl.ds(r0, tc), :].astype(F32))
            y_ref[pl.ds(r0, tc), :] = (h * ge).astype(BF16)
            return h[tc - 1:tc, :]

        lax.fori_loop(0, nch, chunk, jnp.zeros((1, C), F32))

    col, vec, cwspec, w, own = _rnn_specs(T)
    return pl.pallas_call(
        body, name=name, grid=(RNN_BLOCKS,),
        in_specs=[col(C_XR), col(C_GR), cwspec, vec, w, vec, w, vec, vec],
        out_specs=[own, own],
        out_shape=[jax.ShapeDtypeStruct((T, D_MODEL), BF16), jax.ShapeDtypeStruct((T, D_MODEL), F32)],
        scratch_shapes=[pltpu.VMEM((tc + 8, C), F32)],
        compiler_params=pltpu.CompilerParams(dimension_semantics=("parallel",)),
    )(P, P, cw, cb, wrg, brg, wig, big, lam)


def _rnn_bwd(P, hs, dy, cw, cb, wrg, brg, wig, big, lam, name):
    T = P.shape[0]
    C = RNN_BLOCK
    tc = _pick(T, (TIME_CHUNK,))
    nch = T // tc

    def body(x_ref, g_ref, hs_ref, dy_ref, cw_ref, cb_ref, wr_ref, br_ref, wi_ref, bi_ref, lam_ref,
             dx_ref, dg_ref, dcw_ref, dcb_ref, dwr_ref, dbr_ref, dwi_ref, dbi_ref, dlam_ref,
             xs_ref, hp_ref, an_ref, dn_ref):
        lam_v = lam_ref[...]
        sp = _softplus_neg(lam_v)
        wr = wr_ref[0]
        wi = wi_ref[0]
        dcw_ref[...] = jnp.zeros_like(dcw_ref)
        dcb_ref[...] = jnp.zeros_like(dcb_ref)
        dwr_ref[...] = jnp.zeros_like(dwr_ref)
        dbr_ref[...] = jnp.zeros_like(dbr_ref)
        dwi_ref[...] = jnp.zeros_like(dwi_ref)
        dbi_ref[...] = jnp.zeros_like(dbi_ref)
        dlam_ref[...] = jnp.zeros_like(dlam_ref)
        an_ref[tc:, :] = jnp.zeros((8, C), F32)
        dn_ref[tc:, :] = jnp.zeros((8, C), F32)

        def chunk(step, gnext):
            c = nch - 1 - step
            r0 = pl.multiple_of(c * tc, tc)
            p0 = pl.multiple_of(jnp.maximum(r0 - 8, 0), 8)
            q0 = pl.multiple_of(jnp.maximum(r0 - 16, 0), 16)
            live = c > 0
            x = x_ref[pl.ds(r0, tc), :].astype(F32)
            xs_ref[0:8, :] = jnp.where(live, x_ref[pl.ds(q0, 16), :].astype(F32)[8:, :], 0.0)
            xs_ref[8:, :] = x
            xsh = [xs_ref[pl.ds(8 - (CONV_WIDTH - 1 - k), tc), :] for k in range(CONV_WIDTH)]
            xc = cb_ref[...] + jnp.zeros((tc, C), F32)
            for k in range(CONV_WIDTH):
                xc = xc + xsh[k] * cw_ref[k:k + 1, :]
            r, i, a, om, mult = _rnn_gates(xc, wr, br_ref[...], wi, bi_ref[...], sp)
            h = hs_ref[pl.ds(r0, tc), :]
            hp_ref[0:8, :] = jnp.where(live, hs_ref[pl.ds(p0, 8), :], 0.0)
            hp_ref[8:, :] = h
            hm1 = hp_ref[pl.ds(7, tc), :]
            g = g_ref[pl.ds(r0, tc), :].astype(F32)
            ge, th = _gelu(g)
            dy = dy_ref[pl.ds(r0, tc), :]
            dg_ref[pl.ds(r0, tc), :] = (dy * h * _gelu_grad(g, th)).astype(BF16)
            an_ref[0:tc, :] = a
            coef = an_ref[pl.ds(1, tc), :]
            ccum, bcum = _scan_bwd(coef, dy * ge)
            G = bcum + ccum * gnext
            an_ref[tc:, :] = a[0:8, :]
            da = G * hm1
            ixc = i * xc
            dmult = G * ixc
            di = G * mult * xc
            dxc = G * mult * i
            dla = da * a - dmult * (1.0 - om) / mult
            dr = dla * (-LRU_C * sp)
            dlam_ref[...] += jnp.sum(dla * r, axis=0, keepdims=True)
            dzr = dr * r * (1.0 - r)
            dzi = di * i * (1.0 - i)
            dbr_ref[...] += jnp.sum(dzr, axis=0, keepdims=True)
            dbi_ref[...] += jnp.sum(dzi, axis=0, keepdims=True)
            xb = xc.astype(BF16)
            dzrb = dzr.astype(BF16)
            dzib = dzi.astype(BF16)
            dwr_ref[0] += lax.dot_general(xb, dzrb, _DIMS["tn"], preferred_element_type=F32)
            dwi_ref[0] += lax.dot_general(xb, dzib, _DIMS["tn"], preferred_element_type=F32)
            dxc = dxc + lax.dot_general(dzrb, wr, _DIMS["nt"], preferred_element_type=F32)
            dxc = dxc + lax.dot_general(dzib, wi, _DIMS["nt"], preferred_element_type=F32)
            dcb_ref[...] += jnp.sum(dxc, axis=0, keepdims=True)
            for k in range(CONV_WIDTH):
                dcw_ref[k:k + 1, :] += jnp.sum(dxc * xsh[k], axis=0, keepdims=True)
            dn_ref[0:tc, :] = dxc
            dx = jnp.zeros((tc, C), F32)
            for k in range(CONV_WIDTH):
                dx = dx + dn_ref[pl.ds(CONV_WIDTH - 1 - k, tc), :] * cw_ref[k:k + 1, :]
            dn_ref[tc:, :] = dxc[0:8, :]
            dx_ref[pl.ds(r0, tc), :] = dx.astype(BF16)
            return G[0:1, :]

        lax.fori_loop(0, nch, chunk, jnp.zeros((1, C), F32))
        dlam_ref[...] = dlam_ref[...] * (LRU_C * _sigmoid(-lam_v))

    col, vec, cwspec, w, own = _rnn_specs(T)
    vshape = jax.ShapeDtypeStruct((1, D_MODEL), F32)
    wshape = jax.ShapeDtypeStruct((RNN_BLOCKS, C, C), F32)
    return pl.pallas_call(
        body, name=name, grid=(RNN_BLOCKS,),
        in_specs=[col(C_XR), col(C_GR), own, own, cwspec, vec, w, vec, w, vec, vec],
        out_specs=[own, own, cwspec, vec, w, vec, w, vec, vec],
        out_shape=[jax.ShapeDtypeStruct((T, D_MODEL), BF16), jax.ShapeDtypeStruct((T, D_MODEL), BF16),
                   jax.ShapeDtypeStruct((CONV_WIDTH, D_MODEL), F32), vshape, wshape, vshape, wshape, vshape, vshape],
        scratch_shapes=[pltpu.VMEM((tc + 8, C), F32), pltpu.VMEM((tc + 8, C), F32),
                        pltpu.VMEM((tc + 8, C), F32), pltpu.VMEM((tc + 8, C), F32)],
        compiler_params=pltpu.CompilerParams(dimension_semantics=("parallel",)),
    )(P, P, hs, dy, cw, cb, wrg, brg, wig, big, lam)


def _rope_table(T):
    half = ROT_DIM // 2
    pos = jnp.arange(T, dtype=F32)
    inv_freq = ROPE_THETA ** (-jnp.arange(0, ROT_DIM, 2, dtype=F32) / ROT_DIM)
    ang = pos[:, None] * inv_freq[None, :]
    cos, sin = jnp.cos(ang), jnp.sin(ang)
    one = jnp.ones((T, HEAD_DIM - ROT_DIM), F32)
    zero = jnp.zeros((T, HEAD_DIM - ROT_DIM), F32)
    z8 = jnp.zeros((T, half), F32)
    c = jnp.concatenate([cos, cos, one], axis=1)
    a = jnp.concatenate([-sin, z8, zero], axis=1)
    b = jnp.concatenate([z8, sin, zero], axis=1)
    return jnp.stack([jnp.tile(c, (1, 2)), jnp.tile(a, (1, 2)), jnp.tile(b, (1, 2))])


def _rope(x, tab, sign):
    W = x.shape[1]
    rep = W // 128
    c = jnp.tile(tab[0], (1, rep)) if rep > 1 else tab[0]
    a = jnp.tile(tab[1], (1, rep)) if rep > 1 else tab[1]
    b = jnp.tile(tab[2], (1, rep)) if rep > 1 else tab[2]
    return x * c + sign * (pltpu.roll(x, W - ROT_DIM // 2, 1) * a + pltpu.roll(x, ROT_DIM // 2, 1) * b)


def _swa_mask(n):
    rows = lax.broadcasted_iota(jnp.int32, (GROUP * BLOCK, 2 * BLOCK), 0) & (BLOCK - 1)
    cols = lax.broadcasted_iota(jnp.int32, (GROUP * BLOCK, 2 * BLOCK), 1)
    return (cols > rows) & (cols <= rows + BLOCK) & ((n > 0) | (cols >= BLOCK))


def _swa_probs(qg, k2, sink, valid):
    s = lax.dot_general(qg, k2, _DIMS["nt"], preferred_element_type=F32) * (HEAD_DIM ** -0.5)
    s = jnp.where(valid, s, NEG_INF)
    m = jnp.maximum(jnp.max(s, axis=1, keepdims=True), sink)
    p = jnp.exp(s - m)
    ps = jnp.exp(sink - m)
    inv = 1.0 / (jnp.sum(p, axis=1, keepdims=True) + ps)
    return p * inv, ps * inv


def _swa_specs(T):
    nb = T // BLOCK
    qspec = pl.BlockSpec((BLOCK, D_MODEL), lambda n: (n, C_Q // D_MODEL))
    cur = lambda off: pl.BlockSpec((BLOCK, KV_WIDTH), lambda n, off=off: (n, off // KV_WIDTH))
    prev = lambda off: pl.BlockSpec((BLOCK, KV_WIDTH), lambda n, off=off: (jnp.maximum(n - 1, 0), off // KV_WIDTH))
    tcur = pl.BlockSpec((3, BLOCK, 128), lambda n: (0, n, 0))
    tprev = pl.BlockSpec((3, BLOCK, 128), lambda n: (0, jnp.maximum(n - 1, 0), 0))
    sink = pl.BlockSpec((N_KV_HEADS, GROUP * BLOCK, 1), lambda n: (0, 0, 0))
    own = pl.BlockSpec((BLOCK, D_MODEL), lambda n: (n, 0))
    return nb, qspec, cur, prev, tcur, tprev, sink, own


def _stack_heads(x, hk):
    return jnp.concatenate([x[:, (hk * GROUP + g) * HEAD_DIM:(hk * GROUP + g + 1) * HEAD_DIM] for g in range(GROUP)],
                           axis=0)


def _swa_fwd(P, tab, sink_col, name):
    T = P.shape[0]
    nb, qspec, cur, prev, tcur, tprev, sink, own = _swa_specs(T)

    def body(q_ref, kc_ref, kp_ref, vc_ref, vp_ref, tc_ref, tp_ref, sk_ref, o_ref):
        n = pl.program_id(0)
        valid = _swa_mask(n)
        q = _rope(q_ref[...].astype(F32), tc_ref[...], 1.0).astype(BF16)
        k2 = jnp.concatenate([_rope(kp_ref[...].astype(F32), tp_ref[...], 1.0),
                              _rope(kc_ref[...].astype(F32), tc_ref[...], 1.0)], axis=0).astype(BF16)
        v2 = jnp.concatenate([vp_ref[...], vc_ref[...]], axis=0).astype(BF16)
        parts = []
        for hk in range(N_KV_HEADS):
            sl = slice(hk * HEAD_DIM, (hk + 1) * HEAD_DIM)
            pn, _ = _swa_probs(_stack_heads(q, hk), k2[:, sl], sk_ref[hk], valid)
            og = jnp.dot(pn.astype(BF16), v2[:, sl], preferred_element_type=F32)
            parts += [og[g * BLOCK:(g + 1) * BLOCK, :] for g in range(GROUP)]
        o_ref[...] = jnp.concatenate(parts, axis=1).astype(BF16)

    return pl.pallas_call(
        body, name=name, grid=(nb,),
        in_specs=[qspec, cur(C_K), prev(C_K), cur(C_V), prev(C_V), tcur, tprev, sink],
        out_specs=own, out_shape=jax.ShapeDtypeStruct((T, D_MODEL), BF16),
        compiler_params=pltpu.CompilerParams(dimension_semantics=("parallel",)),
    )(P, P, P, P, P, tab, tab, sink_col)


def _swa_bwd(P, do, tab, sink_col, name):
    T = P.shape[0]
    nb, qspec, cur, prev, tcur, tprev, sink, own = _swa_specs(T)

    def body(q_ref, kc_ref, kp_ref, vc_ref, vp_ref, do_ref, tc_ref, tp_ref, sk_ref,
             dq_ref, dk_ref, dv_ref, ds_ref):
        n = pl.program_id(0)

        @pl.when(n == 0)
        def _():
            dk_ref[...] = jnp.zeros_like(dk_ref)
            dv_ref[...] = jnp.zeros_like(dv_ref)
            ds_ref[...] = jnp.zeros_like(ds_ref)

        valid = _swa_mask(n)
        tcur_v = tc_ref[...]
        tprev_v = tp_ref[...]
        q = _rope(q_ref[...].astype(F32), tcur_v, 1.0).astype(BF16)
        k2 = jnp.concatenate([_rope(kp_ref[...].astype(F32), tprev_v, 1.0),
                              _rope(kc_ref[...].astype(F32), tcur_v, 1.0)], axis=0).astype(BF16)
        v2 = jnp.concatenate([vp_ref[...], vc_ref[...]], axis=0).astype(BF16)
        dob = do_ref[...].astype(BF16)
        dq_parts = []
        dk_parts = []
        dv_parts = []
        for hk in range(N_KV_HEADS):
            sl = slice(hk * HEAD_DIM, (hk + 1) * HEAD_DIM)
            qg = _stack_heads(q, hk)
            dog = _stack_heads(dob, hk)
            pn, psn = _swa_probs(qg, k2[:, sl], sk_ref[hk], valid)
            dp = lax.dot_general(dog, v2[:, sl], _DIMS["nt"], preferred_element_type=F32)
            delta = jnp.sum(pn * dp, axis=1, keepdims=True)
            dsc = (pn * (dp - delta) * (HEAD_DIM ** -0.5)).astype(BF16)
            dsink = -psn * delta
            for g in range(GROUP):
                ds_ref[hk, g:g + 1, :] += jnp.broadcast_to(
                    jnp.sum(dsink[g * BLOCK:(g + 1) * BLOCK], axis=0, keepdims=True), (1, 128))
            dqg = jnp.dot(dsc, k2[:, sl], preferred_element_type=F32)
            dq_parts += [dqg[g * BLOCK:(g + 1) * BLOCK, :] for g in range(GROUP)]
            dk_parts.append(lax.dot_general(dsc, qg, _DIMS["tn"], preferred_element_type=F32))
            dv_parts.append(lax.dot_general(pn.astype(BF16), dog, _DIMS["tn"], preferred_element_type=F32))
        dq_ref[...] = _rope(jnp.concatenate(dq_parts, axis=1), tcur_v, -1.0).astype(BF16)
        dk2 = jnp.concatenate(dk_parts, axis=1)
        dv2 = jnp.concatenate(dv_parts, axis=1)
        c0 = pl.multiple_of(n * BLOCK, BLOCK)
        p0 = pl.multiple_of(jnp.maximum(n - 1, 0) * BLOCK, BLOCK)
        dk_ref[pl.ds(p0, BLOCK), :] += _rope(dk2[:BLOCK], tprev_v, -1.0)
        dv_ref[pl.ds(p0, BLOCK), :] += dv2[:BLOCK]
        dk_ref[pl.ds(c0, BLOCK), :] += _rope(dk2[BLOCK:], tcur_v, -1.0)
        dv_ref[pl.ds(c0, BLOCK), :] += dv2[BLOCK:]

    full = pl.BlockSpec((T, KV_WIDTH), lambda n: (0, 0))
    return pl.pallas_call(
        body, name=name, grid=(nb,),
        in_specs=[qspec, cur(C_K), prev(C_K), cur(C_V), prev(C_V), own, tcur, tprev, sink],
        out_specs=[own, full, full, pl.BlockSpec((N_KV_HEADS, GROUP, 128), lambda n: (0, 0, 0))],
        out_shape=[jax.ShapeDtypeStruct((T, D_MODEL), BF16), jax.ShapeDtypeStruct((T, KV_WIDTH), F32),
                   jax.ShapeDtypeStruct((T, KV_WIDTH), F32), jax.ShapeDtypeStruct((N_KV_HEADS, GROUP, 128), F32)],
        compiler_params=pltpu.CompilerParams(dimension_semantics=("arbitrary",)),
    )(P, P, P, P, P, do, tab, tab, sink_col)


_MW = 256


def _gate_specs(T, rows, width):
    tr = _pick(T, (rows, 256, 128, 64, 32, 16, 8))
    col = lambda off: pl.BlockSpec((tr, width), lambda i, j, off=off: (i, off // width + j))
    own = pl.BlockSpec((tr, width), lambda i, j: (i, j))
    return tr, col, own


def _merge_fwd(P, mr, ma, name):
    T = P.shape[0]
    tr, col, own = _gate_specs(T, 1024, _MW)

    def body(gr_ref, ga_ref, mr_ref, ma_ref, o_ref):
        o_ref[...] = (_sigmoid(gr_ref[...].astype(F32)) * mr_ref[...]
                      + _sigmoid(ga_ref[...].astype(F32)) * ma_ref[...]).astype(BF16)

    return pl.pallas_call(
        body, name=name, grid=(T // tr, D_MODEL // _MW), in_specs=[col(C_GRNN), col(C_GATTN), own, own],
        out_specs=own, out_shape=jax.ShapeDtypeStruct((T, D_MODEL), BF16),
        compiler_params=pltpu.CompilerParams(dimension_semantics=("parallel", "parallel")),
    )(P, P, mr, ma)


def _merge_bwd(P, mr, ma, dm, name):
    T = P.shape[0]
    tr, col, own = _gate_specs(T, 512, _MW)

    def body(gr_ref, ga_ref, mr_ref, ma_ref, dm_ref, dmr_ref, dma_ref, dgr_ref, dga_ref):
        dm = dm_ref[...]
        sr = _sigmoid(gr_ref[...].astype(F32))
        sa = _sigmoid(ga_ref[...].astype(F32))
        dmr_ref[...] = (dm * sr).astype(BF16)
        dma_ref[...] = (dm * sa).astype(BF16)
        dgr_ref[...] = (dm * mr_ref[...] * sr * (1.0 - sr)).astype(BF16)
        dga_ref[...] = (dm * ma_ref[...] * sa * (1.0 - sa)).astype(BF16)

    shp = jax.ShapeDtypeStruct((T, D_MODEL), BF16)
    return pl.pallas_call(
        body, name=name, grid=(T // tr, D_MODEL // _MW), in_specs=[col(C_GRNN), col(C_GATTN), own, own, own],
        out_specs=[own] * 4, out_shape=[shp] * 4,
        compiler_params=pltpu.CompilerParams(dimension_semantics=("parallel", "parallel")),
    )(P, P, mr, ma, dm)


_FFN_ROWS = 128


def _swiglu_fwd(U, name):
    T = U.shape[0]
    tr = _pick(T, (_FFN_ROWS, 64, 32, 16))
    half = lambda j: pl.BlockSpec((tr, D_FF), lambda i, j=j: (i, j))

    def body(g_ref, u_ref, o_ref):
        g = g_ref[...].astype(F32)
        o_ref[...] = (g * _sigmoid(g) * u_ref[...].astype(F32)).astype(BF16)

    return pl.pallas_call(
        body, name=name, grid=(T // tr,), in_specs=[half(0), half(1)],
        out_specs=half(0), out_shape=jax.ShapeDtypeStruct((T, D_FF), BF16),
        compiler_params=pltpu.CompilerParams(dimension_semantics=("parallel",)),
    )(U, U)


def _swiglu_bwd(U, dact, name):
    T = U.shape[0]
    tr = _pick(T, (_FFN_ROWS, 64, 32, 16))
    half = lambda j: pl.BlockSpec((tr, D_FF), lambda i, j=j: (i, j))

    def body(g_ref, u_ref, da_ref, o_ref):
        g = g_ref[...].astype(F32)
        da = da_ref[...].astype(F32)
        s = _sigmoid(g)
        o_ref[:, :D_FF] = (da * u_ref[...].astype(F32) * s * (1.0 + g * (1.0 - s))).astype(BF16)
        o_ref[:, D_FF:] = (da * g * s).astype(BF16)

    return pl.pallas_call(
        body, name=name, grid=(T // tr,), in_specs=[half(0), half(1), half(0)],
        out_specs=pl.BlockSpec((tr, 2 * D_FF), lambda i: (i, 0)),
        out_shape=jax.ShapeDtypeStruct((T, 2 * D_FF), BF16),
        compiler_params=pltpu.CompilerParams(dimension_semantics=("parallel",)),
    )(U, U, dact)


def _cross_probs(qh, kh):
    s = lax.dot_general(qh, kh, _DIMS["nt"], preferred_element_type=F32) * (CROSS_HEAD_DIM ** -0.5)
    p = jnp.exp(s - jnp.max(s, axis=1, keepdims=True))
    return p / jnp.sum(p, axis=1, keepdims=True)


def _cross_fwd(q, kv, name):
    T = q.shape[0]
    M = kv.shape[0]
    tr = _pick(T, (ROW_TILE, 128, 64, 32, 16, 8))
    W = CROSS_HEAD_DIM

    def body(q_ref, kv_ref, o_ref):
        for h in range(CROSS_HEADS):
            qh = q_ref[:, h * W:(h + 1) * W].astype(BF16)
            kh = kv_ref[:, h * W:(h + 1) * W].astype(BF16)
            vh = kv_ref[:, D_MODEL + h * W:D_MODEL + (h + 1) * W].astype(BF16)
            pn = _cross_probs(qh, kh)
            o_ref[:, h * W:(h + 1) * W] = jnp.dot(pn.astype(BF16), vh, preferred_element_type=F32).astype(BF16)

    row = pl.BlockSpec((tr, D_MODEL), lambda i: (i, 0))
    return pl.pallas_call(
        body, name=name, grid=(T // tr,), in_specs=[row, pl.BlockSpec((M, 2 * D_MODEL), lambda i: (0, 0))],
        out_specs=row, out_shape=jax.ShapeDtypeStruct((T, D_MODEL), BF16),
        compiler_params=pltpu.CompilerParams(dimension_semantics=("parallel",)),
    )(q, kv)


def _cross_bwd(q, kv, do, name):
    T = q.shape[0]
    M = kv.shape[0]
    tr = _pick(T, (ROW_TILE, 128, 64, 32, 16, 8))
    W = CROSS_HEAD_DIM

    def body(q_ref, kv_ref, do_ref, dq_ref, dkv_ref):
        @pl.when(pl.program_id(0) == 0)
        def _():
            dkv_ref[...] = jnp.zeros_like(dkv_ref)

        for h in range(CROSS_HEADS):
            qh = q_ref[:, h * W:(h + 1) * W].astype(BF16)
            kh = kv_ref[:, h * W:(h + 1) * W].astype(BF16)
            vh = kv_ref[:, D_MODEL + h * W:D_MODEL + (h + 1) * W].astype(BF16)
            doh = do_ref[:, h * W:(h + 1) * W].astype(BF16)
            pn = _cross_probs(qh, kh)
            dp = lax.dot_general(doh, vh, _DIMS["nt"], preferred_element_type=F32)
            delta = jnp.sum(pn * dp, axis=1, keepdims=True)
            dsc = (pn * (dp - delta) * (W ** -0.5)).astype(BF16)
            dq_ref[:, h * W:(h + 1) * W] = jnp.dot(dsc, kh, preferred_element_type=F32).astype(BF16)
            dkv_ref[:, h * W:(h + 1) * W] += lax.dot_general(dsc, qh, _DIMS["tn"], preferred_element_type=F32)
            dkv_ref[:, D_MODEL + h * W:D_MODEL + (h + 1) * W] += lax.dot_general(
                pn.astype(BF16), doh, _DIMS["tn"], preferred_element_type=F32)

    row = pl.BlockSpec((tr, D_MODEL), lambda i: (i, 0))
    full = pl.BlockSpec((M, 2 * D_MODEL), lambda i: (0, 0))
    return pl.pallas_call(
        body, name=name, grid=(T // tr,), in_specs=[row, full, row], out_specs=[row, full],
        out_shape=[jax.ShapeDtypeStruct((T, D_MODEL), BF16), jax.ShapeDtypeStruct((M, 2 * D_MODEL), F32)],
        compiler_params=pltpu.CompilerParams(dimension_semantics=("arbitrary",)),
    )(q, kv, do)


def _loss_head(y, target, name):
    T, D = y.shape
    tr = _pick(T, (ROW_TILE, 128, 64, 32, 16, 8))

    def body(y_ref, t_ref, l_ref, dy_ref):
        @pl.when(pl.program_id(0) == 0)
        def _():
            l_ref[...] = jnp.zeros_like(l_ref)

        err = y_ref[...] - t_ref[...]
        dy_ref[...] = err * (1.0 / D)
        l_ref[...] += jnp.broadcast_to(0.5 * jnp.sum(jnp.mean(err * err, axis=-1, keepdims=True), axis=0, keepdims=True),
                                       (8, 128))

    row = pl.BlockSpec((tr, D), lambda i: (i, 0))
    return pl.pallas_call(
        body, name=name, grid=(T // tr,), in_specs=[row, row],
        out_specs=[pl.BlockSpec((8, 128), lambda i: (0, 0)), row],
        out_shape=[jax.ShapeDtypeStruct((8, 128), F32), jax.ShapeDtypeStruct((T, D), F32)],
        compiler_params=pltpu.CompilerParams(dimension_semantics=("arbitrary",)),
    )(y, target)


def _sum_slots(recv, name, mine=None):
    _, R, C = recv.shape
    tr = _pick(R, (ROW_TILE, 224, 368, 128, 64, 32, 16, 8))

    def body(*refs):
        r_ref, o_ref = refs[0], refs[-1]
        if mine is None:
            acc = r_ref[0].astype(F32)
            for d in range(1, N_DEV):
                acc = acc + r_ref[d].astype(F32)
        else:
            me = _my_slot()
            acc = refs[1][...].astype(F32)
            for d in range(N_DEV):
                acc = acc + jnp.where(d == me, 0.0, r_ref[d].astype(F32))
        o_ref[...] = acc

    in_specs = [pl.BlockSpec((N_DEV, tr, C), lambda i: (0, i, 0))]
    args = [recv]
    if mine is not None:
        in_specs.append(pl.BlockSpec((tr, C), lambda i: (i, 0)))
        args.append(mine)
    return pl.pallas_call(
        body, name=name, grid=(R // tr,), in_specs=in_specs,
        out_specs=pl.BlockSpec((tr, C), lambda i: (i, 0)), out_shape=jax.ShapeDtypeStruct((R, C), F32),
        compiler_params=pltpu.CompilerParams(dimension_semantics=("parallel",)),
    )(*args)


def _adamw(w, g, m, v, name):
    shape = w.shape
    C = shape[-1]
    R = math.prod(shape[:-1])
    w2, g2, m2, v2 = (t.reshape(R, C) for t in (w, g, m, v))
    tr = _pick(R, (ROW_TILE, 128, 64, 32, 16, 8))

    def body(w_ref, g_ref, m_ref, v_ref, d_ref, mo_ref, vo_ref):
        gg = g_ref[...]
        mn = ADAM_B1 * m_ref[...] + (1.0 - ADAM_B1) * gg
        vn = ADAM_B2 * v_ref[...] + (1.0 - ADAM_B2) * (gg * gg)
        m_hat = mn / (1.0 - ADAM_B1 ** ADAM_STEP)
        v_hat = vn / (1.0 - ADAM_B2 ** ADAM_STEP)
        d_ref[...] = -ADAM_LR * (m_hat / (jnp.sqrt(v_hat) + ADAM_EPS) + ADAM_WD * w_ref[...])
        mo_ref[...] = mn
        vo_ref[...] = vn

    blk = pl.BlockSpec((tr, C), lambda i: (i, 0))
    shp = jax.ShapeDtypeStruct((R, C), F32)
    d, mo, vo = pl.pallas_call(
        body, name=name, grid=(R // tr,), in_specs=[blk] * 4, out_specs=[blk] * 3, out_shape=[shp] * 3,
        compiler_params=pltpu.CompilerParams(dimension_semantics=("parallel",)),
    )(w2, g2, m2, v2)
    return d.reshape(shape), mo.reshape(shape), vo.reshape(shape)


def _all_gather_many(bufs, name):
    n = len(bufs)

    def body(*refs):
        xs, outs = refs[:n], refs[n:2 * n]
        send_sems, recv_sems, local_sems = refs[2 * n:]
        x, y, c = lax.axis_index("x"), lax.axis_index("y"), lax.axis_index("c")
        me, sibling = (x, y, c), (x, y, 1 - c)
        chips = [(1 - x, y), (x, 1 - y), (1 - x, 1 - y)]

        def slot(i, px, py, pc):
            return outs[i].at[4 * px + 2 * py + pc]

        def copy(i, k, block, to, src=None):
            return pltpu.make_async_remote_copy(
                src_ref=slot(i, *block) if src is None else src, dst_ref=slot(i, *block),
                send_sem=send_sems.at[7 * i + k], recv_sem=recv_sems.at[7 * i + k],
                device_id=to, device_id_type=pl.DeviceIdType.MESH)

        mine = [pltpu.make_async_copy(xs[i], slot(i, *me), local_sems.at[i]) for i in range(n)]
        for cp in mine:
            cp.start()
        first = [copy(i, 0, me, sibling, src=xs[i]) for i in range(n)]
        for j, chip in enumerate(chips):
            first += [copy(i, 1 + j, me, (*chip, c), src=xs[i]) for i in range(n)]
        for cp in first:
            cp.start()
        passed = []
        for j, chip in enumerate(chips):
            for i in range(n):
                copy(i, 1 + j, (*chip, c), me).wait_recv()
                passed.append(copy(i, 4 + j, (*chip, c), sibling))
                passed[-1].start()
        for i in range(n):
            copy(i, 0, sibling, me).wait_recv()
        for j, chip in enumerate(chips):
            for i in range(n):
                copy(i, 4 + j, (*chip, 1 - c), me).wait_recv()
        for cp in first + passed:
            cp.wait_send()
        for cp in mine:
            cp.wait()

    hbm = pl.BlockSpec(memory_space=pl.ANY)
    return pl.pallas_call(
        body, name=name, out_shape=[jax.ShapeDtypeStruct((N_DEV,) + b.shape, b.dtype) for b in bufs],
        in_specs=[hbm] * n, out_specs=[hbm] * n,
        scratch_shapes=[pltpu.SemaphoreType.DMA((7 * n,)), pltpu.SemaphoreType.DMA((7 * n,)),
                        pltpu.SemaphoreType.DMA((n,))],
    )(*bufs)


def _all_gather(buf, name):
    return _all_gather_many([buf], name)[0]


_HBM = pl.BlockSpec(memory_space=pltpu.HBM)
_SEM = pl.BlockSpec(memory_space=pltpu.SEMAPHORE)
_EFFECT = pltpu.SideEffectType.DATAFLOW_SIDE_EFFECTING


def _peer(k):
    x, y, c = lax.axis_index("x"), lax.axis_index("y"), lax.axis_index("c")
    return x ^ ((k >> 2) & 1), y ^ ((k >> 1) & 1), c ^ (k & 1)


def _my_slot():
    return 4 * lax.axis_index("x") + 2 * lax.axis_index("y") + lax.axis_index("c")


def _split_copy(src_refs, land_refs, send_sems, recv_sems, i, k):
    px, py, pc = _peer(k)
    return pltpu.make_async_remote_copy(
        src_ref=src_refs[i].at[4 * px + 2 * py + pc], dst_ref=land_refs[i].at[_my_slot()],
        send_sem=send_sems.at[7 * i + k - 1], recv_sem=recv_sems.at[7 * i + k - 1],
        device_id=(px, py, pc), device_id_type=pl.DeviceIdType.MESH)


def _split_start(srcs, lands, name):
    n = len(srcs)

    def body(*refs):
        src_refs, land_refs = refs[:n], refs[n:2 * n]
        send_sems, recv_sems = refs[2 * n], refs[2 * n + 1]
        token = refs[-1]
        for i in range(n):
            for k in range(1, N_DEV):
                _split_copy(src_refs, land_refs, send_sems, recv_sems, i, k).start()
        token[...] = jnp.zeros_like(token)

    outs = pl.pallas_call(
        body, name=name,
        out_shape=(pltpu.SemaphoreType.DMA((7 * n,)), pltpu.SemaphoreType.DMA((7 * n,)),
                   *[pltpu.HBM(a.shape, a.dtype) for a in srcs], *[pltpu.HBM(a.shape, a.dtype) for a in lands],
                   jax.ShapeDtypeStruct((8, 128), F32)),
        in_specs=[_HBM] * (2 * n),
        out_specs=(_SEM, _SEM, *([_HBM] * (2 * n)), pl.BlockSpec(memory_space=pltpu.VMEM)),
        input_output_aliases={i: 2 + i for i in range(2 * n)},
        compiler_params=pltpu.CompilerParams(has_side_effects=_EFFECT),
    )(*[pltpu.with_memory_space_constraint(a, pltpu.HBM) for a in list(srcs) + list(lands)])
    return outs[0], outs[1], outs[2:2 + n], outs[2 + n:2 + 2 * n], outs[-1]


def _split_wait(send_sems, recv_sems, srcs, lands, after, name):
    n = len(srcs)

    def body(*refs):
        src_refs, land_refs = refs[:n], refs[n:2 * n]
        ssem, rsem = refs[2 * n], refs[2 * n + 1]
        for i in range(n):
            for k in range(1, N_DEV):
                cp = _split_copy(src_refs, land_refs, ssem, rsem, i, k)
                cp.wait_send()
                cp.wait_recv()

    outs = pl.pallas_call(
        body, name=name,
        out_shape=(*[pltpu.HBM(a.shape, a.dtype) for a in srcs], *[pltpu.HBM(a.shape, a.dtype) for a in lands]),
        in_specs=[*([_HBM] * (2 * n)), _SEM, _SEM, pl.BlockSpec(memory_space=pl.ANY)],
        out_specs=tuple([_HBM] * (2 * n)),
        input_output_aliases={i: i for i in range(2 * n)},
        compiler_params=pltpu.CompilerParams(has_side_effects=_EFFECT),
    )(*srcs, *lands, send_sems, recv_sems, after)
    return outs[:n], outs[n:]


def _gather_first(src_refs, land_refs, send_sems, recv_sems, i, k):
    x, y, c = lax.axis_index("x"), lax.axis_index("y"), lax.axis_index("c")
    to = ((x, y, 1 - c), (1 - x, y, c), (x, 1 - y, c), (1 - x, 1 - y, c))[k]
    return pltpu.make_async_remote_copy(
        src_ref=src_refs[i], dst_ref=land_refs[i].at[_my_slot()],
        send_sem=send_sems.at[4 * i + k], recv_sem=recv_sems.at[4 * i + k],
        device_id=to, device_id_type=pl.DeviceIdType.MESH)


def _gather_second(land_refs, send_sems, recv_sems, i, j):
    x, y, c = lax.axis_index("x"), lax.axis_index("y"), lax.axis_index("c")
    px, py = ((1 - x, y), (x, 1 - y), (1 - x, 1 - y))[j]
    slot = land_refs[i].at[4 * px + 2 * py + c]
    return pltpu.make_async_remote_copy(
        src_ref=slot, dst_ref=slot, send_sem=send_sems.at[3 * i + j], recv_sem=recv_sems.at[3 * i + j],
        device_id=(x, y, 1 - c), device_id_type=pl.DeviceIdType.MESH)


def _gather_start(srcs, lands, name, after):
    n = len(srcs)

    def body(*refs):
        src_refs, land_refs = refs[:n], refs[n:2 * n]
        send_sems, recv_sems = refs[2 * n + 1], refs[2 * n + 2]
        token = refs[-1]
        for k in range(4):
            for i in range(n):
                _gather_first(src_refs, land_refs, send_sems, recv_sems, i, k).start()
        token[...] = jnp.zeros_like(token)

    outs = pl.pallas_call(
        body, name=name,
        out_shape=(pltpu.SemaphoreType.DMA((4 * n,)), pltpu.SemaphoreType.DMA((4 * n,)),
                   *[pltpu.HBM(a.shape, a.dtype) for a in srcs], *[pltpu.HBM(a.shape, a.dtype) for a in lands],
                   jax.ShapeDtypeStruct((8, 128), F32)),
        in_specs=[_HBM] * (2 * n) + [pl.BlockSpec(memory_space=pl.ANY)],
        out_specs=(_SEM, _SEM, *([_HBM] * (2 * n)), pl.BlockSpec(memory_space=pltpu.VMEM)),
        input_output_aliases={i: 2 + i for i in range(2 * n)},
        compiler_params=pltpu.CompilerParams(has_side_effects=_EFFECT),
    )(*[pltpu.with_memory_space_constraint(a, pltpu.HBM) for a in list(srcs) + list(lands)], after)
    return outs[0], outs[1], outs[2:2 + n], outs[2 + n:2 + 2 * n], outs[-1]


def _gather_forward(send1, recv1, srcs, lands, after, name):
    n = len(srcs)

    def body(*refs):
        src_refs, land_refs = refs[:n], refs[n:2 * n]
        s1, r1 = refs[2 * n], refs[2 * n + 1]
        s2, r2 = refs[2 * n + 3], refs[2 * n + 4]
        token = refs[-1]
        for j in range(3):
            for i in range(n):
                _gather_first(src_refs, land_refs, s1, r1, i, 1 + j).wait_recv()
                _gather_second(land_refs, s2, r2, i, j).start()
        for i in range(n):
            _gather_first(src_refs, land_refs, s1, r1, i, 0).wait_recv()
            for k in range(4):
                _gather_first(src_refs, land_refs, s1, r1, i, k).wait_send()
        token[...] = jnp.zeros_like(token)

    outs = pl.pallas_call(
        body, name=name,
        out_shape=(pltpu.SemaphoreType.DMA((3 * n,)), pltpu.SemaphoreType.DMA((3 * n,)),
                   *[pltpu.HBM(a.shape, a.dtype) for a in srcs], *[pltpu.HBM(a.shape, a.dtype) for a in lands],
                   jax.ShapeDtypeStruct((8, 128), F32)),
        in_specs=[*([_HBM] * (2 * n)), _SEM, _SEM, pl.BlockSpec(memory_space=pl.ANY)],
        out_specs=(_SEM, _SEM, *([_HBM] * (2 * n)), pl.BlockSpec(memory_space=pltpu.VMEM)),
        input_output_aliases={i: 2 + i for i in range(2 * n)},
        compiler_params=pltpu.CompilerParams(has_side_effects=_EFFECT),
    )(*srcs, *lands, send1, recv1, after)
    return outs[0], outs[1], outs[2:2 + n], outs[2 + n:2 + 2 * n], outs[-1]


def _gather_wait(send2, recv2, srcs, lands, after, name):
    n = len(srcs)

    def body(*refs):
        land_refs = refs[n:2 * n]
        s2, r2 = refs[2 * n], refs[2 * n + 1]
        for i in range(n):
            for j in range(3):
                cp = _gather_second(land_refs, s2, r2, i, j)
                cp.wait_send()
                cp.wait_recv()

    outs = pl.pallas_call(
        body, name=name,
        out_shape=(*[pltpu.HBM(a.shape, a.dtype) for a in srcs], *[pltpu.HBM(a.shape, a.dtype) for a in lands]),
        in_specs=[*([_HBM] * (2 * n)), _SEM, _SEM, pl.BlockSpec(memory_space=pl.ANY)],
        out_specs=tuple([_HBM] * (2 * n)),
        input_output_aliases={i: i for i in range(2 * n)},
        compiler_params=pltpu.CompilerParams(has_side_effects=_EFFECT),
    )(*srcs, *lands, send2, recv2, after)
    return outs[:n], outs[n:]


def _row(a, l):
    return a[l:l + 1]


def _tie(a, token):
    return a if token is None else a + token[0, 0]


def _layer_fwd(h, mem, W, l, tab, after, at):
    s = {}
    n = f"l{l}_"
    s["h0"] = h
    P = _matmul(h, W["w_in"][l], "nt", n + "proj", after=after, out_dtype=BF16)
    s["P"] = P
    sink_col = jnp.repeat(W["sinks"][l].reshape(N_KV_HEADS, GROUP), BLOCK, axis=1)[:, :, None]
    s["sink_col"] = sink_col
    y_rnn, hs = _rnn_fwd(P, W["conv_w"][l], _row(W["conv_b"], l), W["w_rg"][l], _row(W["b_rg"], l),
                         W["w_ig"][l], _row(W["b_ig"], l), _row(W["lru_lambda"], l), n + "rnn_fwd")
    y_attn = _swa_fwd(P, tab, _tie(sink_col, at(l, "proj", P)), n + "swa_fwd")
    at(l, "attn", y_attn)
    mr = _matmul(y_rnn, W["w_br_rnn"][l], "nn", n + "br_rnn")
    ma = _matmul(y_attn, W["w_br_attn"][l], "nn", n + "br_attn")
    merged = _merge_fwd(P, mr, ma, n + "merge_fwd")
    h1, xh1, rs1 = _matmul_ln(merged, W["w_out"][l], h, _tie(_row(W["ln1_g"], l), at(l, "mix", merged)),
                              _row(W["ln1_b"], l), n + "w_out_ln1")
    at(l, "ln1", h1)
    s.update(hs=hs, y_rnn=y_rnn, y_attn=y_attn, mr=mr, ma=ma, merged=merged, xh1=xh1, rs1=rs1, h1=h1)

    qc = _matmul(h1, W["cq_w"][l], "nn", n + "cq", out_dtype=BF16)
    kv = _matmul(mem, W["ckv_w"][l], "nt", n + "ckv", out_dtype=BF16)
    oc = _cross_fwd(qc, kv, n + "cross_fwd")
    h2, xh2, rs2 = _matmul_ln(oc, W["co_w"][l], h1, _row(W["ln2_g"], l), _row(W["ln2_b"], l), n + "co_ln2")
    s.update(qc=qc, kv=kv, oc=oc, xh2=xh2, rs2=rs2, h2=h2)

    U = _matmul(h2, W["ffn_wi"][l], "nt", n + "ffn_wi", after=at(l, "ln2", h2), out_dtype=BF16)
    act = _swiglu_fwd(U, n + "swiglu_fwd")
    h3, xh3, rs3 = _matmul_ln(act, W["ffn_wo"][l], h2, _tie(_row(W["ln3_g"], l), at(l, "ffn", act)),
                              _row(W["ln3_b"], l), n + "ffn_wo_ln3")
    s.update(U=U, act=act, xh3=xh3, rs3=rs3)
    return h3, s


GRAD_PARTS = (("ffn_wi", "ffn_wo", "cq_w", "ckv_w", "co_w"), ("w_out", "w_br_rnn", "w_br_attn"),
              ("w_in", "w_rg", "w_ig"))


def _layer_bwd(dh3, mem, W, l, tab, s, send):
    n = f"l{l}_"
    g = {}
    dz3, g["ln3_g"], g["ln3_b"] = _ln_bwd(dh3, s["xh3"], s["rs3"], _row(W["ln3_g"], l), n + "ln3_bwd")
    g["ffn_wo"] = _matmul(s["act"], dz3, "tn", n + "d_ffn_wo", out_dtype=BF16)
    dact = _matmul(dz3, W["ffn_wo"][l], "nt", n + "d_act", out_dtype=BF16)
    dU = _swiglu_bwd(s["U"], dact, n + "swiglu_bwd")
    g["ffn_wi"] = _matmul(dU, s["h2"], "tn", n + "d_ffn_wi", out_dtype=BF16)
    dh2 = _matmul(dU, W["ffn_wi"][l], "nn", n + "d_h2", add=dz3, add_scale=ALPHA)
    dz2, g["ln2_g"], g["ln2_b"] = _ln_bwd(dh2, s["xh2"], s["rs2"], _row(W["ln2_g"], l), n + "ln2_bwd")
    g["co_w"] = _matmul(s["oc"], dz2, "tn", n + "d_co", out_dtype=BF16)
    doc = _matmul(dz2, W["co_w"][l], "nt", n + "d_oc", out_dtype=BF16)
    dqc, dkv = _cross_bwd(s["qc"], s["kv"], doc, n + "cross_bwd")
    g["ckv_w"] = _matmul(dkv, mem, "tn", n + "d_ckv", out_dtype=BF16)
    g["cq_w"] = _matmul(s["h1"], dqc, "tn", n + "d_cq", out_dtype=BF16)
    after = send(l, 0, g)
    dh1 = _matmul(dqc, W["cq_w"][l], "nt", n + "d_h1", add=dz2, add_scale=ALPHA, after=after)
    dz1, g["ln1_g"], g["ln1_b"] = _ln_bwd(dh1, s["xh1"], s["rs1"], _row(W["ln1_g"], l), n + "ln1_bwd")
    g["w_out"] = _matmul(s["merged"], dz1, "tn", n + "d_w_out", out_dtype=BF16)
    dmerged = _matmul(dz1, W["w_out"][l], "nt", n + "d_merged")
    dmr, dma, dgrnn, dgattn = _merge_bwd(s["P"], s["mr"], s["ma"], dmerged, n + "merge_bwd")
    g["w_br_rnn"] = _matmul(s["y_rnn"], dmr, "tn", n + "d_br_rnn", out_dtype=BF16)
    g["w_br_attn"] = _matmul(s["y_attn"], dma, "tn", n + "d_br_attn", out_dtype=BF16)
    after = send(l, 1, g)
    dy_rnn = _matmul(dmr, W["w_br_rnn"][l], "nt", n + "d_y_rnn", after=after)
    dy_attn = _matmul(dma, W["w_br_attn"][l], "nt", n + "d_y_attn", out_dtype=BF16)
    dxr, dgr, g["conv_w"], g["conv_b"], g["w_rg"], g["b_rg"], g["w_ig"], g["b_ig"], g["lru_lambda"] = _rnn_bwd(
        s["P"], s["hs"], dy_rnn, W["conv_w"][l], _row(W["conv_b"], l), W["w_rg"][l], _row(W["b_rg"], l),
        W["w_ig"][l], _row(W["b_ig"], l), _row(W["lru_lambda"], l), n + "rnn_bwd")
    dq, dk, dv, dsk = _swa_bwd(s["P"], dy_attn, tab, s["sink_col"], n + "swa_bwd")
    g["sinks"] = dsk[:, :, 0].reshape(1, N_Q_HEADS)
    dP = jnp.concatenate([dxr, dgr, dq, dk.astype(BF16), dv.astype(BF16), dgrnn, dgattn], axis=1)
    g["w_in"] = _matmul(dP, s["h0"], "tn", n + "d_w_in", out_dtype=BF16)
    after = send(l, 2, g)
    dh = _matmul(dP, W["w_in"][l], "nn", n + "d_h0", add=dz1, add_scale=ALPHA, after=after)
    return dh, g


def _local_step(x, mem, target, W, at, send):
    T = x.shape[0]
    tab = _rope_table(T)
    h = x
    saved = []
    for l in range(DEPTH):
        after = at(l, "start", h)
        h, s = _layer_fwd(h, mem, W, l, tab, after, at)
        saved.append(s)
    lblk, dh = _loss_head(h, target, "loss_head")
    grads = [None] * DEPTH
    for l in reversed(range(DEPTH)):
        dh, grads[l] = _layer_bwd(dh, mem, W, l, tab, saved[l], send)
    return lblk[0, 0], dh, grads


COL_SHARDED = ("w_in", "ckv_w", "ffn_wi")
GATE_MATS = ("w_rg", "w_ig")


def _shard_rows(shards, l):
    out = []
    for n, r in PACK_ROWS:
        a = shards[n][l].astype(BF16)
        if n in COL_SHARDED:
            a = a.T
        elif n in GATE_MATS:
            a = a.reshape(RNN_BLOCKS * RNN_BLOCK // N_DEV, RNN_BLOCK)
        out.append(a)
    return out


def _full_weight(G, name):
    if name in GATE_MATS:
        return jnp.transpose(G.reshape(N_DEV, RNN_BLOCKS, RNN_BLOCK // N_DEV, RNN_BLOCK), (1, 0, 2, 3)).reshape(
            RNN_BLOCKS, RNN_BLOCK, RNN_BLOCK)
    return G.reshape(N_DEV * G.shape[1], G.shape[2])


SHARD_ROWS = dict(PACK_ROWS)


def _pack_blocks(g, names):
    parts = []
    for name in names:
        a = g[name]
        if name in GATE_MATS:
            a = jnp.transpose(a.astype(BF16).reshape(RNN_BLOCKS, N_DEV, RNN_BLOCK // N_DEV, RNN_BLOCK), (1, 0, 2, 3))
        parts.append(a.reshape(N_DEV, SHARD_ROWS[name], D_MODEL))
    return jnp.concatenate(parts, axis=1)


def _pack_small(g):
    rows = [g["conv_w"]]
    for nme in SMALL_NAMES:
        a = g[nme]
        if nme == "sinks":
            a = jnp.pad(a, ((0, 0), (0, D_MODEL - N_Q_HEADS)))
        rows.append(a)
    rows.append(jnp.zeros((SMALL_ROWS - CONV_WIDTH - len(SMALL_NAMES), D_MODEL), F32))
    return jnp.concatenate(rows, axis=0)


_SHARD_SHAPES = {"w_in": (1024, 672), "w_br_rnn": (128, 1024), "w_br_attn": (128, 1024), "w_out": (128, 1024),
                 "cq_w": (128, 1024), "ckv_w": (1024, 256), "co_w": (128, 1024), "ffn_wi": (1024, 704),
                 "ffn_wo": (352, 1024), "w_rg": (4, 32, 256), "w_ig": (4, 32, 256)}

LAYER0_GROUPS = (("w_in", "w_rg", "w_ig"), ("w_br_rnn", "w_br_attn", "w_out"),
                 ("cq_w", "ckv_w", "co_w", "ffn_wi", "ffn_wo"))
LAYER0_FORWARD_AT = {"proj": 1, "mix": 2}
LAYER0_WAIT_AT = {"attn": 1, "ln1": 2}
NEXT_LAYER_FORWARD_AT = ("ffn", "ln2", "ln2", None)

WEIGHT_NAMES = ("w_in", "conv_w", "conv_b", "w_rg", "b_rg", "w_ig", "b_ig", "lru_lambda", "w_br_rnn", "w_br_attn",
                "sinks", "w_out", "ln1_g", "ln1_b", "cq_w", "ckv_w", "co_w", "ln2_g", "ln2_b", "ffn_wi", "ffn_wo",
                "ln3_g", "ln3_b")


def kernel(x, mem, w_in, conv_w, conv_b, w_rg, b_rg, w_ig, b_ig, lru_lambda, w_br_rnn, w_br_attn, sinks, w_out, ln1_g, ln1_b, cq_w, ckv_w, co_w, ln2_g, ln2_b, ffn_wi, ffn_wo, ln3_g, ln3_b, loss_target, m_w_in, m_conv_w, m_conv_b, m_w_rg, m_b_rg, m_w_ig, m_b_ig, m_lru_lambda, m_w_br_rnn, m_w_br_attn, m_sinks, m_w_out, m_ln1_g, m_ln1_b, m_cq_w, m_ckv_w, m_co_w, m_ln2_g, m_ln2_b, m_ffn_wi, m_ffn_wo, m_ln3_g, m_ln3_b, v_w_in, v_conv_w, v_conv_b, v_w_rg, v_b_rg, v_w_ig, v_b_ig, v_lru_lambda, v_w_br_rnn, v_w_br_attn, v_sinks, v_w_out, v_ln1_g, v_ln1_b, v_cq_w, v_ckv_w, v_co_w, v_ln2_g, v_ln2_b, v_ffn_wi, v_ffn_wo, v_ln3_g, v_ln3_b):
    w = dict(w_in=w_in, conv_w=conv_w, conv_b=conv_b, w_rg=w_rg, b_rg=b_rg, w_ig=w_ig, b_ig=b_ig,
             lru_lambda=lru_lambda, w_br_rnn=w_br_rnn, w_br_attn=w_br_attn, sinks=sinks, w_out=w_out, ln1_g=ln1_g,
             ln1_b=ln1_b, cq_w=cq_w, ckv_w=ckv_w, co_w=co_w, ln2_g=ln2_g, ln2_b=ln2_b, ffn_wi=ffn_wi, ffn_wo=ffn_wo,
             ln3_g=ln3_g, ln3_b=ln3_b)
    m = dict(w_in=m_w_in, conv_w=m_conv_w, conv_b=m_conv_b, w_rg=m_w_rg, b_rg=m_b_rg, w_ig=m_w_ig, b_ig=m_b_ig,
             lru_lambda=m_lru_lambda, w_br_rnn=m_w_br_rnn, w_br_attn=m_w_br_attn, sinks=m_sinks, w_out=m_w_out,
             ln1_g=m_ln1_g, ln1_b=m_ln1_b, cq_w=m_cq_w, ckv_w=m_ckv_w, co_w=m_co_w, ln2_g=m_ln2_g, ln2_b=m_ln2_b,
             ffn_wi=m_ffn_wi, ffn_wo=m_ffn_wo, ln3_g=m_ln3_g, ln3_b=m_ln3_b)
    v = dict(w_in=v_w_in, conv_w=v_conv_w, conv_b=v_conv_b, w_rg=v_w_rg, b_rg=v_b_rg, w_ig=v_w_ig, b_ig=v_b_ig,
             lru_lambda=v_lru_lambda, w_br_rnn=v_w_br_rnn, w_br_attn=v_w_br_attn, sinks=v_sinks, w_out=v_w_out,
             ln1_g=v_ln1_g, ln1_b=v_ln1_b, cq_w=v_cq_w, ckv_w=v_ckv_w, co_w=v_co_w, ln2_g=v_ln2_g, ln2_b=v_ln2_b,
             ffn_wi=v_ffn_wi, ffn_wo=v_ffn_wo, ln3_g=v_ln3_g, ln3_b=v_ln3_b)
    my_dev = 4 * lax.axis_index("x") + 2 * lax.axis_index("y") + lax.axis_index("c")

    W = {n: [None] * DEPTH for n, _ in PACK_ROWS}
    shards0 = dict(zip([n for n, _ in PACK_ROWS], _shard_rows(w, 0)))
    gathered = _all_gather_many([shards0[n] for n in LAYER0_GROUPS[0]], "l0_gather_first")
    for n, G in zip(LAYER0_GROUPS[0], gathered):
        W[n][0] = _full_weight(G, n)
    conv_all = _all_gather(conv_w.reshape(DEPTH * CONV_WIDTH, D_MODEL // N_DEV), "gather_conv")
    W["conv_w"] = jnp.transpose(conv_all, (1, 0, 2)).reshape(DEPTH, CONV_WIDTH, D_MODEL)
    for n in SMALL_NAMES:
        W[n] = w[n]
    flying = {}
    token = conv_all
    groups = [((0, gi), LAYER0_GROUPS[gi], [shards0[n] for n in LAYER0_GROUPS[gi]]) for gi in (1, 2)]
    groups += [((l, 0), [n for n, _ in PACK_ROWS], _shard_rows(w, l)) for l in range(1, DEPTH)]
    for key, names, srcs in groups:
        lands = [lax.empty((N_DEV,) + a.shape, a.dtype) for a in srcs]
        flying[key] = (names,) + _gather_start(srcs, lands, f"l{key[0]}_gather_start{key[1]}", token)
        token = flying[key][5]
    first_token = token

    def forward(key, after):
        names, send1, recv1, srcs, lands, _ = flying[key]
        flying[key] = (names,) + _gather_forward(send1, recv1, srcs, lands, after, f"l{key[0]}_gather_forward{key[1]}")
        return flying[key][5]

    def arrive(key, after):
        names, send2, recv2, srcs, lands, _ = flying.pop(key)
        srcs, lands = _gather_wait(send2, recv2, srcs, lands, after, f"l{key[0]}_gather_wait{key[1]}")
        for n, mine, G in zip(names, srcs, lands):
            W[n][key[0]] = _full_weight(lax.dynamic_update_index_in_dim(G, mine, my_dev, 0), n)

    def at(l, point, x):
        if point == "start":
            if l == 0:
                return first_token
            arrive((l, 0), x)
        elif l == 0 and point in LAYER0_FORWARD_AT:
            return forward((0, LAYER0_FORWARD_AT[point]), x)
        elif l == 0 and point in LAYER0_WAIT_AT:
            arrive((0, LAYER0_WAIT_AT[point]), x)
        elif point == NEXT_LAYER_FORWARD_AT[l]:
            return forward((l + 1, 0), x)
        return None

    summed = {}
    sent = {}

    def finish(l, part, after):
        ssem, rsem, blocks, land, mine = sent.pop((l, part))
        _, (recv,) = _split_wait(ssem, rsem, [blocks], [land], after, f"l{l}_exchange_wait{part}")
        summed[(l, part)] = _sum_slots(recv, f"l{l}_sum_grads{part}", mine=mine)

    def send(l, part, g):
        blocks = _pack_blocks(g, GRAD_PARTS[part])
        mine = lax.dynamic_index_in_dim(blocks, my_dev, 0, keepdims=False)
        if (l + 1, part) in sent:
            finish(l + 1, part, blocks)
        ssem, rsem, (blocks,), (land,), token = _split_start([blocks], [lax.empty(blocks.shape, blocks.dtype)],
                                                             f"l{l}_exchange_start{part}")
        sent[(l, part)] = (ssem, rsem, blocks, land, mine)
        return token

    loss_local, dx, layer_grads = _local_step(x[0], mem[0], loss_target[0], W, at, send)
    for part in range(len(GRAD_PARTS)):
        finish(0, part, dx)
    loss = lax.psum(loss_local, MESH_AXES)

    small_all = _all_gather(jnp.concatenate([_pack_small(g) for g in layer_grads], axis=0), "gather_small_grads")
    small_sum = _sum_slots(small_all, "sum_small_grads").reshape(DEPTH, SMALL_ROWS, D_MODEL)

    grads = {}
    for part, names in enumerate(GRAD_PARTS):
        G = jnp.stack([summed[(l, part)] for l in range(DEPTH)])
        o = 0
        for n in names:
            r = SHARD_ROWS[n]
            blk = G[:, o:o + r, :]
            grads[n] = jnp.transpose(blk, (0, 2, 1)) if n in COL_SHARDED else blk.reshape((DEPTH,) + _SHARD_SHAPES[n])
            o += r
    conv_full = small_sum[:, :CONV_WIDTH, :]
    grads["conv_w"] = lax.dynamic_slice_in_dim(conv_full, my_dev * (D_MODEL // N_DEV), D_MODEL // N_DEV, axis=2)
    for i, n in enumerate(SMALL_NAMES):
        row = small_sum[:, CONV_WIDTH + i, :]
        grads[n] = row[:, :N_Q_HEADS] if n == "sinks" else row

    deltas, new_m, new_v = {}, {}, {}
    for n in WEIGHT_NAMES:
        deltas[n], new_m[n], new_v[n] = _adamw(w[n], grads[n], m[n], v[n], "adamw_" + n)

    return (loss, dx[None], *[grads[n] for n in WEIGHT_NAMES], *[deltas[n] for n in WEIGHT_NAMES],
            *[new_m[n] for n in WEIGHT_NAMES], *[new_v[n] for n in WEIGHT_NAMES])
```

```python
import functools
import math

import jax
import jax.numpy as jnp
from jax import lax
from jax.experimental import pallas as pl
from jax.experimental.pallas import tpu as pltpu

F32 = jnp.float32
BF16 = jnp.bfloat16

D_MODEL = 1024
DEPTH = 4
N_DEV = 8
RNN_BLOCKS = 4
RNN_BLOCK = 256
CONV_WIDTH = 4
LRU_C = 8.0
HEAD_DIM = 64
N_Q_HEADS = 16
N_KV_HEADS = 2
GROUP = 8
KV_WIDTH = 128
BLOCK = 128
ROPE_THETA = 500000.0
ROT_DIM = 16
IN_COLS = 5376
CROSS_HEADS = 4
CROSS_HEAD_DIM = 256
D_FF = 2816
LN_EPS = 1e-5
ALPHA = (2 * DEPTH) ** 0.25
NEG_INF = -1e30

ADAM_LR = 0.001
ADAM_B1 = 0.9
ADAM_B2 = 0.999
ADAM_EPS = 1e-08
ADAM_WD = 0.01
ADAM_STEP = 10

C_XR, C_GR, C_Q, C_K, C_V, C_GRNN, C_GATTN = 0, 1024, 2048, 3072, 3200, 3328, 4352

TIME_CHUNK = 256
ROW_TILE = 256

MESH_AXES = ("x", "y", "c")

PACK_ROWS = (("w_in", 672), ("w_br_rnn", 128), ("w_br_attn", 128), ("w_out", 128), ("cq_w", 128),
             ("ckv_w", 256), ("co_w", 128), ("ffn_wi", 704), ("ffn_wo", 352), ("w_rg", 32), ("w_ig", 32))
SMALL_NAMES = ("conv_b", "b_rg", "b_ig", "lru_lambda", "sinks", "ln1_g", "ln1_b", "ln2_g", "ln2_b", "ln3_g", "ln3_b")
SMALL_ROWS = 16


def _pick(dim, cands):
    for c in cands:
        if dim % c == 0:
            return c
    return dim


_DIMS = {"nn": (((1,), (0,)), ((), ())), "nt": (((1,), (1,)), ((), ())), "tn": (((0,), (0,)), ((), ()))}

MATMUL_VMEM_BUDGET = 44 * 2 ** 20
MATMUL_MAX_TILE = 2048
MXU_DIM = 256
STEP_COST_BYTES = 500_000
MIN_ROW_TILE = 512


def _tile_candidates(dim, whole=False, step=MXU_DIM):
    c = [d for d in range(step, min(dim, MATMUL_MAX_TILE) + 1, step) if dim % d == 0]
    if whole and dim not in c:
        c.append(dim)
    return c or [dim]


def _matmul_tiles(M, N, K, sa, sb, so, has_add):
    best = None
    for tk in _tile_candidates(K, whole=True):
        nk = K // tk
        for tm in [t for t in _tile_candidates(M, step=128) if t >= min(M, MIN_ROW_TILE)]:
            for tn in _tile_candidates(N):
                vmem = 2 * (tm * tk * sa + tk * tn * sb + tm * tn * so) + tm * tn * 4
                vmem += tm * tn * 4 if nk > 1 else 0
                vmem += 2 * tm * tn * 4 if has_add else 0
                vmem += (tm * tk * 2 if sa == 4 else 0) + (tk * tn * 2 if sb == 4 else 0)
                if vmem > MATMUL_VMEM_BUDGET:
                    continue
                steps = (M // tm) * (N // tn) * nk
                exposed = tm * tk * sa + tk * tn * sb + tm * tn * so
                acc_moves = steps * tm * tn * 2 if nk > 1 else 0
                fixed = M * N * so + steps * STEP_COST_BYTES + exposed + acc_moves
                a_in = M * K * sa * ((N // tn) if nk > 1 else 1) + K * N * sb * (M // tm)
                b_in = M * K * sa * (N // tn) + K * N * sb * ((M // tm) if nk > 1 else 1)
                for cost, m_outer in ((a_in + fixed, True), (b_in + fixed, False)):
                    if best is None or cost < best[0]:
                        best = (cost, tm, tn, tk, m_outer)
    return best[1:]


def _matmul(a, b, mode, name, add=None, add_scale=1.0, out_dtype=F32, after=None):
    if mode == "nn":
        (M, K), (_, N) = a.shape, b.shape
    elif mode == "nt":
        (M, K), (N, _) = a.shape, b.shape
    else:
        (K, M), (_, N) = a.shape, b.shape
    tm, tn, tk, m_outer = _matmul_tiles(M, N, K, a.dtype.itemsize, b.dtype.itemsize, jnp.dtype(out_dtype).itemsize,
                                        add is not None)
    nk = K // tk
    dims = _DIMS[mode]

    def body(*refs):
        if after is not None:
            refs = refs[:-2 - (nk > 1)] + refs[-1 - (nk > 1):]
        a_ref, b_ref = refs[0], refs[1]
        c_ref = refs[2] if add is not None else None
        o_ref = refs[3] if add is not None else refs[2]

        def finish(r):
            if add is not None:
                r = r + add_scale * c_ref[...]
            o_ref[...] = r.astype(out_dtype)

        prod = lax.dot_general(a_ref[...].astype(BF16), b_ref[...].astype(BF16), dims, preferred_element_type=F32)
        if nk == 1:
            finish(prod)
            return
        acc_ref = refs[-1]
        k = pl.program_id(2)

        @pl.when(k == 0)
        def _():
            acc_ref[...] = prod

        @pl.when(k > 0)
        def _():
            acc_ref[...] += prod

        @pl.when(k == nk - 1)
        def _():
            finish(acc_ref[...])

    ij = (lambda p, q: (p, q)) if m_outer else (lambda p, q: (q, p))
    if mode == "nn":
        a_spec = pl.BlockSpec((tm, tk), lambda p, q, k: (ij(p, q)[0], k))
        b_spec = pl.BlockSpec((tk, tn), lambda p, q, k: (k, ij(p, q)[1]))
    elif mode == "nt":
        a_spec = pl.BlockSpec((tm, tk), lambda p, q, k: (ij(p, q)[0], k))
        b_spec = pl.BlockSpec((tn, tk), lambda p, q, k: (ij(p, q)[1], k))
    else:
        a_spec = pl.BlockSpec((tk, tm), lambda p, q, k: (k, ij(p, q)[0]))
        b_spec = pl.BlockSpec((tk, tn), lambda p, q, k: (k, ij(p, q)[1]))
    o_spec = pl.BlockSpec((tm, tn), lambda p, q, k: ij(p, q))
    o_shape = jax.ShapeDtypeStruct((M, N), out_dtype)
    in_specs = [a_spec, b_spec]
    args = [a, b]
    if add is not None:
        in_specs.append(o_spec)
        args.append(add)
    if after is not None:
        in_specs.append(pl.BlockSpec(memory_space=pl.ANY))
        args.append(after)
    return pl.pallas_call(
        body, name=name, grid=(M // tm, N // tn, nk) if m_outer else (N // tn, M // tm, nk),
        in_specs=in_specs, out_specs=o_spec, out_shape=o_shape,
        scratch_shapes=[pltpu.VMEM((tm, tn), F32)] if nk > 1 else [],
        compiler_params=pltpu.CompilerParams(dimension_semantics=("parallel", "parallel", "arbitrary")),
    )(*args)


LN_ROWS = 512


def _matmul_ln(a, w, h, g, b, name):
    T, K = a.shape
    D = w.shape[1]
    tr = _pick(T, (LN_ROWS, 256, 128, 64, 32, 16, 8))

    def body(a_ref, w_ref, h_ref, g_ref, b_ref, o_ref, xh_ref, rs_ref):
        f = jnp.dot(a_ref[...].astype(BF16), w_ref[...], preferred_element_type=F32)
        z = ALPHA * h_ref[...] + f
        mu = jnp.mean(z, axis=-1, keepdims=True)
        zc = z - mu
        var = jnp.mean(zc * zc, axis=-1, keepdims=True)
        rstd = lax.rsqrt(var + LN_EPS)
        xh = zc * rstd
        xh_ref[...] = xh
        rs_ref[...] = rstd
        o_ref[...] = xh * g_ref[...] + b_ref[...]

    row = pl.BlockSpec((tr, D), lambda i: (i, 0))
    vec = pl.BlockSpec((1, D), lambda i: (0, 0))
    return pl.pallas_call(
        body, name=name, grid=(T // tr,),
        in_specs=[pl.BlockSpec((tr, K), lambda i: (i, 0)), pl.BlockSpec((K, D), lambda i: (0, 0)), row, vec, vec],
        out_specs=[row, row, pl.BlockSpec((tr, 1), lambda i: (i, 0))],
        out_shape=[jax.ShapeDtypeStruct((T, D), F32), jax.ShapeDtypeStruct((T, D), F32),
                   jax.ShapeDtypeStruct((T, 1), F32)],
        compiler_params=pltpu.CompilerParams(dimension_semantics=("parallel",)),
    )(a, w, h, g, b)


def _ln_bwd(dout, xh, rstd, g, name, after=None):
    T, D = dout.shape
    tr = _pick(T, (LN_ROWS, 256, 128, 64, 32, 16, 8))

    def body(do_ref, xh_ref, rs_ref, g_ref, *rest):
        dz_ref, dg_ref, db_ref = rest[-3:]

        @pl.when(pl.program_id(0) == 0)
        def _():
            dg_ref[...] = jnp.zeros_like(dg_ref)
            db_ref[...] = jnp.zeros_like(db_ref)

        do = do_ref[...]
        xh = xh_ref[...]
        dxh = do * g_ref[...]
        m1 = jnp.mean(dxh, axis=-1, keepdims=True)
        m2 = jnp.mean(dxh * xh, axis=-1, keepdims=True)
        dz_ref[...] = rs_ref[...] * (dxh - m1 - xh * m2)
        dg_ref[...] += jnp.sum(do * xh, axis=0, keepdims=True)
        db_ref[...] += jnp.sum(do, axis=0, keepdims=True)

    row = pl.BlockSpec((tr, D), lambda i: (i, 0))
    vec = pl.BlockSpec((1, D), lambda i: (0, 0))
    in_specs = [row, row, pl.BlockSpec((tr, 1), lambda i: (i, 0)), vec]
    args = [dout, xh, rstd, g]
    if after is not None:
        in_specs.append(pl.BlockSpec(memory_space=pl.ANY))
        args.append(after)
    return pl.pallas_call(
        body, name=name, grid=(T // tr,),
        in_specs=in_specs, out_specs=[row, vec, vec],
        out_shape=[jax.ShapeDtypeStruct((T, D), F32), jax.ShapeDtypeStruct((1, D), F32),
                   jax.ShapeDtypeStruct((1, D), F32)],
        compiler_params=pltpu.CompilerParams(dimension_semantics=("arbitrary",)),
    )(*args)


_GELU_C = math.sqrt(2.0 / math.pi)


def _gelu(x):
    t = jnp.tanh(_GELU_C * (x + 0.044715 * x * x * x))
    return 0.5 * x * (1.0 + t), t


def _gelu_grad(x, t):
    return 0.5 * (1.0 + t) + 0.5 * x * (1.0 - t * t) * _GELU_C * (1.0 + 3 * 0.044715 * x * x)


def _sigmoid(x):
    return 1.0 / (1.0 + jnp.exp(-x))


def _softplus_neg(lam):
    z = jnp.exp(-jnp.abs(lam))
    u = 1.0 + z
    l1p = jnp.where(u == 1.0, z, jnp.log(u) * z / jnp.where(u == 1.0, 1.0, u - 1.0))
    return jnp.maximum(-lam, 0.0) + l1p


def _neg_expm1(x):
    series = x * (1.0 + x * 0.5 * (1.0 + x * (1.0 / 3.0) * (1.0 + x * 0.25 * (1.0 + x * 0.2))))
    return -jnp.where(x > -0.05, series, jnp.exp(x) - 1.0)


def _scan_fwd(a, b):
    n = a.shape[0]
    rows = lax.broadcasted_iota(jnp.int32, a.shape, 0)
    s = 1
    while s < n:
        keep = rows >= s
        b = jnp.where(keep, a * pltpu.roll(b, s, 0) + b, b)
        a = jnp.where(keep, a * pltpu.roll(a, s, 0), a)
        s *= 2
    return a, b


def _scan_bwd(c, b):
    n = c.shape[0]
    rows = lax.broadcasted_iota(jnp.int32, c.shape, 0)
    s = 1
    while s < n:
        keep = rows < n - s
        b = jnp.where(keep, c * pltpu.roll(b, n - s, 0) + b, b)
        c = jnp.where(keep, c * pltpu.roll(c, n - s, 0), c)
        s *= 2
    return c, b


def _rnn_gates(xc, wr, br, wi, bi, sp):
    xb = xc.astype(BF16)
    r = _sigmoid(jnp.dot(xb, wr, preferred_element_type=F32) + br)
    i = _sigmoid(jnp.dot(xb, wi, preferred_element_type=F32) + bi)
    la = -LRU_C * r * sp
    a = jnp.exp(la)
    om = _neg_expm1(2.0 * la)
    mult = jnp.sqrt(om)
    return r, i, a, om, mult


def _rnn_specs(T):
    C = RNN_BLOCK
    col = lambda off: pl.BlockSpec((T, C), lambda n, off=off: (0, off // C + n))
    vec = pl.BlockSpec((1, C), lambda n: (0, n))
    cw = pl.BlockSpec((CONV_WIDTH, C), lambda n: (0, n))
    w = pl.BlockSpec((1, C, C), lambda n: (n, 0, 0))
    own = pl.BlockSpec((T, C), lambda n: (0, n))
    return col, vec, cw, w, own


def _rnn_fwd(P, cw, cb, wrg, brg, wig, big, lam, name):
    T = P.shape[0]
    C = RNN_BLOCK
    tc = _pick(T, (TIME_CHUNK,))
    nch = T // tc

    def body(x_ref, g_ref, cw_ref, cb_ref, wr_ref, br_ref, wi_ref, bi_ref, lam_ref, y_ref, hs_ref, xs_ref):
        sp = _softplus_neg(lam_ref[...])
        wr = wr_ref[0]
        wi = wi_ref[0]
        xs_ref[0:8, :] = jnp.zeros((8, C), F32)

        def chunk(c, hprev):
            r0 = pl.multiple_of(c * tc, tc)
            x = x_ref[pl.ds(r0, tc), :].astype(F32)
            xs_ref[8:, :] = x
            xc = cb_ref[...] + jnp.zeros((tc, C), F32)
            for k in range(CONV_WIDTH):
                xc = xc + xs_ref[pl.ds(8 - (CONV_WIDTH - 1 - k), tc), :] * cw_ref[k:k + 1, :]
            xs_ref[0:8, :] = x[tc - 8:, :]
            r, i, a, om, mult = _rnn_gates(xc, wr, br_ref[...], wi, bi_ref[...], sp)
            acum, bcum = _scan_fwd(a, mult * (i * xc))
            h = acum * hprev + bcum
            hs_ref[pl.ds(r0, tc), :] = h
            ge, _ = _gelu(g_ref[pl.ds(r0, tc), :].astype(F32))
            y_ref[pl.ds(r0, tc), :] = (h * ge).astype(BF16)
            return h[tc - 1:tc, :]

        lax.fori_loop(0, nch, chunk, jnp.zeros((1, C), F32))

    col, vec, cwspec, w, own = _rnn_specs(T)
    return pl.pallas_call(
        body, name=name, grid=(RNN_BLOCKS,),
        in_specs=[col(C_XR), col(C_GR), cwspec, vec, w, vec, w, vec, vec],
        out_specs=[own, own],
        out_shape=[jax.ShapeDtypeStruct((T, D_MODEL), BF16), jax.ShapeDtypeStruct((T, D_MODEL), F32)],
        scratch_shapes=[pltpu.VMEM((tc + 8, C), F32)],
        compiler_params=pltpu.CompilerParams(dimension_semantics=("parallel",)),
    )(P, P, cw, cb, wrg, brg, wig, big, lam)


def _rnn_bwd(P, hs, dy, cw, cb, wrg, brg, wig, big, lam, name):
    T = P.shape[0]
    C = RNN_BLOCK
    tc = _pick(T, (TIME_CHUNK,))
    nch = T // tc

    def body(x_ref, g_ref, hs_ref, dy_ref, cw_ref, cb_ref, wr_ref, br_ref, wi_ref, bi_ref, lam_ref,
             dx_ref, dg_ref, dcw_ref, dcb_ref, dwr_ref, dbr_ref, dwi_ref, dbi_ref, dlam_ref,
             xs_ref, hp_ref, an_ref, dn_ref):
        lam_v = lam_ref[...]
        sp = _softplus_neg(lam_v)
        wr = wr_ref[0]
        wi = wi_ref[0]
        dcw_ref[...] = jnp.zeros_like(dcw_ref)
        dcb_ref[...] = jnp.zeros_like(dcb_ref)
        dwr_ref[...] = jnp.zeros_like(dwr_ref)
        dbr_ref[...] = jnp.zeros_like(dbr_ref)
        dwi_ref[...] = jnp.zeros_like(dwi_ref)
        dbi_ref[...] = jnp.zeros_like(dbi_ref)
        dlam_ref[...] = jnp.zeros_like(dlam_ref)
        an_ref[tc:, :] = jnp.zeros((8, C), F32)
        dn_ref[tc:, :] = jnp.zeros((8, C), F32)

        def chunk(step, gnext):
            c = nch - 1 - step
            r0 = pl.multiple_of(c * tc, tc)
            p0 = pl.multiple_of(jnp.maximum(r0 - 8, 0), 8)
            q0 = pl.multiple_of(jnp.maximum(r0 - 16, 0), 16)
            live = c > 0
            x = x_ref[pl.ds(r0, tc), :].astype(F32)
            xs_ref[0:8, :] = jnp.where(live, x_ref[pl.ds(q0, 16), :].astype(F32)[8:, :], 0.0)
            xs_ref[8:, :] = x
            xsh = [xs_ref[pl.ds(8 - (CONV_WIDTH - 1 - k), tc), :] for k in range(CONV_WIDTH)]
            xc = cb_ref[...] + jnp.zeros((tc, C), F32)
            for k in range(CONV_WIDTH):
                xc = xc + xsh[k] * cw_ref[k:k + 1, :]
            r, i, a, om, mult = _rnn_gates(xc, wr, br_ref[...], wi, bi_ref[...], sp)
            h = hs_ref[pl.ds(r0, tc), :]
            hp_ref[0:8, :] = jnp.where(live, hs_ref[pl.ds(p0, 8), :], 0.0)
            hp_ref[8:, :] = h
            hm1 = hp_ref[pl.ds(7, tc), :]
            g = g_ref[pl.ds(r0, tc), :].astype(F32)
            ge, th = _gelu(g)
            dy = dy_ref[pl.ds(r0, tc), :]
            dg_ref[pl.ds(r0, tc), :] = (dy * h * _gelu_grad(g, th)).astype(BF16)
            an_ref[0:tc, :] = a
            coef = an_ref[pl.ds(1, tc), :]
            ccum, bcum = _scan_bwd(coef, dy * ge)
            G = bcum + ccum * gnext
            an_ref[tc:, :] = a[0:8, :]
            da = G * hm1
            ixc = i * xc
            dmult = G * ixc
            di = G * mult * xc
            dxc = G * mult * i
            dla = da * a - dmult * (1.0 - om) / mult
            dr = dla * (-LRU_C * sp)
            dlam_ref[...] += jnp.sum(dla * r, axis=0, keepdims=True)
            dzr = dr * r * (1.0 - r)
            dzi = di * i * (1.0 - i)
            dbr_ref[...] += jnp.sum(dzr, axis=0, keepdims=True)
            dbi_ref[...] += jnp.sum(dzi, axis=0, keepdims=True)
            xb = xc.astype(BF16)
            dzrb = dzr.astype(BF16)
            dzib = dzi.astype(BF16)
            dwr_ref[0] += lax.dot_general(xb, dzrb, _DIMS["tn"], preferred_element_type=F32)
            dwi_ref[0] += lax.dot_general(xb, dzib, _DIMS["tn"], preferred_element_type=F32)
            dxc = dxc + lax.dot_general(dzrb, wr, _DIMS["nt"], preferred_element_type=F32)
            dxc = dxc + lax.dot_general(dzib, wi, _DIMS["nt"], preferred_element_type=F32)
            dcb_ref[...] += jnp.sum(dxc, axis=0, keepdims=True)
            for k in range(CONV_WIDTH):
                dcw_ref[k:k + 1, :] += jnp.sum(dxc * xsh[k], axis=0, keepdims=True)
            dn_ref[0:tc, :] = dxc
            dx = jnp.zeros((tc, C), F32)
            for k in range(CONV_WIDTH):
                dx = dx + dn_ref[pl.ds(CONV_WIDTH - 1 - k, tc), :] * cw_ref[k:k + 1, :]
            dn_ref[tc:, :] = dxc[0:8, :]
            dx_ref[pl.ds(r0, tc), :] = dx.astype(BF16)
            return G[0:1, :]

        lax.fori_loop(0, nch, chunk, jnp.zeros((1, C), F32))
        dlam_ref[...] = dlam_ref[...] * (LRU_C * _sigmoid(-lam_v))

    col, vec, cwspec, w, own = _rnn_specs(T)
    vshape = jax.ShapeDtypeStruct((1, D_MODEL), F32)
    wshape = jax.ShapeDtypeStruct((RNN_BLOCKS, C, C), F32)
    return pl.pallas_call(
        body, name=name, grid=(RNN_BLOCKS,),
        in_specs=[col(C_XR), col(C_GR), own, own, cwspec, vec, w, vec, w, vec, vec],
        out_specs=[own, own, cwspec, vec, w, vec, w, vec, vec],
        out_shape=[jax.ShapeDtypeStruct((T, D_MODEL), BF16), jax.ShapeDtypeStruct((T, D_MODEL), BF16),
                   jax.ShapeDtypeStruct((CONV_WIDTH, D_MODEL), F32), vshape, wshape, vshape, wshape, vshape, vshape],
        scratch_shapes=[pltpu.VMEM((tc + 8, C), F32), pltpu.VMEM((tc + 8, C), F32),
                        pltpu.VMEM((tc + 8, C), F32), pltpu.VMEM((tc + 8, C), F32)],
        compiler_params=pltpu.CompilerParams(dimension_semantics=("parallel",)),
    )(P, P, hs, dy, cw, cb, wrg, brg, wig, big, lam)


def _rope_table(T):
    half = ROT_DIM // 2
    pos = jnp.arange(T, dtype=F32)
    inv_freq = ROPE_THETA ** (-jnp.arange(0, ROT_DIM, 2, dtype=F32) / ROT_DIM)
    ang = pos[:, None] * inv_freq[None, :]
    cos, sin = jnp.cos(ang), jnp.sin(ang)
    one = jnp.ones((T, HEAD_DIM - ROT_DIM), F32)
    zero = jnp.zeros((T, HEAD_DIM - ROT_DIM), F32)
    z8 = jnp.zeros((T, half), F32)
    c = jnp.concatenate([cos, cos, one], axis=1)
    a = jnp.concatenate([-sin, z8, zero], axis=1)
    b = jnp.concatenate([z8, sin, zero], axis=1)
    return jnp.stack([jnp.tile(c, (1, 2)), jnp.tile(a, (1, 2)), jnp.tile(b, (1, 2))])


def _rope(x, tab, sign):
    W = x.shape[1]
    rep = W // 128
    c = jnp.tile(tab[0], (1, rep)) if rep > 1 else tab[0]
    a = jnp.tile(tab[1], (1, rep)) if rep > 1 else tab[1]
    b = jnp.tile(tab[2], (1, rep)) if rep > 1 else tab[2]
    return x * c + sign * (pltpu.roll(x, W - ROT_DIM // 2, 1) * a + pltpu.roll(x, ROT_DIM // 2, 1) * b)


def _swa_mask(n):
    rows = lax.broadcasted_iota(jnp.int32, (GROUP * BLOCK, 2 * BLOCK), 0) & (BLOCK - 1)
    cols = lax.broadcasted_iota(jnp.int32, (GROUP * BLOCK, 2 * BLOCK), 1)
    return (cols > rows) & (cols <= rows + BLOCK) & ((n > 0) | (cols >= BLOCK))


def _swa_probs(qg, k2, sink, valid):
    s = lax.dot_general(qg, k2, _DIMS["nt"], preferred_element_type=F32) * (HEAD_DIM ** -0.5)
    s = jnp.where(valid, s, NEG_INF)
    m = jnp.maximum(jnp.max(s, axis=1, keepdims=True), sink)
    p = jnp.exp(s - m)
    ps = jnp.exp(sink - m)
    inv = 1.0 / (jnp.sum(p, axis=1, keepdims=True) + ps)
    return p * inv, ps * inv


def _swa_specs(T):
    nb = T // BLOCK
    qspec = pl.BlockSpec((BLOCK, D_MODEL), lambda n: (n, C_Q // D_MODEL))
    cur = lambda off: pl.BlockSpec((BLOCK, KV_WIDTH), lambda n, off=off: (n, off // KV_WIDTH))
    prev = lambda off: pl.BlockSpec((BLOCK, KV_WIDTH), lambda n, off=off: (jnp.maximum(n - 1, 0), off // KV_WIDTH))
    tcur = pl.BlockSpec((3, BLOCK, 128), lambda n: (0, n, 0))
    tprev = pl.BlockSpec((3, BLOCK, 128), lambda n: (0, jnp.maximum(n - 1, 0), 0))
    sink = pl.BlockSpec((N_KV_HEADS, GROUP * BLOCK, 1), lambda n: (0, 0, 0))
    own = pl.BlockSpec((BLOCK, D_MODEL), lambda n: (n, 0))
    return nb, qspec, cur, prev, tcur, tprev, sink, own


def _stack_heads(x, hk):
    return jnp.concatenate([x[:, (hk * GROUP + g) * HEAD_DIM:(hk * GROUP + g + 1) * HEAD_DIM] for g in range(GROUP)],
                           axis=0)


def _swa_fwd(P, tab, sink_col, name):
    T = P.shape[0]
    nb, qspec, cur, prev, tcur, tprev, sink, own = _swa_specs(T)

    def body(q_ref, kc_ref, kp_ref, vc_ref, vp_ref, tc_ref, tp_ref, sk_ref, o_ref):
        n = pl.program_id(0)
        valid = _swa_mask(n)
        q = _rope(q_ref[...].astype(F32), tc_ref[...], 1.0).astype(BF16)
        k2 = jnp.concatenate([_rope(kp_ref[...].astype(F32), tp_ref[...], 1.0),
                              _rope(kc_ref[...].astype(F32), tc_ref[...], 1.0)], axis=0).astype(BF16)
        v2 = jnp.concatenate([vp_ref[...], vc_ref[...]], axis=0).astype(BF16)
        parts = []
        for hk in range(N_KV_HEADS):
            sl = slice(hk * HEAD_DIM, (hk + 1) * HEAD_DIM)
            pn, _ = _swa_probs(_stack_heads(q, hk), k2[:, sl], sk_ref[hk], valid)
            og = jnp.dot(pn.astype(BF16), v2[:, sl], preferred_element_type=F32)
            parts += [og[g * BLOCK:(g + 1) * BLOCK, :] for g in range(GROUP)]
        o_ref[...] = jnp.concatenate(parts, axis=1).astype(BF16)

    return pl.pallas_call(
        body, name=name, grid=(nb,),
        in_specs=[qspec, cur(C_K), prev(C_K), cur(C_V), prev(C_V), tcur, tprev, sink],
        out_specs=own, out_shape=jax.ShapeDtypeStruct((T, D_MODEL), BF16),
        compiler_params=pltpu.CompilerParams(dimension_semantics=("parallel",)),
    )(P, P, P, P, P, tab, tab, sink_col)


def _swa_bwd(P, do, tab, sink_col, name):
    T = P.shape[0]
    nb, qspec, cur, prev, tcur, tprev, sink, own = _swa_specs(T)

    def body(q_ref, kc_ref, kp_ref, vc_ref, vp_ref, do_ref, tc_ref, tp_ref, sk_ref,
             dq_ref, dk_ref, dv_ref, ds_ref):
        n = pl.program_id(0)

        @pl.when(n == 0)
        def _():
            dk_ref[...] = jnp.zeros_like(dk_ref)
            dv_ref[...] = jnp.zeros_like(dv_ref)
            ds_ref[...] = jnp.zeros_like(ds_ref)

        valid = _swa_mask(n)
        tcur_v = tc_ref[...]
        tprev_v = tp_ref[...]
        q = _rope(q_ref[...].astype(F32), tcur_v, 1.0).astype(BF16)
        k2 = jnp.concatenate([_rope(kp_ref[...].astype(F32), tprev_v, 1.0),
                              _rope(kc_ref[...].astype(F32), tcur_v, 1.0)], axis=0).astype(BF16)
        v2 = jnp.concatenate([vp_ref[...], vc_ref[...]], axis=0).astype(BF16)
        dob = do_ref[...].astype(BF16)
        dq_parts = []
        dk_parts = []
        dv_parts = []
        for hk in range(N_KV_HEADS):
            sl = slice(hk * HEAD_DIM, (hk + 1) * HEAD_DIM)
            qg = _stack_heads(q, hk)
            dog = _stack_heads(dob, hk)
            pn, psn = _swa_probs(qg, k2[:, sl], sk_ref[hk], valid)
            dp = lax.dot_general(dog, v2[:, sl], _DIMS["nt"], preferred_element_type=F32)
            delta = jnp.sum(pn * dp, axis=1, keepdims=True)
            dsc = (pn * (dp - delta) * (HEAD_DIM ** -0.5)).astype(BF16)
            dsink = -psn * delta
            for g in range(GROUP):
                ds_ref[hk, g:g + 1, :] += jnp.broadcast_to(
                    jnp.sum(dsink[g * BLOCK:(g + 1) * BLOCK], axis=0, keepdims=True), (1, 128))
            dqg = jnp.dot(dsc, k2[:, sl], preferred_element_type=F32)
            dq_parts += [dqg[g * BLOCK:(g + 1) * BLOCK, :] for g in range(GROUP)]
            dk_parts.append(lax.dot_general(dsc, qg, _DIMS["tn"], preferred_element_type=F32))
            dv_parts.append(lax.dot_general(pn.astype(BF16), dog, _DIMS["tn"], preferred_element_type=F32))
        dq_ref[...] = _rope(jnp.concatenate(dq_parts, axis=1), tcur_v, -1.0).astype(BF16)
        dk2 = jnp.concatenate(dk_parts, axis=1)
        dv2 = jnp.concatenate(dv_parts, axis=1)
        c0 = pl.multiple_of(n * BLOCK, BLOCK)
        p0 = pl.multiple_of(jnp.maximum(n - 1, 0) * BLOCK, BLOCK)
        dk_ref[pl.ds(p0, BLOCK), :] += _rope(dk2[:BLOCK], tprev_v, -1.0)
        dv_ref[pl.ds(p0, BLOCK), :] += dv2[:BLOCK]
        dk_ref[pl.ds(c0, BLOCK), :] += _rope(dk2[BLOCK:], tcur_v, -1.0)
        dv_ref[pl.ds(c0, BLOCK), :] += dv2[BLOCK:]

    full = pl.BlockSpec((T, KV_WIDTH), lambda n: (0, 0))
    return pl.pallas_call(
        body, name=name, grid=(nb,),
        in_specs=[qspec, cur(C_K), prev(C_K), cur(C_V), prev(C_V), own, tcur, tprev, sink],
        out_specs=[own, full, full, pl.BlockSpec((N_KV_HEADS, GROUP, 128), lambda n: (0, 0, 0))],
        out_shape=[jax.ShapeDtypeStruct((T, D_MODEL), BF16), jax.ShapeDtypeStruct((T, KV_WIDTH), F32),
                   jax.ShapeDtypeStruct((T, KV_WIDTH), F32), jax.ShapeDtypeStruct((N_KV_HEADS, GROUP, 128), F32)],
        compiler_params=pltpu.CompilerParams(dimension_semantics=("arbitrary",)),
    )(P, P, P, P, P, do, tab, tab, sink_col)


_MW = 256


def _gate_specs(T, rows, width):
    tr = _pick(T, (rows, 256, 128, 64, 32, 16, 8))
    col = lambda off: pl.BlockSpec((tr, width), lambda i, j, off=off: (i, off // width + j))
    own = pl.BlockSpec((tr, width), lambda i, j: (i, j))
    return tr, col, own


def _merge_fwd(P, mr, ma, name):
    T = P.shape[0]
    tr, col, own = _gate_specs(T, 1024, _MW)

    def body(gr_ref, ga_ref, mr_ref, ma_ref, o_ref):
        o_ref[...] = (_sigmoid(gr_ref[...].astype(F32)) * mr_ref[...]
                      + _sigmoid(ga_ref[...].astype(F32)) * ma_ref[...]).astype(BF16)

    return pl.pallas_call(
        body, name=name, grid=(T // tr, D_MODEL // _MW), in_specs=[col(C_GRNN), col(C_GATTN), own, own],
        out_specs=own, out_shape=jax.ShapeDtypeStruct((T, D_MODEL), BF16),
        compiler_params=pltpu.CompilerParams(dimension_semantics=("parallel", "parallel")),
    )(P, P, mr, ma)


def _merge_bwd(P, mr, ma, dm, name):
    T = P.shape[0]
    tr, col, own = _gate_specs(T, 512, _MW)

    def body(gr_ref, ga_ref, mr_ref, ma_ref, dm_ref, dmr_ref, dma_ref, dgr_ref, dga_ref):
        dm = dm_ref[...]
        sr = _sigmoid(gr_ref[...].astype(F32))
        sa = _sigmoid(ga_ref[...].astype(F32))
        dmr_ref[...] = (dm * sr).astype(BF16)
        dma_ref[...] = (dm * sa).astype(BF16)
        dgr_ref[...] = (dm * mr_ref[...] * sr * (1.0 - sr)).astype(BF16)
        dga_ref[...] = (dm * ma_ref[...] * sa * (1.0 - sa)).astype(BF16)

    shp = jax.ShapeDtypeStruct((T, D_MODEL), BF16)
    return pl.pallas_call(
        body, name=name, grid=(T // tr, D_MODEL // _MW), in_specs=[col(C_GRNN), col(C_GATTN), own, own, own],
        out_specs=[own] * 4, out_shape=[shp] * 4,
        compiler_params=pltpu.CompilerParams(dimension_semantics=("parallel", "parallel")),
    )(P, P, mr, ma, dm)


_FFN_ROWS = 128


def _swiglu_fwd(U, name):
    T = U.shape[0]
    tr = _pick(T, (_FFN_ROWS, 64, 32, 16))
    half = lambda j: pl.BlockSpec((tr, D_FF), lambda i, j=j: (i, j))

    def body(g_ref, u_ref, o_ref):
        g = g_ref[...].astype(F32)
        o_ref[...] = (g * _sigmoid(g) * u_ref[...].astype(F32)).astype(BF16)

    return pl.pallas_call(
        body, name=name, grid=(T // tr,), in_specs=[half(0), half(1)],
        out_specs=half(0), out_shape=jax.ShapeDtypeStruct((T, D_FF), BF16),
        compiler_params=pltpu.CompilerParams(dimension_semantics=("parallel",)),
    )(U, U)


def _swiglu_bwd(U, dact, name):
    T = U.shape[0]
    tr = _pick(T, (_FFN_ROWS, 64, 32, 16))
    half = lambda j: pl.BlockSpec((tr, D_FF), lambda i, j=j: (i, j))

    def body(g_ref, u_ref, da_ref, o_ref):
        g = g_ref[...].astype(F32)
        da = da_ref[...].astype(F32)
        s = _sigmoid(g)
        o_ref[:, :D_FF] = (da * u_ref[...].astype(F32) * s * (1.0 + g * (1.0 - s))).astype(BF16)
        o_ref[:, D_FF:] = (da * g * s).astype(BF16)

    return pl.pallas_call(
        body, name=name, grid=(T // tr,), in_specs=[half(0), half(1), half(0)],
        out_specs=pl.BlockSpec((tr, 2 * D_FF), lambda i: (i, 0)),
        out_shape=jax.ShapeDtypeStruct((T, 2 * D_FF), BF16),
        compiler_params=pltpu.CompilerParams(dimension_semantics=("parallel",)),
    )(U, U, dact)


def _cross_probs(qh, kh):
    s = lax.dot_general(qh, kh, _DIMS["nt"], preferred_element_type=F32) * (CROSS_HEAD_DIM ** -0.5)
    p = jnp.exp(s - jnp.max(s, axis=1, keepdims=True))
    return p / jnp.sum(p, axis=1, keepdims=True)


def _cross_fwd(q, kv, name):
    T = q.shape[0]
    M = kv.shape[0]
    tr = _pick(T, (ROW_TILE, 128, 64, 32, 16, 8))
    W = CROSS_HEAD_DIM

    def body(q_ref, kv_ref, o_ref):
        for h in range(CROSS_HEADS):
            qh = q_ref[:, h * W:(h + 1) * W].astype(BF16)
            kh = kv_ref[:, h * W:(h + 1) * W].astype(BF16)
            vh = kv_ref[:, D_MODEL + h * W:D_MODEL + (h + 1) * W].astype(BF16)
            pn = _cross_probs(qh, kh)
            o_ref[:, h * W:(h + 1) * W] = jnp.dot(pn.astype(BF16), vh, preferred_element_type=F32).astype(BF16)

    row = pl.BlockSpec((tr, D_MODEL), lambda i: (i, 0))
    return pl.pallas_call(
        body, name=name, grid=(T // tr,), in_specs=[row, pl.BlockSpec((M, 2 * D_MODEL), lambda i: (0, 0))],
        out_specs=row, out_shape=jax.ShapeDtypeStruct((T, D_MODEL), BF16),
        compiler_params=pltpu.CompilerParams(dimension_semantics=("parallel",)),
    )(q, kv)


def _cross_bwd(q, kv, do, name):
    T = q.shape[0]
    M = kv.shape[0]
    tr = _pick(T, (ROW_TILE, 128, 64, 32, 16, 8))
    W = CROSS_HEAD_DIM

    def body(q_ref, kv_ref, do_ref, dq_ref, dkv_ref):
        @pl.when(pl.program_id(0) == 0)
        def _():
            dkv_ref[...] = jnp.zeros_like(dkv_ref)

        for h in range(CROSS_HEADS):
            qh = q_ref[:, h * W:(h + 1) * W].astype(BF16)
            kh = kv_ref[:, h * W:(h + 1) * W].astype(BF16)
            vh = kv_ref[:, D_MODEL + h * W:D_MODEL + (h + 1) * W].astype(BF16)
            doh = do_ref[:, h * W:(h + 1) * W].astype(BF16)
            pn = _cross_probs(qh, kh)
            dp = lax.dot_general(doh, vh, _DIMS["nt"], preferred_element_type=F32)
            delta = jnp.sum(pn * dp, axis=1, keepdims=True)
            dsc = (pn * (dp - delta) * (W ** -0.5)).astype(BF16)
            dq_ref[:, h * W:(h + 1) * W] = jnp.dot(dsc, kh, preferred_element_type=F32).astype(BF16)
            dkv_ref[:, h * W:(h + 1) * W] += lax.dot_general(dsc, qh, _DIMS["tn"], preferred_element_type=F32)
            dkv_ref[:, D_MODEL + h * W:D_MODEL + (h + 1) * W] += lax.dot_general(
                pn.astype(BF16), doh, _DIMS["tn"], preferred_element_type=F32)

    row = pl.BlockSpec((tr, D_MODEL), lambda i: (i, 0))
    full = pl.BlockSpec((M, 2 * D_MODEL), lambda i: (0, 0))
    return pl.pallas_call(
        body, name=name, grid=(T // tr,), in_specs=[row, full, row], out_specs=[row, full],
        out_shape=[jax.ShapeDtypeStruct((T, D_MODEL), BF16), jax.ShapeDtypeStruct((M, 2 * D_MODEL), F32)],
        compiler_params=pltpu.CompilerParams(dimension_semantics=("arbitrary",)),
    )(q, kv, do)


def _loss_head(y, target, name):
    T, D = y.shape
    tr = _pick(T, (ROW_TILE, 128, 64, 32, 16, 8))

    def body(y_ref, t_ref, l_ref, dy_ref):
        @pl.when(pl.program_id(0) == 0)
        def _():
            l_ref[...] = jnp.zeros_like(l_ref)

        err = y_ref[...] - t_ref[...]
        dy_ref[...] = err * (1.0 / D)
        l_ref[...] += jnp.broadcast_to(0.5 * jnp.sum(jnp.mean(err * err, axis=-1, keepdims=True), axis=0, keepdims=True),
                                       (8, 128))

    row = pl.BlockSpec((tr, D), lambda i: (i, 0))
    return pl.pallas_call(
        body, name=name, grid=(T // tr,), in_specs=[row, row],
        out_specs=[pl.BlockSpec((8, 128), lambda i: (0, 0)), row],
        out_shape=[jax.ShapeDtypeStruct((8, 128), F32), jax.ShapeDtypeStruct((T, D), F32)],
        compiler_params=pltpu.CompilerParams(dimension_semantics=("arbitrary",)),
    )(y, target)


def _sum_slots(recv, name):
    _, R, C = recv.shape
    tr = _pick(R, (ROW_TILE, 128, 64, 32, 16, 8))

    def body(r_ref, o_ref):
        acc = r_ref[0].astype(F32)
        for d in range(1, N_DEV):
            acc = acc + r_ref[d].astype(F32)
        o_ref[...] = acc

    return pl.pallas_call(
        body, name=name, grid=(R // tr,), in_specs=[pl.BlockSpec((N_DEV, tr, C), lambda i: (0, i, 0))],
        out_specs=pl.BlockSpec((tr, C), lambda i: (i, 0)), out_shape=jax.ShapeDtypeStruct((R, C), F32),
        compiler_params=pltpu.CompilerParams(dimension_semantics=("parallel",)),
    )(recv)


SUM_STEPS = 2


def _sum_blocks(recvs, sents, me, name):
    n = len(recvs)

    def body(me_ref, *refs):
        me = me_ref[0]
        for i in range(n):
            r_ref, s_ref, o_ref = refs[i], refs[n + i], refs[2 * n + i]
            acc = s_ref[0].astype(F32)
            for d in range(N_DEV):
                acc = acc + jnp.where(d == me, 0.0, r_ref[d].astype(F32))
            o_ref[...] = acc

    tiles = [r.shape[1] // SUM_STEPS for r in recvs]
    C = recvs[0].shape[2]
    return pl.pallas_call(
        body, name=name,
        grid_spec=pltpu.PrefetchScalarGridSpec(
            num_scalar_prefetch=1, grid=(SUM_STEPS,),
            in_specs=[pl.BlockSpec((N_DEV, t, C), lambda s, me_ref: (0, s, 0)) for t in tiles]
            + [pl.BlockSpec((1, t, C), lambda s, me_ref: (me_ref[0], s, 0)) for t in tiles],
            out_specs=[pl.BlockSpec((t, C), lambda s, me_ref: (s, 0)) for t in tiles]),
        out_shape=[jax.ShapeDtypeStruct((r.shape[1], C), F32) for r in recvs],
        compiler_params=pltpu.CompilerParams(dimension_semantics=("parallel",)),
    )(me, *recvs, *sents)


def _adamw(w, g, m, v, name):
    shape = w.shape
    C = shape[-1]
    R = math.prod(shape[:-1])
    w2, g2, m2, v2 = (t.reshape(R, C) for t in (w, g, m, v))
    tr = _pick(R, (ROW_TILE, 128, 64, 32, 16, 8))

    def body(w_ref, g_ref, m_ref, v_ref, d_ref, mo_ref, vo_ref):
        gg = g_ref[...]
        mn = ADAM_B1 * m_ref[...] + (1.0 - ADAM_B1) * gg
        vn = ADAM_B2 * v_ref[...] + (1.0 - ADAM_B2) * (gg * gg)
        m_hat = mn / (1.0 - ADAM_B1 ** ADAM_STEP)
        v_hat = vn / (1.0 - ADAM_B2 ** ADAM_STEP)
        d_ref[...] = -ADAM_LR * (m_hat / (jnp.sqrt(v_hat) + ADAM_EPS) + ADAM_WD * w_ref[...])
        mo_ref[...] = mn
        vo_ref[...] = vn

    blk = pl.BlockSpec((tr, C), lambda i: (i, 0))
    shp = jax.ShapeDtypeStruct((R, C), F32)
    d, mo, vo = pl.pallas_call(
        body, name=name, grid=(R // tr,), in_specs=[blk] * 4, out_specs=[blk] * 3, out_shape=[shp] * 3,
        compiler_params=pltpu.CompilerParams(dimension_semantics=("parallel",)),
    )(w2, g2, m2, v2)
    return d.reshape(shape), mo.reshape(shape), vo.reshape(shape)


def _all_gather_many(bufs, name):
    n = len(bufs)

    def body(*refs):
        xs, outs = refs[:n], refs[n:2 * n]
        send_sems, recv_sems, local_sems = refs[2 * n:]
        x, y, c = lax.axis_index("x"), lax.axis_index("y"), lax.axis_index("c")
        me, sibling = (x, y, c), (x, y, 1 - c)
        chips = [(1 - x, y), (x, 1 - y), (1 - x, 1 - y)]

        def slot(i, px, py, pc):
            return outs[i].at[4 * px + 2 * py + pc]

        def copy(i, k, block, to, src=None):
            return pltpu.make_async_remote_copy(
                src_ref=slot(i, *block) if src is None else src, dst_ref=slot(i, *block),
                send_sem=send_sems.at[7 * i + k], recv_sem=recv_sems.at[7 * i + k],
                device_id=to, device_id_type=pl.DeviceIdType.MESH)

        mine = [pltpu.make_async_copy(xs[i], slot(i, *me), local_sems.at[i]) for i in range(n)]
        for cp in mine:
            cp.start()
        first = [copy(i, 0, me, sibling, src=xs[i]) for i in range(n)]
        for j, chip in enumerate(chips):
            first += [copy(i, 1 + j, me, (*chip, c), src=xs[i]) for i in range(n)]
        for cp in first:
            cp.start()
        passed = []
        for j, chip in enumerate(chips):
            for i in range(n):
                copy(i, 1 + j, (*chip, c), me).wait_recv()
                passed.append(copy(i, 4 + j, (*chip, c), sibling))
                passed[-1].start()
        for i in range(n):
            copy(i, 0, sibling, me).wait_recv()
        for j, chip in enumerate(chips):
            for i in range(n):
                copy(i, 4 + j, (*chip, 1 - c), me).wait_recv()
        for cp in first + passed:
            cp.wait_send()
        for cp in mine:
            cp.wait()

    hbm = pl.BlockSpec(memory_space=pl.ANY)
    return pl.pallas_call(
        body, name=name, out_shape=[jax.ShapeDtypeStruct((N_DEV,) + b.shape, b.dtype) for b in bufs],
        in_specs=[hbm] * n, out_specs=[hbm] * n,
        scratch_shapes=[pltpu.SemaphoreType.DMA((7 * n,)), pltpu.SemaphoreType.DMA((7 * n,)),
                        pltpu.SemaphoreType.DMA((n,))],
    )(*bufs)


def _all_gather(buf, name):
    return _all_gather_many([buf], name)[0]


_HBM = pl.BlockSpec(memory_space=pltpu.HBM)
_SEM = pl.BlockSpec(memory_space=pltpu.SEMAPHORE)
_EFFECT = pltpu.SideEffectType.DATAFLOW_SIDE_EFFECTING


def _peer(k):
    x, y, c = lax.axis_index("x"), lax.axis_index("y"), lax.axis_index("c")
    return x ^ ((k >> 2) & 1), y ^ ((k >> 1) & 1), c ^ (k & 1)


def _my_slot():
    return 4 * lax.axis_index("x") + 2 * lax.axis_index("y") + lax.axis_index("c")


def _split_copy(src_refs, land_refs, send_sems, recv_sems, i, k):
    px, py, pc = _peer(k)
    return pltpu.make_async_remote_copy(
        src_ref=src_refs[i].at[4 * px + 2 * py + pc], dst_ref=land_refs[i].at[_my_slot()],
        send_sem=send_sems.at[7 * i + k - 1], recv_sem=recv_sems.at[7 * i + k - 1],
        device_id=(px, py, pc), device_id_type=pl.DeviceIdType.MESH)


def _split_start(srcs, lands, name):
    n = len(srcs)

    def body(*refs):
        src_refs, land_refs = refs[:n], refs[n:2 * n]
        send_sems, recv_sems = refs[2 * n], refs[2 * n + 1]
        token = refs[-1]
        for i in range(n):
            for k in range(1, N_DEV):
                _split_copy(src_refs, land_refs, send_sems, recv_sems, i, k).start()
        token[...] = jnp.zeros_like(token)

    outs = pl.pallas_call(
        body, name=name,
        out_shape=(pltpu.SemaphoreType.DMA((7 * n,)), pltpu.SemaphoreType.DMA((7 * n,)),
                   *[pltpu.HBM(a.shape, a.dtype) for a in srcs], *[pltpu.HBM(a.shape, a.dtype) for a in lands],
                   jax.ShapeDtypeStruct((8, 128), F32)),
        in_specs=[_HBM] * (2 * n),
        out_specs=(_SEM, _SEM, *([_HBM] * (2 * n)), pl.BlockSpec(memory_space=pltpu.VMEM)),
        input_output_aliases={i: 2 + i for i in range(2 * n)},
        compiler_params=pltpu.CompilerParams(has_side_effects=_EFFECT),
    )(*[pltpu.with_memory_space_constraint(a, pltpu.HBM) for a in list(srcs) + list(lands)])
    return outs[0], outs[1], outs[2:2 + n], outs[2 + n:2 + 2 * n], outs[-1]


def _split_wait(send_sems, recv_sems, srcs, lands, after, name):
    n = len(srcs)

    def body(*refs):
        src_refs, land_refs = refs[:n], refs[n:2 * n]
        ssem, rsem = refs[2 * n], refs[2 * n + 1]
        for i in range(n):
            for k in range(1, N_DEV):
                cp = _split_copy(src_refs, land_refs, ssem, rsem, i, k)
                cp.wait_send()
                cp.wait_recv()

    outs = pl.pallas_call(
        body, name=name,
        out_shape=(*[pltpu.HBM(a.shape, a.dtype) for a in srcs], *[pltpu.HBM(a.shape, a.dtype) for a in lands]),
        in_specs=[*([_HBM] * (2 * n)), _SEM, _SEM, pl.BlockSpec(memory_space=pl.ANY)],
        out_specs=tuple([_HBM] * (2 * n)),
        input_output_aliases={i: i for i in range(2 * n)},
        compiler_params=pltpu.CompilerParams(has_side_effects=_EFFECT),
    )(*srcs, *lands, send_sems, recv_sems, after)
    return outs[:n], outs[n:]


def _gather_first(src_refs, land_refs, send_sems, recv_sems, i, k):
    x, y, c = lax.axis_index("x"), lax.axis_index("y"), lax.axis_index("c")
    to = ((x, y, 1 - c), (1 - x, y, c), (x, 1 - y, c), (1 - x, 1 - y, c))[k]
    return pltpu.make_async_remote_copy(
        src_ref=src_refs[i], dst_ref=land_refs[i].at[_my_slot()],
        send_sem=send_sems.at[4 * i + k], recv_sem=recv_sems.at[4 * i + k],
        device_id=to, device_id_type=pl.DeviceIdType.MESH)


def _gather_second(land_refs, send_sems, recv_sems, i, j):
    x, y, c = lax.axis_index("x"), lax.axis_index("y"), lax.axis_index("c")
    px, py = ((1 - x, y), (x, 1 - y), (1 - x, 1 - y))[j]
    slot = land_refs[i].at[4 * px + 2 * py + c]
    return pltpu.make_async_remote_copy(
        src_ref=slot, dst_ref=slot, send_sem=send_sems.at[3 * i + j], recv_sem=recv_sems.at[3 * i + j],
        device_id=(x, y, 1 - c), device_id_type=pl.DeviceIdType.MESH)


def _gather_start(srcs, lands, name, after):
    n = len(srcs)

    def body(*refs):
        src_refs, land_refs = refs[:n], refs[n:2 * n]
        send_sems, recv_sems = refs[2 * n + 1], refs[2 * n + 2]
        token = refs[-1]
        for k in range(4):
            for i in range(n):
                _gather_first(src_refs, land_refs, send_sems, recv_sems, i, k).start()
        token[...] = jnp.zeros_like(token)

    outs = pl.pallas_call(
        body, name=name,
        out_shape=(pltpu.SemaphoreType.DMA((4 * n,)), pltpu.SemaphoreType.DMA((4 * n,)),
                   *[pltpu.HBM(a.shape, a.dtype) for a in srcs], *[pltpu.HBM(a.shape, a.dtype) for a in lands],
                   jax.ShapeDtypeStruct((8, 128), F32)),
        in_specs=[_HBM] * (2 * n) + [pl.BlockSpec(memory_space=pl.ANY)],
        out_specs=(_SEM, _SEM, *([_HBM] * (2 * n)), pl.BlockSpec(memory_space=pltpu.VMEM)),
        input_output_aliases={i: 2 + i for i in range(2 * n)},
        compiler_params=pltpu.CompilerParams(has_side_effects=_EFFECT),
    )(*[pltpu.with_memory_space_constraint(a, pltpu.HBM) for a in list(srcs) + list(lands)], after)
    return outs[0], outs[1], outs[2:2 + n], outs[2 + n:2 + 2 * n], outs[-1]


def _gather_forward(send1, recv1, srcs, lands, after, name):
    n = len(srcs)

    def body(*refs):
        src_refs, land_refs = refs[:n], refs[n:2 * n]
        s1, r1 = refs[2 * n], refs[2 * n + 1]
        s2, r2 = refs[2 * n + 3], refs[2 * n + 4]
        token = refs[-1]
        for j in range(3):
            for i in range(n):
                _gather_first(src_refs, land_refs, s1, r1, i, 1 + j).wait_recv()
                _gather_second(land_refs, s2, r2, i, j).start()
        for i in range(n):
            _gather_first(src_refs, land_refs, s1, r1, i, 0).wait_recv()
            for k in range(4):
                _gather_first(src_refs, land_refs, s1, r1, i, k).wait_send()
        token[...] = jnp.zeros_like(token)

    outs = pl.pallas_call(
        body, name=name,
        out_shape=(pltpu.SemaphoreType.DMA((3 * n,)), pltpu.SemaphoreType.DMA((3 * n,)),
                   *[pltpu.HBM(a.shape, a.dtype) for a in srcs], *[pltpu.HBM(a.shape, a.dtype) for a in lands],
                   jax.ShapeDtypeStruct((8, 128), F32)),
        in_specs=[*([_HBM] * (2 * n)), _SEM, _SEM, pl.BlockSpec(memory_space=pl.ANY)],
        out_specs=(_SEM, _SEM, *([_HBM] * (2 * n)), pl.BlockSpec(memory_space=pltpu.VMEM)),
        input_output_aliases={i: 2 + i for i in range(2 * n)},
        compiler_params=pltpu.CompilerParams(has_side_effects=_EFFECT),
    )(*srcs, *lands, send1, recv1, after)
    return outs[0], outs[1], outs[2:2 + n], outs[2 + n:2 + 2 * n], outs[-1]


def _gather_wait(send2, recv2, srcs, lands, after, name):
    n = len(srcs)

    def body(*refs):
        land_refs = refs[n:2 * n]
        s2, r2 = refs[2 * n], refs[2 * n + 1]
        for i in range(n):
            for j in range(3):
                cp = _gather_second(land_refs, s2, r2, i, j)
                cp.wait_send()
                cp.wait_recv()

    outs = pl.pallas_call(
        body, name=name,
        out_shape=(*[pltpu.HBM(a.shape, a.dtype) for a in srcs], *[pltpu.HBM(a.shape, a.dtype) for a in lands]),
        in_specs=[*([_HBM] * (2 * n)), _SEM, _SEM, pl.BlockSpec(memory_space=pl.ANY)],
        out_specs=tuple([_HBM] * (2 * n)),
        input_output_aliases={i: i for i in range(2 * n)},
        compiler_params=pltpu.CompilerParams(has_side_effects=_EFFECT),
    )(*srcs, *lands, send2, recv2, after)
    return outs[:n], outs[n:]


def _row(a, l):
    return a[l:l + 1]


def _tie(a, token):
    return a if token is None else a + token[0, 0]


def _layer_fwd(h, mem, W, l, tab, after, at):
    s = {}
    n = f"l{l}_"
    s["h0"] = h
    P = _matmul(h, W["w_in"][l], "nt", n + "proj", after=after, out_dtype=BF16)
    s["P"] = P
    sink_col = jnp.repeat(W["sinks"][l].reshape(N_KV_HEADS, GROUP), BLOCK, axis=1)[:, :, None]
    s["sink_col"] = sink_col
    y_rnn, hs = _rnn_fwd(P, W["conv_w"][l], _row(W["conv_b"], l), W["w_rg"][l], _row(W["b_rg"], l),
                         W["w_ig"][l], _row(W["b_ig"], l), _row(W["lru_lambda"], l), n + "rnn_fwd")
    y_attn = _swa_fwd(P, tab, _tie(sink_col, at(l, "proj", P)), n + "swa_fwd")
    at(l, "attn", y_attn)
    mr = _matmul(y_rnn, W["w_br_rnn"][l], "nn", n + "br_rnn")
    ma = _matmul(y_attn, W["w_br_attn"][l], "nn", n + "br_attn")
    merged = _merge_fwd(P, mr, ma, n + "merge_fwd")
    h1, xh1, rs1 = _matmul_ln(merged, W["w_out"][l], h, _tie(_row(W["ln1_g"], l), at(l, "mix", merged)),
                              _row(W["ln1_b"], l), n + "w_out_ln1")
    at(l, "ln1", h1)
    s.update(hs=hs, y_rnn=y_rnn, y_attn=y_attn, mr=mr, ma=ma, merged=merged, xh1=xh1, rs1=rs1, h1=h1)

    qc = _matmul(h1, W["cq_w"][l], "nn", n + "cq", out_dtype=BF16)
    kv = _matmul(mem, W["ckv_w"][l], "nt", n + "ckv", out_dtype=BF16)
    oc = _cross_fwd(qc, kv, n + "cross_fwd")
    h2, xh2, rs2 = _matmul_ln(oc, W["co_w"][l], h1, _row(W["ln2_g"], l), _row(W["ln2_b"], l), n + "co_ln2")
    s.update(qc=qc, kv=kv, oc=oc, xh2=xh2, rs2=rs2, h2=h2)

    U = _matmul(h2, W["ffn_wi"][l], "nt", n + "ffn_wi", after=at(l, "ln2", h2), out_dtype=BF16)
    act = _swiglu_fwd(U, n + "swiglu_fwd")
    h3, xh3, rs3 = _matmul_ln(act, W["ffn_wo"][l], h2, _tie(_row(W["ln3_g"], l), at(l, "ffn", act)),
                              _row(W["ln3_b"], l), n + "ffn_wo_ln3")
    s.update(U=U, act=act, xh3=xh3, rs3=rs3)
    return h3, s


GRAD_PARTS = (("ffn_wi", "ffn_wo", "cq_w", "ckv_w", "co_w"), ("w_out", "w_br_rnn", "w_br_attn"),
              ("w_in", "w_rg", "w_ig"))


def _layer_bwd(dh3, mem, W, l, tab, s, send):
    n = f"l{l}_"
    g = {}
    dz3, g["ln3_g"], g["ln3_b"] = _ln_bwd(dh3, s["xh3"], s["rs3"], _row(W["ln3_g"], l), n + "ln3_bwd")
    g["ffn_wo"] = _matmul(s["act"], dz3, "tn", n + "d_ffn_wo", out_dtype=BF16)
    dact = _matmul(dz3, W["ffn_wo"][l], "nt", n + "d_act", out_dtype=BF16)
    dU = _swiglu_bwd(s["U"], dact, n + "swiglu_bwd")
    g["ffn_wi"] = _matmul(dU, s["h2"], "tn", n + "d_ffn_wi", out_dtype=BF16)
    dh2 = _matmul(dU, W["ffn_wi"][l], "nn", n + "d_h2", add=dz3, add_scale=ALPHA)
    dz2, g["ln2_g"], g["ln2_b"] = _ln_bwd(dh2, s["xh2"], s["rs2"], _row(W["ln2_g"], l), n + "ln2_bwd")
    g["co_w"] = _matmul(s["oc"], dz2, "tn", n + "d_co", out_dtype=BF16)
    doc = _matmul(dz2, W["co_w"][l], "nt", n + "d_oc", out_dtype=BF16)
    dqc, dkv = _cross_bwd(s["qc"], s["kv"], doc, n + "cross_bwd")
    g["ckv_w"] = _matmul(dkv, mem, "tn", n + "d_ckv", out_dtype=BF16)
    g["cq_w"] = _matmul(s["h1"], dqc, "tn", n + "d_cq", out_dtype=BF16)
    after = send(l, 0, g)
    dh1 = _matmul(dqc, W["cq_w"][l], "nt", n + "d_h1", add=dz2, add_scale=ALPHA, after=after)
    dz1, g["ln1_g"], g["ln1_b"] = _ln_bwd(dh1, s["xh1"], s["rs1"], _row(W["ln1_g"], l), n + "ln1_bwd")
    g["w_out"] = _matmul(s["merged"], dz1, "tn", n + "d_w_out", out_dtype=BF16)
    dmerged = _matmul(dz1, W["w_out"][l], "nt", n + "d_merged")
    dmr, dma, dgrnn, dgattn = _merge_bwd(s["P"], s["mr"], s["ma"], dmerged, n + "merge_bwd")
    g["w_br_rnn"] = _matmul(s["y_rnn"], dmr, "tn", n + "d_br_rnn", out_dtype=BF16)
    g["w_br_attn"] = _matmul(s["y_attn"], dma, "tn", n + "d_br_attn", out_dtype=BF16)
    after = send(l, 1, g)
    dy_rnn = _matmul(dmr, W["w_br_rnn"][l], "nt", n + "d_y_rnn", after=after)
    dy_attn = _matmul(dma, W["w_br_attn"][l], "nt", n + "d_y_attn", out_dtype=BF16)
    dxr, dgr, g["conv_w"], g["conv_b"], g["w_rg"], g["b_rg"], g["w_ig"], g["b_ig"], g["lru_lambda"] = _rnn_bwd(
        s["P"], s["hs"], dy_rnn, W["conv_w"][l], _row(W["conv_b"], l), W["w_rg"][l], _row(W["b_rg"], l),
        W["w_ig"][l], _row(W["b_ig"], l), _row(W["lru_lambda"], l), n + "rnn_bwd")
    dq, dk, dv, dsk = _swa_bwd(s["P"], dy_attn, tab, s["sink_col"], n + "swa_bwd")
    g["sinks"] = dsk[:, :, 0].reshape(1, N_Q_HEADS)
    dP = jnp.concatenate([dxr, dgr, dq, dk.astype(BF16), dv.astype(BF16), dgrnn, dgattn], axis=1)
    g["w_in"] = _matmul(dP, s["h0"], "tn", n + "d_w_in", out_dtype=BF16)
    after = send(l, 2, g)
    dh = _matmul(dP, W["w_in"][l], "nn", n + "d_h0", add=dz1, add_scale=ALPHA, after=after)
    return dh, g


def _local_step(x, mem, target, W, at, send):
    T = x.shape[0]
    tab = _rope_table(T)
    h = x
    saved = []
    for l in range(DEPTH):
        after = at(l, "start", h)
        h, s = _layer_fwd(h, mem, W, l, tab, after, at)
        saved.append(s)
    lblk, dh = _loss_head(h, target, "loss_head")
    grads = [None] * DEPTH
    for l in reversed(range(DEPTH)):
        dh, grads[l] = _layer_bwd(dh, mem, W, l, tab, saved[l], send)
    return lblk[0, 0], dh, grads


COL_SHARDED = ("w_in", "ckv_w", "ffn_wi")
GATE_MATS = ("w_rg", "w_ig")


def _shard_rows(shards, l):
    out = []
    for n, r in PACK_ROWS:
        a = shards[n][l].astype(BF16)
        if n in COL_SHARDED:
            a = a.T
        elif n in GATE_MATS:
            a = a.reshape(RNN_BLOCKS * RNN_BLOCK // N_DEV, RNN_BLOCK)
        out.append(a)
    return out


def _full_weight(G, name):
    if name in GATE_MATS:
        return jnp.transpose(G.reshape(N_DEV, RNN_BLOCKS, RNN_BLOCK // N_DEV, RNN_BLOCK), (1, 0, 2, 3)).reshape(
            RNN_BLOCKS, RNN_BLOCK, RNN_BLOCK)
    return G.reshape(N_DEV * G.shape[1], G.shape[2])


SHARD_ROWS = dict(PACK_ROWS)


def _owner_blocks(g, names):
    out = []
    for name in names:
        a = g[name]
        if name in GATE_MATS:
            a = jnp.transpose(a.astype(BF16).reshape(RNN_BLOCKS, N_DEV, RNN_BLOCK // N_DEV, RNN_BLOCK), (1, 0, 2, 3))
        out.append(a.reshape(N_DEV, SHARD_ROWS[name], D_MODEL))
    return out


def _pack_small(g):
    rows = [g["conv_w"]]
    for nme in SMALL_NAMES:
        a = g[nme]
        if nme == "sinks":
            a = jnp.pad(a, ((0, 0), (0, D_MODEL - N_Q_HEADS)))
        rows.append(a)
    rows.append(jnp.zeros((SMALL_ROWS - CONV_WIDTH - len(SMALL_NAMES), D_MODEL), F32))
    return jnp.concatenate(rows, axis=0)


_SHARD_SHAPES = {"w_in": (1024, 672), "w_br_rnn": (128, 1024), "w_br_attn": (128, 1024), "w_out": (128, 1024),
                 "cq_w": (128, 1024), "ckv_w": (1024, 256), "co_w": (128, 1024), "ffn_wi": (1024, 704),
                 "ffn_wo": (352, 1024), "w_rg": (4, 32, 256), "w_ig": (4, 32, 256)}

LAYER0_GROUPS = (("w_in", "w_rg", "w_ig"), ("w_br_rnn", "w_br_attn", "w_out"),
                 ("cq_w", "ckv_w", "co_w", "ffn_wi", "ffn_wo"))
LAYER0_FORWARD_AT = {"proj": 1, "mix": 2}
LAYER0_WAIT_AT = {"attn": 1, "ln1": 2}
NEXT_LAYER_FORWARD_AT = ("ffn", "ln2", "ln2", None)

WEIGHT_NAMES = ("w_in", "conv_w", "conv_b", "w_rg", "b_rg", "w_ig", "b_ig", "lru_lambda", "w_br_rnn", "w_br_attn",
                "sinks", "w_out", "ln1_g", "ln1_b", "cq_w", "ckv_w", "co_w", "ln2_g", "ln2_b", "ffn_wi", "ffn_wo",
                "ln3_g", "ln3_b")


def kernel(x, mem, w_in, conv_w, conv_b, w_rg, b_rg, w_ig, b_ig, lru_lambda, w_br_rnn, w_br_attn, sinks, w_out, ln1_g, ln1_b, cq_w, ckv_w, co_w, ln2_g, ln2_b, ffn_wi, ffn_wo, ln3_g, ln3_b, loss_target, m_w_in, m_conv_w, m_conv_b, m_w_rg, m_b_rg, m_w_ig, m_b_ig, m_lru_lambda, m_w_br_rnn, m_w_br_attn, m_sinks, m_w_out, m_ln1_g, m_ln1_b, m_cq_w, m_ckv_w, m_co_w, m_ln2_g, m_ln2_b, m_ffn_wi, m_ffn_wo, m_ln3_g, m_ln3_b, v_w_in, v_conv_w, v_conv_b, v_w_rg, v_b_rg, v_w_ig, v_b_ig, v_lru_lambda, v_w_br_rnn, v_w_br_attn, v_sinks, v_w_out, v_ln1_g, v_ln1_b, v_cq_w, v_ckv_w, v_co_w, v_ln2_g, v_ln2_b, v_ffn_wi, v_ffn_wo, v_ln3_g, v_ln3_b):
    w = dict(w_in=w_in, conv_w=conv_w, conv_b=conv_b, w_rg=w_rg, b_rg=b_rg, w_ig=w_ig, b_ig=b_ig,
             lru_lambda=lru_lambda, w_br_rnn=w_br_rnn, w_br_attn=w_br_attn, sinks=sinks, w_out=w_out, ln1_g=ln1_g,
             ln1_b=ln1_b, cq_w=cq_w, ckv_w=ckv_w, co_w=co_w, ln2_g=ln2_g, ln2_b=ln2_b, ffn_wi=ffn_wi, ffn_wo=ffn_wo,
             ln3_g=ln3_g, ln3_b=ln3_b)
    m = dict(w_in=m_w_in, conv_w=m_conv_w, conv_b=m_conv_b, w_rg=m_w_rg, b_rg=m_b_rg, w_ig=m_w_ig, b_ig=m_b_ig,
             lru_lambda=m_lru_lambda, w_br_rnn=m_w_br_rnn, w_br_attn=m_w_br_attn, sinks=m_sinks, w_out=m_w_out,
             ln1_g=m_ln1_g, ln1_b=m_ln1_b, cq_w=m_cq_w, ckv_w=m_ckv_w, co_w=m_co_w, ln2_g=m_ln2_g, ln2_b=m_ln2_b,
             ffn_wi=m_ffn_wi, ffn_wo=m_ffn_wo, ln3_g=m_ln3_g, ln3_b=m_ln3_b)
    v = dict(w_in=v_w_in, conv_w=v_conv_w, conv_b=v_conv_b, w_rg=v_w_rg, b_rg=v_b_rg, w_ig=v_w_ig, b_ig=v_b_ig,
             lru_lambda=v_lru_lambda, w_br_rnn=v_w_br_rnn, w_br_attn=v_w_br_attn, sinks=v_sinks, w_out=v_w_out,
             ln1_g=v_ln1_g, ln1_b=v_ln1_b, cq_w=v_cq_w, ckv_w=v_ckv_w, co_w=v_co_w, ln2_g=v_ln2_g, ln2_b=v_ln2_b,
             ffn_wi=v_ffn_wi, ffn_wo=v_ffn_wo, ln3_g=v_ln3_g, ln3_b=v_ln3_b)
    my_dev = 4 * lax.axis_index("x") + 2 * lax.axis_index("y") + lax.axis_index("c")

    W = {n: [None] * DEPTH for n, _ in PACK_ROWS}
    shards0 = dict(zip([n for n, _ in PACK_ROWS], _shard_rows(w, 0)))
    gathered = _all_gather_many([shards0[n] for n in LAYER0_GROUPS[0]], "l0_gather_first")
    for n, G in zip(LAYER0_GROUPS[0], gathered):
        W[n][0] = _full_weight(G, n)
    conv_all = _all_gather(conv_w.reshape(DEPTH * CONV_WIDTH, D_MODEL // N_DEV), "gather_conv")
    W["conv_w"] = jnp.transpose(conv_all, (1, 0, 2)).reshape(DEPTH, CONV_WIDTH, D_MODEL)
    for n in SMALL_NAMES:
        W[n] = w[n]
    flying = {}
    token = conv_all
    groups = [((0, gi), LAYER0_GROUPS[gi], [shards0[n] for n in LAYER0_GROUPS[gi]]) for gi in (1, 2)]
    groups += [((l, 0), [n for n, _ in PACK_ROWS], _shard_rows(w, l)) for l in range(1, DEPTH)]
    for key, names, srcs in groups:
        lands = [lax.empty((N_DEV,) + a.shape, a.dtype) for a in srcs]
        flying[key] = (names,) + _gather_start(srcs, lands, f"l{key[0]}_gather_start{key[1]}", token)
        token = flying[key][5]
    first_token = token

    def forward(key, after):
        names, send1, recv1, srcs, lands, _ = flying[key]
        flying[key] = (names,) + _gather_forward(send1, recv1, srcs, lands, after, f"l{key[0]}_gather_forward{key[1]}")
        return flying[key][5]

    def arrive(key, after):
        names, send2, recv2, srcs, lands, _ = flying.pop(key)
        srcs, lands = _gather_wait(send2, recv2, srcs, lands, after, f"l{key[0]}_gather_wait{key[1]}")
        for n, mine, G in zip(names, srcs, lands):
            W[n][key[0]] = _full_weight(lax.dynamic_update_index_in_dim(G, mine, my_dev, 0), n)

    def at(l, point, x):
        if point == "start":
            if l == 0:
                return first_token
            arrive((l, 0), x)
        elif l == 0 and point in LAYER0_FORWARD_AT:
            return forward((0, LAYER0_FORWARD_AT[point]), x)
        elif l == 0 and point in LAYER0_WAIT_AT:
            arrive((0, LAYER0_WAIT_AT[point]), x)
        elif point == NEXT_LAYER_FORWARD_AT[l]:
            return forward((l + 1, 0), x)
        return None

    summed = {}
    sent = {}
    me = my_dev.astype(jnp.int32).reshape(1)

    def finish(l, part, after):
        ssem, rsem, blocks, lands = sent.pop((l, part))
        blocks, recvs = _split_wait(ssem, rsem, blocks, lands, after, f"l{l}_exchange_wait{part}")
        for n, s in zip(GRAD_PARTS[part], _sum_blocks(recvs, blocks, me, f"l{l}_sum_grads{part}")):
            summed[(l, n)] = s

    def send(l, part, g):
        blocks = _owner_blocks(g, GRAD_PARTS[part])
        if (l + 1, part) in sent:
            finish(l + 1, part, blocks[0])
        ssem, rsem, blocks, lands, token = _split_start(blocks, [lax.empty(b.shape, b.dtype) for b in blocks],
                                                        f"l{l}_exchange_start{part}")
        sent[(l, part)] = (ssem, rsem, blocks, lands)
        return token

    loss_local, dx, layer_grads = _local_step(x[0], mem[0], loss_target[0], W, at, send)
    for part in range(len(GRAD_PARTS)):
        finish(0, part, dx)
    loss = lax.psum(loss_local, MESH_AXES)

    small_all = _all_gather(jnp.concatenate([_pack_small(g) for g in layer_grads], axis=0), "gather_small_grads")
    small_sum = _sum_slots(small_all, "sum_small_grads").reshape(DEPTH, SMALL_ROWS, D_MODEL)

    grads = {}
    for n, _ in PACK_ROWS:
        blk = jnp.stack([summed[(l, n)] for l in range(DEPTH)])
        grads[n] = jnp.transpose(blk, (0, 2, 1)) if n in COL_SHARDED else blk.reshape((DEPTH,) + _SHARD_SHAPES[n])
    conv_full = small_sum[:, :CONV_WIDTH, :]
    grads["conv_w"] = lax.dynamic_slice_in_dim(conv_full, my_dev * (D_MODEL // N_DEV), D_MODEL // N_DEV, axis=2)
    for i, n in enumerate(SMALL_NAMES):
        row = small_sum[:, CONV_WIDTH + i, :]
        grads[n] = row[:, :N_Q_HEADS] if n == "sinks" else row

    deltas, new_m, new_v = {}, {}, {}
    for n in WEIGHT_NAMES:
        deltas[n], new_m[n], new_v[n] = _adamw(w[n], grads[n], m[n], v[n], "adamw_" + n)

    return (loss, dx[None], *[grads[n] for n in WEIGHT_NAMES], *[deltas[n] for n in WEIGHT_NAMES],
            *[new_m[n] for n in WEIGHT_NAMES], *[new_v[n] for n in WEIGHT_NAMES])
```

```python
import functools
import math

import jax
import jax.numpy as jnp
from jax import lax
from jax.experimental import pallas as pl
from jax.experimental.pallas import tpu as pltpu

F32 = jnp.float32
BF16 = jnp.bfloat16

D_MODEL = 1024
DEPTH = 4
N_DEV = 8
RNN_BLOCKS = 4
RNN_BLOCK = 256
CONV_WIDTH = 4
LRU_C = 8.0
HEAD_DIM = 64
N_Q_HEADS = 16
N_KV_HEADS = 2
GROUP = 8
KV_WIDTH = 128
BLOCK = 128
ROPE_THETA = 500000.0
ROT_DIM = 16
IN_COLS = 5376
CROSS_HEADS = 4
CROSS_HEAD_DIM = 256
D_FF = 2816
LN_EPS = 1e-5
ALPHA = (2 * DEPTH) ** 0.25
NEG_INF = -1e30

ADAM_LR = 0.001
ADAM_B1 = 0.9
ADAM_B2 = 0.999
ADAM_EPS = 1e-08
ADAM_WD = 0.01
ADAM_STEP = 10

C_XR, C_GR, C_Q, C_K, C_V, C_GRNN, C_GATTN = 0, 1024, 2048, 3072, 3200, 3328, 4352

TIME_CHUNK = 256
ROW_TILE = 256

MESH_AXES = ("x", "y", "c")

PACK_ROWS = (("w_in", 672), ("w_br_rnn", 128), ("w_br_attn", 128), ("w_out", 128), ("cq_w", 128),
             ("ckv_w", 256), ("co_w", 128), ("ffn_wi", 704), ("ffn_wo", 352), ("w_rg", 32), ("w_ig", 32))
SMALL_NAMES = ("conv_b", "b_rg", "b_ig", "lru_lambda", "sinks", "ln1_g", "ln1_b", "ln2_g", "ln2_b", "ln3_g", "ln3_b")
SMALL_ROWS = 16


def _pick(dim, cands):
    for c in cands:
        if dim % c == 0:
            return c
    return dim


_DIMS = {"nn": (((1,), (0,)), ((), ())), "nt": (((1,), (1,)), ((), ())), "tn": (((0,), (0,)), ((), ()))}

MATMUL_VMEM_BUDGET = 44 * 2 ** 20
MATMUL_MAX_TILE = 2048
MXU_DIM = 256
STEP_COST_BYTES = 500_000
MIN_ROW_TILE = 512


def _tile_candidates(dim, whole=False, step=MXU_DIM):
    c = [d for d in range(step, min(dim, MATMUL_MAX_TILE) + 1, step) if dim % d == 0]
    if whole and dim not in c:
        c.append(dim)
    return c or [dim]


def _matmul_tiles(M, N, K, sa, sb, so, has_add):
    best = None
    for tk in _tile_candidates(K, whole=True):
        nk = K // tk
        for tm in [t for t in _tile_candidates(M, step=128) if t >= min(M, MIN_ROW_TILE)]:
            for tn in _tile_candidates(N):
                vmem = 2 * (tm * tk * sa + tk * tn * sb + tm * tn * so) + tm * tn * 4
                vmem += tm * tn * 4 if nk > 1 else 0
                vmem += 2 * tm * tn * 4 if has_add else 0
                vmem += (tm * tk * 2 if sa == 4 else 0) + (tk * tn * 2 if sb == 4 else 0)
                if vmem > MATMUL_VMEM_BUDGET:
                    continue
                steps = (M // tm) * (N // tn) * nk
                exposed = tm * tk * sa + tk * tn * sb + tm * tn * so
                acc_moves = steps * tm * tn * 2 if nk > 1 else 0
                fixed = M * N * so + steps * STEP_COST_BYTES + exposed + acc_moves
                a_in = M * K * sa * ((N // tn) if nk > 1 else 1) + K * N * sb * (M // tm)
                b_in = M * K * sa * (N // tn) + K * N * sb * ((M // tm) if nk > 1 else 1)
                for cost, m_outer in ((a_in + fixed, True), (b_in + fixed, False)):
                    if best is None or cost < best[0]:
                        best = (cost, tm, tn, tk, m_outer)
    return best[1:]


def _matmul(a, b, mode, name, add=None, add_scale=1.0, out_dtype=F32, after=None):
    if mode == "nn":
        (M, K), (_, N) = a.shape, b.shape
    elif mode == "nt":
        (M, K), (N, _) = a.shape, b.shape
    else:
        (K, M), (_, N) = a.shape, b.shape
    tm, tn, tk, m_outer = _matmul_tiles(M, N, K, a.dtype.itemsize, b.dtype.itemsize, jnp.dtype(out_dtype).itemsize,
                                        add is not None)
    nk = K // tk
    dims = _DIMS[mode]

    def body(*refs):
        if after is not None:
            refs = refs[:-2 - (nk > 1)] + refs[-1 - (nk > 1):]
        a_ref, b_ref = refs[0], refs[1]
        c_ref = refs[2] if add is not None else None
        o_ref = refs[3] if add is not None else refs[2]

        def finish(r):
            if add is not None:
                r = r + add_scale * c_ref[...]
            o_ref[...] = r.astype(out_dtype)

        prod = lax.dot_general(a_ref[...].astype(BF16), b_ref[...].astype(BF16), dims, preferred_element_type=F32)
        if nk == 1:
            finish(prod)
            return
        acc_ref = refs[-1]
        k = pl.program_id(2)

        @pl.when(k == 0)
        def _():
            acc_ref[...] = prod

        @pl.when(k > 0)
        def _():
            acc_ref[...] += prod

        @pl.when(k == nk - 1)
        def _():
            finish(acc_ref[...])

    ij = (lambda p, q: (p, q)) if m_outer else (lambda p, q: (q, p))
    if mode == "nn":
        a_spec = pl.BlockSpec((tm, tk), lambda p, q, k: (ij(p, q)[0], k))
        b_spec = pl.BlockSpec((tk, tn), lambda p, q, k: (k, ij(p, q)[1]))
    elif mode == "nt":
        a_spec = pl.BlockSpec((tm, tk), lambda p, q, k: (ij(p, q)[0], k))
        b_spec = pl.BlockSpec((tn, tk), lambda p, q, k: (ij(p, q)[1], k))
    else:
        a_spec = pl.BlockSpec((tk, tm), lambda p, q, k: (k, ij(p, q)[0]))
        b_spec = pl.BlockSpec((tk, tn), lambda p, q, k: (k, ij(p, q)[1]))
    o_spec = pl.BlockSpec((tm, tn), lambda p, q, k: ij(p, q))
    o_shape = jax.ShapeDtypeStruct((M, N), out_dtype)
    in_specs = [a_spec, b_spec]
    args = [a, b]
    if add is not None:
        in_specs.append(o_spec)
        args.append(add)
    if after is not None:
        in_specs.append(pl.BlockSpec(memory_space=pl.ANY))
        args.append(after)
    return pl.pallas_call(
        body, name=name, grid=(M // tm, N // tn, nk) if m_outer else (N // tn, M // tm, nk),
        in_specs=in_specs, out_specs=o_spec, out_shape=o_shape,
        scratch_shapes=[pltpu.VMEM((tm, tn), F32)] if nk > 1 else [],
        compiler_params=pltpu.CompilerParams(dimension_semantics=("parallel", "parallel", "arbitrary")),
    )(*args)


LN_ROWS = 512


def _matmul_ln(a, w, h, g, b, name):
    T, K = a.shape
    D = w.shape[1]
    tr = _pick(T, (LN_ROWS, 256, 128, 64, 32, 16, 8))

    def body(a_ref, w_ref, h_ref, g_ref, b_ref, o_ref, xh_ref, rs_ref):
        f = jnp.dot(a_ref[...].astype(BF16), w_ref[...], preferred_element_type=F32)
        z = ALPHA * h_ref[...] + f
        mu = jnp.mean(z, axis=-1, keepdims=True)
        zc = z - mu
        var = jnp.mean(zc * zc, axis=-1, keepdims=True)
        rstd = lax.rsqrt(var + LN_EPS)
        xh = zc * rstd
        xh_ref[...] = xh
        rs_ref[...] = rstd
        o_ref[...] = xh * g_ref[...] + b_ref[...]

    row = pl.BlockSpec((tr, D), lambda i: (i, 0))
    vec = pl.BlockSpec((1, D), lambda i: (0, 0))
    return pl.pallas_call(
        body, name=name, grid=(T // tr,),
        in_specs=[pl.BlockSpec((tr, K), lambda i: (i, 0)), pl.BlockSpec((K, D), lambda i: (0, 0)), row, vec, vec],
        out_specs=[row, row, pl.BlockSpec((tr, 1), lambda i: (i, 0))],
        out_shape=[jax.ShapeDtypeStruct((T, D), F32), jax.ShapeDtypeStruct((T, D), F32),
                   jax.ShapeDtypeStruct((T, 1), F32)],
        compiler_params=pltpu.CompilerParams(dimension_semantics=("parallel",)),
    )(a, w, h, g, b)


def _ln_bwd(dout, xh, rstd, g, name, after=None):
    T, D = dout.shape
    tr = _pick(T, (LN_ROWS, 256, 128, 64, 32, 16, 8))

    def body(do_ref, xh_ref, rs_ref, g_ref, *rest):
        dz_ref, dg_ref, db_ref = rest[-3:]

        @pl.when(pl.program_id(0) == 0)
        def _():
            dg_ref[...] = jnp.zeros_like(dg_ref)
            db_ref[...] = jnp.zeros_like(db_ref)

        do = do_ref[...]
        xh = xh_ref[...]
        dxh = do * g_ref[...]
        m1 = jnp.mean(dxh, axis=-1, keepdims=True)
        m2 = jnp.mean(dxh * xh, axis=-1, keepdims=True)
        dz_ref[...] = rs_ref[...] * (dxh - m1 - xh * m2)
        dg_ref[...] += jnp.sum(do * xh, axis=0, keepdims=True)
        db_ref[...] += jnp.sum(do, axis=0, keepdims=True)

    row = pl.BlockSpec((tr, D), lambda i: (i, 0))
    vec = pl.BlockSpec((1, D), lambda i: (0, 0))
    in_specs = [row, row, pl.BlockSpec((tr, 1), lambda i: (i, 0)), vec]
    args = [dout, xh, rstd, g]
    if after is not None:
        in_specs.append(pl.BlockSpec(memory_space=pl.ANY))
        args.append(after)
    return pl.pallas_call(
        body, name=name, grid=(T // tr,),
        in_specs=in_specs, out_specs=[row, vec, vec],
        out_shape=[jax.ShapeDtypeStruct((T, D), F32), jax.ShapeDtypeStruct((1, D), F32),
                   jax.ShapeDtypeStruct((1, D), F32)],
        compiler_params=pltpu.CompilerParams(dimension_semantics=("arbitrary",)),
    )(*args)


_GELU_C = math.sqrt(2.0 / math.pi)


def _gelu(x):
    t = jnp.tanh(_GELU_C * (x + 0.044715 * x * x * x))
    return 0.5 * x * (1.0 + t), t


def _gelu_grad(x, t):
    return 0.5 * (1.0 + t) + 0.5 * x * (1.0 - t * t) * _GELU_C * (1.0 + 3 * 0.044715 * x * x)


def _sigmoid(x):
    return 1.0 / (1.0 + jnp.exp(-x))


def _softplus_neg(lam):
    z = jnp.exp(-jnp.abs(lam))
    u = 1.0 + z
    l1p = jnp.where(u == 1.0, z, jnp.log(u) * z / jnp.where(u == 1.0, 1.0, u - 1.0))
    return jnp.maximum(-lam, 0.0) + l1p


def _neg_expm1(x):
    series = x * (1.0 + x * 0.5 * (1.0 + x * (1.0 / 3.0) * (1.0 + x * 0.25 * (1.0 + x * 0.2))))
    return -jnp.where(x > -0.05, series, jnp.exp(x) - 1.0)


def _scan_fwd(a, b):
    n = a.shape[0]
    rows = lax.broadcasted_iota(jnp.int32, a.shape, 0)
    s = 1
    while s < n:
        keep = rows >= s
        b = jnp.where(keep, a * pltpu.roll(b, s, 0) + b, b)
        a = jnp.where(keep, a * pltpu.roll(a, s, 0), a)
        s *= 2
    return a, b


def _scan_bwd(c, b):
    n = c.shape[0]
    rows = lax.broadcasted_iota(jnp.int32, c.shape, 0)
    s = 1
    while s < n:
        keep = rows < n - s
        b = jnp.where(keep, c * pltpu.roll(b, n - s, 0) + b, b)
        c = jnp.where(keep, c * pltpu.roll(c, n - s, 0), c)
        s *= 2
    return c, b


def _rnn_gates(xc, wr, br, wi, bi, sp):
    xb = xc.astype(BF16)
    r = _sigmoid(jnp.dot(xb, wr, preferred_element_type=F32) + br)
    i = _sigmoid(jnp.dot(xb, wi, preferred_element_type=F32) + bi)
    la = -LRU_C * r * sp
    a = jnp.exp(la)
    om = _neg_expm1(2.0 * la)
    mult = jnp.sqrt(om)
    return r, i, a, om, mult


def _rnn_specs(T):
    C = RNN_BLOCK
    col = lambda off: pl.BlockSpec((T, C), lambda n, off=off: (0, off // C + n))
    vec = pl.BlockSpec((1, C), lambda n: (0, n))
    cw = pl.BlockSpec((CONV_WIDTH, C), lambda n: (0, n))
    w = pl.BlockSpec((1, C, C), lambda n: (n, 0, 0))
    own = pl.BlockSpec((T, C), lambda n: (0, n))
    return col, vec, cw, w, own


def _rnn_fwd(P, cw, cb, wrg, brg, wig, big, lam, name):
    T = P.shape[0]
    C = RNN_BLOCK
    tc = _pick(T, (TIME_CHUNK,))
    nch = T // tc

    def body(x_ref, g_ref, cw_ref, cb_ref, wr_ref, br_ref, wi_ref, bi_ref, lam_ref, y_ref, hs_ref, xs_ref):
        sp = _softplus_neg(lam_ref[...])
        wr = wr_ref[0]
        wi = wi_ref[0]
        xs_ref[0:8, :] = jnp.zeros((8, C), F32)

        def chunk(c, hprev):
            r0 = pl.multiple_of(c * tc, tc)
            x = x_ref[pl.ds(r0, tc), :].astype(F32)
            xs_ref[8:, :] = x
            xc = cb_ref[...] + jnp.zeros((tc, C), F32)
            for k in range(CONV_WIDTH):
                xc = xc + xs_ref[pl.ds(8 - (CONV_WIDTH - 1 - k), tc), :] * cw_ref[k:k + 1, :]
            xs_ref[0:8, :] = x[tc - 8:, :]
            r, i, a, om, mult = _rnn_gates(xc, wr, br_ref[...], wi, bi_ref[...], sp)
            acum, bcum = _scan_fwd(a, mult * (i * xc))
            h = acum * hprev + bcum
            hs_ref[pl.ds(r0, tc), :] = h
            ge, _ = _gelu(g_ref[pl.ds(r0, tc), :].astype(F32))
            y_ref[pl.ds(r0, tc), :] = (h * ge).astype(BF16)
            return h[tc - 1:tc, :]

        lax.fori_loop(0, nch, chunk, jnp.zeros((1, C), F32))

    col, vec, cwspec, w, own = _rnn_specs(T)
    return pl.pallas_call(
        body, name=name, grid=(RNN_BLOCKS,),
        in_specs=[col(C_XR), col(C_GR), cwspec, vec, w, vec, w, vec, vec],
        out_specs=[own, own],
        out_shape=[jax.ShapeDtypeStruct((T, D_MODEL), BF16), jax.ShapeDtypeStruct((T, D_MODEL), F32)],
        scratch_shapes=[pltpu.VMEM((tc + 8, C), F32)],
        compiler_params=pltpu.CompilerParams(dimension_semantics=("parallel",)),
    )(P, P, cw, cb, wrg, brg, wig, big, lam)


def _rnn_bwd(P, hs, dy, cw, cb, wrg, brg, wig, big, lam, name):
    T = P.shape[0]
    C = RNN_BLOCK
    tc = _pick(T, (TIME_CHUNK,))
    nch = T // tc

    def body(x_ref, g_ref, hs_ref, dy_ref, cw_ref, cb_ref, wr_ref, br_ref, wi_ref, bi_ref, lam_ref,
             dx_ref, dg_ref, dcw_ref, dcb_ref, dwr_ref, dbr_ref, dwi_ref, dbi_ref, dlam_ref,
             xs_ref, hp_ref, an_ref, dn_ref):
        lam_v = lam_ref[...]
        sp = _softplus_neg(lam_v)
        wr = wr_ref[0]
        wi = wi_ref[0]
        dcw_ref[...] = jnp.zeros_like(dcw_ref)
        dcb_ref[...] = jnp.zeros_like(dcb_ref)
        dwr_ref[...] = jnp.zeros_like(dwr_ref)
        dbr_ref[...] = jnp.zeros_like(dbr_ref)
        dwi_ref[...] = jnp.zeros_like(dwi_ref)
        dbi_ref[...] = jnp.zeros_like(dbi_ref)
        dlam_ref[...] = jnp.zeros_like(dlam_ref)
        an_ref[tc:, :] = jnp.zeros((8, C), F32)
        dn_ref[tc:, :] = jnp.zeros((8, C), F32)

        def chunk(step, gnext):
            c = nch - 1 - step
            r0 = pl.multiple_of(c * tc, tc)
            p0 = pl.multiple_of(jnp.maximum(r0 - 8, 0), 8)
            q0 = pl.multiple_of(jnp.maximum(r0 - 16, 0), 16)
            live = c > 0
            x = x_ref[pl.ds(r0, tc), :].astype(F32)
            xs_ref[0:8, :] = jnp.where(live, x_ref[pl.ds(q0, 16), :].astype(F32)[8:, :], 0.0)
            xs_ref[8:, :] = x
            xsh = [xs_ref[pl.ds(8 - (CONV_WIDTH - 1 - k), tc), :] for k in range(CONV_WIDTH)]
            xc = cb_ref[...] + jnp.zeros((tc, C), F32)
            for k in range(CONV_WIDTH):
                xc = xc + xsh[k] * cw_ref[k:k + 1, :]
            r, i, a, om, mult = _rnn_gates(xc, wr, br_ref[...], wi, bi_ref[...], sp)
            h = hs_ref[pl.ds(r0, tc), :]
            hp_ref[0:8, :] = jnp.where(live, hs_ref[pl.ds(p0, 8), :], 0.0)
            hp_ref[8:, :] = h
            hm1 = hp_ref[pl.ds(7, tc), :]
            g = g_ref[pl.ds(r0, tc), :].astype(F32)
            ge, th = _gelu(g)
            dy = dy_ref[pl.ds(r0, tc), :]
            dg_ref[pl.ds(r0, tc), :] = (dy * h * _gelu_grad(g, th)).astype(BF16)
            an_ref[0:tc, :] = a
            coef = an_ref[pl.ds(1, tc), :]
            ccum, bcum = _scan_bwd(coef, dy * ge)
            G = bcum + ccum * gnext
            an_ref[tc:, :] = a[0:8, :]
            da = G * hm1
            ixc = i * xc
            dmult = G * ixc
            di = G * mult * xc
            dxc = G * mult * i
            dla = da * a - dmult * (1.0 - om) / mult
            dr = dla * (-LRU_C * sp)
            dlam_ref[...] += jnp.sum(dla * r, axis=0, keepdims=True)
            dzr = dr * r * (1.0 - r)
            dzi = di * i * (1.0 - i)
            dbr_ref[...] += jnp.sum(dzr, axis=0, keepdims=True)
            dbi_ref[...] += jnp.sum(dzi, axis=0, keepdims=True)
            xb = xc.astype(BF16)
            dzrb = dzr.astype(BF16)
            dzib = dzi.astype(BF16)
            dwr_ref[0] += lax.dot_general(xb, dzrb, _DIMS["tn"], preferred_element_type=F32)
            dwi_ref[0] += lax.dot_general(xb, dzib, _DIMS["tn"], preferred_element_type=F32)
            dxc = dxc + lax.dot_general(dzrb, wr, _DIMS["nt"], preferred_element_type=F32)
            dxc = dxc + lax.dot_general(dzib, wi, _DIMS["nt"], preferred_element_type=F32)
            dcb_ref[...] += jnp.sum(dxc, axis=0, keepdims=True)
            for k in range(CONV_WIDTH):
                dcw_ref[k:k + 1, :] += jnp.sum(dxc * xsh[k], axis=0, keepdims=True)
            dn_ref[0:tc, :] = dxc
            dx = jnp.zeros((tc, C), F32)
            for k in range(CONV_WIDTH):
                dx = dx + dn_ref[pl.ds(CONV_WIDTH - 1 - k, tc), :] * cw_ref[k:k + 1, :]
            dn_ref[tc:, :] = dxc[0:8, :]
            dx_ref[pl.ds(r0, tc), :] = dx.astype(BF16)
            return G[0:1, :]

        lax.fori_loop(0, nch, chunk, jnp.zeros((1, C), F32))
        dlam_ref[...] = dlam_ref[...] * (LRU_C * _sigmoid(-lam_v))

    col, vec, cwspec, w, own = _rnn_specs(T)
    vshape = jax.ShapeDtypeStruct((1, D_MODEL), F32)
    wshape = jax.ShapeDtypeStruct((RNN_BLOCKS, C, C), F32)
    return pl.pallas_call(
        body, name=name, grid=(RNN_BLOCKS,),
        in_specs=[col(C_XR), col(C_GR), own, own, cwspec, vec, w, vec, w, vec, vec],
        out_specs=[own, own, cwspec, vec, w, vec, w, vec, vec],
        out_shape=[jax.ShapeDtypeStruct((T, D_MODEL), BF16), jax.ShapeDtypeStruct((T, D_MODEL), BF16),
                   jax.ShapeDtypeStruct((CONV_WIDTH, D_MODEL), F32), vshape, wshape, vshape, wshape, vshape, vshape],
        scratch_shapes=[pltpu.VMEM((tc + 8, C), F32), pltpu.VMEM((tc + 8, C), F32),
                        pltpu.VMEM((tc + 8, C), F32), pltpu.VMEM((tc + 8, C), F32)],
        compiler_params=pltpu.CompilerParams(dimension_semantics=("parallel",)),
    )(P, P, hs, dy, cw, cb, wrg, brg, wig, big, lam)


def _rope_table(T):
    half = ROT_DIM // 2
    pos = jnp.arange(T, dtype=F32)
    inv_freq = ROPE_THETA ** (-jnp.arange(0, ROT_DIM, 2, dtype=F32) / ROT_DIM)
    ang = pos[:, None] * inv_freq[None, :]
    cos, sin = jnp.cos(ang), jnp.sin(ang)
    one = jnp.ones((T, HEAD_DIM - ROT_DIM), F32)
    zero = jnp.zeros((T, HEAD_DIM - ROT_DIM), F32)
    z8 = jnp.zeros((T, half), F32)
    c = jnp.concatenate([cos, cos, one], axis=1)
    a = jnp.concatenate([-sin, z8, zero], axis=1)
    b = jnp.concatenate([z8, sin, zero], axis=1)
    return jnp.stack([jnp.tile(c, (1, 2)), jnp.tile(a, (1, 2)), jnp.tile(b, (1, 2))])


def _rope(x, tab, sign):
    W = x.shape[1]
    rep = W // 128
    c = jnp.tile(tab[0], (1, rep)) if rep > 1 else tab[0]
    a = jnp.tile(tab[1], (1, rep)) if rep > 1 else tab[1]
    b = jnp.tile(tab[2], (1, rep)) if rep > 1 else tab[2]
    return x * c + sign * (pltpu.roll(x, W - ROT_DIM // 2, 1) * a + pltpu.roll(x, ROT_DIM // 2, 1) * b)


def _swa_mask(n):
    rows = lax.broadcasted_iota(jnp.int32, (GROUP * BLOCK, 2 * BLOCK), 0) & (BLOCK - 1)
    cols = lax.broadcasted_iota(jnp.int32, (GROUP * BLOCK, 2 * BLOCK), 1)
    return (cols > rows) & (cols <= rows + BLOCK) & ((n > 0) | (cols >= BLOCK))


def _swa_probs(qg, k2, sink, valid):
    s = lax.dot_general(qg, k2, _DIMS["nt"], preferred_element_type=F32) * (HEAD_DIM ** -0.5)
    s = jnp.where(valid, s, NEG_INF)
    m = jnp.maximum(jnp.max(s, axis=1, keepdims=True), sink)
    p = jnp.exp(s - m)
    ps = jnp.exp(sink - m)
    inv = 1.0 / (jnp.sum(p, axis=1, keepdims=True) + ps)
    return p * inv, ps * inv


def _swa_specs(T):
    nb = T // BLOCK
    qspec = pl.BlockSpec((BLOCK, D_MODEL), lambda n: (n, C_Q // D_MODEL))
    cur = lambda off: pl.BlockSpec((BLOCK, KV_WIDTH), lambda n, off=off: (n, off // KV_WIDTH))
    prev = lambda off: pl.BlockSpec((BLOCK, KV_WIDTH), lambda n, off=off: (jnp.maximum(n - 1, 0), off // KV_WIDTH))
    tcur = pl.BlockSpec((3, BLOCK, 128), lambda n: (0, n, 0))
    tprev = pl.BlockSpec((3, BLOCK, 128), lambda n: (0, jnp.maximum(n - 1, 0), 0))
    sink = pl.BlockSpec((N_KV_HEADS, GROUP * BLOCK, 1), lambda n: (0, 0, 0))
    own = pl.BlockSpec((BLOCK, D_MODEL), lambda n: (n, 0))
    return nb, qspec, cur, prev, tcur, tprev, sink, own


def _stack_heads(x, hk):
    return jnp.concatenate([x[:, (hk * GROUP + g) * HEAD_DIM:(hk * GROUP + g + 1) * HEAD_DIM] for g in range(GROUP)],
                           axis=0)


def _swa_fwd(P, tab, sink_col, name):
    T = P.shape[0]
    nb, qspec, cur, prev, tcur, tprev, sink, own = _swa_specs(T)

    def body(q_ref, kc_ref, kp_ref, vc_ref, vp_ref, tc_ref, tp_ref, sk_ref, o_ref):
        n = pl.program_id(0)
        valid = _swa_mask(n)
        q = _rope(q_ref[...].astype(F32), tc_ref[...], 1.0).astype(BF16)
        k2 = jnp.concatenate([_rope(kp_ref[...].astype(F32), tp_ref[...], 1.0),
                              _rope(kc_ref[...].astype(F32), tc_ref[...], 1.0)], axis=0).astype(BF16)
        v2 = jnp.concatenate([vp_ref[...], vc_ref[...]], axis=0).astype(BF16)
        parts = []
        for hk in range(N_KV_HEADS):
            sl = slice(hk * HEAD_DIM, (hk + 1) * HEAD_DIM)
            pn, _ = _swa_probs(_stack_heads(q, hk), k2[:, sl], sk_ref[hk], valid)
            og = jnp.dot(pn.astype(BF16), v2[:, sl], preferred_element_type=F32)
            parts += [og[g * BLOCK:(g + 1) * BLOCK, :] for g in range(GROUP)]
        o_ref[...] = jnp.concatenate(parts, axis=1).astype(BF16)

    return pl.pallas_call(
        body, name=name, grid=(nb,),
        in_specs=[qspec, cur(C_K), prev(C_K), cur(C_V), prev(C_V), tcur, tprev, sink],
        out_specs=own, out_shape=jax.ShapeDtypeStruct((T, D_MODEL), BF16),
        compiler_params=pltpu.CompilerParams(dimension_semantics=("parallel",)),
    )(P, P, P, P, P, tab, tab, sink_col)


def _swa_bwd(P, do, tab, sink_col, name):
    T = P.shape[0]
    nb, qspec, cur, prev, tcur, tprev, sink, own = _swa_specs(T)

    def body(q_ref, kc_ref, kp_ref, vc_ref, vp_ref, do_ref, tc_ref, tp_ref, sk_ref,
             dq_ref, dk_ref, dv_ref, ds_ref):
        n = pl.program_id(0)

        @pl.when(n == 0)
        def _():
            dk_ref[...] = jnp.zeros_like(dk_ref)
            dv_ref[...] = jnp.zeros_like(dv_ref)
            ds_ref[...] = jnp.zeros_like(ds_ref)

        valid = _swa_mask(n)
        tcur_v = tc_ref[...]
        tprev_v = tp_ref[...]
        q = _rope(q_ref[...].astype(F32), tcur_v, 1.0).astype(BF16)
        k2 = jnp.concatenate([_rope(kp_ref[...].astype(F32), tprev_v, 1.0),
                              _rope(kc_ref[...].astype(F32), tcur_v, 1.0)], axis=0).astype(BF16)
        v2 = jnp.concatenate([vp_ref[...], vc_ref[...]], axis=0).astype(BF16)
        dob = do_ref[...].astype(BF16)
        dq_parts = []
        dk_parts = []
        dv_parts = []
        for hk in range(N_KV_HEADS):
            sl = slice(hk * HEAD_DIM, (hk + 1) * HEAD_DIM)
            qg = _stack_heads(q, hk)
            dog = _stack_heads(dob, hk)
            pn, psn = _swa_probs(qg, k2[:, sl], sk_ref[hk], valid)
            dp = lax.dot_general(dog, v2[:, sl], _DIMS["nt"], preferred_element_type=F32)
            delta = jnp.sum(pn * dp, axis=1, keepdims=True)
            dsc = (pn * (dp - delta) * (HEAD_DIM ** -0.5)).astype(BF16)
            dsink = -psn * delta
            for g in range(GROUP):
                ds_ref[hk, g:g + 1, :] += jnp.broadcast_to(
                    jnp.sum(dsink[g * BLOCK:(g + 1) * BLOCK], axis=0, keepdims=True), (1, 128))
            dqg = jnp.dot(dsc, k2[:, sl], preferred_element_type=F32)
            dq_parts += [dqg[g * BLOCK:(g + 1) * BLOCK, :] for g in range(GROUP)]
            dk_parts.append(lax.dot_general(dsc, qg, _DIMS["tn"], preferred_element_type=F32))
            dv_parts.append(lax.dot_general(pn.astype(BF16), dog, _DIMS["tn"], preferred_element_type=F32))
        dq_ref[...] = _rope(jnp.concatenate(dq_parts, axis=1), tcur_v, -1.0).astype(BF16)
        dk2 = jnp.concatenate(dk_parts, axis=1)
        dv2 = jnp.concatenate(dv_parts, axis=1)
        c0 = pl.multiple_of(n * BLOCK, BLOCK)
        p0 = pl.multiple_of(jnp.maximum(n - 1, 0) * BLOCK, BLOCK)
        dk_ref[pl.ds(p0, BLOCK), :] += _rope(dk2[:BLOCK], tprev_v, -1.0)
        dv_ref[pl.ds(p0, BLOCK), :] += dv2[:BLOCK]
        dk_ref[pl.ds(c0, BLOCK), :] += _rope(dk2[BLOCK:], tcur_v, -1.0)
        dv_ref[pl.ds(c0, BLOCK), :] += dv2[BLOCK:]

    full = pl.BlockSpec((T, KV_WIDTH), lambda n: (0, 0))
    return pl.pallas_call(
        body, name=name, grid=(nb,),
        in_specs=[qspec, cur(C_K), prev(C_K), cur(C_V), prev(C_V), own, tcur, tprev, sink],
        out_specs=[own, full, full, pl.BlockSpec((N_KV_HEADS, GROUP, 128), lambda n: (0, 0, 0))],
        out_shape=[jax.ShapeDtypeStruct((T, D_MODEL), BF16), jax.ShapeDtypeStruct((T, KV_WIDTH), F32),
                   jax.ShapeDtypeStruct((T, KV_WIDTH), F32), jax.ShapeDtypeStruct((N_KV_HEADS, GROUP, 128), F32)],
        compiler_params=pltpu.CompilerParams(dimension_semantics=("arbitrary",)),
    )(P, P, P, P, P, do, tab, tab, sink_col)


_MW = 256


def _gate_specs(T, rows, width):
    tr = _pick(T, (rows, 256, 128, 64, 32, 16, 8))
    col = lambda off: pl.BlockSpec((tr, width), lambda i, j, off=off: (i, off // width + j))
    own = pl.BlockSpec((tr, width), lambda i, j: (i, j))
    return tr, col, own


def _merge_fwd(P, mr, ma, name):
    T = P.shape[0]
    tr, col, own = _gate_specs(T, 1024, _MW)

    def body(gr_ref, ga_ref, mr_ref, ma_ref, o_ref):
        o_ref[...] = (_sigmoid(gr_ref[...].astype(F32)) * mr_ref[...]
                      + _sigmoid(ga_ref[...].astype(F32)) * ma_ref[...]).astype(BF16)

    return pl.pallas_call(
        body, name=name, grid=(T // tr, D_MODEL // _MW), in_specs=[col(C_GRNN), col(C_GATTN), own, own],
        out_specs=own, out_shape=jax.ShapeDtypeStruct((T, D_MODEL), BF16),
        compiler_params=pltpu.CompilerParams(dimension_semantics=("parallel", "parallel")),
    )(P, P, mr, ma)


def _merge_bwd(P, mr, ma, dm, name):
    T = P.shape[0]
    tr, col, own = _gate_specs(T, 512, _MW)

    def body(gr_ref, ga_ref, mr_ref, ma_ref, dm_ref, dmr_ref, dma_ref, dgr_ref, dga_ref):
        dm = dm_ref[...]
        sr = _sigmoid(gr_ref[...].astype(F32))
        sa = _sigmoid(ga_ref[...].astype(F32))
        dmr_ref[...] = (dm * sr).astype(BF16)
        dma_ref[...] = (dm * sa).astype(BF16)
        dgr_ref[...] = (dm * mr_ref[...] * sr * (1.0 - sr)).astype(BF16)
        dga_ref[...] = (dm * ma_ref[...] * sa * (1.0 - sa)).astype(BF16)

    shp = jax.ShapeDtypeStruct((T, D_MODEL), BF16)
    return pl.pallas_call(
        body, name=name, grid=(T // tr, D_MODEL // _MW), in_specs=[col(C_GRNN), col(C_GATTN), own, own, own],
        out_specs=[own] * 4, out_shape=[shp] * 4,
        compiler_params=pltpu.CompilerParams(dimension_semantics=("parallel", "parallel")),
    )(P, P, mr, ma, dm)


_FFN_ROWS = 128


def _swiglu_fwd(U, name):
    T = U.shape[0]
    tr = _pick(T, (_FFN_ROWS, 64, 32, 16))
    half = lambda j: pl.BlockSpec((tr, D_FF), lambda i, j=j: (i, j))

    def body(g_ref, u_ref, o_ref):
        g = g_ref[...].astype(F32)
        o_ref[...] = (g * _sigmoid(g) * u_ref[...].astype(F32)).astype(BF16)

    return pl.pallas_call(
        body, name=name, grid=(T // tr,), in_specs=[half(0), half(1)],
        out_specs=half(0), out_shape=jax.ShapeDtypeStruct((T, D_FF), BF16),
        compiler_params=pltpu.CompilerParams(dimension_semantics=("parallel",)),
    )(U, U)


def _swiglu_bwd(U, dact, name):
    T = U.shape[0]
    tr = _pick(T, (_FFN_ROWS, 64, 32, 16))
    half = lambda j: pl.BlockSpec((tr, D_FF), lambda i, j=j: (i, j))

    def body(g_ref, u_ref, da_ref, o_ref):
        g = g_ref[...].astype(F32)
        da = da_ref[...].astype(F32)
        s = _sigmoid(g)
        o_ref[:, :D_FF] = (da * u_ref[...].astype(F32) * s * (1.0 + g * (1.0 - s))).astype(BF16)
        o_ref[:, D_FF:] = (da * g * s).astype(BF16)

    return pl.pallas_call(
        body, name=name, grid=(T // tr,), in_specs=[half(0), half(1), half(0)],
        out_specs=pl.BlockSpec((tr, 2 * D_FF), lambda i: (i, 0)),
        out_shape=jax.ShapeDtypeStruct((T, 2 * D_FF), BF16),
        compiler_params=pltpu.CompilerParams(dimension_semantics=("parallel",)),
    )(U, U, dact)


def _cross_probs(qh, kh):
    s = lax.dot_general(qh, kh, _DIMS["nt"], preferred_element_type=F32) * (CROSS_HEAD_DIM ** -0.5)
    p = jnp.exp(s - jnp.max(s, axis=1, keepdims=True))
    return p / jnp.sum(p, axis=1, keepdims=True)


def _cross_fwd(q, kv, name):
    T = q.shape[0]
    M = kv.shape[0]
    tr = _pick(T, (ROW_TILE, 128, 64, 32, 16, 8))
    W = CROSS_HEAD_DIM

    def body(q_ref, kv_ref, o_ref):
        for h in range(CROSS_HEADS):
            qh = q_ref[:, h * W:(h + 1) * W].astype(BF16)
            kh = kv_ref[:, h * W:(h + 1) * W].astype(BF16)
            vh = kv_ref[:, D_MODEL + h * W:D_MODEL + (h + 1) * W].astype(BF16)
            pn = _cross_probs(qh, kh)
            o_ref[:, h * W:(h + 1) * W] = jnp.dot(pn.astype(BF16), vh, preferred_element_type=F32).astype(BF16)

    row = pl.BlockSpec((tr, D_MODEL), lambda i: (i, 0))
    return pl.pallas_call(
        body, name=name, grid=(T // tr,), in_specs=[row, pl.BlockSpec((M, 2 * D_MODEL), lambda i: (0, 0))],
        out_specs=row, out_shape=jax.ShapeDtypeStruct((T, D_MODEL), BF16),
        compiler_params=pltpu.CompilerParams(dimension_semantics=("parallel",)),
    )(q, kv)


def _cross_bwd(q, kv, do, name):
    T = q.shape[0]
    M = kv.shape[0]
    tr = _pick(T, (ROW_TILE, 128, 64, 32, 16, 8))
    W = CROSS_HEAD_DIM

    def body(q_ref, kv_ref, do_ref, dq_ref, dkv_ref):
        @pl.when(pl.program_id(0) == 0)
        def _():
            dkv_ref[...] = jnp.zeros_like(dkv_ref)

        for h in range(CROSS_HEADS):
            qh = q_ref[:, h * W:(h + 1) * W].astype(BF16)
            kh = kv_ref[:, h * W:(h + 1) * W].astype(BF16)
            vh = kv_ref[:, D_MODEL + h * W:D_MODEL + (h + 1) * W].astype(BF16)
            doh = do_ref[:, h * W:(h + 1) * W].astype(BF16)
            pn = _cross_probs(qh, kh)
            dp = lax.dot_general(doh, vh, _DIMS["nt"], preferred_element_type=F32)
            delta = jnp.sum(pn * dp, axis=1, keepdims=True)
            dsc = (pn * (dp - delta) * (W ** -0.5)).astype(BF16)
            dq_ref[:, h * W:(h + 1) * W] = jnp.dot(dsc, kh, preferred_element_type=F32).astype(BF16)
            dkv_ref[:, h * W:(h + 1) * W] += lax.dot_general(dsc, qh, _DIMS["tn"], preferred_element_type=F32)
            dkv_ref[:, D_MODEL + h * W:D_MODEL + (h + 1) * W] += lax.dot_general(
                pn.astype(BF16), doh, _DIMS["tn"], preferred_element_type=F32)

    row = pl.BlockSpec((tr, D_MODEL), lambda i: (i, 0))
    full = pl.BlockSpec((M, 2 * D_MODEL), lambda i: (0, 0))
    return pl.pallas_call(
        body, name=name, grid=(T // tr,), in_specs=[row, full, row], out_specs=[row, full],
        out_shape=[jax.ShapeDtypeStruct((T, D_MODEL), BF16), jax.ShapeDtypeStruct((M, 2 * D_MODEL), F32)],
        compiler_params=pltpu.CompilerParams(dimension_semantics=("arbitrary",)),
    )(q, kv, do)


def _loss_head(y, target, name):
    T, D = y.shape
    tr = _pick(T, (ROW_TILE, 128, 64, 32, 16, 8))

    def body(y_ref, t_ref, l_ref, dy_ref):
        @pl.when(pl.program_id(0) == 0)
        def _():
            l_ref[...] = jnp.zeros_like(l_ref)

        err = y_ref[...] - t_ref[...]
        dy_ref[...] = err * (1.0 / D)
        l_ref[...] += jnp.broadcast_to(0.5 * jnp.sum(jnp.mean(err * err, axis=-1, keepdims=True), axis=0, keepdims=True),
                                       (8, 128))

    row = pl.BlockSpec((tr, D), lambda i: (i, 0))
    return pl.pallas_call(
        body, name=name, grid=(T // tr,), in_specs=[row, row],
        out_specs=[pl.BlockSpec((8, 128), lambda i: (0, 0)), row],
        out_shape=[jax.ShapeDtypeStruct((8, 128), F32), jax.ShapeDtypeStruct((T, D), F32)],
        compiler_params=pltpu.CompilerParams(dimension_semantics=("arbitrary",)),
    )(y, target)


def _sum_slots(recv, name):
    _, R, C = recv.shape
    tr = _pick(R, (ROW_TILE, 128, 64, 32, 16, 8))

    def body(r_ref, o_ref):
        acc = r_ref[0].astype(F32)
        for d in range(1, N_DEV):
            acc = acc + r_ref[d].astype(F32)
        o_ref[...] = acc

    return pl.pallas_call(
        body, name=name, grid=(R // tr,), in_specs=[pl.BlockSpec((N_DEV, tr, C), lambda i: (0, i, 0))],
        out_specs=pl.BlockSpec((tr, C), lambda i: (i, 0)), out_shape=jax.ShapeDtypeStruct((R, C), F32),
        compiler_params=pltpu.CompilerParams(dimension_semantics=("parallel",)),
    )(recv)


SUM_STEPS = 2


def _sum_blocks(recvs, sents, me, name):
    n = len(recvs)

    def body(me_ref, *refs):
        me = me_ref[0]
        for i in range(n):
            r_ref, s_ref, o_ref = refs[i], refs[n + i], refs[2 * n + i]
            acc = s_ref[0].astype(F32)
            for d in range(N_DEV):
                acc = acc + jnp.where(d == me, 0.0, r_ref[d].astype(F32))
            o_ref[...] = acc

    tiles = [r.shape[1] // SUM_STEPS for r in recvs]
    C = recvs[0].shape[2]
    return pl.pallas_call(
        body, name=name,
        grid_spec=pltpu.PrefetchScalarGridSpec(
            num_scalar_prefetch=1, grid=(SUM_STEPS,),
            in_specs=[pl.BlockSpec((N_DEV, t, C), lambda s, me_ref: (0, s, 0)) for t in tiles]
            + [pl.BlockSpec((1, t, C), lambda s, me_ref: (me_ref[0], s, 0)) for t in tiles],
            out_specs=[pl.BlockSpec((t, C), lambda s, me_ref: (s, 0)) for t in tiles]),
        out_shape=[jax.ShapeDtypeStruct((r.shape[1], C), F32) for r in recvs],
        compiler_params=pltpu.CompilerParams(dimension_semantics=("parallel",)),
    )(me, *recvs, *sents)


def _adamw(w, g, m, v, name):
    shape = w.shape
    C = shape[-1]
    R = math.prod(shape[:-1])
    w2, g2, m2, v2 = (t.reshape(R, C) for t in (w, g, m, v))
    tr = _pick(R, (ROW_TILE, 128, 64, 32, 16, 8))

    def body(w_ref, g_ref, m_ref, v_ref, d_ref, mo_ref, vo_ref):
        gg = g_ref[...]
        mn = ADAM_B1 * m_ref[...] + (1.0 - ADAM_B1) * gg
        vn = ADAM_B2 * v_ref[...] + (1.0 - ADAM_B2) * (gg * gg)
        m_hat = mn / (1.0 - ADAM_B1 ** ADAM_STEP)
        v_hat = vn / (1.0 - ADAM_B2 ** ADAM_STEP)
        d_ref[...] = -ADAM_LR * (m_hat / (jnp.sqrt(v_hat) + ADAM_EPS) + ADAM_WD * w_ref[...])
        mo_ref[...] = mn
        vo_ref[...] = vn

    blk = pl.BlockSpec((tr, C), lambda i: (i, 0))
    shp = jax.ShapeDtypeStruct((R, C), F32)
    d, mo, vo = pl.pallas_call(
        body, name=name, grid=(R // tr,), in_specs=[blk] * 4, out_specs=[blk] * 3, out_shape=[shp] * 3,
        compiler_params=pltpu.CompilerParams(dimension_semantics=("parallel",)),
    )(w2, g2, m2, v2)
    return d.reshape(shape), mo.reshape(shape), vo.reshape(shape)


def _all_gather_many(bufs, name):
    n = len(bufs)

    def body(*refs):
        xs, outs = refs[:n], refs[n:2 * n]
        send_sems, recv_sems, local_sems = refs[2 * n:]
        x, y, c = lax.axis_index("x"), lax.axis_index("y"), lax.axis_index("c")
        me, sibling = (x, y, c), (x, y, 1 - c)
        chips = [(1 - x, y), (x, 1 - y), (1 - x, 1 - y)]

        def slot(i, px, py, pc):
            return outs[i].at[4 * px + 2 * py + pc]

        def copy(i, k, block, to, src=None):
            return pltpu.make_async_remote_copy(
                src_ref=slot(i, *block) if src is None else src, dst_ref=slot(i, *block),
                send_sem=send_sems.at[7 * i + k], recv_sem=recv_sems.at[7 * i + k],
                device_id=to, device_id_type=pl.DeviceIdType.MESH)

        mine = [pltpu.make_async_copy(xs[i], slot(i, *me), local_sems.at[i]) for i in range(n)]
        for cp in mine:
            cp.start()
        first = [copy(i, 0, me, sibling, src=xs[i]) for i in range(n)]
        for j, chip in enumerate(chips):
            first += [copy(i, 1 + j, me, (*chip, c), src=xs[i]) for i in range(n)]
        for cp in first:
            cp.start()
        passed = []
        for j, chip in enumerate(chips):
            for i in range(n):
                copy(i, 1 + j, (*chip, c), me).wait_recv()
                passed.append(copy(i, 4 + j, (*chip, c), sibling))
                passed[-1].start()
        for i in range(n):
            copy(i, 0, sibling, me).wait_recv()
        for j, chip in enumerate(chips):
            for i in range(n):
                copy(i, 4 + j, (*chip, 1 - c), me).wait_recv()
        for cp in first + passed:
            cp.wait_send()
        for cp in mine:
            cp.wait()

    hbm = pl.BlockSpec(memory_space=pl.ANY)
    return pl.pallas_call(
        body, name=name, out_shape=[jax.ShapeDtypeStruct((N_DEV,) + b.shape, b.dtype) for b in bufs],
        in_specs=[hbm] * n, out_specs=[hbm] * n,
        scratch_shapes=[pltpu.SemaphoreType.DMA((7 * n,)), pltpu.SemaphoreType.DMA((7 * n,)),
                        pltpu.SemaphoreType.DMA((n,))],
    )(*bufs)


def _all_gather(buf, name):
    return _all_gather_many([buf], name)[0]


_HBM = pl.BlockSpec(memory_space=pltpu.HBM)
_SEM = pl.BlockSpec(memory_space=pltpu.SEMAPHORE)
_EFFECT = pltpu.SideEffectType.DATAFLOW_SIDE_EFFECTING


def _peer(k):
    x, y, c = lax.axis_index("x"), lax.axis_index("y"), lax.axis_index("c")
    return x ^ ((k >> 2) & 1), y ^ ((k >> 1) & 1), c ^ (k & 1)


def _my_slot():
    return 4 * lax.axis_index("x") + 2 * lax.axis_index("y") + lax.axis_index("c")


def _split_copy(src_refs, land_refs, send_sems, recv_sems, i, k):
    px, py, pc = _peer(k)
    return pltpu.make_async_remote_copy(
        src_ref=src_refs[i].at[4 * px + 2 * py + pc], dst_ref=land_refs[i].at[_my_slot()],
        send_sem=send_sems.at[7 * i + k - 1], recv_sem=recv_sems.at[7 * i + k - 1],
        device_id=(px, py, pc), device_id_type=pl.DeviceIdType.MESH)


def _split_start(srcs, lands, name):
    n = len(srcs)

    def body(*refs):
        src_refs, land_refs = refs[:n], refs[n:2 * n]
        send_sems, recv_sems = refs[2 * n], refs[2 * n + 1]
        token = refs[-1]
        for i in range(n):
            for k in range(1, N_DEV):
                _split_copy(src_refs, land_refs, send_sems, recv_sems, i, k).start()
        token[...] = jnp.zeros_like(token)

    outs = pl.pallas_call(
        body, name=name,
        out_shape=(pltpu.SemaphoreType.DMA((7 * n,)), pltpu.SemaphoreType.DMA((7 * n,)),
                   *[pltpu.HBM(a.shape, a.dtype) for a in srcs], *[pltpu.HBM(a.shape, a.dtype) for a in lands],
                   jax.ShapeDtypeStruct((8, 128), F32)),
        in_specs=[_HBM] * (2 * n),
        out_specs=(_SEM, _SEM, *([_HBM] * (2 * n)), pl.BlockSpec(memory_space=pltpu.VMEM)),
        input_output_aliases={i: 2 + i for i in range(2 * n)},
        compiler_params=pltpu.CompilerParams(has_side_effects=_EFFECT),
    )(*[pltpu.with_memory_space_constraint(a, pltpu.HBM) for a in list(srcs) + list(lands)])
    return outs[0], outs[1], outs[2:2 + n], outs[2 + n:2 + 2 * n], outs[-1]


def _split_wait(send_sems, recv_sems, srcs, lands, after, name):
    n = len(srcs)

    def body(*refs):
        src_refs, land_refs = refs[:n], refs[n:2 * n]
        ssem, rsem = refs[2 * n], refs[2 * n + 1]
        for i in range(n):
            for k in range(1, N_DEV):
                cp = _split_copy(src_refs, land_refs, ssem, rsem, i, k)
                cp.wait_send()
                cp.wait_recv()

    outs = pl.pallas_call(
        body, name=name,
        out_shape=(*[pltpu.HBM(a.shape, a.dtype) for a in srcs], *[pltpu.HBM(a.shape, a.dtype) for a in lands]),
        in_specs=[*([_HBM] * (2 * n)), _SEM, _SEM, pl.BlockSpec(memory_space=pl.ANY)],
        out_specs=tuple([_HBM] * (2 * n)),
        input_output_aliases={i: i for i in range(2 * n)},
        compiler_params=pltpu.CompilerParams(has_side_effects=_EFFECT),
    )(*srcs, *lands, send_sems, recv_sems, after)
    return outs[:n], outs[n:]


def _gather_first(src_refs, land_refs, send_sems, recv_sems, i, k):
    x, y, c = lax.axis_index("x"), lax.axis_index("y"), lax.axis_index("c")
    to = ((x, y, 1 - c), (1 - x, y, c), (x, 1 - y, c), (1 - x, 1 - y, c))[k]
    return pltpu.make_async_remote_copy(
        src_ref=src_refs[i], dst_ref=land_refs[i].at[_my_slot()],
        send_sem=send_sems.at[4 * i + k], recv_sem=recv_sems.at[4 * i + k],
        device_id=to, device_id_type=pl.DeviceIdType.MESH)


def _gather_second(land_refs, send_sems, recv_sems, i, j):
    x, y, c = lax.axis_index("x"), lax.axis_index("y"), lax.axis_index("c")
    px, py = ((1 - x, y), (x, 1 - y), (1 - x, 1 - y))[j]
    slot = land_refs[i].at[4 * px + 2 * py + c]
    return pltpu.make_async_remote_copy(
        src_ref=slot, dst_ref=slot, send_sem=send_sems.at[3 * i + j], recv_sem=recv_sems.at[3 * i + j],
        device_id=(x, y, 1 - c), device_id_type=pl.DeviceIdType.MESH)


def _gather_start(srcs, lands, name, after):
    n = len(srcs)

    def body(*refs):
        src_refs, land_refs = refs[:n], refs[n:2 * n]
        send_sems, recv_sems = refs[2 * n + 1], refs[2 * n + 2]
        token = refs[-1]
        for k in range(4):
            for i in range(n):
                _gather_first(src_refs, land_refs, send_sems, recv_sems, i, k).start()
        token[...] = jnp.zeros_like(token)

    outs = pl.pallas_call(
        body, name=name,
        out_shape=(pltpu.SemaphoreType.DMA((4 * n,)), pltpu.SemaphoreType.DMA((4 * n,)),
                   *[pltpu.HBM(a.shape, a.dtype) for a in srcs], *[pltpu.HBM(a.shape, a.dtype) for a in lands],
                   jax.ShapeDtypeStruct((8, 128), F32)),
        in_specs=[_HBM] * (2 * n) + [pl.BlockSpec(memory_space=pl.ANY)],
        out_specs=(_SEM, _SEM, *([_HBM] * (2 * n)), pl.BlockSpec(memory_space=pltpu.VMEM)),
        input_output_aliases={i: 2 + i for i in range(2 * n)},
        compiler_params=pltpu.CompilerParams(has_side_effects=_EFFECT),
    )(*[pltpu.with_memory_space_constraint(a, pltpu.HBM) for a in list(srcs) + list(lands)], after)
    return outs[0], outs[1], outs[2:2 + n], outs[2 + n:2 + 2 * n], outs[-1]


def _gather_forward(send1, recv1, srcs, lands, after, name):
    n = len(srcs)

    def body(*refs):
        src_refs, land_refs = refs[:n], refs[n:2 * n]
        s1, r1 = refs[2 * n], refs[2 * n + 1]
        s2, r2 = refs[2 * n + 3], refs[2 * n + 4]
        token = refs[-1]
        for j in range(3):
            for i in range(n):
                _gather_first(src_refs, land_refs, s1, r1, i, 1 + j).wait_recv()
                _gather_second(land_refs, s2, r2, i, j).start()
        for i in range(n):
            _gather_first(src_refs, land_refs, s1, r1, i, 0).wait_recv()
            for k in range(4):
                _gather_first(src_refs, land_refs, s1, r1, i, k).wait_send()
        token[...] = jnp.zeros_like(token)

    outs = pl.pallas_call(
        body, name=name,
        out_shape=(pltpu.SemaphoreType.DMA((3 * n,)), pltpu.SemaphoreType.DMA((3 * n,)),
                   *[pltpu.HBM(a.shape, a.dtype) for a in srcs], *[pltpu.HBM(a.shape, a.dtype) for a in lands],
                   jax.ShapeDtypeStruct((8, 128), F32)),
        in_specs=[*([_HBM] * (2 * n)), _SEM, _SEM, pl.BlockSpec(memory_space=pl.ANY)],
        out_specs=(_SEM, _SEM, *([_HBM] * (2 * n)), pl.BlockSpec(memory_space=pltpu.VMEM)),
        input_output_aliases={i: 2 + i for i in range(2 * n)},
        compiler_params=pltpu.CompilerParams(has_side_effects=_EFFECT),
    )(*srcs, *lands, send1, recv1, after)
    return outs[0], outs[1], outs[2:2 + n], outs[2 + n:2 + 2 * n], outs[-1]


def _gather_wait(send2, recv2, srcs, lands, after, name):
    n = len(srcs)

    def body(*refs):
        land_refs = refs[n:2 * n]
        s2, r2 = refs[2 * n], refs[2 * n + 1]
        for i in range(n):
            for j in range(3):
                cp = _gather_second(land_refs, s2, r2, i, j)
                cp.wait_send()
                cp.wait_recv()

    outs = pl.pallas_call(
        body, name=name,
        out_shape=(*[pltpu.HBM(a.shape, a.dtype) for a in srcs], *[pltpu.HBM(a.shape, a.dtype) for a in lands]),
        in_specs=[*([_HBM] * (2 * n)), _SEM, _SEM, pl.BlockSpec(memory_space=pl.ANY)],
        out_specs=tuple([_HBM] * (2 * n)),
        input_output_aliases={i: i for i in range(2 * n)},
        compiler_params=pltpu.CompilerParams(has_side_effects=_EFFECT),
    )(*srcs, *lands, send2, recv2, after)
    return outs[:n], outs[n:]


def _row(a, l):
    return a[l:l + 1]


def _tie(a, token):
    return a if token is None else a + token[0, 0]


def _layer_fwd(h, mem, W, l, tab, after, at):
    s = {}
    n = f"l{l}_"
    s["h0"] = h
    P = _matmul(h, W["w_in"][l], "nt", n + "proj", after=after, out_dtype=BF16)
    s["P"] = P
    sink_col = jnp.repeat(W["sinks"][l].reshape(N_KV_HEADS, GROUP), BLOCK, axis=1)[:, :, None]
    s["sink_col"] = sink_col
    y_rnn, hs = _rnn_fwd(P, W["conv_w"][l], _row(W["conv_b"], l), W["w_rg"][l], _row(W["b_rg"], l),
                         W["w_ig"][l], _row(W["b_ig"], l), _row(W["lru_lambda"], l), n + "rnn_fwd")
    y_attn = _swa_fwd(P, tab, _tie(sink_col, at(l, "proj", P)), n + "swa_fwd")
    at(l, "attn", y_attn)
    mr = _matmul(y_rnn, W["w_br_rnn"][l], "nn", n + "br_rnn")
    ma = _matmul(y_attn, W["w_br_attn"][l], "nn", n + "br_attn")
    merged = _merge_fwd(P, mr, ma, n + "merge_fwd")
    h1, xh1, rs1 = _matmul_ln(merged, W["w_out"][l], h, _tie(_row(W["ln1_g"], l), at(l, "mix", merged)),
                              _row(W["ln1_b"], l), n + "w_out_ln1")
    at(l, "ln1", h1)
    s.update(hs=hs, y_rnn=y_rnn, y_attn=y_attn, mr=mr, ma=ma, merged=merged, xh1=xh1, rs1=rs1, h1=h1)

    qc = _matmul(h1, W["cq_w"][l], "nn", n + "cq", out_dtype=BF16)
    kv = _matmul(mem, W["ckv_w"][l], "nt", n + "ckv", out_dtype=BF16)
    oc = _cross_fwd(qc, kv, n + "cross_fwd")
    h2, xh2, rs2 = _matmul_ln(oc, W["co_w"][l], h1, _row(W["ln2_g"], l), _row(W["ln2_b"], l), n + "co_ln2")
    s.update(qc=qc, kv=kv, oc=oc, xh2=xh2, rs2=rs2, h2=h2)

    U = _matmul(h2, W["ffn_wi"][l], "nt", n + "ffn_wi", after=at(l, "ln2", h2), out_dtype=BF16)
    act = _swiglu_fwd(U, n + "swiglu_fwd")
    h3, xh3, rs3 = _matmul_ln(act, W["ffn_wo"][l], h2, _tie(_row(W["ln3_g"], l), at(l, "ffn", act)),
                              _row(W["ln3_b"], l), n + "ffn_wo_ln3")
    s.update(U=U, act=act, xh3=xh3, rs3=rs3)
    return h3, s


GRAD_PARTS = (("ffn_wi", "ffn_wo", "cq_w", "ckv_w", "co_w"), ("w_out", "w_br_rnn", "w_br_attn"),
              ("w_in", "w_rg", "w_ig"))


def _layer_bwd(dh3, mem, W, l, tab, s, send):
    n = f"l{l}_"
    g = {}
    dz3, g["ln3_g"], g["ln3_b"] = _ln_bwd(dh3, s["xh3"], s["rs3"], _row(W["ln3_g"], l), n + "ln3_bwd")
    g["ffn_wo"] = _matmul(s["act"], dz3, "tn", n + "d_ffn_wo", out_dtype=BF16)
    dact = _matmul(dz3, W["ffn_wo"][l], "nt", n + "d_act", out_dtype=BF16)
    dU = _swiglu_bwd(s["U"], dact, n + "swiglu_bwd")
    g["ffn_wi"] = _matmul(dU, s["h2"], "tn", n + "d_ffn_wi", out_dtype=BF16)
    dh2 = _matmul(dU, W["ffn_wi"][l], "nn", n + "d_h2", add=dz3, add_scale=ALPHA)
    dz2, g["ln2_g"], g["ln2_b"] = _ln_bwd(dh2, s["xh2"], s["rs2"], _row(W["ln2_g"], l), n + "ln2_bwd")
    g["co_w"] = _matmul(s["oc"], dz2, "tn", n + "d_co", out_dtype=BF16)
    doc = _matmul(dz2, W["co_w"][l], "nt", n + "d_oc", out_dtype=BF16)
    dqc, dkv = _cross_bwd(s["qc"], s["kv"], doc, n + "cross_bwd")
    g["ckv_w"] = _matmul(dkv, mem, "tn", n + "d_ckv", out_dtype=BF16)
    g["cq_w"] = _matmul(s["h1"], dqc, "tn", n + "d_cq", out_dtype=BF16)
    after = send(l, 0, g)
    dh1 = _matmul(dqc, W["cq_w"][l], "nt", n + "d_h1", add=dz2, add_scale=ALPHA, after=after)
    dz1, g["ln1_g"], g["ln1_b"] = _ln_bwd(dh1, s["xh1"], s["rs1"], _row(W["ln1_g"], l), n + "ln1_bwd")
    g["w_out"] = _matmul(s["merged"], dz1, "tn", n + "d_w_out", out_dtype=BF16)
    dmerged = _matmul(dz1, W["w_out"][l], "nt", n + "d_merged")
    dmr, dma, dgrnn, dgattn = _merge_bwd(s["P"], s["mr"], s["ma"], dmerged, n + "merge_bwd")
    g["w_br_rnn"] = _matmul(s["y_rnn"], dmr, "tn", n + "d_br_rnn", out_dtype=BF16)
    g["w_br_attn"] = _matmul(s["y_attn"], dma, "tn", n + "d_br_attn", out_dtype=BF16)
    after = send(l, 1, g)
    dy_rnn = _matmul(dmr, W["w_br_rnn"][l], "nt", n + "d_y_rnn", after=after)
    dy_attn = _matmul(dma, W["w_br_attn"][l], "nt", n + "d_y_attn", out_dtype=BF16)
    dxr, dgr, g["conv_w"], g["conv_b"], g["w_rg"], g["b_rg"], g["w_ig"], g["b_ig"], g["lru_lambda"] = _rnn_bwd(
        s["P"], s["hs"], dy_rnn, W["conv_w"][l], _row(W["conv_b"], l), W["w_rg"][l], _row(W["b_rg"], l),
        W["w_ig"][l], _row(W["b_ig"], l), _row(W["lru_lambda"], l), n + "rnn_bwd")
    dq, dk, dv, dsk = _swa_bwd(s["P"], dy_attn, tab, s["sink_col"], n + "swa_bwd")
    g["sinks"] = dsk[:, :, 0].reshape(1, N_Q_HEADS)
    dP = jnp.concatenate([dxr, dgr, dq, dk.astype(BF16), dv.astype(BF16), dgrnn, dgattn], axis=1)
    g["w_in"] = _matmul(dP, s["h0"], "tn", n + "d_w_in", out_dtype=BF16)
    after = send(l, 2, g)
    dh = _matmul(dP, W["w_in"][l], "nn", n + "d_h0", add=dz1, add_scale=ALPHA, after=after)
    return dh, g


def _local_step(x, mem, target, W, at, send):
    T = x.shape[0]
    tab = _rope_table(T)
    h = x
    saved = []
    for l in range(DEPTH):
        after = at(l, "start", h)
        h, s = _layer_fwd(h, mem, W, l, tab, after, at)
        saved.append(s)
    lblk, dh = _loss_head(h, target, "loss_head")
    grads = [None] * DEPTH
    for l in reversed(range(DEPTH)):
        dh, grads[l] = _layer_bwd(dh, mem, W, l, tab, saved[l], send)
    return lblk[0, 0], dh, grads


COL_SHARDED = ("w_in", "ckv_w", "ffn_wi")
GATE_MATS = ("w_rg", "w_ig")


def _shard_rows(shards, l):
    out = []
    for n, r in PACK_ROWS:
        a = shards[n][l].astype(BF16)
        if n in COL_SHARDED:
            a = a.T
        elif n in GATE_MATS:
            a = a.reshape(RNN_BLOCKS * RNN_BLOCK // N_DEV, RNN_BLOCK)
        out.append(a)
    return out


def _full_weight(G, name):
    if name in GATE_MATS:
        return jnp.transpose(G.reshape(N_DEV, RNN_BLOCKS, RNN_BLOCK // N_DEV, RNN_BLOCK), (1, 0, 2, 3)).reshape(
            RNN_BLOCKS, RNN_BLOCK, RNN_BLOCK)
    return G.reshape(N_DEV * G.shape[1], G.shape[2])


SHARD_ROWS = dict(PACK_ROWS)


def _owner_blocks(g, names):
    out = []
    for name in names:
        a = g[name]
        if name in GATE_MATS:
            a = jnp.transpose(a.astype(BF16).reshape(RNN_BLOCKS, N_DEV, RNN_BLOCK // N_DEV, RNN_BLOCK), (1, 0, 2, 3))
        out.append(a.reshape(N_DEV, SHARD_ROWS[name], D_MODEL))
    return out


def _pack_small(g):
    rows = [g["conv_w"]]
    for nme in SMALL_NAMES:
        a = g[nme]
        if nme == "sinks":
            a = jnp.pad(a, ((0, 0), (0, D_MODEL - N_Q_HEADS)))
        rows.append(a)
    rows.append(jnp.zeros((SMALL_ROWS - CONV_WIDTH - len(SMALL_NAMES), D_MODEL), F32))
    return jnp.concatenate(rows, axis=0)


_SHARD_SHAPES = {"w_in": (1024, 672), "w_br_rnn": (128, 1024), "w_br_attn": (128, 1024), "w_out": (128, 1024),
                 "cq_w": (128, 1024), "ckv_w": (1024, 256), "co_w": (128, 1024), "ffn_wi": (1024, 704),
                 "ffn_wo": (352, 1024), "w_rg": (4, 32, 256), "w_ig": (4, 32, 256)}

LAYER0_GROUPS = (("w_in", "w_rg", "w_ig"), ("w_br_rnn", "w_br_attn", "w_out"),
                 ("cq_w", "ckv_w", "co_w", "ffn_wi", "ffn_wo"))
LAYER0_FORWARD_AT = {"proj": 1, "mix": 2}
LAYER0_WAIT_AT = {"attn": 1, "ln1": 2}
NEXT_LAYER_FORWARD_AT = ("ffn", "ln2", "ln2", None)

WEIGHT_NAMES = ("w_in", "conv_w", "conv_b", "w_rg", "b_rg", "w_ig", "b_ig", "lru_lambda", "w_br_rnn", "w_br_attn",
                "sinks", "w_out", "ln1_g", "ln1_b", "cq_w", "ckv_w", "co_w", "ln2_g", "ln2_b", "ffn_wi", "ffn_wo",
                "ln3_g", "ln3_b")


def kernel(x, mem, w_in, conv_w, conv_b, w_rg, b_rg, w_ig, b_ig, lru_lambda, w_br_rnn, w_br_attn, sinks, w_out, ln1_g, ln1_b, cq_w, ckv_w, co_w, ln2_g, ln2_b, ffn_wi, ffn_wo, ln3_g, ln3_b, loss_target, m_w_in, m_conv_w, m_conv_b, m_w_rg, m_b_rg, m_w_ig, m_b_ig, m_lru_lambda, m_w_br_rnn, m_w_br_attn, m_sinks, m_w_out, m_ln1_g, m_ln1_b, m_cq_w, m_ckv_w, m_co_w, m_ln2_g, m_ln2_b, m_ffn_wi, m_ffn_wo, m_ln3_g, m_ln3_b, v_w_in, v_conv_w, v_conv_b, v_w_rg, v_b_rg, v_w_ig, v_b_ig, v_lru_lambda, v_w_br_rnn, v_w_br_attn, v_sinks, v_w_out, v_ln1_g, v_ln1_b, v_cq_w, v_ckv_w, v_co_w, v_ln2_g, v_ln2_b, v_ffn_wi, v_ffn_wo, v_ln3_g, v_ln3_b):
    w = dict(w_in=w_in, conv_w=conv_w, conv_b=conv_b, w_rg=w_rg, b_rg=b_rg, w_ig=w_ig, b_ig=b_ig,
             lru_lambda=lru_lambda, w_br_rnn=w_br_rnn, w_br_attn=w_br_attn, sinks=sinks, w_out=w_out, ln1_g=ln1_g,
             ln1_b=ln1_b, cq_w=cq_w, ckv_w=ckv_w, co_w=co_w, ln2_g=ln2_g, ln2_b=ln2_b, ffn_wi=ffn_wi, ffn_wo=ffn_wo,
             ln3_g=ln3_g, ln3_b=ln3_b)
    m = dict(w_in=m_w_in, conv_w=m_conv_w, conv_b=m_conv_b, w_rg=m_w_rg, b_rg=m_b_rg, w_ig=m_w_ig, b_ig=m_b_ig,
             lru_lambda=m_lru_lambda, w_br_rnn=m_w_br_rnn, w_br_attn=m_w_br_attn, sinks=m_sinks, w_out=m_w_out,
             ln1_g=m_ln1_g, ln1_b=m_ln1_b, cq_w=m_cq_w, ckv_w=m_ckv_w, co_w=m_co_w, ln2_g=m_ln2_g, ln2_b=m_ln2_b,
             ffn_wi=m_ffn_wi, ffn_wo=m_ffn_wo, ln3_g=m_ln3_g, ln3_b=m_ln3_b)
    v = dict(w_in=v_w_in, conv_w=v_conv_w, conv_b=v_conv_b, w_rg=v_w_rg, b_rg=v_b_rg, w_ig=v_w_ig, b_ig=v_b_ig,
             lru_lambda=v_lru_lambda, w_br_rnn=v_w_br_rnn, w_br_attn=v_w_br_attn, sinks=v_sinks, w_out=v_w_out,
             ln1_g=v_ln1_g, ln1_b=v_ln1_b, cq_w=v_cq_w, ckv_w=v_ckv_w, co_w=v_co_w, ln2_g=v_ln2_g, ln2_b=v_ln2_b,
             ffn_wi=v_ffn_wi, ffn_wo=v_ffn_wo, ln3_g=v_ln3_g, ln3_b=v_ln3_b)
    my_dev = 4 * lax.axis_index("x") + 2 * lax.axis_index("y") + lax.axis_index("c")

    W = {n: [None] * DEPTH for n, _ in PACK_ROWS}
    shards0 = dict(zip([n for n, _ in PACK_ROWS], _shard_rows(w, 0)))
    *gathered, conv_all = _all_gather_many(
        [shards0[n] for n in LAYER0_GROUPS[0]] + [conv_w.reshape(DEPTH * CONV_WIDTH, D_MODEL // N_DEV)],
        "l0_gather_first")
    for n, G in zip(LAYER0_GROUPS[0], gathered):
        W[n][0] = _full_weight(G, n)
    W["conv_w"] = jnp.transpose(conv_all, (1, 0, 2)).reshape(DEPTH, CONV_WIDTH, D_MODEL)
    for n in SMALL_NAMES:
        W[n] = w[n]
    flying = {}
    token = conv_all
    groups = [((0, gi), LAYER0_GROUPS[gi], [shards0[n] for n in LAYER0_GROUPS[gi]]) for gi in (1, 2)]
    groups += [((l, 0), [n for n, _ in PACK_ROWS], _shard_rows(w, l)) for l in range(1, DEPTH)]
    for key, names, srcs in groups:
        lands = [lax.empty((N_DEV,) + a.shape, a.dtype) for a in srcs]
        flying[key] = (names,) + _gather_start(srcs, lands, f"l{key[0]}_gather_start{key[1]}", token)
        token = flying[key][5]
    first_token = token

    def forward(key, after):
        names, send1, recv1, srcs, lands, _ = flying[key]
        flying[key] = (names,) + _gather_forward(send1, recv1, srcs, lands, after, f"l{key[0]}_gather_forward{key[1]}")
        return flying[key][5]

    def arrive(key, after):
        names, send2, recv2, srcs, lands, _ = flying.pop(key)
        srcs, lands = _gather_wait(send2, recv2, srcs, lands, after, f"l{key[0]}_gather_wait{key[1]}")
        for n, mine, G in zip(names, srcs, lands):
            W[n][key[0]] = _full_weight(lax.dynamic_update_index_in_dim(G, mine, my_dev, 0), n)

    def at(l, point, x):
        if point == "start":
            if l == 0:
                return first_token
            arrive((l, 0), x)
        elif l == 0 and point in LAYER0_FORWARD_AT:
            return forward((0, LAYER0_FORWARD_AT[point]), x)
        elif l == 0 and point in LAYER0_WAIT_AT:
            arrive((0, LAYER0_WAIT_AT[point]), x)
        elif point == NEXT_LAYER_FORWARD_AT[l]:
            return forward((l + 1, 0), x)
        return None

    summed = {}
    sent = {}
    me = my_dev.astype(jnp.int32).reshape(1)

    def finish(l, part, after):
        ssem, rsem, blocks, lands = sent.pop((l, part))
        blocks, recvs = _split_wait(ssem, rsem, blocks, lands, after, f"l{l}_exchange_wait{part}")
        for n, s in zip(GRAD_PARTS[part], _sum_blocks(recvs, blocks, me, f"l{l}_sum_grads{part}")):
            summed[(l, n)] = s

    def send(l, part, g):
        blocks = _owner_blocks(g, GRAD_PARTS[part])
        if (l + 1, part) in sent:
            finish(l + 1, part, blocks[0])
        ssem, rsem, blocks, lands, token = _split_start(blocks, [lax.empty(b.shape, b.dtype) for b in blocks],
                                                        f"l{l}_exchange_start{part}")
        sent[(l, part)] = (ssem, rsem, blocks, lands)
        return token

    loss_local, dx, layer_grads = _local_step(x[0], mem[0], loss_target[0], W, at, send)
    loss = lax.psum(loss_local, MESH_AXES)

    grads, deltas, new_m, new_v = {}, {}, {}, {}

    def update(n):
        if n in SHARD_ROWS:
            blk = jnp.stack([summed[(l, n)] for l in range(DEPTH)])
            grads[n] = jnp.transpose(blk, (0, 2, 1)) if n in COL_SHARDED else blk.reshape((DEPTH,) + _SHARD_SHAPES[n])
        deltas[n], new_m[n], new_v[n] = _adamw(w[n], grads[n], m[n], v[n], "adamw_" + n)

    early = GRAD_PARTS[0] + GRAD_PARTS[1]
    for part in (0, 1):
        finish(0, part, dx)
    for n in early:
        update(n)
    finish(0, 2, sum(deltas[n].reshape(-1)[:128] for n in early))
    for n in GRAD_PARTS[2]:
        update(n)

    small_all = _all_gather(jnp.concatenate([_pack_small(g) for g in layer_grads], axis=0), "gather_small_grads")
    small_sum = _sum_slots(small_all, "sum_small_grads").reshape(DEPTH, SMALL_ROWS, D_MODEL)
    conv_full = small_sum[:, :CONV_WIDTH, :]
    grads["conv_w"] = lax.dynamic_slice_in_dim(conv_full, my_dev * (D_MODEL // N_DEV), D_MODEL // N_DEV, axis=2)
    for i, n in enumerate(SMALL_NAMES):
        row = small_sum[:, CONV_WIDTH + i, :]
        grads[n] = row[:, :N_Q_HEADS] if n == "sinks" else row
    for n in ("conv_w",) + SMALL_NAMES:
        update(n)

    return (loss, dx[None], *[grads[n] for n in WEIGHT_NAMES], *[deltas[n] for n in WEIGHT_NAMES],
            *[new_m[n] for n in WEIGHT_NAMES], *[new_v[n] for n in WEIGHT_NAMES])
```

```python
import functools
import math

import jax
import jax.numpy as jnp
from jax import lax
from jax.experimental import pallas as pl
from jax.experimental.pallas import tpu as pltpu

F32 = jnp.float32
BF16 = jnp.bfloat16

D_MODEL = 1024
DEPTH = 4
N_DEV = 8
RNN_BLOCKS = 4
RNN_BLOCK = 256
CONV_WIDTH = 4
LRU_C = 8.0
HEAD_DIM = 64
N_Q_HEADS = 16
N_KV_HEADS = 2
GROUP = 8
KV_WIDTH = 128
BLOCK = 128
ROPE_THETA = 500000.0
ROT_DIM = 16
IN_COLS = 5376
CROSS_HEADS = 4
CROSS_HEAD_DIM = 256
D_FF = 2816
LN_EPS = 1e-5
ALPHA = (2 * DEPTH) ** 0.25
NEG_INF = -1e30

ADAM_LR = 0.001
ADAM_B1 = 0.9
ADAM_B2 = 0.999
ADAM_EPS = 1e-08
ADAM_WD = 0.01
ADAM_STEP = 10

C_XR, C_GR, C_Q, C_K, C_V, C_GRNN, C_GATTN = 0, 1024, 2048, 3072, 3200, 3328, 4352

TIME_CHUNK = 256
ROW_TILE = 256

MESH_AXES = ("x", "y", "c")

PACK_ROWS = (("w_in", 672), ("w_br_rnn", 128), ("w_br_attn", 128), ("w_out", 128), ("cq_w", 128),
             ("ckv_w", 256), ("co_w", 128), ("ffn_wi", 704), ("ffn_wo", 352), ("w_rg", 32), ("w_ig", 32))
SMALL_NAMES = ("conv_b", "b_rg", "b_ig", "lru_lambda", "sinks", "ln1_g", "ln1_b", "ln2_g", "ln2_b", "ln3_g", "ln3_b")
SMALL_ROWS = 16


def _pick(dim, cands):
    for c in cands:
        if dim % c == 0:
            return c
    return dim


_DIMS = {"nn": (((1,), (0,)), ((), ())), "nt": (((1,), (1,)), ((), ())), "tn": (((0,), (0,)), ((), ()))}

MATMUL_VMEM_BUDGET = 44 * 2 ** 20
MATMUL_MAX_TILE = 2048
MXU_DIM = 256
STEP_COST_BYTES = 500_000
MIN_ROW_TILE = 512


def _tile_candidates(dim, whole=False, step=MXU_DIM):
    c = [d for d in range(step, min(dim, MATMUL_MAX_TILE) + 1, step) if dim % d == 0]
    if whole and dim not in c:
        c.append(dim)
    return c or [dim]


def _matmul_tiles(M, N, K, sa, sb, so, has_add):
    best = None
    for tk in _tile_candidates(K, whole=True):
        nk = K // tk
        for tm in [t for t in _tile_candidates(M, step=128) if t >= min(M, MIN_ROW_TILE)]:
            for tn in _tile_candidates(N):
                vmem = 2 * (tm * tk * sa + tk * tn * sb + tm * tn * so) + tm * tn * 4
                vmem += tm * tn * 4 if nk > 1 else 0
                vmem += 2 * tm * tn * 4 if has_add else 0
                vmem += (tm * tk * 2 if sa == 4 else 0) + (tk * tn * 2 if sb == 4 else 0)
                if vmem > MATMUL_VMEM_BUDGET:
                    continue
                steps = (M // tm) * (N // tn) * nk
                exposed = tm * tk * sa + tk * tn * sb + tm * tn * so
                acc_moves = steps * tm * tn * 2 if nk > 1 else 0
                fixed = M * N * so + steps * STEP_COST_BYTES + exposed + acc_moves
                a_in = M * K * sa * ((N // tn) if nk > 1 else 1) + K * N * sb * (M // tm)
                b_in = M * K * sa * (N // tn) + K * N * sb * ((M // tm) if nk > 1 else 1)
                for cost, m_outer in ((a_in + fixed, True), (b_in + fixed, False)):
                    if best is None or cost < best[0]:
                        best = (cost, tm, tn, tk, m_outer)
    return best[1:]


def _matmul(a, b, mode, name, add=None, add_scale=1.0, out_dtype=F32, after=None):
    if mode == "nn":
        (M, K), (_, N) = a.shape, b.shape
    elif mode == "nt":
        (M, K), (N, _) = a.shape, b.shape
    else:
        (K, M), (_, N) = a.shape, b.shape
    tm, tn, tk, m_outer = _matmul_tiles(M, N, K, a.dtype.itemsize, b.dtype.itemsize, jnp.dtype(out_dtype).itemsize,
                                        add is not None)
    nk = K // tk
    dims = _DIMS[mode]

    def body(*refs):
        if after is not None:
            refs = refs[:-2 - (nk > 1)] + refs[-1 - (nk > 1):]
        a_ref, b_ref = refs[0], refs[1]
        c_ref = refs[2] if add is not None else None
        o_ref = refs[3] if add is not None else refs[2]

        def finish(r):
            if add is not None:
                r = r + add_scale * c_ref[...]
            o_ref[...] = r.astype(out_dtype)

        prod = lax.dot_general(a_ref[...].astype(BF16), b_ref[...].astype(BF16), dims, preferred_element_type=F32)
        if nk == 1:
            finish(prod)
            return
        acc_ref = refs[-1]
        k = pl.program_id(2)

        @pl.when(k == 0)
        def _():
            acc_ref[...] = prod

        @pl.when(k > 0)
        def _():
            acc_ref[...] += prod

        @pl.when(k == nk - 1)
        def _():
            finish(acc_ref[...])

    ij = (lambda p, q: (p, q)) if m_outer else (lambda p, q: (q, p))
    if mode == "nn":
        a_spec = pl.BlockSpec((tm, tk), lambda p, q, k: (ij(p, q)[0], k))
        b_spec = pl.BlockSpec((tk, tn), lambda p, q, k: (k, ij(p, q)[1]))
    elif mode == "nt":
        a_spec = pl.BlockSpec((tm, tk), lambda p, q, k: (ij(p, q)[0], k))
        b_spec = pl.BlockSpec((tn, tk), lambda p, q, k: (ij(p, q)[1], k))
    else:
        a_spec = pl.BlockSpec((tk, tm), lambda p, q, k: (k, ij(p, q)[0]))
        b_spec = pl.BlockSpec((tk, tn), lambda p, q, k: (k, ij(p, q)[1]))
    o_spec = pl.BlockSpec((tm, tn), lambda p, q, k: ij(p, q))
    o_shape = jax.ShapeDtypeStruct((M, N), out_dtype)
    in_specs = [a_spec, b_spec]
    args = [a, b]
    if add is not None:
        in_specs.append(o_spec)
        args.append(add)
    if after is not None:
        in_specs.append(pl.BlockSpec(memory_space=pl.ANY))
        args.append(after)
    return pl.pallas_call(
        body, name=name, grid=(M // tm, N // tn, nk) if m_outer else (N // tn, M // tm, nk),
        in_specs=in_specs, out_specs=o_spec, out_shape=o_shape,
        scratch_shapes=[pltpu.VMEM((tm, tn), F32)] if nk > 1 else [],
        compiler_params=pltpu.CompilerParams(dimension_semantics=("parallel", "parallel", "arbitrary")),
    )(*args)


LN_ROWS = 512


def _matmul_ln(a, w, h, g, b, name):
    T, K = a.shape
    D = w.shape[1]
    tr = _pick(T, (LN_ROWS, 256, 128, 64, 32, 16, 8))

    def body(a_ref, w_ref, h_ref, g_ref, b_ref, o_ref, xh_ref, rs_ref):
        f = jnp.dot(a_ref[...].astype(BF16), w_ref[...], preferred_element_type=F32)
        z = ALPHA * h_ref[...] + f
        mu = jnp.mean(z, axis=-1, keepdims=True)
        zc = z - mu
        var = jnp.mean(zc * zc, axis=-1, keepdims=True)
        rstd = lax.rsqrt(var + LN_EPS)
        xh = zc * rstd
        xh_ref[...] = xh
        rs_ref[...] = rstd
        o_ref[...] = xh * g_ref[...] + b_ref[...]

    row = pl.BlockSpec((tr, D), lambda i: (i, 0))
    vec = pl.BlockSpec((1, D), lambda i: (0, 0))
    return pl.pallas_call(
        body, name=name, grid=(T // tr,),
        in_specs=[pl.BlockSpec((tr, K), lambda i: (i, 0)), pl.BlockSpec((K, D), lambda i: (0, 0)), row, vec, vec],
        out_specs=[row, row, pl.BlockSpec((tr, 1), lambda i: (i, 0))],
        out_shape=[jax.ShapeDtypeStruct((T, D), F32), jax.ShapeDtypeStruct((T, D), F32),
                   jax.ShapeDtypeStruct((T, 1), F32)],
        compiler_params=pltpu.CompilerParams(dimension_semantics=("parallel",)),
    )(a, w, h, g, b)


def _ln_bwd(dout, xh, rstd, g, name, after=None):
    T, D = dout.shape
    tr = _pick(T, (LN_ROWS, 256, 128, 64, 32, 16, 8))

    def body(do_ref, xh_ref, rs_ref, g_ref, *rest):
        dz_ref, dg_ref, db_ref = rest[-3:]

        @pl.when(pl.program_id(0) == 0)
        def _():
            dg_ref[...] = jnp.zeros_like(dg_ref)
            db_ref[...] = jnp.zeros_like(db_ref)

        do = do_ref[...]
        xh = xh_ref[...]
        dxh = do * g_ref[...]
        m1 = jnp.mean(dxh, axis=-1, keepdims=True)
        m2 = jnp.mean(dxh * xh, axis=-1, keepdims=True)
        dz_ref[...] = rs_ref[...] * (dxh - m1 - xh * m2)
        dg_ref[...] += jnp.sum(do * xh, axis=0, keepdims=True)
        db_ref[...] += jnp.sum(do, axis=0, keepdims=True)

    row = pl.BlockSpec((tr, D), lambda i: (i, 0))
    vec = pl.BlockSpec((1, D), lambda i: (0, 0))
    in_specs = [row, row, pl.BlockSpec((tr, 1), lambda i: (i, 0)), vec]
    args = [dout, xh, rstd, g]
    if after is not None:
        in_specs.append(pl.BlockSpec(memory_space=pl.ANY))
        args.append(after)
    return pl.pallas_call(
        body, name=name, grid=(T // tr,),
        in_specs=in_specs, out_specs=[row, vec, vec],
        out_shape=[jax.ShapeDtypeStruct((T, D), F32), jax.ShapeDtypeStruct((1, D), F32),
                   jax.ShapeDtypeStruct((1, D), F32)],
        compiler_params=pltpu.CompilerParams(dimension_semantics=("arbitrary",)),
    )(*args)


_GELU_C = math.sqrt(2.0 / math.pi)


def _gelu(x):
    t = jnp.tanh(_GELU_C * (x + 0.044715 * x * x * x))
    return 0.5 * x * (1.0 + t), t


def _gelu_grad(x, t):
    return 0.5 * (1.0 + t) + 0.5 * x * (1.0 - t * t) * _GELU_C * (1.0 + 3 * 0.044715 * x * x)


def _sigmoid(x):
    return 1.0 / (1.0 + jnp.exp(-x))


def _softplus_neg(lam):
    z = jnp.exp(-jnp.abs(lam))
    u = 1.0 + z
    l1p = jnp.where(u == 1.0, z, jnp.log(u) * z / jnp.where(u == 1.0, 1.0, u - 1.0))
    return jnp.maximum(-lam, 0.0) + l1p


def _neg_expm1(x):
    series = x * (1.0 + x * 0.5 * (1.0 + x * (1.0 / 3.0) * (1.0 + x * 0.25 * (1.0 + x * 0.2))))
    return -jnp.where(x > -0.05, series, jnp.exp(x) - 1.0)


def _scan_fwd(a, b):
    n = a.shape[0]
    rows = lax.broadcasted_iota(jnp.int32, a.shape, 0)
    s = 1
    while s < n:
        keep = rows >= s
        b = jnp.where(keep, a * pltpu.roll(b, s, 0) + b, b)
        a = jnp.where(keep, a * pltpu.roll(a, s, 0), a)
        s *= 2
    return a, b


def _scan_bwd(c, b):
    n = c.shape[0]
    rows = lax.broadcasted_iota(jnp.int32, c.shape, 0)
    s = 1
    while s < n:
        keep = rows < n - s
        b = jnp.where(keep, c * pltpu.roll(b, n - s, 0) + b, b)
        c = jnp.where(keep, c * pltpu.roll(c, n - s, 0), c)
        s *= 2
    return c, b


def _rnn_gates(xc, wr, br, wi, bi, sp):
    xb = xc.astype(BF16)
    r = _sigmoid(jnp.dot(xb, wr, preferred_element_type=F32) + br)
    i = _sigmoid(jnp.dot(xb, wi, preferred_element_type=F32) + bi)
    la = -LRU_C * r * sp
    a = jnp.exp(la)
    om = _neg_expm1(2.0 * la)
    mult = jnp.sqrt(om)
    return r, i, a, om, mult


def _rnn_specs(T):
    C = RNN_BLOCK
    col = lambda off: pl.BlockSpec((T, C), lambda n, off=off: (0, off // C + n))
    vec = pl.BlockSpec((1, C), lambda n: (0, n))
    cw = pl.BlockSpec((CONV_WIDTH, C), lambda n: (0, n))
    w = pl.BlockSpec((1, C, C), lambda n: (n, 0, 0))
    own = pl.BlockSpec((T, C), lambda n: (0, n))
    return col, vec, cw, w, own


def _rnn_fwd(P, cw, cb, wrg, brg, wig, big, lam, name):
    T = P.shape[0]
    C = RNN_BLOCK
    tc = _pick(T, (TIME_CHUNK,))
    nch = T // tc

    def body(x_ref, g_ref, cw_ref, cb_ref, wr_ref, br_ref, wi_ref, bi_ref, lam_ref, y_ref, hs_ref, xs_ref):
        sp = _softplus_neg(lam_ref[...])
        wr = wr_ref[0]
        wi = wi_ref[0]
        xs_ref[0:8, :] = jnp.zeros((8, C), F32)

        def chunk(c, hprev):
            r0 = pl.multiple_of(c * tc, tc)
            x = x_ref[pl.ds(r0, tc), :].astype(F32)
            xs_ref[8:, :] = x
            xc = cb_ref[...] + jnp.zeros((tc, C), F32)
            for k in range(CONV_WIDTH):
                xc = xc + xs_ref[pl.ds(8 - (CONV_WIDTH - 1 - k), tc), :] * cw_ref[k:k + 1, :]
            xs_ref[0:8, :] = x[tc - 8:, :]
            r, i, a, om, mult = _rnn_gates(xc, wr, br_ref[...], wi, bi_ref[...], sp)
            acum, bcum = _scan_fwd(a, mult * (i * xc))
            h = acum * hprev + bcum
            hs_ref[pl.ds(r0, tc), :] = h
            ge, _ = _gelu(g_ref[pl.ds(r0, tc), :].astype(F32))
            y_ref[pl.ds(r0, tc), :] = (h * ge).astype(BF16)
            return h[tc - 1:tc, :]

        lax.fori_loop(0, nch, chunk, jnp.zeros((1, C), F32))

    col, vec, cwspec, w, own = _rnn_specs(T)
    return pl.pallas_call(
        body, name=name, grid=(RNN_BLOCKS,),
        in_specs=[col(C_XR), col(C_GR), cwspec, vec, w, vec, w, vec, vec],
        out_specs=[own, own],
        out_shape=[jax.ShapeDtypeStruct((T, D_MODEL), BF16), jax.ShapeDtypeStruct((T, D_MODEL), F32)],
        scratch_shapes=[pltpu.VMEM((tc + 8, C), F32)],
        compiler_params=pltpu.CompilerParams(dimension_semantics=("parallel",)),
    )(P, P, cw, cb, wrg, brg, wig, big, lam)


def _rnn_bwd(P, hs, dy, cw, cb, wrg, brg, wig, big, lam, name):
    T = P.shape[0]
    C = RNN_BLOCK
    tc = _pick(T, (TIME_CHUNK,))
    nch = T // tc

    def body(x_ref, g_ref, hs_ref, dy_ref, cw_ref, cb_ref, wr_ref, br_ref, wi_ref, bi_ref, lam_ref,
             dx_ref, dg_ref, dcw_ref, dcb_ref, dwr_ref, dbr_ref, dwi_ref, dbi_ref, dlam_ref,
             xs_ref, hp_ref, an_ref, dn_ref):
        lam_v = lam_ref[...]
        sp = _softplus_neg(lam_v)
        wr = wr_ref[0]
        wi = wi_ref[0]
        dcw_ref[...] = jnp.zeros_like(dcw_ref)
        dcb_ref[...] = jnp.zeros_like(dcb_ref)
        dwr_ref[...] = jnp.zeros_like(dwr_ref)
        dbr_ref[...] = jnp.zeros_like(dbr_ref)
        dwi_ref[...] = jnp.zeros_like(dwi_ref)
        dbi_ref[...] = jnp.zeros_like(dbi_ref)
        dlam_ref[...] = jnp.zeros_like(dlam_ref)
        an_ref[tc:, :] = jnp.zeros((8, C), F32)
        dn_ref[tc:, :] = jnp.zeros((8, C), F32)

        def chunk(step, gnext):
            c = nch - 1 - step
            r0 = pl.multiple_of(c * tc, tc)
            p0 = pl.multiple_of(jnp.maximum(r0 - 8, 0), 8)
            q0 = pl.multiple_of(jnp.maximum(r0 - 16, 0), 16)
            live = c > 0
            x = x_ref[pl.ds(r0, tc), :].astype(F32)
            xs_ref[0:8, :] = jnp.where(live, x_ref[pl.ds(q0, 16), :].astype(F32)[8:, :], 0.0)
            xs_ref[8:, :] = x
            xsh = [xs_ref[pl.ds(8 - (CONV_WIDTH - 1 - k), tc), :] for k in range(CONV_WIDTH)]
            xc = cb_ref[...] + jnp.zeros((tc, C), F32)
            for k in range(CONV_WIDTH):
                xc = xc + xsh[k] * cw_ref[k:k + 1, :]
            r, i, a, om, mult = _rnn_gates(xc, wr, br_ref[...], wi, bi_ref[...], sp)
            h = hs_ref[pl.ds(r0, tc), :]
            hp_ref[0:8, :] = jnp.where(live, hs_ref[pl.ds(p0, 8), :], 0.0)
            hp_ref[8:, :] = h
            hm1 = hp_ref[pl.ds(7, tc), :]
            g = g_ref[pl.ds(r0, tc), :].astype(F32)
            ge, th = _gelu(g)
            dy = dy_ref[pl.ds(r0, tc), :]
            dg_ref[pl.ds(r0, tc), :] = (dy * h * _gelu_grad(g, th)).astype(BF16)
            an_ref[0:tc, :] = a
            coef = an_ref[pl.ds(1, tc), :]
            ccum, bcum = _scan_bwd(coef, dy * ge)
            G = bcum + ccum * gnext
            an_ref[tc:, :] = a[0:8, :]
            da = G * hm1
            ixc = i * xc
            dmult = G * ixc
            di = G * mult * xc
            dxc = G * mult * i
            dla = da * a - dmult * (1.0 - om) / mult
            dr = dla * (-LRU_C * sp)
            dlam_ref[...] += jnp.sum(dla * r, axis=0, keepdims=True)
            dzr = dr * r * (1.0 - r)
            dzi = di * i * (1.0 - i)
            dbr_ref[...] += jnp.sum(dzr, axis=0, keepdims=True)
            dbi_ref[...] += jnp.sum(dzi, axis=0, keepdims=True)
            xb = xc.astype(BF16)
            dzrb = dzr.astype(BF16)
            dzib = dzi.astype(BF16)
            dwr_ref[0] += lax.dot_general(xb, dzrb, _DIMS["tn"], preferred_element_type=F32)
            dwi_ref[0] += lax.dot_general(xb, dzib, _DIMS["tn"], preferred_element_type=F32)
            dxc = dxc + lax.dot_general(dzrb, wr, _DIMS["nt"], preferred_element_type=F32)
            dxc = dxc + lax.dot_general(dzib, wi, _DIMS["nt"], preferred_element_type=F32)
            dcb_ref[...] += jnp.sum(dxc, axis=0, keepdims=True)
            for k in range(CONV_WIDTH):
                dcw_ref[k:k + 1, :] += jnp.sum(dxc * xsh[k], axis=0, keepdims=True)
            dn_ref[0:tc, :] = dxc
            dx = jnp.zeros((tc, C), F32)
            for k in range(CONV_WIDTH):
                dx = dx + dn_ref[pl.ds(CONV_WIDTH - 1 - k, tc), :] * cw_ref[k:k + 1, :]
            dn_ref[tc:, :] = dxc[0:8, :]
            dx_ref[pl.ds(r0, tc), :] = dx.astype(BF16)
            return G[0:1, :]

        lax.fori_loop(0, nch, chunk, jnp.zeros((1, C), F32))
        dlam_ref[...] = dlam_ref[...] * (LRU_C * _sigmoid(-lam_v))

    col, vec, cwspec, w, own = _rnn_specs(T)
    vshape = jax.ShapeDtypeStruct((1, D_MODEL), F32)
    wshape = jax.ShapeDtypeStruct((RNN_BLOCKS, C, C), F32)
    return pl.pallas_call(
        body, name=name, grid=(RNN_BLOCKS,),
        in_specs=[col(C_XR), col(C_GR), own, own, cwspec, vec, w, vec, w, vec, vec],
        out_specs=[own, own, cwspec, vec, w, vec, w, vec, vec],
        out_shape=[jax.ShapeDtypeStruct((T, D_MODEL), BF16), jax.ShapeDtypeStruct((T, D_MODEL), BF16),
                   jax.ShapeDtypeStruct((CONV_WIDTH, D_MODEL), F32), vshape, wshape, vshape, wshape, vshape, vshape],
        scratch_shapes=[pltpu.VMEM((tc + 8, C), F32), pltpu.VMEM((tc + 8, C), F32),
                        pltpu.VMEM((tc + 8, C), F32), pltpu.VMEM((tc + 8, C), F32)],
        compiler_params=pltpu.CompilerParams(dimension_semantics=("parallel",)),
    )(P, P, hs, dy, cw, cb, wrg, brg, wig, big, lam)


def _rope_table(T):
    half = ROT_DIM // 2
    pos = jnp.arange(T, dtype=F32)
    inv_freq = ROPE_THETA ** (-jnp.arange(0, ROT_DIM, 2, dtype=F32) / ROT_DIM)
    ang = pos[:, None] * inv_freq[None, :]
    cos, sin = jnp.cos(ang), jnp.sin(ang)
    one = jnp.ones((T, HEAD_DIM - ROT_DIM), F32)
    zero = jnp.zeros((T, HEAD_DIM - ROT_DIM), F32)
    z8 = jnp.zeros((T, half), F32)
    c = jnp.concatenate([cos, cos, one], axis=1)
    a = jnp.concatenate([-sin, z8, zero], axis=1)
    b = jnp.concatenate([z8, sin, zero], axis=1)
    return jnp.stack([jnp.tile(c, (1, 2)), jnp.tile(a, (1, 2)), jnp.tile(b, (1, 2))])


def _rope(x, tab, sign):
    W = x.shape[1]
    rep = W // 128
    c = jnp.tile(tab[0], (1, rep)) if rep > 1 else tab[0]
    a = jnp.tile(tab[1], (1, rep)) if rep > 1 else tab[1]
    b = jnp.tile(tab[2], (1, rep)) if rep > 1 else tab[2]
    return x * c + sign * (pltpu.roll(x, W - ROT_DIM // 2, 1) * a + pltpu.roll(x, ROT_DIM // 2, 1) * b)


def _swa_mask(n):
    rows = lax.broadcasted_iota(jnp.int32, (GROUP * BLOCK, 2 * BLOCK), 0) & (BLOCK - 1)
    cols = lax.broadcasted_iota(jnp.int32, (GROUP * BLOCK, 2 * BLOCK), 1)
    return (cols > rows) & (cols <= rows + BLOCK) & ((n > 0) | (cols >= BLOCK))


def _swa_probs(qg, k2, sink, valid):
    s = lax.dot_general(qg, k2, _DIMS["nt"], preferred_element_type=F32) * (HEAD_DIM ** -0.5)
    s = jnp.where(valid, s, NEG_INF)
    m = jnp.maximum(jnp.max(s, axis=1, keepdims=True), sink)
    p = jnp.exp(s - m)
    ps = jnp.exp(sink - m)
    inv = 1.0 / (jnp.sum(p, axis=1, keepdims=True) + ps)
    return p * inv, ps * inv


def _swa_specs(T):
    nb = T // BLOCK
    qspec = pl.BlockSpec((BLOCK, D_MODEL), lambda n: (n, C_Q // D_MODEL))
    cur = lambda off: pl.BlockSpec((BLOCK, KV_WIDTH), lambda n, off=off: (n, off // KV_WIDTH))
    prev = lambda off: pl.BlockSpec((BLOCK, KV_WIDTH), lambda n, off=off: (jnp.maximum(n - 1, 0), off // KV_WIDTH))
    tcur = pl.BlockSpec((3, BLOCK, 128), lambda n: (0, n, 0))
    tprev = pl.BlockSpec((3, BLOCK, 128), lambda n: (0, jnp.maximum(n - 1, 0), 0))
    sink = pl.BlockSpec((N_KV_HEADS, GROUP * BLOCK, 1), lambda n: (0, 0, 0))
    own = pl.BlockSpec((BLOCK, D_MODEL), lambda n: (n, 0))
    return nb, qspec, cur, prev, tcur, tprev, sink, own


def _stack_heads(x, hk):
    return jnp.concatenate([x[:, (hk * GROUP + g) * HEAD_DIM:(hk * GROUP + g + 1) * HEAD_DIM] for g in range(GROUP)],
                           axis=0)


def _swa_fwd(P, tab, sink_col, name):
    T = P.shape[0]
    nb, qspec, cur, prev, tcur, tprev, sink, own = _swa_specs(T)

    def body(q_ref, kc_ref, kp_ref, vc_ref, vp_ref, tc_ref, tp_ref, sk_ref, o_ref):
        n = pl.program_id(0)
        valid = _swa_mask(n)
        q = _rope(q_ref[...].astype(F32), tc_ref[...], 1.0).astype(BF16)
        k2 = jnp.concatenate([_rope(kp_ref[...].astype(F32), tp_ref[...], 1.0),
                              _rope(kc_ref[...].astype(F32), tc_ref[...], 1.0)], axis=0).astype(BF16)
        v2 = jnp.concatenate([vp_ref[...], vc_ref[...]], axis=0).astype(BF16)
        parts = []
        for hk in range(N_KV_HEADS):
            sl = slice(hk * HEAD_DIM, (hk + 1) * HEAD_DIM)
            pn, _ = _swa_probs(_stack_heads(q, hk), k2[:, sl], sk_ref[hk], valid)
            og = jnp.dot(pn.astype(BF16), v2[:, sl], preferred_element_type=F32)
            parts += [og[g * BLOCK:(g + 1) * BLOCK, :] for g in range(GROUP)]
        o_ref[...] = jnp.concatenate(parts, axis=1).astype(BF16)

    return pl.pallas_call(
        body, name=name, grid=(nb,),
        in_specs=[qspec, cur(C_K), prev(C_K), cur(C_V), prev(C_V), tcur, tprev, sink],
        out_specs=own, out_shape=jax.ShapeDtypeStruct((T, D_MODEL), BF16),
        compiler_params=pltpu.CompilerParams(dimension_semantics=("parallel",)),
    )(P, P, P, P, P, tab, tab, sink_col)


def _swa_bwd(P, do, tab, sink_col, name):
    T = P.shape[0]
    nb, qspec, cur, prev, tcur, tprev, sink, own = _swa_specs(T)

    def body(q_ref, kc_ref, kp_ref, vc_ref, vp_ref, do_ref, tc_ref, tp_ref, sk_ref,
             dq_ref, dk_ref, dv_ref, ds_ref):
        n = pl.program_id(0)

        @pl.when(n == 0)
        def _():
            dk_ref[...] = jnp.zeros_like(dk_ref)
            dv_ref[...] = jnp.zeros_like(dv_ref)
            ds_ref[...] = jnp.zeros_like(ds_ref)

        valid = _swa_mask(n)
        tcur_v = tc_ref[...]
        tprev_v = tp_ref[...]
        q = _rope(q_ref[...].astype(F32), tcur_v, 1.0).astype(BF16)
        k2 = jnp.concatenate([_rope(kp_ref[...].astype(F32), tprev_v, 1.0),
                              _rope(kc_ref[...].astype(F32), tcur_v, 1.0)], axis=0).astype(BF16)
        v2 = jnp.concatenate([vp_ref[...], vc_ref[...]], axis=0).astype(BF16)
        dob = do_ref[...].astype(BF16)
        dq_parts = []
        dk_parts = []
        dv_parts = []
        for hk in range(N_KV_HEADS):
            sl = slice(hk * HEAD_DIM, (hk + 1) * HEAD_DIM)
            qg = _stack_heads(q, hk)
            dog = _stack_heads(dob, hk)
            pn, psn = _swa_probs(qg, k2[:, sl], sk_ref[hk], valid)
            dp = lax.dot_general(dog, v2[:, sl], _DIMS["nt"], preferred_element_type=F32)
            delta = jnp.sum(pn * dp, axis=1, keepdims=True)
            dsc = (pn * (dp - delta) * (HEAD_DIM ** -0.5)).astype(BF16)
            dsink = -psn * delta
            for g in range(GROUP):
                ds_ref[hk, g:g + 1, :] += jnp.broadcast_to(
                    jnp.sum(dsink[g * BLOCK:(g + 1) * BLOCK], axis=0, keepdims=True), (1, 128))
            dqg = jnp.dot(dsc, k2[:, sl], preferred_element_type=F32)
            dq_parts += [dqg[g * BLOCK:(g + 1) * BLOCK, :] for g in range(GROUP)]
            dk_parts.append(lax.dot_general(dsc, qg, _DIMS["tn"], preferred_element_type=F32))
            dv_parts.append(lax.dot_general(pn.astype(BF16), dog, _DIMS["tn"], preferred_element_type=F32))
        dq_ref[...] = _rope(jnp.concatenate(dq_parts, axis=1), tcur_v, -1.0).astype(BF16)
        dk2 = jnp.concatenate(dk_parts, axis=1)
        dv2 = jnp.concatenate(dv_parts, axis=1)
        c0 = pl.multiple_of(n * BLOCK, BLOCK)
        p0 = pl.multiple_of(jnp.maximum(n - 1, 0) * BLOCK, BLOCK)
        dk_ref[pl.ds(p0, BLOCK), :] += _rope(dk2[:BLOCK], tprev_v, -1.0)
        dv_ref[pl.ds(p0, BLOCK), :] += dv2[:BLOCK]
        dk_ref[pl.ds(c0, BLOCK), :] += _rope(dk2[BLOCK:], tcur_v, -1.0)
        dv_ref[pl.ds(c0, BLOCK), :] += dv2[BLOCK:]

    full = pl.BlockSpec((T, KV_WIDTH), lambda n: (0, 0))
    return pl.pallas_call(
        body, name=name, grid=(nb,),
        in_specs=[qspec, cur(C_K), prev(C_K), cur(C_V), prev(C_V), own, tcur, tprev, sink],
        out_specs=[own, full, full, pl.BlockSpec((N_KV_HEADS, GROUP, 128), lambda n: (0, 0, 0))],
        out_shape=[jax.ShapeDtypeStruct((T, D_MODEL), BF16), jax.ShapeDtypeStruct((T, KV_WIDTH), F32),
                   jax.ShapeDtypeStruct((T, KV_WIDTH), F32), jax.ShapeDtypeStruct((N_KV_HEADS, GROUP, 128), F32)],
        compiler_params=pltpu.CompilerParams(dimension_semantics=("arbitrary",)),
    )(P, P, P, P, P, do, tab, tab, sink_col)


_MW = 256


def _gate_specs(T, rows, width):
    tr = _pick(T, (rows, 256, 128, 64, 32, 16, 8))
    col = lambda off: pl.BlockSpec((tr, width), lambda i, j, off=off: (i, off // width + j))
    own = pl.BlockSpec((tr, width), lambda i, j: (i, j))
    return tr, col, own


def _merge_fwd(P, mr, ma, name):
    T = P.shape[0]
    tr, col, own = _gate_specs(T, 1024, _MW)

    def body(gr_ref, ga_ref, mr_ref, ma_ref, o_ref):
        o_ref[...] = (_sigmoid(gr_ref[...].astype(F32)) * mr_ref[...]
                      + _sigmoid(ga_ref[...].astype(F32)) * ma_ref[...]).astype(BF16)

    return pl.pallas_call(
        body, name=name, grid=(T // tr, D_MODEL // _MW), in_specs=[col(C_GRNN), col(C_GATTN), own, own],
        out_specs=own, out_shape=jax.ShapeDtypeStruct((T, D_MODEL), BF16),
        compiler_params=pltpu.CompilerParams(dimension_semantics=("parallel", "parallel")),
    )(P, P, mr, ma)


def _merge_bwd(P, mr, ma, dm, name):
    T = P.shape[0]
    tr, col, own = _gate_specs(T, 512, _MW)

    def body(gr_ref, ga_ref, mr_ref, ma_ref, dm_ref, dmr_ref, dma_ref, dgr_ref, dga_ref):
        dm = dm_ref[...]
        sr = _sigmoid(gr_ref[...].astype(F32))
        sa = _sigmoid(ga_ref[...].astype(F32))
        dmr_ref[...] = (dm * sr).astype(BF16)
        dma_ref[...] = (dm * sa).astype(BF16)
        dgr_ref[...] = (dm * mr_ref[...] * sr * (1.0 - sr)).astype(BF16)
        dga_ref[...] = (dm * ma_ref[...] * sa * (1.0 - sa)).astype(BF16)

    shp = jax.ShapeDtypeStruct((T, D_MODEL), BF16)
    return pl.pallas_call(
        body, name=name, grid=(T // tr, D_MODEL // _MW), in_specs=[col(C_GRNN), col(C_GATTN), own, own, own],
        out_specs=[own] * 4, out_shape=[shp] * 4,
        compiler_params=pltpu.CompilerParams(dimension_semantics=("parallel", "parallel")),
    )(P, P, mr, ma, dm)


_FFN_ROWS = 128


def _swiglu_fwd(U, name):
    T = U.shape[0]
    tr = _pick(T, (_FFN_ROWS, 64, 32, 16))
    half = lambda j: pl.BlockSpec((tr, D_FF), lambda i, j=j: (i, j))

    def body(g_ref, u_ref, o_ref):
        g = g_ref[...].astype(F32)
        o_ref[...] = (g * _sigmoid(g) * u_ref[...].astype(F32)).astype(BF16)

    return pl.pallas_call(
        body, name=name, grid=(T // tr,), in_specs=[half(0), half(1)],
        out_specs=half(0), out_shape=jax.ShapeDtypeStruct((T, D_FF), BF16),
        compiler_params=pltpu.CompilerParams(dimension_semantics=("parallel",)),
    )(U, U)


def _swiglu_bwd(U, dact, name):
    T = U.shape[0]
    tr = _pick(T, (_FFN_ROWS, 64, 32, 16))
    half = lambda j: pl.BlockSpec((tr, D_FF), lambda i, j=j: (i, j))

    def body(g_ref, u_ref, da_ref, o_ref):
        g = g_ref[...].astype(F32)
        da = da_ref[...].astype(F32)
        s = _sigmoid(g)
        o_ref[:, :D_FF] = (da * u_ref[...].astype(F32) * s * (1.0 + g * (1.0 - s))).astype(BF16)
        o_ref[:, D_FF:] = (da * g * s).astype(BF16)

    return pl.pallas_call(
        body, name=name, grid=(T // tr,), in_specs=[half(0), half(1), half(0)],
        out_specs=pl.BlockSpec((tr, 2 * D_FF), lambda i: (i, 0)),
        out_shape=jax.ShapeDtypeStruct((T, 2 * D_FF), BF16),
        compiler_params=pltpu.CompilerParams(dimension_semantics=("parallel",)),
    )(U, U, dact)


def _cross_probs(qh, kh):
    s = lax.dot_general(qh, kh, _DIMS["nt"], preferred_element_type=F32) * (CROSS_HEAD_DIM ** -0.5)
    p = jnp.exp(s - jnp.max(s, axis=1, keepdims=True))
    return p / jnp.sum(p, axis=1, keepdims=True)


def _cross_fwd(q, kv, name):
    T = q.shape[0]
    M = kv.shape[0]
    tr = _pick(T, (ROW_TILE, 128, 64, 32, 16, 8))
    W = CROSS_HEAD_DIM

    def body(q_ref, kv_ref, o_ref):
        for h in range(CROSS_HEADS):
            qh = q_ref[:, h * W:(h + 1) * W].astype(BF16)
            kh = kv_ref[:, h * W:(h + 1) * W].astype(BF16)
            vh = kv_ref[:, D_MODEL + h * W:D_MODEL + (h + 1) * W].astype(BF16)
            pn = _cross_probs(qh, kh)
            o_ref[:, h * W:(h + 1) * W] = jnp.dot(pn.astype(BF16), vh, preferred_element_type=F32).astype(BF16)

    row = pl.BlockSpec((tr, D_MODEL), lambda i: (i, 0))
    return pl.pallas_call(
        body, name=name, grid=(T // tr,), in_specs=[row, pl.BlockSpec((M, 2 * D_MODEL), lambda i: (0, 0))],
        out_specs=row, out_shape=jax.ShapeDtypeStruct((T, D_MODEL), BF16),
        compiler_params=pltpu.CompilerParams(dimension_semantics=("parallel",)),
    )(q, kv)


def _cross_bwd(q, kv, do, name):
    T = q.shape[0]
    M = kv.shape[0]
    tr = _pick(T, (ROW_TILE, 128, 64, 32, 16, 8))
    W = CROSS_HEAD_DIM

    def body(q_ref, kv_ref, do_ref, dq_ref, dkv_ref):
        @pl.when(pl.program_id(0) == 0)
        def _():
            dkv_ref[...] = jnp.zeros_like(dkv_ref)

        for h in range(CROSS_HEADS):
            qh = q_ref[:, h * W:(h + 1) * W].astype(BF16)
            kh = kv_ref[:, h * W:(h + 1) * W].astype(BF16)
            vh = kv_ref[:, D_MODEL + h * W:D_MODEL + (h + 1) * W].astype(BF16)
            doh = do_ref[:, h * W:(h + 1) * W].astype(BF16)
            pn = _cross_probs(qh, kh)
            dp = lax.dot_general(doh, vh, _DIMS["nt"], preferred_element_type=F32)
            delta = jnp.sum(pn * dp, axis=1, keepdims=True)
            dsc = (pn * (dp - delta) * (W ** -0.5)).astype(BF16)
            dq_ref[:, h * W:(h + 1) * W] = jnp.dot(dsc, kh, preferred_element_type=F32).astype(BF16)
            dkv_ref[:, h * W:(h + 1) * W] += lax.dot_general(dsc, qh, _DIMS["tn"], preferred_element_type=F32)
            dkv_ref[:, D_MODEL + h * W:D_MODEL + (h + 1) * W] += lax.dot_general(
                pn.astype(BF16), doh, _DIMS["tn"], preferred_element_type=F32)

    row = pl.BlockSpec((tr, D_MODEL), lambda i: (i, 0))
    full = pl.BlockSpec((M, 2 * D_MODEL), lambda i: (0, 0))
    return pl.pallas_call(
        body, name=name, grid=(T // tr,), in_specs=[row, full, row], out_specs=[row, full],
        out_shape=[jax.ShapeDtypeStruct((T, D_MODEL), BF16), jax.ShapeDtypeStruct((M, 2 * D_MODEL), F32)],
        compiler_params=pltpu.CompilerParams(dimension_semantics=("arbitrary",)),
    )(q, kv, do)


def _loss_head(y, target, name):
    T, D = y.shape
    tr = _pick(T, (ROW_TILE, 128, 64, 32, 16, 8))

    def body(y_ref, t_ref, l_ref, dy_ref):
        @pl.when(pl.program_id(0) == 0)
        def _():
            l_ref[...] = jnp.zeros_like(l_ref)

        err = y_ref[...] - t_ref[...]
        dy_ref[...] = err * (1.0 / D)
        l_ref[...] += jnp.broadcast_to(0.5 * jnp.sum(jnp.mean(err * err, axis=-1, keepdims=True), axis=0, keepdims=True),
                                       (8, 128))

    row = pl.BlockSpec((tr, D), lambda i: (i, 0))
    return pl.pallas_call(
        body, name=name, grid=(T // tr,), in_specs=[row, row],
        out_specs=[pl.BlockSpec((8, 128), lambda i: (0, 0)), row],
        out_shape=[jax.ShapeDtypeStruct((8, 128), F32), jax.ShapeDtypeStruct((T, D), F32)],
        compiler_params=pltpu.CompilerParams(dimension_semantics=("arbitrary",)),
    )(y, target)


def _sum_slots(recv, name):
    _, R, C = recv.shape
    tr = _pick(R, (ROW_TILE, 128, 64, 32, 16, 8))

    def body(r_ref, o_ref):
        acc = r_ref[0].astype(F32)
        for d in range(1, N_DEV):
            acc = acc + r_ref[d].astype(F32)
        o_ref[...] = acc

    return pl.pallas_call(
        body, name=name, grid=(R // tr,), in_specs=[pl.BlockSpec((N_DEV, tr, C), lambda i: (0, i, 0))],
        out_specs=pl.BlockSpec((tr, C), lambda i: (i, 0)), out_shape=jax.ShapeDtypeStruct((R, C), F32),
        compiler_params=pltpu.CompilerParams(dimension_semantics=("parallel",)),
    )(recv)


SUM_STEPS = 2


def _sum_blocks(recvs, sents, me, name):
    n = len(recvs)

    def body(me_ref, *refs):
        me = me_ref[0]
        for i in range(n):
            r_ref, s_ref, o_ref = refs[i], refs[n + i], refs[2 * n + i]
            acc = s_ref[0].astype(F32)
            for d in range(N_DEV):
                acc = acc + jnp.where(d == me, 0.0, r_ref[d].astype(F32))
            o_ref[...] = acc

    tiles = [r.shape[1] // SUM_STEPS for r in recvs]
    C = recvs[0].shape[2]
    return pl.pallas_call(
        body, name=name,
        grid_spec=pltpu.PrefetchScalarGridSpec(
            num_scalar_prefetch=1, grid=(SUM_STEPS,),
            in_specs=[pl.BlockSpec((N_DEV, t, C), lambda s, me_ref: (0, s, 0)) for t in tiles]
            + [pl.BlockSpec((1, t, C), lambda s, me_ref: (me_ref[0], s, 0)) for t in tiles],
            out_specs=[pl.BlockSpec((t, C), lambda s, me_ref: (s, 0)) for t in tiles]),
        out_shape=[jax.ShapeDtypeStruct((r.shape[1], C), F32) for r in recvs],
        compiler_params=pltpu.CompilerParams(dimension_semantics=("parallel",)),
    )(me, *recvs, *sents)


def _adamw(w, g, m, v, name):
    shape = w.shape
    C = shape[-1]
    R = math.prod(shape[:-1])
    w2, g2, m2, v2 = (t.reshape(R, C) for t in (w, g, m, v))
    tr = _pick(R, (ROW_TILE, 128, 64, 32, 16, 8))

    def body(w_ref, g_ref, m_ref, v_ref, d_ref, mo_ref, vo_ref):
        gg = g_ref[...]
        mn = ADAM_B1 * m_ref[...] + (1.0 - ADAM_B1) * gg
        vn = ADAM_B2 * v_ref[...] + (1.0 - ADAM_B2) * (gg * gg)
        m_hat = mn / (1.0 - ADAM_B1 ** ADAM_STEP)
        v_hat = vn / (1.0 - ADAM_B2 ** ADAM_STEP)
        d_ref[...] = -ADAM_LR * (m_hat / (jnp.sqrt(v_hat) + ADAM_EPS) + ADAM_WD * w_ref[...])
        mo_ref[...] = mn
        vo_ref[...] = vn

    blk = pl.BlockSpec((tr, C), lambda i: (i, 0))
    shp = jax.ShapeDtypeStruct((R, C), F32)
    d, mo, vo = pl.pallas_call(
        body, name=name, grid=(R // tr,), in_specs=[blk] * 4, out_specs=[blk] * 3, out_shape=[shp] * 3,
        compiler_params=pltpu.CompilerParams(dimension_semantics=("parallel",)),
    )(w2, g2, m2, v2)
    return d.reshape(shape), mo.reshape(shape), vo.reshape(shape)


def _all_gather_many(bufs, name):
    n = len(bufs)

    def body(*refs):
        xs, outs = refs[:n], refs[n:2 * n]
        send_sems, recv_sems, local_sems = refs[2 * n:]
        x, y, c = lax.axis_index("x"), lax.axis_index("y"), lax.axis_index("c")
        me, sibling = (x, y, c), (x, y, 1 - c)
        chips = [(1 - x, y), (x, 1 - y), (1 - x, 1 - y)]

        def slot(i, px, py, pc):
            return outs[i].at[4 * px + 2 * py + pc]

        def copy(i, k, block, to, src=None):
            return pltpu.make_async_remote_copy(
                src_ref=slot(i, *block) if src is None else src, dst_ref=slot(i, *block),
                send_sem=send_sems.at[7 * i + k], recv_sem=recv_sems.at[7 * i + k],
                device_id=to, device_id_type=pl.DeviceIdType.MESH)

        mine = [pltpu.make_async_copy(xs[i], slot(i, *me), local_sems.at[i]) for i in range(n)]
        for cp in mine:
            cp.start()
        first = [copy(i, 0, me, sibling, src=xs[i]) for i in range(n)]
        for j, chip in enumerate(chips):
            first += [copy(i, 1 + j, me, (*chip, c), src=xs[i]) for i in range(n)]
        for cp in first:
            cp.start()
        passed = []
        for j, chip in enumerate(chips):
            for i in range(n):
                copy(i, 1 + j, (*chip, c), me).wait_recv()
                passed.append(copy(i, 4 + j, (*chip, c), sibling))
                passed[-1].start()
        for i in range(n):
            copy(i, 0, sibling, me).wait_recv()
        for j, chip in enumerate(chips):
            for i in range(n):
                copy(i, 4 + j, (*chip, 1 - c), me).wait_recv()
        for cp in first + passed:
            cp.wait_send()
        for cp in mine:
            cp.wait()

    hbm = pl.BlockSpec(memory_space=pl.ANY)
    return pl.pallas_call(
        body, name=name, out_shape=[jax.ShapeDtypeStruct((N_DEV,) + b.shape, b.dtype) for b in bufs],
        in_specs=[hbm] * n, out_specs=[hbm] * n,
        scratch_shapes=[pltpu.SemaphoreType.DMA((7 * n,)), pltpu.SemaphoreType.DMA((7 * n,)),
                        pltpu.SemaphoreType.DMA((n,))],
    )(*bufs)


def _all_gather(buf, name):
    return _all_gather_many([buf], name)[0]


_HBM = pl.BlockSpec(memory_space=pltpu.HBM)
_SEM = pl.BlockSpec(memory_space=pltpu.SEMAPHORE)
_EFFECT = pltpu.SideEffectType.DATAFLOW_SIDE_EFFECTING


def _peer(k):
    x, y, c = lax.axis_index("x"), lax.axis_index("y"), lax.axis_index("c")
    return x ^ ((k >> 2) & 1), y ^ ((k >> 1) & 1), c ^ (k & 1)


def _my_slot():
    return 4 * lax.axis_index("x") + 2 * lax.axis_index("y") + lax.axis_index("c")


def _split_copy(src_refs, land_refs, send_sems, recv_sems, i, k):
    px, py, pc = _peer(k)
    return pltpu.make_async_remote_copy(
        src_ref=src_refs[i].at[4 * px + 2 * py + pc], dst_ref=land_refs[i].at[_my_slot()],
        send_sem=send_sems.at[7 * i + k - 1], recv_sem=recv_sems.at[7 * i + k - 1],
        device_id=(px, py, pc), device_id_type=pl.DeviceIdType.MESH)


def _split_start(srcs, lands, name):
    n = len(srcs)

    def body(*refs):
        src_refs, land_refs = refs[:n], refs[n:2 * n]
        send_sems, recv_sems = refs[2 * n], refs[2 * n + 1]
        token = refs[-1]
        for i in range(n):
            for k in range(1, N_DEV):
                _split_copy(src_refs, land_refs, send_sems, recv_sems, i, k).start()
        token[...] = jnp.zeros_like(token)

    outs = pl.pallas_call(
        body, name=name,
        out_shape=(pltpu.SemaphoreType.DMA((7 * n,)), pltpu.SemaphoreType.DMA((7 * n,)),
                   *[pltpu.HBM(a.shape, a.dtype) for a in srcs], *[pltpu.HBM(a.shape, a.dtype) for a in lands],
                   jax.ShapeDtypeStruct((8, 128), F32)),
        in_specs=[_HBM] * (2 * n),
        out_specs=(_SEM, _SEM, *([_HBM] * (2 * n)), pl.BlockSpec(memory_space=pltpu.VMEM)),
        input_output_aliases={i: 2 + i for i in range(2 * n)},
        compiler_params=pltpu.CompilerParams(has_side_effects=_EFFECT),
    )(*[pltpu.with_memory_space_constraint(a, pltpu.HBM) for a in list(srcs) + list(lands)])
    return outs[0], outs[1], outs[2:2 + n], outs[2 + n:2 + 2 * n], outs[-1]


def _split_wait(send_sems, recv_sems, srcs, lands, after, name):
    n = len(srcs)

    def body(*refs):
        src_refs, land_refs = refs[:n], refs[n:2 * n]
        ssem, rsem = refs[2 * n], refs[2 * n + 1]
        for i in range(n):
            for k in range(1, N_DEV):
                cp = _split_copy(src_refs, land_refs, ssem, rsem, i, k)
                cp.wait_send()
                cp.wait_recv()

    outs = pl.pallas_call(
        body, name=name,
        out_shape=(*[pltpu.HBM(a.shape, a.dtype) for a in srcs], *[pltpu.HBM(a.shape, a.dtype) for a in lands]),
        in_specs=[*([_HBM] * (2 * n)), _SEM, _SEM, pl.BlockSpec(memory_space=pl.ANY)],
        out_specs=tuple([_HBM] * (2 * n)),
        input_output_aliases={i: i for i in range(2 * n)},
        compiler_params=pltpu.CompilerParams(has_side_effects=_EFFECT),
    )(*srcs, *lands, send_sems, recv_sems, after)
    return outs[:n], outs[n:]


def _gather_first(src_refs, land_refs, send_sems, recv_sems, i, k):
    x, y, c = lax.axis_index("x"), lax.axis_index("y"), lax.axis_index("c")
    to = ((x, y, 1 - c), (1 - x, y, c), (x, 1 - y, c), (1 - x, 1 - y, c))[k]
    return pltpu.make_async_remote_copy(
        src_ref=src_refs[i], dst_ref=land_refs[i].at[_my_slot()],
        send_sem=send_sems.at[4 * i + k], recv_sem=recv_sems.at[4 * i + k],
        device_id=to, device_id_type=pl.DeviceIdType.MESH)


def _gather_second(land_refs, send_sems, recv_sems, i, j):
    x, y, c = lax.axis_index("x"), lax.axis_index("y"), lax.axis_index("c")
    px, py = ((1 - x, y), (x, 1 - y), (1 - x, 1 - y))[j]
    slot = land_refs[i].at[4 * px + 2 * py + c]
    return pltpu.make_async_remote_copy(
        src_ref=slot, dst_ref=slot, send_sem=send_sems.at[3 * i + j], recv_sem=recv_sems.at[3 * i + j],
        device_id=(x, y, 1 - c), device_id_type=pl.DeviceIdType.MESH)


def _gather_start(srcs, lands, name, after):
    n = len(srcs)

    def body(*refs):
        src_refs, land_refs = refs[:n], refs[n:2 * n]
        send_sems, recv_sems = refs[2 * n + 1], refs[2 * n + 2]
        token = refs[-1]
        for k in range(4):
            for i in range(n):
                _gather_first(src_refs, land_refs, send_sems, recv_sems, i, k).start()
        token[...] = jnp.zeros_like(token)

    outs = pl.pallas_call(
        body, name=name,
        out_shape=(pltpu.SemaphoreType.DMA((4 * n,)), pltpu.SemaphoreType.DMA((4 * n,)),
                   *[pltpu.HBM(a.shape, a.dtype) for a in srcs], *[pltpu.HBM(a.shape, a.dtype) for a in lands],
                   jax.ShapeDtypeStruct((8, 128), F32)),
        in_specs=[_HBM] * (2 * n) + [pl.BlockSpec(memory_space=pl.ANY)],
        out_specs=(_SEM, _SEM, *([_HBM] * (2 * n)), pl.BlockSpec(memory_space=pltpu.VMEM)),
        input_output_aliases={i: 2 + i for i in range(2 * n)},
        compiler_params=pltpu.CompilerParams(has_side_effects=_EFFECT),
    )(*[pltpu.with_memory_space_constraint(a, pltpu.HBM) for a in list(srcs) + list(lands)], after)
    return outs[0], outs[1], outs[2:2 + n], outs[2 + n:2 + 2 * n], outs[-1]


def _gather_forward(send1, recv1, srcs, lands, after, name):
    n = len(srcs)

    def body(*refs):
        src_refs, land_refs = refs[:n], refs[n:2 * n]
        s1, r1 = refs[2 * n], refs[2 * n + 1]
        s2, r2 = refs[2 * n + 3], refs[2 * n + 4]
        token = refs[-1]
        for j in range(3):
            for i in range(n):
                _gather_first(src_refs, land_refs, s1, r1, i, 1 + j).wait_recv()
                _gather_second(land_refs, s2, r2, i, j).start()
        for i in range(n):
            _gather_first(src_refs, land_refs, s1, r1, i, 0).wait_recv()
            for k in range(4):
                _gather_first(src_refs, land_refs, s1, r1, i, k).wait_send()
        token[...] = jnp.zeros_like(token)

    outs = pl.pallas_call(
        body, name=name,
        out_shape=(pltpu.SemaphoreType.DMA((3 * n,)), pltpu.SemaphoreType.DMA((3 * n,)),
                   *[pltpu.HBM(a.shape, a.dtype) for a in srcs], *[pltpu.HBM(a.shape, a.dtype) for a in lands],
                   jax.ShapeDtypeStruct((8, 128), F32)),
        in_specs=[*([_HBM] * (2 * n)), _SEM, _SEM, pl.BlockSpec(memory_space=pl.ANY)],
        out_specs=(_SEM, _SEM, *([_HBM] * (2 * n)), pl.BlockSpec(memory_space=pltpu.VMEM)),
        input_output_aliases={i: 2 + i for i in range(2 * n)},
        compiler_params=pltpu.CompilerParams(has_side_effects=_EFFECT),
    )(*srcs, *lands, send1, recv1, after)
    return outs[0], outs[1], outs[2:2 + n], outs[2 + n:2 + 2 * n], outs[-1]


def _gather_wait(send2, recv2, srcs, lands, after, name):
    n = len(srcs)

    def body(*refs):
        land_refs = refs[n:2 * n]
        s2, r2 = refs[2 * n], refs[2 * n + 1]
        for i in range(n):
            for j in range(3):
                cp = _gather_second(land_refs, s2, r2, i, j)
                cp.wait_send()
                cp.wait_recv()

    outs = pl.pallas_call(
        body, name=name,
        out_shape=(*[pltpu.HBM(a.shape, a.dtype) for a in srcs], *[pltpu.HBM(a.shape, a.dtype) for a in lands]),
        in_specs=[*([_HBM] * (2 * n)), _SEM, _SEM, pl.BlockSpec(memory_space=pl.ANY)],
        out_specs=tuple([_HBM] * (2 * n)),
        input_output_aliases={i: i for i in range(2 * n)},
        compiler_params=pltpu.CompilerParams(has_side_effects=_EFFECT),
    )(*srcs, *lands, send2, recv2, after)
    return outs[:n], outs[n:]


def _row(a, l):
    return a[l:l + 1]


def _tie(a, token):
    return a if token is None else a + token[0, 0]


def _layer_fwd(h, mem, W, l, tab, after, at):
    s = {}
    n = f"l{l}_"
    s["h0"] = h
    P = _matmul(h, W["w_in"][l], "nt", n + "proj", after=after, out_dtype=BF16)
    s["P"] = P
    sink_col = jnp.repeat(W["sinks"][l].reshape(N_KV_HEADS, GROUP), BLOCK, axis=1)[:, :, None]
    s["sink_col"] = sink_col
    y_rnn, hs = _rnn_fwd(P, W["conv_w"][l], _row(W["conv_b"], l), W["w_rg"][l], _row(W["b_rg"], l),
                         W["w_ig"][l], _row(W["b_ig"], l), _row(W["lru_lambda"], l), n + "rnn_fwd")
    y_attn = _swa_fwd(P, tab, _tie(sink_col, at(l, "proj", P)), n + "swa_fwd")
    at(l, "attn", y_attn)
    mr = _matmul(y_rnn, W["w_br_rnn"][l], "nn", n + "br_rnn")
    ma = _matmul(y_attn, W["w_br_attn"][l], "nn", n + "br_attn")
    merged = _merge_fwd(P, mr, ma, n + "merge_fwd")
    h1, xh1, rs1 = _matmul_ln(merged, W["w_out"][l], h, _tie(_row(W["ln1_g"], l), at(l, "mix", merged)),
                              _row(W["ln1_b"], l), n + "w_out_ln1")
    at(l, "ln1", h1)
    s.update(hs=hs, y_rnn=y_rnn, y_attn=y_attn, mr=mr, ma=ma, merged=merged, xh1=xh1, rs1=rs1, h1=h1)

    qc = _matmul(h1, W["cq_w"][l], "nn", n + "cq", out_dtype=BF16)
    kv = _matmul(mem, W["ckv_w"][l], "nt", n + "ckv", out_dtype=BF16)
    oc = _cross_fwd(qc, kv, n + "cross_fwd")
    h2, xh2, rs2 = _matmul_ln(oc, W["co_w"][l], h1, _row(W["ln2_g"], l), _row(W["ln2_b"], l), n + "co_ln2")
    s.update(qc=qc, kv=kv, oc=oc, xh2=xh2, rs2=rs2, h2=h2)

    U = _matmul(h2, W["ffn_wi"][l], "nt", n + "ffn_wi", after=at(l, "ln2", h2), out_dtype=BF16)
    act = _swiglu_fwd(U, n + "swiglu_fwd")
    h3, xh3, rs3 = _matmul_ln(act, W["ffn_wo"][l], h2, _tie(_row(W["ln3_g"], l), at(l, "ffn", act)),
                              _row(W["ln3_b"], l), n + "ffn_wo_ln3")
    s.update(U=U, act=act, xh3=xh3, rs3=rs3)
    return h3, s


GRAD_PARTS = (("ffn_wi", "ffn_wo", "cq_w", "ckv_w", "co_w"), ("w_out", "w_br_rnn", "w_br_attn"),
              ("w_in", "w_rg", "w_ig"))


def _layer_bwd(dh3, mem, W, l, tab, s, send):
    n = f"l{l}_"
    g = {}
    dz3, g["ln3_g"], g["ln3_b"] = _ln_bwd(dh3, s["xh3"], s["rs3"], _row(W["ln3_g"], l), n + "ln3_bwd")
    g["ffn_wo"] = _matmul(s["act"], dz3, "tn", n + "d_ffn_wo", out_dtype=BF16)
    dact = _matmul(dz3, W["ffn_wo"][l], "nt", n + "d_act", out_dtype=BF16)
    dU = _swiglu_bwd(s["U"], dact, n + "swiglu_bwd")
    g["ffn_wi"] = _matmul(dU, s["h2"], "tn", n + "d_ffn_wi", out_dtype=BF16)
    dh2 = _matmul(dU, W["ffn_wi"][l], "nn", n + "d_h2", add=dz3, add_scale=ALPHA)
    dz2, g["ln2_g"], g["ln2_b"] = _ln_bwd(dh2, s["xh2"], s["rs2"], _row(W["ln2_g"], l), n + "ln2_bwd")
    g["co_w"] = _matmul(s["oc"], dz2, "tn", n + "d_co", out_dtype=BF16)
    doc = _matmul(dz2, W["co_w"][l], "nt", n + "d_oc", out_dtype=BF16)
    dqc, dkv = _cross_bwd(s["qc"], s["kv"], doc, n + "cross_bwd")
    g["ckv_w"] = _matmul(dkv, mem, "tn", n + "d_ckv", out_dtype=BF16)
    g["cq_w"] = _matmul(s["h1"], dqc, "tn", n + "d_cq", out_dtype=BF16)
    after = send(l, 0, g)
    dh1 = _matmul(dqc, W["cq_w"][l], "nt", n + "d_h1", add=dz2, add_scale=ALPHA, after=after)
    dz1, g["ln1_g"], g["ln1_b"] = _ln_bwd(dh1, s["xh1"], s["rs1"], _row(W["ln1_g"], l), n + "ln1_bwd")
    g["w_out"] = _matmul(s["merged"], dz1, "tn", n + "d_w_out", out_dtype=BF16)
    dmerged = _matmul(dz1, W["w_out"][l], "nt", n + "d_merged")
    dmr, dma, dgrnn, dgattn = _merge_bwd(s["P"], s["mr"], s["ma"], dmerged, n + "merge_bwd")
    g["w_br_rnn"] = _matmul(s["y_rnn"], dmr, "tn", n + "d_br_rnn", out_dtype=BF16)
    g["w_br_attn"] = _matmul(s["y_attn"], dma, "tn", n + "d_br_attn", out_dtype=BF16)
    after = send(l, 1, g)
    dy_rnn = _matmul(dmr, W["w_br_rnn"][l], "nt", n + "d_y_rnn", after=after)
    dy_attn = _matmul(dma, W["w_br_attn"][l], "nt", n + "d_y_attn", out_dtype=BF16)
    dxr, dgr, g["conv_w"], g["conv_b"], g["w_rg"], g["b_rg"], g["w_ig"], g["b_ig"], g["lru_lambda"] = _rnn_bwd(
        s["P"], s["hs"], dy_rnn, W["conv_w"][l], _row(W["conv_b"], l), W["w_rg"][l], _row(W["b_rg"], l),
        W["w_ig"][l], _row(W["b_ig"], l), _row(W["lru_lambda"], l), n + "rnn_bwd")
    dq, dk, dv, dsk = _swa_bwd(s["P"], dy_attn, tab, s["sink_col"], n + "swa_bwd")
    g["sinks"] = dsk[:, :, 0].reshape(1, N_Q_HEADS)
    dP = jnp.concatenate([dxr, dgr, dq, dk.astype(BF16), dv.astype(BF16), dgrnn, dgattn], axis=1)
    g["w_in"] = _matmul(dP, s["h0"], "tn", n + "d_w_in", out_dtype=BF16)
    after = send(l, 2, g)
    dh = _matmul(dP, W["w_in"][l], "nn", n + "d_h0", add=dz1, add_scale=ALPHA, after=after)
    return dh, g


def _local_step(x, mem, target, W, at, send):
    T = x.shape[0]
    tab = _rope_table(T)
    h = x
    saved = []
    for l in range(DEPTH):
        after = at(l, "start", h)
        h, s = _layer_fwd(h, mem, W, l, tab, after, at)
        saved.append(s)
    lblk, dh = _loss_head(h, target, "loss_head")
    grads = [None] * DEPTH
    for l in reversed(range(DEPTH)):
        dh, grads[l] = _layer_bwd(dh, mem, W, l, tab, saved[l], send)
    return lblk[0, 0], dh, grads


COL_SHARDED = ("w_in", "ckv_w", "ffn_wi")
GATE_MATS = ("w_rg", "w_ig")


def _shard_rows(shards, l):
    out = []
    for n, r in PACK_ROWS:
        a = shards[n][l].astype(BF16)
        if n in COL_SHARDED:
            a = a.T
        elif n in GATE_MATS:
            a = a.reshape(RNN_BLOCKS * RNN_BLOCK // N_DEV, RNN_BLOCK)
        out.append(a)
    return out


def _full_weight(G, name):
    if name in GATE_MATS:
        return jnp.transpose(G.reshape(N_DEV, RNN_BLOCKS, RNN_BLOCK // N_DEV, RNN_BLOCK), (1, 0, 2, 3)).reshape(
            RNN_BLOCKS, RNN_BLOCK, RNN_BLOCK)
    return G.reshape(N_DEV * G.shape[1], G.shape[2])


SHARD_ROWS = dict(PACK_ROWS)


def _owner_blocks(g, names):
    out = []
    for name in names:
        a = g[name]
        if name in GATE_MATS:
            a = jnp.transpose(a.astype(BF16).reshape(RNN_BLOCKS, N_DEV, RNN_BLOCK // N_DEV, RNN_BLOCK), (1, 0, 2, 3))
        out.append(a.reshape(N_DEV, SHARD_ROWS[name], D_MODEL))
    return out


def _pack_small(g):
    rows = [g["conv_w"]]
    for nme in SMALL_NAMES:
        a = g[nme]
        if nme == "sinks":
            a = jnp.pad(a, ((0, 0), (0, D_MODEL - N_Q_HEADS)))
        rows.append(a)
    rows.append(jnp.zeros((SMALL_ROWS - CONV_WIDTH - len(SMALL_NAMES), D_MODEL), F32))
    return jnp.concatenate(rows, axis=0)


_SHARD_SHAPES = {"w_in": (1024, 672), "w_br_rnn": (128, 1024), "w_br_attn": (128, 1024), "w_out": (128, 1024),
                 "cq_w": (128, 1024), "ckv_w": (1024, 256), "co_w": (128, 1024), "ffn_wi": (1024, 704),
                 "ffn_wo": (352, 1024), "w_rg": (4, 32, 256), "w_ig": (4, 32, 256)}

LAYER0_GROUPS = (("w_in", "w_rg", "w_ig"), ("w_br_rnn", "w_br_attn", "w_out"),
                 ("cq_w", "ckv_w", "co_w", "ffn_wi", "ffn_wo"))
LAYER0_FORWARD_AT = {"proj": 1, "mix": 2}
LAYER0_WAIT_AT = {"attn": 1, "ln1": 2}
LAYER1_FORWARD_AT = {"mix": 2}
LAYER1_WAIT_AT = {"ln1": 2}
NEXT_LAYER_FORWARD_AT = ("ffn", "ln2", "ln2", None)

WEIGHT_NAMES = ("w_in", "conv_w", "conv_b", "w_rg", "b_rg", "w_ig", "b_ig", "lru_lambda", "w_br_rnn", "w_br_attn",
                "sinks", "w_out", "ln1_g", "ln1_b", "cq_w", "ckv_w", "co_w", "ln2_g", "ln2_b", "ffn_wi", "ffn_wo",
                "ln3_g", "ln3_b")


def kernel(x, mem, w_in, conv_w, conv_b, w_rg, b_rg, w_ig, b_ig, lru_lambda, w_br_rnn, w_br_attn, sinks, w_out, ln1_g, ln1_b, cq_w, ckv_w, co_w, ln2_g, ln2_b, ffn_wi, ffn_wo, ln3_g, ln3_b, loss_target, m_w_in, m_conv_w, m_conv_b, m_w_rg, m_b_rg, m_w_ig, m_b_ig, m_lru_lambda, m_w_br_rnn, m_w_br_attn, m_sinks, m_w_out, m_ln1_g, m_ln1_b, m_cq_w, m_ckv_w, m_co_w, m_ln2_g, m_ln2_b, m_ffn_wi, m_ffn_wo, m_ln3_g, m_ln3_b, v_w_in, v_conv_w, v_conv_b, v_w_rg, v_b_rg, v_w_ig, v_b_ig, v_lru_lambda, v_w_br_rnn, v_w_br_attn, v_sinks, v_w_out, v_ln1_g, v_ln1_b, v_cq_w, v_ckv_w, v_co_w, v_ln2_g, v_ln2_b, v_ffn_wi, v_ffn_wo, v_ln3_g, v_ln3_b):
    w = dict(w_in=w_in, conv_w=conv_w, conv_b=conv_b, w_rg=w_rg, b_rg=b_rg, w_ig=w_ig, b_ig=b_ig,
             lru_lambda=lru_lambda, w_br_rnn=w_br_rnn, w_br_attn=w_br_attn, sinks=sinks, w_out=w_out, ln1_g=ln1_g,
             ln1_b=ln1_b, cq_w=cq_w, ckv_w=ckv_w, co_w=co_w, ln2_g=ln2_g, ln2_b=ln2_b, ffn_wi=ffn_wi, ffn_wo=ffn_wo,
             ln3_g=ln3_g, ln3_b=ln3_b)
    m = dict(w_in=m_w_in, conv_w=m_conv_w, conv_b=m_conv_b, w_rg=m_w_rg, b_rg=m_b_rg, w_ig=m_w_ig, b_ig=m_b_ig,
             lru_lambda=m_lru_lambda, w_br_rnn=m_w_br_rnn, w_br_attn=m_w_br_attn, sinks=m_sinks, w_out=m_w_out,
             ln1_g=m_ln1_g, ln1_b=m_ln1_b, cq_w=m_cq_w, ckv_w=m_ckv_w, co_w=m_co_w, ln2_g=m_ln2_g, ln2_b=m_ln2_b,
             ffn_wi=m_ffn_wi, ffn_wo=m_ffn_wo, ln3_g=m_ln3_g, ln3_b=m_ln3_b)
    v = dict(w_in=v_w_in, conv_w=v_conv_w, conv_b=v_conv_b, w_rg=v_w_rg, b_rg=v_b_rg, w_ig=v_w_ig, b_ig=v_b_ig,
             lru_lambda=v_lru_lambda, w_br_rnn=v_w_br_rnn, w_br_attn=v_w_br_attn, sinks=v_sinks, w_out=v_w_out,
             ln1_g=v_ln1_g, ln1_b=v_ln1_b, cq_w=v_cq_w, ckv_w=v_ckv_w, co_w=v_co_w, ln2_g=v_ln2_g, ln2_b=v_ln2_b,
             ffn_wi=v_ffn_wi, ffn_wo=v_ffn_wo, ln3_g=v_ln3_g, ln3_b=v_ln3_b)
    my_dev = 4 * lax.axis_index("x") + 2 * lax.axis_index("y") + lax.axis_index("c")

    W = {n: [None] * DEPTH for n, _ in PACK_ROWS}
    shards0 = dict(zip([n for n, _ in PACK_ROWS], _shard_rows(w, 0)))
    *gathered, conv_all = _all_gather_many(
        [shards0[n] for n in LAYER0_GROUPS[0]] + [conv_w.reshape(DEPTH * CONV_WIDTH, D_MODEL // N_DEV)],
        "l0_gather_first")
    for n, G in zip(LAYER0_GROUPS[0], gathered):
        W[n][0] = _full_weight(G, n)
    W["conv_w"] = jnp.transpose(conv_all, (1, 0, 2)).reshape(DEPTH, CONV_WIDTH, D_MODEL)
    for n in SMALL_NAMES:
        W[n] = w[n]
    flying = {}
    token = conv_all
    groups = [((0, gi), LAYER0_GROUPS[gi], [shards0[n] for n in LAYER0_GROUPS[gi]]) for gi in (1, 2)]
    shards1 = dict(zip([n for n, _ in PACK_ROWS], _shard_rows(w, 1)))
    first1 = LAYER0_GROUPS[0] + LAYER0_GROUPS[1]
    groups += [((1, 0), first1, [shards1[n] for n in first1]),
               ((1, 2), LAYER0_GROUPS[2], [shards1[n] for n in LAYER0_GROUPS[2]])]
    groups += [((l, 0), [n for n, _ in PACK_ROWS], _shard_rows(w, l)) for l in range(2, DEPTH)]
    for key, names, srcs in groups:
        lands = [lax.empty((N_DEV,) + a.shape, a.dtype) for a in srcs]
        flying[key] = (names,) + _gather_start(srcs, lands, f"l{key[0]}_gather_start{key[1]}", token)
        token = flying[key][5]
    first_token = token

    def forward(key, after):
        names, send1, recv1, srcs, lands, _ = flying[key]
        flying[key] = (names,) + _gather_forward(send1, recv1, srcs, lands, after, f"l{key[0]}_gather_forward{key[1]}")
        return flying[key][5]

    def arrive(key, after):
        names, send2, recv2, srcs, lands, _ = flying.pop(key)
        srcs, lands = _gather_wait(send2, recv2, srcs, lands, after, f"l{key[0]}_gather_wait{key[1]}")
        for n, mine, G in zip(names, srcs, lands):
            W[n][key[0]] = _full_weight(lax.dynamic_update_index_in_dim(G, mine, my_dev, 0), n)

    def at(l, point, x):
        if point == "start":
            if l == 0:
                return first_token
            arrive((l, 0), x)
        elif l == 0 and point in LAYER0_FORWARD_AT:
            return forward((0, LAYER0_FORWARD_AT[point]), x)
        elif l == 0 and point in LAYER0_WAIT_AT:
            arrive((0, LAYER0_WAIT_AT[point]), x)
        elif l == 1 and point in LAYER1_FORWARD_AT:
            return forward((1, LAYER1_FORWARD_AT[point]), x)
        elif l == 1 and point in LAYER1_WAIT_AT:
            arrive((1, LAYER1_WAIT_AT[point]), x)
        elif point == NEXT_LAYER_FORWARD_AT[l]:
            return forward((l + 1, 0), x)
        return None

    summed = {}
    sent = {}
    me = my_dev.astype(jnp.int32).reshape(1)

    def finish(l, part, after):
        ssem, rsem, blocks, lands = sent.pop((l, part))
        blocks, recvs = _split_wait(ssem, rsem, blocks, lands, after, f"l{l}_exchange_wait{part}")
        for n, s in zip(GRAD_PARTS[part], _sum_blocks(recvs, blocks, me, f"l{l}_sum_grads{part}")):
            summed[(l, n)] = s

    def send(l, part, g):
        blocks = _owner_blocks(g, GRAD_PARTS[part])
        if (l + 1, part) in sent:
            finish(l + 1, part, blocks[0])
        ssem, rsem, blocks, lands, token = _split_start(blocks, [lax.empty(b.shape, b.dtype) for b in blocks],
                                                        f"l{l}_exchange_start{part}")
        sent[(l, part)] = (ssem, rsem, blocks, lands)
        return token

    loss_local, dx, layer_grads = _local_step(x[0], mem[0], loss_target[0], W, at, send)
    loss = lax.psum(loss_local, MESH_AXES)

    grads, deltas, new_m, new_v = {}, {}, {}, {}

    def update(n):
        if n in SHARD_ROWS:
            blk = jnp.stack([summed[(l, n)] for l in range(DEPTH)])
            grads[n] = jnp.transpose(blk, (0, 2, 1)) if n in COL_SHARDED else blk.reshape((DEPTH,) + _SHARD_SHAPES[n])
        deltas[n], new_m[n], new_v[n] = _adamw(w[n], grads[n], m[n], v[n], "adamw_" + n)

    early = GRAD_PARTS[0] + GRAD_PARTS[1]
    for part in (0, 1):
        finish(0, part, dx)
    for n in early:
        update(n)
    finish(0, 2, sum(deltas[n].reshape(-1)[:128] for n in early))
    for n in GRAD_PARTS[2]:
        update(n)

    small_all = _all_gather(jnp.concatenate([_pack_small(g) for g in layer_grads], axis=0), "gather_small_grads")
    small_sum = _sum_slots(small_all, "sum_small_grads").reshape(DEPTH, SMALL_ROWS, D_MODEL)
    conv_full = small_sum[:, :CONV_WIDTH, :]
    grads["conv_w"] = lax.dynamic_slice_in_dim(conv_full, my_dev * (D_MODEL // N_DEV), D_MODEL // N_DEV, axis=2)
    for i, n in enumerate(SMALL_NAMES):
        row = small_sum[:, CONV_WIDTH + i, :]
        grads[n] = row[:, :N_Q_HEADS] if n == "sinks" else row
    for n in ("conv_w",) + SMALL_NAMES:
        update(n)

    return (loss, dx[None], *[grads[n] for n in WEIGHT_NAMES], *[deltas[n] for n in WEIGHT_NAMES],
            *[new_m[n] for n in WEIGHT_NAMES], *[new_v[n] for n in WEIGHT_NAMES])
```

```python
import functools
import math

import jax
import jax.numpy as jnp
from jax import lax
from jax.experimental import pallas as pl
from jax.experimental.pallas import tpu as pltpu

F32 = jnp.float32
BF16 = jnp.bfloat16

D_MODEL = 1024
DEPTH = 4
N_DEV = 8
RNN_BLOCKS = 4
RNN_BLOCK = 256
CONV_WIDTH = 4
LRU_C = 8.0
HEAD_DIM = 64
N_Q_HEADS = 16
N_KV_HEADS = 2
GROUP = 8
KV_WIDTH = 128
BLOCK = 128
ROPE_THETA = 500000.0
ROT_DIM = 16
IN_COLS = 5376
CROSS_HEADS = 4
CROSS_HEAD_DIM = 256
D_FF = 2816
LN_EPS = 1e-5
ALPHA = (2 * DEPTH) ** 0.25
NEG_INF = -1e30

ADAM_LR = 0.001
ADAM_B1 = 0.9
ADAM_B2 = 0.999
ADAM_EPS = 1e-08
ADAM_WD = 0.01
ADAM_STEP = 10

C_XR, C_GR, C_Q, C_K, C_V, C_GRNN, C_GATTN = 0, 1024, 2048, 3072, 3200, 3328, 4352

TIME_CHUNK = 128
ROW_TILE = 256

MESH_AXES = ("x", "y", "c")

PACK_ROWS = (("w_in", 672), ("w_br_rnn", 128), ("w_br_attn", 128), ("w_out", 128), ("cq_w", 128),
             ("ckv_w", 256), ("co_w", 128), ("ffn_wi", 704), ("ffn_wo", 352), ("w_rg", 32), ("w_ig", 32))
SMALL_NAMES = ("conv_b", "b_rg", "b_ig", "lru_lambda", "sinks", "ln1_g", "ln1_b", "ln2_g", "ln2_b", "ln3_g", "ln3_b")
SMALL_ROWS = 16


def _pick(dim, cands):
    for c in cands:
        if dim % c == 0:
            return c
    return dim


_DIMS = {"nn": (((1,), (0,)), ((), ())), "nt": (((1,), (1,)), ((), ())), "tn": (((0,), (0,)), ((), ()))}

MATMUL_VMEM_BUDGET = 44 * 2 ** 20
MATMUL_MAX_TILE = 2048
MXU_DIM = 256
STEP_COST_BYTES = 500_000
MIN_ROW_TILE = 512


def _tile_candidates(dim, whole=False, step=MXU_DIM):
    c = [d for d in range(step, min(dim, MATMUL_MAX_TILE) + 1, step) if dim % d == 0]
    if whole and dim not in c:
        c.append(dim)
    return c or [dim]


def _matmul_tiles(M, N, K, sa, sb, so, has_add):
    best = None
    for tk in _tile_candidates(K, whole=True):
        nk = K // tk
        for tm in [t for t in _tile_candidates(M, step=128) if t >= min(M, MIN_ROW_TILE)]:
            for tn in _tile_candidates(N):
                vmem = 2 * (tm * tk * sa + tk * tn * sb + tm * tn * so) + tm * tn * 4
                vmem += tm * tn * 4 if nk > 1 else 0
                vmem += 2 * tm * tn * 4 if has_add else 0
                vmem += (tm * tk * 2 if sa == 4 else 0) + (tk * tn * 2 if sb == 4 else 0)
                if vmem > MATMUL_VMEM_BUDGET:
                    continue
                steps = (M // tm) * (N // tn) * nk
                exposed = tm * tk * sa + tk * tn * sb + tm * tn * so
                acc_moves = steps * tm * tn * 2 if nk > 1 else 0
                fixed = M * N * so + steps * STEP_COST_BYTES + exposed + acc_moves
                a_in = M * K * sa * ((N // tn) if nk > 1 else 1) + K * N * sb * (M // tm)
                b_in = M * K * sa * (N // tn) + K * N * sb * ((M // tm) if nk > 1 else 1)
                for cost, m_outer in ((a_in + fixed, True), (b_in + fixed, False)):
                    if best is None or cost < best[0]:
                        best = (cost, tm, tn, tk, m_outer)
    return best[1:]


def _matmul(a, b, mode, name, add=None, add_scale=1.0, out_dtype=F32, after=None):
    if mode == "nn":
        (M, K), (_, N) = a.shape, b.shape
    elif mode == "nt":
        (M, K), (N, _) = a.shape, b.shape
    else:
        (K, M), (_, N) = a.shape, b.shape
    tm, tn, tk, m_outer = _matmul_tiles(M, N, K, a.dtype.itemsize, b.dtype.itemsize, jnp.dtype(out_dtype).itemsize,
                                        add is not None)
    nk = K // tk
    dims = _DIMS[mode]

    def body(*refs):
        if after is not None:
            refs = refs[:-2 - (nk > 1)] + refs[-1 - (nk > 1):]
        a_ref, b_ref = refs[0], refs[1]
        c_ref = refs[2] if add is not None else None
        o_ref = refs[3] if add is not None else refs[2]

        def finish(r):
            if add is not None:
                r = r + add_scale * c_ref[...]
            o_ref[...] = r.astype(out_dtype)

        prod = lax.dot_general(a_ref[...].astype(BF16), b_ref[...].astype(BF16), dims, preferred_element_type=F32)
        if nk == 1:
            finish(prod)
            return
        acc_ref = refs[-1]
        k = pl.program_id(2)

        @pl.when(k == 0)
        def _():
            acc_ref[...] = prod

        @pl.when(k > 0)
        def _():
            acc_ref[...] += prod

        @pl.when(k == nk - 1)
        def _():
            finish(acc_ref[...])

    ij = (lambda p, q: (p, q)) if m_outer else (lambda p, q: (q, p))
    if mode == "nn":
        a_spec = pl.BlockSpec((tm, tk), lambda p, q, k: (ij(p, q)[0], k))
        b_spec = pl.BlockSpec((tk, tn), lambda p, q, k: (k, ij(p, q)[1]))
    elif mode == "nt":
        a_spec = pl.BlockSpec((tm, tk), lambda p, q, k: (ij(p, q)[0], k))
        b_spec = pl.BlockSpec((tn, tk), lambda p, q, k: (ij(p, q)[1], k))
    else:
        a_spec = pl.BlockSpec((tk, tm), lambda p, q, k: (k, ij(p, q)[0]))
        b_spec = pl.BlockSpec((tk, tn), lambda p, q, k: (k, ij(p, q)[1]))
    o_spec = pl.BlockSpec((tm, tn), lambda p, q, k: ij(p, q))
    o_shape = jax.ShapeDtypeStruct((M, N), out_dtype)
    in_specs = [a_spec, b_spec]
    args = [a, b]
    if add is not None:
        in_specs.append(o_spec)
        args.append(add)
    if after is not None:
        in_specs.append(pl.BlockSpec(memory_space=pl.ANY))
        args.append(after)
    return pl.pallas_call(
        body, name=name, grid=(M // tm, N // tn, nk) if m_outer else (N // tn, M // tm, nk),
        in_specs=in_specs, out_specs=o_spec, out_shape=o_shape,
        scratch_shapes=[pltpu.VMEM((tm, tn), F32)] if nk > 1 else [],
        compiler_params=pltpu.CompilerParams(dimension_semantics=("parallel", "parallel", "arbitrary")),
    )(*args)


LN_ROWS = 512


def _matmul_ln(a, w, h, g, b, name):
    T, K = a.shape
    D = w.shape[1]
    tr = _pick(T, (LN_ROWS, 256, 128, 64, 32, 16, 8))

    def body(a_ref, w_ref, h_ref, g_ref, b_ref, o_ref, xh_ref, rs_ref):
        f = jnp.dot(a_ref[...].astype(BF16), w_ref[...], preferred_element_type=F32)
        z = ALPHA * h_ref[...] + f
        mu = jnp.mean(z, axis=-1, keepdims=True)
        zc = z - mu
        var = jnp.mean(zc * zc, axis=-1, keepdims=True)
        rstd = lax.rsqrt(var + LN_EPS)
        xh = zc * rstd
        xh_ref[...] = xh
        rs_ref[...] = rstd
        o_ref[...] = xh * g_ref[...] + b_ref[...]

    row = pl.BlockSpec((tr, D), lambda i: (i, 0))
    vec = pl.BlockSpec((1, D), lambda i: (0, 0))
    return pl.pallas_call(
        body, name=name, grid=(T // tr,),
        in_specs=[pl.BlockSpec((tr, K), lambda i: (i, 0)), pl.BlockSpec((K, D), lambda i: (0, 0)), row, vec, vec],
        out_specs=[row, row, pl.BlockSpec((tr, 1), lambda i: (i, 0))],
        out_shape=[jax.ShapeDtypeStruct((T, D), F32), jax.ShapeDtypeStruct((T, D), F32),
                   jax.ShapeDtypeStruct((T, 1), F32)],
        compiler_params=pltpu.CompilerParams(dimension_semantics=("parallel",)),
    )(a, w, h, g, b)


def _ln_bwd(dout, xh, rstd, g, name):
    T, D = dout.shape
    tr = _pick(T, (LN_ROWS, 256, 128, 64, 32, 16, 8))

    def body(do_ref, xh_ref, rs_ref, g_ref, dz_ref, dg_ref, db_ref):
        @pl.when(pl.program_id(0) == 0)
        def _():
            dg_ref[...] = jnp.zeros_like(dg_ref)
            db_ref[...] = jnp.zeros_like(db_ref)

        do = do_ref[...]
        xh = xh_ref[...]
        dxh = do * g_ref[...]
        m1 = jnp.mean(dxh, axis=-1, keepdims=True)
        m2 = jnp.mean(dxh * xh, axis=-1, keepdims=True)
        dz_ref[...] = rs_ref[...] * (dxh - m1 - xh * m2)
        dg_ref[...] += jnp.sum(do * xh, axis=0, keepdims=True)
        db_ref[...] += jnp.sum(do, axis=0, keepdims=True)

    row = pl.BlockSpec((tr, D), lambda i: (i, 0))
    vec = pl.BlockSpec((1, D), lambda i: (0, 0))
    return pl.pallas_call(
        body, name=name, grid=(T // tr,),
        in_specs=[row, row, pl.BlockSpec((tr, 1), lambda i: (i, 0)), vec], out_specs=[row, vec, vec],
        out_shape=[jax.ShapeDtypeStruct((T, D), F32), jax.ShapeDtypeStruct((1, D), F32),
                   jax.ShapeDtypeStruct((1, D), F32)],
        compiler_params=pltpu.CompilerParams(dimension_semantics=("arbitrary",)),
    )(dout, xh, rstd, g)


_GELU_C = math.sqrt(2.0 / math.pi)


def _gelu(x):
    t = jnp.tanh(_GELU_C * (x + 0.044715 * x * x * x))
    return 0.5 * x * (1.0 + t), t


def _gelu_grad(x, t):
    return 0.5 * (1.0 + t) + 0.5 * x * (1.0 - t * t) * _GELU_C * (1.0 + 3 * 0.044715 * x * x)


def _sigmoid(x):
    return 1.0 / (1.0 + jnp.exp(-x))


def _softplus_neg(lam):
    z = jnp.exp(-jnp.abs(lam))
    u = 1.0 + z
    l1p = jnp.where(u == 1.0, z, jnp.log(u) * z / jnp.where(u == 1.0, 1.0, u - 1.0))
    return jnp.maximum(-lam, 0.0) + l1p


def _neg_expm1(x):
    series = x * (1.0 + x * 0.5 * (1.0 + x * (1.0 / 3.0) * (1.0 + x * 0.25 * (1.0 + x * 0.2))))
    return -jnp.where(x > -0.05, series, jnp.exp(x) - 1.0)


def _scan_fwd(a, b):
    n = a.shape[0]
    rows = lax.broadcasted_iota(jnp.int32, a.shape, 0)
    s = 1
    while s < n:
        keep = rows >= s
        b = jnp.where(keep, a * pltpu.roll(b, s, 0) + b, b)
        a = jnp.where(keep, a * pltpu.roll(a, s, 0), a)
        s *= 2
    return a, b


def _scan_bwd(c, b):
    n = c.shape[0]
    rows = lax.broadcasted_iota(jnp.int32, c.shape, 0)
    s = 1
    while s < n:
        keep = rows < n - s
        b = jnp.where(keep, c * pltpu.roll(b, n - s, 0) + b, b)
        c = jnp.where(keep, c * pltpu.roll(c, n - s, 0), c)
        s *= 2
    return c, b


def _rnn_gates(xc, wr, br, wi, bi, sp):
    xb = xc.astype(BF16)
    r = _sigmoid(jnp.dot(xb, wr, preferred_element_type=F32) + br)
    i = _sigmoid(jnp.dot(xb, wi, preferred_element_type=F32) + bi)
    la = -LRU_C * r * sp
    a = jnp.exp(la)
    om = _neg_expm1(2.0 * la)
    mult = jnp.sqrt(om)
    return r, i, a, om, mult


def _rnn_specs(T):
    C = RNN_BLOCK
    col = lambda off: pl.BlockSpec((T, C), lambda n, off=off: (0, off // C + n))
    vec = pl.BlockSpec((1, C), lambda n: (0, n))
    cw = pl.BlockSpec((CONV_WIDTH, C), lambda n: (0, n))
    w = pl.BlockSpec((1, C, C), lambda n: (n, 0, 0))
    own = pl.BlockSpec((T, C), lambda n: (0, n))
    return col, vec, cw, w, own


def _rnn_fwd(P, cw, cb, wrg, brg, wig, big, lam, name):
    T = P.shape[0]
    C = RNN_BLOCK
    tc = _pick(T, (TIME_CHUNK,))
    nch = T // tc

    def body(x_ref, g_ref, cw_ref, cb_ref, wr_ref, br_ref, wi_ref, bi_ref, lam_ref, y_ref, hs_ref, xs_ref):
        sp = _softplus_neg(lam_ref[...])
        wr = wr_ref[0]
        wi = wi_ref[0]
        xs_ref[0:8, :] = jnp.zeros((8, C), F32)

        def chunk(c, hprev):
            r0 = pl.multiple_of(c * tc, tc)
            x = x_ref[pl.ds(r0, tc), :].astype(F32)
            xs_ref[8:, :] = x
            xc = cb_ref[...] + jnp.zeros((tc, C), F32)
            for k in range(CONV_WIDTH):
                xc = xc + xs_ref[pl.ds(8 - (CONV_WIDTH - 1 - k), tc), :] * cw_ref[k:k + 1, :]
            xs_ref[0:8, :] = x[tc - 8:, :]
            r, i, a, om, mult = _rnn_gates(xc, wr, br_ref[...], wi, bi_ref[...], sp)
            acum, bcum = _scan_fwd(a, mult * (i * xc))
            h = acum * hprev + bcum
            hs_ref[pl.ds(r0, tc), :] = h
            ge, _ = _gelu(g_ref[pl.ds(r0, tc), :].astype(F32))
            y_ref[pl.ds(r0, tc), :] = (h * ge).astype(BF16)
            return h[tc - 1:tc, :]

        lax.fori_loop(0, nch, chunk, jnp.zeros((1, C), F32))

    col, vec, cwspec, w, own = _rnn_specs(T)
    return pl.pallas_call(
        body, name=name, grid=(RNN_BLOCKS,),
        in_specs=[col(C_XR), col(C_GR), cwspec, vec, w, vec, w, vec, vec],
        out_specs=[own, own],
        out_shape=[jax.ShapeDtypeStruct((T, D_MODEL), BF16), jax.ShapeDtypeStruct((T, D_MODEL), F32)],
        scratch_shapes=[pltpu.VMEM((tc + 8, C), F32)],
        compiler_params=pltpu.CompilerParams(dimension_semantics=("parallel",)),
    )(P, P, cw, cb, wrg, brg, wig, big, lam)


def _rnn_bwd(P, hs, dy, cw, cb, wrg, brg, wig, big, lam, name):
    T = P.shape[0]
    C = RNN_BLOCK
    tc = _pick(T, (TIME_CHUNK,))
    nch = T // tc

    def body(x_ref, g_ref, hs_ref, dy_ref, cw_ref, cb_ref, wr_ref, br_ref, wi_ref, bi_ref, lam_ref,
             dx_ref, dg_ref, dcw_ref, dcb_ref, dwr_ref, dbr_ref, dwi_ref, dbi_ref, dlam_ref,
             xs_ref, hp_ref, an_ref, dn_ref):
        lam_v = lam_ref[...]
        sp = _softplus_neg(lam_v)
        wr = wr_ref[0]
        wi = wi_ref[0]
        dcw_ref[...] = jnp.zeros_like(dcw_ref)
        dcb_ref[...] = jnp.zeros_like(dcb_ref)
        dwr_ref[...] = jnp.zeros_like(dwr_ref)
        dbr_ref[...] = jnp.zeros_like(dbr_ref)
        dwi_ref[...] = jnp.zeros_like(dwi_ref)
        dbi_ref[...] = jnp.zeros_like(dbi_ref)
        dlam_ref[...] = jnp.zeros_like(dlam_ref)
        an_ref[tc:, :] = jnp.zeros((8, C), F32)
        dn_ref[tc:, :] = jnp.zeros((8, C), F32)

        def chunk(step, gnext):
            c = nch - 1 - step
            r0 = pl.multiple_of(c * tc, tc)
            p0 = pl.multiple_of(jnp.maximum(r0 - 8, 0), 8)
            q0 = pl.multiple_of(jnp.maximum(r0 - 16, 0), 16)
            live = c > 0
            x = x_ref[pl.ds(r0, tc), :].astype(F32)
            xs_ref[0:8, :] = jnp.where(live, x_ref[pl.ds(q0, 16), :].astype(F32)[8:, :], 0.0)
            xs_ref[8:, :] = x
            xsh = [xs_ref[pl.ds(8 - (CONV_WIDTH - 1 - k), tc), :] for k in range(CONV_WIDTH)]
            xc = cb_ref[...] + jnp.zeros((tc, C), F32)
            for k in range(CONV_WIDTH):
                xc = xc + xsh[k] * cw_ref[k:k + 1, :]
            r, i, a, om, mult = _rnn_gates(xc, wr, br_ref[...], wi, bi_ref[...], sp)
            h = hs_ref[pl.ds(r0, tc), :]
            hp_ref[0:8, :] = jnp.where(live, hs_ref[pl.ds(p0, 8), :], 0.0)
            hp_ref[8:, :] = h
            hm1 = hp_ref[pl.ds(7, tc), :]
            g = g_ref[pl.ds(r0, tc), :].astype(F32)
            ge, th = _gelu(g)
            dy = dy_ref[pl.ds(r0, tc), :]
            dg_ref[pl.ds(r0, tc), :] = (dy * h * _gelu_grad(g, th)).astype(BF16)
            an_ref[0:tc, :] = a
            coef = an_ref[pl.ds(1, tc), :]
            ccum, bcum = _scan_bwd(coef, dy * ge)
            G = bcum + ccum * gnext
            an_ref[tc:, :] = a[0:8, :]
            da = G * hm1
            ixc = i * xc
            dmult = G * ixc
            di = G * mult * xc
            dxc = G * mult * i
            dla = da * a - dmult * (1.0 - om) / mult
            dr = dla * (-LRU_C * sp)
            dlam_ref[...] += jnp.sum(dla * r, axis=0, keepdims=True)
            dzr = dr * r * (1.0 - r)
            dzi = di * i * (1.0 - i)
            dbr_ref[...] += jnp.sum(dzr, axis=0, keepdims=True)
            dbi_ref[...] += jnp.sum(dzi, axis=0, keepdims=True)
            xb = xc.astype(BF16)
            dzrb = dzr.astype(BF16)
            dzib = dzi.astype(BF16)
            dwr_ref[0] += lax.dot_general(xb, dzrb, _DIMS["tn"], preferred_element_type=F32)
            dwi_ref[0] += lax.dot_general(xb, dzib, _DIMS["tn"], preferred_element_type=F32)
            dxc = dxc + lax.dot_general(dzrb, wr, _DIMS["nt"], preferred_element_type=F32)
            dxc = dxc + lax.dot_general(dzib, wi, _DIMS["nt"], preferred_element_type=F32)
            dcb_ref[...] += jnp.sum(dxc, axis=0, keepdims=True)
            for k in range(CONV_WIDTH):
                dcw_ref[k:k + 1, :] += jnp.sum(dxc * xsh[k], axis=0, keepdims=True)
            dn_ref[0:tc, :] = dxc
            dx = jnp.zeros((tc, C), F32)
            for k in range(CONV_WIDTH):
                dx = dx + dn_ref[pl.ds(CONV_WIDTH - 1 - k, tc), :] * cw_ref[k:k + 1, :]
            dn_ref[tc:, :] = dxc[0:8, :]
            dx_ref[pl.ds(r0, tc), :] = dx.astype(BF16)
            return G[0:1, :]

        lax.fori_loop(0, nch, chunk, jnp.zeros((1, C), F32))
        dlam_ref[...] = dlam_ref[...] * (LRU_C * _sigmoid(-lam_v))

    col, vec, cwspec, w, own = _rnn_specs(T)
    vshape = jax.ShapeDtypeStruct((1, D_MODEL), F32)
    wshape = jax.ShapeDtypeStruct((RNN_BLOCKS, C, C), F32)
    return pl.pallas_call(
        body, name=name, grid=(RNN_BLOCKS,),
        in_specs=[col(C_XR), col(C_GR), own, own, cwspec, vec, w, vec, w, vec, vec],
        out_specs=[own, own, cwspec, vec, w, vec, w, vec, vec],
        out_shape=[jax.ShapeDtypeStruct((T, D_MODEL), BF16), jax.ShapeDtypeStruct((T, D_MODEL), BF16),
                   jax.ShapeDtypeStruct((CONV_WIDTH, D_MODEL), F32), vshape, wshape, vshape, wshape, vshape, vshape],
        scratch_shapes=[pltpu.VMEM((tc + 8, C), F32), pltpu.VMEM((tc + 8, C), F32),
                        pltpu.VMEM((tc + 8, C), F32), pltpu.VMEM((tc + 8, C), F32)],
        compiler_params=pltpu.CompilerParams(dimension_semantics=("parallel",)),
    )(P, P, hs, dy, cw, cb, wrg, brg, wig, big, lam)


def _rope_table(T):
    half = ROT_DIM // 2
    pos = jnp.arange(T, dtype=F32)
    inv_freq = ROPE_THETA ** (-jnp.arange(0, ROT_DIM, 2, dtype=F32) / ROT_DIM)
    ang = pos[:, None] * inv_freq[None, :]
    cos, sin = jnp.cos(ang), jnp.sin(ang)
    one = jnp.ones((T, HEAD_DIM - ROT_DIM), F32)
    zero = jnp.zeros((T, HEAD_DIM - ROT_DIM), F32)
    z8 = jnp.zeros((T, half), F32)
    c = jnp.concatenate([cos, cos, one], axis=1)
    a = jnp.concatenate([-sin, z8, zero], axis=1)
    b = jnp.concatenate([z8, sin, zero], axis=1)
    return jnp.stack([jnp.tile(c, (1, 2)), jnp.tile(a, (1, 2)), jnp.tile(b, (1, 2))])


def _rope(x, tab, sign):
    W = x.shape[1]
    rep = W // 128
    c = jnp.tile(tab[0], (1, rep)) if rep > 1 else tab[0]
    a = jnp.tile(tab[1], (1, rep)) if rep > 1 else tab[1]
    b = jnp.tile(tab[2], (1, rep)) if rep > 1 else tab[2]
    return x * c + sign * (pltpu.roll(x, W - ROT_DIM // 2, 1) * a + pltpu.roll(x, ROT_DIM // 2, 1) * b)


def _swa_mask(n):
    rows = lax.broadcasted_iota(jnp.int32, (GROUP * BLOCK, 2 * BLOCK), 0) & (BLOCK - 1)
    cols = lax.broadcasted_iota(jnp.int32, (GROUP * BLOCK, 2 * BLOCK), 1)
    return (cols > rows) & (cols <= rows + BLOCK) & ((n > 0) | (cols >= BLOCK))


def _swa_probs(qg, k2, sink, valid):
    s = lax.dot_general(qg, k2, _DIMS["nt"], preferred_element_type=F32) * (HEAD_DIM ** -0.5)
    s = jnp.where(valid, s, NEG_INF)
    m = jnp.maximum(jnp.max(s, axis=1, keepdims=True), sink)
    p = jnp.exp(s - m)
    ps = jnp.exp(sink - m)
    inv = 1.0 / (jnp.sum(p, axis=1, keepdims=True) + ps)
    return p * inv, ps * inv


def _swa_specs(T):
    nb = T // BLOCK
    qspec = pl.BlockSpec((BLOCK, D_MODEL), lambda n: (n, C_Q // D_MODEL))
    cur = lambda off: pl.BlockSpec((BLOCK, KV_WIDTH), lambda n, off=off: (n, off // KV_WIDTH))
    prev = lambda off: pl.BlockSpec((BLOCK, KV_WIDTH), lambda n, off=off: (jnp.maximum(n - 1, 0), off // KV_WIDTH))
    tcur = pl.BlockSpec((3, BLOCK, 128), lambda n: (0, n, 0))
    tprev = pl.BlockSpec((3, BLOCK, 128), lambda n: (0, jnp.maximum(n - 1, 0), 0))
    sink = pl.BlockSpec((N_KV_HEADS, GROUP * BLOCK, 1), lambda n: (0, 0, 0))
    own = pl.BlockSpec((BLOCK, D_MODEL), lambda n: (n, 0))
    return nb, qspec, cur, prev, tcur, tprev, sink, own


def _stack_heads(x, hk):
    return jnp.concatenate([x[:, (hk * GROUP + g) * HEAD_DIM:(hk * GROUP + g + 1) * HEAD_DIM] for g in range(GROUP)],
                           axis=0)


def _swa_fwd(P, tab, sink_col, name):
    T = P.shape[0]
    nb, qspec, cur, prev, tcur, tprev, sink, own = _swa_specs(T)

    def body(q_ref, kc_ref, kp_ref, vc_ref, vp_ref, tc_ref, tp_ref, sk_ref, o_ref):
        n = pl.program_id(0)
        valid = _swa_mask(n)
        q = _rope(q_ref[...].astype(F32), tc_ref[...], 1.0).astype(BF16)
        k2 = jnp.concatenate([_rope(kp_ref[...].astype(F32), tp_ref[...], 1.0),
                              _rope(kc_ref[...].astype(F32), tc_ref[...], 1.0)], axis=0).astype(BF16)
        v2 = jnp.concatenate([vp_ref[...], vc_ref[...]], axis=0).astype(BF16)
        parts = []
        for hk in range(N_KV_HEADS):
            sl = slice(hk * HEAD_DIM, (hk + 1) * HEAD_DIM)
            pn, _ = _swa_probs(_stack_heads(q, hk), k2[:, sl], sk_ref[hk], valid)
            og = jnp.dot(pn.astype(BF16), v2[:, sl], preferred_element_type=F32)
            parts += [og[g * BLOCK:(g + 1) * BLOCK, :] for g in range(GROUP)]
        o_ref[...] = jnp.concatenate(parts, axis=1).astype(BF16)

    return pl.pallas_call(
        body, name=name, grid=(nb,),
        in_specs=[qspec, cur(C_K), prev(C_K), cur(C_V), prev(C_V), tcur, tprev, sink],
        out_specs=own, out_shape=jax.ShapeDtypeStruct((T, D_MODEL), BF16),
        compiler_params=pltpu.CompilerParams(dimension_semantics=("parallel",)),
    )(P, P, P, P, P, tab, tab, sink_col)


def _swa_bwd(P, do, tab, sink_col, name):
    T = P.shape[0]
    nb, qspec, cur, prev, tcur, tprev, sink, own = _swa_specs(T)

    def body(q_ref, kc_ref, kp_ref, vc_ref, vp_ref, do_ref, tc_ref, tp_ref, sk_ref,
             dq_ref, dk_ref, dv_ref, ds_ref):
        n = pl.program_id(0)

        @pl.when(n == 0)
        def _():
            dk_ref[...] = jnp.zeros_like(dk_ref)
            dv_ref[...] = jnp.zeros_like(dv_ref)
            ds_ref[...] = jnp.zeros_like(ds_ref)

        valid = _swa_mask(n)
        tcur_v = tc_ref[...]
        tprev_v = tp_ref[...]
        q = _rope(q_ref[...].astype(F32), tcur_v, 1.0).astype(BF16)
        k2 = jnp.concatenate([_rope(kp_ref[...].astype(F32), tprev_v, 1.0),
                              _rope(kc_ref[...].astype(F32), tcur_v, 1.0)], axis=0).astype(BF16)
        v2 = jnp.concatenate([vp_ref[...], vc_ref[...]], axis=0).astype(BF16)
        dob = do_ref[...].astype(BF16)
        dq_parts = []
        dk_parts = []
        dv_parts = []
        for hk in range(N_KV_HEADS):
            sl = slice(hk * HEAD_DIM, (hk + 1) * HEAD_DIM)
            qg = _stack_heads(q, hk)
            dog = _stack_heads(dob, hk)
            pn, psn = _swa_probs(qg, k2[:, sl], sk_ref[hk], valid)
            dp = lax.dot_general(dog, v2[:, sl], _DIMS["nt"], preferred_element_type=F32)
            delta = jnp.sum(pn * dp, axis=1, keepdims=True)
            dsc = (pn * (dp - delta) * (HEAD_DIM ** -0.5)).astype(BF16)
            dsink = -psn * delta
            for g in range(GROUP):
                ds_ref[hk, g:g + 1, :] += jnp.broadcast_to(
                    jnp.sum(dsink[g * BLOCK:(g + 1) * BLOCK], axis=0, keepdims=True), (1, 128))
            dqg = jnp.dot(dsc, k2[:, sl], preferred_element_type=F32)
            dq_parts += [dqg[g * BLOCK:(g + 1) * BLOCK, :] for g in range(GROUP)]
            dk_parts.append(lax.dot_general(dsc, qg, _DIMS["tn"], preferred_element_type=F32))
            dv_parts.append(lax.dot_general(pn.astype(BF16), dog, _DIMS["tn"], preferred_element_type=F32))
        dq_ref[...] = _rope(jnp.concatenate(dq_parts, axis=1), tcur_v, -1.0).astype(BF16)
        dk2 = jnp.concatenate(dk_parts, axis=1)
        dv2 = jnp.concatenate(dv_parts, axis=1)
        c0 = pl.multiple_of(n * BLOCK, BLOCK)
        p0 = pl.multiple_of(jnp.maximum(n - 1, 0) * BLOCK, BLOCK)
        dk_ref[pl.ds(p0, BLOCK), :] += _rope(dk2[:BLOCK], tprev_v, -1.0)
        dv_ref[pl.ds(p0, BLOCK), :] += dv2[:BLOCK]
        dk_ref[pl.ds(c0, BLOCK), :] += _rope(dk2[BLOCK:], tcur_v, -1.0)
        dv_ref[pl.ds(c0, BLOCK), :] += dv2[BLOCK:]

    full = pl.BlockSpec((T, KV_WIDTH), lambda n: (0, 0))
    return pl.pallas_call(
        body, name=name, grid=(nb,),
        in_specs=[qspec, cur(C_K), prev(C_K), cur(C_V), prev(C_V), own, tcur, tprev, sink],
        out_specs=[own, full, full, pl.BlockSpec((N_KV_HEADS, GROUP, 128), lambda n: (0, 0, 0))],
        out_shape=[jax.ShapeDtypeStruct((T, D_MODEL), BF16), jax.ShapeDtypeStruct((T, KV_WIDTH), F32),
                   jax.ShapeDtypeStruct((T, KV_WIDTH), F32), jax.ShapeDtypeStruct((N_KV_HEADS, GROUP, 128), F32)],
        compiler_params=pltpu.CompilerParams(dimension_semantics=("arbitrary",)),
    )(P, P, P, P, P, do, tab, tab, sink_col)


_MW = 256


def _gate_specs(T, rows, width):
    tr = _pick(T, (rows, 256, 128, 64, 32, 16, 8))
    col = lambda off: pl.BlockSpec((tr, width), lambda i, j, off=off: (i, off // width + j))
    own = pl.BlockSpec((tr, width), lambda i, j: (i, j))
    return tr, col, own


def _merge_fwd(P, mr, ma, name):
    T = P.shape[0]
    tr, col, own = _gate_specs(T, 1024, _MW)

    def body(gr_ref, ga_ref, mr_ref, ma_ref, o_ref):
        o_ref[...] = (_sigmoid(gr_ref[...].astype(F32)) * mr_ref[...]
                      + _sigmoid(ga_ref[...].astype(F32)) * ma_ref[...]).astype(BF16)

    return pl.pallas_call(
        body, name=name, grid=(T // tr, D_MODEL // _MW), in_specs=[col(C_GRNN), col(C_GATTN), own, own],
        out_specs=own, out_shape=jax.ShapeDtypeStruct((T, D_MODEL), BF16),
        compiler_params=pltpu.CompilerParams(dimension_semantics=("parallel", "parallel")),
    )(P, P, mr, ma)


def _merge_bwd(P, mr, ma, dm, name):
    T = P.shape[0]
    tr, col, own = _gate_specs(T, 512, _MW)

    def body(gr_ref, ga_ref, mr_ref, ma_ref, dm_ref, dmr_ref, dma_ref, dgr_ref, dga_ref):
        dm = dm_ref[...]
        sr = _sigmoid(gr_ref[...].astype(F32))
        sa = _sigmoid(ga_ref[...].astype(F32))
        dmr_ref[...] = (dm * sr).astype(BF16)
        dma_ref[...] = (dm * sa).astype(BF16)
        dgr_ref[...] = (dm * mr_ref[...] * sr * (1.0 - sr)).astype(BF16)
        dga_ref[...] = (dm * ma_ref[...] * sa * (1.0 - sa)).astype(BF16)

    shp = jax.ShapeDtypeStruct((T, D_MODEL), BF16)
    return pl.pallas_call(
        body, name=name, grid=(T // tr, D_MODEL // _MW), in_specs=[col(C_GRNN), col(C_GATTN), own, own, own],
        out_specs=[own] * 4, out_shape=[shp] * 4,
        compiler_params=pltpu.CompilerParams(dimension_semantics=("parallel", "parallel")),
    )(P, P, mr, ma, dm)


_FFN_ROWS = 128


def _swiglu_fwd(U, name):
    T = U.shape[0]
    tr = _pick(T, (_FFN_ROWS, 64, 32, 16))
    half = lambda j: pl.BlockSpec((tr, D_FF), lambda i, j=j: (i, j))

    def body(g_ref, u_ref, o_ref):
        g = g_ref[...].astype(F32)
        o_ref[...] = (g * _sigmoid(g) * u_ref[...].astype(F32)).astype(BF16)

    return pl.pallas_call(
        body, name=name, grid=(T // tr,), in_specs=[half(0), half(1)],
        out_specs=half(0), out_shape=jax.ShapeDtypeStruct((T, D_FF), BF16),
        compiler_params=pltpu.CompilerParams(dimension_semantics=("parallel",)),
    )(U, U)


def _swiglu_bwd(U, dact, name):
    T = U.shape[0]
    tr = _pick(T, (_FFN_ROWS, 64, 32, 16))
    half = lambda j: pl.BlockSpec((tr, D_FF), lambda i, j=j: (i, j))

    def body(g_ref, u_ref, da_ref, o_ref):
        g = g_ref[...].astype(F32)
        da = da_ref[...].astype(F32)
        s = _sigmoid(g)
        o_ref[:, :D_FF] = (da * u_ref[...].astype(F32) * s * (1.0 + g * (1.0 - s))).astype(BF16)
        o_ref[:, D_FF:] = (da * g * s).astype(BF16)

    return pl.pallas_call(
        body, name=name, grid=(T // tr,), in_specs=[half(0), half(1), half(0)],
        out_specs=pl.BlockSpec((tr, 2 * D_FF), lambda i: (i, 0)),
        out_shape=jax.ShapeDtypeStruct((T, 2 * D_FF), BF16),
        compiler_params=pltpu.CompilerParams(dimension_semantics=("parallel",)),
    )(U, U, dact)


def _cross_probs(qh, kh):
    s = lax.dot_general(qh, kh, _DIMS["nt"], preferred_element_type=F32) * (CROSS_HEAD_DIM ** -0.5)
    p = jnp.exp(s - jnp.max(s, axis=1, keepdims=True))
    return p / jnp.sum(p, axis=1, keepdims=True)


def _cross_fwd(q, kv, name):
    T = q.shape[0]
    M = kv.shape[0]
    tr = _pick(T, (ROW_TILE, 128, 64, 32, 16, 8))
    W = CROSS_HEAD_DIM

    def body(q_ref, kv_ref, o_ref):
        for h in range(CROSS_HEADS):
            qh = q_ref[:, h * W:(h + 1) * W].astype(BF16)
            kh = kv_ref[:, h * W:(h + 1) * W].astype(BF16)
            vh = kv_ref[:, D_MODEL + h * W:D_MODEL + (h + 1) * W].astype(BF16)
            pn = _cross_probs(qh, kh)
            o_ref[:, h * W:(h + 1) * W] = jnp.dot(pn.astype(BF16), vh, preferred_element_type=F32).astype(BF16)

    row = pl.BlockSpec((tr, D_MODEL), lambda i: (i, 0))
    return pl.pallas_call(
        body, name=name, grid=(T // tr,), in_specs=[row, pl.BlockSpec((M, 2 * D_MODEL), lambda i: (0, 0))],
        out_specs=row, out_shape=jax.ShapeDtypeStruct((T, D_MODEL), BF16),
        compiler_params=pltpu.CompilerParams(dimension_semantics=("parallel",)),
    )(q, kv)


def _cross_bwd(q, kv, do, name):
    T = q.shape[0]
    M = kv.shape[0]
    tr = _pick(T, (ROW_TILE, 128, 64, 32, 16, 8))
    W = CROSS_HEAD_DIM

    def body(q_ref, kv_ref, do_ref, dq_ref, dkv_ref):
        @pl.when(pl.program_id(0) == 0)
        def _():
            dkv_ref[...] = jnp.zeros_like(dkv_ref)

        for h in range(CROSS_HEADS):
            qh = q_ref[:, h * W:(h + 1) * W].astype(BF16)
            kh = kv_ref[:, h * W:(h + 1) * W].astype(BF16)
            vh = kv_ref[:, D_MODEL + h * W:D_MODEL + (h + 1) * W].astype(BF16)
            doh = do_ref[:, h * W:(h + 1) * W].astype(BF16)
            pn = _cross_probs(qh, kh)
            dp = lax.dot_general(doh, vh, _DIMS["nt"], preferred_element_type=F32)
            delta = jnp.sum(pn * dp, axis=1, keepdims=True)
            dsc = (pn * (dp - delta) * (W ** -0.5)).astype(BF16)
            dq_ref[:, h * W:(h + 1) * W] = jnp.dot(dsc, kh, preferred_element_type=F32).astype(BF16)
            dkv_ref[:, h * W:(h + 1) * W] += lax.dot_general(dsc, qh, _DIMS["tn"], preferred_element_type=F32)
            dkv_ref[:, D_MODEL + h * W:D_MODEL + (h + 1) * W] += lax.dot_general(
                pn.astype(BF16), doh, _DIMS["tn"], preferred_element_type=F32)

    row = pl.BlockSpec((tr, D_MODEL), lambda i: (i, 0))
    full = pl.BlockSpec((M, 2 * D_MODEL), lambda i: (0, 0))
    return pl.pallas_call(
        body, name=name, grid=(T // tr,), in_specs=[row, full, row], out_specs=[row, full],
        out_shape=[jax.ShapeDtypeStruct((T, D_MODEL), BF16), jax.ShapeDtypeStruct((M, 2 * D_MODEL), F32)],
        compiler_params=pltpu.CompilerParams(dimension_semantics=("arbitrary",)),
    )(q, kv, do)


def _loss_head(y, target, name):
    T, D = y.shape
    tr = _pick(T, (ROW_TILE, 128, 64, 32, 16, 8))

    def body(y_ref, t_ref, l_ref, dy_ref):
        @pl.when(pl.program_id(0) == 0)
        def _():
            l_ref[...] = jnp.zeros_like(l_ref)

        err = y_ref[...] - t_ref[...]
        dy_ref[...] = err * (1.0 / D)
        l_ref[...] += jnp.broadcast_to(0.5 * jnp.sum(jnp.mean(err * err, axis=-1, keepdims=True), axis=0, keepdims=True),
                                       (8, 128))

    row = pl.BlockSpec((tr, D), lambda i: (i, 0))
    return pl.pallas_call(
        body, name=name, grid=(T // tr,), in_specs=[row, row],
        out_specs=[pl.BlockSpec((8, 128), lambda i: (0, 0)), row],
        out_shape=[jax.ShapeDtypeStruct((8, 128), F32), jax.ShapeDtypeStruct((T, D), F32)],
        compiler_params=pltpu.CompilerParams(dimension_semantics=("arbitrary",)),
    )(y, target)


def _sum_slots(recv, name):
    _, R, C = recv.shape
    tr = _pick(R, (ROW_TILE, 128, 64, 32, 16, 8))

    def body(r_ref, o_ref):
        acc = r_ref[0].astype(F32)
        for d in range(1, N_DEV):
            acc = acc + r_ref[d].astype(F32)
        o_ref[...] = acc

    return pl.pallas_call(
        body, name=name, grid=(R // tr,), in_specs=[pl.BlockSpec((N_DEV, tr, C), lambda i: (0, i, 0))],
        out_specs=pl.BlockSpec((tr, C), lambda i: (i, 0)), out_shape=jax.ShapeDtypeStruct((R, C), F32),
        compiler_params=pltpu.CompilerParams(dimension_semantics=("parallel",)),
    )(recv)


SUM_STEPS = 2


def _sum_blocks(recvs, sents, me, name):
    n = len(recvs)

    def body(me_ref, *refs):
        me = me_ref[0]
        for i in range(n):
            r_ref, s_ref, o_ref = refs[i], refs[n + i], refs[2 * n + i]
            acc = s_ref[0].astype(F32)
            for d in range(N_DEV):
                acc = acc + jnp.where(d == me, 0.0, r_ref[d].astype(F32))
            o_ref[...] = acc

    tiles = [r.shape[1] // SUM_STEPS for r in recvs]
    C = recvs[0].shape[2]
    return pl.pallas_call(
        body, name=name,
        grid_spec=pltpu.PrefetchScalarGridSpec(
            num_scalar_prefetch=1, grid=(SUM_STEPS,),
            in_specs=[pl.BlockSpec((N_DEV, t, C), lambda s, me_ref: (0, s, 0)) for t in tiles]
            + [pl.BlockSpec((1, t, C), lambda s, me_ref: (me_ref[0], s, 0)) for t in tiles],
            out_specs=[pl.BlockSpec((t, C), lambda s, me_ref: (s, 0)) for t in tiles]),
        out_shape=[jax.ShapeDtypeStruct((r.shape[1], C), F32) for r in recvs],
        compiler_params=pltpu.CompilerParams(dimension_semantics=("parallel",)),
    )(me, *recvs, *sents)


def _adamw(w, g, m, v, name):
    shape = w.shape
    C = shape[-1]
    R = math.prod(shape[:-1])
    w2, g2, m2, v2 = (t.reshape(R, C) for t in (w, g, m, v))
    tr = _pick(R, (ROW_TILE, 128, 64, 32, 16, 8))

    def body(w_ref, g_ref, m_ref, v_ref, d_ref, mo_ref, vo_ref):
        gg = g_ref[...]
        mn = ADAM_B1 * m_ref[...] + (1.0 - ADAM_B1) * gg
        vn = ADAM_B2 * v_ref[...] + (1.0 - ADAM_B2) * (gg * gg)
        m_hat = mn / (1.0 - ADAM_B1 ** ADAM_STEP)
        v_hat = vn / (1.0 - ADAM_B2 ** ADAM_STEP)
        d_ref[...] = -ADAM_LR * (m_hat / (jnp.sqrt(v_hat) + ADAM_EPS) + ADAM_WD * w_ref[...])
        mo_ref[...] = mn
        vo_ref[...] = vn

    blk = pl.BlockSpec((tr, C), lambda i: (i, 0))
    shp = jax.ShapeDtypeStruct((R, C), F32)
    d, mo, vo = pl.pallas_call(
        body, name=name, grid=(R // tr,), in_specs=[blk] * 4, out_specs=[blk] * 3, out_shape=[shp] * 3,
        compiler_params=pltpu.CompilerParams(dimension_semantics=("parallel",)),
    )(w2, g2, m2, v2)
    return d.reshape(shape), mo.reshape(shape), vo.reshape(shape)


def _all_gather_many(bufs, name):
    n = len(bufs)

    def body(*refs):
        xs, outs = refs[:n], refs[n:2 * n]
        send_sems, recv_sems, local_sems = refs[2 * n:]
        x, y, c = lax.axis_index("x"), lax.axis_index("y"), lax.axis_index("c")
        me, sibling = (x, y, c), (x, y, 1 - c)
        chips = [(1 - x, y), (x, 1 - y), (1 - x, 1 - y)]

        def slot(i, px, py, pc):
            return outs[i].at[4 * px + 2 * py + pc]

        def copy(i, k, block, to, src=None):
            return pltpu.make_async_remote_copy(
                src_ref=slot(i, *block) if src is None else src, dst_ref=slot(i, *block),
                send_sem=send_sems.at[7 * i + k], recv_sem=recv_sems.at[7 * i + k],
                device_id=to, device_id_type=pl.DeviceIdType.MESH)

        mine = [pltpu.make_async_copy(xs[i], slot(i, *me), local_sems.at[i]) for i in range(n)]
        for cp in mine:
            cp.start()
        first = [copy(i, 0, me, sibling, src=xs[i]) for i in range(n)]
        for j, chip in enumerate(chips):
            first += [copy(i, 1 + j, me, (*chip, c), src=xs[i]) for i in range(n)]
        for cp in first:
            cp.start()
        passed = []
        for j, chip in enumerate(chips):
            for i in range(n):
                copy(i, 1 + j, (*chip, c), me).wait_recv()
                passed.append(copy(i, 4 + j, (*chip, c), sibling))
                passed[-1].start()
        for i in range(n):
            copy(i, 0, sibling, me).wait_recv()
        for j, chip in enumerate(chips):
            for i in range(n):
                copy(i, 4 + j, (*chip, 1 - c), me).wait_recv()
        for cp in first + passed:
            cp.wait_send()
        for cp in mine:
            cp.wait()

    hbm = pl.BlockSpec(memory_space=pl.ANY)
    return pl.pallas_call(
        body, name=name, out_shape=[jax.ShapeDtypeStruct((N_DEV,) + b.shape, b.dtype) for b in bufs],
        in_specs=[hbm] * n, out_specs=[hbm] * n,
        scratch_shapes=[pltpu.SemaphoreType.DMA((7 * n,)), pltpu.SemaphoreType.DMA((7 * n,)),
                        pltpu.SemaphoreType.DMA((n,))],
    )(*bufs)


def _all_gather(buf, name):
    return _all_gather_many([buf], name)[0]


_HBM = pl.BlockSpec(memory_space=pltpu.HBM)
_SEM = pl.BlockSpec(memory_space=pltpu.SEMAPHORE)
_EFFECT = pltpu.SideEffectType.DATAFLOW_SIDE_EFFECTING


def _peer(k):
    x, y, c = lax.axis_index("x"), lax.axis_index("y"), lax.axis_index("c")
    return x ^ ((k >> 2) & 1), y ^ ((k >> 1) & 1), c ^ (k & 1)


def _my_slot():
    return 4 * lax.axis_index("x") + 2 * lax.axis_index("y") + lax.axis_index("c")


def _split_copy(src_refs, land_refs, send_sems, recv_sems, i, k):
    px, py, pc = _peer(k)
    return pltpu.make_async_remote_copy(
        src_ref=src_refs[i].at[4 * px + 2 * py + pc], dst_ref=land_refs[i].at[_my_slot()],
        send_sem=send_sems.at[7 * i + k - 1], recv_sem=recv_sems.at[7 * i + k - 1],
        device_id=(px, py, pc), device_id_type=pl.DeviceIdType.MESH)


def _split_start(srcs, lands, name):
    n = len(srcs)

    def body(*refs):
        src_refs, land_refs = refs[:n], refs[n:2 * n]
        send_sems, recv_sems = refs[2 * n], refs[2 * n + 1]
        token = refs[-1]
        for i in range(n):
            for k in range(1, N_DEV):
                _split_copy(src_refs, land_refs, send_sems, recv_sems, i, k).start()
        token[...] = jnp.zeros_like(token)

    outs = pl.pallas_call(
        body, name=name,
        out_shape=(pltpu.SemaphoreType.DMA((7 * n,)), pltpu.SemaphoreType.DMA((7 * n,)),
                   *[pltpu.HBM(a.shape, a.dtype) for a in srcs], *[pltpu.HBM(a.shape, a.dtype) for a in lands],
                   jax.ShapeDtypeStruct((8, 128), F32)),
        in_specs=[_HBM] * (2 * n),
        out_specs=(_SEM, _SEM, *([_HBM] * (2 * n)), pl.BlockSpec(memory_space=pltpu.VMEM)),
        input_output_aliases={i: 2 + i for i in range(2 * n)},
        compiler_params=pltpu.CompilerParams(has_side_effects=_EFFECT),
    )(*[pltpu.with_memory_space_constraint(a, pltpu.HBM) for a in list(srcs) + list(lands)])
    return outs[0], outs[1], outs[2:2 + n], outs[2 + n:2 + 2 * n], outs[-1]


def _split_wait(send_sems, recv_sems, srcs, lands, after, name):
    n = len(srcs)

    def body(*refs):
        src_refs, land_refs = refs[:n], refs[n:2 * n]
        ssem, rsem = refs[2 * n], refs[2 * n + 1]
        for i in range(n):
            for k in range(1, N_DEV):
                cp = _split_copy(src_refs, land_refs, ssem, rsem, i, k)
                cp.wait_send()
                cp.wait_recv()

    outs = pl.pallas_call(
        body, name=name,
        out_shape=(*[pltpu.HBM(a.shape, a.dtype) for a in srcs], *[pltpu.HBM(a.shape, a.dtype) for a in lands]),
        in_specs=[*([_HBM] * (2 * n)), _SEM, _SEM, pl.BlockSpec(memory_space=pl.ANY)],
        out_specs=tuple([_HBM] * (2 * n)),
        input_output_aliases={i: i for i in range(2 * n)},
        compiler_params=pltpu.CompilerParams(has_side_effects=_EFFECT),
    )(*srcs, *lands, send_sems, recv_sems, after)
    return outs[:n], outs[n:]


def _gather_first(src_refs, land_refs, send_sems, recv_sems, i, k):
    x, y, c = lax.axis_index("x"), lax.axis_index("y"), lax.axis_index("c")
    to = ((x, y, 1 - c), (1 - x, y, c), (x, 1 - y, c), (1 - x, 1 - y, c))[k]
    return pltpu.make_async_remote_copy(
        src_ref=src_refs[i], dst_ref=land_refs[i].at[_my_slot()],
        send_sem=send_sems.at[4 * i + k], recv_sem=recv_sems.at[4 * i + k],
        device_id=to, device_id_type=pl.DeviceIdType.MESH)


def _gather_second(land_refs, send_sems, recv_sems, i, j):
    x, y, c = lax.axis_index("x"), lax.axis_index("y"), lax.axis_index("c")
    px, py = ((1 - x, y), (x, 1 - y), (1 - x, 1 - y))[j]
    slot = land_refs[i].at[4 * px + 2 * py + c]
    return pltpu.make_async_remote_copy(
        src_ref=slot, dst_ref=slot, send_sem=send_sems.at[3 * i + j], recv_sem=recv_sems.at[3 * i + j],
        device_id=(x, y, 1 - c), device_id_type=pl.DeviceIdType.MESH)


def _gather_start(srcs, lands, name, after):
    n = len(srcs)

    def body(*refs):
        src_refs, land_refs = refs[:n], refs[n:2 * n]
        send_sems, recv_sems = refs[2 * n + 1], refs[2 * n + 2]
        token = refs[-1]
        for k in range(4):
            for i in range(n):
                _gather_first(src_refs, land_refs, send_sems, recv_sems, i, k).start()
        token[...] = jnp.zeros_like(token)

    outs = pl.pallas_call(
        body, name=name,
        out_shape=(pltpu.SemaphoreType.DMA((4 * n,)), pltpu.SemaphoreType.DMA((4 * n,)),
                   *[pltpu.HBM(a.shape, a.dtype) for a in srcs], *[pltpu.HBM(a.shape, a.dtype) for a in lands],
                   jax.ShapeDtypeStruct((8, 128), F32)),
        in_specs=[_HBM] * (2 * n) + [pl.BlockSpec(memory_space=pl.ANY)],
        out_specs=(_SEM, _SEM, *([_HBM] * (2 * n)), pl.BlockSpec(memory_space=pltpu.VMEM)),
        input_output_aliases={i: 2 + i for i in range(2 * n)},
        compiler_params=pltpu.CompilerParams(has_side_effects=_EFFECT),
    )(*[pltpu.with_memory_space_constraint(a, pltpu.HBM) for a in list(srcs) + list(lands)], after)
    return outs[0], outs[1], outs[2:2 + n], outs[2 + n:2 + 2 * n], outs[-1]


def _gather_forward(send1, recv1, srcs, lands, after, name):
    n = len(srcs)

    def body(*refs):
        src_refs, land_refs = refs[:n], refs[n:2 * n]
        s1, r1 = refs[2 * n], refs[2 * n + 1]
        s2, r2 = refs[2 * n + 3], refs[2 * n + 4]
        token = refs[-1]
        for j in range(3):
            for i in range(n):
                _gather_first(src_refs, land_refs, s1, r1, i, 1 + j).wait_recv()
                _gather_second(land_refs, s2, r2, i, j).start()
        for i in range(n):
            _gather_first(src_refs, land_refs, s1, r1, i, 0).wait_recv()
            for k in range(4):
                _gather_first(src_refs, land_refs, s1, r1, i, k).wait_send()
        token[...] = jnp.zeros_like(token)

    outs = pl.pallas_call(
        body, name=name,
        out_shape=(pltpu.SemaphoreType.DMA((3 * n,)), pltpu.SemaphoreType.DMA((3 * n,)),
                   *[pltpu.HBM(a.shape, a.dtype) for a in srcs], *[pltpu.HBM(a.shape, a.dtype) for a in lands],
                   jax.ShapeDtypeStruct((8, 128), F32)),
        in_specs=[*([_HBM] * (2 * n)), _SEM, _SEM, pl.BlockSpec(memory_space=pl.ANY)],
        out_specs=(_SEM, _SEM, *([_HBM] * (2 * n)), pl.BlockSpec(memory_space=pltpu.VMEM)),
        input_output_aliases={i: 2 + i for i in range(2 * n)},
        compiler_params=pltpu.CompilerParams(has_side_effects=_EFFECT),
    )(*srcs, *lands, send1, recv1, after)
    return outs[0], outs[1], outs[2:2 + n], outs[2 + n:2 + 2 * n], outs[-1]


def _gather_wait(send2, recv2, srcs, lands, after, name):
    n = len(srcs)

    def body(*refs):
        land_refs = refs[n:2 * n]
        s2, r2 = refs[2 * n], refs[2 * n + 1]
        for i in range(n):
            for j in range(3):
                cp = _gather_second(land_refs, s2, r2, i, j)
                cp.wait_send()
                cp.wait_recv()

    outs = pl.pallas_call(
        body, name=name,
        out_shape=(*[pltpu.HBM(a.shape, a.dtype) for a in srcs], *[pltpu.HBM(a.shape, a.dtype) for a in lands]),
        in_specs=[*([_HBM] * (2 * n)), _SEM, _SEM, pl.BlockSpec(memory_space=pl.ANY)],
        out_specs=tuple([_HBM] * (2 * n)),
        input_output_aliases={i: i for i in range(2 * n)},
        compiler_params=pltpu.CompilerParams(has_side_effects=_EFFECT),
    )(*srcs, *lands, send2, recv2, after)
    return outs[:n], outs[n:]


def _row(a, l):
    return a[l:l + 1]


def _tie(a, token):
    return a if token is None else a + token[0, 0]


def _layer_fwd(h, mem, W, l, tab, after, at):
    s = {}
    n = f"l{l}_"
    s["h0"] = h
    P = _matmul(h, W["w_in"][l], "nt", n + "proj", after=after, out_dtype=BF16)
    s["P"] = P
    sink_col = jnp.repeat(W["sinks"][l].reshape(N_KV_HEADS, GROUP), BLOCK, axis=1)[:, :, None]
    s["sink_col"] = sink_col
    y_rnn, hs = _rnn_fwd(P, W["conv_w"][l], _row(W["conv_b"], l), W["w_rg"][l], _row(W["b_rg"], l),
                         W["w_ig"][l], _row(W["b_ig"], l), _row(W["lru_lambda"], l), n + "rnn_fwd")
    y_attn = _swa_fwd(P, tab, _tie(sink_col, at(l, "proj", P)), n + "swa_fwd")
    at(l, "attn", y_attn)
    mr = _matmul(y_rnn, W["w_br_rnn"][l], "nn", n + "br_rnn")
    ma = _matmul(y_attn, W["w_br_attn"][l], "nn", n + "br_attn")
    merged = _merge_fwd(P, mr, ma, n + "merge_fwd")
    h1, xh1, rs1 = _matmul_ln(merged, W["w_out"][l], h, _tie(_row(W["ln1_g"], l), at(l, "mix", merged)),
                              _row(W["ln1_b"], l), n + "w_out_ln1")
    at(l, "ln1", h1)
    s.update(hs=hs, y_rnn=y_rnn, y_attn=y_attn, mr=mr, ma=ma, merged=merged, xh1=xh1, rs1=rs1, h1=h1)

    qc = _matmul(h1, W["cq_w"][l], "nn", n + "cq", out_dtype=BF16)
    kv = _matmul(mem, W["ckv_w"][l], "nt", n + "ckv", out_dtype=BF16)
    oc = _cross_fwd(qc, kv, n + "cross_fwd")
    h2, xh2, rs2 = _matmul_ln(oc, W["co_w"][l], h1, _row(W["ln2_g"], l), _row(W["ln2_b"], l), n + "co_ln2")
    s.update(qc=qc, kv=kv, oc=oc, xh2=xh2, rs2=rs2, h2=h2)

    U = _matmul(h2, W["ffn_wi"][l], "nt", n + "ffn_wi", after=at(l, "ln2", h2), out_dtype=BF16)
    act = _swiglu_fwd(U, n + "swiglu_fwd")
    h3, xh3, rs3 = _matmul_ln(act, W["ffn_wo"][l], h2, _tie(_row(W["ln3_g"], l), at(l, "ffn", act)),
                              _row(W["ln3_b"], l), n + "ffn_wo_ln3")
    s.update(U=U, act=act, xh3=xh3, rs3=rs3)
    return h3, s


GRAD_PARTS = (("ffn_wi", "ffn_wo", "cq_w", "ckv_w", "co_w"), ("w_out", "w_br_rnn", "w_br_attn"),
              ("w_in", "w_rg", "w_ig"))


def _layer_bwd(dh3, mem, W, l, tab, s, send):
    n = f"l{l}_"
    g = {}
    dz3, g["ln3_g"], g["ln3_b"] = _ln_bwd(dh3, s["xh3"], s["rs3"], _row(W["ln3_g"], l), n + "ln3_bwd")
    g["ffn_wo"] = _matmul(s["act"], dz3, "tn", n + "d_ffn_wo", out_dtype=BF16)
    dact = _matmul(dz3, W["ffn_wo"][l], "nt", n + "d_act", out_dtype=BF16)
    dU = _swiglu_bwd(s["U"], dact, n + "swiglu_bwd")
    g["ffn_wi"] = _matmul(dU, s["h2"], "tn", n + "d_ffn_wi", out_dtype=BF16)
    dh2 = _matmul(dU, W["ffn_wi"][l], "nn", n + "d_h2", add=dz3, add_scale=ALPHA)
    dz2, g["ln2_g"], g["ln2_b"] = _ln_bwd(dh2, s["xh2"], s["rs2"], _row(W["ln2_g"], l), n + "ln2_bwd")
    g["co_w"] = _matmul(s["oc"], dz2, "tn", n + "d_co", out_dtype=BF16)
    doc = _matmul(dz2, W["co_w"][l], "nt", n + "d_oc", out_dtype=BF16)
    dqc, dkv = _cross_bwd(s["qc"], s["kv"], doc, n + "cross_bwd")
    g["ckv_w"] = _matmul(dkv, mem, "tn", n + "d_ckv", out_dtype=BF16)
    g["cq_w"] = _matmul(s["h1"], dqc, "tn", n + "d_cq", out_dtype=BF16)
    after = send(l, 0, g)
    dh1 = _matmul(dqc, W["cq_w"][l], "nt", n + "d_h1", add=dz2, add_scale=ALPHA, after=after)
    dz1, g["ln1_g"], g["ln1_b"] = _ln_bwd(dh1, s["xh1"], s["rs1"], _row(W["ln1_g"], l), n + "ln1_bwd")
    g["w_out"] = _matmul(s["merged"], dz1, "tn", n + "d_w_out", out_dtype=BF16)
    dmerged = _matmul(dz1, W["w_out"][l], "nt", n + "d_merged")
    dmr, dma, dgrnn, dgattn = _merge_bwd(s["P"], s["mr"], s["ma"], dmerged, n + "merge_bwd")
    g["w_br_rnn"] = _matmul(s["y_rnn"], dmr, "tn", n + "d_br_rnn", out_dtype=BF16)
    g["w_br_attn"] = _matmul(s["y_attn"], dma, "tn", n + "d_br_attn", out_dtype=BF16)
    after = send(l, 1, g)
    dy_rnn = _matmul(dmr, W["w_br_rnn"][l], "nt", n + "d_y_rnn", after=after)
    dy_attn = _matmul(dma, W["w_br_attn"][l], "nt", n + "d_y_attn", out_dtype=BF16)
    dxr, dgr, g["conv_w"], g["conv_b"], g["w_rg"], g["b_rg"], g["w_ig"], g["b_ig"], g["lru_lambda"] = _rnn_bwd(
        s["P"], s["hs"], dy_rnn, W["conv_w"][l], _row(W["conv_b"], l), W["w_rg"][l], _row(W["b_rg"], l),
        W["w_ig"][l], _row(W["b_ig"], l), _row(W["lru_lambda"], l), n + "rnn_bwd")
    dq, dk, dv, dsk = _swa_bwd(s["P"], dy_attn, tab, s["sink_col"], n + "swa_bwd")
    g["sinks"] = dsk[:, :, 0].reshape(1, N_Q_HEADS)
    dP = jnp.concatenate([dxr, dgr, dq, dk.astype(BF16), dv.astype(BF16), dgrnn, dgattn], axis=1)
    g["w_in"] = _matmul(dP, s["h0"], "tn", n + "d_w_in", out_dtype=BF16)
    after = send(l, 2, g)
    dh = _matmul(dP, W["w_in"][l], "nn", n + "d_h0", add=dz1, add_scale=ALPHA, after=after)
    return dh, g


def _local_step(x, mem, target, W, at, send):
    T = x.shape[0]
    tab = _rope_table(T)
    h = x
    saved = []
    for l in range(DEPTH):
        after = at(l, "start", h)
        h, s = _layer_fwd(h, mem, W, l, tab, after, at)
        saved.append(s)
    lblk, dh = _loss_head(h, target, "loss_head")
    grads = [None] * DEPTH
    for l in reversed(range(DEPTH)):
        dh, grads[l] = _layer_bwd(dh, mem, W, l, tab, saved[l], send)
    return lblk[0, 0], dh, grads


COL_SHARDED = ("w_in", "ckv_w", "ffn_wi")
GATE_MATS = ("w_rg", "w_ig")


def _shard_rows(shards, l):
    out = []
    for n, r in PACK_ROWS:
        a = shards[n][l].astype(BF16)
        if n in COL_SHARDED:
            a = a.T
        elif n in GATE_MATS:
            a = a.reshape(RNN_BLOCKS * RNN_BLOCK // N_DEV, RNN_BLOCK)
        out.append(a)
    return out


def _full_weight(G, name):
    if name in GATE_MATS:
        return jnp.transpose(G.reshape(N_DEV, RNN_BLOCKS, RNN_BLOCK // N_DEV, RNN_BLOCK), (1, 0, 2, 3)).reshape(
            RNN_BLOCKS, RNN_BLOCK, RNN_BLOCK)
    return G.reshape(N_DEV * G.shape[1], G.shape[2])


SHARD_ROWS = dict(PACK_ROWS)


def _owner_blocks(g, names):
    out = []
    for name in names:
        a = g[name]
        if name in GATE_MATS:
            a = jnp.transpose(a.astype(BF16).reshape(RNN_BLOCKS, N_DEV, RNN_BLOCK // N_DEV, RNN_BLOCK), (1, 0, 2, 3))
        out.append(a.reshape(N_DEV, SHARD_ROWS[name], D_MODEL))
    return out


def _pack_small(g):
    rows = [g["conv_w"]]
    for nme in SMALL_NAMES:
        a = g[nme]
        if nme == "sinks":
            a = jnp.pad(a, ((0, 0), (0, D_MODEL - N_Q_HEADS)))
        rows.append(a)
    rows.append(jnp.zeros((SMALL_ROWS - CONV_WIDTH - len(SMALL_NAMES), D_MODEL), F32))
    return jnp.concatenate(rows, axis=0)


_SHARD_SHAPES = {"w_in": (1024, 672), "w_br_rnn": (128, 1024), "w_br_attn": (128, 1024), "w_out": (128, 1024),
                 "cq_w": (128, 1024), "ckv_w": (1024, 256), "co_w": (128, 1024), "ffn_wi": (1024, 704),
                 "ffn_wo": (352, 1024), "w_rg": (4, 32, 256), "w_ig": (4, 32, 256)}

LAYER0_GROUPS = (("w_in", "w_rg", "w_ig"), ("w_br_rnn", "w_br_attn", "w_out"),
                 ("cq_w", "ckv_w", "co_w", "ffn_wi", "ffn_wo"))
LAYER0_FORWARD_AT = {"proj": 1, "mix": 2}
LAYER0_WAIT_AT = {"attn": 1, "ln1": 2}
LAYER1_FORWARD_AT = {"mix": 2}
LAYER1_WAIT_AT = {"ln1": 2}
NEXT_LAYER_FORWARD_AT = ("ffn", "ln2", "ln2", None)

WEIGHT_NAMES = ("w_in", "conv_w", "conv_b", "w_rg", "b_rg", "w_ig", "b_ig", "lru_lambda", "w_br_rnn", "w_br_attn",
                "sinks", "w_out", "ln1_g", "ln1_b", "cq_w", "ckv_w", "co_w", "ln2_g", "ln2_b", "ffn_wi", "ffn_wo",
                "ln3_g", "ln3_b")


def kernel(x, mem, w_in, conv_w, conv_b, w_rg, b_rg, w_ig, b_ig, lru_lambda, w_br_rnn, w_br_attn, sinks, w_out, ln1_g, ln1_b, cq_w, ckv_w, co_w, ln2_g, ln2_b, ffn_wi, ffn_wo, ln3_g, ln3_b, loss_target, m_w_in, m_conv_w, m_conv_b, m_w_rg, m_b_rg, m_w_ig, m_b_ig, m_lru_lambda, m_w_br_rnn, m_w_br_attn, m_sinks, m_w_out, m_ln1_g, m_ln1_b, m_cq_w, m_ckv_w, m_co_w, m_ln2_g, m_ln2_b, m_ffn_wi, m_ffn_wo, m_ln3_g, m_ln3_b, v_w_in, v_conv_w, v_conv_b, v_w_rg, v_b_rg, v_w_ig, v_b_ig, v_lru_lambda, v_w_br_rnn, v_w_br_attn, v_sinks, v_w_out, v_ln1_g, v_ln1_b, v_cq_w, v_ckv_w, v_co_w, v_ln2_g, v_ln2_b, v_ffn_wi, v_ffn_wo, v_ln3_g, v_ln3_b):
    w = dict(w_in=w_in, conv_w=conv_w, conv_b=conv_b, w_rg=w_rg, b_rg=b_rg, w_ig=w_ig, b_ig=b_ig,
             lru_lambda=lru_lambda, w_br_rnn=w_br_rnn, w_br_attn=w_br_attn, sinks=sinks, w_out=w_out, ln1_g=ln1_g,
             ln1_b=ln1_b, cq_w=cq_w, ckv_w=ckv_w, co_w=co_w, ln2_g=ln2_g, ln2_b=ln2_b, ffn_wi=ffn_wi, ffn_wo=ffn_wo,
             ln3_g=ln3_g, ln3_b=ln3_b)
    m = dict(w_in=m_w_in, conv_w=m_conv_w, conv_b=m_conv_b, w_rg=m_w_rg, b_rg=m_b_rg, w_ig=m_w_ig, b_ig=m_b_ig,
             lru_lambda=m_lru_lambda, w_br_rnn=m_w_br_rnn, w_br_attn=m_w_br_attn, sinks=m_sinks, w_out=m_w_out,
             ln1_g=m_ln1_g, ln1_b=m_ln1_b, cq_w=m_cq_w, ckv_w=m_ckv_w, co_w=m_co_w, ln2_g=m_ln2_g, ln2_b=m_ln2_b,
             ffn_wi=m_ffn_wi, ffn_wo=m_ffn_wo, ln3_g=m_ln3_g, ln3_b=m_ln3_b)
    v = dict(w_in=v_w_in, conv_w=v_conv_w, conv_b=v_conv_b, w_rg=v_w_rg, b_rg=v_b_rg, w_ig=v_w_ig, b_ig=v_b_ig,
             lru_lambda=v_lru_lambda, w_br_rnn=v_w_br_rnn, w_br_attn=v_w_br_attn, sinks=v_sinks, w_out=v_w_out,
             ln1_g=v_ln1_g, ln1_b=v_ln1_b, cq_w=v_cq_w, ckv_w=v_ckv_w, co_w=v_co_w, ln2_g=v_ln2_g, ln2_b=v_ln2_b,
             ffn_wi=v_ffn_wi, ffn_wo=v_ffn_wo, ln3_g=v_ln3_g, ln3_b=v_ln3_b)
    my_dev = 4 * lax.axis_index("x") + 2 * lax.axis_index("y") + lax.axis_index("c")

    W = {n: [None] * DEPTH for n, _ in PACK_ROWS}
    shards0 = dict(zip([n for n, _ in PACK_ROWS], _shard_rows(w, 0)))
    *gathered, conv_all = _all_gather_many(
        [shards0[n] for n in LAYER0_GROUPS[0]] + [conv_w.reshape(DEPTH * CONV_WIDTH, D_MODEL // N_DEV)],
        "l0_gather_first")
    for n, G in zip(LAYER0_GROUPS[0], gathered):
        W[n][0] = _full_weight(G, n)
    W["conv_w"] = jnp.transpose(conv_all, (1, 0, 2)).reshape(DEPTH, CONV_WIDTH, D_MODEL)
    for n in SMALL_NAMES:
        W[n] = w[n]
    flying = {}
    token = conv_all
    groups = [((0, gi), LAYER0_GROUPS[gi], [shards0[n] for n in LAYER0_GROUPS[gi]]) for gi in (1, 2)]
    shards1 = dict(zip([n for n, _ in PACK_ROWS], _shard_rows(w, 1)))
    first1 = LAYER0_GROUPS[0] + LAYER0_GROUPS[1]
    groups += [((1, 0), first1, [shards1[n] for n in first1]),
               ((1, 2), LAYER0_GROUPS[2], [shards1[n] for n in LAYER0_GROUPS[2]])]
    groups += [((l, 0), [n for n, _ in PACK_ROWS], _shard_rows(w, l)) for l in range(2, DEPTH)]
    for key, names, srcs in groups:
        lands = [lax.empty((N_DEV,) + a.shape, a.dtype) for a in srcs]
        flying[key] = (names,) + _gather_start(srcs, lands, f"l{key[0]}_gather_start{key[1]}", token)
        token = flying[key][5]
    first_token = token

    def forward(key, after):
        names, send1, recv1, srcs, lands, _ = flying[key]
        flying[key] = (names,) + _gather_forward(send1, recv1, srcs, lands, after, f"l{key[0]}_gather_forward{key[1]}")
        return flying[key][5]

    def arrive(key, after):
        names, send2, recv2, srcs, lands, _ = flying.pop(key)
        srcs, lands = _gather_wait(send2, recv2, srcs, lands, after, f"l{key[0]}_gather_wait{key[1]}")
        for n, mine, G in zip(names, srcs, lands):
            W[n][key[0]] = _full_weight(lax.dynamic_update_index_in_dim(G, mine, my_dev, 0), n)

    def at(l, point, x):
        if point == "start":
            if l == 0:
                return first_token
            arrive((l, 0), x)
        elif l == 0 and point in LAYER0_FORWARD_AT:
            return forward((0, LAYER0_FORWARD_AT[point]), x)
        elif l == 0 and point in LAYER0_WAIT_AT:
            arrive((0, LAYER0_WAIT_AT[point]), x)
        elif l == 1 and point in LAYER1_FORWARD_AT:
            return forward((1, LAYER1_FORWARD_AT[point]), x)
        elif l == 1 and point in LAYER1_WAIT_AT:
            arrive((1, LAYER1_WAIT_AT[point]), x)
        elif point == NEXT_LAYER_FORWARD_AT[l]:
            return forward((l + 1, 0), x)
        return None

    summed = {}
    sent = {}
    me = my_dev.astype(jnp.int32).reshape(1)

    def finish(l, part, after):
        ssem, rsem, blocks, lands = sent.pop((l, part))
        blocks, recvs = _split_wait(ssem, rsem, blocks, lands, after, f"l{l}_exchange_wait{part}")
        for n, s in zip(GRAD_PARTS[part], _sum_blocks(recvs, blocks, me, f"l{l}_sum_grads{part}")):
            summed[(l, n)] = s

    def send(l, part, g):
        blocks = _owner_blocks(g, GRAD_PARTS[part])
        if (l + 1, part) in sent:
            finish(l + 1, part, blocks[0])
        ssem, rsem, blocks, lands, token = _split_start(blocks, [lax.empty(b.shape, b.dtype) for b in blocks],
                                                        f"l{l}_exchange_start{part}")
        sent[(l, part)] = (ssem, rsem, blocks, lands)
        return token

    loss_local, dx, layer_grads = _local_step(x[0], mem[0], loss_target[0], W, at, send)
    loss = lax.psum(loss_local, MESH_AXES)

    grads, deltas, new_m, new_v = {}, {}, {}, {}

    def update(n):
        if n in SHARD_ROWS:
            blk = jnp.stack([summed[(l, n)] for l in range(DEPTH)])
            grads[n] = jnp.transpose(blk, (0, 2, 1)) if n in COL_SHARDED else blk.reshape((DEPTH,) + _SHARD_SHAPES[n])
        deltas[n], new_m[n], new_v[n] = _adamw(w[n], grads[n], m[n], v[n], "adamw_" + n)

    early = GRAD_PARTS[0] + GRAD_PARTS[1]
    for part in (0, 1):
        finish(0, part, dx)
    for n in early:
        update(n)
    finish(0, 2, sum(deltas[n].reshape(-1)[:128] for n in early))
    for n in GRAD_PARTS[2]:
        update(n)

    small_all = _all_gather(jnp.concatenate([_pack_small(g) for g in layer_grads], axis=0), "gather_small_grads")
    small_sum = _sum_slots(small_all, "sum_small_grads").reshape(DEPTH, SMALL_ROWS, D_MODEL)
    conv_full = small_sum[:, :CONV_WIDTH, :]
    grads["conv_w"] = lax.dynamic_slice_in_dim(conv_full, my_dev * (D_MODEL // N_DEV), D_MODEL // N_DEV, axis=2)
    for i, n in enumerate(SMALL_NAMES):
        row = small_sum[:, CONV_WIDTH + i, :]
        grads[n] = row[:, :N_Q_HEADS] if n == "sinks" else row
    for n in ("conv_w",) + SMALL_NAMES:
        update(n)

    return (loss, dx[None], *[grads[n] for n in WEIGHT_NAMES], *[deltas[n] for n in WEIGHT_NAMES],
            *[new_m[n] for n in WEIGHT_NAMES], *[new_v[n] for n in WEIGHT_NAMES])
```
